```python
import math
import jax, jax.numpy as jnp
from jax import lax
import numpy as np

D_MODEL = 1024
BATCH = 8
SEQ = 8192
DEPTH = 1

N_META = 16
D_FF = 2816
D_SSM = D_MODEL // 2
GROUP_CH = 16
N_GROUPS = D_SSM // GROUP_CH
STATE = 64
D_CONV = D_MODEL // 2
CONV_W = 3
N_BRANCH = 2
IN_COLS = D_SSM + 3 * D_CONV + N_BRANCH * D_MODEL
EPS = 1e-6

kernel_name = "hybrid_s5_shortconv_gated_macaron"


def rmsnorm(x, g):
    xf = x.astype(jnp.float32)
    y = xf * lax.rsqrt(jnp.mean(xf * xf, axis=-1, keepdims=True) + EPS)
    return (y * g.astype(jnp.float32)).astype(x.dtype)


def swiglu(x, w_gate, w_up, w_down):
    return (jax.nn.silu(x @ w_gate) * (x @ w_up)) @ w_down


def s5_mixer(u, a_re, a_im, log_dt, b_re, b_im, c_re, c_im, d_skip):
    bsz, length, _ = u.shape
    f32 = jnp.float32
    ug = u.astype(f32).reshape(bsz, length, N_GROUPS, GROUP_CH)
    a_re = a_re.astype(f32); a_im = a_im.astype(f32)
    dt = jnp.exp(log_dt.astype(f32))[:, None]
    mag = jnp.exp(a_re * dt)
    lam_re = mag * jnp.cos(a_im * dt)
    lam_im = mag * jnp.sin(a_im * dt)
    den = a_re * a_re + a_im * a_im
    q_re = ((lam_re - 1.0) * a_re + lam_im * a_im) / den
    q_im = (lam_im * a_re - (lam_re - 1.0) * a_im) / den
    b_re = b_re.astype(f32); b_im = b_im.astype(f32)
    bb_re = q_re[..., None] * b_re - q_im[..., None] * b_im
    bb_im = q_re[..., None] * b_im + q_im[..., None] * b_re
    bu_re = jnp.einsum('blgc,gpc->blgp', ug, bb_re)
    bu_im = jnp.einsum('blgc,gpc->blgp', ug, bb_im)
    la_re = jnp.broadcast_to(lam_re, bu_re.shape)
    la_im = jnp.broadcast_to(lam_im, bu_re.shape)

    def combine(e1, e2):
        a1r, a1i, b1r, b1i = e1
        a2r, a2i, b2r, b2i = e2
        return (a1r * a2r - a1i * a2i,
                a1r * a2i + a1i * a2r,
                a2r * b1r - a2i * b1i + b2r,
                a2r * b1i + a2i * b1r + b2i)

    _, _, h_re, h_im = lax.associative_scan(combine, (la_re, la_im, bu_re, bu_im), axis=1)
    y = (jnp.einsum('blgp,gcp->blgc', h_re, c_re.astype(f32))
         - jnp.einsum('blgp,gcp->blgc', h_im, c_im.astype(f32))
         + d_skip.astype(f32).reshape(N_GROUPS, GROUP_CH) * ug)
    return y.reshape(bsz, length, D_SSM).astype(u.dtype)


def short_conv(v, w_conv):
    return lax.conv_general_dilated(
        v, w_conv.astype(v.dtype), window_strides=(1,), padding=[(CONV_W - 1, 0)],
        dimension_numbers=('NWC', 'WIO', 'NWC'), feature_group_count=v.shape[-1])


def _fwd_setup_inputs(seed: int = 0) -> dict:
    key = jax.random.key(seed)
    ks = jax.random.split(key, 32)
    D = D_MODEL
    nrm = lambda k, shape, s: jax.random.normal(k, shape, jnp.float32) * s
    gain = lambda k, shape: 1.0 + 0.02 * jax.random.normal(k, shape, jnp.float32)
    a_im_base = jnp.pi * jnp.arange(STATE, dtype=jnp.float32)
    return {
        "x": nrm(ks[0], (BATCH, SEQ, D), 1.0),
        "meta_tokens": nrm(ks[1], (N_META, D), 1.0),
        "g_ffn1": gain(ks[2], (DEPTH, D)),
        "ffn1_w_gate": nrm(ks[3], (DEPTH, D, D_FF), D ** -0.5),
        "ffn1_w_up": nrm(ks[4], (DEPTH, D, D_FF), D ** -0.5),
        "ffn1_w_down": nrm(ks[5], (DEPTH, D_FF, D), D_FF ** -0.5),
        "g_mix": gain(ks[6], (DEPTH, D)),
        "w_in": nrm(ks[7], (DEPTH, D, IN_COLS), D ** -0.5),
        "b_gate": nrm(ks[8], (DEPTH, N_BRANCH * D), 0.01),
        "ssm_a_re": -0.5 + nrm(ks[9], (DEPTH, N_GROUPS, STATE), 0.01),
        "ssm_a_im": a_im_base + nrm(ks[10], (DEPTH, N_GROUPS, STATE), 0.01),
        "ssm_log_dt": jax.random.uniform(ks[11], (DEPTH, N_GROUPS), jnp.float32,
                                         math.log(1e-3), math.log(1e-1)),
        "ssm_b_re": nrm(ks[12], (DEPTH, N_GROUPS, STATE, GROUP_CH), GROUP_CH ** -0.5),
        "ssm_b_im": nrm(ks[13], (DEPTH, N_GROUPS, STATE, GROUP_CH), GROUP_CH ** -0.5),
        "ssm_c_re": nrm(ks[14], (DEPTH, N_GROUPS, GROUP_CH, STATE), (2 * STATE) ** -0.5),
        "ssm_c_im": nrm(ks[15], (DEPTH, N_GROUPS, GROUP_CH, STATE), (2 * STATE) ** -0.5),
        "ssm_d": nrm(ks[16], (DEPTH, D_SSM), 1.0),
        "ssm_w_glu": nrm(ks[17], (DEPTH, D_SSM, 2 * D), D_SSM ** -0.5),
        "conv_w": nrm(ks[18], (DEPTH, CONV_W, 1, D_CONV), CONV_W ** -0.5),
        "conv_w_out": nrm(ks[19], (DEPTH, D_CONV, D), D_CONV ** -0.5),
        "w_o": nrm(ks[20], (DEPTH, D, D), D ** -0.5),
        "g_ffn2": gain(ks[21], (DEPTH, D)),
        "ffn2_w_gate": nrm(ks[22], (DEPTH, D, D_FF), D ** -0.5),
        "ffn2_w_up": nrm(ks[23], (DEPTH, D, D_FF), D ** -0.5),
        "ffn2_w_down": nrm(ks[24], (DEPTH, D_FF, D), D_FF ** -0.5),
        "g_final": gain(ks[25], (D,)),
    }


def _fwd_reference(x, meta_tokens, g_ffn1, ffn1_w_gate, ffn1_w_up, ffn1_w_down, g_mix, w_in,
              b_gate, ssm_a_re, ssm_a_im, ssm_log_dt, ssm_b_re, ssm_b_im, ssm_c_re,
              ssm_c_im, ssm_d, ssm_w_glu, conv_w, conv_w_out, w_o, g_ffn2,
              ffn2_w_gate, ffn2_w_up, ffn2_w_down, g_final):
    bsz = x.shape[0]
    meta = jnp.broadcast_to(meta_tokens.astype(x.dtype)[None], (bsz, N_META, D_MODEL))
    h = jnp.concatenate([meta, x], axis=1)
    s0 = D_SSM
    s1 = s0 + D_CONV
    s2 = s1 + D_CONV
    s3 = s2 + D_CONV
    s4 = s3 + D_MODEL
    for l in range(DEPTH):
        h = h + 0.5 * swiglu(rmsnorm(h, g_ffn1[l]), ffn1_w_gate[l], ffn1_w_up[l], ffn1_w_down[l])
        u = rmsnorm(h, g_mix[l])
        p = u @ w_in[l]
        gates = jax.nn.sigmoid(p[..., s3:] + b_gate[l])
        gate_ssm, gate_conv = gates[..., :D_MODEL], gates[..., D_MODEL:]
        y_ssm = s5_mixer(p[..., :s0], ssm_a_re[l], ssm_a_im[l], ssm_log_dt[l], ssm_b_re[l],
                         ssm_b_im[l], ssm_c_re[l], ssm_c_im[l], ssm_d[l])
        z = jax.nn.gelu(y_ssm) @ ssm_w_glu[l]
        y_ssm = z[..., :D_MODEL] * jax.nn.sigmoid(z[..., D_MODEL:])
        v, gb, gc = p[..., s0:s1], p[..., s1:s2], p[..., s2:s3]
        y_conv = (gb * short_conv(gc * v, conv_w[l])) @ conv_w_out[l]
        mixed = gate_ssm * y_ssm + gate_conv * y_conv
        h = h + mixed @ w_o[l]
        h = h + 0.5 * swiglu(rmsnorm(h, g_ffn2[l]), ffn2_w_gate[l], ffn2_w_up[l], ffn2_w_down[l])
    out = rmsnorm(h, g_final)
    return out[:, N_META:]


import jax as _jax
import jax.numpy as _jnp

TWIN_FORMAT = 'train_step'
FWD_PARAMS = ['x', 'meta_tokens', 'g_ffn1', 'ffn1_w_gate', 'ffn1_w_up', 'ffn1_w_down', 'g_mix', 'w_in', 'b_gate', 'ssm_a_re', 'ssm_a_im', 'ssm_log_dt', 'ssm_b_re', 'ssm_b_im', 'ssm_c_re', 'ssm_c_im', 'ssm_d', 'ssm_w_glu', 'conv_w', 'conv_w_out', 'w_o', 'g_ffn2', 'ffn2_w_gate', 'ffn2_w_up', 'ffn2_w_down', 'g_final']
TWIN_WEIGHTS = ['meta_tokens', 'g_ffn1', 'ffn1_w_gate', 'ffn1_w_up', 'ffn1_w_down', 'g_mix', 'w_in', 'b_gate', 'ssm_a_re', 'ssm_a_im', 'ssm_log_dt', 'ssm_b_re', 'ssm_b_im', 'ssm_c_re', 'ssm_c_im', 'ssm_d', 'ssm_w_glu', 'conv_w', 'conv_w_out', 'w_o', 'g_ffn2', 'ffn2_w_gate', 'ffn2_w_up', 'ffn2_w_down', 'g_final']
TWIN_DIFF_INPUT = 'x'
TWIN_INPUTS = ['x', 'meta_tokens', 'g_ffn1', 'ffn1_w_gate', 'ffn1_w_up', 'ffn1_w_down', 'g_mix', 'w_in', 'b_gate', 'ssm_a_re', 'ssm_a_im', 'ssm_log_dt', 'ssm_b_re', 'ssm_b_im', 'ssm_c_re', 'ssm_c_im', 'ssm_d', 'ssm_w_glu', 'conv_w', 'conv_w_out', 'w_o', 'g_ffn2', 'ffn2_w_gate', 'ffn2_w_up', 'ffn2_w_down', 'g_final', 'loss_target', 'm_meta_tokens', 'm_g_ffn1', 'm_ffn1_w_gate', 'm_ffn1_w_up', 'm_ffn1_w_down', 'm_g_mix', 'm_w_in', 'm_b_gate', 'm_ssm_a_re', 'm_ssm_a_im', 'm_ssm_log_dt', 'm_ssm_b_re', 'm_ssm_b_im', 'm_ssm_c_re', 'm_ssm_c_im', 'm_ssm_d', 'm_ssm_w_glu', 'm_conv_w', 'm_conv_w_out', 'm_w_o', 'm_g_ffn2', 'm_ffn2_w_gate', 'm_ffn2_w_up', 'm_ffn2_w_down', 'm_g_final', 'v_meta_tokens', 'v_g_ffn1', 'v_ffn1_w_gate', 'v_ffn1_w_up', 'v_ffn1_w_down', 'v_g_mix', 'v_w_in', 'v_b_gate', 'v_ssm_a_re', 'v_ssm_a_im', 'v_ssm_log_dt', 'v_ssm_b_re', 'v_ssm_b_im', 'v_ssm_c_re', 'v_ssm_c_im', 'v_ssm_d', 'v_ssm_w_glu', 'v_conv_w', 'v_conv_w_out', 'v_w_o', 'v_g_ffn2', 'v_ffn2_w_gate', 'v_ffn2_w_up', 'v_ffn2_w_down', 'v_g_final']
TWIN_OUTPUTS = ['loss', 'grad_x', 'grad_meta_tokens', 'grad_g_ffn1', 'grad_ffn1_w_gate', 'grad_ffn1_w_up', 'grad_ffn1_w_down', 'grad_g_mix', 'grad_w_in', 'grad_b_gate', 'grad_ssm_a_re', 'grad_ssm_a_im', 'grad_ssm_log_dt', 'grad_ssm_b_re', 'grad_ssm_b_im', 'grad_ssm_c_re', 'grad_ssm_c_im', 'grad_ssm_d', 'grad_ssm_w_glu', 'grad_conv_w', 'grad_conv_w_out', 'grad_w_o', 'grad_g_ffn2', 'grad_ffn2_w_gate', 'grad_ffn2_w_up', 'grad_ffn2_w_down', 'grad_g_final', 'delta_meta_tokens', 'delta_g_ffn1', 'delta_ffn1_w_gate', 'delta_ffn1_w_up', 'delta_ffn1_w_down', 'delta_g_mix', 'delta_w_in', 'delta_b_gate', 'delta_ssm_a_re', 'delta_ssm_a_im', 'delta_ssm_log_dt', 'delta_ssm_b_re', 'delta_ssm_b_im', 'delta_ssm_c_re', 'delta_ssm_c_im', 'delta_ssm_d', 'delta_ssm_w_glu', 'delta_conv_w', 'delta_conv_w_out', 'delta_w_o', 'delta_g_ffn2', 'delta_ffn2_w_gate', 'delta_ffn2_w_up', 'delta_ffn2_w_down', 'delta_g_final', 'new_m_meta_tokens', 'new_m_g_ffn1', 'new_m_ffn1_w_gate', 'new_m_ffn1_w_up', 'new_m_ffn1_w_down', 'new_m_g_mix', 'new_m_w_in', 'new_m_b_gate', 'new_m_ssm_a_re', 'new_m_ssm_a_im', 'new_m_ssm_log_dt', 'new_m_ssm_b_re', 'new_m_ssm_b_im', 'new_m_ssm_c_re', 'new_m_ssm_c_im', 'new_m_ssm_d', 'new_m_ssm_w_glu', 'new_m_conv_w', 'new_m_conv_w_out', 'new_m_w_o', 'new_m_g_ffn2', 'new_m_ffn2_w_gate', 'new_m_ffn2_w_up', 'new_m_ffn2_w_down', 'new_m_g_final', 'new_v_meta_tokens', 'new_v_g_ffn1', 'new_v_ffn1_w_gate', 'new_v_ffn1_w_up', 'new_v_ffn1_w_down', 'new_v_g_mix', 'new_v_w_in', 'new_v_b_gate', 'new_v_ssm_a_re', 'new_v_ssm_a_im', 'new_v_ssm_log_dt', 'new_v_ssm_b_re', 'new_v_ssm_b_im', 'new_v_ssm_c_re', 'new_v_ssm_c_im', 'new_v_ssm_d', 'new_v_ssm_w_glu', 'new_v_conv_w', 'new_v_conv_w_out', 'new_v_w_o', 'new_v_g_ffn2', 'new_v_ffn2_w_gate', 'new_v_ffn2_w_up', 'new_v_ffn2_w_down', 'new_v_g_final']
TWIN_LEAF_KINDS = {'loss': 'loss', 'grad_x': 'grad_x', 'grad_meta_tokens': 'grad_w', 'grad_g_ffn1': 'grad_w', 'grad_ffn1_w_gate': 'grad_w', 'grad_ffn1_w_up': 'grad_w', 'grad_ffn1_w_down': 'grad_w', 'grad_g_mix': 'grad_w', 'grad_w_in': 'grad_w', 'grad_b_gate': 'grad_w', 'grad_ssm_a_re': 'grad_w', 'grad_ssm_a_im': 'grad_w', 'grad_ssm_log_dt': 'grad_w', 'grad_ssm_b_re': 'grad_w', 'grad_ssm_b_im': 'grad_w', 'grad_ssm_c_re': 'grad_w', 'grad_ssm_c_im': 'grad_w', 'grad_ssm_d': 'grad_w', 'grad_ssm_w_glu': 'grad_w', 'grad_conv_w': 'grad_w', 'grad_conv_w_out': 'grad_w', 'grad_w_o': 'grad_w', 'grad_g_ffn2': 'grad_w', 'grad_ffn2_w_gate': 'grad_w', 'grad_ffn2_w_up': 'grad_w', 'grad_ffn2_w_down': 'grad_w', 'grad_g_final': 'grad_w', 'delta_meta_tokens': 'delta_w', 'delta_g_ffn1': 'delta_w', 'delta_ffn1_w_gate': 'delta_w', 'delta_ffn1_w_up': 'delta_w', 'delta_ffn1_w_down': 'delta_w', 'delta_g_mix': 'delta_w', 'delta_w_in': 'delta_w', 'delta_b_gate': 'delta_w', 'delta_ssm_a_re': 'delta_w', 'delta_ssm_a_im': 'delta_w', 'delta_ssm_log_dt': 'delta_w', 'delta_ssm_b_re': 'delta_w', 'delta_ssm_b_im': 'delta_w', 'delta_ssm_c_re': 'delta_w', 'delta_ssm_c_im': 'delta_w', 'delta_ssm_d': 'delta_w', 'delta_ssm_w_glu': 'delta_w', 'delta_conv_w': 'delta_w', 'delta_conv_w_out': 'delta_w', 'delta_w_o': 'delta_w', 'delta_g_ffn2': 'delta_w', 'delta_ffn2_w_gate': 'delta_w', 'delta_ffn2_w_up': 'delta_w', 'delta_ffn2_w_down': 'delta_w', 'delta_g_final': 'delta_w', 'new_m_meta_tokens': 'new_m', 'new_m_g_ffn1': 'new_m', 'new_m_ffn1_w_gate': 'new_m', 'new_m_ffn1_w_up': 'new_m', 'new_m_ffn1_w_down': 'new_m', 'new_m_g_mix': 'new_m', 'new_m_w_in': 'new_m', 'new_m_b_gate': 'new_m', 'new_m_ssm_a_re': 'new_m', 'new_m_ssm_a_im': 'new_m', 'new_m_ssm_log_dt': 'new_m', 'new_m_ssm_b_re': 'new_m', 'new_m_ssm_b_im': 'new_m', 'new_m_ssm_c_re': 'new_m', 'new_m_ssm_c_im': 'new_m', 'new_m_ssm_d': 'new_m', 'new_m_ssm_w_glu': 'new_m', 'new_m_conv_w': 'new_m', 'new_m_conv_w_out': 'new_m', 'new_m_w_o': 'new_m', 'new_m_g_ffn2': 'new_m', 'new_m_ffn2_w_gate': 'new_m', 'new_m_ffn2_w_up': 'new_m', 'new_m_ffn2_w_down': 'new_m', 'new_m_g_final': 'new_m', 'new_v_meta_tokens': 'new_v', 'new_v_g_ffn1': 'new_v', 'new_v_ffn1_w_gate': 'new_v', 'new_v_ffn1_w_up': 'new_v', 'new_v_ffn1_w_down': 'new_v', 'new_v_g_mix': 'new_v', 'new_v_w_in': 'new_v', 'new_v_b_gate': 'new_v', 'new_v_ssm_a_re': 'new_v', 'new_v_ssm_a_im': 'new_v', 'new_v_ssm_log_dt': 'new_v', 'new_v_ssm_b_re': 'new_v', 'new_v_ssm_b_im': 'new_v', 'new_v_ssm_c_re': 'new_v', 'new_v_ssm_c_im': 'new_v', 'new_v_ssm_d': 'new_v', 'new_v_ssm_w_glu': 'new_v', 'new_v_conv_w': 'new_v', 'new_v_conv_w_out': 'new_v', 'new_v_w_o': 'new_v', 'new_v_g_ffn2': 'new_v', 'new_v_ffn2_w_gate': 'new_v', 'new_v_ffn2_w_up': 'new_v', 'new_v_ffn2_w_down': 'new_v', 'new_v_g_final': 'new_v'}


def _forward(args):
    return _fwd_reference(*[args[k] for k in FWD_PARAMS])


def _output_shape():
    def fwd():
        inp = _fwd_setup_inputs(0)
        return _fwd_reference(*[inp[k] for k in FWD_PARAMS])
    out = _jax.eval_shape(fwd)
    return out.shape, out.dtype

N_MICROBATCH = 1
ADAM_LR = 0.001
ADAM_B1 = 0.9
ADAM_B2 = 0.999
ADAM_EPS = 1e-08
ADAM_WD = 0.01
ADAM_STEP = 10
PER_EXAMPLE_BATCH_AXIS = {'x': 0, 'loss_target': 0}
SHARED_INPUTS = []
_WEIGHT_DTYPES = {'meta_tokens': _jnp.float32, 'g_ffn1': _jnp.float32, 'ffn1_w_gate': _jnp.float32, 'ffn1_w_up': _jnp.float32, 'ffn1_w_down': _jnp.float32, 'g_mix': _jnp.float32, 'w_in': _jnp.float32, 'b_gate': _jnp.float32, 'ssm_a_re': _jnp.float32, 'ssm_a_im': _jnp.float32, 'ssm_log_dt': _jnp.float32, 'ssm_b_re': _jnp.float32, 'ssm_b_im': _jnp.float32, 'ssm_c_re': _jnp.float32, 'ssm_c_im': _jnp.float32, 'ssm_d': _jnp.float32, 'ssm_w_glu': _jnp.float32, 'conv_w': _jnp.float32, 'conv_w_out': _jnp.float32, 'w_o': _jnp.float32, 'g_ffn2': _jnp.float32, 'ffn2_w_gate': _jnp.float32, 'ffn2_w_up': _jnp.float32, 'ffn2_w_down': _jnp.float32, 'g_final': _jnp.float32}
MOMENT_SCALE = {'meta_tokens': 2.774438e-03, 'g_ffn1': 1.291347e-01, 'ffn1_w_gate': 5.530536e-02, 'ffn1_w_up': 5.359230e-02, 'ffn1_w_down': 8.893641e-02, 'g_mix': 2.223860e-01, 'w_in': 1.063078e-01, 'b_gate': 3.540824e-02, 'ssm_a_re': 4.736061e-03, 'ssm_a_im': 4.546629e-03, 'ssm_log_dt': 6.390286e+00, 'ssm_b_re': 2.015922e-03, 'ssm_b_im': 2.014335e-03, 'ssm_c_re': 5.846038e-03, 'ssm_c_im': 5.800146e-03, 'ssm_d': 6.613973e-02, 'ssm_w_glu': 3.006572e-02, 'conv_w': 1.664282e-01, 'conv_w_out': 1.177176e-01, 'w_o': 1.251807e-01, 'g_ffn2': 8.702763e-02, 'ffn2_w_gate': 3.811791e-02, 'ffn2_w_up': 3.689292e-02, 'ffn2_w_down': 6.122928e-02, 'g_final': 6.393000e+01}


def _to_microbatches(a, axis):
    t = _jnp.moveaxis(a, axis, 0)
    t = t.reshape((N_MICROBATCH, t.shape[0] // N_MICROBATCH) + t.shape[1:])
    return _jnp.moveaxis(t, 1, axis + 1)


def setup_inputs(seed: int = 0) -> dict:
    inp = _fwd_setup_inputs(seed)
    key = _jax.random.fold_in(_jax.random.key(seed), 7919)
    shape, _ = _output_shape()
    out = dict(inp)
    out["loss_target"] = _jax.random.normal(_jax.random.fold_in(key, 0), shape, _jnp.float32)
    for i, name in enumerate(TWIN_WEIGHTS):
        w = inp[name].astype(_jnp.float32)
        if MOMENT_SCALE is None:
            s = _jnp.sqrt(_jnp.mean(_jnp.square(w)) + 1e-30)
        else:
            s = MOMENT_SCALE[name]
        km, kv = _jax.random.split(_jax.random.fold_in(key, i + 1))
        out[name] = w
        out["m_" + name] = s * _jax.random.normal(km, w.shape, _jnp.float32)
        out["v_" + name] = (s * s) * _jax.random.uniform(kv, w.shape, _jnp.float32, 0.5, 1.5)
    if N_MICROBATCH > 1:
        for name, axis in PER_EXAMPLE_BATCH_AXIS.items():
            out[name] = _to_microbatches(out[name], axis)
    return {'x': out['x'], 'meta_tokens': out['meta_tokens'], 'g_ffn1': out['g_ffn1'], 'ffn1_w_gate': out['ffn1_w_gate'], 'ffn1_w_up': out['ffn1_w_up'], 'ffn1_w_down': out['ffn1_w_down'], 'g_mix': out['g_mix'], 'w_in': out['w_in'], 'b_gate': out['b_gate'], 'ssm_a_re': out['ssm_a_re'], 'ssm_a_im': out['ssm_a_im'], 'ssm_log_dt': out['ssm_log_dt'], 'ssm_b_re': out['ssm_b_re'], 'ssm_b_im': out['ssm_b_im'], 'ssm_c_re': out['ssm_c_re'], 'ssm_c_im': out['ssm_c_im'], 'ssm_d': out['ssm_d'], 'ssm_w_glu': out['ssm_w_glu'], 'conv_w': out['conv_w'], 'conv_w_out': out['conv_w_out'], 'w_o': out['w_o'], 'g_ffn2': out['g_ffn2'], 'ffn2_w_gate': out['ffn2_w_gate'], 'ffn2_w_up': out['ffn2_w_up'], 'ffn2_w_down': out['ffn2_w_down'], 'g_final': out['g_final'], 'loss_target': out['loss_target'], 'm_meta_tokens': out['m_meta_tokens'], 'm_g_ffn1': out['m_g_ffn1'], 'm_ffn1_w_gate': out['m_ffn1_w_gate'], 'm_ffn1_w_up': out['m_ffn1_w_up'], 'm_ffn1_w_down': out['m_ffn1_w_down'], 'm_g_mix': out['m_g_mix'], 'm_w_in': out['m_w_in'], 'm_b_gate': out['m_b_gate'], 'm_ssm_a_re': out['m_ssm_a_re'], 'm_ssm_a_im': out['m_ssm_a_im'], 'm_ssm_log_dt': out['m_ssm_log_dt'], 'm_ssm_b_re': out['m_ssm_b_re'], 'm_ssm_b_im': out['m_ssm_b_im'], 'm_ssm_c_re': out['m_ssm_c_re'], 'm_ssm_c_im': out['m_ssm_c_im'], 'm_ssm_d': out['m_ssm_d'], 'm_ssm_w_glu': out['m_ssm_w_glu'], 'm_conv_w': out['m_conv_w'], 'm_conv_w_out': out['m_conv_w_out'], 'm_w_o': out['m_w_o'], 'm_g_ffn2': out['m_g_ffn2'], 'm_ffn2_w_gate': out['m_ffn2_w_gate'], 'm_ffn2_w_up': out['m_ffn2_w_up'], 'm_ffn2_w_down': out['m_ffn2_w_down'], 'm_g_final': out['m_g_final'], 'v_meta_tokens': out['v_meta_tokens'], 'v_g_ffn1': out['v_g_ffn1'], 'v_ffn1_w_gate': out['v_ffn1_w_gate'], 'v_ffn1_w_up': out['v_ffn1_w_up'], 'v_ffn1_w_down': out['v_ffn1_w_down'], 'v_g_mix': out['v_g_mix'], 'v_w_in': out['v_w_in'], 'v_b_gate': out['v_b_gate'], 'v_ssm_a_re': out['v_ssm_a_re'], 'v_ssm_a_im': out['v_ssm_a_im'], 'v_ssm_log_dt': out['v_ssm_log_dt'], 'v_ssm_b_re': out['v_ssm_b_re'], 'v_ssm_b_im': out['v_ssm_b_im'], 'v_ssm_c_re': out['v_ssm_c_re'], 'v_ssm_c_im': out['v_ssm_c_im'], 'v_ssm_d': out['v_ssm_d'], 'v_ssm_w_glu': out['v_ssm_w_glu'], 'v_conv_w': out['v_conv_w'], 'v_conv_w_out': out['v_conv_w_out'], 'v_w_o': out['v_w_o'], 'v_g_ffn2': out['v_g_ffn2'], 'v_ffn2_w_gate': out['v_ffn2_w_gate'], 'v_ffn2_w_up': out['v_ffn2_w_up'], 'v_ffn2_w_down': out['v_ffn2_w_down'], 'v_g_final': out['v_g_final']}


def _loss(weights, diff, rest, loss_target):
    with _jax.named_scope("forward"):
        args = {**rest, TWIN_DIFF_INPUT: diff, **{k: w.astype(_WEIGHT_DTYPES[k]) for k, w in weights.items()}}
        y = _forward(args)
    with _jax.named_scope("loss_head"):
        err = _jnp.square(y.astype(_jnp.float32) - loss_target)
        return 0.5 * _jnp.sum(_jnp.mean(err, axis=-1)) if err.ndim else 0.5 * err


def _adamw(w, g, m, v):
    m = ADAM_B1 * m + (1.0 - ADAM_B1) * g
    v = ADAM_B2 * v + (1.0 - ADAM_B2) * _jnp.square(g)
    m_hat = m / (1.0 - ADAM_B1 ** ADAM_STEP)
    v_hat = v / (1.0 - ADAM_B2 ** ADAM_STEP)
    delta = -ADAM_LR * (m_hat / (_jnp.sqrt(v_hat) + ADAM_EPS) + ADAM_WD * w)
    return delta, m, v


def reference(x, meta_tokens, g_ffn1, ffn1_w_gate, ffn1_w_up, ffn1_w_down, g_mix, w_in, b_gate, ssm_a_re, ssm_a_im, ssm_log_dt, ssm_b_re, ssm_b_im, ssm_c_re, ssm_c_im, ssm_d, ssm_w_glu, conv_w, conv_w_out, w_o, g_ffn2, ffn2_w_gate, ffn2_w_up, ffn2_w_down, g_final, loss_target, m_meta_tokens, m_g_ffn1, m_ffn1_w_gate, m_ffn1_w_up, m_ffn1_w_down, m_g_mix, m_w_in, m_b_gate, m_ssm_a_re, m_ssm_a_im, m_ssm_log_dt, m_ssm_b_re, m_ssm_b_im, m_ssm_c_re, m_ssm_c_im, m_ssm_d, m_ssm_w_glu, m_conv_w, m_conv_w_out, m_w_o, m_g_ffn2, m_ffn2_w_gate, m_ffn2_w_up, m_ffn2_w_down, m_g_final, v_meta_tokens, v_g_ffn1, v_ffn1_w_gate, v_ffn1_w_up, v_ffn1_w_down, v_g_mix, v_w_in, v_b_gate, v_ssm_a_re, v_ssm_a_im, v_ssm_log_dt, v_ssm_b_re, v_ssm_b_im, v_ssm_c_re, v_ssm_c_im, v_ssm_d, v_ssm_w_glu, v_conv_w, v_conv_w_out, v_w_o, v_g_ffn2, v_ffn2_w_gate, v_ffn2_w_up, v_ffn2_w_down, v_g_final):
    given = dict(x=x, meta_tokens=meta_tokens, g_ffn1=g_ffn1, ffn1_w_gate=ffn1_w_gate, ffn1_w_up=ffn1_w_up, ffn1_w_down=ffn1_w_down, g_mix=g_mix, w_in=w_in, b_gate=b_gate, ssm_a_re=ssm_a_re, ssm_a_im=ssm_a_im, ssm_log_dt=ssm_log_dt, ssm_b_re=ssm_b_re, ssm_b_im=ssm_b_im, ssm_c_re=ssm_c_re, ssm_c_im=ssm_c_im, ssm_d=ssm_d, ssm_w_glu=ssm_w_glu, conv_w=conv_w, conv_w_out=conv_w_out, w_o=w_o, g_ffn2=g_ffn2, ffn2_w_gate=ffn2_w_gate, ffn2_w_up=ffn2_w_up, ffn2_w_down=ffn2_w_down, g_final=g_final, loss_target=loss_target, m_meta_tokens=m_meta_tokens, m_g_ffn1=m_g_ffn1, m_ffn1_w_gate=m_ffn1_w_gate, m_ffn1_w_up=m_ffn1_w_up, m_ffn1_w_down=m_ffn1_w_down, m_g_mix=m_g_mix, m_w_in=m_w_in, m_b_gate=m_b_gate, m_ssm_a_re=m_ssm_a_re, m_ssm_a_im=m_ssm_a_im, m_ssm_log_dt=m_ssm_log_dt, m_ssm_b_re=m_ssm_b_re, m_ssm_b_im=m_ssm_b_im, m_ssm_c_re=m_ssm_c_re, m_ssm_c_im=m_ssm_c_im, m_ssm_d=m_ssm_d, m_ssm_w_glu=m_ssm_w_glu, m_conv_w=m_conv_w, m_conv_w_out=m_conv_w_out, m_w_o=m_w_o, m_g_ffn2=m_g_ffn2, m_ffn2_w_gate=m_ffn2_w_gate, m_ffn2_w_up=m_ffn2_w_up, m_ffn2_w_down=m_ffn2_w_down, m_g_final=m_g_final, v_meta_tokens=v_meta_tokens, v_g_ffn1=v_g_ffn1, v_ffn1_w_gate=v_ffn1_w_gate, v_ffn1_w_up=v_ffn1_w_up, v_ffn1_w_down=v_ffn1_w_down, v_g_mix=v_g_mix, v_w_in=v_w_in, v_b_gate=v_b_gate, v_ssm_a_re=v_ssm_a_re, v_ssm_a_im=v_ssm_a_im, v_ssm_log_dt=v_ssm_log_dt, v_ssm_b_re=v_ssm_b_re, v_ssm_b_im=v_ssm_b_im, v_ssm_c_re=v_ssm_c_re, v_ssm_c_im=v_ssm_c_im, v_ssm_d=v_ssm_d, v_ssm_w_glu=v_ssm_w_glu, v_conv_w=v_conv_w, v_conv_w_out=v_conv_w_out, v_w_o=v_w_o, v_g_ffn2=v_g_ffn2, v_ffn2_w_gate=v_ffn2_w_gate, v_ffn2_w_up=v_ffn2_w_up, v_ffn2_w_down=v_ffn2_w_down, v_g_final=v_g_final)
    weights = {n: given[n] for n in TWIN_WEIGHTS}
    shared = {n: given[n] for n in SHARED_INPUTS}
    per_example = {n: given[n] for n in ['x']}
    grad_fn = _jax.value_and_grad(_loss, argnums=(0, 1))

    def one_microbatch(ex, loss_target):
        ex = dict(ex)
        diff = ex.pop(TWIN_DIFF_INPUT)
        return grad_fn(weights, diff, {**shared, **ex}, loss_target)

    if N_MICROBATCH == 1:
        loss, (grad_w, grad_x) = one_microbatch(per_example, given["loss_target"])
    else:
        def body(carry, xs):
            loss_sum, grad_sum = carry
            l_k, (gw_k, gx_k) = one_microbatch(xs[0], xs[1])
            with _jax.named_scope("update"):
                return (loss_sum + l_k, _jax.tree.map(_jnp.add, grad_sum, gw_k)), gx_k

        init = (_jnp.zeros((), _jnp.float32), _jax.tree.map(_jnp.zeros_like, weights))
        (loss, grad_w), grad_x = _jax.lax.scan(body, init, (per_example, given["loss_target"]))
    with _jax.named_scope("update"):
        delta_w, new_m, new_v = {}, {}, {}
        for n in TWIN_WEIGHTS:
            delta_w[n], new_m[n], new_v[n] = _adamw(weights[n], grad_w[n], given["m_" + n], given["v_" + n])
    return (loss, grad_x, *[grad_w[n] for n in TWIN_WEIGHTS], *[delta_w[n] for n in TWIN_WEIGHTS],
            *[new_m[n] for n in TWIN_WEIGHTS], *[new_v[n] for n in TWIN_WEIGHTS])
```

```python
import functools
import math

import jax
import jax.numpy as jnp
from jax import lax
from jax.experimental import pallas as pl
from jax.experimental.pallas import tpu as pltpu

F32 = jnp.float32
BF16 = jnp.bfloat16
MESH = pl.DeviceIdType.MESH

RMS_EPS = 1e-6
ADAM_LR = 0.001
ADAM_B1 = 0.9
ADAM_B2 = 0.999
ADAM_EPS = 1e-08
ADAM_WD = 0.01
ADAM_STEP = 10

LANES = 128
SUBLANES = 8
VMEM_LIMIT = 56 * 1024 * 1024

ROW_ALIGN = 256
SCAN_TILE = 256
GROUPS_PER_BLOCK = 8


def _params(sem, vmem=VMEM_LIMIT):
    return pltpu.CompilerParams(dimension_semantics=sem, vmem_limit_bytes=vmem)


def _pick_tile(n, candidates):
    for c in candidates:
        if n % c == 0:
            return c
    raise ValueError(f"no tile for {n}")


def _dot(a, b):
    return jnp.dot(a, b, preferred_element_type=F32)


def _dot_nt(a, b):
    return lax.dot_general(a, b, (((1,), (1,)), ((), ())), preferred_element_type=F32)


def _dot_tn(a, b):
    return lax.dot_general(a, b, (((0,), (0,)), ((), ())), preferred_element_type=F32)


def _sigmoid(x):
    return 1.0 / (1.0 + jnp.exp(-x))


def _rms_stats(h):
    r = lax.rsqrt(jnp.mean(h * h, axis=-1, keepdims=True) + RMS_EPS)
    return h * r, r


def _rms_bwd(xhat, r, g, dn):
    dxh = dn * g
    return r * (dxh - xhat * jnp.mean(dxh * xhat, axis=-1, keepdims=True))


GELU_K = math.sqrt(2.0 / math.pi)
GELU_C = 0.044715


def _gelu(x):
    return 0.5 * x * (1.0 + jnp.tanh(GELU_K * (x + GELU_C * x * x * x)))


def _gelu_grad(x):
    t = jnp.tanh(GELU_K * (x + GELU_C * x * x * x))
    return 0.5 * (1.0 + t) + 0.5 * x * (1.0 - t * t) * GELU_K * (1.0 + 3.0 * GELU_C * x * x)


def _ffn_fwd(h, g, wg, wu, wd, name, final=None):
    tp, d = h.shape
    ns, _, f4 = wg.shape
    tm = _pick_tile(tp, (768, 512, 256))
    ni = tp // tm

    def body(*refs):
        if final is None:
            h_ref, g_ref, wg_ref, wu_ref, wd_ref, ho_ref, a_ref, b_ref, n_scr, acc = refs
        else:
            (h_ref, g_ref, wg_ref, wu_ref, wd_ref, gf_ref, tg_ref,
             ho_ref, a_ref, b_ref, dgf_ref, loss_ref, n_scr, acc) = refs
        i = pl.program_id(0)
        k = pl.program_id(1)

        @pl.when(k == 0)
        def _():
            xhat, _ = _rms_stats(h_ref[...])
            n_scr[...] = (xhat * g_ref[...]).astype(BF16)
            acc[...] = jnp.zeros_like(acc)

        n = n_scr[...]
        a = _dot(n, wg_ref[...])
        b = _dot(n, wu_ref[...])
        a_ref[...] = a.astype(BF16)
        b_ref[...] = b.astype(BF16)
        s = (a * _sigmoid(a) * b).astype(BF16)
        acc[...] += _dot(s, wd_ref[...])

        if final is None:
            @pl.when(k == ns - 1)
            def _():
                ho_ref[...] = h_ref[...] + 0.5 * acc[...]
        else:
            n_meta, seq = final[2], final[3]

            @pl.when((i == 0) & (k == 0))
            def _():
                dgf_ref[...] = jnp.zeros_like(dgf_ref)
                loss_ref[...] = jnp.zeros_like(loss_ref)

            @pl.when(k == ns - 1)
            def _():
                h3 = h_ref[...] + 0.5 * acc[...]
                xhat, r = _rms_stats(h3)
                gf = gf_ref[...]
                row = i * tm + lax.broadcasted_iota(jnp.int32, (tm, d), 0)
                valid = (row >= n_meta) & (row < n_meta + seq)
                diff = jnp.where(valid, xhat * gf - tg_ref[...], 0.0)
                dout = diff * (1.0 / d)
                loss_ref[...] += jnp.full(loss_ref.shape, 0.5 * jnp.sum(diff * diff) * (1.0 / d), F32)
                dgf_ref[...] += jnp.sum(dout * xhat, axis=0, keepdims=True)
                ho_ref[...] = _rms_bwd(xhat, r, gf, dout)

    row_spec = pl.BlockSpec((tm, d), lambda i, k: (i, 0))
    vec_spec = pl.BlockSpec((1, d), lambda i, k: (0, 0))
    in_specs = [row_spec, vec_spec,
                pl.BlockSpec((None, d, f4), lambda i, k: (k, 0, 0)),
                pl.BlockSpec((None, d, f4), lambda i, k: (k, 0, 0)),
                pl.BlockSpec((None, f4, d), lambda i, k: (k, 0, 0))]
    act_spec = pl.BlockSpec((None, tm, f4), lambda i, k: (k, i, 0))
    out_specs = [row_spec, act_spec, act_spec]
    out_shape = [jax.ShapeDtypeStruct((tp, d), F32),
                 jax.ShapeDtypeStruct((ns, tp, f4), BF16),
                 jax.ShapeDtypeStruct((ns, tp, f4), BF16)]
    args = [h, g, wg, wu, wd]
    if final is not None:
        in_specs += [vec_spec, row_spec]
        args += [final[0], final[1]]
        out_specs += [vec_spec, pl.BlockSpec((1, LANES), lambda i, k: (0, 0))]
        out_shape += [jax.ShapeDtypeStruct((1, d), F32), jax.ShapeDtypeStruct((1, LANES), F32)]
    return pl.pallas_call(
        body, name=name, grid=(ni, ns), in_specs=in_specs, out_specs=out_specs, out_shape=out_shape,
        scratch_shapes=[pltpu.VMEM((tm, d), BF16), pltpu.VMEM((tm, d), F32)],
        compiler_params=_params(("arbitrary", "arbitrary")),
    )(*args)


def _ffn_bwd_shard(k, ns, dn_prev, dh_out, h_in, g, a, b, wg, wu, wd, name):
    tp, d = h_in.shape
    f4 = wg.shape[2]
    tm = ROW_ALIGN
    ni = tp // tm
    first, last = k == 0, k == ns - 1

    def body(*refs):
        refs = list(refs)
        acc_in = None if first else refs.pop(0)
        dh_ref, h_ref, g_ref, a_ref, b_ref, wg_ref, wu_ref, wd_ref, acc_out, dwg_ref, dwu_ref, dwd_ref = refs[:12]
        i = pl.program_id(0)

        @pl.when(i == 0)
        def _():
            dwg_ref[...] = jnp.zeros_like(dwg_ref)
            dwu_ref[...] = jnp.zeros_like(dwu_ref)
            dwd_ref[...] = jnp.zeros_like(dwd_ref)
            if last:
                refs[12][...] = jnp.zeros_like(refs[12])

        xhat, r = _rms_stats(h_ref[...])
        gv = g_ref[...]
        n = (xhat * gv).astype(BF16)
        dy = (0.5 * dh_ref[...]).astype(BF16)
        av = a_ref[...].astype(F32)
        bv = b_ref[...].astype(F32)
        sg = _sigmoid(av)
        silu = av * sg
        ds = _dot_nt(dy, wd_ref[...])
        da = (ds * bv * (sg * (1.0 + av * (1.0 - sg)))).astype(BF16)
        db = (ds * silu).astype(BF16)
        s = (silu * bv).astype(BF16)
        dwd_ref[...] += _dot_tn(s, dy)
        dwg_ref[...] += _dot_tn(n, da)
        dwu_ref[...] += _dot_tn(n, db)
        dn = _dot_nt(da, wg_ref[...]) + _dot_nt(db, wu_ref[...])
        if not first:
            dn = dn + acc_in[...]
        if last:
            refs[12][...] += jnp.sum(dn * xhat, axis=0, keepdims=True)
            acc_out[...] = dh_ref[...] + _rms_bwd(xhat, r, gv, dn)
        else:
            acc_out[...] = dn

    row_spec = pl.BlockSpec((tm, d), lambda i: (i, 0))
    vec_spec = pl.BlockSpec((1, d), lambda i: (0, 0))
    act_spec = pl.BlockSpec((None, tm, f4), lambda i: (k, i, 0))
    wcol = pl.BlockSpec((None, d, f4), lambda i: (k, 0, 0))
    wrow = pl.BlockSpec((None, f4, d), lambda i: (k, 0, 0))
    whole = lambda shape: pl.BlockSpec(shape, lambda i: (0, 0))
    in_specs = [row_spec, row_spec, vec_spec, act_spec, act_spec, wcol, wcol, wrow]
    args = [dh_out, h_in, g, a, b, wg, wu, wd]
    if not first:
        in_specs.insert(0, row_spec)
        args.insert(0, dn_prev)
    out_specs = [row_spec, whole((d, f4)), whole((d, f4)), whole((f4, d))]
    out_shape = [jax.ShapeDtypeStruct((tp, d), F32), jax.ShapeDtypeStruct((d, f4), F32),
                 jax.ShapeDtypeStruct((d, f4), F32), jax.ShapeDtypeStruct((f4, d), F32)]
    if last:
        out_specs.append(vec_spec)
        out_shape.append(jax.ShapeDtypeStruct((1, d), F32))
    return pl.pallas_call(
        body, name=f"{name}_{k}", grid=(ni,), in_specs=in_specs, out_specs=out_specs, out_shape=out_shape,
        compiler_params=_params(("arbitrary",)),
    )(*args)


def _ffn_bwd(dh_out, h_in, g, a, b, wg, wu, wd, name):
    ns = wg.shape[0]
    acc, dwg, dwu, dwd, dg = None, [], [], [], None
    for k in range(ns):
        outs = _ffn_bwd_shard(k, ns, acc, dh_out, h_in, g, a, b, wg, wu, wd, name)
        acc = outs[0]
        dwg.append(outs[1])
        dwu.append(outs[2])
        dwd.append(outs[3])
        if k == ns - 1:
            dg = outs[4]
    return acc, jnp.stack(dwg), jnp.stack(dwu), jnp.stack(dwd), dg


def _win_fwd(h, g, w_in):
    tp, d = h.shape
    ns = w_in.shape[0]
    tm = _pick_tile(tp, (768, 512, 256))

    def body(h_ref, g_ref, w_ref, u_ref, p_ref):
        @pl.when(pl.program_id(1) == 0)
        def _():
            xhat, _ = _rms_stats(h_ref[...])
            u_ref[...] = (xhat * g_ref[...]).astype(BF16)

        p_ref[...] = _dot(u_ref[...], w_ref[...]).astype(BF16)

    return pl.pallas_call(
        body, name="win_fwd", grid=(tp // tm, ns),
        in_specs=[pl.BlockSpec((tm, d), lambda i, k: (i, 0)),
                  pl.BlockSpec((1, d), lambda i, k: (0, 0)),
                  pl.BlockSpec((None, d, d), lambda i, k: (k, 0, 0))],
        out_specs=[pl.BlockSpec((tm, d), lambda i, k: (i, 0)),
                   pl.BlockSpec((None, tm, d), lambda i, k: (k, i, 0))],
        out_shape=[jax.ShapeDtypeStruct((tp, d), BF16), jax.ShapeDtypeStruct((ns, tp, d), BF16)],
        compiler_params=_params(("arbitrary", "arbitrary")),
    )(h, g, w_in)


def _win_bwd_shard(k, ns, du_prev, dpb, dug, u, w_in, h1, g, dh2):
    tp, d = h1.shape
    dh = d // 2
    tm = ROW_ALIGN
    first, last = k == 0, k == ns - 1

    def body(*refs):
        refs = list(refs)
        acc_in = None if first else refs.pop(0)
        dug_ref = refs.pop(0) if first else None
        dp_ref, u_ref, w_ref = refs[:3]
        refs = refs[3:]
        if last:
            h_ref, g_ref, dh2_ref, acc_out, dw_ref, dg_ref = refs
        else:
            acc_out, dw_ref = refs
        i = pl.program_id(0)

        @pl.when(i == 0)
        def _():
            dw_ref[...] = jnp.zeros_like(dw_ref)
            if last:
                dg_ref[...] = jnp.zeros_like(dg_ref)

        dp = dp_ref[...]
        if first:
            dp = jnp.concatenate([dug_ref[...], dp[:, dh:]], axis=1)
        dw_ref[...] += _dot_tn(u_ref[...], dp)
        du = _dot_nt(dp, w_ref[...])
        if not first:
            du = du + acc_in[...]
        if last:
            xhat, r = _rms_stats(h_ref[...])
            dg_ref[...] += jnp.sum(du * xhat, axis=0, keepdims=True)
            acc_out[...] = dh2_ref[...] + _rms_bwd(xhat, r, g_ref[...], du)
        else:
            acc_out[...] = du

    row_spec = pl.BlockSpec((tm, d), lambda i: (i, 0))
    vec_spec = pl.BlockSpec((1, d), lambda i: (0, 0))
    in_specs = [pl.BlockSpec((None, tm, d), lambda i: (k, i, 0)), row_spec,
                pl.BlockSpec((None, d, d), lambda i: (k, 0, 0))]
    args = [dpb, u, w_in]
    if first:
        in_specs.insert(0, pl.BlockSpec((tm, dh), lambda i: (i, 0)))
        args.insert(0, dug)
    else:
        in_specs.insert(0, row_spec)
        args.insert(0, du_prev)
    out_specs = [row_spec, pl.BlockSpec((d, d), lambda i: (0, 0))]
    out_shape = [jax.ShapeDtypeStruct((tp, d), F32), jax.ShapeDtypeStruct((d, d), F32)]
    if last:
        in_specs += [row_spec, vec_spec, row_spec]
        args += [h1, g, dh2]
        out_specs.append(vec_spec)
        out_shape.append(jax.ShapeDtypeStruct((1, d), F32))
    return pl.pallas_call(
        body, name=f"win_bwd_{k}", grid=(tp // tm,), in_specs=in_specs, out_specs=out_specs,
        out_shape=out_shape, compiler_params=_params(("arbitrary",)),
    )(*args)


def _win_bwd(dpb, dug, u, w_in, h1, g, dh2):
    ns = w_in.shape[0]
    acc, dws, dg = None, [], None
    for k in range(ns):
        outs = _win_bwd_shard(k, ns, acc, dpb, dug, u, w_in, h1, g, dh2)
        acc = outs[0]
        dws.append(outs[1])
        if k == ns - 1:
            dg = outs[2]
    return acc, jnp.stack(dws), dg


def _cmul(ar, ai, br, bi):
    return ar * br - ai * bi, ar * bi + ai * br


def _scan_rows(j, sub):
    return pl.ds(j * SUBLANES, SUBLANES)


def _permute_rows(src_ref, dst_ref, sub):
    for j in range(sub):
        dst_ref[pl.ds(j * SUBLANES, SUBLANES), :] = src_ref[pl.ds(j, SUBLANES, stride=sub), :]


def _unpermute_rows(src_ref, dst_ref, sub):
    for j in range(sub):
        dst_ref[pl.ds(j, SUBLANES, stride=sub), :] = src_ref[pl.ds(j * SUBLANES, SUBLANES), :]


def _local_scan(x_ref, lr, li, w, sub, reverse):
    hr = jnp.zeros((SUBLANES, w), F32)
    hi = jnp.zeros((SUBLANES, w), F32)
    order = range(sub - 1, -1, -1) if reverse else range(sub)
    for j in order:
        xr = x_ref[_scan_rows(j, sub), pl.ds(0, w)]
        xi = x_ref[_scan_rows(j, sub), pl.ds(w, w)]
        if reverse:
            hr, hi = lr * hr + li * hi + xr, lr * hi - li * hr + xi
        else:
            hr, hi = lr * hr - li * hi + xr, lr * hi + li * hr + xi
        x_ref[_scan_rows(j, sub), pl.ds(0, w)] = hr
        x_ref[_scan_rows(j, sub), pl.ds(w, w)] = hi
    return hr, hi


def _entering_states(er, ei, fr, fi, pow_ref, w, sub, reverse):
    lane = lax.broadcasted_iota(jnp.int32, (SUBLANES, w), 0)
    if reverse:
        edge, shift1 = SUBLANES - 1, SUBLANES - 1
    else:
        edge, shift1 = 0, 1
    zr = jnp.where(lane == edge, pltpu.roll(fr, shift1, 0), pltpu.roll(er, shift1, 0))
    zi = jnp.where(lane == edge, pltpu.roll(fi, shift1, 0), pltpu.roll(ei, shift1, 0))
    for step, row in ((1, sub - 1), (2, sub), (4, sub + 1)):
        ar = pow_ref[pl.ds(row, 1), pl.ds(0, w)]
        ai = pow_ref[pl.ds(row, 1), pl.ds(w, w)]
        if reverse:
            ai = -ai
            keep = lane < SUBLANES - step
            sr = jnp.where(keep, pltpu.roll(zr, SUBLANES - step, 0), 0.0)
            si = jnp.where(keep, pltpu.roll(zi, SUBLANES - step, 0), 0.0)
        else:
            keep = lane >= step
            sr = jnp.where(keep, pltpu.roll(zr, step, 0), 0.0)
            si = jnp.where(keep, pltpu.roll(zi, step, 0), 0.0)
        pr, pi = _cmul(ar, ai, sr, si)
        zr, zi = zr + pr, zi + pi
    ar = pow_ref[pl.ds(sub - 1, 1), pl.ds(0, w)]
    ai = pow_ref[pl.ds(sub - 1, 1), pl.ds(w, w)]
    if reverse:
        ai = -ai
    pr, pi = _cmul(ar, ai, zr, zi)
    return zr, zi, er + pr, ei + pi


def _scan_fwd(p, mb, mc, powt, dskip):
    _, tp, d = p.shape
    nb, cb, w2 = mb.shape
    w = w2 // 2
    q = SCAN_TILE
    sub = q // SUBLANES
    nt = tp // q
    ds = d // 2

    def body(ug_ref, mb_ref, mc_ref, pow_ref, d_ref, y_ref, bnd_ref, x_scr, carry, nat, perm):
        t = pl.program_id(1)

        @pl.when(t == 0)
        def _():
            carry[...] = jnp.zeros_like(carry)

        ugf = ug_ref[...].astype(F32)
        nat[...] = ugf
        _permute_rows(nat, perm, sub)
        x_scr[...] = _dot(perm[...].astype(BF16), mb_ref[...])
        lr = jnp.broadcast_to(pow_ref[pl.ds(0, 1), pl.ds(0, w)], (SUBLANES, w))
        li = jnp.broadcast_to(pow_ref[pl.ds(0, 1), pl.ds(w, w)], (SUBLANES, w))
        er, ei = _local_scan(x_scr, lr, li, w, sub, False)
        zr, zi, fr, fi = _entering_states(er, ei, carry[:, pl.ds(0, w)], carry[:, pl.ds(w, w)],
                                          pow_ref, w, sub, False)
        carry[:, pl.ds(0, w)] = fr
        carry[:, pl.ds(w, w)] = fi
        bnd_ref[:, pl.ds(0, w)] = fr
        bnd_ref[:, pl.ds(w, w)] = fi
        for j in range(sub):
            pr = pow_ref[pl.ds(j, 1), pl.ds(0, w)]
            pi = pow_ref[pl.ds(j, 1), pl.ds(w, w)]
            cr, ci = _cmul(pr, pi, zr, zi)
            x_scr[_scan_rows(j, sub), pl.ds(0, w)] += cr
            x_scr[_scan_rows(j, sub), pl.ds(w, w)] += ci
        hb = x_scr[...].astype(BF16)
        perm[...] = _dot_nt(hb, mc_ref[...])
        _unpermute_rows(perm, nat, sub)
        y_ref[...] = nat[...] + d_ref[...] * ugf

    return pl.pallas_call(
        body, name="s5_scan_fwd", grid=(nb, nt),
        in_specs=[pl.BlockSpec((None, q, cb), lambda b, t: (0, t, b)),
                  pl.BlockSpec((None, cb, w2), lambda b, t: (b, 0, 0)),
                  pl.BlockSpec((None, cb, w2), lambda b, t: (b, 0, 0)),
                  pl.BlockSpec((None, powt.shape[1], w2), lambda b, t: (b, 0, 0)),
                  pl.BlockSpec((1, cb), lambda b, t: (0, b))],
        out_specs=[pl.BlockSpec((q, cb), lambda b, t: (t, b)),
                   pl.BlockSpec((None, None, SUBLANES, w2), lambda b, t: (b, t, 0, 0))],
        out_shape=[jax.ShapeDtypeStruct((tp, ds), F32),
                   jax.ShapeDtypeStruct((nb, nt, SUBLANES, w2), F32)],
        scratch_shapes=[pltpu.VMEM((q, w2), F32), pltpu.VMEM((SUBLANES, w2), F32),
                        pltpu.VMEM((q, cb), F32), pltpu.VMEM((q, cb), F32)],
        compiler_params=_params(("arbitrary", "arbitrary")),
    )(p, mb, mc, powt, dskip)


def _scan_bwd(p, dy, mb, mc, powt, dskip, bnd):
    _, tp, d = p.shape
    nb, cb, w2 = mb.shape
    w = w2 // 2
    q = SCAN_TILE
    sub = q // SUBLANES
    nt = tp // q
    ds = d // 2

    def body(ug_ref, dy_ref, mb_ref, mc_ref, pow_ref, d_ref, bnd_ref,
             dug_ref, dmb_ref, dmc_ref, dlam_ref, dd_ref, x_scr, y_scr, gcarry, nat, perm):
        t = pl.program_id(1)
        tt = nt - 1 - t

        @pl.when(t == 0)
        def _():
            gcarry[...] = jnp.zeros_like(gcarry)
            dmb_ref[...] = jnp.zeros_like(dmb_ref)
            dmc_ref[...] = jnp.zeros_like(dmc_ref)
            dlam_ref[...] = jnp.zeros_like(dlam_ref)
            dd_ref[...] = jnp.zeros_like(dd_ref)

        ugf = ug_ref[...].astype(F32)
        dyf = dy_ref[...].astype(F32)
        dd_ref[...] += jnp.sum((dyf * ugf).reshape(q // SUBLANES, SUBLANES, cb), axis=0)
        nat[...] = ugf
        _permute_rows(nat, perm, sub)
        ug = perm[...].astype(BF16)
        nat[...] = dyf
        _permute_rows(nat, perm, sub)
        dyb = perm[...].astype(BF16)
        lr = jnp.broadcast_to(pow_ref[pl.ds(0, 1), pl.ds(0, w)], (SUBLANES, w))
        li = jnp.broadcast_to(pow_ref[pl.ds(0, 1), pl.ds(w, w)], (SUBLANES, w))

        x_scr[...] = _dot(ug, mb_ref[...])
        er, ei = _local_scan(x_scr, lr, li, w, sub, False)
        first = tt == 0
        pfr = jnp.where(first, 0.0, bnd_ref[:, pl.ds(0, w)])
        pfi = jnp.where(first, 0.0, bnd_ref[:, pl.ds(w, w)])
        hzr, hzi, _, _ = _entering_states(er, ei, pfr, pfi, pow_ref, w, sub, False)
        for j in range(sub):
            pr = pow_ref[pl.ds(j, 1), pl.ds(0, w)]
            pi = pow_ref[pl.ds(j, 1), pl.ds(w, w)]
            cr, ci = _cmul(pr, pi, hzr, hzi)
            x_scr[_scan_rows(j, sub), pl.ds(0, w)] += cr
            x_scr[_scan_rows(j, sub), pl.ds(w, w)] += ci

        y_scr[...] = _dot(dyb, mc_ref[...])
        er, ei = _local_scan(y_scr, lr, li, w, sub, True)
        gzr, gzi, fr, fi = _entering_states(er, ei, gcarry[:, pl.ds(0, w)], gcarry[:, pl.ds(w, w)],
                                            pow_ref, w, sub, True)
        gcarry[:, pl.ds(0, w)] = fr
        gcarry[:, pl.ds(w, w)] = fi
        accr = jnp.zeros((SUBLANES, w), F32)
        acci = jnp.zeros((SUBLANES, w), F32)
        for j in range(sub):
            pr = pow_ref[pl.ds(sub - 1 - j, 1), pl.ds(0, w)]
            pi = pow_ref[pl.ds(sub - 1 - j, 1), pl.ds(w, w)]
            cr, ci = _cmul(pr, -pi, gzr, gzi)
            gr = y_scr[_scan_rows(j, sub), pl.ds(0, w)] + cr
            gi = y_scr[_scan_rows(j, sub), pl.ds(w, w)] + ci
            y_scr[_scan_rows(j, sub), pl.ds(0, w)] = gr
            y_scr[_scan_rows(j, sub), pl.ds(w, w)] = gi
            if j == 0:
                hpr, hpi = hzr, hzi
            else:
                hpr = x_scr[_scan_rows(j - 1, sub), pl.ds(0, w)]
                hpi = x_scr[_scan_rows(j - 1, sub), pl.ds(w, w)]
            accr += hpr * gr + hpi * gi
            acci += hpr * gi - hpi * gr
        dlam_ref[:, pl.ds(0, w)] += accr
        dlam_ref[:, pl.ds(w, w)] += acci

        hb = x_scr[...].astype(BF16)
        gb = y_scr[...].astype(BF16)
        dmc_ref[...] += _dot_tn(dyb, hb)
        dmb_ref[...] += _dot_tn(ug, gb)
        perm[...] = _dot_nt(gb, mb_ref[...])
        _unpermute_rows(perm, nat, sub)
        dug_ref[...] = (nat[...] + d_ref[...] * dyf).astype(BF16)

    blk = lambda b, t: (b, 0, 0)
    return pl.pallas_call(
        body, name="s5_scan_bwd", grid=(nb, nt),
        in_specs=[pl.BlockSpec((None, q, cb), lambda b, t: (0, nt - 1 - t, b)),
                  pl.BlockSpec((q, cb), lambda b, t: (nt - 1 - t, b)),
                  pl.BlockSpec((None, cb, w2), blk),
                  pl.BlockSpec((None, cb, w2), blk),
                  pl.BlockSpec((None, powt.shape[1], w2), blk),
                  pl.BlockSpec((1, cb), lambda b, t: (0, b)),
                  pl.BlockSpec((None, None, SUBLANES, w2),
                               lambda b, t: (b, jnp.maximum(nt - 2 - t, 0), 0, 0))],
        out_specs=[pl.BlockSpec((q, cb), lambda b, t: (nt - 1 - t, b)),
                   pl.BlockSpec((None, cb, w2), blk),
                   pl.BlockSpec((None, cb, w2), blk),
                   pl.BlockSpec((None, SUBLANES, w2), blk),
                   pl.BlockSpec((SUBLANES, cb), lambda b, t: (0, b))],
        out_shape=[jax.ShapeDtypeStruct((tp, ds), BF16),
                   jax.ShapeDtypeStruct((nb, cb, w2), F32),
                   jax.ShapeDtypeStruct((nb, cb, w2), F32),
                   jax.ShapeDtypeStruct((nb, SUBLANES, w2), F32),
                   jax.ShapeDtypeStruct((SUBLANES, ds), F32)],
        scratch_shapes=[pltpu.VMEM((q, w2), F32), pltpu.VMEM((q, w2), F32),
                        pltpu.VMEM((SUBLANES, w2), F32), pltpu.VMEM((q, cb), F32), pltpu.VMEM((q, cb), F32)],
        compiler_params=_params(("arbitrary", "arbitrary")),
    )(p, dy, mb, mc, powt, dskip, bnd)


HALO = 16


def _mix_tile(ys5, p0, p1, p2, p3, prev_cin, cw, bgate, wglu, wco, d):
    dh = d // 2
    tm = ys5.shape[0]
    v = p0[:, dh:].astype(F32)
    gbr = p1[:, :dh].astype(F32)
    gcr = p1[:, dh:].astype(F32)
    gact = _gelu(ys5).astype(BF16)
    z = _dot(gact, wglu)
    z1, z2 = z[:, :d], z[:, d:]
    sg = _sigmoid(z2)
    y_ssm = z1 * sg
    cin = gcr * v
    ext = jnp.concatenate([cin, prev_cin], axis=0)
    r1 = pltpu.roll(ext, 1, 0)[:tm]
    r2 = pltpu.roll(ext, 2, 0)[:tm]
    cv = cw[2] * cin + cw[1] * r1 + cw[0] * r2
    cg = (gbr * cv).astype(BF16)
    y_conv = _dot(cg, wco)
    g_s = _sigmoid(p2.astype(F32) + bgate[:, :d])
    g_c = _sigmoid(p3.astype(F32) + bgate[:, d:])
    mixed = g_s * y_ssm + g_c * y_conv
    return dict(v=v, gb=gbr, gc=gcr, gact=gact, z1=z1, sg=sg, y_ssm=y_ssm, cin=cin, r1=r1, r2=r2,
                cv=cv, cg=cg, y_conv=y_conv, g_s=g_s, g_c=g_c, mixed=mixed)


def _mix_fwd(h1, ys5, p, cw, bgate, wglu, wco, wo):
    tp, d = h1.shape
    dh = d // 2
    tm = ROW_ALIGN

    def body(h_ref, y_ref, p0_ref, p1_ref, p2_ref, p3_ref, cw_ref, bg_ref, wglu_ref, wco_ref, wo_ref,
             o_ref, prev):
        @pl.when(pl.program_id(0) == 0)
        def _():
            prev[...] = jnp.zeros_like(prev)

        cw = [cw_ref[pl.ds(t, 1), :] for t in range(3)]
        f = _mix_tile(y_ref[...], p0_ref[...], p1_ref[...], p2_ref[...], p3_ref[...], prev[...],
                      cw, bg_ref[...], wglu_ref[...], wco_ref[...], d)
        prev[...] = f["cin"][tm - HALO:, :]
        o_ref[...] = h_ref[...] + _dot(f["mixed"].astype(BF16), wo_ref[...])

    row = pl.BlockSpec((tm, d), lambda i: (i, 0))
    full = lambda a: pl.BlockSpec(a.shape, lambda i: (0,) * a.ndim)
    pk = lambda k: pl.BlockSpec((None, tm, d), lambda i, k=k: (k, i, 0))
    return pl.pallas_call(
        body, name="mix_fwd", grid=(tp // tm,),
        in_specs=[row, pl.BlockSpec((tm, dh), lambda i: (i, 0)), pk(0), pk(1), pk(2), pk(3),
                  full(cw), full(bgate), full(wglu), full(wco), full(wo)],
        out_specs=row, out_shape=jax.ShapeDtypeStruct((tp, d), F32),
        scratch_shapes=[pltpu.VMEM((HALO, dh), F32)],
        compiler_params=_params(("arbitrary",)),
    )(h1, ys5, p, p, p, p, cw, bgate, wglu, wco, wo)


def _mix_bwd(dh2, ys5, p, cw, bgate, wglu, wco, wo):
    tp, d = dh2.shape
    dh = d // 2
    tm = ROW_ALIGN
    ni = tp // tm
    hb = tm // HALO

    def body(dh_ref, y_ref, p0_ref, p1_ref, p2_ref, p3_ref, h0_ref, h1_ref,
             cw_ref, bg_ref, wglu_ref, wco_ref, wo_ref,
             dys_ref, dpb_ref, dwo_ref, dwglu_ref, dwco_ref, dcw_ref, dbg_ref, nxt):
        i = pl.program_id(0)
        tt = ni - 1 - i

        @pl.when(i == 0)
        def _():
            nxt[...] = jnp.zeros_like(nxt)
            dwo_ref[...] = jnp.zeros_like(dwo_ref)
            dwglu_ref[...] = jnp.zeros_like(dwglu_ref)
            dwco_ref[...] = jnp.zeros_like(dwco_ref)
            dcw_ref[...] = jnp.zeros_like(dcw_ref)
            dbg_ref[...] = jnp.zeros_like(dbg_ref)

        cw = [cw_ref[pl.ds(t, 1), :] for t in range(3)]
        prev_cin = h1_ref[:, dh:].astype(F32) * h0_ref[:, dh:].astype(F32)
        prev_cin = jnp.where(tt == 0, 0.0, prev_cin)
        ys5 = y_ref[...]
        f = _mix_tile(ys5, p0_ref[...], p1_ref[...], p2_ref[...], p3_ref[...], prev_cin,
                      cw, bg_ref[...], wglu_ref[...], wco_ref[...], d)
        dhb = dh_ref[...].astype(BF16)
        dmixed = _dot_nt(dhb, wo_ref[...])
        dwo_ref[...] += _dot_tn(f["mixed"].astype(BF16), dhb)

        g_s, g_c, sg = f["g_s"], f["g_c"], f["sg"]
        dy_ssm = dmixed * g_s
        dy_conv = dmixed * g_c
        dp2 = dmixed * f["y_ssm"] * g_s * (1.0 - g_s)
        dp3 = dmixed * f["y_conv"] * g_c * (1.0 - g_c)
        dbg_ref[:, pl.ds(0, d)] += jnp.sum(dp2, axis=0, keepdims=True)
        dbg_ref[:, pl.ds(d, d)] += jnp.sum(dp3, axis=0, keepdims=True)

        dz = jnp.concatenate([dy_ssm * sg, dy_ssm * f["z1"] * sg * (1.0 - sg)], axis=1).astype(BF16)
        dwglu_ref[...] += _dot_tn(f["gact"], dz)
        dys_ref[...] = (_dot_nt(dz, wglu_ref[...]) * _gelu_grad(ys5)).astype(BF16)

        dycb = dy_conv.astype(BF16)
        dwco_ref[...] += _dot_tn(f["cg"], dycb)
        dcg = _dot_nt(dycb, wco_ref[...])
        dgb = dcg * f["cv"]
        dcv = dcg * f["gb"]
        ext = jnp.concatenate([dcv, nxt[...]], axis=0)
        n1 = pltpu.roll(ext, tm + HALO - 1, 0)[:tm]
        n2 = pltpu.roll(ext, tm + HALO - 2, 0)[:tm]
        nxt[...] = dcv[:HALO, :]
        dcin = cw[2] * dcv + cw[1] * n1 + cw[0] * n2
        dcw_ref[pl.ds(0, 1), :] += jnp.sum(dcv * f["r2"], axis=0, keepdims=True)
        dcw_ref[pl.ds(1, 1), :] += jnp.sum(dcv * f["r1"], axis=0, keepdims=True)
        dcw_ref[pl.ds(2, 1), :] += jnp.sum(dcv * f["cin"], axis=0, keepdims=True)
        dgc = dcin * f["v"]
        dv = dcin * f["gc"]
        dpb_ref[0] = jnp.concatenate([jnp.zeros_like(dv), dv], axis=1).astype(BF16)
        dpb_ref[1] = jnp.concatenate([dgb, dgc], axis=1).astype(BF16)
        dpb_ref[2] = dp2.astype(BF16)
        dpb_ref[3] = dp3.astype(BF16)

    rev = lambda i: ni - 1 - i
    row = pl.BlockSpec((tm, d), lambda i: (rev(i), 0))
    half = pl.BlockSpec((tm, dh), lambda i: (rev(i), 0))
    full = lambda a: pl.BlockSpec(a.shape, lambda i: (0,) * a.ndim)
    pk = lambda k: pl.BlockSpec((None, tm, d), lambda i, k=k: (k, rev(i), 0))
    halo = lambda k: pl.BlockSpec((None, HALO, d), lambda i, k=k: (k, jnp.maximum(rev(i) * hb - 1, 0), 0))
    acc = lambda shape: pl.BlockSpec(shape, lambda i: (0,) * len(shape))
    return pl.pallas_call(
        body, name="mix_bwd", grid=(ni,),
        in_specs=[row, half, pk(0), pk(1), pk(2), pk(3), halo(0), halo(1),
                  full(cw), full(bgate), full(wglu), full(wco), full(wo)],
        out_specs=[half, pl.BlockSpec((4, tm, d), lambda i: (0, rev(i), 0)),
                   acc((d, d)), acc((dh, 2 * d)), acc((dh, d)), acc((SUBLANES, dh)), acc((1, 2 * d))],
        out_shape=[jax.ShapeDtypeStruct((tp, dh), BF16), jax.ShapeDtypeStruct((4, tp, d), BF16),
                   jax.ShapeDtypeStruct((d, d), F32), jax.ShapeDtypeStruct((dh, 2 * d), F32),
                   jax.ShapeDtypeStruct((dh, d), F32), jax.ShapeDtypeStruct((SUBLANES, dh), F32),
                   jax.ShapeDtypeStruct((1, 2 * d), F32)],
        scratch_shapes=[pltpu.VMEM((HALO, dh), F32)],
        compiler_params=_params(("arbitrary",)),
    )(dh2, ys5, p, p, p, p, p, p, cw, bgate, wglu, wco, wo)


ANY = pl.BlockSpec(memory_space=pl.ANY)


def _position():
    return lax.axis_index("x"), lax.axis_index("y"), lax.axis_index("c")


def _remote(src, dst, ssem, rsem, dev):
    return pltpu.make_async_remote_copy(src_ref=src, dst_ref=dst, send_sem=ssem, recv_sem=rsem,
                                        device_id=dev, device_id_type=MESH)


def _cast_piece(w, pos):
    rows, cols = w.shape
    r2 = rows // 2

    def body(pos_ref, w_ref, o_ref):
        o_ref[...] = w_ref[...].astype(BF16)

    return pl.pallas_call(
        body, name="cast_piece",
        grid_spec=pltpu.PrefetchScalarGridSpec(
            num_scalar_prefetch=1, grid=(1,),
            in_specs=[pl.BlockSpec((r2, cols), lambda i, pos: (pos[2], 0))],
            out_specs=pl.BlockSpec((r2, cols), lambda i, pos: (0, 0))),
        out_shape=jax.ShapeDtypeStruct((r2, cols), BF16),
        compiler_params=_params(("arbitrary",)),
    )(pos, w)


def _allgather(pieces, smalls):
    n, ns = len(pieces), len(smalls)
    n_big = 7 * n

    def body(*refs):
        pin, sin = refs[:n], refs[n:n + ns]
        wall, sall = refs[n + ns:2 * n + ns], refs[2 * n + ns:2 * n + 2 * ns]
        ssem, rsem, lsem = refs[2 * n + 2 * ns:]
        x, y, c = _position()
        local = [pltpu.make_async_copy(pin[i], wall[i].at[x, y, c], lsem.at[i]) for i in range(n)]
        local += [pltpu.make_async_copy(sin[i], sall[i].at[2 * x + y], lsem.at[n + i]) for i in range(ns)]
        for cp in local:
            cp.start()
        chips = [(1 - x, y), (x, 1 - y), (1 - x, 1 - y)]
        small = []
        for j, (px, py) in enumerate(chips):
            for i in range(ns):
                s = n_big + j * ns + i
                small.append(_remote(sin[i], sall[i].at[2 * x + y], ssem.at[s], rsem.at[s], (px, py, c)))
        for cp in small:
            cp.start()
        sends = []
        stage = [_remote(pin[i], wall[i].at[x, y, c], ssem.at[i], rsem.at[i], (1 - x, y, c)) for i in range(n)]
        for cp in stage:
            cp.start()
        for i in range(n):
            _remote(pin[i], wall[i].at[1 - x, y, c], ssem.at[i], rsem.at[i], (1 - x, y, c)).wait_recv()
        sends += stage
        for cp in local:
            cp.wait()
        stage = []
        for i in range(n):
            for xx in range(2):
                s = n + 2 * i + xx
                blk = wall[i].at[xx, y, c]
                stage.append(_remote(blk, blk, ssem.at[s], rsem.at[s], (x, 1 - y, c)))
        for cp in stage:
            cp.start()
        for i in range(n):
            for xx in range(2):
                s = n + 2 * i + xx
                blk = wall[i].at[xx, 1 - y, c]
                _remote(blk, blk, ssem.at[s], rsem.at[s], (x, 1 - y, c)).wait_recv()
        sends += stage
        stage = []
        for i in range(n):
            for xy in range(4):
                s = 3 * n + 4 * i + xy
                blk = wall[i].at[xy // 2, xy % 2, c]
                stage.append(_remote(blk, blk, ssem.at[s], rsem.at[s], (x, y, 1 - c)))
        for cp in stage:
            cp.start()
        for i in range(n):
            for xy in range(4):
                s = 3 * n + 4 * i + xy
                blk = wall[i].at[xy // 2, xy % 2, 1 - c]
                _remote(blk, blk, ssem.at[s], rsem.at[s], (x, y, 1 - c)).wait_recv()
        sends += stage
        for j, (px, py) in enumerate(chips):
            for i in range(ns):
                s = n_big + j * ns + i
                _remote(sin[i], sall[i].at[2 * px + py], ssem.at[s], rsem.at[s], (px, py, c)).wait_recv()
        for cp in sends + small:
            cp.wait_send()

    nsem = n_big + 3 * ns
    out_shape = [jax.ShapeDtypeStruct((2, 2, 2) + a.shape, a.dtype) for a in pieces]
    out_shape += [jax.ShapeDtypeStruct((4,) + a.shape, a.dtype) for a in smalls]
    outs = pl.pallas_call(
        body, name="allgather_weights",
        in_specs=[ANY] * (n + ns), out_specs=[ANY] * (n + ns), out_shape=out_shape,
        scratch_shapes=[pltpu.SemaphoreType.DMA((nsem,)), pltpu.SemaphoreType.DMA((nsem,)),
                        pltpu.SemaphoreType.DMA((n + ns,))],
    )(*pieces, *smalls)
    return outs[:n], outs[n:]


def _exchange(name, arrays, out_shapes, plan):
    na = len(arrays)
    no = len(out_shapes)
    probe = plan([None] * na, [None] * no, None)

    def body(*refs):
        ins, outs = refs[:na], refs[na:na + no]
        ssem, rsem = refs[na + no:]
        copies = [_remote(src, dst, ssem.at[j], rsem.at[j], peer)
                  for j, (src, dst, peer) in enumerate(plan(ins, outs, _position()))]
        for cp in copies:
            cp.start()
        for cp in copies:
            cp.wait()

    return pl.pallas_call(
        body, name=name, in_specs=[ANY] * na, out_specs=[ANY] * no, out_shape=out_shapes,
        scratch_shapes=[pltpu.SemaphoreType.DMA((probe,)), pltpu.SemaphoreType.DMA((probe,))],
    )(*arrays)


class _Grad:
    def __init__(self, arr, kind, shard_shape):
        self.arr, self.kind = arr, kind
        self.rows, self.cols = shard_shape
        self.r2 = self.rows // 2

    def view(self, ref, k, h):
        r2 = self.r2
        if self.kind == "stacked":
            return ref.at[k, pl.ds(h * r2, r2), :]
        if self.kind == "col":
            return ref.at[pl.ds(h * r2, r2), pl.ds(k * self.cols, self.cols)]
        return ref.at[pl.ds((2 * k + h) * r2, r2), :]

    def spec(self, kh):
        r2, cols = self.r2, self.cols
        if self.kind == "stacked":
            return pl.BlockSpec((None, r2, cols), lambda i, pos: (kh(i, pos)[0], kh(i, pos)[1], 0))
        if self.kind == "col":
            return pl.BlockSpec((r2, cols), lambda i, pos: (kh(i, pos)[1], kh(i, pos)[0]))
        return pl.BlockSpec((r2, cols), lambda i, pos: (2 * kh(i, pos)[0] + kh(i, pos)[1], 0))


def _add_stage(name, mine, mine_spec, recv, n_out, pos):
    r2, cols = recv.shape[-2:]

    def body(pos_ref, m_ref, r_ref, of_ref, ob_ref):
        s = m_ref[...] + r_ref[...].astype(F32)
        of_ref[...] = s
        ob_ref[...] = s.astype(BF16)

    blk = pl.BlockSpec((None, r2, cols), lambda i, pos: (i, 0, 0))
    return pl.pallas_call(
        body, name=name,
        grid_spec=pltpu.PrefetchScalarGridSpec(
            num_scalar_prefetch=1, grid=(n_out,), in_specs=[mine_spec, blk], out_specs=[blk, blk]),
        out_shape=[jax.ShapeDtypeStruct((n_out, r2, cols), F32), jax.ShapeDtypeStruct((n_out, r2, cols), BF16)],
        compiler_params=_params(("arbitrary",)),
    )(pos, mine, recv)


def _adamw_math(w, g, m, v):
    m = ADAM_B1 * m + (1.0 - ADAM_B1) * g
    v = ADAM_B2 * v + (1.0 - ADAM_B2) * (g * g)
    m_hat = m / (1.0 - ADAM_B1 ** ADAM_STEP)
    v_hat = v / (1.0 - ADAM_B2 ** ADAM_STEP)
    delta = -ADAM_LR * (m_hat / (jnp.sqrt(v_hat) + ADAM_EPS) + ADAM_WD * w)
    return delta, m, v


def _adamw_big(w, m, v, own, sib, pos):
    rows, cols = w.shape
    r2 = rows // 2

    def body(pos_ref, w_ref, m_ref, v_ref, own_ref, sib_ref, g_ref, d_ref, nm_ref, nv_ref):
        h = pl.program_id(0)
        g = jnp.where(h == pos_ref[2], own_ref[...], sib_ref[...])
        g_ref[...] = g
        d_ref[...], nm_ref[...], nv_ref[...] = _adamw_math(w_ref[...], g, m_ref[...], v_ref[...])

    half = pl.BlockSpec((r2, cols), lambda h, pos: (h, 0))
    whole = pl.BlockSpec((r2, cols), lambda h, pos: (0, 0))
    out = jax.ShapeDtypeStruct((rows, cols), F32)
    return pl.pallas_call(
        body, name="adamw",
        grid_spec=pltpu.PrefetchScalarGridSpec(
            num_scalar_prefetch=1, grid=(2,),
            in_specs=[half, half, half, whole, whole],
            out_specs=[half, half, half, half]),
        out_shape=[out, out, out, out],
        compiler_params=_params(("arbitrary",)),
    )(pos, w, m, v, own, sib)


def _own_sum(s2f, recv3, pos):
    _, r2, cols = s2f.shape

    def body(pos_ref, s_ref, r_ref, o_ref):
        o_ref[...] = s_ref[...] + r_ref[...].astype(F32)

    whole = pl.BlockSpec((r2, cols), lambda i, pos: (0, 0))
    return pl.pallas_call(
        body, name="own_sum",
        grid_spec=pltpu.PrefetchScalarGridSpec(
            num_scalar_prefetch=1, grid=(1,),
            in_specs=[pl.BlockSpec((None, r2, cols), lambda i, pos: (pos[0], 0, 0)), whole],
            out_specs=whole),
        out_shape=jax.ShapeDtypeStruct((r2, cols), F32),
        compiler_params=_params(("arbitrary",)),
    )(pos, s2f, recv3)


def _allreduce_small(buf):
    def body(x_ref, o_ref, recv, ssem, rsem):
        x, y, c = _position()
        o_ref[...] = x_ref[...]
        for s, peer in enumerate([(x, y, 1 - c), (x, 1 - y, c), (1 - x, y, c)]):
            cp = _remote(o_ref, recv.at[s], ssem.at[s], rsem.at[s], peer)
            cp.start()
            cp.wait()
            o_ref[...] = o_ref[...] + recv[s]

    vm = pl.BlockSpec(memory_space=pltpu.VMEM)
    return pl.pallas_call(
        body, name="allreduce_small", in_specs=[vm], out_specs=vm,
        out_shape=jax.ShapeDtypeStruct(buf.shape, F32),
        scratch_shapes=[pltpu.VMEM((3,) + buf.shape, F32),
                        pltpu.SemaphoreType.DMA((3,)), pltpu.SemaphoreType.DMA((3,))],
    )(buf)


def _adamw_small(w, g, m, v):
    def body(w_ref, g_ref, m_ref, v_ref, d_ref, nm_ref, nv_ref):
        d_ref[...], nm_ref[...], nv_ref[...] = _adamw_math(w_ref[...], g_ref[...], m_ref[...], v_ref[...])

    vm = pl.BlockSpec(memory_space=pltpu.VMEM)
    out = jax.ShapeDtypeStruct(w.shape, F32)
    return pl.pallas_call(body, name="adamw_small", in_specs=[vm] * 4, out_specs=[vm] * 3,
                          out_shape=[out, out, out])(w, g, m, v)


def _reduce_scatter_adamw(grads, weights, pos):
    n = len(grads)
    arrs = [g.arr for g in grads]

    def plan1(ins, outs, p):
        if p is None:
            return 4 * n
        x, y, c = p
        return [(grads[i].view(ins[i], k, 1 - c), outs[i].at[k], (x, y, 1 - c))
                for i in range(n) for k in range(4)]

    recv1 = _exchange("rs_exchange_c", arrs, [jax.ShapeDtypeStruct((4, g.r2, g.cols), F32) for g in grads], plan1)
    s1 = [_add_stage("rs_add_c", g.arr, g.spec(lambda i, p: (i, p[2])), r, 4, pos)
          for g, r in zip(grads, recv1)]

    def plan2(ins, outs, p):
        if p is None:
            return 2 * n
        x, y, c = p
        return [(ins[i].at[2 * xx + (1 - y)], outs[i].at[xx], (x, 1 - y, c))
                for i in range(n) for xx in range(2)]

    recv2 = _exchange("rs_exchange_y", [s[1] for s in s1],
                      [jax.ShapeDtypeStruct((2, g.r2, g.cols), BF16) for g in grads], plan2)
    s2 = [_add_stage("rs_add_y", s[0], pl.BlockSpec((None, g.r2, g.cols), lambda i, p: (2 * i + p[1], 0, 0)),
                     r, 2, pos) for g, s, r in zip(grads, s1, recv2)]

    def plan3(ins, outs, p):
        if p is None:
            return n
        x, y, c = p
        return [(ins[i].at[1 - x], outs[i], (1 - x, y, c)) for i in range(n)]

    recv3 = _exchange("rs_exchange_x", [s[1] for s in s2],
                      [jax.ShapeDtypeStruct((g.r2, g.cols), BF16) for g in grads], plan3)
    own = [_own_sum(s[0], r, pos) for s, r in zip(s2, recv3)]

    def plan4(ins, outs, p):
        if p is None:
            return n
        x, y, c = p
        return [(ins[i], outs[i], (x, y, 1 - c)) for i in range(n)]

    sib = _exchange("rs_exchange_sibling", own, [jax.ShapeDtypeStruct((g.r2, g.cols), F32) for g in grads], plan4)
    return [_adamw_big(w, m, v, o, sb, pos) for (w, m, v), o, sb in zip(weights, own, sib)]


def _block_diag(t, nb):
    g, c, p = t.shape
    gb = g // nb
    t = t.reshape(nb, gb, c, p)
    eye = jnp.eye(gb, dtype=t.dtype)
    return jnp.einsum("bgcp,gh->bgchp", t, eye).reshape(nb, gb * c, gb * p)


def _s5_discretise(a_re, a_im, log_dt, b_re, b_im, c_re, c_im):
    g, p = a_re.shape
    nb = g // GROUPS_PER_BLOCK
    dt = jnp.exp(log_dt)[:, None]
    mag = jnp.exp(a_re * dt)
    lam_re = mag * jnp.cos(a_im * dt)
    lam_im = mag * jnp.sin(a_im * dt)
    den = a_re * a_re + a_im * a_im
    q_re = ((lam_re - 1.0) * a_re + lam_im * a_im) / den
    q_im = (lam_im * a_re - (lam_re - 1.0) * a_im) / den
    bb_re = q_re[..., None] * b_re - q_im[..., None] * b_im
    bb_im = q_re[..., None] * b_im + q_im[..., None] * b_re
    tr = lambda t: jnp.swapaxes(t, 1, 2)
    mb = jnp.concatenate([_block_diag(tr(bb_re), nb), _block_diag(tr(bb_im), nb)], axis=-1)
    mc = jnp.concatenate([_block_diag(c_re, nb), -_block_diag(c_im, nb)], axis=-1)
    lam = jnp.concatenate([lam_re.reshape(nb, -1), lam_im.reshape(nb, -1)], axis=-1)
    return mb, mc, lam


def _s5_powers(a_re, a_im, log_dt, sub):
    g, p = a_re.shape
    nb = g // GROUPS_PER_BLOCK
    dt = jnp.exp(log_dt)[:, None]
    ns = list(range(1, sub + 1)) + [2 * sub, 4 * sub]
    ns += [0] * (-len(ns) % SUBLANES)
    e = jnp.asarray(ns, F32)[:, None, None]
    mag = jnp.exp(a_re[None] * dt[None] * e)
    ang = a_im[None] * dt[None] * e
    re = (mag * jnp.cos(ang)).reshape(len(ns), nb, -1)
    im = (mag * jnp.sin(ang)).reshape(len(ns), nb, -1)
    return jnp.transpose(jnp.concatenate([re, im], axis=-1), (1, 0, 2))


def _pack(parts):
    flat = jnp.concatenate([a.reshape(-1).astype(F32) for a in parts])
    n = flat.shape[0]
    pad = -n % (SUBLANES * LANES)
    return jnp.pad(flat, (0, pad)).reshape(-1, LANES)


def _unpack(buf, like):
    flat = buf.reshape(-1)
    out, o = [], 0
    for a in like:
        out.append(flat[o:o + a.size].reshape(a.shape))
        o += a.size
    return out


def kernel(x, meta_tokens, g_ffn1, ffn1_w_gate, ffn1_w_up, ffn1_w_down, g_mix, w_in, b_gate, ssm_a_re, ssm_a_im, ssm_log_dt, ssm_b_re, ssm_b_im, ssm_c_re, ssm_c_im, ssm_d, ssm_w_glu, conv_w, conv_w_out, w_o, g_ffn2, ffn2_w_gate, ffn2_w_up, ffn2_w_down, g_final, loss_target, m_meta_tokens, m_g_ffn1, m_ffn1_w_gate, m_ffn1_w_up, m_ffn1_w_down, m_g_mix, m_w_in, m_b_gate, m_ssm_a_re, m_ssm_a_im, m_ssm_log_dt, m_ssm_b_re, m_ssm_b_im, m_ssm_c_re, m_ssm_c_im, m_ssm_d, m_ssm_w_glu, m_conv_w, m_conv_w_out, m_w_o, m_g_ffn2, m_ffn2_w_gate, m_ffn2_w_up, m_ffn2_w_down, m_g_final, v_meta_tokens, v_g_ffn1, v_ffn1_w_gate, v_ffn1_w_up, v_ffn1_w_down, v_g_mix, v_w_in, v_b_gate, v_ssm_a_re, v_ssm_a_im, v_ssm_log_dt, v_ssm_b_re, v_ssm_b_im, v_ssm_c_re, v_ssm_c_im, v_ssm_d, v_ssm_w_glu, v_conv_w, v_conv_w_out, v_w_o, v_g_ffn2, v_ffn2_w_gate, v_ffn2_w_up, v_ffn2_w_down, v_g_final):
    seq, d = x.shape[1], x.shape[2]
    n_meta = meta_tokens.shape[0]
    dh = d // 2
    tp = -(-(n_meta + seq) // ROW_ALIGN) * ROW_ALIGN
    mx, my, mc_ = _position()
    pos = jnp.stack([mx, my, mc_]).astype(jnp.int32)
    shard = 2 * mx + my

    big_names = ["ffn1_w_gate", "ffn1_w_up", "ffn1_w_down", "w_in", "ssm_w_glu", "conv_w_out", "w_o",
                 "ffn2_w_gate", "ffn2_w_up", "ffn2_w_down"]
    big_w = [ffn1_w_gate[0], ffn1_w_up[0], ffn1_w_down[0], w_in[0], ssm_w_glu[0], conv_w_out[0], w_o[0],
             ffn2_w_gate[0], ffn2_w_up[0], ffn2_w_down[0]]
    big_m = [m_ffn1_w_gate[0], m_ffn1_w_up[0], m_ffn1_w_down[0], m_w_in[0], m_ssm_w_glu[0], m_conv_w_out[0],
             m_w_o[0], m_ffn2_w_gate[0], m_ffn2_w_up[0], m_ffn2_w_down[0]]
    big_v = [v_ffn1_w_gate[0], v_ffn1_w_up[0], v_ffn1_w_down[0], v_w_in[0], v_ssm_w_glu[0], v_conv_w_out[0],
             v_w_o[0], v_ffn2_w_gate[0], v_ffn2_w_up[0], v_ffn2_w_down[0]]
    pieces = [_cast_piece(w, pos) for w in big_w]
    conv_local = conv_w.reshape(conv_w.shape[1], conv_w.shape[3])
    walls, smalls = _allgather(pieces, [meta_tokens, conv_local])
    stacked = [wl.reshape((4, 2 * wl.shape[3], wl.shape[4])) for wl in walls]
    w1g, w1u, w1d, win_all, wglu_s, wco_s, wo_s, w2g, w2u, w2d = stacked
    natural_cols = lambda s: jnp.transpose(s, (1, 0, 2)).reshape(s.shape[1], 4 * s.shape[2])
    wglu_all = natural_cols(wglu_s)
    wco_all = natural_cols(wco_s)
    wo_all = wo_s.reshape(d, d)
    meta_full = natural_cols(smalls[0])
    cw_full = natural_cols(smalls[1])
    cw_pad = jnp.pad(cw_full, ((0, SUBLANES - cw_full.shape[0]), (0, 0)))

    s5_args = (ssm_a_re[0], ssm_a_im[0], ssm_log_dt[0], ssm_b_re[0], ssm_b_im[0], ssm_c_re[0], ssm_c_im[0])
    (mb, mc, _), disc_vjp = jax.vjp(_s5_discretise, *s5_args)
    powt = _s5_powers(ssm_a_re[0], ssm_a_im[0], ssm_log_dt[0], SCAN_TILE // SUBLANES)
    mb16, mc16 = mb.astype(BF16), mc.astype(BF16)

    pad_rows = tp - n_meta - seq
    h0 = jnp.concatenate([meta_full, x[0], jnp.zeros((pad_rows, d), F32)], axis=0)
    tgt = jnp.concatenate([jnp.zeros((n_meta, d), F32), loss_target[0], jnp.zeros((pad_rows, d), F32)], axis=0)
    h1, a1, b1 = _ffn_fwd(h0, g_ffn1, w1g, w1u, w1d, "ffn1_fwd")
    u, p = _win_fwd(h1, g_mix, win_all)
    ys5, bnd = _scan_fwd(p, mb16, mc16, powt, ssm_d)
    h2 = _mix_fwd(h1, ys5, p, cw_pad, b_gate, wglu_all, wco_all, wo_all)
    dh3, a2, b2, dg_final, loss_part = _ffn_fwd(
        h2, g_ffn2, w2g, w2u, w2d, "ffn2_fwd_loss", final=(g_final.reshape(1, d), tgt, n_meta, seq))

    dh2, dw2g, dw2u, dw2d, dg_ffn2 = _ffn_bwd(dh3, h2, g_ffn2, a2, b2, w2g, w2u, w2d, "ffn2_bwd")
    dys5, dpb, dwo, dwglu, dwco, dcw, dbg = _mix_bwd(dh2, ys5, p, cw_pad, b_gate, wglu_all, wco_all, wo_all)
    dug, dmb, dmc, dlam, dd = _scan_bwd(p, dys5, mb16, mc16, powt, ssm_d, bnd)
    dh1, dwin, dg_mix = _win_bwd(dpb, dug, u, win_all, h1, g_mix, dh2)
    dh0, dw1g, dw1u, dw1d, dg_ffn1 = _ffn_bwd(dh1, h0, g_ffn1, a1, b1, w1g, w1u, w1d, "ffn1_bwd")

    s5_grads = disc_vjp((dmb, dmc, jnp.sum(dlam, axis=1)))
    grad_x = dh0[n_meta:n_meta + seq][None]

    shapes = [w.shape for w in big_w]
    kinds = ["stacked", "stacked", "stacked", "stacked", "col", "col", "row", "stacked", "stacked", "stacked"]
    full_grads = [dw1g, dw1u, dw1d, dwin, dwglu, dwco, dwo, dw2g, dw2u, dw2d]
    grads = [_Grad(a, k, s) for a, k, s in zip(full_grads, kinds, shapes)]
    big_out = _reduce_scatter_adamw(grads, list(zip(big_w, big_m, big_v)), pos)
    big_out = {nme: tuple(o[None] for o in outs) for nme, outs in zip(big_names, big_out)}

    small_names = ["g_ffn1", "g_mix", "b_gate", "ssm_a_re", "ssm_a_im", "ssm_log_dt", "ssm_b_re", "ssm_b_im",
                   "ssm_c_re", "ssm_c_im", "ssm_d", "g_ffn2", "g_final", "meta_tokens", "conv_w"]
    small_w = [g_ffn1, g_mix, b_gate, ssm_a_re, ssm_a_im, ssm_log_dt, ssm_b_re, ssm_b_im, ssm_c_re, ssm_c_im,
               ssm_d, g_ffn2, g_final, meta_tokens, conv_w]
    small_m = [m_g_ffn1, m_g_mix, m_b_gate, m_ssm_a_re, m_ssm_a_im, m_ssm_log_dt, m_ssm_b_re, m_ssm_b_im,
               m_ssm_c_re, m_ssm_c_im, m_ssm_d, m_g_ffn2, m_g_final, m_meta_tokens, m_conv_w]
    small_v = [v_g_ffn1, v_g_mix, v_b_gate, v_ssm_a_re, v_ssm_a_im, v_ssm_log_dt, v_ssm_b_re, v_ssm_b_im,
               v_ssm_c_re, v_ssm_c_im, v_ssm_d, v_g_ffn2, v_g_final, v_meta_tokens, v_conv_w]
    local_small = [dg_ffn1, dg_mix, dbg, *s5_grads, jnp.sum(dd, axis=0), dg_ffn2, dg_final,
                   dh0[:n_meta], dcw[:conv_w.shape[1]]]
    reduced = _unpack(_allreduce_small(_pack(local_small)), local_small)
    reduced[-2] = lax.dynamic_slice_in_dim(reduced[-2], shard * meta_tokens.shape[1], meta_tokens.shape[1], 1)
    reduced[-1] = lax.dynamic_slice_in_dim(reduced[-1], shard * conv_w.shape[3], conv_w.shape[3], 1)
    small_g = [r.reshape(w.shape) for r, w in zip(reduced, small_w)]
    ds_, nm_, nv_ = _adamw_small(_pack(small_w), _pack(small_g), _pack(small_m), _pack(small_v))
    small_out = {nme: o for nme, o in zip(
        small_names, zip(small_g, _unpack(ds_, small_w), _unpack(nm_, small_w), _unpack(nv_, small_w)))}

    loss = lax.psum(loss_part[0, 0], ("x", "y", "c"))
    order = ["meta_tokens", "g_ffn1", "ffn1_w_gate", "ffn1_w_up", "ffn1_w_down", "g_mix", "w_in", "b_gate",
             "ssm_a_re", "ssm_a_im", "ssm_log_dt", "ssm_b_re", "ssm_b_im", "ssm_c_re", "ssm_c_im", "ssm_d",
             "ssm_w_glu", "conv_w", "conv_w_out", "w_o", "g_ffn2", "ffn2_w_gate", "ffn2_w_up", "ffn2_w_down",
             "g_final"]
    res = {**big_out, **small_out}
    return (loss, grad_x, *[res[nme][0] for nme in order], *[res[nme][1] for nme in order],
            *[res[nme][2] for nme in order], *[res[nme][3] for nme in order])
```

```python
import functools
import math

import jax
import jax.numpy as jnp
from jax import lax
from jax.experimental import pallas as pl
from jax.experimental.pallas import tpu as pltpu

F32 = jnp.float32
BF16 = jnp.bfloat16
MESH = pl.DeviceIdType.MESH

RMS_EPS = 1e-6
ADAM_LR = 0.001
ADAM_B1 = 0.9
ADAM_B2 = 0.999
ADAM_EPS = 1e-08
ADAM_WD = 0.01
ADAM_STEP = 10

LANES = 128
SUBLANES = 8
VMEM_LIMIT = 56 * 1024 * 1024

ROW_ALIGN = 256
SCAN_TILE = 256
GROUPS_PER_BLOCK = 8


def _params(sem, vmem=VMEM_LIMIT):
    return pltpu.CompilerParams(dimension_semantics=sem, vmem_limit_bytes=vmem)


def _pick_tile(n, candidates):
    for c in candidates:
        if n % c == 0:
            return c
    raise ValueError(f"no tile for {n}")


def _dot(a, b):
    return jnp.dot(a, b, preferred_element_type=F32)


def _dot_nt(a, b):
    return lax.dot_general(a, b, (((1,), (1,)), ((), ())), preferred_element_type=F32)


def _dot_tn(a, b):
    return lax.dot_general(a, b, (((0,), (0,)), ((), ())), preferred_element_type=F32)


def _sigmoid(x):
    return 1.0 / (1.0 + jnp.exp(-x))


def _rms_stats(h):
    r = lax.rsqrt(jnp.mean(h * h, axis=-1, keepdims=True) + RMS_EPS)
    return h * r, r


def _rms_bwd(xhat, r, g, dn):
    dxh = dn * g
    return r * (dxh - xhat * jnp.mean(dxh * xhat, axis=-1, keepdims=True))


GELU_K = math.sqrt(2.0 / math.pi)
GELU_C = 0.044715


def _gelu(x):
    return 0.5 * x * (1.0 + jnp.tanh(GELU_K * (x + GELU_C * x * x * x)))


def _gelu_grad(x):
    t = jnp.tanh(GELU_K * (x + GELU_C * x * x * x))
    return 0.5 * (1.0 + t) + 0.5 * x * (1.0 - t * t) * GELU_K * (1.0 + 3.0 * GELU_C * x * x)


def _ffn_fwd(h, g, wg, wu, wd, name, final=None):
    tp, d = h.shape
    ns, _, f4 = wg.shape
    tm = _pick_tile(tp, (768, 512, 256))
    ni = tp // tm

    def body(*refs):
        if final is None:
            h_ref, g_ref, wg_ref, wu_ref, wd_ref, ho_ref, a_ref, b_ref, n_scr, acc = refs
        else:
            (h_ref, g_ref, wg_ref, wu_ref, wd_ref, gf_ref, tg_ref,
             ho_ref, a_ref, b_ref, dgf_ref, loss_ref, n_scr, acc) = refs
        i = pl.program_id(0)
        k = pl.program_id(1)

        @pl.when(k == 0)
        def _():
            xhat, _ = _rms_stats(h_ref[...])
            n_scr[...] = (xhat * g_ref[...]).astype(BF16)
            acc[...] = jnp.zeros_like(acc)

        n = n_scr[...]
        a = _dot(n, wg_ref[...])
        b = _dot(n, wu_ref[...])
        a_ref[...] = a.astype(BF16)
        b_ref[...] = b.astype(BF16)
        s = (a * _sigmoid(a) * b).astype(BF16)
        acc[...] += _dot(s, wd_ref[...])

        if final is None:
            @pl.when(k == ns - 1)
            def _():
                ho_ref[...] = h_ref[...] + 0.5 * acc[...]
        else:
            n_meta, seq = final[2], final[3]

            @pl.when((i == 0) & (k == 0))
            def _():
                dgf_ref[...] = jnp.zeros_like(dgf_ref)
                loss_ref[...] = jnp.zeros_like(loss_ref)

            @pl.when(k == ns - 1)
            def _():
                h3 = h_ref[...] + 0.5 * acc[...]
                xhat, r = _rms_stats(h3)
                gf = gf_ref[...]
                row = i * tm + lax.broadcasted_iota(jnp.int32, (tm, d), 0)
                valid = (row >= n_meta) & (row < n_meta + seq)
                diff = jnp.where(valid, xhat * gf - tg_ref[...], 0.0)
                dout = diff * (1.0 / d)
                loss_ref[...] += jnp.full(loss_ref.shape, 0.5 * jnp.sum(diff * diff) * (1.0 / d), F32)
                dgf_ref[...] += jnp.sum(dout * xhat, axis=0, keepdims=True)
                ho_ref[...] = _rms_bwd(xhat, r, gf, dout)

    row_spec = pl.BlockSpec((tm, d), lambda i, k: (i, 0))
    vec_spec = pl.BlockSpec((1, d), lambda i, k: (0, 0))
    in_specs = [row_spec, vec_spec,
                pl.BlockSpec((None, d, f4), lambda i, k: (k, 0, 0)),
                pl.BlockSpec((None, d, f4), lambda i, k: (k, 0, 0)),
                pl.BlockSpec((None, f4, d), lambda i, k: (k, 0, 0))]
    act_spec = pl.BlockSpec((None, tm, f4), lambda i, k: (k, i, 0))
    out_specs = [row_spec, act_spec, act_spec]
    out_shape = [jax.ShapeDtypeStruct((tp, d), F32),
                 jax.ShapeDtypeStruct((ns, tp, f4), BF16),
                 jax.ShapeDtypeStruct((ns, tp, f4), BF16)]
    args = [h, g, wg, wu, wd]
    if final is not None:
        in_specs += [vec_spec, row_spec]
        args += [final[0], final[1]]
        out_specs += [vec_spec, pl.BlockSpec((1, LANES), lambda i, k: (0, 0))]
        out_shape += [jax.ShapeDtypeStruct((1, d), F32), jax.ShapeDtypeStruct((1, LANES), F32)]
    return pl.pallas_call(
        body, name=name, grid=(ni, ns), in_specs=in_specs, out_specs=out_specs, out_shape=out_shape,
        scratch_shapes=[pltpu.VMEM((tm, d), BF16), pltpu.VMEM((tm, d), F32)],
        compiler_params=_params(("arbitrary", "arbitrary")),
    )(*args)


def _ffn_bwd_shard(k, ns, dn_prev, dh_out, h_in, g, a, b, wg, wu, wd, name):
    tp, d = h_in.shape
    f4 = wg.shape[2]
    tm = ROW_ALIGN
    ni = tp // tm
    first, last = k == 0, k == ns - 1

    def body(*refs):
        refs = list(refs)
        acc_in = None if first else refs.pop(0)
        dh_ref, h_ref, g_ref, a_ref, b_ref, wg_ref, wu_ref, wd_ref, acc_out, dwg_ref, dwu_ref, dwd_ref = refs[:12]
        i = pl.program_id(0)

        @pl.when(i == 0)
        def _():
            dwg_ref[...] = jnp.zeros_like(dwg_ref)
            dwu_ref[...] = jnp.zeros_like(dwu_ref)
            dwd_ref[...] = jnp.zeros_like(dwd_ref)
            if last:
                refs[12][...] = jnp.zeros_like(refs[12])

        xhat, r = _rms_stats(h_ref[...])
        gv = g_ref[...]
        n = (xhat * gv).astype(BF16)
        dy = (0.5 * dh_ref[...]).astype(BF16)
        av = a_ref[...].astype(F32)
        bv = b_ref[...].astype(F32)
        sg = _sigmoid(av)
        silu = av * sg
        ds = _dot_nt(dy, wd_ref[...])
        da = (ds * bv * (sg * (1.0 + av * (1.0 - sg)))).astype(BF16)
        db = (ds * silu).astype(BF16)
        s = (silu * bv).astype(BF16)
        dwd_ref[...] += _dot_tn(s, dy)
        dwg_ref[...] += _dot_tn(n, da)
        dwu_ref[...] += _dot_tn(n, db)
        dn = _dot_nt(da, wg_ref[...]) + _dot_nt(db, wu_ref[...])
        if not first:
            dn = dn + acc_in[...]
        if last:
            refs[12][...] += jnp.sum(dn * xhat, axis=0, keepdims=True)
            acc_out[...] = dh_ref[...] + _rms_bwd(xhat, r, gv, dn)
        else:
            acc_out[...] = dn

    row_spec = pl.BlockSpec((tm, d), lambda i: (i, 0))
    vec_spec = pl.BlockSpec((1, d), lambda i: (0, 0))
    act_spec = pl.BlockSpec((None, tm, f4), lambda i: (k, i, 0))
    wcol = pl.BlockSpec((None, d, f4), lambda i: (k, 0, 0))
    wrow = pl.BlockSpec((None, f4, d), lambda i: (k, 0, 0))
    whole = lambda shape: pl.BlockSpec(shape, lambda i: (0, 0))
    in_specs = [row_spec, row_spec, vec_spec, act_spec, act_spec, wcol, wcol, wrow]
    args = [dh_out, h_in, g, a, b, wg, wu, wd]
    if not first:
        in_specs.insert(0, row_spec)
        args.insert(0, dn_prev)
    out_specs = [row_spec, whole((d, f4)), whole((d, f4)), whole((f4, d))]
    out_shape = [jax.ShapeDtypeStruct((tp, d), F32), jax.ShapeDtypeStruct((d, f4), F32),
                 jax.ShapeDtypeStruct((d, f4), F32), jax.ShapeDtypeStruct((f4, d), F32)]
    if last:
        out_specs.append(vec_spec)
        out_shape.append(jax.ShapeDtypeStruct((1, d), F32))
    return pl.pallas_call(
        body, name=f"{name}_{k}", grid=(ni,), in_specs=in_specs, out_specs=out_specs, out_shape=out_shape,
        compiler_params=_params(("arbitrary",)),
    )(*args)


def _ffn_bwd(dh_out, h_in, g, a, b, wg, wu, wd, name):
    ns = wg.shape[0]
    acc, dwg, dwu, dwd, dg = None, [], [], [], None
    for k in range(ns):
        outs = _ffn_bwd_shard(k, ns, acc, dh_out, h_in, g, a, b, wg, wu, wd, name)
        acc = outs[0]
        dwg.append(outs[1])
        dwu.append(outs[2])
        dwd.append(outs[3])
        if k == ns - 1:
            dg = outs[4]
    return acc, jnp.stack(dwg), jnp.stack(dwu), jnp.stack(dwd), dg


def _win_fwd(h, g, w_in):
    tp, d = h.shape
    ns = w_in.shape[0]
    tm = _pick_tile(tp, (768, 512, 256))

    def body(h_ref, g_ref, w_ref, u_ref, p_ref):
        @pl.when(pl.program_id(1) == 0)
        def _():
            xhat, _ = _rms_stats(h_ref[...])
            u_ref[...] = (xhat * g_ref[...]).astype(BF16)

        p_ref[...] = _dot(u_ref[...], w_ref[...]).astype(BF16)

    return pl.pallas_call(
        body, name="win_fwd", grid=(tp // tm, ns),
        in_specs=[pl.BlockSpec((tm, d), lambda i, k: (i, 0)),
                  pl.BlockSpec((1, d), lambda i, k: (0, 0)),
                  pl.BlockSpec((None, d, d), lambda i, k: (k, 0, 0))],
        out_specs=[pl.BlockSpec((tm, d), lambda i, k: (i, 0)),
                   pl.BlockSpec((None, tm, d), lambda i, k: (k, i, 0))],
        out_shape=[jax.ShapeDtypeStruct((tp, d), BF16), jax.ShapeDtypeStruct((ns, tp, d), BF16)],
        compiler_params=_params(("arbitrary", "arbitrary")),
    )(h, g, w_in)


def _win_bwd_shard(k, ns, du_prev, dpb, dug, u, w_in, h1, g, dh2):
    tp, d = h1.shape
    dh = d // 2
    tm = ROW_ALIGN
    first, last = k == 0, k == ns - 1

    def body(*refs):
        refs = list(refs)
        acc_in = None if first else refs.pop(0)
        dug_ref = refs.pop(0) if first else None
        dp_ref, u_ref, w_ref = refs[:3]
        refs = refs[3:]
        if last:
            h_ref, g_ref, dh2_ref, acc_out, dw_ref, dg_ref = refs
        else:
            acc_out, dw_ref = refs
        i = pl.program_id(0)

        @pl.when(i == 0)
        def _():
            dw_ref[...] = jnp.zeros_like(dw_ref)
            if last:
                dg_ref[...] = jnp.zeros_like(dg_ref)

        dp = dp_ref[...]
        if first:
            dp = jnp.concatenate([dug_ref[...], dp[:, dh:]], axis=1)
        dw_ref[...] += _dot_tn(u_ref[...], dp)
        du = _dot_nt(dp, w_ref[...])
        if not first:
            du = du + acc_in[...]
        if last:
            xhat, r = _rms_stats(h_ref[...])
            dg_ref[...] += jnp.sum(du * xhat, axis=0, keepdims=True)
            acc_out[...] = dh2_ref[...] + _rms_bwd(xhat, r, g_ref[...], du)
        else:
            acc_out[...] = du

    row_spec = pl.BlockSpec((tm, d), lambda i: (i, 0))
    vec_spec = pl.BlockSpec((1, d), lambda i: (0, 0))
    in_specs = [pl.BlockSpec((None, tm, d), lambda i: (k, i, 0)), row_spec,
                pl.BlockSpec((None, d, d), lambda i: (k, 0, 0))]
    args = [dpb, u, w_in]
    if first:
        in_specs.insert(0, pl.BlockSpec((tm, dh), lambda i: (i, 0)))
        args.insert(0, dug)
    else:
        in_specs.insert(0, row_spec)
        args.insert(0, du_prev)
    out_specs = [row_spec, pl.BlockSpec((d, d), lambda i: (0, 0))]
    out_shape = [jax.ShapeDtypeStruct((tp, d), F32), jax.ShapeDtypeStruct((d, d), F32)]
    if last:
        in_specs += [row_spec, vec_spec, row_spec]
        args += [h1, g, dh2]
        out_specs.append(vec_spec)
        out_shape.append(jax.ShapeDtypeStruct((1, d), F32))
    return pl.pallas_call(
        body, name=f"win_bwd_{k}", grid=(tp // tm,), in_specs=in_specs, out_specs=out_specs,
        out_shape=out_shape, compiler_params=_params(("arbitrary",)),
    )(*args)


def _win_bwd(dpb, dug, u, w_in, h1, g, dh2):
    ns = w_in.shape[0]
    acc, dws, dg = None, [], None
    for k in range(ns):
        outs = _win_bwd_shard(k, ns, acc, dpb, dug, u, w_in, h1, g, dh2)
        acc = outs[0]
        dws.append(outs[1])
        if k == ns - 1:
            dg = outs[2]
    return acc, jnp.stack(dws), dg


def _cmul(ar, ai, br, bi):
    return ar * br - ai * bi, ar * bi + ai * br


def _scan_rows(j, sub):
    return pl.ds(j * SUBLANES, SUBLANES)


def _permute_rows(src_ref, dst_ref, sub):
    for j in range(sub):
        dst_ref[pl.ds(j * SUBLANES, SUBLANES), :] = src_ref[pl.ds(j, SUBLANES, stride=sub), :]


def _unpermute_rows(src_ref, dst_ref, sub):
    for j in range(sub):
        dst_ref[pl.ds(j, SUBLANES, stride=sub), :] = src_ref[pl.ds(j * SUBLANES, SUBLANES), :]


def _local_scan(x_ref, lr, li, w, sub, reverse):
    hr = jnp.zeros((SUBLANES, w), F32)
    hi = jnp.zeros((SUBLANES, w), F32)
    order = range(sub - 1, -1, -1) if reverse else range(sub)
    for j in order:
        xr = x_ref[_scan_rows(j, sub), pl.ds(0, w)]
        xi = x_ref[_scan_rows(j, sub), pl.ds(w, w)]
        if reverse:
            hr, hi = lr * hr + li * hi + xr, lr * hi - li * hr + xi
        else:
            hr, hi = lr * hr - li * hi + xr, lr * hi + li * hr + xi
        x_ref[_scan_rows(j, sub), pl.ds(0, w)] = hr
        x_ref[_scan_rows(j, sub), pl.ds(w, w)] = hi
    return hr, hi


def _entering_states(er, ei, fr, fi, pow_ref, w, sub, reverse):
    lane = lax.broadcasted_iota(jnp.int32, (SUBLANES, w), 0)
    if reverse:
        edge, shift1 = SUBLANES - 1, SUBLANES - 1
    else:
        edge, shift1 = 0, 1
    zr = jnp.where(lane == edge, pltpu.roll(fr, shift1, 0), pltpu.roll(er, shift1, 0))
    zi = jnp.where(lane == edge, pltpu.roll(fi, shift1, 0), pltpu.roll(ei, shift1, 0))
    for step, row in ((1, sub - 1), (2, sub), (4, sub + 1)):
        ar = pow_ref[pl.ds(row, 1), pl.ds(0, w)]
        ai = pow_ref[pl.ds(row, 1), pl.ds(w, w)]
        if reverse:
            ai = -ai
            keep = lane < SUBLANES - step
            sr = jnp.where(keep, pltpu.roll(zr, SUBLANES - step, 0), 0.0)
            si = jnp.where(keep, pltpu.roll(zi, SUBLANES - step, 0), 0.0)
        else:
            keep = lane >= step
            sr = jnp.where(keep, pltpu.roll(zr, step, 0), 0.0)
            si = jnp.where(keep, pltpu.roll(zi, step, 0), 0.0)
        pr, pi = _cmul(ar, ai, sr, si)
        zr, zi = zr + pr, zi + pi
    ar = pow_ref[pl.ds(sub - 1, 1), pl.ds(0, w)]
    ai = pow_ref[pl.ds(sub - 1, 1), pl.ds(w, w)]
    if reverse:
        ai = -ai
    pr, pi = _cmul(ar, ai, zr, zi)
    return zr, zi, er + pr, ei + pi


def _scan_fwd(p, mb, mc, powt, dskip):
    _, tp, d = p.shape
    nb, cb, w2 = mb.shape
    w = w2 // 2
    q = SCAN_TILE
    sub = q // SUBLANES
    nt = tp // q
    ds = d // 2

    def body(ug_ref, mb_ref, mc_ref, pow_ref, d_ref, y_ref, bnd_ref, x_scr, carry, nat, perm):
        t = pl.program_id(1)

        @pl.when(t == 0)
        def _():
            carry[...] = jnp.zeros_like(carry)

        ugf = ug_ref[...].astype(F32)
        nat[...] = ugf
        _permute_rows(nat, perm, sub)
        x_scr[...] = _dot(perm[...].astype(BF16), mb_ref[...])
        lr = jnp.broadcast_to(pow_ref[pl.ds(0, 1), pl.ds(0, w)], (SUBLANES, w))
        li = jnp.broadcast_to(pow_ref[pl.ds(0, 1), pl.ds(w, w)], (SUBLANES, w))
        er, ei = _local_scan(x_scr, lr, li, w, sub, False)
        zr, zi, fr, fi = _entering_states(er, ei, carry[:, pl.ds(0, w)], carry[:, pl.ds(w, w)],
                                          pow_ref, w, sub, False)
        carry[:, pl.ds(0, w)] = fr
        carry[:, pl.ds(w, w)] = fi
        bnd_ref[:, pl.ds(0, w)] = fr
        bnd_ref[:, pl.ds(w, w)] = fi
        for j in range(sub):
            pr = pow_ref[pl.ds(j, 1), pl.ds(0, w)]
            pi = pow_ref[pl.ds(j, 1), pl.ds(w, w)]
            cr, ci = _cmul(pr, pi, zr, zi)
            x_scr[_scan_rows(j, sub), pl.ds(0, w)] += cr
            x_scr[_scan_rows(j, sub), pl.ds(w, w)] += ci
        hb = x_scr[...].astype(BF16)
        perm[...] = _dot_nt(hb, mc_ref[...])
        _unpermute_rows(perm, nat, sub)
        y_ref[...] = nat[...] + d_ref[...] * ugf

    return pl.pallas_call(
        body, name="s5_scan_fwd", grid=(nb, nt),
        in_specs=[pl.BlockSpec((None, q, cb), lambda b, t: (0, t, b)),
                  pl.BlockSpec((None, cb, w2), lambda b, t: (b, 0, 0)),
                  pl.BlockSpec((None, cb, w2), lambda b, t: (b, 0, 0)),
                  pl.BlockSpec((None, powt.shape[1], w2), lambda b, t: (b, 0, 0)),
                  pl.BlockSpec((1, cb), lambda b, t: (0, b))],
        out_specs=[pl.BlockSpec((q, cb), lambda b, t: (t, b)),
                   pl.BlockSpec((None, None, SUBLANES, w2), lambda b, t: (b, t, 0, 0))],
        out_shape=[jax.ShapeDtypeStruct((tp, ds), F32),
                   jax.ShapeDtypeStruct((nb, nt, SUBLANES, w2), F32)],
        scratch_shapes=[pltpu.VMEM((q, w2), F32), pltpu.VMEM((SUBLANES, w2), F32),
                        pltpu.VMEM((q, cb), F32), pltpu.VMEM((q, cb), F32)],
        compiler_params=_params(("arbitrary", "arbitrary")),
    )(p, mb, mc, powt, dskip)


def _scan_bwd(p, dy, mb, mc, powt, dskip, bnd):
    _, tp, d = p.shape
    nb, cb, w2 = mb.shape
    w = w2 // 2
    q = SCAN_TILE
    sub = q // SUBLANES
    nt = tp // q
    ds = d // 2

    def body(ug_ref, dy_ref, mb_ref, mc_ref, pow_ref, d_ref, bnd_ref,
             dug_ref, dmb_ref, dmc_ref, dlam_ref, dd_ref, x_scr, y_scr, gcarry, nat, perm):
        t = pl.program_id(1)
        tt = nt - 1 - t

        @pl.when(t == 0)
        def _():
            gcarry[...] = jnp.zeros_like(gcarry)
            dmb_ref[...] = jnp.zeros_like(dmb_ref)
            dmc_ref[...] = jnp.zeros_like(dmc_ref)
            dlam_ref[...] = jnp.zeros_like(dlam_ref)
            dd_ref[...] = jnp.zeros_like(dd_ref)

        ugf = ug_ref[...].astype(F32)
        dyf = dy_ref[...].astype(F32)
        dd_ref[...] += jnp.sum((dyf * ugf).reshape(q // SUBLANES, SUBLANES, cb), axis=0)
        nat[...] = ugf
        _permute_rows(nat, perm, sub)
        ug = perm[...].astype(BF16)
        nat[...] = dyf
        _permute_rows(nat, perm, sub)
        dyb = perm[...].astype(BF16)
        lr = jnp.broadcast_to(pow_ref[pl.ds(0, 1), pl.ds(0, w)], (SUBLANES, w))
        li = jnp.broadcast_to(pow_ref[pl.ds(0, 1), pl.ds(w, w)], (SUBLANES, w))

        x_scr[...] = _dot(ug, mb_ref[...])
        er, ei = _local_scan(x_scr, lr, li, w, sub, False)
        first = tt == 0
        pfr = jnp.where(first, 0.0, bnd_ref[:, pl.ds(0, w)])
        pfi = jnp.where(first, 0.0, bnd_ref[:, pl.ds(w, w)])
        hzr, hzi, _, _ = _entering_states(er, ei, pfr, pfi, pow_ref, w, sub, False)
        for j in range(sub):
            pr = pow_ref[pl.ds(j, 1), pl.ds(0, w)]
            pi = pow_ref[pl.ds(j, 1), pl.ds(w, w)]
            cr, ci = _cmul(pr, pi, hzr, hzi)
            x_scr[_scan_rows(j, sub), pl.ds(0, w)] += cr
            x_scr[_scan_rows(j, sub), pl.ds(w, w)] += ci

        y_scr[...] = _dot(dyb, mc_ref[...])
        er, ei = _local_scan(y_scr, lr, li, w, sub, True)
        gzr, gzi, fr, fi = _entering_states(er, ei, gcarry[:, pl.ds(0, w)], gcarry[:, pl.ds(w, w)],
                                            pow_ref, w, sub, True)
        gcarry[:, pl.ds(0, w)] = fr
        gcarry[:, pl.ds(w, w)] = fi
        accr = jnp.zeros((SUBLANES, w), F32)
        acci = jnp.zeros((SUBLANES, w), F32)
        for j in range(sub):
            pr = pow_ref[pl.ds(sub - 1 - j, 1), pl.ds(0, w)]
            pi = pow_ref[pl.ds(sub - 1 - j, 1), pl.ds(w, w)]
            cr, ci = _cmul(pr, -pi, gzr, gzi)
            gr = y_scr[_scan_rows(j, sub), pl.ds(0, w)] + cr
            gi = y_scr[_scan_rows(j, sub), pl.ds(w, w)] + ci
            y_scr[_scan_rows(j, sub), pl.ds(0, w)] = gr
            y_scr[_scan_rows(j, sub), pl.ds(w, w)] = gi
            if j == 0:
                hpr, hpi = hzr, hzi
            else:
                hpr = x_scr[_scan_rows(j - 1, sub), pl.ds(0, w)]
                hpi = x_scr[_scan_rows(j - 1, sub), pl.ds(w, w)]
            accr += hpr * gr + hpi * gi
            acci += hpr * gi - hpi * gr
        dlam_ref[:, pl.ds(0, w)] += accr
        dlam_ref[:, pl.ds(w, w)] += acci

        hb = x_scr[...].astype(BF16)
        gb = y_scr[...].astype(BF16)
        dmc_ref[...] += _dot_tn(dyb, hb)
        dmb_ref[...] += _dot_tn(ug, gb)
        perm[...] = _dot_nt(gb, mb_ref[...])
        _unpermute_rows(perm, nat, sub)
        dug_ref[...] = (nat[...] + d_ref[...] * dyf).astype(BF16)

    blk = lambda b, t: (b, 0, 0)
    return pl.pallas_call(
        body, name="s5_scan_bwd", grid=(nb, nt),
        in_specs=[pl.BlockSpec((None, q, cb), lambda b, t: (0, nt - 1 - t, b)),
                  pl.BlockSpec((q, cb), lambda b, t: (nt - 1 - t, b)),
                  pl.BlockSpec((None, cb, w2), blk),
                  pl.BlockSpec((None, cb, w2), blk),
                  pl.BlockSpec((None, powt.shape[1], w2), blk),
                  pl.BlockSpec((1, cb), lambda b, t: (0, b)),
                  pl.BlockSpec((None, None, SUBLANES, w2),
                               lambda b, t: (b, jnp.maximum(nt - 2 - t, 0), 0, 0))],
        out_specs=[pl.BlockSpec((q, cb), lambda b, t: (nt - 1 - t, b)),
                   pl.BlockSpec((None, cb, w2), blk),
                   pl.BlockSpec((None, cb, w2), blk),
                   pl.BlockSpec((None, SUBLANES, w2), blk),
                   pl.BlockSpec((SUBLANES, cb), lambda b, t: (0, b))],
        out_shape=[jax.ShapeDtypeStruct((tp, ds), BF16),
                   jax.ShapeDtypeStruct((nb, cb, w2), F32),
                   jax.ShapeDtypeStruct((nb, cb, w2), F32),
                   jax.ShapeDtypeStruct((nb, SUBLANES, w2), F32),
                   jax.ShapeDtypeStruct((SUBLANES, ds), F32)],
        scratch_shapes=[pltpu.VMEM((q, w2), F32), pltpu.VMEM((q, w2), F32),
                        pltpu.VMEM((SUBLANES, w2), F32), pltpu.VMEM((q, cb), F32), pltpu.VMEM((q, cb), F32)],
        compiler_params=_params(("arbitrary", "arbitrary")),
    )(p, dy, mb, mc, powt, dskip, bnd)


HALO = 16


def _mix_tile(ys5, p0, p1, p2, p3, prev_cin, cw, bgate, wglu, wco, d):
    dh = d // 2
    tm = ys5.shape[0]
    v = p0[:, dh:].astype(F32)
    gbr = p1[:, :dh].astype(F32)
    gcr = p1[:, dh:].astype(F32)
    gact = _gelu(ys5).astype(BF16)
    z = _dot(gact, wglu)
    z1, z2 = z[:, :d], z[:, d:]
    sg = _sigmoid(z2)
    y_ssm = z1 * sg
    cin = gcr * v
    ext = jnp.concatenate([cin, prev_cin], axis=0)
    r1 = pltpu.roll(ext, 1, 0)[:tm]
    r2 = pltpu.roll(ext, 2, 0)[:tm]
    cv = cw[2] * cin + cw[1] * r1 + cw[0] * r2
    cg = (gbr * cv).astype(BF16)
    y_conv = _dot(cg, wco)
    g_s = _sigmoid(p2.astype(F32) + bgate[:, :d])
    g_c = _sigmoid(p3.astype(F32) + bgate[:, d:])
    mixed = g_s * y_ssm + g_c * y_conv
    return dict(v=v, gb=gbr, gc=gcr, gact=gact, z1=z1, sg=sg, y_ssm=y_ssm, cin=cin, r1=r1, r2=r2,
                cv=cv, cg=cg, y_conv=y_conv, g_s=g_s, g_c=g_c, mixed=mixed)


def _mix_fwd(h1, ys5, p, cw, bgate, wglu, wco, wo):
    tp, d = h1.shape
    dh = d // 2
    tm = ROW_ALIGN

    def body(h_ref, y_ref, p0_ref, p1_ref, p2_ref, p3_ref, cw_ref, bg_ref, wglu_ref, wco_ref, wo_ref,
             o_ref, prev):
        @pl.when(pl.program_id(0) == 0)
        def _():
            prev[...] = jnp.zeros_like(prev)

        cw = [cw_ref[pl.ds(t, 1), :] for t in range(3)]
        f = _mix_tile(y_ref[...], p0_ref[...], p1_ref[...], p2_ref[...], p3_ref[...], prev[...],
                      cw, bg_ref[...], wglu_ref[...], wco_ref[...], d)
        prev[...] = f["cin"][tm - HALO:, :]
        o_ref[...] = h_ref[...] + _dot(f["mixed"].astype(BF16), wo_ref[...])

    row = pl.BlockSpec((tm, d), lambda i: (i, 0))
    full = lambda a: pl.BlockSpec(a.shape, lambda i: (0,) * a.ndim)
    pk = lambda k: pl.BlockSpec((None, tm, d), lambda i, k=k: (k, i, 0))
    return pl.pallas_call(
        body, name="mix_fwd", grid=(tp // tm,),
        in_specs=[row, pl.BlockSpec((tm, dh), lambda i: (i, 0)), pk(0), pk(1), pk(2), pk(3),
                  full(cw), full(bgate), full(wglu), full(wco), full(wo)],
        out_specs=row, out_shape=jax.ShapeDtypeStruct((tp, d), F32),
        scratch_shapes=[pltpu.VMEM((HALO, dh), F32)],
        compiler_params=_params(("arbitrary",)),
    )(h1, ys5, p, p, p, p, cw, bgate, wglu, wco, wo)


def _mix_bwd(dh2, ys5, p, cw, bgate, wglu, wco, wo):
    tp, d = dh2.shape
    dh = d // 2
    tm = ROW_ALIGN
    ni = tp // tm
    hb = tm // HALO

    def body(dh_ref, y_ref, p0_ref, p1_ref, p2_ref, p3_ref, h0_ref, h1_ref,
             cw_ref, bg_ref, wglu_ref, wco_ref, wo_ref,
             dys_ref, dpb_ref, dwo_ref, dwglu_ref, dwco_ref, dcw_ref, dbg_ref, nxt):
        i = pl.program_id(0)
        tt = ni - 1 - i

        @pl.when(i == 0)
        def _():
            nxt[...] = jnp.zeros_like(nxt)
            dwo_ref[...] = jnp.zeros_like(dwo_ref)
            dwglu_ref[...] = jnp.zeros_like(dwglu_ref)
            dwco_ref[...] = jnp.zeros_like(dwco_ref)
            dcw_ref[...] = jnp.zeros_like(dcw_ref)
            dbg_ref[...] = jnp.zeros_like(dbg_ref)

        cw = [cw_ref[pl.ds(t, 1), :] for t in range(3)]
        prev_cin = h1_ref[:, dh:].astype(F32) * h0_ref[:, dh:].astype(F32)
        prev_cin = jnp.where(tt == 0, 0.0, prev_cin)
        ys5 = y_ref[...]
        f = _mix_tile(ys5, p0_ref[...], p1_ref[...], p2_ref[...], p3_ref[...], prev_cin,
                      cw, bg_ref[...], wglu_ref[...], wco_ref[...], d)
        dhb = dh_ref[...].astype(BF16)
        dmixed = _dot_nt(dhb, wo_ref[...])
        dwo_ref[...] += _dot_tn(f["mixed"].astype(BF16), dhb)

        g_s, g_c, sg = f["g_s"], f["g_c"], f["sg"]
        dy_ssm = dmixed * g_s
        dy_conv = dmixed * g_c
        dp2 = dmixed * f["y_ssm"] * g_s * (1.0 - g_s)
        dp3 = dmixed * f["y_conv"] * g_c * (1.0 - g_c)
        dbg_ref[:, pl.ds(0, d)] += jnp.sum(dp2, axis=0, keepdims=True)
        dbg_ref[:, pl.ds(d, d)] += jnp.sum(dp3, axis=0, keepdims=True)

        dz = jnp.concatenate([dy_ssm * sg, dy_ssm * f["z1"] * sg * (1.0 - sg)], axis=1).astype(BF16)
        dwglu_ref[...] += _dot_tn(f["gact"], dz)
        dys_ref[...] = (_dot_nt(dz, wglu_ref[...]) * _gelu_grad(ys5)).astype(BF16)

        dycb = dy_conv.astype(BF16)
        dwco_ref[...] += _dot_tn(f["cg"], dycb)
        dcg = _dot_nt(dycb, wco_ref[...])
        dgb = dcg * f["cv"]
        dcv = dcg * f["gb"]
        ext = jnp.concatenate([dcv, nxt[...]], axis=0)
        n1 = pltpu.roll(ext, tm + HALO - 1, 0)[:tm]
        n2 = pltpu.roll(ext, tm + HALO - 2, 0)[:tm]
        nxt[...] = dcv[:HALO, :]
        dcin = cw[2] * dcv + cw[1] * n1 + cw[0] * n2
        dcw_ref[pl.ds(0, 1), :] += jnp.sum(dcv * f["r2"], axis=0, keepdims=True)
        dcw_ref[pl.ds(1, 1), :] += jnp.sum(dcv * f["r1"], axis=0, keepdims=True)
        dcw_ref[pl.ds(2, 1), :] += jnp.sum(dcv * f["cin"], axis=0, keepdims=True)
        dgc = dcin * f["v"]
        dv = dcin * f["gc"]
        dpb_ref[0] = jnp.concatenate([jnp.zeros_like(dv), dv], axis=1).astype(BF16)
        dpb_ref[1] = jnp.concatenate([dgb, dgc], axis=1).astype(BF16)
        dpb_ref[2] = dp2.astype(BF16)
        dpb_ref[3] = dp3.astype(BF16)

    rev = lambda i: ni - 1 - i
    row = pl.BlockSpec((tm, d), lambda i: (rev(i), 0))
    half = pl.BlockSpec((tm, dh), lambda i: (rev(i), 0))
    full = lambda a: pl.BlockSpec(a.shape, lambda i: (0,) * a.ndim)
    pk = lambda k: pl.BlockSpec((None, tm, d), lambda i, k=k: (k, rev(i), 0))
    halo = lambda k: pl.BlockSpec((None, HALO, d), lambda i, k=k: (k, jnp.maximum(rev(i) * hb - 1, 0), 0))
    acc = lambda shape: pl.BlockSpec(shape, lambda i: (0,) * len(shape))
    return pl.pallas_call(
        body, name="mix_bwd", grid=(ni,),
        in_specs=[row, half, pk(0), pk(1), pk(2), pk(3), halo(0), halo(1),
                  full(cw), full(bgate), full(wglu), full(wco), full(wo)],
        out_specs=[half, pl.BlockSpec((4, tm, d), lambda i: (0, rev(i), 0)),
                   acc((d, d)), acc((dh, 2 * d)), acc((dh, d)), acc((SUBLANES, dh)), acc((1, 2 * d))],
        out_shape=[jax.ShapeDtypeStruct((tp, dh), BF16), jax.ShapeDtypeStruct((4, tp, d), BF16),
                   jax.ShapeDtypeStruct((d, d), F32), jax.ShapeDtypeStruct((dh, 2 * d), F32),
                   jax.ShapeDtypeStruct((dh, d), F32), jax.ShapeDtypeStruct((SUBLANES, dh), F32),
                   jax.ShapeDtypeStruct((1, 2 * d), F32)],
        scratch_shapes=[pltpu.VMEM((HALO, dh), F32)],
        compiler_params=_params(("arbitrary",)),
    )(dh2, ys5, p, p, p, p, p, p, cw, bgate, wglu, wco, wo)


ANY = pl.BlockSpec(memory_space=pl.ANY)


def _position():
    return lax.axis_index("x"), lax.axis_index("y"), lax.axis_index("c")


def _remote(src, dst, ssem, rsem, dev):
    return pltpu.make_async_remote_copy(src_ref=src, dst_ref=dst, send_sem=ssem, recv_sem=rsem,
                                        device_id=dev, device_id_type=MESH)


def _cast_piece(w, pos):
    rows, cols = w.shape
    r2 = rows // 2

    def body(pos_ref, w_ref, o_ref):
        o_ref[...] = w_ref[...].astype(BF16)

    return pl.pallas_call(
        body, name="cast_piece",
        grid_spec=pltpu.PrefetchScalarGridSpec(
            num_scalar_prefetch=1, grid=(1,),
            in_specs=[pl.BlockSpec((r2, cols), lambda i, pos: (pos[2], 0))],
            out_specs=pl.BlockSpec((r2, cols), lambda i, pos: (0, 0))),
        out_shape=jax.ShapeDtypeStruct((r2, cols), BF16),
        compiler_params=_params(("arbitrary",)),
    )(pos, w)


def _allgather(pieces, smalls):
    n, ns = len(pieces), len(smalls)
    per = 10
    n_big = per * n

    def body(*refs):
        pin, sin = refs[:n], refs[n:n + ns]
        wall, sall = refs[n + ns:2 * n + ns], refs[2 * n + ns:2 * n + 2 * ns]
        ssem, rsem, lsem = refs[2 * n + 2 * ns:]
        x, y, c = _position()
        r4 = [p.shape[0] // 2 for p in pin]
        half = lambda i, h: pin[i].at[pl.ds(h * r4[i], r4[i]), :]
        local = [pltpu.make_async_copy(half(i, h), wall[i].at[x, y, c, h], lsem.at[2 * i + h])
                 for i in range(n) for h in range(2)]
        local += [pltpu.make_async_copy(sin[i], sall[i].at[2 * x + y], lsem.at[2 * n + i]) for i in range(ns)]
        for cp in local:
            cp.start()
        chips = [(1 - x, y), (x, 1 - y), (1 - x, 1 - y)]
        small = []
        for j, (px, py) in enumerate(chips):
            for i in range(ns):
                s = n_big + j * ns + i
                small.append(_remote(sin[i], sall[i].at[2 * x + y], ssem.at[s], rsem.at[s], (px, py, c)))
        for cp in small:
            cp.start()
        sends = []
        xnb, ynb = (1 - x, y, c), (x, 1 - y, c)
        stage = []
        for i in range(n):
            s = per * i
            stage.append(_remote(half(i, 0), wall[i].at[x, y, c, 0], ssem.at[s], rsem.at[s], xnb))
            stage.append(_remote(half(i, 1), wall[i].at[x, y, c, 1], ssem.at[s + 1], rsem.at[s + 1], ynb))
        for cp in stage:
            cp.start()
        for i in range(n):
            s = per * i
            blk = wall[i].at[1 - x, y, c, 0]
            _remote(blk, blk, ssem.at[s], rsem.at[s], xnb).wait_recv()
            blk = wall[i].at[x, 1 - y, c, 1]
            _remote(blk, blk, ssem.at[s + 1], rsem.at[s + 1], ynb).wait_recv()
        sends += stage
        for cp in local:
            cp.wait()
        stage = []
        for i in range(n):
            for j in range(2):
                s = per * i + 2 + j
                blk = wall[i].at[j, y, c, 0]
                stage.append(_remote(blk, blk, ssem.at[s], rsem.at[s], ynb))
                s = per * i + 4 + j
                blk = wall[i].at[x, j, c, 1]
                stage.append(_remote(blk, blk, ssem.at[s], rsem.at[s], xnb))
        for cp in stage:
            cp.start()
        for i in range(n):
            for j in range(2):
                s = per * i + 2 + j
                blk = wall[i].at[j, 1 - y, c, 0]
                _remote(blk, blk, ssem.at[s], rsem.at[s], ynb).wait_recv()
                s = per * i + 4 + j
                blk = wall[i].at[1 - x, j, c, 1]
                _remote(blk, blk, ssem.at[s], rsem.at[s], xnb).wait_recv()
        sends += stage
        stage = []
        for i in range(n):
            for xy in range(4):
                s = per * i + 6 + xy
                blk = wall[i].at[xy // 2, xy % 2, c]
                stage.append(_remote(blk, blk, ssem.at[s], rsem.at[s], (x, y, 1 - c)))
        for cp in stage:
            cp.start()
        for i in range(n):
            for xy in range(4):
                s = per * i + 6 + xy
                blk = wall[i].at[xy // 2, xy % 2, 1 - c]
                _remote(blk, blk, ssem.at[s], rsem.at[s], (x, y, 1 - c)).wait_recv()
        sends += stage
        for j, (px, py) in enumerate(chips):
            for i in range(ns):
                s = n_big + j * ns + i
                _remote(sin[i], sall[i].at[2 * px + py], ssem.at[s], rsem.at[s], (px, py, c)).wait_recv()
        for cp in sends + small:
            cp.wait_send()

    nsem = n_big + 3 * ns
    out_shape = [jax.ShapeDtypeStruct((2, 2, 2, 2, a.shape[0] // 2, a.shape[1]), a.dtype) for a in pieces]
    out_shape += [jax.ShapeDtypeStruct((4,) + a.shape, a.dtype) for a in smalls]
    outs = pl.pallas_call(
        body, name="allgather_weights",
        in_specs=[ANY] * (n + ns), out_specs=[ANY] * (n + ns), out_shape=out_shape,
        scratch_shapes=[pltpu.SemaphoreType.DMA((nsem,)), pltpu.SemaphoreType.DMA((nsem,)),
                        pltpu.SemaphoreType.DMA((2 * n + ns,))],
    )(*pieces, *smalls)
    return outs[:n], outs[n:]


def _exchange(name, arrays, out_shapes, plan):
    na = len(arrays)
    no = len(out_shapes)
    probe = plan([None] * na, [None] * no, None)

    def body(*refs):
        ins, outs = refs[:na], refs[na:na + no]
        ssem, rsem = refs[na + no:]
        copies = [_remote(src, dst, ssem.at[j], rsem.at[j], peer)
                  for j, (src, dst, peer) in enumerate(plan(ins, outs, _position()))]
        for cp in copies:
            cp.start()
        for cp in copies:
            cp.wait()

    return pl.pallas_call(
        body, name=name, in_specs=[ANY] * na, out_specs=[ANY] * no, out_shape=out_shapes,
        scratch_shapes=[pltpu.SemaphoreType.DMA((probe,)), pltpu.SemaphoreType.DMA((probe,))],
    )(*arrays)


class _Grad:
    def __init__(self, arr, kind, shard_shape):
        self.arr, self.kind = arr, kind
        self.rows, self.cols = shard_shape
        self.r2 = self.rows // 2

    def view(self, ref, k, h):
        r2 = self.r2
        if self.kind == "stacked":
            return ref.at[k, pl.ds(h * r2, r2), :]
        if self.kind == "col":
            return ref.at[pl.ds(h * r2, r2), pl.ds(k * self.cols, self.cols)]
        return ref.at[pl.ds((2 * k + h) * r2, r2), :]

    def spec(self, kh):
        r2, cols = self.r2, self.cols
        if self.kind == "stacked":
            return pl.BlockSpec((None, r2, cols), lambda i, pos: (kh(i, pos)[0], kh(i, pos)[1], 0))
        if self.kind == "col":
            return pl.BlockSpec((r2, cols), lambda i, pos: (kh(i, pos)[1], kh(i, pos)[0]))
        return pl.BlockSpec((r2, cols), lambda i, pos: (2 * kh(i, pos)[0] + kh(i, pos)[1], 0))


def _add_stage(name, mine, mine_spec, recv, n_out, pos):
    r2, cols = recv.shape[-2:]

    def body(pos_ref, m_ref, r_ref, of_ref, ob_ref):
        s = m_ref[...] + r_ref[...].astype(F32)
        of_ref[...] = s
        ob_ref[...] = s.astype(BF16)

    blk = pl.BlockSpec((None, r2, cols), lambda i, pos: (i, 0, 0))
    return pl.pallas_call(
        body, name=name,
        grid_spec=pltpu.PrefetchScalarGridSpec(
            num_scalar_prefetch=1, grid=(n_out,), in_specs=[mine_spec, blk], out_specs=[blk, blk]),
        out_shape=[jax.ShapeDtypeStruct((n_out, r2, cols), F32), jax.ShapeDtypeStruct((n_out, r2, cols), BF16)],
        compiler_params=_params(("arbitrary",)),
    )(pos, mine, recv)


def _adamw_math(w, g, m, v):
    m = ADAM_B1 * m + (1.0 - ADAM_B1) * g
    v = ADAM_B2 * v + (1.0 - ADAM_B2) * (g * g)
    m_hat = m / (1.0 - ADAM_B1 ** ADAM_STEP)
    v_hat = v / (1.0 - ADAM_B2 ** ADAM_STEP)
    delta = -ADAM_LR * (m_hat / (jnp.sqrt(v_hat) + ADAM_EPS) + ADAM_WD * w)
    return delta, m, v


def _adamw_big(w, m, v, own, sib, pos):
    rows, cols = w.shape
    r2 = rows // 2

    def body(pos_ref, w_ref, m_ref, v_ref, own_ref, sib_ref, g_ref, d_ref, nm_ref, nv_ref):
        h = pl.program_id(0)
        g = jnp.where(h == pos_ref[2], own_ref[...], sib_ref[...])
        g_ref[...] = g
        d_ref[...], nm_ref[...], nv_ref[...] = _adamw_math(w_ref[...], g, m_ref[...], v_ref[...])

    half = pl.BlockSpec((r2, cols), lambda h, pos: (h, 0))
    whole = pl.BlockSpec((r2, cols), lambda h, pos: (0, 0))
    out = jax.ShapeDtypeStruct((rows, cols), F32)
    return pl.pallas_call(
        body, name="adamw",
        grid_spec=pltpu.PrefetchScalarGridSpec(
            num_scalar_prefetch=1, grid=(2,),
            in_specs=[half, half, half, whole, whole],
            out_specs=[half, half, half, half]),
        out_shape=[out, out, out, out],
        compiler_params=_params(("arbitrary",)),
    )(pos, w, m, v, own, sib)


def _add_hop1(s1f, recv, pos):
    _, _, r4, cols = recv.shape
    s1v = s1f.reshape(4, 2, r4, cols)

    def body(pos_ref, m_ref, r_ref, of_ref, ob_ref):
        s = m_ref[...] + r_ref[...].astype(F32)
        of_ref[...] = s
        ob_ref[...] = s.astype(BF16)

    def mine(h, j, pos):
        return (jnp.where(h == 0, 2 * j + pos[1], 2 * pos[0] + j), h, 0, 0)

    blk = pl.BlockSpec((None, None, r4, cols), lambda h, j, pos: (h, j, 0, 0))
    return pl.pallas_call(
        body, name="rs_add_1",
        grid_spec=pltpu.PrefetchScalarGridSpec(
            num_scalar_prefetch=1, grid=(2, 2),
            in_specs=[pl.BlockSpec((None, None, r4, cols), mine), blk], out_specs=[blk, blk]),
        out_shape=[jax.ShapeDtypeStruct((2, 2, r4, cols), F32), jax.ShapeDtypeStruct((2, 2, r4, cols), BF16)],
        compiler_params=_params(("arbitrary", "arbitrary")),
    )(pos, s1v, recv)


def _own_sum(s2f, recv3, pos):
    _, _, r4, cols = s2f.shape

    def body(pos_ref, s_ref, r_ref, o_ref):
        o_ref[...] = s_ref[...] + r_ref[...].astype(F32)

    blk = pl.BlockSpec((None, r4, cols), lambda h, pos: (h, 0, 0))
    return pl.pallas_call(
        body, name="own_sum",
        grid_spec=pltpu.PrefetchScalarGridSpec(
            num_scalar_prefetch=1, grid=(2,),
            in_specs=[pl.BlockSpec((None, None, r4, cols),
                                   lambda h, pos: (h, jnp.where(h == 0, pos[0], pos[1]), 0, 0)), blk],
            out_specs=blk),
        out_shape=jax.ShapeDtypeStruct((2, r4, cols), F32),
        compiler_params=_params(("arbitrary",)),
    )(pos, s2f, recv3)


def _allreduce_small(buf):
    def body(x_ref, o_ref, recv, ssem, rsem):
        x, y, c = _position()
        o_ref[...] = x_ref[...]
        for s, peer in enumerate([(x, y, 1 - c), (x, 1 - y, c), (1 - x, y, c)]):
            cp = _remote(o_ref, recv.at[s], ssem.at[s], rsem.at[s], peer)
            cp.start()
            cp.wait()
            o_ref[...] = o_ref[...] + recv[s]

    vm = pl.BlockSpec(memory_space=pltpu.VMEM)
    return pl.pallas_call(
        body, name="allreduce_small", in_specs=[vm], out_specs=vm,
        out_shape=jax.ShapeDtypeStruct(buf.shape, F32),
        scratch_shapes=[pltpu.VMEM((3,) + buf.shape, F32),
                        pltpu.SemaphoreType.DMA((3,)), pltpu.SemaphoreType.DMA((3,))],
    )(buf)


def _adamw_small(w, g, m, v):
    def body(w_ref, g_ref, m_ref, v_ref, d_ref, nm_ref, nv_ref):
        d_ref[...], nm_ref[...], nv_ref[...] = _adamw_math(w_ref[...], g_ref[...], m_ref[...], v_ref[...])

    vm = pl.BlockSpec(memory_space=pltpu.VMEM)
    out = jax.ShapeDtypeStruct(w.shape, F32)
    return pl.pallas_call(body, name="adamw_small", in_specs=[vm] * 4, out_specs=[vm] * 3,
                          out_shape=[out, out, out])(w, g, m, v)


def _reduce_scatter_adamw(grads, weights, pos):
    n = len(grads)
    arrs = [g.arr for g in grads]

    def plan1(ins, outs, p):
        if p is None:
            return 4 * n
        x, y, c = p
        return [(grads[i].view(ins[i], k, 1 - c), outs[i].at[k], (x, y, 1 - c))
                for i in range(n) for k in range(4)]

    recv1 = _exchange("rs_exchange_c", arrs, [jax.ShapeDtypeStruct((4, g.r2, g.cols), F32) for g in grads], plan1)
    s1 = [_add_stage("rs_add_c", g.arr, g.spec(lambda i, p: (i, p[2])), r, 4, pos)
          for g, r in zip(grads, recv1)]

    def plan2(ins, outs, p):
        if p is None:
            return 4 * n
        x, y, c = p
        copies = []
        for i in range(n):
            r4 = grads[i].r2 // 2
            for j in range(2):
                copies.append((ins[i].at[2 * j + (1 - y), pl.ds(0, r4), :], outs[i].at[0, j], (x, 1 - y, c)))
                copies.append((ins[i].at[2 * (1 - x) + j, pl.ds(r4, r4), :], outs[i].at[1, j], (1 - x, y, c)))
        return copies

    recv2 = _exchange("rs_exchange_1", [s[1] for s in s1],
                      [jax.ShapeDtypeStruct((2, 2, g.r2 // 2, g.cols), BF16) for g in grads], plan2)
    s2 = [_add_hop1(s[0], r, pos) for s, r in zip(s1, recv2)]

    def plan3(ins, outs, p):
        if p is None:
            return 2 * n
        x, y, c = p
        copies = []
        for i in range(n):
            copies.append((ins[i].at[0, 1 - x], outs[i].at[0], (1 - x, y, c)))
            copies.append((ins[i].at[1, 1 - y], outs[i].at[1], (x, 1 - y, c)))
        return copies

    recv3 = _exchange("rs_exchange_2", [s[1] for s in s2],
                      [jax.ShapeDtypeStruct((2, g.r2 // 2, g.cols), BF16) for g in grads], plan3)
    own = [_own_sum(s[0], r, pos).reshape(g.r2, g.cols) for g, s, r in zip(grads, s2, recv3)]

    def plan4(ins, outs, p):
        if p is None:
            return n
        x, y, c = p
        return [(ins[i], outs[i], (x, y, 1 - c)) for i in range(n)]

    sib = _exchange("rs_exchange_sibling", own, [jax.ShapeDtypeStruct((g.r2, g.cols), F32) for g in grads], plan4)
    return [_adamw_big(w, m, v, o, sb, pos) for (w, m, v), o, sb in zip(weights, own, sib)]


def _block_diag(t, nb):
    g, c, p = t.shape
    gb = g // nb
    t = t.reshape(nb, gb, c, p)
    eye = jnp.eye(gb, dtype=t.dtype)
    return jnp.einsum("bgcp,gh->bgchp", t, eye).reshape(nb, gb * c, gb * p)


def _s5_discretise(a_re, a_im, log_dt, b_re, b_im, c_re, c_im):
    g, p = a_re.shape
    nb = g // GROUPS_PER_BLOCK
    dt = jnp.exp(log_dt)[:, None]
    mag = jnp.exp(a_re * dt)
    lam_re = mag * jnp.cos(a_im * dt)
    lam_im = mag * jnp.sin(a_im * dt)
    den = a_re * a_re + a_im * a_im
    q_re = ((lam_re - 1.0) * a_re + lam_im * a_im) / den
    q_im = (lam_im * a_re - (lam_re - 1.0) * a_im) / den
    bb_re = q_re[..., None] * b_re - q_im[..., None] * b_im
    bb_im = q_re[..., None] * b_im + q_im[..., None] * b_re
    tr = lambda t: jnp.swapaxes(t, 1, 2)
    mb = jnp.concatenate([_block_diag(tr(bb_re), nb), _block_diag(tr(bb_im), nb)], axis=-1)
    mc = jnp.concatenate([_block_diag(c_re, nb), -_block_diag(c_im, nb)], axis=-1)
    lam = jnp.concatenate([lam_re.reshape(nb, -1), lam_im.reshape(nb, -1)], axis=-1)
    return mb, mc, lam


def _s5_powers(a_re, a_im, log_dt, sub):
    g, p = a_re.shape
    nb = g // GROUPS_PER_BLOCK
    dt = jnp.exp(log_dt)[:, None]
    ns = list(range(1, sub + 1)) + [2 * sub, 4 * sub]
    ns += [0] * (-len(ns) % SUBLANES)
    e = jnp.asarray(ns, F32)[:, None, None]
    mag = jnp.exp(a_re[None] * dt[None] * e)
    ang = a_im[None] * dt[None] * e
    re = (mag * jnp.cos(ang)).reshape(len(ns), nb, -1)
    im = (mag * jnp.sin(ang)).reshape(len(ns), nb, -1)
    return jnp.transpose(jnp.concatenate([re, im], axis=-1), (1, 0, 2))


def _pack(parts):
    flat = jnp.concatenate([a.reshape(-1).astype(F32) for a in parts])
    n = flat.shape[0]
    pad = -n % (SUBLANES * LANES)
    return jnp.pad(flat, (0, pad)).reshape(-1, LANES)


def _unpack(buf, like):
    flat = buf.reshape(-1)
    out, o = [], 0
    for a in like:
        out.append(flat[o:o + a.size].reshape(a.shape))
        o += a.size
    return out


def kernel(x, meta_tokens, g_ffn1, ffn1_w_gate, ffn1_w_up, ffn1_w_down, g_mix, w_in, b_gate, ssm_a_re, ssm_a_im, ssm_log_dt, ssm_b_re, ssm_b_im, ssm_c_re, ssm_c_im, ssm_d, ssm_w_glu, conv_w, conv_w_out, w_o, g_ffn2, ffn2_w_gate, ffn2_w_up, ffn2_w_down, g_final, loss_target, m_meta_tokens, m_g_ffn1, m_ffn1_w_gate, m_ffn1_w_up, m_ffn1_w_down, m_g_mix, m_w_in, m_b_gate, m_ssm_a_re, m_ssm_a_im, m_ssm_log_dt, m_ssm_b_re, m_ssm_b_im, m_ssm_c_re, m_ssm_c_im, m_ssm_d, m_ssm_w_glu, m_conv_w, m_conv_w_out, m_w_o, m_g_ffn2, m_ffn2_w_gate, m_ffn2_w_up, m_ffn2_w_down, m_g_final, v_meta_tokens, v_g_ffn1, v_ffn1_w_gate, v_ffn1_w_up, v_ffn1_w_down, v_g_mix, v_w_in, v_b_gate, v_ssm_a_re, v_ssm_a_im, v_ssm_log_dt, v_ssm_b_re, v_ssm_b_im, v_ssm_c_re, v_ssm_c_im, v_ssm_d, v_ssm_w_glu, v_conv_w, v_conv_w_out, v_w_o, v_g_ffn2, v_ffn2_w_gate, v_ffn2_w_up, v_ffn2_w_down, v_g_final):
    seq, d = x.shape[1], x.shape[2]
    n_meta = meta_tokens.shape[0]
    dh = d // 2
    tp = -(-(n_meta + seq) // ROW_ALIGN) * ROW_ALIGN
    mx, my, mc_ = _position()
    pos = jnp.stack([mx, my, mc_]).astype(jnp.int32)
    shard = 2 * mx + my

    big_names = ["ffn1_w_gate", "ffn1_w_up", "ffn1_w_down", "w_in", "ssm_w_glu", "conv_w_out", "w_o",
                 "ffn2_w_gate", "ffn2_w_up", "ffn2_w_down"]
    big_w = [ffn1_w_gate[0], ffn1_w_up[0], ffn1_w_down[0], w_in[0], ssm_w_glu[0], conv_w_out[0], w_o[0],
             ffn2_w_gate[0], ffn2_w_up[0], ffn2_w_down[0]]
    big_m = [m_ffn1_w_gate[0], m_ffn1_w_up[0], m_ffn1_w_down[0], m_w_in[0], m_ssm_w_glu[0], m_conv_w_out[0],
             m_w_o[0], m_ffn2_w_gate[0], m_ffn2_w_up[0], m_ffn2_w_down[0]]
    big_v = [v_ffn1_w_gate[0], v_ffn1_w_up[0], v_ffn1_w_down[0], v_w_in[0], v_ssm_w_glu[0], v_conv_w_out[0],
             v_w_o[0], v_ffn2_w_gate[0], v_ffn2_w_up[0], v_ffn2_w_down[0]]
    pieces = [_cast_piece(w, pos) for w in big_w]
    conv_local = conv_w.reshape(conv_w.shape[1], conv_w.shape[3])
    walls, smalls = _allgather(pieces, [meta_tokens, conv_local])
    stacked = [wl.reshape((4, -1, wl.shape[-1])) for wl in walls]
    w1g, w1u, w1d, win_all, wglu_s, wco_s, wo_s, w2g, w2u, w2d = stacked
    natural_cols = lambda s: jnp.transpose(s, (1, 0, 2)).reshape(s.shape[1], 4 * s.shape[2])
    wglu_all = natural_cols(wglu_s)
    wco_all = natural_cols(wco_s)
    wo_all = wo_s.reshape(d, d)
    meta_full = natural_cols(smalls[0])
    cw_full = natural_cols(smalls[1])
    cw_pad = jnp.pad(cw_full, ((0, SUBLANES - cw_full.shape[0]), (0, 0)))

    s5_args = (ssm_a_re[0], ssm_a_im[0], ssm_log_dt[0], ssm_b_re[0], ssm_b_im[0], ssm_c_re[0], ssm_c_im[0])
    (mb, mc, _), disc_vjp = jax.vjp(_s5_discretise, *s5_args)
    powt = _s5_powers(ssm_a_re[0], ssm_a_im[0], ssm_log_dt[0], SCAN_TILE // SUBLANES)
    mb16, mc16 = mb.astype(BF16), mc.astype(BF16)

    pad_rows = tp - n_meta - seq
    h0 = jnp.concatenate([meta_full, x[0], jnp.zeros((pad_rows, d), F32)], axis=0)
    tgt = jnp.concatenate([jnp.zeros((n_meta, d), F32), loss_target[0], jnp.zeros((pad_rows, d), F32)], axis=0)
    h1, a1, b1 = _ffn_fwd(h0, g_ffn1, w1g, w1u, w1d, "ffn1_fwd")
    u, p = _win_fwd(h1, g_mix, win_all)
    ys5, bnd = _scan_fwd(p, mb16, mc16, powt, ssm_d)
    h2 = _mix_fwd(h1, ys5, p, cw_pad, b_gate, wglu_all, wco_all, wo_all)
    dh3, a2, b2, dg_final, loss_part = _ffn_fwd(
        h2, g_ffn2, w2g, w2u, w2d, "ffn2_fwd_loss", final=(g_final.reshape(1, d), tgt, n_meta, seq))

    dh2, dw2g, dw2u, dw2d, dg_ffn2 = _ffn_bwd(dh3, h2, g_ffn2, a2, b2, w2g, w2u, w2d, "ffn2_bwd")
    dys5, dpb, dwo, dwglu, dwco, dcw, dbg = _mix_bwd(dh2, ys5, p, cw_pad, b_gate, wglu_all, wco_all, wo_all)
    dug, dmb, dmc, dlam, dd = _scan_bwd(p, dys5, mb16, mc16, powt, ssm_d, bnd)
    dh1, dwin, dg_mix = _win_bwd(dpb, dug, u, win_all, h1, g_mix, dh2)
    dh0, dw1g, dw1u, dw1d, dg_ffn1 = _ffn_bwd(dh1, h0, g_ffn1, a1, b1, w1g, w1u, w1d, "ffn1_bwd")

    s5_grads = disc_vjp((dmb, dmc, jnp.sum(dlam, axis=1)))
    grad_x = dh0[n_meta:n_meta + seq][None]

    shapes = [w.shape for w in big_w]
    kinds = ["stacked", "stacked", "stacked", "stacked", "col", "col", "row", "stacked", "stacked", "stacked"]
    full_grads = [dw1g, dw1u, dw1d, dwin, dwglu, dwco, dwo, dw2g, dw2u, dw2d]
    grads = [_Grad(a, k, s) for a, k, s in zip(full_grads, kinds, shapes)]
    big_out = _reduce_scatter_adamw(grads, list(zip(big_w, big_m, big_v)), pos)
    big_out = {nme: tuple(o[None] for o in outs) for nme, outs in zip(big_names, big_out)}

    small_names = ["g_ffn1", "g_mix", "b_gate", "ssm_a_re", "ssm_a_im", "ssm_log_dt", "ssm_b_re", "ssm_b_im",
                   "ssm_c_re", "ssm_c_im", "ssm_d", "g_ffn2", "g_final", "meta_tokens", "conv_w"]
    small_w = [g_ffn1, g_mix, b_gate, ssm_a_re, ssm_a_im, ssm_log_dt, ssm_b_re, ssm_b_im, ssm_c_re, ssm_c_im,
               ssm_d, g_ffn2, g_final, meta_tokens, conv_w]
    small_m = [m_g_ffn1, m_g_mix, m_b_gate, m_ssm_a_re, m_ssm_a_im, m_ssm_log_dt, m_ssm_b_re, m_ssm_b_im,
               m_ssm_c_re, m_ssm_c_im, m_ssm_d, m_g_ffn2, m_g_final, m_meta_tokens, m_conv_w]
    small_v = [v_g_ffn1, v_g_mix, v_b_gate, v_ssm_a_re, v_ssm_a_im, v_ssm_log_dt, v_ssm_b_re, v_ssm_b_im,
               v_ssm_c_re, v_ssm_c_im, v_ssm_d, v_g_ffn2, v_g_final, v_meta_tokens, v_conv_w]
    local_small = [dg_ffn1, dg_mix, dbg, *s5_grads, jnp.sum(dd, axis=0), dg_ffn2, dg_final,
                   dh0[:n_meta], dcw[:conv_w.shape[1]]]
    reduced = _unpack(_allreduce_small(_pack(local_small)), local_small)
    reduced[-2] = lax.dynamic_slice_in_dim(reduced[-2], shard * meta_tokens.shape[1], meta_tokens.shape[1], 1)
    reduced[-1] = lax.dynamic_slice_in_dim(reduced[-1], shard * conv_w.shape[3], conv_w.shape[3], 1)
    small_g = [r.reshape(w.shape) for r, w in zip(reduced, small_w)]
    ds_, nm_, nv_ = _adamw_small(_pack(small_w), _pack(small_g), _pack(small_m), _pack(small_v))
    small_out = {nme: o for nme, o in zip(
        small_names, zip(small_g, _unpack(ds_, small_w), _unpack(nm_, small_w), _unpack(nv_, small_w)))}

    loss = lax.psum(loss_part[0, 0], ("x", "y", "c"))
    order = ["meta_tokens", "g_ffn1", "ffn1_w_gate", "ffn1_w_up", "ffn1_w_down", "g_mix", "w_in", "b_gate",
             "ssm_a_re", "ssm_a_im", "ssm_log_dt", "ssm_b_re", "ssm_b_im", "ssm_c_re", "ssm_c_im", "ssm_d",
             "ssm_w_glu", "conv_w", "conv_w_out", "w_o", "g_ffn2", "ffn2_w_gate", "ffn2_w_up", "ffn2_w_down",
             "g_final"]
    res = {**big_out, **small_out}
    return (loss, grad_x, *[res[nme][0] for nme in order], *[res[nme][1] for nme in order],
            *[res[nme][2] for nme in order], *[res[nme][3] for nme in order])
```

```python
import functools
import math

import jax
import jax.numpy as jnp
from jax import lax
from jax.experimental import pallas as pl
from jax.experimental.pallas import tpu as pltpu

F32 = jnp.float32
BF16 = jnp.bfloat16
MESH = pl.DeviceIdType.MESH

RMS_EPS = 1e-6
ADAM_LR = 0.001
ADAM_B1 = 0.9
ADAM_B2 = 0.999
ADAM_EPS = 1e-08
ADAM_WD = 0.01
ADAM_STEP = 10

LANES = 128
SUBLANES = 8
VMEM_LIMIT = 56 * 1024 * 1024

ROW_ALIGN = 256
SCAN_TILE = 256
GROUPS_PER_BLOCK = 8


def _params(sem, vmem=VMEM_LIMIT):
    return pltpu.CompilerParams(dimension_semantics=sem, vmem_limit_bytes=vmem)


def _pick_tile(n, candidates):
    for c in candidates:
        if n % c == 0:
            return c
    raise ValueError(f"no tile for {n}")


def _dot(a, b):
    return jnp.dot(a, b, preferred_element_type=F32)


def _dot_nt(a, b):
    return lax.dot_general(a, b, (((1,), (1,)), ((), ())), preferred_element_type=F32)


def _dot_tn(a, b):
    return lax.dot_general(a, b, (((0,), (0,)), ((), ())), preferred_element_type=F32)


def _sigmoid(x):
    return 1.0 / (1.0 + jnp.exp(-x))


def _rms_stats(h):
    r = lax.rsqrt(jnp.mean(h * h, axis=-1, keepdims=True) + RMS_EPS)
    return h * r, r


def _rms_bwd(xhat, r, g, dn):
    dxh = dn * g
    return r * (dxh - xhat * jnp.mean(dxh * xhat, axis=-1, keepdims=True))


GELU_K = math.sqrt(2.0 / math.pi)
GELU_C = 0.044715


def _gelu(x):
    return 0.5 * x * (1.0 + jnp.tanh(GELU_K * (x + GELU_C * x * x * x)))


def _gelu_grad(x):
    t = jnp.tanh(GELU_K * (x + GELU_C * x * x * x))
    return 0.5 * (1.0 + t) + 0.5 * x * (1.0 - t * t) * GELU_K * (1.0 + 3.0 * GELU_C * x * x)


def _ffn_fwd(h, g, wg, wu, wd, name, final=None, carry=None):
    tp, d = h.shape
    ns, _, f4 = wg.shape
    tm = _pick_tile(tp, (768, 512, 256))
    ni = tp // tm

    def body(*refs):
        refs, phases = split(refs)
        if final is None:
            h_ref, g_ref, wg_ref, wu_ref, wd_ref, ho_ref, a_ref, b_ref, n_scr, acc = refs
        else:
            (h_ref, g_ref, wg_ref, wu_ref, wd_ref, gf_ref, tg_ref,
             ho_ref, a_ref, b_ref, dgf_ref, loss_ref, n_scr, acc) = refs
        i = pl.program_id(0)
        k = pl.program_id(1)
        _run_phases(phases, carry, i * ns + k, ni * ns)

        @pl.when(k == 0)
        def _():
            xhat, _ = _rms_stats(h_ref[...])
            n_scr[...] = (xhat * g_ref[...]).astype(BF16)
            acc[...] = jnp.zeros_like(acc)

        n = n_scr[...]
        a = _dot(n, wg_ref[...])
        b = _dot(n, wu_ref[...])
        a_ref[...] = a.astype(BF16)
        b_ref[...] = b.astype(BF16)
        s = (a * _sigmoid(a) * b).astype(BF16)
        acc[...] += _dot(s, wd_ref[...])

        if final is None:
            @pl.when(k == ns - 1)
            def _():
                ho_ref[...] = h_ref[...] + 0.5 * acc[...]
        else:
            n_meta, seq = final[2], final[3]

            @pl.when((i == 0) & (k == 0))
            def _():
                dgf_ref[...] = jnp.zeros_like(dgf_ref)
                loss_ref[...] = jnp.zeros_like(loss_ref)

            @pl.when(k == ns - 1)
            def _():
                h3 = h_ref[...] + 0.5 * acc[...]
                xhat, r = _rms_stats(h3)
                gf = gf_ref[...]
                row = i * tm + lax.broadcasted_iota(jnp.int32, (tm, d), 0)
                valid = (row >= n_meta) & (row < n_meta + seq)
                diff = jnp.where(valid, xhat * gf - tg_ref[...], 0.0)
                dout = diff * (1.0 / d)
                loss_ref[...] += jnp.full(loss_ref.shape, 0.5 * jnp.sum(diff * diff) * (1.0 / d), F32)
                dgf_ref[...] += jnp.sum(dout * xhat, axis=0, keepdims=True)
                ho_ref[...] = _rms_bwd(xhat, r, gf, dout)

    row_spec = pl.BlockSpec((tm, d), lambda i, k: (i, 0))
    vec_spec = pl.BlockSpec((1, d), lambda i, k: (0, 0))
    in_specs = [row_spec, vec_spec,
                pl.BlockSpec((None, d, f4), lambda i, k: (k, 0, 0)),
                pl.BlockSpec((None, d, f4), lambda i, k: (k, 0, 0)),
                pl.BlockSpec((None, f4, d), lambda i, k: (k, 0, 0))]
    act_spec = pl.BlockSpec((None, tm, f4), lambda i, k: (k, i, 0))
    out_specs = [row_spec, act_spec, act_spec]
    out_shape = [jax.ShapeDtypeStruct((tp, d), F32),
                 jax.ShapeDtypeStruct((ns, tp, f4), BF16),
                 jax.ShapeDtypeStruct((ns, tp, f4), BF16)]
    args = [h, g, wg, wu, wd]
    if final is not None:
        in_specs += [vec_spec, row_spec]
        args += [final[0], final[1]]
        out_specs += [vec_spec, pl.BlockSpec((1, LANES), lambda i, k: (0, 0))]
        out_shape += [jax.ShapeDtypeStruct((1, d), F32), jax.ShapeDtypeStruct((1, LANES), F32)]
    scratch = [pltpu.VMEM((tm, d), BF16), pltpu.VMEM((tm, d), F32)]
    split = _attach_carry(carry, in_specs, args, out_specs, out_shape, scratch)
    return pl.pallas_call(
        body, name=name, grid=(ni, ns), in_specs=in_specs, out_specs=out_specs, out_shape=out_shape,
        scratch_shapes=scratch, compiler_params=_params(("arbitrary", "arbitrary")),
    )(*args)


def _ffn_bwd_shard(k, ns, dn_prev, dh_out, h_in, g, a, b, wg, wu, wd, name, carry=None):
    tp, d = h_in.shape
    f4 = wg.shape[2]
    tm = ROW_ALIGN
    ni = tp // tm
    first, last = k == 0, k == ns - 1

    def body(*refs):
        refs, phases = split(refs)
        acc_in = None if first else refs.pop(0)
        dh_ref, h_ref, g_ref, a_ref, b_ref, wg_ref, wu_ref, wd_ref, acc_out, dwg_ref, dwu_ref, dwd_ref = refs[:12]
        i = pl.program_id(0)
        _run_phases(phases, carry, i, ni)

        @pl.when(i == 0)
        def _():
            dwg_ref[...] = jnp.zeros_like(dwg_ref)
            dwu_ref[...] = jnp.zeros_like(dwu_ref)
            dwd_ref[...] = jnp.zeros_like(dwd_ref)
            if last:
                refs[12][...] = jnp.zeros_like(refs[12])

        xhat, r = _rms_stats(h_ref[...])
        gv = g_ref[...]
        n = (xhat * gv).astype(BF16)
        dy = (0.5 * dh_ref[...]).astype(BF16)
        av = a_ref[...].astype(F32)
        bv = b_ref[...].astype(F32)
        sg = _sigmoid(av)
        silu = av * sg
        ds = _dot_nt(dy, wd_ref[...])
        da = (ds * bv * (sg * (1.0 + av * (1.0 - sg)))).astype(BF16)
        db = (ds * silu).astype(BF16)
        s = (silu * bv).astype(BF16)
        dwd_ref[...] += _dot_tn(s, dy)
        dwg_ref[...] += _dot_tn(n, da)
        dwu_ref[...] += _dot_tn(n, db)
        dn = _dot_nt(da, wg_ref[...]) + _dot_nt(db, wu_ref[...])
        if not first:
            dn = dn + acc_in[...]
        if last:
            refs[12][...] += jnp.sum(dn * xhat, axis=0, keepdims=True)
            acc_out[...] = dh_ref[...] + _rms_bwd(xhat, r, gv, dn)
        else:
            acc_out[...] = dn

    row_spec = pl.BlockSpec((tm, d), lambda i: (i, 0))
    vec_spec = pl.BlockSpec((1, d), lambda i: (0, 0))
    act_spec = pl.BlockSpec((None, tm, f4), lambda i: (k, i, 0))
    wcol = pl.BlockSpec((None, d, f4), lambda i: (k, 0, 0))
    wrow = pl.BlockSpec((None, f4, d), lambda i: (k, 0, 0))
    whole = lambda shape: pl.BlockSpec(shape, lambda i: (0, 0))
    in_specs = [row_spec, row_spec, vec_spec, act_spec, act_spec, wcol, wcol, wrow]
    args = [dh_out, h_in, g, a, b, wg, wu, wd]
    if not first:
        in_specs.insert(0, row_spec)
        args.insert(0, dn_prev)
    out_specs = [row_spec, whole((d, f4)), whole((d, f4)), whole((f4, d))]
    out_shape = [jax.ShapeDtypeStruct((tp, d), F32), jax.ShapeDtypeStruct((d, f4), F32),
                 jax.ShapeDtypeStruct((d, f4), F32), jax.ShapeDtypeStruct((f4, d), F32)]
    if last:
        out_specs.append(vec_spec)
        out_shape.append(jax.ShapeDtypeStruct((1, d), F32))
    n_host = len(out_shape)
    scratch = []
    split = _attach_carry(carry, in_specs, args, out_specs, out_shape, scratch)
    outs = pl.pallas_call(
        body, name=f"{name}_{k}", grid=(ni,), in_specs=in_specs, out_specs=out_specs, out_shape=out_shape,
        scratch_shapes=scratch, compiler_params=_params(("arbitrary",)),
    )(*args)
    return outs[:n_host], outs[n_host:]


def _ffn_bwd(dh_out, h_in, g, a, b, wg, wu, wd, name, chain=None):
    ns = wg.shape[0]
    acc, dwg, dwu, dwd, dg = None, [], [], [], None
    for k in range(ns):
        carry = chain.carry() if chain is not None else None
        outs, carried = _ffn_bwd_shard(k, ns, acc, dh_out, h_in, g, a, b, wg, wu, wd, name, carry)
        if chain is not None:
            chain.feed(carried)
        acc = outs[0]
        dwg.append(outs[1])
        dwu.append(outs[2])
        dwd.append(outs[3])
        if k == ns - 1:
            dg = outs[4]
    return acc, jnp.stack(dwg), jnp.stack(dwu), jnp.stack(dwd), dg


def _win_fwd(h, g, w_in):
    tp, d = h.shape
    ns = w_in.shape[0]
    tm = _pick_tile(tp, (768, 512, 256))

    def body(h_ref, g_ref, w_ref, u_ref, p_ref):
        @pl.when(pl.program_id(1) == 0)
        def _():
            xhat, _ = _rms_stats(h_ref[...])
            u_ref[...] = (xhat * g_ref[...]).astype(BF16)

        p_ref[...] = _dot(u_ref[...], w_ref[...]).astype(BF16)

    return pl.pallas_call(
        body, name="win_fwd", grid=(tp // tm, ns),
        in_specs=[pl.BlockSpec((tm, d), lambda i, k: (i, 0)),
                  pl.BlockSpec((1, d), lambda i, k: (0, 0)),
                  pl.BlockSpec((None, d, d), lambda i, k: (k, 0, 0))],
        out_specs=[pl.BlockSpec((tm, d), lambda i, k: (i, 0)),
                   pl.BlockSpec((None, tm, d), lambda i, k: (k, i, 0))],
        out_shape=[jax.ShapeDtypeStruct((tp, d), BF16), jax.ShapeDtypeStruct((ns, tp, d), BF16)],
        compiler_params=_params(("arbitrary", "arbitrary")),
    )(h, g, w_in)


def _win_bwd_shard(k, ns, du_prev, dpb, dug, u, w_in, h1, g, dh2):
    tp, d = h1.shape
    dh = d // 2
    tm = ROW_ALIGN
    first, last = k == 0, k == ns - 1

    def body(*refs):
        refs = list(refs)
        acc_in = None if first else refs.pop(0)
        dug_ref = refs.pop(0) if first else None
        dp_ref, u_ref, w_ref = refs[:3]
        refs = refs[3:]
        if last:
            h_ref, g_ref, dh2_ref, acc_out, dw_ref, dg_ref = refs
        else:
            acc_out, dw_ref = refs
        i = pl.program_id(0)

        @pl.when(i == 0)
        def _():
            dw_ref[...] = jnp.zeros_like(dw_ref)
            if last:
                dg_ref[...] = jnp.zeros_like(dg_ref)

        dp = dp_ref[...]
        if first:
            dp = jnp.concatenate([dug_ref[...], dp[:, dh:]], axis=1)
        dw_ref[...] += _dot_tn(u_ref[...], dp)
        du = _dot_nt(dp, w_ref[...])
        if not first:
            du = du + acc_in[...]
        if last:
            xhat, r = _rms_stats(h_ref[...])
            dg_ref[...] += jnp.sum(du * xhat, axis=0, keepdims=True)
            acc_out[...] = dh2_ref[...] + _rms_bwd(xhat, r, g_ref[...], du)
        else:
            acc_out[...] = du

    row_spec = pl.BlockSpec((tm, d), lambda i: (i, 0))
    vec_spec = pl.BlockSpec((1, d), lambda i: (0, 0))
    in_specs = [pl.BlockSpec((None, tm, d), lambda i: (k, i, 0)), row_spec,
                pl.BlockSpec((None, d, d), lambda i: (k, 0, 0))]
    args = [dpb, u, w_in]
    if first:
        in_specs.insert(0, pl.BlockSpec((tm, dh), lambda i: (i, 0)))
        args.insert(0, dug)
    else:
        in_specs.insert(0, row_spec)
        args.insert(0, du_prev)
    out_specs = [row_spec, pl.BlockSpec((d, d), lambda i: (0, 0))]
    out_shape = [jax.ShapeDtypeStruct((tp, d), F32), jax.ShapeDtypeStruct((d, d), F32)]
    if last:
        in_specs += [row_spec, vec_spec, row_spec]
        args += [h1, g, dh2]
        out_specs.append(vec_spec)
        out_shape.append(jax.ShapeDtypeStruct((1, d), F32))
    return pl.pallas_call(
        body, name=f"win_bwd_{k}", grid=(tp // tm,), in_specs=in_specs, out_specs=out_specs,
        out_shape=out_shape, compiler_params=_params(("arbitrary",)),
    )(*args)


def _win_bwd(dpb, dug, u, w_in, h1, g, dh2):
    ns = w_in.shape[0]
    acc, dws, dg = None, [], None
    for k in range(ns):
        outs = _win_bwd_shard(k, ns, acc, dpb, dug, u, w_in, h1, g, dh2)
        acc = outs[0]
        dws.append(outs[1])
        if k == ns - 1:
            dg = outs[2]
    return acc, jnp.stack(dws), dg


def _cmul(ar, ai, br, bi):
    return ar * br - ai * bi, ar * bi + ai * br


def _scan_rows(j, sub):
    return pl.ds(j * SUBLANES, SUBLANES)


def _permute_rows(src_ref, dst_ref, sub):
    for j in range(sub):
        dst_ref[pl.ds(j * SUBLANES, SUBLANES), :] = src_ref[pl.ds(j, SUBLANES, stride=sub), :]


def _unpermute_rows(src_ref, dst_ref, sub):
    for j in range(sub):
        dst_ref[pl.ds(j, SUBLANES, stride=sub), :] = src_ref[pl.ds(j * SUBLANES, SUBLANES), :]


def _local_scan(x_ref, lr, li, w, sub, reverse):
    hr = jnp.zeros((SUBLANES, w), F32)
    hi = jnp.zeros((SUBLANES, w), F32)
    order = range(sub - 1, -1, -1) if reverse else range(sub)
    for j in order:
        xr = x_ref[_scan_rows(j, sub), pl.ds(0, w)]
        xi = x_ref[_scan_rows(j, sub), pl.ds(w, w)]
        if reverse:
            hr, hi = lr * hr + li * hi + xr, lr * hi - li * hr + xi
        else:
            hr, hi = lr * hr - li * hi + xr, lr * hi + li * hr + xi
        x_ref[_scan_rows(j, sub), pl.ds(0, w)] = hr
        x_ref[_scan_rows(j, sub), pl.ds(w, w)] = hi
    return hr, hi


def _entering_states(er, ei, fr, fi, pow_ref, w, sub, reverse):
    lane = lax.broadcasted_iota(jnp.int32, (SUBLANES, w), 0)
    if reverse:
        edge, shift1 = SUBLANES - 1, SUBLANES - 1
    else:
        edge, shift1 = 0, 1
    zr = jnp.where(lane == edge, pltpu.roll(fr, shift1, 0), pltpu.roll(er, shift1, 0))
    zi = jnp.where(lane == edge, pltpu.roll(fi, shift1, 0), pltpu.roll(ei, shift1, 0))
    for step, row in ((1, sub - 1), (2, sub), (4, sub + 1)):
        ar = pow_ref[pl.ds(row, 1), pl.ds(0, w)]
        ai = pow_ref[pl.ds(row, 1), pl.ds(w, w)]
        if reverse:
            ai = -ai
            keep = lane < SUBLANES - step
            sr = jnp.where(keep, pltpu.roll(zr, SUBLANES - step, 0), 0.0)
            si = jnp.where(keep, pltpu.roll(zi, SUBLANES - step, 0), 0.0)
        else:
            keep = lane >= step
            sr = jnp.where(keep, pltpu.roll(zr, step, 0), 0.0)
            si = jnp.where(keep, pltpu.roll(zi, step, 0), 0.0)
        pr, pi = _cmul(ar, ai, sr, si)
        zr, zi = zr + pr, zi + pi
    ar = pow_ref[pl.ds(sub - 1, 1), pl.ds(0, w)]
    ai = pow_ref[pl.ds(sub - 1, 1), pl.ds(w, w)]
    if reverse:
        ai = -ai
    pr, pi = _cmul(ar, ai, zr, zi)
    return zr, zi, er + pr, ei + pi


def _scan_fwd(p, mb, mc, powt, dskip):
    _, tp, d = p.shape
    nb, cb, w2 = mb.shape
    w = w2 // 2
    q = SCAN_TILE
    sub = q // SUBLANES
    nt = tp // q
    ds = d // 2

    def body(ug_ref, mb_ref, mc_ref, pow_ref, d_ref, y_ref, bnd_ref, x_scr, carry, nat, perm):
        t = pl.program_id(1)

        @pl.when(t == 0)
        def _():
            carry[...] = jnp.zeros_like(carry)

        ugf = ug_ref[...].astype(F32)
        nat[...] = ugf
        _permute_rows(nat, perm, sub)
        x_scr[...] = _dot(perm[...].astype(BF16), mb_ref[...])
        lr = jnp.broadcast_to(pow_ref[pl.ds(0, 1), pl.ds(0, w)], (SUBLANES, w))
        li = jnp.broadcast_to(pow_ref[pl.ds(0, 1), pl.ds(w, w)], (SUBLANES, w))
        er, ei = _local_scan(x_scr, lr, li, w, sub, False)
        zr, zi, fr, fi = _entering_states(er, ei, carry[:, pl.ds(0, w)], carry[:, pl.ds(w, w)],
                                          pow_ref, w, sub, False)
        carry[:, pl.ds(0, w)] = fr
        carry[:, pl.ds(w, w)] = fi
        bnd_ref[:, pl.ds(0, w)] = fr
        bnd_ref[:, pl.ds(w, w)] = fi
        for j in range(sub):
            pr = pow_ref[pl.ds(j, 1), pl.ds(0, w)]
            pi = pow_ref[pl.ds(j, 1), pl.ds(w, w)]
            cr, ci = _cmul(pr, pi, zr, zi)
            x_scr[_scan_rows(j, sub), pl.ds(0, w)] += cr
            x_scr[_scan_rows(j, sub), pl.ds(w, w)] += ci
        hb = x_scr[...].astype(BF16)
        perm[...] = _dot_nt(hb, mc_ref[...])
        _unpermute_rows(perm, nat, sub)
        y_ref[...] = nat[...] + d_ref[...] * ugf

    return pl.pallas_call(
        body, name="s5_scan_fwd", grid=(nb, nt),
        in_specs=[pl.BlockSpec((None, q, cb), lambda b, t: (0, t, b)),
                  pl.BlockSpec((None, cb, w2), lambda b, t: (b, 0, 0)),
                  pl.BlockSpec((None, cb, w2), lambda b, t: (b, 0, 0)),
                  pl.BlockSpec((None, powt.shape[1], w2), lambda b, t: (b, 0, 0)),
                  pl.BlockSpec((1, cb), lambda b, t: (0, b))],
        out_specs=[pl.BlockSpec((q, cb), lambda b, t: (t, b)),
                   pl.BlockSpec((None, None, SUBLANES, w2), lambda b, t: (b, t, 0, 0))],
        out_shape=[jax.ShapeDtypeStruct((tp, ds), F32),
                   jax.ShapeDtypeStruct((nb, nt, SUBLANES, w2), F32)],
        scratch_shapes=[pltpu.VMEM((q, w2), F32), pltpu.VMEM((SUBLANES, w2), F32),
                        pltpu.VMEM((q, cb), F32), pltpu.VMEM((q, cb), F32)],
        compiler_params=_params(("arbitrary", "arbitrary")),
    )(p, mb, mc, powt, dskip)


def _scan_bwd(p, dy, mb, mc, powt, dskip, bnd):
    _, tp, d = p.shape
    nb, cb, w2 = mb.shape
    w = w2 // 2
    q = SCAN_TILE
    sub = q // SUBLANES
    nt = tp // q
    ds = d // 2

    def body(ug_ref, dy_ref, mb_ref, mc_ref, pow_ref, d_ref, bnd_ref,
             dug_ref, dmb_ref, dmc_ref, dlam_ref, dd_ref, x_scr, y_scr, gcarry, nat, perm):
        t = pl.program_id(1)
        tt = nt - 1 - t

        @pl.when(t == 0)
        def _():
            gcarry[...] = jnp.zeros_like(gcarry)
            dmb_ref[...] = jnp.zeros_like(dmb_ref)
            dmc_ref[...] = jnp.zeros_like(dmc_ref)
            dlam_ref[...] = jnp.zeros_like(dlam_ref)
            dd_ref[...] = jnp.zeros_like(dd_ref)

        ugf = ug_ref[...].astype(F32)
        dyf = dy_ref[...].astype(F32)
        dd_ref[...] += jnp.sum((dyf * ugf).reshape(q // SUBLANES, SUBLANES, cb), axis=0)
        nat[...] = ugf
        _permute_rows(nat, perm, sub)
        ug = perm[...].astype(BF16)
        nat[...] = dyf
        _permute_rows(nat, perm, sub)
        dyb = perm[...].astype(BF16)
        lr = jnp.broadcast_to(pow_ref[pl.ds(0, 1), pl.ds(0, w)], (SUBLANES, w))
        li = jnp.broadcast_to(pow_ref[pl.ds(0, 1), pl.ds(w, w)], (SUBLANES, w))

        x_scr[...] = _dot(ug, mb_ref[...])
        er, ei = _local_scan(x_scr, lr, li, w, sub, False)
        first = tt == 0
        pfr = jnp.where(first, 0.0, bnd_ref[:, pl.ds(0, w)])
        pfi = jnp.where(first, 0.0, bnd_ref[:, pl.ds(w, w)])
        hzr, hzi, _, _ = _entering_states(er, ei, pfr, pfi, pow_ref, w, sub, False)
        for j in range(sub):
            pr = pow_ref[pl.ds(j, 1), pl.ds(0, w)]
            pi = pow_ref[pl.ds(j, 1), pl.ds(w, w)]
            cr, ci = _cmul(pr, pi, hzr, hzi)
            x_scr[_scan_rows(j, sub), pl.ds(0, w)] += cr
            x_scr[_scan_rows(j, sub), pl.ds(w, w)] += ci

        y_scr[...] = _dot(dyb, mc_ref[...])
        er, ei = _local_scan(y_scr, lr, li, w, sub, True)
        gzr, gzi, fr, fi = _entering_states(er, ei, gcarry[:, pl.ds(0, w)], gcarry[:, pl.ds(w, w)],
                                            pow_ref, w, sub, True)
        gcarry[:, pl.ds(0, w)] = fr
        gcarry[:, pl.ds(w, w)] = fi
        accr = jnp.zeros((SUBLANES, w), F32)
        acci = jnp.zeros((SUBLANES, w), F32)
        for j in range(sub):
            pr = pow_ref[pl.ds(sub - 1 - j, 1), pl.ds(0, w)]
            pi = pow_ref[pl.ds(sub - 1 - j, 1), pl.ds(w, w)]
            cr, ci = _cmul(pr, -pi, gzr, gzi)
            gr = y_scr[_scan_rows(j, sub), pl.ds(0, w)] + cr
            gi = y_scr[_scan_rows(j, sub), pl.ds(w, w)] + ci
            y_scr[_scan_rows(j, sub), pl.ds(0, w)] = gr
            y_scr[_scan_rows(j, sub), pl.ds(w, w)] = gi
            if j == 0:
                hpr, hpi = hzr, hzi
            else:
                hpr = x_scr[_scan_rows(j - 1, sub), pl.ds(0, w)]
                hpi = x_scr[_scan_rows(j - 1, sub), pl.ds(w, w)]
            accr += hpr * gr + hpi * gi
            acci += hpr * gi - hpi * gr
        dlam_ref[:, pl.ds(0, w)] += accr
        dlam_ref[:, pl.ds(w, w)] += acci

        hb = x_scr[...].astype(BF16)
        gb = y_scr[...].astype(BF16)
        dmc_ref[...] += _dot_tn(dyb, hb)
        dmb_ref[...] += _dot_tn(ug, gb)
        perm[...] = _dot_nt(gb, mb_ref[...])
        _unpermute_rows(perm, nat, sub)
        dug_ref[...] = (nat[...] + d_ref[...] * dyf).astype(BF16)

    blk = lambda b, t: (b, 0, 0)
    return pl.pallas_call(
        body, name="s5_scan_bwd", grid=(nb, nt),
        in_specs=[pl.BlockSpec((None, q, cb), lambda b, t: (0, nt - 1 - t, b)),
                  pl.BlockSpec((q, cb), lambda b, t: (nt - 1 - t, b)),
                  pl.BlockSpec((None, cb, w2), blk),
                  pl.BlockSpec((None, cb, w2), blk),
                  pl.BlockSpec((None, powt.shape[1], w2), blk),
                  pl.BlockSpec((1, cb), lambda b, t: (0, b)),
                  pl.BlockSpec((None, None, SUBLANES, w2),
                               lambda b, t: (b, jnp.maximum(nt - 2 - t, 0), 0, 0))],
        out_specs=[pl.BlockSpec((q, cb), lambda b, t: (nt - 1 - t, b)),
                   pl.BlockSpec((None, cb, w2), blk),
                   pl.BlockSpec((None, cb, w2), blk),
                   pl.BlockSpec((None, SUBLANES, w2), blk),
                   pl.BlockSpec((SUBLANES, cb), lambda b, t: (0, b))],
        out_shape=[jax.ShapeDtypeStruct((tp, ds), BF16),
                   jax.ShapeDtypeStruct((nb, cb, w2), F32),
                   jax.ShapeDtypeStruct((nb, cb, w2), F32),
                   jax.ShapeDtypeStruct((nb, SUBLANES, w2), F32),
                   jax.ShapeDtypeStruct((SUBLANES, ds), F32)],
        scratch_shapes=[pltpu.VMEM((q, w2), F32), pltpu.VMEM((q, w2), F32),
                        pltpu.VMEM((SUBLANES, w2), F32), pltpu.VMEM((q, cb), F32), pltpu.VMEM((q, cb), F32)],
        compiler_params=_params(("arbitrary", "arbitrary")),
    )(p, dy, mb, mc, powt, dskip, bnd)


HALO = 16


def _mix_tile(ys5, p0, p1, p2, p3, prev_cin, cw, bgate, wglu, wco, d):
    dh = d // 2
    tm = ys5.shape[0]
    v = p0[:, dh:].astype(F32)
    gbr = p1[:, :dh].astype(F32)
    gcr = p1[:, dh:].astype(F32)
    gact = _gelu(ys5).astype(BF16)
    z = _dot(gact, wglu)
    z1, z2 = z[:, :d], z[:, d:]
    sg = _sigmoid(z2)
    y_ssm = z1 * sg
    cin = gcr * v
    ext = jnp.concatenate([cin, prev_cin], axis=0)
    r1 = pltpu.roll(ext, 1, 0)[:tm]
    r2 = pltpu.roll(ext, 2, 0)[:tm]
    cv = cw[2] * cin + cw[1] * r1 + cw[0] * r2
    cg = (gbr * cv).astype(BF16)
    y_conv = _dot(cg, wco)
    g_s = _sigmoid(p2.astype(F32) + bgate[:, :d])
    g_c = _sigmoid(p3.astype(F32) + bgate[:, d:])
    mixed = g_s * y_ssm + g_c * y_conv
    return dict(v=v, gb=gbr, gc=gcr, gact=gact, z1=z1, sg=sg, y_ssm=y_ssm, cin=cin, r1=r1, r2=r2,
                cv=cv, cg=cg, y_conv=y_conv, g_s=g_s, g_c=g_c, mixed=mixed)


def _mix_fwd(h1, ys5, p, cw, bgate, wglu, wco, wo):
    tp, d = h1.shape
    dh = d // 2
    tm = ROW_ALIGN

    def body(h_ref, y_ref, p0_ref, p1_ref, p2_ref, p3_ref, cw_ref, bg_ref, wglu_ref, wco_ref, wo_ref,
             o_ref, prev):
        @pl.when(pl.program_id(0) == 0)
        def _():
            prev[...] = jnp.zeros_like(prev)

        cw = [cw_ref[pl.ds(t, 1), :] for t in range(3)]
        f = _mix_tile(y_ref[...], p0_ref[...], p1_ref[...], p2_ref[...], p3_ref[...], prev[...],
                      cw, bg_ref[...], wglu_ref[...], wco_ref[...], d)
        prev[...] = f["cin"][tm - HALO:, :]
        o_ref[...] = h_ref[...] + _dot(f["mixed"].astype(BF16), wo_ref[...])

    row = pl.BlockSpec((tm, d), lambda i: (i, 0))
    full = lambda a: pl.BlockSpec(a.shape, lambda i: (0,) * a.ndim)
    pk = lambda k: pl.BlockSpec((None, tm, d), lambda i, k=k: (k, i, 0))
    return pl.pallas_call(
        body, name="mix_fwd", grid=(tp // tm,),
        in_specs=[row, pl.BlockSpec((tm, dh), lambda i: (i, 0)), pk(0), pk(1), pk(2), pk(3),
                  full(cw), full(bgate), full(wglu), full(wco), full(wo)],
        out_specs=row, out_shape=jax.ShapeDtypeStruct((tp, d), F32),
        scratch_shapes=[pltpu.VMEM((HALO, dh), F32)],
        compiler_params=_params(("arbitrary",)),
    )(h1, ys5, p, p, p, p, cw, bgate, wglu, wco, wo)


def _mix_bwd(dh2, ys5, p, cw, bgate, wglu, wco, wo):
    tp, d = dh2.shape
    dh = d // 2
    tm = ROW_ALIGN
    ni = tp // tm
    hb = tm // HALO

    def body(dh_ref, y_ref, p0_ref, p1_ref, p2_ref, p3_ref, h0_ref, h1_ref,
             cw_ref, bg_ref, wglu_ref, wco_ref, wo_ref,
             dys_ref, dpb_ref, dwo_ref, dwglu_ref, dwco_ref, dcw_ref, dbg_ref, nxt):
        i = pl.program_id(0)
        tt = ni - 1 - i

        @pl.when(i == 0)
        def _():
            nxt[...] = jnp.zeros_like(nxt)
            dwo_ref[...] = jnp.zeros_like(dwo_ref)
            dwglu_ref[...] = jnp.zeros_like(dwglu_ref)
            dwco_ref[...] = jnp.zeros_like(dwco_ref)
            dcw_ref[...] = jnp.zeros_like(dcw_ref)
            dbg_ref[...] = jnp.zeros_like(dbg_ref)

        cw = [cw_ref[pl.ds(t, 1), :] for t in range(3)]
        prev_cin = h1_ref[:, dh:].astype(F32) * h0_ref[:, dh:].astype(F32)
        prev_cin = jnp.where(tt == 0, 0.0, prev_cin)
        ys5 = y_ref[...]
        f = _mix_tile(ys5, p0_ref[...], p1_ref[...], p2_ref[...], p3_ref[...], prev_cin,
                      cw, bg_ref[...], wglu_ref[...], wco_ref[...], d)
        dhb = dh_ref[...].astype(BF16)
        dmixed = _dot_nt(dhb, wo_ref[...])
        dwo_ref[...] += _dot_tn(f["mixed"].astype(BF16), dhb)

        g_s, g_c, sg = f["g_s"], f["g_c"], f["sg"]
        dy_ssm = dmixed * g_s
        dy_conv = dmixed * g_c
        dp2 = dmixed * f["y_ssm"] * g_s * (1.0 - g_s)
        dp3 = dmixed * f["y_conv"] * g_c * (1.0 - g_c)
        dbg_ref[:, pl.ds(0, d)] += jnp.sum(dp2, axis=0, keepdims=True)
        dbg_ref[:, pl.ds(d, d)] += jnp.sum(dp3, axis=0, keepdims=True)

        dz = jnp.concatenate([dy_ssm * sg, dy_ssm * f["z1"] * sg * (1.0 - sg)], axis=1).astype(BF16)
        dwglu_ref[...] += _dot_tn(f["gact"], dz)
        dys_ref[...] = (_dot_nt(dz, wglu_ref[...]) * _gelu_grad(ys5)).astype(BF16)

        dycb = dy_conv.astype(BF16)
        dwco_ref[...] += _dot_tn(f["cg"], dycb)
        dcg = _dot_nt(dycb, wco_ref[...])
        dgb = dcg * f["cv"]
        dcv = dcg * f["gb"]
        ext = jnp.concatenate([dcv, nxt[...]], axis=0)
        n1 = pltpu.roll(ext, tm + HALO - 1, 0)[:tm]
        n2 = pltpu.roll(ext, tm + HALO - 2, 0)[:tm]
        nxt[...] = dcv[:HALO, :]
        dcin = cw[2] * dcv + cw[1] * n1 + cw[0] * n2
        dcw_ref[pl.ds(0, 1), :] += jnp.sum(dcv * f["r2"], axis=0, keepdims=True)
        dcw_ref[pl.ds(1, 1), :] += jnp.sum(dcv * f["r1"], axis=0, keepdims=True)
        dcw_ref[pl.ds(2, 1), :] += jnp.sum(dcv * f["cin"], axis=0, keepdims=True)
        dgc = dcin * f["v"]
        dv = dcin * f["gc"]
        dpb_ref[0] = jnp.concatenate([jnp.zeros_like(dv), dv], axis=1).astype(BF16)
        dpb_ref[1] = jnp.concatenate([dgb, dgc], axis=1).astype(BF16)
        dpb_ref[2] = dp2.astype(BF16)
        dpb_ref[3] = dp3.astype(BF16)

    rev = lambda i: ni - 1 - i
    row = pl.BlockSpec((tm, d), lambda i: (rev(i), 0))
    half = pl.BlockSpec((tm, dh), lambda i: (rev(i), 0))
    full = lambda a: pl.BlockSpec(a.shape, lambda i: (0,) * a.ndim)
    pk = lambda k: pl.BlockSpec((None, tm, d), lambda i, k=k: (k, rev(i), 0))
    halo = lambda k: pl.BlockSpec((None, HALO, d), lambda i, k=k: (k, jnp.maximum(rev(i) * hb - 1, 0), 0))
    acc = lambda shape: pl.BlockSpec(shape, lambda i: (0,) * len(shape))
    return pl.pallas_call(
        body, name="mix_bwd", grid=(ni,),
        in_specs=[row, half, pk(0), pk(1), pk(2), pk(3), halo(0), halo(1),
                  full(cw), full(bgate), full(wglu), full(wco), full(wo)],
        out_specs=[half, pl.BlockSpec((4, tm, d), lambda i: (0, rev(i), 0)),
                   acc((d, d)), acc((dh, 2 * d)), acc((dh, d)), acc((SUBLANES, dh)), acc((1, 2 * d))],
        out_shape=[jax.ShapeDtypeStruct((tp, dh), BF16), jax.ShapeDtypeStruct((4, tp, d), BF16),
                   jax.ShapeDtypeStruct((d, d), F32), jax.ShapeDtypeStruct((dh, 2 * d), F32),
                   jax.ShapeDtypeStruct((dh, d), F32), jax.ShapeDtypeStruct((SUBLANES, dh), F32),
                   jax.ShapeDtypeStruct((1, 2 * d), F32)],
        scratch_shapes=[pltpu.VMEM((HALO, dh), F32)],
        compiler_params=_params(("arbitrary",)),
    )(dh2, ys5, p, p, p, p, p, p, cw, bgate, wglu, wco, wo)


ANY = pl.BlockSpec(memory_space=pl.ANY)


def _position():
    return lax.axis_index("x"), lax.axis_index("y"), lax.axis_index("c")


def _remote(src, dst, ssem, rsem, dev):
    return pltpu.make_async_remote_copy(src_ref=src, dst_ref=dst, send_sem=ssem, recv_sem=rsem,
                                        device_id=dev, device_id_type=MESH)


def _cast_piece(w, pos):
    rows, cols = w.shape
    r2 = rows // 2

    def body(pos_ref, w_ref, o_ref):
        o_ref[...] = w_ref[...].astype(BF16)

    return pl.pallas_call(
        body, name="cast_piece",
        grid_spec=pltpu.PrefetchScalarGridSpec(
            num_scalar_prefetch=1, grid=(1,),
            in_specs=[pl.BlockSpec((r2, cols), lambda i, pos: (pos[2], 0))],
            out_specs=pl.BlockSpec((r2, cols), lambda i, pos: (0, 0))),
        out_shape=jax.ShapeDtypeStruct((r2, cols), BF16),
        compiler_params=_params(("arbitrary",)),
    )(pos, w)


class _Carry:
    def __init__(self, name, arrays, out_shapes, nsem, nlsem, make, fracs):
        self.name, self.arrays, self.out_shapes = name, list(arrays), list(out_shapes)
        self.nsem, self.nlsem, self.make, self.fracs = nsem, max(nlsem, 1), make, fracs


def _carry_scratch(carry):
    return [pltpu.SemaphoreType.DMA((carry.nsem,)), pltpu.SemaphoreType.DMA((carry.nsem,)),
            pltpu.SemaphoreType.DMA((carry.nlsem,))]


def _run_carry(carry):
    na, no = len(carry.arrays), len(carry.out_shapes)

    def body(*refs):
        for phase in carry.make(refs[:na], refs[na:na + no], *refs[na + no:]):
            phase()

    return pl.pallas_call(
        body, name=carry.name, in_specs=[ANY] * na, out_specs=[ANY] * no, out_shape=carry.out_shapes,
        scratch_shapes=_carry_scratch(carry),
    )(*carry.arrays)


def _attach_carry(carry, in_specs, args, out_specs, out_shape, scratch):
    nhi, nho, nhs = len(in_specs), len(out_specs), len(scratch)
    if carry is None:
        return lambda refs: (list(refs), [])
    na, no = len(carry.arrays), len(carry.out_shapes)
    in_specs += [ANY] * na
    args += carry.arrays
    out_specs += [ANY] * no
    out_shape += carry.out_shapes
    scratch += _carry_scratch(carry)

    def split(refs):
        refs = list(refs)
        o = nhi + na
        host = refs[:nhi] + refs[o:o + nho] + refs[o + nho + no:o + nho + no + nhs]
        sems = refs[o + nho + no + nhs:]
        return host, carry.make(refs[nhi:o], refs[o + nho:o + nho + no], *sems)

    return split


def _run_phases(phases, carry, step, total):
    for phase, frac in zip(phases, carry.fracs if carry is not None else ()):
        pl.when(step == int(round(frac * (total - 1))))(phase)


def _allgather_carry(name, pieces, smalls):
    n, ns = len(pieces), len(smalls)
    per = 14
    n_big = per * n

    def make(ins, outs, ssem, rsem, lsem):
        pin, sin = ins[:n], ins[n:]
        wall, sall = outs[:n], outs[n:]
        x, y, c = _position()
        xnb, ynb, sib = (1 - x, y, c), (x, 1 - y, c), (x, y, 1 - c)
        chips = [(1 - x, y), (x, 1 - y), (1 - x, 1 - y)]
        r4 = [p.shape[0] // 2 for p in pin]
        own = lambda i, h: pin[i].at[pl.ds(h * r4[i], r4[i]), :]
        slot = lambda i, xx, yy, cc, h: wall[i].at[xx, yy, cc, h]
        cp = lambda src, dst, s, dev: _remote(src, dst, ssem.at[s], rsem.at[s], dev)
        to_sib = lambda i, xx, yy, h: cp(slot(i, xx, yy, c, h), slot(i, xx, yy, c, h),
                                         per * i + 6 + 4 * xx + 2 * yy + h, sib)

        def local():
            cps = [pltpu.make_async_copy(own(i, h), slot(i, x, y, c, h), lsem.at[2 * i + h])
                   for i in range(n) for h in range(2)]
            return cps + [pltpu.make_async_copy(sin[i], sall[i].at[2 * x + y], lsem.at[2 * n + i])
                          for i in range(ns)]

        def small(px, py, j, i, landing):
            s = n_big + j * ns + i
            return cp(sin[i], sall[i].at[landing], s, (px, py, c))

        def first_hop():
            for lc in local():
                lc.start()
            for j, (px, py) in enumerate(chips):
                for i in range(ns):
                    small(px, py, j, i, 2 * x + y).start()
            for i in range(n):
                cp(own(i, 0), slot(i, x, y, c, 0), per * i, xnb).start()
                cp(own(i, 1), slot(i, x, y, c, 1), per * i + 1, ynb).start()
                for h in range(2):
                    cp(own(i, h), slot(i, x, y, c, h), per * i + 6 + 4 * x + 2 * y + h, sib).start()

        def second_hop():
            for lc in local():
                lc.wait()
            for i in range(n):
                cp(slot(i, 1 - x, y, c, 0), slot(i, 1 - x, y, c, 0), per * i, xnb).wait_recv()
                cp(slot(i, x, 1 - y, c, 1), slot(i, x, 1 - y, c, 1), per * i + 1, ynb).wait_recv()
                for j in range(2):
                    cp(slot(i, j, y, c, 0), slot(i, j, y, c, 0), per * i + 2 + j, ynb).start()
                    cp(slot(i, x, j, c, 1), slot(i, x, j, c, 1), per * i + 4 + j, xnb).start()
                to_sib(i, 1 - x, y, 0).start()
                to_sib(i, x, 1 - y, 1).start()

        def last_to_sibling():
            for i in range(n):
                for j in range(2):
                    cp(slot(i, j, 1 - y, c, 0), slot(i, j, 1 - y, c, 0), per * i + 2 + j, ynb).wait_recv()
                    cp(slot(i, 1 - x, j, c, 1), slot(i, 1 - x, j, c, 1), per * i + 4 + j, xnb).wait_recv()
                    to_sib(i, j, 1 - y, 0).start()
                    to_sib(i, 1 - x, j, 1).start()

        def finish():
            for i in range(n):
                for xx in range(2):
                    for yy in range(2):
                        for h in range(2):
                            s = per * i + 6 + 4 * xx + 2 * yy + h
                            cp(slot(i, xx, yy, 1 - c, h), slot(i, xx, yy, 1 - c, h), s, sib).wait_recv()
                            to_sib(i, xx, yy, h).wait_send()
                cp(own(i, 0), slot(i, x, y, c, 0), per * i, xnb).wait_send()
                cp(own(i, 1), slot(i, x, y, c, 1), per * i + 1, ynb).wait_send()
                for j in range(2):
                    cp(slot(i, j, y, c, 0), slot(i, j, y, c, 0), per * i + 2 + j, ynb).wait_send()
                    cp(slot(i, x, j, c, 1), slot(i, x, j, c, 1), per * i + 4 + j, xnb).wait_send()
            for j, (px, py) in enumerate(chips):
                for i in range(ns):
                    small(px, py, j, i, 2 * px + py).wait_recv()
                    small(px, py, j, i, 2 * x + y).wait_send()

        return [first_hop, second_hop, last_to_sibling, finish]

    out_shapes = [jax.ShapeDtypeStruct((2, 2, 2, 2, a.shape[0] // 2, a.shape[1]), a.dtype) for a in pieces]
    out_shapes += [jax.ShapeDtypeStruct((4,) + a.shape, a.dtype) for a in smalls]
    return _Carry(name, list(pieces) + list(smalls), out_shapes, n_big + 3 * ns, 2 * n + ns, make,
                  (0.0, 0.23, 0.73, 1.0))


def _exchange_carry(name, arrays, out_shapes, plan):
    count = plan([None] * len(arrays), [None] * len(out_shapes), None)

    def make(ins, outs, ssem, rsem, lsem):
        def copies():
            return [_remote(src, dst, ssem.at[j], rsem.at[j], peer)
                    for j, (src, dst, peer) in enumerate(plan(ins, outs, _position()))]

        def start():
            for c in copies():
                c.start()

        def wait():
            for c in copies():
                c.wait()

        return [start, wait]

    return _Carry(name, arrays, out_shapes, count, 0, make, (0.0, 1.0))


class _Grad:
    def __init__(self, arr, kind, shard_shape):
        self.arr, self.kind = arr, kind
        self.rows, self.cols = shard_shape
        self.r2 = self.rows // 2

    def view(self, ref, k, h):
        r2 = self.r2
        if self.kind == "stacked":
            return ref.at[k, pl.ds(h * r2, r2), :]
        if self.kind == "col":
            return ref.at[pl.ds(h * r2, r2), pl.ds(k * self.cols, self.cols)]
        return ref.at[pl.ds((2 * k + h) * r2, r2), :]

    def spec(self, kh):
        r2, cols = self.r2, self.cols
        if self.kind == "stacked":
            return pl.BlockSpec((None, r2, cols), lambda i, pos: (kh(i, pos)[0], kh(i, pos)[1], 0))
        if self.kind == "col":
            return pl.BlockSpec((r2, cols), lambda i, pos: (kh(i, pos)[1], kh(i, pos)[0]))
        return pl.BlockSpec((r2, cols), lambda i, pos: (2 * kh(i, pos)[0] + kh(i, pos)[1], 0))


def _add_stage(name, mine, mine_spec, recv, n_out, pos):
    r2, cols = recv.shape[-2:]

    def body(pos_ref, m_ref, r_ref, of_ref, ob_ref):
        s = m_ref[...] + r_ref[...].astype(F32)
        of_ref[...] = s
        ob_ref[...] = s.astype(BF16)

    blk = pl.BlockSpec((None, r2, cols), lambda i, pos: (i, 0, 0))
    return pl.pallas_call(
        body, name=name,
        grid_spec=pltpu.PrefetchScalarGridSpec(
            num_scalar_prefetch=1, grid=(n_out,), in_specs=[mine_spec, blk], out_specs=[blk, blk]),
        out_shape=[jax.ShapeDtypeStruct((n_out, r2, cols), F32), jax.ShapeDtypeStruct((n_out, r2, cols), BF16)],
        compiler_params=_params(("arbitrary",)),
    )(pos, mine, recv)


def _adamw_math(w, g, m, v):
    m = ADAM_B1 * m + (1.0 - ADAM_B1) * g
    v = ADAM_B2 * v + (1.0 - ADAM_B2) * (g * g)
    m_hat = m / (1.0 - ADAM_B1 ** ADAM_STEP)
    v_hat = v / (1.0 - ADAM_B2 ** ADAM_STEP)
    delta = -ADAM_LR * (m_hat / (jnp.sqrt(v_hat) + ADAM_EPS) + ADAM_WD * w)
    return delta, m, v


def _adamw_big(w, m, v, own, sib, pos):
    rows, cols = w.shape
    r2 = rows // 2

    def body(pos_ref, w_ref, m_ref, v_ref, own_ref, sib_ref, g_ref, d_ref, nm_ref, nv_ref):
        h = pl.program_id(0)
        g = jnp.where(h == pos_ref[2], own_ref[...], sib_ref[...])
        g_ref[...] = g
        d_ref[...], nm_ref[...], nv_ref[...] = _adamw_math(w_ref[...], g, m_ref[...], v_ref[...])

    half = pl.BlockSpec((r2, cols), lambda h, pos: (h, 0))
    whole = pl.BlockSpec((r2, cols), lambda h, pos: (0, 0))
    out = jax.ShapeDtypeStruct((rows, cols), F32)
    return pl.pallas_call(
        body, name="adamw",
        grid_spec=pltpu.PrefetchScalarGridSpec(
            num_scalar_prefetch=1, grid=(2,),
            in_specs=[half, half, half, whole, whole],
            out_specs=[half, half, half, half]),
        out_shape=[out, out, out, out],
        compiler_params=_params(("arbitrary",)),
    )(pos, w, m, v, own, sib)


def _add_hop1(s1f, recv, pos):
    _, _, r4, cols = recv.shape
    s1v = s1f.reshape(4, 2, r4, cols)

    def body(pos_ref, m_ref, r_ref, of_ref, ob_ref):
        s = m_ref[...] + r_ref[...].astype(F32)
        of_ref[...] = s
        ob_ref[...] = s.astype(BF16)

    def mine(h, j, pos):
        return (jnp.where(h == 0, 2 * j + pos[1], 2 * pos[0] + j), h, 0, 0)

    blk = pl.BlockSpec((None, None, r4, cols), lambda h, j, pos: (h, j, 0, 0))
    return pl.pallas_call(
        body, name="rs_add_1",
        grid_spec=pltpu.PrefetchScalarGridSpec(
            num_scalar_prefetch=1, grid=(2, 2),
            in_specs=[pl.BlockSpec((None, None, r4, cols), mine), blk], out_specs=[blk, blk]),
        out_shape=[jax.ShapeDtypeStruct((2, 2, r4, cols), F32), jax.ShapeDtypeStruct((2, 2, r4, cols), BF16)],
        compiler_params=_params(("arbitrary", "arbitrary")),
    )(pos, s1v, recv)


def _own_sum(s2f, recv3, pos):
    _, _, r4, cols = s2f.shape

    def body(pos_ref, s_ref, r_ref, o_ref):
        o_ref[...] = s_ref[...] + r_ref[...].astype(F32)

    blk = pl.BlockSpec((None, r4, cols), lambda h, pos: (h, 0, 0))
    return pl.pallas_call(
        body, name="own_sum",
        grid_spec=pltpu.PrefetchScalarGridSpec(
            num_scalar_prefetch=1, grid=(2,),
            in_specs=[pl.BlockSpec((None, None, r4, cols),
                                   lambda h, pos: (h, jnp.where(h == 0, pos[0], pos[1]), 0, 0)), blk],
            out_specs=blk),
        out_shape=jax.ShapeDtypeStruct((2, r4, cols), F32),
        compiler_params=_params(("arbitrary",)),
    )(pos, s2f, recv3)


def _allreduce_small(buf):
    def body(x_ref, o_ref, recv, ssem, rsem):
        x, y, c = _position()
        o_ref[...] = x_ref[...]
        for s, peer in enumerate([(x, y, 1 - c), (x, 1 - y, c), (1 - x, y, c)]):
            cp = _remote(o_ref, recv.at[s], ssem.at[s], rsem.at[s], peer)
            cp.start()
            cp.wait()
            o_ref[...] = o_ref[...] + recv[s]

    vm = pl.BlockSpec(memory_space=pltpu.VMEM)
    return pl.pallas_call(
        body, name="allreduce_small", in_specs=[vm], out_specs=vm,
        out_shape=jax.ShapeDtypeStruct(buf.shape, F32),
        scratch_shapes=[pltpu.VMEM((3,) + buf.shape, F32),
                        pltpu.SemaphoreType.DMA((3,)), pltpu.SemaphoreType.DMA((3,))],
    )(buf)


def _adamw_small(w, g, m, v):
    def body(w_ref, g_ref, m_ref, v_ref, d_ref, nm_ref, nv_ref):
        d_ref[...], nm_ref[...], nv_ref[...] = _adamw_math(w_ref[...], g_ref[...], m_ref[...], v_ref[...])

    vm = pl.BlockSpec(memory_space=pltpu.VMEM)
    out = jax.ShapeDtypeStruct(w.shape, F32)
    return pl.pallas_call(body, name="adamw_small", in_specs=[vm] * 4, out_specs=[vm] * 3,
                          out_shape=[out, out, out])(w, g, m, v)


class _ReduceScatter:
    def __init__(self, tag, grads, pos):
        self.tag, self.grads, self.pos, self.stage = tag, grads, pos, 0

    def carry(self):
        grads, n = self.grads, len(self.grads)
        r4 = [g.r2 // 2 for g in grads]

        def plan_c(ins, outs, p):
            if p is None:
                return 4 * n
            x, y, c = p
            return [(grads[i].view(ins[i], k, 1 - c), outs[i].at[k], (x, y, 1 - c))
                    for i in range(n) for k in range(4)]

        def plan_1(ins, outs, p):
            if p is None:
                return 4 * n
            x, y, c = p
            copies = []
            for i in range(n):
                for j in range(2):
                    copies.append((ins[i].at[2 * j + (1 - y), pl.ds(0, r4[i]), :], outs[i].at[0, j],
                                   (x, 1 - y, c)))
                    copies.append((ins[i].at[2 * (1 - x) + j, pl.ds(r4[i], r4[i]), :], outs[i].at[1, j],
                                   (1 - x, y, c)))
            return copies

        def plan_2(ins, outs, p):
            if p is None:
                return 2 * n
            x, y, c = p
            copies = []
            for i in range(n):
                copies.append((ins[i].at[0, 1 - x], outs[i].at[0], (1 - x, y, c)))
                copies.append((ins[i].at[1, 1 - y], outs[i].at[1], (x, 1 - y, c)))
            return copies

        def plan_s(ins, outs, p):
            if p is None:
                return n
            x, y, c = p
            return [(ins[i], outs[i], (x, y, 1 - c)) for i in range(n)]

        shape = lambda lead, dt: [jax.ShapeDtypeStruct(lead(g) + (g.cols,), dt) for g in grads]
        stage = self.stage
        if stage == 0:
            return _exchange_carry(f"rs_{self.tag}_exchange_c", [g.arr for g in grads],
                                   shape(lambda g: (4, g.r2), F32), plan_c)
        if stage == 1:
            return _exchange_carry(f"rs_{self.tag}_exchange_1", [s[1] for s in self.s1],
                                   shape(lambda g: (2, 2, g.r2 // 2), BF16), plan_1)
        if stage == 2:
            return _exchange_carry(f"rs_{self.tag}_exchange_2", [s[1] for s in self.s2],
                                   shape(lambda g: (2, g.r2 // 2), BF16), plan_2)
        return _exchange_carry(f"rs_{self.tag}_exchange_sibling", self.own, shape(lambda g: (g.r2,), F32), plan_s)

    def feed(self, recv):
        grads, pos = self.grads, self.pos
        if self.stage == 0:
            self.s1 = [_add_stage("rs_add_c", g.arr, g.spec(lambda i, p: (i, p[2])), r, 4, pos)
                       for g, r in zip(grads, recv)]
        elif self.stage == 1:
            self.s2 = [_add_hop1(s[0], r, pos) for s, r in zip(self.s1, recv)]
        elif self.stage == 2:
            self.own = [_own_sum(s[0], r, pos).reshape(g.r2, g.cols) for g, s, r in zip(grads, self.s2, recv)]
        else:
            self.sib = list(recv)
        self.stage += 1

    def run(self):
        while self.stage < 4:
            self.feed(_run_carry(self.carry()))

    def adamw(self, weights):
        return [_adamw_big(w, m, v, o, sb, self.pos) for (w, m, v), o, sb in zip(weights, self.own, self.sib)]


def _block_diag(t, nb):
    g, c, p = t.shape
    gb = g // nb
    t = t.reshape(nb, gb, c, p)
    eye = jnp.eye(gb, dtype=t.dtype)
    return jnp.einsum("bgcp,gh->bgchp", t, eye).reshape(nb, gb * c, gb * p)


def _s5_discretise(a_re, a_im, log_dt, b_re, b_im, c_re, c_im):
    g, p = a_re.shape
    nb = g // GROUPS_PER_BLOCK
    dt = jnp.exp(log_dt)[:, None]
    mag = jnp.exp(a_re * dt)
    lam_re = mag * jnp.cos(a_im * dt)
    lam_im = mag * jnp.sin(a_im * dt)
    den = a_re * a_re + a_im * a_im
    q_re = ((lam_re - 1.0) * a_re + lam_im * a_im) / den
    q_im = (lam_im * a_re - (lam_re - 1.0) * a_im) / den
    bb_re = q_re[..., None] * b_re - q_im[..., None] * b_im
    bb_im = q_re[..., None] * b_im + q_im[..., None] * b_re
    tr = lambda t: jnp.swapaxes(t, 1, 2)
    mb = jnp.concatenate([_block_diag(tr(bb_re), nb), _block_diag(tr(bb_im), nb)], axis=-1)
    mc = jnp.concatenate([_block_diag(c_re, nb), -_block_diag(c_im, nb)], axis=-1)
    lam = jnp.concatenate([lam_re.reshape(nb, -1), lam_im.reshape(nb, -1)], axis=-1)
    return mb, mc, lam


def _s5_powers(a_re, a_im, log_dt, sub):
    g, p = a_re.shape
    nb = g // GROUPS_PER_BLOCK
    dt = jnp.exp(log_dt)[:, None]
    ns = list(range(1, sub + 1)) + [2 * sub, 4 * sub]
    ns += [0] * (-len(ns) % SUBLANES)
    e = jnp.asarray(ns, F32)[:, None, None]
    mag = jnp.exp(a_re[None] * dt[None] * e)
    ang = a_im[None] * dt[None] * e
    re = (mag * jnp.cos(ang)).reshape(len(ns), nb, -1)
    im = (mag * jnp.sin(ang)).reshape(len(ns), nb, -1)
    return jnp.transpose(jnp.concatenate([re, im], axis=-1), (1, 0, 2))


def _pack(parts):
    flat = jnp.concatenate([a.reshape(-1).astype(F32) for a in parts])
    n = flat.shape[0]
    pad = -n % (SUBLANES * LANES)
    return jnp.pad(flat, (0, pad)).reshape(-1, LANES)


def _unpack(buf, like):
    flat = buf.reshape(-1)
    out, o = [], 0
    for a in like:
        out.append(flat[o:o + a.size].reshape(a.shape))
        o += a.size
    return out


def kernel(x, meta_tokens, g_ffn1, ffn1_w_gate, ffn1_w_up, ffn1_w_down, g_mix, w_in, b_gate, ssm_a_re, ssm_a_im, ssm_log_dt, ssm_b_re, ssm_b_im, ssm_c_re, ssm_c_im, ssm_d, ssm_w_glu, conv_w, conv_w_out, w_o, g_ffn2, ffn2_w_gate, ffn2_w_up, ffn2_w_down, g_final, loss_target, m_meta_tokens, m_g_ffn1, m_ffn1_w_gate, m_ffn1_w_up, m_ffn1_w_down, m_g_mix, m_w_in, m_b_gate, m_ssm_a_re, m_ssm_a_im, m_ssm_log_dt, m_ssm_b_re, m_ssm_b_im, m_ssm_c_re, m_ssm_c_im, m_ssm_d, m_ssm_w_glu, m_conv_w, m_conv_w_out, m_w_o, m_g_ffn2, m_ffn2_w_gate, m_ffn2_w_up, m_ffn2_w_down, m_g_final, v_meta_tokens, v_g_ffn1, v_ffn1_w_gate, v_ffn1_w_up, v_ffn1_w_down, v_g_mix, v_w_in, v_b_gate, v_ssm_a_re, v_ssm_a_im, v_ssm_log_dt, v_ssm_b_re, v_ssm_b_im, v_ssm_c_re, v_ssm_c_im, v_ssm_d, v_ssm_w_glu, v_conv_w, v_conv_w_out, v_w_o, v_g_ffn2, v_ffn2_w_gate, v_ffn2_w_up, v_ffn2_w_down, v_g_final):
    seq, d = x.shape[1], x.shape[2]
    n_meta = meta_tokens.shape[0]
    dh = d // 2
    tp = -(-(n_meta + seq) // ROW_ALIGN) * ROW_ALIGN
    mx, my, mc_ = _position()
    pos = jnp.stack([mx, my, mc_]).astype(jnp.int32)
    shard = 2 * mx + my

    big_names = ["ffn1_w_gate", "ffn1_w_up", "ffn1_w_down", "w_in", "ssm_w_glu", "conv_w_out", "w_o",
                 "ffn2_w_gate", "ffn2_w_up", "ffn2_w_down"]
    big_w = [ffn1_w_gate[0], ffn1_w_up[0], ffn1_w_down[0], w_in[0], ssm_w_glu[0], conv_w_out[0], w_o[0],
             ffn2_w_gate[0], ffn2_w_up[0], ffn2_w_down[0]]
    big_m = [m_ffn1_w_gate[0], m_ffn1_w_up[0], m_ffn1_w_down[0], m_w_in[0], m_ssm_w_glu[0], m_conv_w_out[0],
             m_w_o[0], m_ffn2_w_gate[0], m_ffn2_w_up[0], m_ffn2_w_down[0]]
    big_v = [v_ffn1_w_gate[0], v_ffn1_w_up[0], v_ffn1_w_down[0], v_w_in[0], v_ssm_w_glu[0], v_conv_w_out[0],
             v_w_o[0], v_ffn2_w_gate[0], v_ffn2_w_up[0], v_ffn2_w_down[0]]
    pieces = [_cast_piece(w, pos) for w in big_w]
    conv_local = conv_w.reshape(conv_w.shape[1], conv_w.shape[3])
    n_first = 3
    first = _run_carry(_allgather_carry("allgather_first", pieces[:n_first], [meta_tokens, conv_local]))
    smalls = first[n_first:]
    stack4 = lambda wl: wl.reshape((4, -1, wl.shape[-1]))
    w1g, w1u, w1d = [stack4(wl) for wl in first[:n_first]]
    natural_cols = lambda s: jnp.transpose(s, (1, 0, 2)).reshape(s.shape[1], 4 * s.shape[2])
    meta_full = natural_cols(smalls[0])
    cw_full = natural_cols(smalls[1])
    cw_pad = jnp.pad(cw_full, ((0, SUBLANES - cw_full.shape[0]), (0, 0)))

    s5_args = (ssm_a_re[0], ssm_a_im[0], ssm_log_dt[0], ssm_b_re[0], ssm_b_im[0], ssm_c_re[0], ssm_c_im[0])
    (mb, mc, _), disc_vjp = jax.vjp(_s5_discretise, *s5_args)
    powt = _s5_powers(ssm_a_re[0], ssm_a_im[0], ssm_log_dt[0], SCAN_TILE // SUBLANES)
    mb16, mc16 = mb.astype(BF16), mc.astype(BF16)

    pad_rows = tp - n_meta - seq
    h0 = jnp.concatenate([meta_full, x[0], jnp.zeros((pad_rows, d), F32)], axis=0)
    tgt = jnp.concatenate([jnp.zeros((n_meta, d), F32), loss_target[0], jnp.zeros((pad_rows, d), F32)], axis=0)
    h1, a1, b1, *rest = _ffn_fwd(h0, g_ffn1, w1g, w1u, w1d, "ffn1_fwd",
                                 carry=_allgather_carry("allgather_rest", pieces[n_first:], []))
    win_all, wglu_s, wco_s, wo_s, w2g, w2u, w2d = [stack4(wl) for wl in rest]
    wglu_all = natural_cols(wglu_s)
    wco_all = natural_cols(wco_s)
    wo_all = wo_s.reshape(d, d)
    u, p = _win_fwd(h1, g_mix, win_all)
    ys5, bnd = _scan_fwd(p, mb16, mc16, powt, ssm_d)
    h2 = _mix_fwd(h1, ys5, p, cw_pad, b_gate, wglu_all, wco_all, wo_all)
    dh3, a2, b2, dg_final, loss_part = _ffn_fwd(
        h2, g_ffn2, w2g, w2u, w2d, "ffn2_fwd_loss", final=(g_final.reshape(1, d), tgt, n_meta, seq))

    dh2, dw2g, dw2u, dw2d, dg_ffn2 = _ffn_bwd(dh3, h2, g_ffn2, a2, b2, w2g, w2u, w2d, "ffn2_bwd")
    dys5, dpb, dwo, dwglu, dwco, dcw, dbg = _mix_bwd(dh2, ys5, p, cw_pad, b_gate, wglu_all, wco_all, wo_all)
    dug, dmb, dmc, dlam, dd = _scan_bwd(p, dys5, mb16, mc16, powt, ssm_d, bnd)
    dh1, dwin, dg_mix = _win_bwd(dpb, dug, u, win_all, h1, g_mix, dh2)
    shapes = [w.shape for w in big_w]
    kinds = ["stacked", "stacked", "stacked", "stacked", "col", "col", "row", "stacked", "stacked", "stacked"]
    rest_grads = [dwin, dwglu, dwco, dwo, dw2g, dw2u, dw2d]
    rs_rest = _ReduceScatter("rest", [_Grad(a, k, s) for a, k, s in
                                      zip(rest_grads, kinds[n_first:], shapes[n_first:])], pos)
    dh0, dw1g, dw1u, dw1d, dg_ffn1 = _ffn_bwd(dh1, h0, g_ffn1, a1, b1, w1g, w1u, w1d, "ffn1_bwd", chain=rs_rest)
    rs_first = _ReduceScatter("first", [_Grad(a, k, s) for a, k, s in
                                        zip([dw1g, dw1u, dw1d], kinds[:n_first], shapes[:n_first])], pos)
    rs_first.run()
    wmv = list(zip(big_w, big_m, big_v))
    big_out = rs_first.adamw(wmv[:n_first]) + rs_rest.adamw(wmv[n_first:])
    big_out = {nme: tuple(o[None] for o in outs) for nme, outs in zip(big_names, big_out)}

    s5_grads = disc_vjp((dmb, dmc, jnp.sum(dlam, axis=1)))
    grad_x = dh0[n_meta:n_meta + seq][None]

    small_names = ["g_ffn1", "g_mix", "b_gate", "ssm_a_re", "ssm_a_im", "ssm_log_dt", "ssm_b_re", "ssm_b_im",
                   "ssm_c_re", "ssm_c_im", "ssm_d", "g_ffn2", "g_final", "meta_tokens", "conv_w"]
    small_w = [g_ffn1, g_mix, b_gate, ssm_a_re, ssm_a_im, ssm_log_dt, ssm_b_re, ssm_b_im, ssm_c_re, ssm_c_im,
               ssm_d, g_ffn2, g_final, meta_tokens, conv_w]
    small_m = [m_g_ffn1, m_g_mix, m_b_gate, m_ssm_a_re, m_ssm_a_im, m_ssm_log_dt, m_ssm_b_re, m_ssm_b_im,
               m_ssm_c_re, m_ssm_c_im, m_ssm_d, m_g_ffn2, m_g_final, m_meta_tokens, m_conv_w]
    small_v = [v_g_ffn1, v_g_mix, v_b_gate, v_ssm_a_re, v_ssm_a_im, v_ssm_log_dt, v_ssm_b_re, v_ssm_b_im,
               v_ssm_c_re, v_ssm_c_im, v_ssm_d, v_g_ffn2, v_g_final, v_meta_tokens, v_conv_w]
    local_small = [dg_ffn1, dg_mix, dbg, *s5_grads, jnp.sum(dd, axis=0), dg_ffn2, dg_final,
                   dh0[:n_meta], dcw[:conv_w.shape[1]]]
    reduced = _unpack(_allreduce_small(_pack(local_small)), local_small)
    reduced[-2] = lax.dynamic_slice_in_dim(reduced[-2], shard * meta_tokens.shape[1], meta_tokens.shape[1], 1)
    reduced[-1] = lax.dynamic_slice_in_dim(reduced[-1], shard * conv_w.shape[3], conv_w.shape[3], 1)
    small_g = [r.reshape(w.shape) for r, w in zip(reduced, small_w)]
    ds_, nm_, nv_ = _adamw_small(_pack(small_w), _pack(small_g), _pack(small_m), _pack(small_v))
    small_out = {nme: o for nme, o in zip(
        small_names, zip(small_g, _unpack(ds_, small_w), _unpack(nm_, small_w), _unpack(nv_, small_w)))}

    loss = lax.psum(loss_part[0, 0], ("x", "y", "c"))
    order = ["meta_tokens", "g_ffn1", "ffn1_w_gate", "ffn1_w_up", "ffn1_w_down", "g_mix", "w_in", "b_gate",
             "ssm_a_re", "ssm_a_im", "ssm_log_dt", "ssm_b_re", "ssm_b_im", "ssm_c_re", "ssm_c_im", "ssm_d",
             "ssm_w_glu", "conv_w", "conv_w_out", "w_o", "g_ffn2", "ffn2_w_gate", "ffn2_w_up", "ffn2_w_down",
             "g_final"]
    res = {**big_out, **small_out}
    return (loss, grad_x, *[res[nme][0] for nme in order], *[res[nme][1] for nme in order],
            *[res[nme][2] for nme in order], *[res[nme][3] for nme in order])
```

```python
import functools
import math

import jax
import jax.numpy as jnp
from jax import lax
from jax.experimental import pallas as pl
from jax.experimental.pallas import tpu as pltpu

F32 = jnp.float32
BF16 = jnp.bfloat16
MESH = pl.DeviceIdType.MESH

RMS_EPS = 1e-6
ADAM_LR = 0.001
ADAM_B1 = 0.9
ADAM_B2 = 0.999
ADAM_EPS = 1e-08
ADAM_WD = 0.01
ADAM_STEP = 10

LANES = 128
SUBLANES = 8
VMEM_LIMIT = 56 * 1024 * 1024

ROW_ALIGN = 256
SCAN_TILE = 256
GROUPS_PER_BLOCK = 8


def _params(sem, vmem=VMEM_LIMIT):
    return pltpu.CompilerParams(dimension_semantics=sem, vmem_limit_bytes=vmem)


def _pick_tile(n, candidates):
    for c in candidates:
        if n % c == 0:
            return c
    raise ValueError(f"no tile for {n}")


def _dot(a, b):
    return jnp.dot(a, b, preferred_element_type=F32)


def _dot_nt(a, b):
    return lax.dot_general(a, b, (((1,), (1,)), ((), ())), preferred_element_type=F32)


def _dot_tn(a, b):
    return lax.dot_general(a, b, (((0,), (0,)), ((), ())), preferred_element_type=F32)


def _sigmoid(x):
    return 1.0 / (1.0 + jnp.exp(-x))


def _rms_stats(h):
    r = lax.rsqrt(jnp.mean(h * h, axis=-1, keepdims=True) + RMS_EPS)
    return h * r, r


def _rms_bwd(xhat, r, g, dn):
    dxh = dn * g
    return r * (dxh - xhat * jnp.mean(dxh * xhat, axis=-1, keepdims=True))


GELU_K = math.sqrt(2.0 / math.pi)
GELU_C = 0.044715


def _gelu(x):
    return 0.5 * x * (1.0 + jnp.tanh(GELU_K * (x + GELU_C * x * x * x)))


def _gelu_grad(x):
    t = jnp.tanh(GELU_K * (x + GELU_C * x * x * x))
    return 0.5 * (1.0 + t) + 0.5 * x * (1.0 - t * t) * GELU_K * (1.0 + 3.0 * GELU_C * x * x)


def _ffn_fwd(h, g, wg, wu, wd, name, final=None, carry=None):
    tp, d = h.shape
    ns, _, f4 = wg.shape
    tm = _pick_tile(tp, (768, 512, 256))
    ni = tp // tm

    def body(*refs):
        refs, phases = split(refs)
        if final is None:
            h_ref, g_ref, wg_ref, wu_ref, wd_ref, ho_ref, a_ref, b_ref, n_scr, acc = refs
        else:
            (h_ref, g_ref, wg_ref, wu_ref, wd_ref, gf_ref, tg_ref,
             ho_ref, a_ref, b_ref, dgf_ref, loss_ref, n_scr, acc) = refs
        i = pl.program_id(0)
        k = pl.program_id(1)
        _run_phases(phases, carry, i * ns + k, ni * ns)

        @pl.when(k == 0)
        def _():
            xhat, _ = _rms_stats(h_ref[...])
            n_scr[...] = (xhat * g_ref[...]).astype(BF16)
            acc[...] = jnp.zeros_like(acc)

        n = n_scr[...]
        a = _dot(n, wg_ref[...])
        b = _dot(n, wu_ref[...])
        a_ref[...] = a.astype(BF16)
        b_ref[...] = b.astype(BF16)
        s = (a * _sigmoid(a) * b).astype(BF16)
        acc[...] += _dot(s, wd_ref[...])

        if final is None:
            @pl.when(k == ns - 1)
            def _():
                ho_ref[...] = h_ref[...] + 0.5 * acc[...]
        else:
            n_meta, seq = final[2], final[3]

            @pl.when((i == 0) & (k == 0))
            def _():
                dgf_ref[...] = jnp.zeros_like(dgf_ref)
                loss_ref[...] = jnp.zeros_like(loss_ref)

            @pl.when(k == ns - 1)
            def _():
                h3 = h_ref[...] + 0.5 * acc[...]
                xhat, r = _rms_stats(h3)
                gf = gf_ref[...]
                row = i * tm + lax.broadcasted_iota(jnp.int32, (tm, d), 0)
                valid = (row >= n_meta) & (row < n_meta + seq)
                diff = jnp.where(valid, xhat * gf - tg_ref[...], 0.0)
                dout = diff * (1.0 / d)
                loss_ref[...] += jnp.full(loss_ref.shape, 0.5 * jnp.sum(diff * diff) * (1.0 / d), F32)
                dgf_ref[...] += jnp.sum(dout * xhat, axis=0, keepdims=True)
                ho_ref[...] = _rms_bwd(xhat, r, gf, dout)

    row_spec = pl.BlockSpec((tm, d), lambda i, k: (i, 0))
    vec_spec = pl.BlockSpec((1, d), lambda i, k: (0, 0))
    in_specs = [row_spec, vec_spec,
                pl.BlockSpec((None, d, f4), lambda i, k: (k, 0, 0)),
                pl.BlockSpec((None, d, f4), lambda i, k: (k, 0, 0)),
                pl.BlockSpec((None, f4, d), lambda i, k: (k, 0, 0))]
    act_spec = pl.BlockSpec((None, tm, f4), lambda i, k: (k, i, 0))
    out_specs = [row_spec, act_spec, act_spec]
    out_shape = [jax.ShapeDtypeStruct((tp, d), F32),
                 jax.ShapeDtypeStruct((ns, tp, f4), BF16),
                 jax.ShapeDtypeStruct((ns, tp, f4), BF16)]
    args = [h, g, wg, wu, wd]
    if final is not None:
        in_specs += [vec_spec, row_spec]
        args += [final[0], final[1]]
        out_specs += [vec_spec, pl.BlockSpec((1, LANES), lambda i, k: (0, 0))]
        out_shape += [jax.ShapeDtypeStruct((1, d), F32), jax.ShapeDtypeStruct((1, LANES), F32)]
    scratch = [pltpu.VMEM((tm, d), BF16), pltpu.VMEM((tm, d), F32)]
    split = _attach_carry(carry, in_specs, args, out_specs, out_shape, scratch)
    return pl.pallas_call(
        body, name=name, grid=(ni, ns), in_specs=in_specs, out_specs=out_specs, out_shape=out_shape,
        scratch_shapes=scratch, compiler_params=_params(("arbitrary", "arbitrary")),
    )(*args)


def _ffn_bwd_shard(k, ns, dn_prev, dh_out, h_in, g, a, b, wg, wu, wd, name, carry=None):
    tp, d = h_in.shape
    f4 = wg.shape[2]
    tm = ROW_ALIGN
    ni = tp // tm
    first, last = k == 0, k == ns - 1

    def body(*refs):
        refs, phases = split(refs)
        acc_in = None if first else refs.pop(0)
        dh_ref, h_ref, g_ref, a_ref, b_ref, wg_ref, wu_ref, wd_ref, acc_out, dwg_ref, dwu_ref, dwd_ref = refs[:12]
        i = pl.program_id(0)
        _run_phases(phases, carry, i, ni)

        @pl.when(i == 0)
        def _():
            dwg_ref[...] = jnp.zeros_like(dwg_ref)
            dwu_ref[...] = jnp.zeros_like(dwu_ref)
            dwd_ref[...] = jnp.zeros_like(dwd_ref)
            if last:
                refs[12][...] = jnp.zeros_like(refs[12])

        xhat, r = _rms_stats(h_ref[...])
        gv = g_ref[...]
        n = (xhat * gv).astype(BF16)
        dy = (0.5 * dh_ref[...]).astype(BF16)
        av = a_ref[...].astype(F32)
        bv = b_ref[...].astype(F32)
        sg = _sigmoid(av)
        silu = av * sg
        ds = _dot_nt(dy, wd_ref[...])
        da = (ds * bv * (sg * (1.0 + av * (1.0 - sg)))).astype(BF16)
        db = (ds * silu).astype(BF16)
        s = (silu * bv).astype(BF16)
        dwd_ref[...] += _dot_tn(s, dy)
        dwg_ref[...] += _dot_tn(n, da)
        dwu_ref[...] += _dot_tn(n, db)
        dn = _dot_nt(da, wg_ref[...]) + _dot_nt(db, wu_ref[...])
        if not first:
            dn = dn + acc_in[...]
        if last:
            refs[12][...] += jnp.sum(dn * xhat, axis=0, keepdims=True)
            acc_out[...] = dh_ref[...] + _rms_bwd(xhat, r, gv, dn)
        else:
            acc_out[...] = dn

    row_spec = pl.BlockSpec((tm, d), lambda i: (i, 0))
    vec_spec = pl.BlockSpec((1, d), lambda i: (0, 0))
    act_spec = pl.BlockSpec((None, tm, f4), lambda i: (k, i, 0))
    wcol = pl.BlockSpec((None, d, f4), lambda i: (k, 0, 0))
    wrow = pl.BlockSpec((None, f4, d), lambda i: (k, 0, 0))
    whole = lambda shape: pl.BlockSpec(shape, lambda i: (0, 0))
    in_specs = [row_spec, row_spec, vec_spec, act_spec, act_spec, wcol, wcol, wrow]
    args = [dh_out, h_in, g, a, b, wg, wu, wd]
    if not first:
        in_specs.insert(0, row_spec)
        args.insert(0, dn_prev)
    out_specs = [row_spec, whole((d, f4)), whole((d, f4)), whole((f4, d))]
    out_shape = [jax.ShapeDtypeStruct((tp, d), F32), jax.ShapeDtypeStruct((d, f4), F32),
                 jax.ShapeDtypeStruct((d, f4), F32), jax.ShapeDtypeStruct((f4, d), F32)]
    if last:
        out_specs.append(vec_spec)
        out_shape.append(jax.ShapeDtypeStruct((1, d), F32))
    n_host = len(out_shape)
    scratch = []
    split = _attach_carry(carry, in_specs, args, out_specs, out_shape, scratch)
    outs = pl.pallas_call(
        body, name=f"{name}_{k}", grid=(ni,), in_specs=in_specs, out_specs=out_specs, out_shape=out_shape,
        scratch_shapes=scratch, compiler_params=_params(("arbitrary",)),
    )(*args)
    return outs[:n_host], outs[n_host:]


def _ffn_bwd(dh_out, h_in, g, a, b, wg, wu, wd, name, chain=None):
    ns = wg.shape[0]
    acc, dwg, dwu, dwd, dg = None, [], [], [], None
    for k in range(ns):
        carry = chain.carry() if chain is not None else None
        outs, carried = _ffn_bwd_shard(k, ns, acc, dh_out, h_in, g, a, b, wg, wu, wd, name, carry)
        if chain is not None:
            chain.feed(carried)
        acc = outs[0]
        dwg.append(outs[1])
        dwu.append(outs[2])
        dwd.append(outs[3])
        if k == ns - 1:
            dg = outs[4]
    return acc, dwg, dwu, dwd, dg


def _win_fwd(h, g, w_in):
    tp, d = h.shape
    ns = w_in.shape[0]
    tm = _pick_tile(tp, (768, 512, 256))

    def body(h_ref, g_ref, w_ref, u_ref, p_ref):
        @pl.when(pl.program_id(1) == 0)
        def _():
            xhat, _ = _rms_stats(h_ref[...])
            u_ref[...] = (xhat * g_ref[...]).astype(BF16)

        p_ref[...] = _dot(u_ref[...], w_ref[...]).astype(BF16)

    return pl.pallas_call(
        body, name="win_fwd", grid=(tp // tm, ns),
        in_specs=[pl.BlockSpec((tm, d), lambda i, k: (i, 0)),
                  pl.BlockSpec((1, d), lambda i, k: (0, 0)),
                  pl.BlockSpec((None, d, d), lambda i, k: (k, 0, 0))],
        out_specs=[pl.BlockSpec((tm, d), lambda i, k: (i, 0)),
                   pl.BlockSpec((None, tm, d), lambda i, k: (k, i, 0))],
        out_shape=[jax.ShapeDtypeStruct((tp, d), BF16), jax.ShapeDtypeStruct((ns, tp, d), BF16)],
        compiler_params=_params(("arbitrary", "arbitrary")),
    )(h, g, w_in)


def _win_bwd_shard(k, ns, du_prev, dpb, dug, u, w_in, h1, g, dh2):
    tp, d = h1.shape
    dh = d // 2
    tm = ROW_ALIGN
    first, last = k == 0, k == ns - 1

    def body(*refs):
        refs = list(refs)
        acc_in = None if first else refs.pop(0)
        dug_ref = refs.pop(0) if first else None
        dp_ref, u_ref, w_ref = refs[:3]
        refs = refs[3:]
        if last:
            h_ref, g_ref, dh2_ref, acc_out, dw_ref, dg_ref = refs
        else:
            acc_out, dw_ref = refs
        i = pl.program_id(0)

        @pl.when(i == 0)
        def _():
            dw_ref[...] = jnp.zeros_like(dw_ref)
            if last:
                dg_ref[...] = jnp.zeros_like(dg_ref)

        dp = dp_ref[...]
        if first:
            dp = jnp.concatenate([dug_ref[...], dp[:, dh:]], axis=1)
        dw_ref[...] += _dot_tn(u_ref[...], dp)
        du = _dot_nt(dp, w_ref[...])
        if not first:
            du = du + acc_in[...]
        if last:
            xhat, r = _rms_stats(h_ref[...])
            dg_ref[...] += jnp.sum(du * xhat, axis=0, keepdims=True)
            acc_out[...] = dh2_ref[...] + _rms_bwd(xhat, r, g_ref[...], du)
        else:
            acc_out[...] = du

    row_spec = pl.BlockSpec((tm, d), lambda i: (i, 0))
    vec_spec = pl.BlockSpec((1, d), lambda i: (0, 0))
    in_specs = [pl.BlockSpec((None, tm, d), lambda i: (k, i, 0)), row_spec,
                pl.BlockSpec((None, d, d), lambda i: (k, 0, 0))]
    args = [dpb, u, w_in]
    if first:
        in_specs.insert(0, pl.BlockSpec((tm, dh), lambda i: (i, 0)))
        args.insert(0, dug)
    else:
        in_specs.insert(0, row_spec)
        args.insert(0, du_prev)
    out_specs = [row_spec, pl.BlockSpec((d, d), lambda i: (0, 0))]
    out_shape = [jax.ShapeDtypeStruct((tp, d), F32), jax.ShapeDtypeStruct((d, d), F32)]
    if last:
        in_specs += [row_spec, vec_spec, row_spec]
        args += [h1, g, dh2]
        out_specs.append(vec_spec)
        out_shape.append(jax.ShapeDtypeStruct((1, d), F32))
    return pl.pallas_call(
        body, name=f"win_bwd_{k}", grid=(tp // tm,), in_specs=in_specs, out_specs=out_specs,
        out_shape=out_shape, compiler_params=_params(("arbitrary",)),
    )(*args)


def _win_bwd(dpb, dug, u, w_in, h1, g, dh2):
    ns = w_in.shape[0]
    acc, dws, dg = None, [], None
    for k in range(ns):
        outs = _win_bwd_shard(k, ns, acc, dpb, dug, u, w_in, h1, g, dh2)
        acc = outs[0]
        dws.append(outs[1])
        if k == ns - 1:
            dg = outs[2]
    return acc, dws, dg


def _cmul(ar, ai, br, bi):
    return ar * br - ai * bi, ar * bi + ai * br


def _scan_rows(j, sub):
    return pl.ds(j * SUBLANES, SUBLANES)


def _permute_rows(src_ref, dst_ref, sub):
    for j in range(sub):
        dst_ref[pl.ds(j * SUBLANES, SUBLANES), :] = src_ref[pl.ds(j, SUBLANES, stride=sub), :]


def _unpermute_rows(src_ref, dst_ref, sub):
    for j in range(sub):
        dst_ref[pl.ds(j, SUBLANES, stride=sub), :] = src_ref[pl.ds(j * SUBLANES, SUBLANES), :]


def _local_scan(x_ref, lr, li, w, sub, reverse):
    hr = jnp.zeros((SUBLANES, w), F32)
    hi = jnp.zeros((SUBLANES, w), F32)
    order = range(sub - 1, -1, -1) if reverse else range(sub)
    for j in order:
        xr = x_ref[_scan_rows(j, sub), pl.ds(0, w)]
        xi = x_ref[_scan_rows(j, sub), pl.ds(w, w)]
        if reverse:
            hr, hi = lr * hr + li * hi + xr, lr * hi - li * hr + xi
        else:
            hr, hi = lr * hr - li * hi + xr, lr * hi + li * hr + xi
        x_ref[_scan_rows(j, sub), pl.ds(0, w)] = hr
        x_ref[_scan_rows(j, sub), pl.ds(w, w)] = hi
    return hr, hi


def _entering_states(er, ei, fr, fi, pow_ref, w, sub, reverse):
    lane = lax.broadcasted_iota(jnp.int32, (SUBLANES, w), 0)
    if reverse:
        edge, shift1 = SUBLANES - 1, SUBLANES - 1
    else:
        edge, shift1 = 0, 1
    zr = jnp.where(lane == edge, pltpu.roll(fr, shift1, 0), pltpu.roll(er, shift1, 0))
    zi = jnp.where(lane == edge, pltpu.roll(fi, shift1, 0), pltpu.roll(ei, shift1, 0))
    for step, row in ((1, sub - 1), (2, sub), (4, sub + 1)):
        ar = pow_ref[pl.ds(row, 1), pl.ds(0, w)]
        ai = pow_ref[pl.ds(row, 1), pl.ds(w, w)]
        if reverse:
            ai = -ai
            keep = lane < SUBLANES - step
            sr = jnp.where(keep, pltpu.roll(zr, SUBLANES - step, 0), 0.0)
            si = jnp.where(keep, pltpu.roll(zi, SUBLANES - step, 0), 0.0)
        else:
            keep = lane >= step
            sr = jnp.where(keep, pltpu.roll(zr, step, 0), 0.0)
            si = jnp.where(keep, pltpu.roll(zi, step, 0), 0.0)
        pr, pi = _cmul(ar, ai, sr, si)
        zr, zi = zr + pr, zi + pi
    ar = pow_ref[pl.ds(sub - 1, 1), pl.ds(0, w)]
    ai = pow_ref[pl.ds(sub - 1, 1), pl.ds(w, w)]
    if reverse:
        ai = -ai
    pr, pi = _cmul(ar, ai, zr, zi)
    return zr, zi, er + pr, ei + pi


def _scan_fwd(p, mb, mc, powt, dskip, ride=None):
    _, tp, d = p.shape
    nb, cb, w2 = mb.shape
    w = w2 // 2
    q = SCAN_TILE
    sub = q // SUBLANES
    nt = tp // q
    ds = d // 2

    def body(*refs):
        refs, phases = split(refs)
        ug_ref, mb_ref, mc_ref, pow_ref, d_ref, y_ref, bnd_ref, x_scr, carry, nat, perm = refs
        t = pl.program_id(1)
        _run_phases(phases, ride, pl.program_id(0) * nt + t, nb * nt)

        @pl.when(t == 0)
        def _():
            carry[...] = jnp.zeros_like(carry)

        ugf = ug_ref[...].astype(F32)
        nat[...] = ugf
        _permute_rows(nat, perm, sub)
        x_scr[...] = _dot(perm[...].astype(BF16), mb_ref[...])
        lr = jnp.broadcast_to(pow_ref[pl.ds(0, 1), pl.ds(0, w)], (SUBLANES, w))
        li = jnp.broadcast_to(pow_ref[pl.ds(0, 1), pl.ds(w, w)], (SUBLANES, w))
        er, ei = _local_scan(x_scr, lr, li, w, sub, False)
        zr, zi, fr, fi = _entering_states(er, ei, carry[:, pl.ds(0, w)], carry[:, pl.ds(w, w)],
                                          pow_ref, w, sub, False)
        carry[:, pl.ds(0, w)] = fr
        carry[:, pl.ds(w, w)] = fi
        bnd_ref[:, pl.ds(0, w)] = fr
        bnd_ref[:, pl.ds(w, w)] = fi
        for j in range(sub):
            pr = pow_ref[pl.ds(j, 1), pl.ds(0, w)]
            pi = pow_ref[pl.ds(j, 1), pl.ds(w, w)]
            cr, ci = _cmul(pr, pi, zr, zi)
            x_scr[_scan_rows(j, sub), pl.ds(0, w)] += cr
            x_scr[_scan_rows(j, sub), pl.ds(w, w)] += ci
        hb = x_scr[...].astype(BF16)
        perm[...] = _dot_nt(hb, mc_ref[...])
        _unpermute_rows(perm, nat, sub)
        y_ref[...] = nat[...] + d_ref[...] * ugf

    in_specs = [pl.BlockSpec((None, q, cb), lambda b, t: (0, t, b)),
                pl.BlockSpec((None, cb, w2), lambda b, t: (b, 0, 0)),
                pl.BlockSpec((None, cb, w2), lambda b, t: (b, 0, 0)),
                pl.BlockSpec((None, powt.shape[1], w2), lambda b, t: (b, 0, 0)),
                pl.BlockSpec((1, cb), lambda b, t: (0, b))]
    out_specs = [pl.BlockSpec((q, cb), lambda b, t: (t, b)),
                 pl.BlockSpec((None, None, SUBLANES, w2), lambda b, t: (b, t, 0, 0))]
    out_shape = [jax.ShapeDtypeStruct((tp, ds), F32), jax.ShapeDtypeStruct((nb, nt, SUBLANES, w2), F32)]
    scratch = [pltpu.VMEM((q, w2), F32), pltpu.VMEM((SUBLANES, w2), F32),
               pltpu.VMEM((q, cb), F32), pltpu.VMEM((q, cb), F32)]
    args = [p, mb, mc, powt, dskip]
    split = _attach_carry(ride, in_specs, args, out_specs, out_shape, scratch)
    return pl.pallas_call(
        body, name="s5_scan_fwd", grid=(nb, nt), in_specs=in_specs, out_specs=out_specs,
        out_shape=out_shape, scratch_shapes=scratch, compiler_params=_params(("arbitrary", "arbitrary")),
    )(*args)


def _scan_bwd(p, dy, mb, mc, powt, dskip, bnd):
    _, tp, d = p.shape
    nb, cb, w2 = mb.shape
    w = w2 // 2
    q = SCAN_TILE
    sub = q // SUBLANES
    nt = tp // q
    ds = d // 2

    def body(ug_ref, dy_ref, mb_ref, mc_ref, pow_ref, d_ref, bnd_ref,
             dug_ref, dmb_ref, dmc_ref, dlam_ref, dd_ref, x_scr, y_scr, gcarry, nat, perm):
        t = pl.program_id(1)
        tt = nt - 1 - t

        @pl.when(t == 0)
        def _():
            gcarry[...] = jnp.zeros_like(gcarry)
            dmb_ref[...] = jnp.zeros_like(dmb_ref)
            dmc_ref[...] = jnp.zeros_like(dmc_ref)
            dlam_ref[...] = jnp.zeros_like(dlam_ref)
            dd_ref[...] = jnp.zeros_like(dd_ref)

        ugf = ug_ref[...].astype(F32)
        dyf = dy_ref[...].astype(F32)
        dd_ref[...] += jnp.sum((dyf * ugf).reshape(q // SUBLANES, SUBLANES, cb), axis=0)
        nat[...] = ugf
        _permute_rows(nat, perm, sub)
        ug = perm[...].astype(BF16)
        nat[...] = dyf
        _permute_rows(nat, perm, sub)
        dyb = perm[...].astype(BF16)
        lr = jnp.broadcast_to(pow_ref[pl.ds(0, 1), pl.ds(0, w)], (SUBLANES, w))
        li = jnp.broadcast_to(pow_ref[pl.ds(0, 1), pl.ds(w, w)], (SUBLANES, w))

        x_scr[...] = _dot(ug, mb_ref[...])
        er, ei = _local_scan(x_scr, lr, li, w, sub, False)
        first = tt == 0
        pfr = jnp.where(first, 0.0, bnd_ref[:, pl.ds(0, w)])
        pfi = jnp.where(first, 0.0, bnd_ref[:, pl.ds(w, w)])
        hzr, hzi, _, _ = _entering_states(er, ei, pfr, pfi, pow_ref, w, sub, False)
        for j in range(sub):
            pr = pow_ref[pl.ds(j, 1), pl.ds(0, w)]
            pi = pow_ref[pl.ds(j, 1), pl.ds(w, w)]
            cr, ci = _cmul(pr, pi, hzr, hzi)
            x_scr[_scan_rows(j, sub), pl.ds(0, w)] += cr
            x_scr[_scan_rows(j, sub), pl.ds(w, w)] += ci

        y_scr[...] = _dot(dyb, mc_ref[...])
        er, ei = _local_scan(y_scr, lr, li, w, sub, True)
        gzr, gzi, fr, fi = _entering_states(er, ei, gcarry[:, pl.ds(0, w)], gcarry[:, pl.ds(w, w)],
                                            pow_ref, w, sub, True)
        gcarry[:, pl.ds(0, w)] = fr
        gcarry[:, pl.ds(w, w)] = fi
        accr = jnp.zeros((SUBLANES, w), F32)
        acci = jnp.zeros((SUBLANES, w), F32)
        for j in range(sub):
            pr = pow_ref[pl.ds(sub - 1 - j, 1), pl.ds(0, w)]
            pi = pow_ref[pl.ds(sub - 1 - j, 1), pl.ds(w, w)]
            cr, ci = _cmul(pr, -pi, gzr, gzi)
            gr = y_scr[_scan_rows(j, sub), pl.ds(0, w)] + cr
            gi = y_scr[_scan_rows(j, sub), pl.ds(w, w)] + ci
            y_scr[_scan_rows(j, sub), pl.ds(0, w)] = gr
            y_scr[_scan_rows(j, sub), pl.ds(w, w)] = gi
            if j == 0:
                hpr, hpi = hzr, hzi
            else:
                hpr = x_scr[_scan_rows(j - 1, sub), pl.ds(0, w)]
                hpi = x_scr[_scan_rows(j - 1, sub), pl.ds(w, w)]
            accr += hpr * gr + hpi * gi
            acci += hpr * gi - hpi * gr
        dlam_ref[:, pl.ds(0, w)] += accr
        dlam_ref[:, pl.ds(w, w)] += acci

        hb = x_scr[...].astype(BF16)
        gb = y_scr[...].astype(BF16)
        dmc_ref[...] += _dot_tn(dyb, hb)
        dmb_ref[...] += _dot_tn(ug, gb)
        perm[...] = _dot_nt(gb, mb_ref[...])
        _unpermute_rows(perm, nat, sub)
        dug_ref[...] = (nat[...] + d_ref[...] * dyf).astype(BF16)

    blk = lambda b, t: (b, 0, 0)
    return pl.pallas_call(
        body, name="s5_scan_bwd", grid=(nb, nt),
        in_specs=[pl.BlockSpec((None, q, cb), lambda b, t: (0, nt - 1 - t, b)),
                  pl.BlockSpec((q, cb), lambda b, t: (nt - 1 - t, b)),
                  pl.BlockSpec((None, cb, w2), blk),
                  pl.BlockSpec((None, cb, w2), blk),
                  pl.BlockSpec((None, powt.shape[1], w2), blk),
                  pl.BlockSpec((1, cb), lambda b, t: (0, b)),
                  pl.BlockSpec((None, None, SUBLANES, w2),
                               lambda b, t: (b, jnp.maximum(nt - 2 - t, 0), 0, 0))],
        out_specs=[pl.BlockSpec((q, cb), lambda b, t: (nt - 1 - t, b)),
                   pl.BlockSpec((None, cb, w2), blk),
                   pl.BlockSpec((None, cb, w2), blk),
                   pl.BlockSpec((None, SUBLANES, w2), blk),
                   pl.BlockSpec((SUBLANES, cb), lambda b, t: (0, b))],
        out_shape=[jax.ShapeDtypeStruct((tp, ds), BF16),
                   jax.ShapeDtypeStruct((nb, cb, w2), F32),
                   jax.ShapeDtypeStruct((nb, cb, w2), F32),
                   jax.ShapeDtypeStruct((nb, SUBLANES, w2), F32),
                   jax.ShapeDtypeStruct((SUBLANES, ds), F32)],
        scratch_shapes=[pltpu.VMEM((q, w2), F32), pltpu.VMEM((q, w2), F32),
                        pltpu.VMEM((SUBLANES, w2), F32), pltpu.VMEM((q, cb), F32), pltpu.VMEM((q, cb), F32)],
        compiler_params=_params(("arbitrary", "arbitrary")),
    )(p, dy, mb, mc, powt, dskip, bnd)


HALO = 16


def _mix_tile(ys5, p0, p1, p2, p3, prev_cin, cw, bgate, wglu, wco, d):
    dh = d // 2
    tm = ys5.shape[0]
    v = p0[:, dh:].astype(F32)
    gbr = p1[:, :dh].astype(F32)
    gcr = p1[:, dh:].astype(F32)
    gact = _gelu(ys5).astype(BF16)
    z = _dot(gact, wglu)
    z1, z2 = z[:, :d], z[:, d:]
    sg = _sigmoid(z2)
    y_ssm = z1 * sg
    cin = gcr * v
    ext = jnp.concatenate([cin, prev_cin], axis=0)
    r1 = pltpu.roll(ext, 1, 0)[:tm]
    r2 = pltpu.roll(ext, 2, 0)[:tm]
    cv = cw[2] * cin + cw[1] * r1 + cw[0] * r2
    cg = (gbr * cv).astype(BF16)
    y_conv = _dot(cg, wco)
    g_s = _sigmoid(p2.astype(F32) + bgate[:, :d])
    g_c = _sigmoid(p3.astype(F32) + bgate[:, d:])
    mixed = g_s * y_ssm + g_c * y_conv
    return dict(v=v, gb=gbr, gc=gcr, gact=gact, z1=z1, sg=sg, y_ssm=y_ssm, cin=cin, r1=r1, r2=r2,
                cv=cv, cg=cg, y_conv=y_conv, g_s=g_s, g_c=g_c, mixed=mixed)


def _mix_fwd(h1, ys5, p, cw, bgate, wglu, wco, wo):
    tp, d = h1.shape
    dh = d // 2
    tm = ROW_ALIGN

    def body(h_ref, y_ref, p0_ref, p1_ref, p2_ref, p3_ref, cw_ref, bg_ref, wglu_ref, wco_ref, wo_ref,
             o_ref, prev):
        @pl.when(pl.program_id(0) == 0)
        def _():
            prev[...] = jnp.zeros_like(prev)

        cw = [cw_ref[pl.ds(t, 1), :] for t in range(3)]
        f = _mix_tile(y_ref[...], p0_ref[...], p1_ref[...], p2_ref[...], p3_ref[...], prev[...],
                      cw, bg_ref[...], wglu_ref[...], wco_ref[...], d)
        prev[...] = f["cin"][tm - HALO:, :]
        o_ref[...] = h_ref[...] + _dot(f["mixed"].astype(BF16), wo_ref[...])

    row = pl.BlockSpec((tm, d), lambda i: (i, 0))
    full = lambda a: pl.BlockSpec(a.shape, lambda i: (0,) * a.ndim)
    pk = lambda k: pl.BlockSpec((None, tm, d), lambda i, k=k: (k, i, 0))
    return pl.pallas_call(
        body, name="mix_fwd", grid=(tp // tm,),
        in_specs=[row, pl.BlockSpec((tm, dh), lambda i: (i, 0)), pk(0), pk(1), pk(2), pk(3),
                  full(cw), full(bgate), full(wglu), full(wco), full(wo)],
        out_specs=row, out_shape=jax.ShapeDtypeStruct((tp, d), F32),
        scratch_shapes=[pltpu.VMEM((HALO, dh), F32)],
        compiler_params=_params(("arbitrary",)),
    )(h1, ys5, p, p, p, p, cw, bgate, wglu, wco, wo)


def _mix_bwd(dh2, ys5, p, cw, bgate, wglu, wco, wo):
    tp, d = dh2.shape
    dh = d // 2
    tm = ROW_ALIGN
    ni = tp // tm
    hb = tm // HALO

    def body(dh_ref, y_ref, p0_ref, p1_ref, p2_ref, p3_ref, h0_ref, h1_ref,
             cw_ref, bg_ref, wglu_ref, wco_ref, wo_ref,
             dys_ref, dpb_ref, dwo_ref, dwglu_ref, dwco_ref, dcw_ref, dbg_ref, nxt):
        i = pl.program_id(0)
        tt = ni - 1 - i

        @pl.when(i == 0)
        def _():
            nxt[...] = jnp.zeros_like(nxt)
            dwo_ref[...] = jnp.zeros_like(dwo_ref)
            dwglu_ref[...] = jnp.zeros_like(dwglu_ref)
            dwco_ref[...] = jnp.zeros_like(dwco_ref)
            dcw_ref[...] = jnp.zeros_like(dcw_ref)
            dbg_ref[...] = jnp.zeros_like(dbg_ref)

        cw = [cw_ref[pl.ds(t, 1), :] for t in range(3)]
        prev_cin = h1_ref[:, dh:].astype(F32) * h0_ref[:, dh:].astype(F32)
        prev_cin = jnp.where(tt == 0, 0.0, prev_cin)
        ys5 = y_ref[...]
        f = _mix_tile(ys5, p0_ref[...], p1_ref[...], p2_ref[...], p3_ref[...], prev_cin,
                      cw, bg_ref[...], wglu_ref[...], wco_ref[...], d)
        dhb = dh_ref[...].astype(BF16)
        dmixed = _dot_nt(dhb, wo_ref[...])
        dwo_ref[...] += _dot_tn(f["mixed"].astype(BF16), dhb)

        g_s, g_c, sg = f["g_s"], f["g_c"], f["sg"]
        dy_ssm = dmixed * g_s
        dy_conv = dmixed * g_c
        dp2 = dmixed * f["y_ssm"] * g_s * (1.0 - g_s)
        dp3 = dmixed * f["y_conv"] * g_c * (1.0 - g_c)
        dbg_ref[:, pl.ds(0, d)] += jnp.sum(dp2, axis=0, keepdims=True)
        dbg_ref[:, pl.ds(d, d)] += jnp.sum(dp3, axis=0, keepdims=True)

        dz = jnp.concatenate([dy_ssm * sg, dy_ssm * f["z1"] * sg * (1.0 - sg)], axis=1).astype(BF16)
        dwglu_ref[...] += _dot_tn(f["gact"], dz)
        dys_ref[...] = (_dot_nt(dz, wglu_ref[...]) * _gelu_grad(ys5)).astype(BF16)

        dycb = dy_conv.astype(BF16)
        dwco_ref[...] += _dot_tn(f["cg"], dycb)
        dcg = _dot_nt(dycb, wco_ref[...])
        dgb = dcg * f["cv"]
        dcv = dcg * f["gb"]
        ext = jnp.concatenate([dcv, nxt[...]], axis=0)
        n1 = pltpu.roll(ext, tm + HALO - 1, 0)[:tm]
        n2 = pltpu.roll(ext, tm + HALO - 2, 0)[:tm]
        nxt[...] = dcv[:HALO, :]
        dcin = cw[2] * dcv + cw[1] * n1 + cw[0] * n2
        dcw_ref[pl.ds(0, 1), :] += jnp.sum(dcv * f["r2"], axis=0, keepdims=True)
        dcw_ref[pl.ds(1, 1), :] += jnp.sum(dcv * f["r1"], axis=0, keepdims=True)
        dcw_ref[pl.ds(2, 1), :] += jnp.sum(dcv * f["cin"], axis=0, keepdims=True)
        dgc = dcin * f["v"]
        dv = dcin * f["gc"]
        dpb_ref[0] = jnp.concatenate([jnp.zeros_like(dv), dv], axis=1).astype(BF16)
        dpb_ref[1] = jnp.concatenate([dgb, dgc], axis=1).astype(BF16)
        dpb_ref[2] = dp2.astype(BF16)
        dpb_ref[3] = dp3.astype(BF16)

    rev = lambda i: ni - 1 - i
    row = pl.BlockSpec((tm, d), lambda i: (rev(i), 0))
    half = pl.BlockSpec((tm, dh), lambda i: (rev(i), 0))
    full = lambda a: pl.BlockSpec(a.shape, lambda i: (0,) * a.ndim)
    pk = lambda k: pl.BlockSpec((None, tm, d), lambda i, k=k: (k, rev(i), 0))
    halo = lambda k: pl.BlockSpec((None, HALO, d), lambda i, k=k: (k, jnp.maximum(rev(i) * hb - 1, 0), 0))
    acc = lambda shape: pl.BlockSpec(shape, lambda i: (0,) * len(shape))
    return pl.pallas_call(
        body, name="mix_bwd", grid=(ni,),
        in_specs=[row, half, pk(0), pk(1), pk(2), pk(3), halo(0), halo(1),
                  full(cw), full(bgate), full(wglu), full(wco), full(wo)],
        out_specs=[half, pl.BlockSpec((4, tm, d), lambda i: (0, rev(i), 0)),
                   acc((d, d)), acc((dh, 2 * d)), acc((dh, d)), acc((SUBLANES, dh)), acc((1, 2 * d))],
        out_shape=[jax.ShapeDtypeStruct((tp, dh), BF16), jax.ShapeDtypeStruct((4, tp, d), BF16),
                   jax.ShapeDtypeStruct((d, d), F32), jax.ShapeDtypeStruct((dh, 2 * d), F32),
                   jax.ShapeDtypeStruct((dh, d), F32), jax.ShapeDtypeStruct((SUBLANES, dh), F32),
                   jax.ShapeDtypeStruct((1, 2 * d), F32)],
        scratch_shapes=[pltpu.VMEM((HALO, dh), F32)],
        compiler_params=_params(("arbitrary",)),
    )(dh2, ys5, p, p, p, p, p, p, cw, bgate, wglu, wco, wo)


ANY = pl.BlockSpec(memory_space=pl.ANY)


def _position():
    return lax.axis_index("x"), lax.axis_index("y"), lax.axis_index("c")


def _remote(src, dst, ssem, rsem, dev):
    return pltpu.make_async_remote_copy(src_ref=src, dst_ref=dst, send_sem=ssem, recv_sem=rsem,
                                        device_id=dev, device_id_type=MESH)


def _cast_piece(w, pos):
    rows, cols = w.shape
    r2 = rows // 2

    def body(pos_ref, w_ref, o_ref):
        o_ref[...] = w_ref[...].astype(BF16)

    return pl.pallas_call(
        body, name="cast_piece",
        grid_spec=pltpu.PrefetchScalarGridSpec(
            num_scalar_prefetch=1, grid=(1,),
            in_specs=[pl.BlockSpec((r2, cols), lambda i, pos: (pos[2], 0))],
            out_specs=pl.BlockSpec((r2, cols), lambda i, pos: (0, 0))),
        out_shape=jax.ShapeDtypeStruct((r2, cols), BF16),
        compiler_params=_params(("arbitrary",)),
    )(pos, w)


class _Carry:
    def __init__(self, name, arrays, out_shapes, nsem, nlsem, make, fracs):
        self.name, self.arrays, self.out_shapes = name, list(arrays), list(out_shapes)
        self.nsem, self.nlsem, self.make, self.fracs = nsem, max(nlsem, 1), make, fracs


def _carry_scratch(carry):
    return [pltpu.SemaphoreType.DMA((carry.nsem,)), pltpu.SemaphoreType.DMA((carry.nsem,)),
            pltpu.SemaphoreType.DMA((carry.nlsem,))]


def _run_carry(carry):
    na, no = len(carry.arrays), len(carry.out_shapes)

    def body(*refs):
        for phase in carry.make(refs[:na], refs[na:na + no], *refs[na + no:]):
            phase()

    return pl.pallas_call(
        body, name=carry.name, in_specs=[ANY] * na, out_specs=[ANY] * no, out_shape=carry.out_shapes,
        scratch_shapes=_carry_scratch(carry),
    )(*carry.arrays)


def _attach_carry(carry, in_specs, args, out_specs, out_shape, scratch):
    nhi, nho, nhs = len(in_specs), len(out_specs), len(scratch)
    if carry is None:
        return lambda refs: (list(refs), [])
    na, no = len(carry.arrays), len(carry.out_shapes)
    in_specs += [ANY] * na
    args += carry.arrays
    out_specs += [ANY] * no
    out_shape += carry.out_shapes
    scratch += _carry_scratch(carry)

    def split(refs):
        refs = list(refs)
        o = nhi + na
        host = refs[:nhi] + refs[o:o + nho] + refs[o + nho + no:o + nho + no + nhs]
        sems = refs[o + nho + no + nhs:]
        return host, carry.make(refs[nhi:o], refs[o + nho:o + nho + no], *sems)

    return split


def _run_phases(phases, carry, step, total):
    for phase, frac in zip(phases, carry.fracs if carry is not None else ()):
        pl.when(step == int(round(frac * (total - 1))))(phase)


def _allgather_carry(name, pieces, smalls):
    n, ns = len(pieces), len(smalls)
    per = 14
    n_big = per * n

    def make(ins, outs, ssem, rsem, lsem):
        pin, sin = ins[:n], ins[n:]
        wall, sall = outs[:n], outs[n:]
        x, y, c = _position()
        xnb, ynb, sib = (1 - x, y, c), (x, 1 - y, c), (x, y, 1 - c)
        chips = [(1 - x, y), (x, 1 - y), (1 - x, 1 - y)]
        r4 = [p.shape[0] // 2 for p in pin]
        own = lambda i, h: pin[i].at[pl.ds(h * r4[i], r4[i]), :]
        slot = lambda i, xx, yy, cc, h: wall[i].at[xx, yy, cc, h]
        cp = lambda src, dst, s, dev: _remote(src, dst, ssem.at[s], rsem.at[s], dev)
        to_sib = lambda i, xx, yy, h: cp(slot(i, xx, yy, c, h), slot(i, xx, yy, c, h),
                                         per * i + 6 + 4 * xx + 2 * yy + h, sib)

        def local():
            cps = [pltpu.make_async_copy(own(i, h), slot(i, x, y, c, h), lsem.at[2 * i + h])
                   for i in range(n) for h in range(2)]
            return cps + [pltpu.make_async_copy(sin[i], sall[i].at[2 * x + y], lsem.at[2 * n + i])
                          for i in range(ns)]

        def small(px, py, j, i, landing):
            s = n_big + j * ns + i
            return cp(sin[i], sall[i].at[landing], s, (px, py, c))

        def first_hop():
            for lc in local():
                lc.start()
            for j, (px, py) in enumerate(chips):
                for i in range(ns):
                    small(px, py, j, i, 2 * x + y).start()
            for i in range(n):
                cp(own(i, 0), slot(i, x, y, c, 0), per * i, xnb).start()
                cp(own(i, 1), slot(i, x, y, c, 1), per * i + 1, ynb).start()
                for h in range(2):
                    cp(own(i, h), slot(i, x, y, c, h), per * i + 6 + 4 * x + 2 * y + h, sib).start()

        def second_hop():
            for lc in local():
                lc.wait()
            for i in range(n):
                cp(slot(i, 1 - x, y, c, 0), slot(i, 1 - x, y, c, 0), per * i, xnb).wait_recv()
                cp(slot(i, x, 1 - y, c, 1), slot(i, x, 1 - y, c, 1), per * i + 1, ynb).wait_recv()
                for j in range(2):
                    cp(slot(i, j, y, c, 0), slot(i, j, y, c, 0), per * i + 2 + j, ynb).start()
                    cp(slot(i, x, j, c, 1), slot(i, x, j, c, 1), per * i + 4 + j, xnb).start()
                to_sib(i, 1 - x, y, 0).start()
                to_sib(i, x, 1 - y, 1).start()

        def last_to_sibling():
            for i in range(n):
                for j in range(2):
                    cp(slot(i, j, 1 - y, c, 0), slot(i, j, 1 - y, c, 0), per * i + 2 + j, ynb).wait_recv()
                    cp(slot(i, 1 - x, j, c, 1), slot(i, 1 - x, j, c, 1), per * i + 4 + j, xnb).wait_recv()
                    to_sib(i, j, 1 - y, 0).start()
                    to_sib(i, 1 - x, j, 1).start()

        def finish():
            for i in range(n):
                for xx in range(2):
                    for yy in range(2):
                        for h in range(2):
                            s = per * i + 6 + 4 * xx + 2 * yy + h
                            cp(slot(i, xx, yy, 1 - c, h), slot(i, xx, yy, 1 - c, h), s, sib).wait_recv()
                            to_sib(i, xx, yy, h).wait_send()
                cp(own(i, 0), slot(i, x, y, c, 0), per * i, xnb).wait_send()
                cp(own(i, 1), slot(i, x, y, c, 1), per * i + 1, ynb).wait_send()
                for j in range(2):
                    cp(slot(i, j, y, c, 0), slot(i, j, y, c, 0), per * i + 2 + j, ynb).wait_send()
                    cp(slot(i, x, j, c, 1), slot(i, x, j, c, 1), per * i + 4 + j, xnb).wait_send()
            for j, (px, py) in enumerate(chips):
                for i in range(ns):
                    small(px, py, j, i, 2 * px + py).wait_recv()
                    small(px, py, j, i, 2 * x + y).wait_send()

        return [first_hop, second_hop, last_to_sibling, finish]

    out_shapes = [jax.ShapeDtypeStruct((2, 2, 2, 2, a.shape[0] // 2, a.shape[1]), a.dtype) for a in pieces]
    out_shapes += [jax.ShapeDtypeStruct((4,) + a.shape, a.dtype) for a in smalls]
    return _Carry(name, list(pieces) + list(smalls), out_shapes, n_big + 3 * ns, 2 * n + ns, make,
                  (0.0, 0.23, 0.73, 1.0))


def _exchange_carry(name, arrays, out_shapes, plan):
    count = plan([None] * len(arrays), [None] * len(out_shapes), None)

    def make(ins, outs, ssem, rsem, lsem):
        def copies():
            return [_remote(src, dst, ssem.at[j], rsem.at[j], peer)
                    for j, (src, dst, peer) in enumerate(plan(ins, outs, _position()))]

        def start():
            for c in copies():
                c.start()

        def wait():
            for c in copies():
                c.wait()

        return [start, wait]

    return _Carry(name, arrays, out_shapes, count, 0, make, (0.0, 1.0))


class _Grad:
    def __init__(self, arrs, kind, shard_shape):
        self.arrs, self.kind = list(arrs), kind
        self.rows, self.cols = shard_shape
        self.r2 = self.rows // 2

    def view(self, refs, k, h):
        r2 = self.r2
        if self.kind == "list":
            return refs[k].at[pl.ds(h * r2, r2), :]
        if self.kind == "stacked":
            return refs[0].at[k, pl.ds(h * r2, r2), :]
        if self.kind == "col":
            return refs[0].at[pl.ds(h * r2, r2), pl.ds(k * self.cols, self.cols)]
        return refs[0].at[pl.ds((2 * k + h) * r2, r2), :]

    def add_half(self, recv, pos):
        r2, cols = self.r2, self.cols
        n_in = len(self.arrs)
        tr = _row_tile(r2, cols)
        nt = r2 // tr

        def body(pos_ref, *refs):
            m_refs, (r_ref, of_ref, ob_ref) = refs[:n_in], refs[n_in:]
            mine = m_refs[0][...]
            for kk in range(1, n_in):
                mine = jnp.where(pl.program_id(1) == kk, m_refs[kk][...], mine)
            s = mine + r_ref[...]
            of_ref[...] = s
            ob_ref[...] = s.astype(BF16)

        row = lambda t, pos: pos[2] * nt + t
        if self.kind == "list":
            specs = [pl.BlockSpec((tr, cols), lambda t, k, pos: (row(t, pos), 0))] * n_in
        elif self.kind == "stacked":
            specs = [pl.BlockSpec((None, tr, cols), lambda t, k, pos: (k, row(t, pos), 0))]
        elif self.kind == "col":
            specs = [pl.BlockSpec((tr, cols), lambda t, k, pos: (row(t, pos), k))]
        else:
            specs = [pl.BlockSpec((tr, cols), lambda t, k, pos: (2 * k * nt + row(t, pos), 0))]
        blk = pl.BlockSpec((None, tr, cols), lambda t, k, pos: (k, t, 0))
        return pl.pallas_call(
            body, name="rs_add_c",
            grid_spec=pltpu.PrefetchScalarGridSpec(
                num_scalar_prefetch=1, grid=(nt, 4), in_specs=specs + [blk], out_specs=[blk, blk]),
            out_shape=[jax.ShapeDtypeStruct((4, r2, cols), F32), jax.ShapeDtypeStruct((4, r2, cols), BF16)],
            compiler_params=_params(("arbitrary", "arbitrary")),
        )(pos, *self.arrs, recv)


def _row_tile(rows, cols):
    fits = [t for t in range(16, rows + 1, 16) if rows % t == 0 and t * cols * 4 <= 1024 * 1024]
    return max(fits) if fits else rows


def _adamw_math(w, g, m, v):
    m = ADAM_B1 * m + (1.0 - ADAM_B1) * g
    v = ADAM_B2 * v + (1.0 - ADAM_B2) * (g * g)
    m_hat = m / (1.0 - ADAM_B1 ** ADAM_STEP)
    v_hat = v / (1.0 - ADAM_B2 ** ADAM_STEP)
    delta = -ADAM_LR * (m_hat / (jnp.sqrt(v_hat) + ADAM_EPS) + ADAM_WD * w)
    return delta, m, v


def _adamw_big(w, m, v, own, sib, pos):
    rows, cols = w.shape
    r2 = rows // 2

    tr = _row_tile(r2, cols)
    nt = r2 // tr

    def body(pos_ref, w_ref, m_ref, v_ref, own_ref, sib_ref, g_ref, d_ref, nm_ref, nv_ref):
        h = pl.program_id(0)
        g = jnp.where(h == pos_ref[2], own_ref[...], sib_ref[...])
        g_ref[...] = g
        d_ref[...], nm_ref[...], nv_ref[...] = _adamw_math(w_ref[...], g, m_ref[...], v_ref[...])

    half = pl.BlockSpec((tr, cols), lambda h, t, pos: (h * nt + t, 0))
    piece = pl.BlockSpec((tr, cols), lambda h, t, pos: (t, 0))
    out = jax.ShapeDtypeStruct((rows, cols), F32)
    return pl.pallas_call(
        body, name="adamw",
        grid_spec=pltpu.PrefetchScalarGridSpec(
            num_scalar_prefetch=1, grid=(2, nt),
            in_specs=[half, half, half, piece, piece],
            out_specs=[half, half, half, half]),
        out_shape=[out, out, out, out],
        compiler_params=_params(("arbitrary", "arbitrary")),
    )(pos, w, m, v, own, sib)


def _add_hop1(s1f, recv, pos):
    _, _, r4, cols = recv.shape
    s1v = s1f.reshape(4, 2, r4, cols)

    def body(pos_ref, m_ref, r_ref, of_ref, ob_ref):
        s = m_ref[...] + r_ref[...].astype(F32)
        of_ref[...] = s
        ob_ref[...] = s.astype(BF16)

    def mine(h, j, pos):
        return (jnp.where(h == 0, 2 * j + pos[1], 2 * pos[0] + j), h, 0, 0)

    blk = pl.BlockSpec((None, None, r4, cols), lambda h, j, pos: (h, j, 0, 0))
    return pl.pallas_call(
        body, name="rs_add_1",
        grid_spec=pltpu.PrefetchScalarGridSpec(
            num_scalar_prefetch=1, grid=(2, 2),
            in_specs=[pl.BlockSpec((None, None, r4, cols), mine), blk], out_specs=[blk, blk]),
        out_shape=[jax.ShapeDtypeStruct((2, 2, r4, cols), F32), jax.ShapeDtypeStruct((2, 2, r4, cols), BF16)],
        compiler_params=_params(("arbitrary", "arbitrary")),
    )(pos, s1v, recv)


def _own_sum(s2f, recv3, pos):
    _, _, r4, cols = s2f.shape

    def body(pos_ref, s_ref, r_ref, o_ref):
        o_ref[...] = s_ref[...] + r_ref[...].astype(F32)

    blk = pl.BlockSpec((None, r4, cols), lambda h, pos: (h, 0, 0))
    return pl.pallas_call(
        body, name="own_sum",
        grid_spec=pltpu.PrefetchScalarGridSpec(
            num_scalar_prefetch=1, grid=(2,),
            in_specs=[pl.BlockSpec((None, None, r4, cols),
                                   lambda h, pos: (h, jnp.where(h == 0, pos[0], pos[1]), 0, 0)), blk],
            out_specs=blk),
        out_shape=jax.ShapeDtypeStruct((2, r4, cols), F32),
        compiler_params=_params(("arbitrary",)),
    )(pos, s2f, recv3)


def _allreduce_small(buf):
    def body(x_ref, o_ref, recv, ssem, rsem):
        x, y, c = _position()
        o_ref[...] = x_ref[...]
        for s, peer in enumerate([(x, y, 1 - c), (x, 1 - y, c), (1 - x, y, c)]):
            cp = _remote(o_ref, recv.at[s], ssem.at[s], rsem.at[s], peer)
            cp.start()
            cp.wait()
            o_ref[...] = o_ref[...] + recv[s]

    vm = pl.BlockSpec(memory_space=pltpu.VMEM)
    return pl.pallas_call(
        body, name="allreduce_small", in_specs=[vm], out_specs=vm,
        out_shape=jax.ShapeDtypeStruct(buf.shape, F32),
        scratch_shapes=[pltpu.VMEM((3,) + buf.shape, F32),
                        pltpu.SemaphoreType.DMA((3,)), pltpu.SemaphoreType.DMA((3,))],
    )(buf)


def _adamw_small(w, g, m, v):
    def body(w_ref, g_ref, m_ref, v_ref, d_ref, nm_ref, nv_ref):
        d_ref[...], nm_ref[...], nv_ref[...] = _adamw_math(w_ref[...], g_ref[...], m_ref[...], v_ref[...])

    vm = pl.BlockSpec(memory_space=pltpu.VMEM)
    out = jax.ShapeDtypeStruct(w.shape, F32)
    return pl.pallas_call(body, name="adamw_small", in_specs=[vm] * 4, out_specs=[vm] * 3,
                          out_shape=[out, out, out])(w, g, m, v)


class _ReduceScatter:
    def __init__(self, tag, grads, pos):
        self.tag, self.grads, self.pos, self.stage = tag, grads, pos, 0

    def carry(self):
        grads, n = self.grads, len(self.grads)
        r4 = [g.r2 // 2 for g in grads]

        first = [sum(len(g.arrs) for g in grads[:i]) for i in range(n)]

        def plan_c(ins, outs, p):
            if p is None:
                return 4 * n
            x, y, c = p
            mine = lambda i: ins[first[i]:first[i] + len(grads[i].arrs)]
            return [(grads[i].view(mine(i), k, 1 - c), outs[i].at[k], (x, y, 1 - c))
                    for i in range(n) for k in range(4)]

        def plan_1(ins, outs, p):
            if p is None:
                return 4 * n
            x, y, c = p
            copies = []
            for i in range(n):
                for j in range(2):
                    copies.append((ins[i].at[2 * j + (1 - y), pl.ds(0, r4[i]), :], outs[i].at[0, j],
                                   (x, 1 - y, c)))
                    copies.append((ins[i].at[2 * (1 - x) + j, pl.ds(r4[i], r4[i]), :], outs[i].at[1, j],
                                   (1 - x, y, c)))
            return copies

        def plan_2(ins, outs, p):
            if p is None:
                return 2 * n
            x, y, c = p
            copies = []
            for i in range(n):
                copies.append((ins[i].at[0, 1 - x], outs[i].at[0], (1 - x, y, c)))
                copies.append((ins[i].at[1, 1 - y], outs[i].at[1], (x, 1 - y, c)))
            return copies

        def plan_s(ins, outs, p):
            if p is None:
                return n
            x, y, c = p
            return [(ins[i], outs[i], (x, y, 1 - c)) for i in range(n)]

        shape = lambda lead, dt: [jax.ShapeDtypeStruct(lead(g) + (g.cols,), dt) for g in grads]
        stage = self.stage
        if stage == 0:
            return _exchange_carry(f"rs_{self.tag}_exchange_c", [a for g in grads for a in g.arrs],
                                   shape(lambda g: (4, g.r2), F32), plan_c)
        if stage == 1:
            return _exchange_carry(f"rs_{self.tag}_exchange_1", [s[1] for s in self.s1],
                                   shape(lambda g: (2, 2, g.r2 // 2), BF16), plan_1)
        if stage == 2:
            return _exchange_carry(f"rs_{self.tag}_exchange_2", [s[1] for s in self.s2],
                                   shape(lambda g: (2, g.r2 // 2), BF16), plan_2)
        return _exchange_carry(f"rs_{self.tag}_exchange_sibling", self.own, shape(lambda g: (g.r2,), F32), plan_s)

    def feed(self, recv):
        grads, pos = self.grads, self.pos
        if self.stage == 0:
            self.s1 = [g.add_half(r, pos) for g, r in zip(grads, recv)]
        elif self.stage == 1:
            self.s2 = [_add_hop1(s[0], r, pos) for s, r in zip(self.s1, recv)]
        elif self.stage == 2:
            self.own = [_own_sum(s[0], r, pos).reshape(g.r2, g.cols) for g, s, r in zip(grads, self.s2, recv)]
        else:
            self.sib = list(recv)
        self.stage += 1

    def run(self):
        while self.stage < 4:
            self.feed(_run_carry(self.carry()))

    def adamw(self, weights):
        return [_adamw_big(w, m, v, o, sb, self.pos) for (w, m, v), o, sb in zip(weights, self.own, self.sib)]


def _block_diag(t, nb):
    g, c, p = t.shape
    gb = g // nb
    t = t.reshape(nb, gb, c, p)
    eye = jnp.eye(gb, dtype=t.dtype)
    return jnp.einsum("bgcp,gh->bgchp", t, eye).reshape(nb, gb * c, gb * p)


def _s5_discretise(a_re, a_im, log_dt, b_re, b_im, c_re, c_im):
    g, p = a_re.shape
    nb = g // GROUPS_PER_BLOCK
    dt = jnp.exp(log_dt)[:, None]
    mag = jnp.exp(a_re * dt)
    lam_re = mag * jnp.cos(a_im * dt)
    lam_im = mag * jnp.sin(a_im * dt)
    den = a_re * a_re + a_im * a_im
    q_re = ((lam_re - 1.0) * a_re + lam_im * a_im) / den
    q_im = (lam_im * a_re - (lam_re - 1.0) * a_im) / den
    bb_re = q_re[..., None] * b_re - q_im[..., None] * b_im
    bb_im = q_re[..., None] * b_im + q_im[..., None] * b_re
    tr = lambda t: jnp.swapaxes(t, 1, 2)
    mb = jnp.concatenate([_block_diag(tr(bb_re), nb), _block_diag(tr(bb_im), nb)], axis=-1)
    mc = jnp.concatenate([_block_diag(c_re, nb), -_block_diag(c_im, nb)], axis=-1)
    lam = jnp.concatenate([lam_re.reshape(nb, -1), lam_im.reshape(nb, -1)], axis=-1)
    return mb, mc, lam


def _s5_powers(a_re, a_im, log_dt, sub):
    g, p = a_re.shape
    nb = g // GROUPS_PER_BLOCK
    dt = jnp.exp(log_dt)[:, None]
    ns = list(range(1, sub + 1)) + [2 * sub, 4 * sub]
    ns += [0] * (-len(ns) % SUBLANES)
    e = jnp.asarray(ns, F32)[:, None, None]
    mag = jnp.exp(a_re[None] * dt[None] * e)
    ang = a_im[None] * dt[None] * e
    re = (mag * jnp.cos(ang)).reshape(len(ns), nb, -1)
    im = (mag * jnp.sin(ang)).reshape(len(ns), nb, -1)
    return jnp.transpose(jnp.concatenate([re, im], axis=-1), (1, 0, 2))


def _pack(parts):
    flat = jnp.concatenate([a.reshape(-1).astype(F32) for a in parts])
    n = flat.shape[0]
    pad = -n % (SUBLANES * LANES)
    return jnp.pad(flat, (0, pad)).reshape(-1, LANES)


def _unpack(buf, like):
    flat = buf.reshape(-1)
    out, o = [], 0
    for a in like:
        out.append(flat[o:o + a.size].reshape(a.shape))
        o += a.size
    return out


def kernel(x, meta_tokens, g_ffn1, ffn1_w_gate, ffn1_w_up, ffn1_w_down, g_mix, w_in, b_gate, ssm_a_re, ssm_a_im, ssm_log_dt, ssm_b_re, ssm_b_im, ssm_c_re, ssm_c_im, ssm_d, ssm_w_glu, conv_w, conv_w_out, w_o, g_ffn2, ffn2_w_gate, ffn2_w_up, ffn2_w_down, g_final, loss_target, m_meta_tokens, m_g_ffn1, m_ffn1_w_gate, m_ffn1_w_up, m_ffn1_w_down, m_g_mix, m_w_in, m_b_gate, m_ssm_a_re, m_ssm_a_im, m_ssm_log_dt, m_ssm_b_re, m_ssm_b_im, m_ssm_c_re, m_ssm_c_im, m_ssm_d, m_ssm_w_glu, m_conv_w, m_conv_w_out, m_w_o, m_g_ffn2, m_ffn2_w_gate, m_ffn2_w_up, m_ffn2_w_down, m_g_final, v_meta_tokens, v_g_ffn1, v_ffn1_w_gate, v_ffn1_w_up, v_ffn1_w_down, v_g_mix, v_w_in, v_b_gate, v_ssm_a_re, v_ssm_a_im, v_ssm_log_dt, v_ssm_b_re, v_ssm_b_im, v_ssm_c_re, v_ssm_c_im, v_ssm_d, v_ssm_w_glu, v_conv_w, v_conv_w_out, v_w_o, v_g_ffn2, v_ffn2_w_gate, v_ffn2_w_up, v_ffn2_w_down, v_g_final):
    seq, d = x.shape[1], x.shape[2]
    n_meta = meta_tokens.shape[0]
    dh = d // 2
    tp = -(-(n_meta + seq) // ROW_ALIGN) * ROW_ALIGN
    mx, my, mc_ = _position()
    pos = jnp.stack([mx, my, mc_]).astype(jnp.int32)
    shard = 2 * mx + my

    big_names = ["ffn1_w_gate", "ffn1_w_up", "ffn1_w_down", "w_in", "ssm_w_glu", "conv_w_out", "w_o",
                 "ffn2_w_gate", "ffn2_w_up", "ffn2_w_down"]
    drop = lambda arrs: [a.reshape(a.shape[1:]) for a in arrs]
    big_w = drop([ffn1_w_gate, ffn1_w_up, ffn1_w_down, w_in, ssm_w_glu, conv_w_out, w_o,
                  ffn2_w_gate, ffn2_w_up, ffn2_w_down])
    big_m = drop([m_ffn1_w_gate, m_ffn1_w_up, m_ffn1_w_down, m_w_in, m_ssm_w_glu, m_conv_w_out,
                  m_w_o, m_ffn2_w_gate, m_ffn2_w_up, m_ffn2_w_down])
    big_v = drop([v_ffn1_w_gate, v_ffn1_w_up, v_ffn1_w_down, v_w_in, v_ssm_w_glu, v_conv_w_out,
                  v_w_o, v_ffn2_w_gate, v_ffn2_w_up, v_ffn2_w_down])
    pieces = [_cast_piece(w, pos) for w in big_w]
    conv_local = conv_w.reshape(conv_w.shape[1], conv_w.shape[3])
    n_first = 3
    first = _run_carry(_allgather_carry("allgather_first", pieces[:n_first], [meta_tokens, conv_local]))
    smalls = first[n_first:]
    stack4 = lambda wl: wl.reshape((4, -1, wl.shape[-1]))
    w1g, w1u, w1d = [stack4(wl) for wl in first[:n_first]]
    natural_cols = lambda s: jnp.transpose(s, (1, 0, 2)).reshape(s.shape[1], 4 * s.shape[2])
    meta_full = natural_cols(smalls[0])
    cw_full = natural_cols(smalls[1])
    cw_pad = jnp.pad(cw_full, ((0, SUBLANES - cw_full.shape[0]), (0, 0)))

    s5_args = (ssm_a_re[0], ssm_a_im[0], ssm_log_dt[0], ssm_b_re[0], ssm_b_im[0], ssm_c_re[0], ssm_c_im[0])
    (mb, mc, _), disc_vjp = jax.vjp(_s5_discretise, *s5_args)
    powt = _s5_powers(ssm_a_re[0], ssm_a_im[0], ssm_log_dt[0], SCAN_TILE // SUBLANES)
    mb16, mc16 = mb.astype(BF16), mc.astype(BF16)

    pad_rows = tp - n_meta - seq
    h0 = jnp.concatenate([meta_full, x.reshape(seq, d), jnp.zeros((pad_rows, d), F32)], axis=0)
    tgt = jnp.concatenate([jnp.zeros((n_meta, d), F32), loss_target.reshape(seq, d),
                           jnp.zeros((pad_rows, d), F32)], axis=0)
    n_mid = 7
    h1, a1, b1, *mid = _ffn_fwd(h0, g_ffn1, w1g, w1u, w1d, "ffn1_fwd",
                                carry=_allgather_carry("allgather_mid", pieces[n_first:n_mid], []))
    win_all, wglu_s, wco_s, wo_s = [stack4(wl) for wl in mid]
    wglu_all = natural_cols(wglu_s)
    wco_all = natural_cols(wco_s)
    wo_all = wo_s.reshape(d, d)
    u, p = _win_fwd(h1, g_mix, win_all)
    ys5, bnd, *last = _scan_fwd(p, mb16, mc16, powt, ssm_d,
                                ride=_allgather_carry("allgather_last", pieces[n_mid:], []))
    w2g, w2u, w2d = [stack4(wl) for wl in last]
    h2 = _mix_fwd(h1, ys5, p, cw_pad, b_gate, wglu_all, wco_all, wo_all)
    dh3, a2, b2, dg_final, loss_part = _ffn_fwd(
        h2, g_ffn2, w2g, w2u, w2d, "ffn2_fwd_loss", final=(g_final.reshape(1, d), tgt, n_meta, seq))

    dh2, dw2g, dw2u, dw2d, dg_ffn2 = _ffn_bwd(dh3, h2, g_ffn2, a2, b2, w2g, w2u, w2d, "ffn2_bwd")
    dys5, dpb, dwo, dwglu, dwco, dcw, dbg = _mix_bwd(dh2, ys5, p, cw_pad, b_gate, wglu_all, wco_all, wo_all)
    dug, dmb, dmc, dlam, dd = _scan_bwd(p, dys5, mb16, mc16, powt, ssm_d, bnd)
    dh1, dwin, dg_mix = _win_bwd(dpb, dug, u, win_all, h1, g_mix, dh2)
    shapes = [w.shape for w in big_w]
    kinds = ["list", "list", "list", "list", "col", "col", "row", "list", "list", "list"]
    rest_grads = [dwin, [dwglu], [dwco], [dwo], dw2g, dw2u, dw2d]
    rs_rest = _ReduceScatter("rest", [_Grad(a, k, s) for a, k, s in
                                      zip(rest_grads, kinds[n_first:], shapes[n_first:])], pos)
    dh0, dw1g, dw1u, dw1d, dg_ffn1 = _ffn_bwd(dh1, h0, g_ffn1, a1, b1, w1g, w1u, w1d, "ffn1_bwd", chain=rs_rest)
    rs_first = _ReduceScatter("first", [_Grad(a, k, s) for a, k, s in
                                        zip([dw1g, dw1u, dw1d], kinds[:n_first], shapes[:n_first])], pos)
    rs_first.run()
    wmv = list(zip(big_w, big_m, big_v))
    big_out = rs_first.adamw(wmv[:n_first]) + rs_rest.adamw(wmv[n_first:])
    big_out = {nme: tuple(o.reshape((1,) + o.shape) for o in outs) for nme, outs in zip(big_names, big_out)}

    s5_grads = disc_vjp((dmb, dmc, jnp.sum(dlam, axis=1)))
    grad_x = dh0[n_meta:n_meta + seq][None]

    small_names = ["g_ffn1", "g_mix", "b_gate", "ssm_a_re", "ssm_a_im", "ssm_log_dt", "ssm_b_re", "ssm_b_im",
                   "ssm_c_re", "ssm_c_im", "ssm_d", "g_ffn2", "g_final", "meta_tokens", "conv_w"]
    small_w = [g_ffn1, g_mix, b_gate, ssm_a_re, ssm_a_im, ssm_log_dt, ssm_b_re, ssm_b_im, ssm_c_re, ssm_c_im,
               ssm_d, g_ffn2, g_final, meta_tokens, conv_w]
    small_m = [m_g_ffn1, m_g_mix, m_b_gate, m_ssm_a_re, m_ssm_a_im, m_ssm_log_dt, m_ssm_b_re, m_ssm_b_im,
               m_ssm_c_re, m_ssm_c_im, m_ssm_d, m_g_ffn2, m_g_final, m_meta_tokens, m_conv_w]
    small_v = [v_g_ffn1, v_g_mix, v_b_gate, v_ssm_a_re, v_ssm_a_im, v_ssm_log_dt, v_ssm_b_re, v_ssm_b_im,
               v_ssm_c_re, v_ssm_c_im, v_ssm_d, v_g_ffn2, v_g_final, v_meta_tokens, v_conv_w]
    local_small = [dg_ffn1, dg_mix, dbg, *s5_grads, jnp.sum(dd, axis=0), dg_ffn2, dg_final,
                   dh0[:n_meta], dcw[:conv_w.shape[1]]]
    reduced = _unpack(_allreduce_small(_pack(local_small)), local_small)
    reduced[-2] = lax.dynamic_slice_in_dim(reduced[-2], shard * meta_tokens.shape[1], meta_tokens.shape[1], 1)
    reduced[-1] = lax.dynamic_slice_in_dim(reduced[-1], shard * conv_w.shape[3], conv_w.shape[3], 1)
    small_g = [r.reshape(w.shape) for r, w in zip(reduced, small_w)]
    ds_, nm_, nv_ = _adamw_small(_pack(small_w), _pack(small_g), _pack(small_m), _pack(small_v))
    small_out = {nme: o for nme, o in zip(
        small_names, zip(small_g, _unpack(ds_, small_w), _unpack(nm_, small_w), _unpack(nv_, small_w)))}

    loss = lax.psum(loss_part[0, 0], ("x", "y", "c"))
    order = ["meta_tokens", "g_ffn1", "ffn1_w_gate", "ffn1_w_up", "ffn1_w_down", "g_mix", "w_in", "b_gate",
             "ssm_a_re", "ssm_a_im", "ssm_log_dt", "ssm_b_re", "ssm_b_im", "ssm_c_re", "ssm_c_im", "ssm_d",
             "ssm_w_glu", "conv_w", "conv_w_out", "w_o", "g_ffn2", "ffn2_w_gate", "ffn2_w_up", "ffn2_w_down",
             "g_final"]
    res = {**big_out, **small_out}
    return (loss, grad_x, *[res[nme][0] for nme in order], *[res[nme][1] for nme in order],
            *[res[nme][2] for nme in order], *[res[nme][3] for nme in order])
```

```python
import functools
import math

import jax
import jax.numpy as jnp
from jax import lax
from jax.experimental import pallas as pl
from jax.experimental.pallas import tpu as pltpu

F32 = jnp.float32
BF16 = jnp.bfloat16
MESH = pl.DeviceIdType.MESH

RMS_EPS = 1e-6
ADAM_LR = 0.001
ADAM_B1 = 0.9
ADAM_B2 = 0.999
ADAM_EPS = 1e-08
ADAM_WD = 0.01
ADAM_STEP = 10

LANES = 128
SUBLANES = 8
VMEM_LIMIT = 56 * 1024 * 1024

ROW_ALIGN = 256
SCAN_TILE = 256
GROUPS_PER_BLOCK = 8


def _params(sem, vmem=VMEM_LIMIT):
    return pltpu.CompilerParams(dimension_semantics=sem, vmem_limit_bytes=vmem)


def _pick_tile(n, candidates):
    for c in candidates:
        if n % c == 0:
            return c
    raise ValueError(f"no tile for {n}")


def _dot(a, b):
    return jnp.dot(a, b, preferred_element_type=F32)


def _dot_nt(a, b):
    return lax.dot_general(a, b, (((1,), (1,)), ((), ())), preferred_element_type=F32)


def _dot_tn(a, b):
    return lax.dot_general(a, b, (((0,), (0,)), ((), ())), preferred_element_type=F32)


def _sigmoid(x):
    return 1.0 / (1.0 + jnp.exp(-x))


def _rms_stats(h):
    r = lax.rsqrt(jnp.mean(h * h, axis=-1, keepdims=True) + RMS_EPS)
    return h * r, r


def _rms_bwd(xhat, r, g, dn):
    dxh = dn * g
    return r * (dxh - xhat * jnp.mean(dxh * xhat, axis=-1, keepdims=True))


GELU_K = math.sqrt(2.0 / math.pi)
GELU_C = 0.044715


def _gelu(x):
    return 0.5 * x * (1.0 + jnp.tanh(GELU_K * (x + GELU_C * x * x * x)))


def _gelu_grad(x):
    t = jnp.tanh(GELU_K * (x + GELU_C * x * x * x))
    return 0.5 * (1.0 + t) + 0.5 * x * (1.0 - t * t) * GELU_K * (1.0 + 3.0 * GELU_C * x * x)


def _ffn_fwd(h, g, wg, wu, wd, name, final=None, carry=None):
    tp, d = h.shape
    ns, _, f4 = wg.shape
    tm = _pick_tile(tp, (768, 512, 256))
    ni = tp // tm

    def body(*refs):
        refs, phases = split(refs)
        if final is None:
            h_ref, g_ref, wg_ref, wu_ref, wd_ref, ho_ref, a_ref, b_ref, n_scr, acc = refs
        else:
            (h_ref, g_ref, wg_ref, wu_ref, wd_ref, gf_ref, tg_ref,
             ho_ref, a_ref, b_ref, dgf_ref, loss_ref, n_scr, acc) = refs
        i = pl.program_id(0)
        k = pl.program_id(1)
        _run_phases(phases, carry, i * ns + k, ni * ns)

        @pl.when(k == 0)
        def _():
            xhat, _ = _rms_stats(h_ref[...])
            n_scr[...] = (xhat * g_ref[...]).astype(BF16)
            acc[...] = jnp.zeros_like(acc)

        n = n_scr[...]
        a = _dot(n, wg_ref[...])
        b = _dot(n, wu_ref[...])
        a_ref[...] = a.astype(BF16)
        b_ref[...] = b.astype(BF16)
        s = (a * _sigmoid(a) * b).astype(BF16)
        acc[...] += _dot(s, wd_ref[...])

        if final is None:
            @pl.when(k == ns - 1)
            def _():
                ho_ref[...] = h_ref[...] + 0.5 * acc[...]
        else:
            n_meta, seq = final[2], final[3]

            @pl.when((i == 0) & (k == 0))
            def _():
                dgf_ref[...] = jnp.zeros_like(dgf_ref)
                loss_ref[...] = jnp.zeros_like(loss_ref)

            @pl.when(k == ns - 1)
            def _():
                h3 = h_ref[...] + 0.5 * acc[...]
                xhat, r = _rms_stats(h3)
                gf = gf_ref[...]
                row = i * tm + lax.broadcasted_iota(jnp.int32, (tm, d), 0)
                valid = (row >= n_meta) & (row < n_meta + seq)
                diff = jnp.where(valid, xhat * gf - tg_ref[...], 0.0)
                dout = diff * (1.0 / d)
                loss_ref[...] += jnp.full(loss_ref.shape, 0.5 * jnp.sum(diff * diff) * (1.0 / d), F32)
                dgf_ref[...] += jnp.sum(dout * xhat, axis=0, keepdims=True)
                ho_ref[...] = _rms_bwd(xhat, r, gf, dout)

    row_spec = pl.BlockSpec((tm, d), lambda i, k: (i, 0))
    vec_spec = pl.BlockSpec((1, d), lambda i, k: (0, 0))
    in_specs = [row_spec, vec_spec,
                pl.BlockSpec((None, d, f4), lambda i, k: (k, 0, 0)),
                pl.BlockSpec((None, d, f4), lambda i, k: (k, 0, 0)),
                pl.BlockSpec((None, f4, d), lambda i, k: (k, 0, 0))]
    act_spec = pl.BlockSpec((None, tm, f4), lambda i, k: (k, i, 0))
    out_specs = [row_spec, act_spec, act_spec]
    out_shape = [jax.ShapeDtypeStruct((tp, d), F32),
                 jax.ShapeDtypeStruct((ns, tp, f4), BF16),
                 jax.ShapeDtypeStruct((ns, tp, f4), BF16)]
    args = [h, g, wg, wu, wd]
    if final is not None:
        in_specs += [vec_spec, row_spec]
        args += [final[0], final[1]]
        out_specs += [vec_spec, pl.BlockSpec((1, LANES), lambda i, k: (0, 0))]
        out_shape += [jax.ShapeDtypeStruct((1, d), F32), jax.ShapeDtypeStruct((1, LANES), F32)]
    scratch = [pltpu.VMEM((tm, d), BF16), pltpu.VMEM((tm, d), F32)]
    split = _attach_carry(carry, in_specs, args, out_specs, out_shape, scratch)
    return pl.pallas_call(
        body, name=name, grid=(ni, ns), in_specs=in_specs, out_specs=out_specs, out_shape=out_shape,
        scratch_shapes=scratch, compiler_params=_params(("arbitrary", "arbitrary")),
    )(*args)


def _ffn_bwd_shard(k, ns, dn_prev, dh_out, h_in, g, a, b, wg, wu, wd, name, carry=None):
    tp, d = h_in.shape
    f4 = wg.shape[2]
    tm = _pick_tile(tp, (768, 512, 256))
    ni = tp // tm
    first, last = k == 0, k == ns - 1

    def body(*refs):
        refs, phases = split(refs)
        acc_in = None if first else refs.pop(0)
        (dh_ref, h_ref, g_ref, a_ref, b_ref, wg_hbm, wu_hbm, wd_hbm,
         acc_out, dwg_hbm, dwu_hbm, dwd_hbm) = refs[:12]
        rest = refs[12:]
        dg_ref = rest.pop(0) if last else None
        wg_ref, wu_ref, wd_ref, dwg_ref, dwu_ref, dwd_ref = rest
        i = pl.program_id(0)
        _run_phases(phases, carry, i, ni)

        @pl.when(i == 0)
        def _():
            pltpu.sync_copy(wg_hbm.at[k], wg_ref)
            pltpu.sync_copy(wu_hbm.at[k], wu_ref)
            pltpu.sync_copy(wd_hbm.at[k], wd_ref)
            dwg_ref[...] = jnp.zeros_like(dwg_ref)
            dwu_ref[...] = jnp.zeros_like(dwu_ref)
            dwd_ref[...] = jnp.zeros_like(dwd_ref)
            if last:
                dg_ref[...] = jnp.zeros_like(dg_ref)

        xhat, r = _rms_stats(h_ref[...])
        gv = g_ref[...]
        n = (xhat * gv).astype(BF16)
        dy = (0.5 * dh_ref[...]).astype(BF16)
        av = a_ref[...].astype(F32)
        bv = b_ref[...].astype(F32)
        sg = _sigmoid(av)
        silu = av * sg
        ds = _dot_nt(dy, wd_ref[...])
        da = (ds * bv * (sg * (1.0 + av * (1.0 - sg)))).astype(BF16)
        db = (ds * silu).astype(BF16)
        s = (silu * bv).astype(BF16)
        dwd_ref[...] += _dot_tn(s, dy)
        dwg_ref[...] += _dot_tn(n, da)
        dwu_ref[...] += _dot_tn(n, db)
        dn = _dot_nt(da, wg_ref[...]) + _dot_nt(db, wu_ref[...])
        if not first:
            dn = dn + acc_in[...]
        if last:
            dg_ref[...] += jnp.sum(dn * xhat, axis=0, keepdims=True)
            acc_out[...] = dh_ref[...] + _rms_bwd(xhat, r, gv, dn)
        else:
            acc_out[...] = dn

        @pl.when(i == ni - 1)
        def _():
            pltpu.sync_copy(dwg_ref, dwg_hbm)
            pltpu.sync_copy(dwu_ref, dwu_hbm)
            pltpu.sync_copy(dwd_ref, dwd_hbm)

    row_spec = pl.BlockSpec((tm, d), lambda i: (i, 0))
    vec_spec = pl.BlockSpec((1, d), lambda i: (0, 0))
    act_spec = pl.BlockSpec((None, tm, f4), lambda i: (k, i, 0))
    in_specs = [row_spec, row_spec, vec_spec, act_spec, act_spec, ANY, ANY, ANY]
    args = [dh_out, h_in, g, a, b, wg, wu, wd]
    if not first:
        in_specs.insert(0, row_spec)
        args.insert(0, dn_prev)
    out_specs = [row_spec, ANY, ANY, ANY]
    out_shape = [jax.ShapeDtypeStruct((tp, d), F32), jax.ShapeDtypeStruct((d, f4), F32),
                 jax.ShapeDtypeStruct((d, f4), F32), jax.ShapeDtypeStruct((f4, d), F32)]
    if last:
        out_specs.append(vec_spec)
        out_shape.append(jax.ShapeDtypeStruct((1, d), F32))
    n_host = len(out_shape)
    scratch = [pltpu.VMEM((d, f4), BF16), pltpu.VMEM((d, f4), BF16), pltpu.VMEM((f4, d), BF16),
               pltpu.VMEM((d, f4), F32), pltpu.VMEM((d, f4), F32), pltpu.VMEM((f4, d), F32)]
    split = _attach_carry(carry, in_specs, args, out_specs, out_shape, scratch)
    outs = pl.pallas_call(
        body, name=f"{name}_{k}", grid=(ni,), in_specs=in_specs, out_specs=out_specs, out_shape=out_shape,
        scratch_shapes=scratch, compiler_params=_params(("arbitrary",)),
    )(*args)
    return outs[:n_host], outs[n_host:]


def _ffn_bwd(dh_out, h_in, g, a, b, wg, wu, wd, name, chain=None):
    ns = wg.shape[0]
    acc, dwg, dwu, dwd, dg = None, [], [], [], None
    for k in range(ns):
        carry = chain.carry() if chain is not None else None
        outs, carried = _ffn_bwd_shard(k, ns, acc, dh_out, h_in, g, a, b, wg, wu, wd, name, carry)
        if chain is not None:
            chain.feed(carried)
        acc = outs[0]
        dwg.append(outs[1])
        dwu.append(outs[2])
        dwd.append(outs[3])
        if k == ns - 1:
            dg = outs[4]
    return acc, dwg, dwu, dwd, dg


def _win_fwd(h, g, w_in, carry=None):
    tp, d = h.shape
    ns = w_in.shape[0]
    tm = _pick_tile(tp, (768, 512, 256))
    ni = tp // tm

    def body(*refs):
        (h_ref, g_ref, w_ref, u_ref, p_ref), phases = split(refs)
        _run_phases(phases, carry, pl.program_id(0) * ns + pl.program_id(1), ni * ns)

        @pl.when(pl.program_id(1) == 0)
        def _():
            xhat, _ = _rms_stats(h_ref[...])
            u_ref[...] = (xhat * g_ref[...]).astype(BF16)

        p_ref[...] = _dot(u_ref[...], w_ref[...]).astype(BF16)

    in_specs = [pl.BlockSpec((tm, d), lambda i, k: (i, 0)),
                pl.BlockSpec((1, d), lambda i, k: (0, 0)),
                pl.BlockSpec((None, d, d), lambda i, k: (k, 0, 0))]
    out_specs = [pl.BlockSpec((tm, d), lambda i, k: (i, 0)),
                 pl.BlockSpec((None, tm, d), lambda i, k: (k, i, 0))]
    out_shape = [jax.ShapeDtypeStruct((tp, d), BF16), jax.ShapeDtypeStruct((ns, tp, d), BF16)]
    args, scratch = [h, g, w_in], []
    split = _attach_carry(carry, in_specs, args, out_specs, out_shape, scratch)
    return pl.pallas_call(
        body, name="win_fwd", grid=(ni, ns), in_specs=in_specs, out_specs=out_specs, out_shape=out_shape,
        scratch_shapes=scratch, compiler_params=_params(("arbitrary", "arbitrary")),
    )(*args)


def _win_bwd_shard(k, ns, du_prev, dpb, dug, u, w_in, h1, g, dh2):
    tp, d = h1.shape
    dh = d // 2
    tm = _pick_tile(tp, (768, 512, 256))
    first, last = k == 0, k == ns - 1

    def body(*refs):
        refs = list(refs)
        acc_in = None if first else refs.pop(0)
        dug_ref = refs.pop(0) if first else None
        dp_ref, u_ref, w_ref = refs[:3]
        refs = refs[3:]
        if last:
            h_ref, g_ref, dh2_ref, acc_out, dw_ref, dg_ref = refs
        else:
            acc_out, dw_ref = refs
        i = pl.program_id(0)

        @pl.when(i == 0)
        def _():
            dw_ref[...] = jnp.zeros_like(dw_ref)
            if last:
                dg_ref[...] = jnp.zeros_like(dg_ref)

        dp = dp_ref[...]
        if first:
            dp = jnp.concatenate([dug_ref[...], dp[:, dh:]], axis=1)
        dw_ref[...] += _dot_tn(u_ref[...], dp)
        du = _dot_nt(dp, w_ref[...])
        if not first:
            du = du + acc_in[...]
        if last:
            xhat, r = _rms_stats(h_ref[...])
            dg_ref[...] += jnp.sum(du * xhat, axis=0, keepdims=True)
            acc_out[...] = dh2_ref[...] + _rms_bwd(xhat, r, g_ref[...], du)
        else:
            acc_out[...] = du

    row_spec = pl.BlockSpec((tm, d), lambda i: (i, 0))
    vec_spec = pl.BlockSpec((1, d), lambda i: (0, 0))
    in_specs = [pl.BlockSpec((None, tm, d), lambda i: (k, i, 0)), row_spec,
                pl.BlockSpec((None, d, d), lambda i: (k, 0, 0))]
    args = [dpb, u, w_in]
    if first:
        in_specs.insert(0, pl.BlockSpec((tm, dh), lambda i: (i, 0)))
        args.insert(0, dug)
    else:
        in_specs.insert(0, row_spec)
        args.insert(0, du_prev)
    out_specs = [row_spec, pl.BlockSpec((d, d), lambda i: (0, 0))]
    out_shape = [jax.ShapeDtypeStruct((tp, d), F32), jax.ShapeDtypeStruct((d, d), F32)]
    if last:
        in_specs += [row_spec, vec_spec, row_spec]
        args += [h1, g, dh2]
        out_specs.append(vec_spec)
        out_shape.append(jax.ShapeDtypeStruct((1, d), F32))
    return pl.pallas_call(
        body, name=f"win_bwd_{k}", grid=(tp // tm,), in_specs=in_specs, out_specs=out_specs,
        out_shape=out_shape, compiler_params=_params(("arbitrary",)),
    )(*args)


def _win_bwd(dpb, dug, u, w_in, h1, g, dh2):
    ns = w_in.shape[0]
    acc, dws, dg = None, [], None
    for k in range(ns):
        outs = _win_bwd_shard(k, ns, acc, dpb, dug, u, w_in, h1, g, dh2)
        acc = outs[0]
        dws.append(outs[1])
        if k == ns - 1:
            dg = outs[2]
    return acc, dws, dg


def _cmul(ar, ai, br, bi):
    return ar * br - ai * bi, ar * bi + ai * br


def _scan_rows(j, sub):
    return pl.ds(j * SUBLANES, SUBLANES)


def _permute_rows(src_ref, dst_ref, sub):
    for j in range(sub):
        dst_ref[pl.ds(j * SUBLANES, SUBLANES), :] = src_ref[pl.ds(j, SUBLANES, stride=sub), :]


def _unpermute_rows(src_ref, dst_ref, sub):
    for j in range(sub):
        dst_ref[pl.ds(j, SUBLANES, stride=sub), :] = src_ref[pl.ds(j * SUBLANES, SUBLANES), :]


def _local_scan(x_ref, lr, li, w, sub, reverse):
    hr = jnp.zeros((SUBLANES, w), F32)
    hi = jnp.zeros((SUBLANES, w), F32)
    order = range(sub - 1, -1, -1) if reverse else range(sub)
    for j in order:
        xr = x_ref[_scan_rows(j, sub), pl.ds(0, w)]
        xi = x_ref[_scan_rows(j, sub), pl.ds(w, w)]
        if reverse:
            hr, hi = lr * hr + li * hi + xr, lr * hi - li * hr + xi
        else:
            hr, hi = lr * hr - li * hi + xr, lr * hi + li * hr + xi
        x_ref[_scan_rows(j, sub), pl.ds(0, w)] = hr
        x_ref[_scan_rows(j, sub), pl.ds(w, w)] = hi
    return hr, hi


def _entering_states(er, ei, fr, fi, pow_ref, w, sub, reverse):
    lane = lax.broadcasted_iota(jnp.int32, (SUBLANES, w), 0)
    if reverse:
        edge, shift1 = SUBLANES - 1, SUBLANES - 1
    else:
        edge, shift1 = 0, 1
    zr = jnp.where(lane == edge, pltpu.roll(fr, shift1, 0), pltpu.roll(er, shift1, 0))
    zi = jnp.where(lane == edge, pltpu.roll(fi, shift1, 0), pltpu.roll(ei, shift1, 0))
    for step, row in ((1, sub - 1), (2, sub), (4, sub + 1)):
        ar = pow_ref[pl.ds(row, 1), pl.ds(0, w)]
        ai = pow_ref[pl.ds(row, 1), pl.ds(w, w)]
        if reverse:
            ai = -ai
            keep = lane < SUBLANES - step
            sr = jnp.where(keep, pltpu.roll(zr, SUBLANES - step, 0), 0.0)
            si = jnp.where(keep, pltpu.roll(zi, SUBLANES - step, 0), 0.0)
        else:
            keep = lane >= step
            sr = jnp.where(keep, pltpu.roll(zr, step, 0), 0.0)
            si = jnp.where(keep, pltpu.roll(zi, step, 0), 0.0)
        pr, pi = _cmul(ar, ai, sr, si)
        zr, zi = zr + pr, zi + pi
    ar = pow_ref[pl.ds(sub - 1, 1), pl.ds(0, w)]
    ai = pow_ref[pl.ds(sub - 1, 1), pl.ds(w, w)]
    if reverse:
        ai = -ai
    pr, pi = _cmul(ar, ai, zr, zi)
    return zr, zi, er + pr, ei + pi


def _scan_fwd(p, mb, mc, powt, dskip):
    _, tp, d = p.shape
    nb, cb, w2 = mb.shape
    w = w2 // 2
    q = SCAN_TILE
    sub = q // SUBLANES
    nt = tp // q
    ds = d // 2

    def body(ug_ref, mb_ref, mc_ref, pow_ref, d_ref, y_ref, bnd_ref, x_scr, carry, nat, perm):
        t = pl.program_id(1)

        @pl.when(t == 0)
        def _():
            carry[...] = jnp.zeros_like(carry)

        ugf = ug_ref[...].astype(F32)
        nat[...] = ugf
        _permute_rows(nat, perm, sub)
        x_scr[...] = _dot(perm[...].astype(BF16), mb_ref[...])
        lr = jnp.broadcast_to(pow_ref[pl.ds(0, 1), pl.ds(0, w)], (SUBLANES, w))
        li = jnp.broadcast_to(pow_ref[pl.ds(0, 1), pl.ds(w, w)], (SUBLANES, w))
        er, ei = _local_scan(x_scr, lr, li, w, sub, False)
        zr, zi, fr, fi = _entering_states(er, ei, carry[:, pl.ds(0, w)], carry[:, pl.ds(w, w)],
                                          pow_ref, w, sub, False)
        carry[:, pl.ds(0, w)] = fr
        carry[:, pl.ds(w, w)] = fi
        bnd_ref[:, pl.ds(0, w)] = fr
        bnd_ref[:, pl.ds(w, w)] = fi
        for j in range(sub):
            pr = pow_ref[pl.ds(j, 1), pl.ds(0, w)]
            pi = pow_ref[pl.ds(j, 1), pl.ds(w, w)]
            cr, ci = _cmul(pr, pi, zr, zi)
            x_scr[_scan_rows(j, sub), pl.ds(0, w)] += cr
            x_scr[_scan_rows(j, sub), pl.ds(w, w)] += ci
        hb = x_scr[...].astype(BF16)
        perm[...] = _dot_nt(hb, mc_ref[...])
        _unpermute_rows(perm, nat, sub)
        y_ref[...] = nat[...] + d_ref[...] * ugf

    in_specs = [pl.BlockSpec((None, q, cb), lambda b, t: (0, t, b)),
                pl.BlockSpec((None, cb, w2), lambda b, t: (b, 0, 0)),
                pl.BlockSpec((None, cb, w2), lambda b, t: (b, 0, 0)),
                pl.BlockSpec((None, powt.shape[1], w2), lambda b, t: (b, 0, 0)),
                pl.BlockSpec((1, cb), lambda b, t: (0, b))]
    out_specs = [pl.BlockSpec((q, cb), lambda b, t: (t, b)),
                 pl.BlockSpec((None, None, SUBLANES, w2), lambda b, t: (b, t, 0, 0))]
    out_shape = [jax.ShapeDtypeStruct((tp, ds), F32), jax.ShapeDtypeStruct((nb, nt, SUBLANES, w2), F32)]
    scratch = [pltpu.VMEM((q, w2), F32), pltpu.VMEM((SUBLANES, w2), F32),
               pltpu.VMEM((q, cb), F32), pltpu.VMEM((q, cb), F32)]
    return pl.pallas_call(
        body, name="s5_scan_fwd", grid=(nb, nt), in_specs=in_specs, out_specs=out_specs,
        out_shape=out_shape, scratch_shapes=scratch, compiler_params=_params(("arbitrary", "arbitrary")),
    )(p, mb, mc, powt, dskip)


def _scan_bwd(p, dy, mb, mc, powt, dskip, bnd):
    _, tp, d = p.shape
    nb, cb, w2 = mb.shape
    w = w2 // 2
    q = SCAN_TILE
    sub = q // SUBLANES
    nt = tp // q
    ds = d // 2

    def body(ug_ref, dy_ref, mb_ref, mc_ref, pow_ref, d_ref, bnd_ref,
             dug_ref, dmb_ref, dmc_ref, dlam_ref, dd_ref, x_scr, y_scr, gcarry, nat, perm):
        t = pl.program_id(1)
        tt = nt - 1 - t

        @pl.when(t == 0)
        def _():
            gcarry[...] = jnp.zeros_like(gcarry)
            dmb_ref[...] = jnp.zeros_like(dmb_ref)
            dmc_ref[...] = jnp.zeros_like(dmc_ref)
            dlam_ref[...] = jnp.zeros_like(dlam_ref)
            dd_ref[...] = jnp.zeros_like(dd_ref)

        ugf = ug_ref[...].astype(F32)
        dyf = dy_ref[...].astype(F32)
        dd_ref[...] += jnp.sum((dyf * ugf).reshape(q // SUBLANES, SUBLANES, cb), axis=0)
        nat[...] = ugf
        _permute_rows(nat, perm, sub)
        ug = perm[...].astype(BF16)
        nat[...] = dyf
        _permute_rows(nat, perm, sub)
        dyb = perm[...].astype(BF16)
        lr = jnp.broadcast_to(pow_ref[pl.ds(0, 1), pl.ds(0, w)], (SUBLANES, w))
        li = jnp.broadcast_to(pow_ref[pl.ds(0, 1), pl.ds(w, w)], (SUBLANES, w))

        x_scr[...] = _dot(ug, mb_ref[...])
        er, ei = _local_scan(x_scr, lr, li, w, sub, False)
        first = tt == 0
        pfr = jnp.where(first, 0.0, bnd_ref[:, pl.ds(0, w)])
        pfi = jnp.where(first, 0.0, bnd_ref[:, pl.ds(w, w)])
        hzr, hzi, _, _ = _entering_states(er, ei, pfr, pfi, pow_ref, w, sub, False)
        for j in range(sub):
            pr = pow_ref[pl.ds(j, 1), pl.ds(0, w)]
            pi = pow_ref[pl.ds(j, 1), pl.ds(w, w)]
            cr, ci = _cmul(pr, pi, hzr, hzi)
            x_scr[_scan_rows(j, sub), pl.ds(0, w)] += cr
            x_scr[_scan_rows(j, sub), pl.ds(w, w)] += ci

        y_scr[...] = _dot(dyb, mc_ref[...])
        er, ei = _local_scan(y_scr, lr, li, w, sub, True)
        gzr, gzi, fr, fi = _entering_states(er, ei, gcarry[:, pl.ds(0, w)], gcarry[:, pl.ds(w, w)],
                                            pow_ref, w, sub, True)
        gcarry[:, pl.ds(0, w)] = fr
        gcarry[:, pl.ds(w, w)] = fi
        accr = jnp.zeros((SUBLANES, w), F32)
        acci = jnp.zeros((SUBLANES, w), F32)
        for j in range(sub):
            pr = pow_ref[pl.ds(sub - 1 - j, 1), pl.ds(0, w)]
            pi = pow_ref[pl.ds(sub - 1 - j, 1), pl.ds(w, w)]
            cr, ci = _cmul(pr, -pi, gzr, gzi)
            gr = y_scr[_scan_rows(j, sub), pl.ds(0, w)] + cr
            gi = y_scr[_scan_rows(j, sub), pl.ds(w, w)] + ci
            y_scr[_scan_rows(j, sub), pl.ds(0, w)] = gr
            y_scr[_scan_rows(j, sub), pl.ds(w, w)] = gi
            if j == 0:
                hpr, hpi = hzr, hzi
            else:
                hpr = x_scr[_scan_rows(j - 1, sub), pl.ds(0, w)]
                hpi = x_scr[_scan_rows(j - 1, sub), pl.ds(w, w)]
            accr += hpr * gr + hpi * gi
            acci += hpr * gi - hpi * gr
        dlam_ref[:, pl.ds(0, w)] += accr
        dlam_ref[:, pl.ds(w, w)] += acci

        hb = x_scr[...].astype(BF16)
        gb = y_scr[...].astype(BF16)
        dmc_ref[...] += _dot_tn(dyb, hb)
        dmb_ref[...] += _dot_tn(ug, gb)
        perm[...] = _dot_nt(gb, mb_ref[...])
        _unpermute_rows(perm, nat, sub)
        dug_ref[...] = (nat[...] + d_ref[...] * dyf).astype(BF16)

    blk = lambda b, t: (b, 0, 0)
    return pl.pallas_call(
        body, name="s5_scan_bwd", grid=(nb, nt),
        in_specs=[pl.BlockSpec((None, q, cb), lambda b, t: (0, nt - 1 - t, b)),
                  pl.BlockSpec((q, cb), lambda b, t: (nt - 1 - t, b)),
                  pl.BlockSpec((None, cb, w2), blk),
                  pl.BlockSpec((None, cb, w2), blk),
                  pl.BlockSpec((None, powt.shape[1], w2), blk),
                  pl.BlockSpec((1, cb), lambda b, t: (0, b)),
                  pl.BlockSpec((None, None, SUBLANES, w2),
                               lambda b, t: (b, jnp.maximum(nt - 2 - t, 0), 0, 0))],
        out_specs=[pl.BlockSpec((q, cb), lambda b, t: (nt - 1 - t, b)),
                   pl.BlockSpec((None, cb, w2), blk),
                   pl.BlockSpec((None, cb, w2), blk),
                   pl.BlockSpec((None, SUBLANES, w2), blk),
                   pl.BlockSpec((SUBLANES, cb), lambda b, t: (0, b))],
        out_shape=[jax.ShapeDtypeStruct((tp, ds), BF16),
                   jax.ShapeDtypeStruct((nb, cb, w2), F32),
                   jax.ShapeDtypeStruct((nb, cb, w2), F32),
                   jax.ShapeDtypeStruct((nb, SUBLANES, w2), F32),
                   jax.ShapeDtypeStruct((SUBLANES, ds), F32)],
        scratch_shapes=[pltpu.VMEM((q, w2), F32), pltpu.VMEM((q, w2), F32),
                        pltpu.VMEM((SUBLANES, w2), F32), pltpu.VMEM((q, cb), F32), pltpu.VMEM((q, cb), F32)],
        compiler_params=_params(("arbitrary", "arbitrary")),
    )(p, dy, mb, mc, powt, dskip, bnd)


HALO = 16


def _mix_tile(ys5, p0, p1, p2, p3, prev_cin, cw, bgate, wglu, wco, d):
    dh = d // 2
    tm = ys5.shape[0]
    v = p0[:, dh:].astype(F32)
    gbr = p1[:, :dh].astype(F32)
    gcr = p1[:, dh:].astype(F32)
    gact = _gelu(ys5).astype(BF16)
    z = _dot(gact, wglu)
    z1, z2 = z[:, :d], z[:, d:]
    sg = _sigmoid(z2)
    y_ssm = z1 * sg
    cin = gcr * v
    ext = jnp.concatenate([cin, prev_cin], axis=0)
    r1 = pltpu.roll(ext, 1, 0)[:tm]
    r2 = pltpu.roll(ext, 2, 0)[:tm]
    cv = cw[2] * cin + cw[1] * r1 + cw[0] * r2
    cg = (gbr * cv).astype(BF16)
    y_conv = _dot(cg, wco)
    g_s = _sigmoid(p2.astype(F32) + bgate[:, :d])
    g_c = _sigmoid(p3.astype(F32) + bgate[:, d:])
    mixed = g_s * y_ssm + g_c * y_conv
    return dict(v=v, gb=gbr, gc=gcr, gact=gact, z1=z1, sg=sg, y_ssm=y_ssm, cin=cin, r1=r1, r2=r2,
                cv=cv, cg=cg, y_conv=y_conv, g_s=g_s, g_c=g_c, mixed=mixed)


def _mix_fwd(h1, ys5, p, cw, bgate, wglu, wco, wo, carry=None):
    tp, d = h1.shape
    dh = d // 2
    tm = ROW_ALIGN
    ni = tp // tm

    def body(*refs):
        refs, phases = split(refs)
        (h_ref, y_ref, p0_ref, p1_ref, p2_ref, p3_ref, cw_ref, bg_ref, wglu_ref, wco_ref, wo_ref,
         o_ref, prev) = refs
        _run_phases(phases, carry, pl.program_id(0), ni)

        @pl.when(pl.program_id(0) == 0)
        def _():
            prev[...] = jnp.zeros_like(prev)

        cw = [cw_ref[pl.ds(t, 1), :] for t in range(3)]
        f = _mix_tile(y_ref[...], p0_ref[...], p1_ref[...], p2_ref[...], p3_ref[...], prev[...],
                      cw, bg_ref[...], wglu_ref[...], wco_ref[...], d)
        prev[...] = f["cin"][tm - HALO:, :]
        o_ref[...] = h_ref[...] + _dot(f["mixed"].astype(BF16), wo_ref[...])

    row = pl.BlockSpec((tm, d), lambda i: (i, 0))
    full = lambda a: pl.BlockSpec(a.shape, lambda i: (0,) * a.ndim)
    pk = lambda k: pl.BlockSpec((None, tm, d), lambda i, k=k: (k, i, 0))
    in_specs = [row, pl.BlockSpec((tm, dh), lambda i: (i, 0)), pk(0), pk(1), pk(2), pk(3),
                full(cw), full(bgate), full(wglu), full(wco), full(wo)]
    out_specs, out_shape = [row], [jax.ShapeDtypeStruct((tp, d), F32)]
    args, scratch = [h1, ys5, p, p, p, p, cw, bgate, wglu, wco, wo], [pltpu.VMEM((HALO, dh), F32)]
    split = _attach_carry(carry, in_specs, args, out_specs, out_shape, scratch)
    return pl.pallas_call(
        body, name="mix_fwd", grid=(ni,), in_specs=in_specs, out_specs=out_specs, out_shape=out_shape,
        scratch_shapes=scratch, compiler_params=_params(("arbitrary",)),
    )(*args)


def _mix_bwd(dh2, ys5, p, cw, bgate, wglu, wco, wo):
    tp, d = dh2.shape
    dh = d // 2
    tm = ROW_ALIGN
    ni = tp // tm
    hb = tm // HALO

    def body(dh_ref, y_ref, p0_ref, p1_ref, p2_ref, p3_ref, h0_ref, h1_ref,
             cw_ref, bg_ref, wglu_ref, wco_ref, wo_ref,
             dys_ref, dpb_ref, dwo_ref, dwglu_ref, dwco_ref, dcw_ref, dbg_ref, nxt):
        i = pl.program_id(0)
        tt = ni - 1 - i

        @pl.when(i == 0)
        def _():
            nxt[...] = jnp.zeros_like(nxt)
            dwo_ref[...] = jnp.zeros_like(dwo_ref)
            dwglu_ref[...] = jnp.zeros_like(dwglu_ref)
            dwco_ref[...] = jnp.zeros_like(dwco_ref)
            dcw_ref[...] = jnp.zeros_like(dcw_ref)
            dbg_ref[...] = jnp.zeros_like(dbg_ref)

        cw = [cw_ref[pl.ds(t, 1), :] for t in range(3)]
        prev_cin = h1_ref[:, dh:].astype(F32) * h0_ref[:, dh:].astype(F32)
        prev_cin = jnp.where(tt == 0, 0.0, prev_cin)
        ys5 = y_ref[...]
        f = _mix_tile(ys5, p0_ref[...], p1_ref[...], p2_ref[...], p3_ref[...], prev_cin,
                      cw, bg_ref[...], wglu_ref[...], wco_ref[...], d)
        dhb = dh_ref[...].astype(BF16)
        dmixed = _dot_nt(dhb, wo_ref[...])
        dwo_ref[...] += _dot_tn(f["mixed"].astype(BF16), dhb)

        g_s, g_c, sg = f["g_s"], f["g_c"], f["sg"]
        dy_ssm = dmixed * g_s
        dy_conv = dmixed * g_c
        dp2 = dmixed * f["y_ssm"] * g_s * (1.0 - g_s)
        dp3 = dmixed * f["y_conv"] * g_c * (1.0 - g_c)
        dbg_ref[:, pl.ds(0, d)] += jnp.sum(dp2, axis=0, keepdims=True)
        dbg_ref[:, pl.ds(d, d)] += jnp.sum(dp3, axis=0, keepdims=True)

        dz = jnp.concatenate([dy_ssm * sg, dy_ssm * f["z1"] * sg * (1.0 - sg)], axis=1).astype(BF16)
        dwglu_ref[...] += _dot_tn(f["gact"], dz)
        dys_ref[...] = (_dot_nt(dz, wglu_ref[...]) * _gelu_grad(ys5)).astype(BF16)

        dycb = dy_conv.astype(BF16)
        dwco_ref[...] += _dot_tn(f["cg"], dycb)
        dcg = _dot_nt(dycb, wco_ref[...])
        dgb = dcg * f["cv"]
        dcv = dcg * f["gb"]
        ext = jnp.concatenate([dcv, nxt[...]], axis=0)
        n1 = pltpu.roll(ext, tm + HALO - 1, 0)[:tm]
        n2 = pltpu.roll(ext, tm + HALO - 2, 0)[:tm]
        nxt[...] = dcv[:HALO, :]
        dcin = cw[2] * dcv + cw[1] * n1 + cw[0] * n2
        dcw_ref[pl.ds(0, 1), :] += jnp.sum(dcv * f["r2"], axis=0, keepdims=True)
        dcw_ref[pl.ds(1, 1), :] += jnp.sum(dcv * f["r1"], axis=0, keepdims=True)
        dcw_ref[pl.ds(2, 1), :] += jnp.sum(dcv * f["cin"], axis=0, keepdims=True)
        dgc = dcin * f["v"]
        dv = dcin * f["gc"]
        dpb_ref[0] = jnp.concatenate([jnp.zeros_like(dv), dv], axis=1).astype(BF16)
        dpb_ref[1] = jnp.concatenate([dgb, dgc], axis=1).astype(BF16)
        dpb_ref[2] = dp2.astype(BF16)
        dpb_ref[3] = dp3.astype(BF16)

    rev = lambda i: ni - 1 - i
    row = pl.BlockSpec((tm, d), lambda i: (rev(i), 0))
    half = pl.BlockSpec((tm, dh), lambda i: (rev(i), 0))
    full = lambda a: pl.BlockSpec(a.shape, lambda i: (0,) * a.ndim)
    pk = lambda k: pl.BlockSpec((None, tm, d), lambda i, k=k: (k, rev(i), 0))
    halo = lambda k: pl.BlockSpec((None, HALO, d), lambda i, k=k: (k, jnp.maximum(rev(i) * hb - 1, 0), 0))
    acc = lambda shape: pl.BlockSpec(shape, lambda i: (0,) * len(shape))
    return pl.pallas_call(
        body, name="mix_bwd", grid=(ni,),
        in_specs=[row, half, pk(0), pk(1), pk(2), pk(3), halo(0), halo(1),
                  full(cw), full(bgate), full(wglu), full(wco), full(wo)],
        out_specs=[half, pl.BlockSpec((4, tm, d), lambda i: (0, rev(i), 0)),
                   acc((d, d)), acc((dh, 2 * d)), acc((dh, d)), acc((SUBLANES, dh)), acc((1, 2 * d))],
        out_shape=[jax.ShapeDtypeStruct((tp, dh), BF16), jax.ShapeDtypeStruct((4, tp, d), BF16),
                   jax.ShapeDtypeStruct((d, d), F32), jax.ShapeDtypeStruct((dh, 2 * d), F32),
                   jax.ShapeDtypeStruct((dh, d), F32), jax.ShapeDtypeStruct((SUBLANES, dh), F32),
                   jax.ShapeDtypeStruct((1, 2 * d), F32)],
        scratch_shapes=[pltpu.VMEM((HALO, dh), F32)],
        compiler_params=_params(("arbitrary",)),
    )(dh2, ys5, p, p, p, p, p, p, cw, bgate, wglu, wco, wo)


ANY = pl.BlockSpec(memory_space=pl.ANY)


def _position():
    return lax.axis_index("x"), lax.axis_index("y"), lax.axis_index("c")


def _remote(src, dst, ssem, rsem, dev):
    return pltpu.make_async_remote_copy(src_ref=src, dst_ref=dst, send_sem=ssem, recv_sem=rsem,
                                        device_id=dev, device_id_type=MESH)


def _cast_piece(w, pos):
    rows, cols = w.shape
    r2 = rows // 2

    def body(pos_ref, w_ref, o_ref):
        o_ref[...] = w_ref[...].astype(BF16)

    return pl.pallas_call(
        body, name="cast_piece",
        grid_spec=pltpu.PrefetchScalarGridSpec(
            num_scalar_prefetch=1, grid=(1,),
            in_specs=[pl.BlockSpec((r2, cols), lambda i, pos: (pos[2], 0))],
            out_specs=pl.BlockSpec((r2, cols), lambda i, pos: (0, 0))),
        out_shape=jax.ShapeDtypeStruct((r2, cols), BF16),
        compiler_params=_params(("arbitrary",)),
    )(pos, w)


class _Carry:
    def __init__(self, name, arrays, out_shapes, nsem, nlsem, make, fracs):
        self.name, self.arrays, self.out_shapes = name, list(arrays), list(out_shapes)
        self.nsem, self.nlsem, self.make, self.fracs = nsem, max(nlsem, 1), make, fracs


def _carry_scratch(carry):
    return [pltpu.SemaphoreType.DMA((carry.nsem,)), pltpu.SemaphoreType.DMA((carry.nsem,)),
            pltpu.SemaphoreType.DMA((carry.nlsem,))]


def _run_carry(carry):
    na, no = len(carry.arrays), len(carry.out_shapes)

    def body(*refs):
        for phase in carry.make(refs[:na], refs[na:na + no], *refs[na + no:]):
            phase()

    return pl.pallas_call(
        body, name=carry.name, in_specs=[ANY] * na, out_specs=[ANY] * no, out_shape=carry.out_shapes,
        scratch_shapes=_carry_scratch(carry),
    )(*carry.arrays)


def _attach_carry(carry, in_specs, args, out_specs, out_shape, scratch):
    nhi, nho, nhs = len(in_specs), len(out_specs), len(scratch)
    if carry is None:
        return lambda refs: (list(refs), [])
    na, no = len(carry.arrays), len(carry.out_shapes)
    in_specs += [ANY] * na
    args += carry.arrays
    out_specs += [ANY] * no
    out_shape += carry.out_shapes
    scratch += _carry_scratch(carry)

    def split(refs):
        refs = list(refs)
        o = nhi + na
        host = refs[:nhi] + refs[o:o + nho] + refs[o + nho + no:o + nho + no + nhs]
        sems = refs[o + nho + no + nhs:]
        return host, carry.make(refs[nhi:o], refs[o + nho:o + nho + no], *sems)

    return split


def _run_phases(phases, carry, step, total):
    for phase, frac in zip(phases, carry.fracs if carry is not None else ()):
        pl.when(step == int(round(frac * (total - 1))))(phase)


def _allgather_carry(name, pieces, smalls):
    n, ns = len(pieces), len(smalls)
    per = 14
    n_big = per * n

    def make(ins, outs, ssem, rsem, lsem):
        pin, sin = ins[:n], ins[n:]
        wall, sall = outs[:n], outs[n:]
        x, y, c = _position()
        xnb, ynb, sib = (1 - x, y, c), (x, 1 - y, c), (x, y, 1 - c)
        chips = [(1 - x, y), (x, 1 - y), (1 - x, 1 - y)]
        r4 = [p.shape[0] // 2 for p in pin]
        own = lambda i, h: pin[i].at[pl.ds(h * r4[i], r4[i]), :]
        slot = lambda i, xx, yy, cc, h: wall[i].at[xx, yy, cc, h]
        cp = lambda src, dst, s, dev: _remote(src, dst, ssem.at[s], rsem.at[s], dev)
        to_sib = lambda i, xx, yy, h: cp(slot(i, xx, yy, c, h), slot(i, xx, yy, c, h),
                                         per * i + 6 + 4 * xx + 2 * yy + h, sib)

        def local():
            cps = [pltpu.make_async_copy(own(i, h), slot(i, x, y, c, h), lsem.at[2 * i + h])
                   for i in range(n) for h in range(2)]
            return cps + [pltpu.make_async_copy(sin[i], sall[i].at[2 * x + y], lsem.at[2 * n + i])
                          for i in range(ns)]

        def small(px, py, j, i, landing):
            s = n_big + j * ns + i
            return cp(sin[i], sall[i].at[landing], s, (px, py, c))

        def first_hop():
            for lc in local():
                lc.start()
            for j, (px, py) in enumerate(chips):
                for i in range(ns):
                    small(px, py, j, i, 2 * x + y).start()
            for i in range(n):
                cp(own(i, 0), slot(i, x, y, c, 0), per * i, xnb).start()
                cp(own(i, 1), slot(i, x, y, c, 1), per * i + 1, ynb).start()
                for h in range(2):
                    cp(own(i, h), slot(i, x, y, c, h), per * i + 6 + 4 * x + 2 * y + h, sib).start()

        def second_hop():
            for lc in local():
                lc.wait()
            for i in range(n):
                cp(slot(i, 1 - x, y, c, 0), slot(i, 1 - x, y, c, 0), per * i, xnb).wait_recv()
                cp(slot(i, x, 1 - y, c, 1), slot(i, x, 1 - y, c, 1), per * i + 1, ynb).wait_recv()
                for j in range(2):
                    cp(slot(i, j, y, c, 0), slot(i, j, y, c, 0), per * i + 2 + j, ynb).start()
                    cp(slot(i, x, j, c, 1), slot(i, x, j, c, 1), per * i + 4 + j, xnb).start()
                to_sib(i, 1 - x, y, 0).start()
                to_sib(i, x, 1 - y, 1).start()

        def last_to_sibling():
            for i in range(n):
                for j in range(2):
                    cp(slot(i, j, 1 - y, c, 0), slot(i, j, 1 - y, c, 0), per * i + 2 + j, ynb).wait_recv()
                    cp(slot(i, 1 - x, j, c, 1), slot(i, 1 - x, j, c, 1), per * i + 4 + j, xnb).wait_recv()
                    to_sib(i, j, 1 - y, 0).start()
                    to_sib(i, 1 - x, j, 1).start()

        def finish():
            for i in range(n):
                for xx in range(2):
                    for yy in range(2):
                        for h in range(2):
                            s = per * i + 6 + 4 * xx + 2 * yy + h
                            cp(slot(i, xx, yy, 1 - c, h), slot(i, xx, yy, 1 - c, h), s, sib).wait_recv()
                            to_sib(i, xx, yy, h).wait_send()
                cp(own(i, 0), slot(i, x, y, c, 0), per * i, xnb).wait_send()
                cp(own(i, 1), slot(i, x, y, c, 1), per * i + 1, ynb).wait_send()
                for j in range(2):
                    cp(slot(i, j, y, c, 0), slot(i, j, y, c, 0), per * i + 2 + j, ynb).wait_send()
                    cp(slot(i, x, j, c, 1), slot(i, x, j, c, 1), per * i + 4 + j, xnb).wait_send()
            for j, (px, py) in enumerate(chips):
                for i in range(ns):
                    small(px, py, j, i, 2 * px + py).wait_recv()
                    small(px, py, j, i, 2 * x + y).wait_send()

        return [first_hop, second_hop, last_to_sibling, finish]

    out_shapes = [jax.ShapeDtypeStruct((2, 2, 2, 2, a.shape[0] // 2, a.shape[1]), a.dtype) for a in pieces]
    out_shapes += [jax.ShapeDtypeStruct((4,) + a.shape, a.dtype) for a in smalls]
    return _Carry(name, list(pieces) + list(smalls), out_shapes, n_big + 3 * ns, 2 * n + ns, make,
                  (0.0, 0.23, 0.73, 1.0))


def _exchange_carry(name, arrays, out_shapes, plan):
    count = plan([None] * len(arrays), [None] * len(out_shapes), None)

    def make(ins, outs, ssem, rsem, lsem):
        def copies():
            return [_remote(src, dst, ssem.at[j], rsem.at[j], peer)
                    for j, (src, dst, peer) in enumerate(plan(ins, outs, _position()))]

        def start():
            for c in copies():
                c.start()

        def wait():
            for c in copies():
                c.wait()

        return [start, wait]

    return _Carry(name, arrays, out_shapes, count, 0, make, (0.0, 1.0))


class _Grad:
    def __init__(self, arrs, kind, shard_shape):
        self.arrs, self.kind = list(arrs), kind
        self.rows, self.cols = shard_shape
        self.r2 = self.rows // 2

    def view(self, refs, k, h):
        r2 = self.r2
        if self.kind == "list":
            return refs[k].at[pl.ds(h * r2, r2), :]
        if self.kind == "stacked":
            return refs[0].at[k, pl.ds(h * r2, r2), :]
        if self.kind == "col":
            return refs[0].at[pl.ds(h * r2, r2), pl.ds(k * self.cols, self.cols)]
        return refs[0].at[pl.ds((2 * k + h) * r2, r2), :]

    def add_half(self, recv, pos):
        r2, cols = self.r2, self.cols
        n_in = len(self.arrs)
        tr = _row_tile(r2, cols)
        nt = r2 // tr

        def body(pos_ref, *refs):
            m_refs, (r_ref, of_ref, ob_ref) = refs[:n_in], refs[n_in:]
            mine = m_refs[0][...]
            for kk in range(1, n_in):
                mine = jnp.where(pl.program_id(1) == kk, m_refs[kk][...], mine)
            s = mine + r_ref[...]
            of_ref[...] = s
            ob_ref[...] = s.astype(BF16)

        row = lambda t, pos: pos[2] * nt + t
        if self.kind == "list":
            specs = [pl.BlockSpec((tr, cols), lambda t, k, pos: (row(t, pos), 0))] * n_in
        elif self.kind == "stacked":
            specs = [pl.BlockSpec((None, tr, cols), lambda t, k, pos: (k, row(t, pos), 0))]
        elif self.kind == "col":
            specs = [pl.BlockSpec((tr, cols), lambda t, k, pos: (row(t, pos), k))]
        else:
            specs = [pl.BlockSpec((tr, cols), lambda t, k, pos: (2 * k * nt + row(t, pos), 0))]
        blk = pl.BlockSpec((None, tr, cols), lambda t, k, pos: (k, t, 0))
        return pl.pallas_call(
            body, name="rs_add_c",
            grid_spec=pltpu.PrefetchScalarGridSpec(
                num_scalar_prefetch=1, grid=(nt, 4), in_specs=specs + [blk], out_specs=[blk, blk]),
            out_shape=[jax.ShapeDtypeStruct((4, r2, cols), F32), jax.ShapeDtypeStruct((4, r2, cols), BF16)],
            compiler_params=_params(("arbitrary", "arbitrary")),
        )(pos, *self.arrs, recv)


def _row_tile(rows, cols):
    fits = [t for t in range(16, rows + 1, 16) if rows % t == 0 and t * cols * 4 <= 2 * 1024 * 1024]
    return max(fits) if fits else rows


def _adamw_math(w, g, m, v):
    m = ADAM_B1 * m + (1.0 - ADAM_B1) * g
    v = ADAM_B2 * v + (1.0 - ADAM_B2) * (g * g)
    m_hat = m / (1.0 - ADAM_B1 ** ADAM_STEP)
    v_hat = v / (1.0 - ADAM_B2 ** ADAM_STEP)
    delta = -ADAM_LR * (m_hat / (jnp.sqrt(v_hat) + ADAM_EPS) + ADAM_WD * w)
    return delta, m, v


def _adamw_big(w, m, v, own, sib, pos):
    rows, cols = w.shape
    r2 = rows // 2

    tr = _row_tile(r2, cols)
    nt = r2 // tr

    def body(pos_ref, w_ref, m_ref, v_ref, own_ref, sib_ref, g_ref, d_ref, nm_ref, nv_ref):
        h = pl.program_id(0)
        g = jnp.where(h == pos_ref[2], own_ref[...], sib_ref[...])
        g_ref[...] = g
        d_ref[...], nm_ref[...], nv_ref[...] = _adamw_math(w_ref[...], g, m_ref[...], v_ref[...])

    half = pl.BlockSpec((tr, cols), lambda h, t, pos: (h * nt + t, 0))
    piece = pl.BlockSpec((tr, cols), lambda h, t, pos: (t, 0))
    out = jax.ShapeDtypeStruct((rows, cols), F32)
    return pl.pallas_call(
        body, name="adamw",
        grid_spec=pltpu.PrefetchScalarGridSpec(
            num_scalar_prefetch=1, grid=(2, nt),
            in_specs=[half, half, half, piece, piece],
            out_specs=[half, half, half, half]),
        out_shape=[out, out, out, out],
        compiler_params=_params(("arbitrary", "arbitrary")),
    )(pos, w, m, v, own, sib)


def _add_hop1(s1f, recv, pos):
    _, _, r4, cols = recv.shape
    s1v = s1f.reshape(4, 2, r4, cols)

    def body(pos_ref, m_ref, r_ref, of_ref, ob_ref):
        s = m_ref[...] + r_ref[...].astype(F32)
        of_ref[...] = s
        ob_ref[...] = s.astype(BF16)

    def mine(h, j, pos):
        return (jnp.where(h == 0, 2 * j + pos[1], 2 * pos[0] + j), h, 0, 0)

    blk = pl.BlockSpec((None, None, r4, cols), lambda h, j, pos: (h, j, 0, 0))
    return pl.pallas_call(
        body, name="rs_add_1",
        grid_spec=pltpu.PrefetchScalarGridSpec(
            num_scalar_prefetch=1, grid=(2, 2),
            in_specs=[pl.BlockSpec((None, None, r4, cols), mine), blk], out_specs=[blk, blk]),
        out_shape=[jax.ShapeDtypeStruct((2, 2, r4, cols), F32), jax.ShapeDtypeStruct((2, 2, r4, cols), BF16)],
        compiler_params=_params(("arbitrary", "arbitrary")),
    )(pos, s1v, recv)


def _own_sum(s2f, recv3, pos):
    _, _, r4, cols = s2f.shape

    def body(pos_ref, s_ref, r_ref, o_ref):
        o_ref[...] = s_ref[...] + r_ref[...].astype(F32)

    blk = pl.BlockSpec((None, r4, cols), lambda h, pos: (h, 0, 0))
    return pl.pallas_call(
        body, name="own_sum",
        grid_spec=pltpu.PrefetchScalarGridSpec(
            num_scalar_prefetch=1, grid=(2,),
            in_specs=[pl.BlockSpec((None, None, r4, cols),
                                   lambda h, pos: (h, jnp.where(h == 0, pos[0], pos[1]), 0, 0)), blk],
            out_specs=blk),
        out_shape=jax.ShapeDtypeStruct((2, r4, cols), F32),
        compiler_params=_params(("arbitrary",)),
    )(pos, s2f, recv3)


def _allreduce_small(buf):
    def body(x_ref, o_ref, recv, ssem, rsem):
        x, y, c = _position()
        o_ref[...] = x_ref[...]
        for s, peer in enumerate([(x, y, 1 - c), (x, 1 - y, c), (1 - x, y, c)]):
            cp = _remote(o_ref, recv.at[s], ssem.at[s], rsem.at[s], peer)
            cp.start()
            cp.wait()
            o_ref[...] = o_ref[...] + recv[s]

    vm = pl.BlockSpec(memory_space=pltpu.VMEM)
    return pl.pallas_call(
        body, name="allreduce_small", in_specs=[vm], out_specs=vm,
        out_shape=jax.ShapeDtypeStruct(buf.shape, F32),
        scratch_shapes=[pltpu.VMEM((3,) + buf.shape, F32),
                        pltpu.SemaphoreType.DMA((3,)), pltpu.SemaphoreType.DMA((3,))],
    )(buf)


def _adamw_small(w, g, m, v):
    def body(w_ref, g_ref, m_ref, v_ref, d_ref, nm_ref, nv_ref):
        d_ref[...], nm_ref[...], nv_ref[...] = _adamw_math(w_ref[...], g_ref[...], m_ref[...], v_ref[...])

    vm = pl.BlockSpec(memory_space=pltpu.VMEM)
    out = jax.ShapeDtypeStruct(w.shape, F32)
    return pl.pallas_call(body, name="adamw_small", in_specs=[vm] * 4, out_specs=[vm] * 3,
                          out_shape=[out, out, out])(w, g, m, v)


class _ReduceScatter:
    def __init__(self, tag, grads, pos):
        self.tag, self.grads, self.pos, self.stage = tag, grads, pos, 0

    def carry(self):
        grads, n = self.grads, len(self.grads)
        r4 = [g.r2 // 2 for g in grads]

        first = [sum(len(g.arrs) for g in grads[:i]) for i in range(n)]

        def plan_c(ins, outs, p):
            if p is None:
                return 4 * n
            x, y, c = p
            mine = lambda i: ins[first[i]:first[i] + len(grads[i].arrs)]
            return [(grads[i].view(mine(i), k, 1 - c), outs[i].at[k], (x, y, 1 - c))
                    for i in range(n) for k in range(4)]

        def plan_1(ins, outs, p):
            if p is None:
                return 4 * n
            x, y, c = p
            copies = []
            for i in range(n):
                for j in range(2):
                    copies.append((ins[i].at[2 * j + (1 - y), pl.ds(0, r4[i]), :], outs[i].at[0, j],
                                   (x, 1 - y, c)))
                    copies.append((ins[i].at[2 * (1 - x) + j, pl.ds(r4[i], r4[i]), :], outs[i].at[1, j],
                                   (1 - x, y, c)))
            return copies

        def plan_2(ins, outs, p):
            if p is None:
                return 2 * n
            x, y, c = p
            copies = []
            for i in range(n):
                copies.append((ins[i].at[0, 1 - x], outs[i].at[0], (1 - x, y, c)))
                copies.append((ins[i].at[1, 1 - y], outs[i].at[1], (x, 1 - y, c)))
            return copies

        def plan_s(ins, outs, p):
            if p is None:
                return n
            x, y, c = p
            return [(ins[i], outs[i], (x, y, 1 - c)) for i in range(n)]

        shape = lambda lead, dt: [jax.ShapeDtypeStruct(lead(g) + (g.cols,), dt) for g in grads]
        stage = self.stage
        if stage == 0:
            return _exchange_carry(f"rs_{self.tag}_exchange_c", [a for g in grads for a in g.arrs],
                                   shape(lambda g: (4, g.r2), F32), plan_c)
        if stage == 1:
            return _exchange_carry(f"rs_{self.tag}_exchange_1", [s[1] for s in self.s1],
                                   shape(lambda g: (2, 2, g.r2 // 2), BF16), plan_1)
        if stage == 2:
            return _exchange_carry(f"rs_{self.tag}_exchange_2", [s[1] for s in self.s2],
                                   shape(lambda g: (2, g.r2 // 2), BF16), plan_2)
        return _exchange_carry(f"rs_{self.tag}_exchange_sibling", self.own, shape(lambda g: (g.r2,), F32), plan_s)

    def feed(self, recv):
        grads, pos = self.grads, self.pos
        if self.stage == 0:
            self.s1 = [g.add_half(r, pos) for g, r in zip(grads, recv)]
        elif self.stage == 1:
            self.s2 = [_add_hop1(s[0], r, pos) for s, r in zip(self.s1, recv)]
        elif self.stage == 2:
            self.own = [_own_sum(s[0], r, pos).reshape(g.r2, g.cols) for g, s, r in zip(grads, self.s2, recv)]
        else:
            self.sib = list(recv)
        self.stage += 1

    def run(self):
        while self.stage < 4:
            self.feed(_run_carry(self.carry()))

    def adamw(self, weights):
        return [_adamw_big(w, m, v, o, sb, self.pos) for (w, m, v), o, sb in zip(weights, self.own, self.sib)]


def _block_diag(t, nb):
    g, c, p = t.shape
    gb = g // nb
    t = t.reshape(nb, gb, c, p)
    eye = jnp.eye(gb, dtype=t.dtype)
    return jnp.einsum("bgcp,gh->bgchp", t, eye).reshape(nb, gb * c, gb * p)


def _s5_discretise(a_re, a_im, log_dt, b_re, b_im, c_re, c_im):
    g, p = a_re.shape
    nb = g // GROUPS_PER_BLOCK
    dt = jnp.exp(log_dt)[:, None]
    mag = jnp.exp(a_re * dt)
    lam_re = mag * jnp.cos(a_im * dt)
    lam_im = mag * jnp.sin(a_im * dt)
    den = a_re * a_re + a_im * a_im
    q_re = ((lam_re - 1.0) * a_re + lam_im * a_im) / den
    q_im = (lam_im * a_re - (lam_re - 1.0) * a_im) / den
    bb_re = q_re[..., None] * b_re - q_im[..., None] * b_im
    bb_im = q_re[..., None] * b_im + q_im[..., None] * b_re
    tr = lambda t: jnp.swapaxes(t, 1, 2)
    mb = jnp.concatenate([_block_diag(tr(bb_re), nb), _block_diag(tr(bb_im), nb)], axis=-1)
    mc = jnp.concatenate([_block_diag(c_re, nb), -_block_diag(c_im, nb)], axis=-1)
    lam = jnp.concatenate([lam_re.reshape(nb, -1), lam_im.reshape(nb, -1)], axis=-1)
    return mb, mc, lam


def _s5_powers(a_re, a_im, log_dt, sub):
    g, p = a_re.shape
    nb = g // GROUPS_PER_BLOCK
    dt = jnp.exp(log_dt)[:, None]
    ns = list(range(1, sub + 1)) + [2 * sub, 4 * sub]
    ns += [0] * (-len(ns) % SUBLANES)
    e = jnp.asarray(ns, F32)[:, None, None]
    mag = jnp.exp(a_re[None] * dt[None] * e)
    ang = a_im[None] * dt[None] * e
    re = (mag * jnp.cos(ang)).reshape(len(ns), nb, -1)
    im = (mag * jnp.sin(ang)).reshape(len(ns), nb, -1)
    return jnp.transpose(jnp.concatenate([re, im], axis=-1), (1, 0, 2))


def _pack(parts):
    flat = jnp.concatenate([a.reshape(-1).astype(F32) for a in parts])
    n = flat.shape[0]
    pad = -n % (SUBLANES * LANES)
    return jnp.pad(flat, (0, pad)).reshape(-1, LANES)


def _unpack(buf, like):
    flat = buf.reshape(-1)
    out, o = [], 0
    for a in like:
        out.append(flat[o:o + a.size].reshape(a.shape))
        o += a.size
    return out


def kernel(x, meta_tokens, g_ffn1, ffn1_w_gate, ffn1_w_up, ffn1_w_down, g_mix, w_in, b_gate, ssm_a_re, ssm_a_im, ssm_log_dt, ssm_b_re, ssm_b_im, ssm_c_re, ssm_c_im, ssm_d, ssm_w_glu, conv_w, conv_w_out, w_o, g_ffn2, ffn2_w_gate, ffn2_w_up, ffn2_w_down, g_final, loss_target, m_meta_tokens, m_g_ffn1, m_ffn1_w_gate, m_ffn1_w_up, m_ffn1_w_down, m_g_mix, m_w_in, m_b_gate, m_ssm_a_re, m_ssm_a_im, m_ssm_log_dt, m_ssm_b_re, m_ssm_b_im, m_ssm_c_re, m_ssm_c_im, m_ssm_d, m_ssm_w_glu, m_conv_w, m_conv_w_out, m_w_o, m_g_ffn2, m_ffn2_w_gate, m_ffn2_w_up, m_ffn2_w_down, m_g_final, v_meta_tokens, v_g_ffn1, v_ffn1_w_gate, v_ffn1_w_up, v_ffn1_w_down, v_g_mix, v_w_in, v_b_gate, v_ssm_a_re, v_ssm_a_im, v_ssm_log_dt, v_ssm_b_re, v_ssm_b_im, v_ssm_c_re, v_ssm_c_im, v_ssm_d, v_ssm_w_glu, v_conv_w, v_conv_w_out, v_w_o, v_g_ffn2, v_ffn2_w_gate, v_ffn2_w_up, v_ffn2_w_down, v_g_final):
    seq, d = x.shape[1], x.shape[2]
    n_meta = meta_tokens.shape[0]
    dh = d // 2
    tp = -(-(n_meta + seq) // ROW_ALIGN) * ROW_ALIGN
    mx, my, mc_ = _position()
    pos = jnp.stack([mx, my, mc_]).astype(jnp.int32)
    shard = 2 * mx + my

    big_names = ["ffn1_w_gate", "ffn1_w_up", "ffn1_w_down", "w_in", "ssm_w_glu", "conv_w_out", "w_o",
                 "ffn2_w_gate", "ffn2_w_up", "ffn2_w_down"]
    drop = lambda arrs: [a.reshape(a.shape[1:]) for a in arrs]
    big_w = drop([ffn1_w_gate, ffn1_w_up, ffn1_w_down, w_in, ssm_w_glu, conv_w_out, w_o,
                  ffn2_w_gate, ffn2_w_up, ffn2_w_down])
    big_m = drop([m_ffn1_w_gate, m_ffn1_w_up, m_ffn1_w_down, m_w_in, m_ssm_w_glu, m_conv_w_out,
                  m_w_o, m_ffn2_w_gate, m_ffn2_w_up, m_ffn2_w_down])
    big_v = drop([v_ffn1_w_gate, v_ffn1_w_up, v_ffn1_w_down, v_w_in, v_ssm_w_glu, v_conv_w_out,
                  v_w_o, v_ffn2_w_gate, v_ffn2_w_up, v_ffn2_w_down])
    pieces = [_cast_piece(w, pos) for w in big_w]
    conv_local = conv_w.reshape(conv_w.shape[1], conv_w.shape[3])
    n_first = 3
    first = _run_carry(_allgather_carry("allgather_first", pieces[:n_first], [meta_tokens, conv_local]))
    smalls = first[n_first:]
    stack4 = lambda wl: wl.reshape((4, -1, wl.shape[-1]))
    w1g, w1u, w1d = [stack4(wl) for wl in first[:n_first]]
    natural_cols = lambda s: jnp.transpose(s, (1, 0, 2)).reshape(s.shape[1], 4 * s.shape[2])
    meta_full = natural_cols(smalls[0])
    cw_full = natural_cols(smalls[1])
    cw_pad = jnp.pad(cw_full, ((0, SUBLANES - cw_full.shape[0]), (0, 0)))

    s5_args = (ssm_a_re[0], ssm_a_im[0], ssm_log_dt[0], ssm_b_re[0], ssm_b_im[0], ssm_c_re[0], ssm_c_im[0])
    (mb, mc, _), disc_vjp = jax.vjp(_s5_discretise, *s5_args)
    powt = _s5_powers(ssm_a_re[0], ssm_a_im[0], ssm_log_dt[0], SCAN_TILE // SUBLANES)
    mb16, mc16 = mb.astype(BF16), mc.astype(BF16)

    pad_rows = tp - n_meta - seq
    h0 = jnp.concatenate([meta_full, x.reshape(seq, d), jnp.zeros((pad_rows, d), F32)], axis=0)
    tgt = jnp.concatenate([jnp.zeros((n_meta, d), F32), loss_target.reshape(seq, d),
                           jnp.zeros((pad_rows, d), F32)], axis=0)
    h1, a1, b1, *mid = _ffn_fwd(h0, g_ffn1, w1g, w1u, w1d, "ffn1_fwd",
                                carry=_allgather_carry("allgather_mixer", pieces[3:7], []))
    win_all, wglu_s, wco_s, wo_s = [stack4(wl) for wl in mid]
    wglu_all = natural_cols(wglu_s)
    wco_all = natural_cols(wco_s)
    wo_all = wo_s.reshape(d, d)
    u, p, w2g, w2u = _win_fwd(h1, g_mix, win_all, carry=_allgather_carry("allgather_ffn2_in", pieces[7:9], []))
    ys5, bnd = _scan_fwd(p, mb16, mc16, powt, ssm_d)
    h2, w2d = _mix_fwd(h1, ys5, p, cw_pad, b_gate, wglu_all, wco_all, wo_all,
                       carry=_allgather_carry("allgather_ffn2_out", pieces[9:], []))
    w2g, w2u, w2d = stack4(w2g), stack4(w2u), stack4(w2d)
    dh3, a2, b2, dg_final, loss_part = _ffn_fwd(
        h2, g_ffn2, w2g, w2u, w2d, "ffn2_fwd_loss", final=(g_final.reshape(1, d), tgt, n_meta, seq))

    dh2, dw2g, dw2u, dw2d, dg_ffn2 = _ffn_bwd(dh3, h2, g_ffn2, a2, b2, w2g, w2u, w2d, "ffn2_bwd")
    dys5, dpb, dwo, dwglu, dwco, dcw, dbg = _mix_bwd(dh2, ys5, p, cw_pad, b_gate, wglu_all, wco_all, wo_all)
    dug, dmb, dmc, dlam, dd = _scan_bwd(p, dys5, mb16, mc16, powt, ssm_d, bnd)
    dh1, dwin, dg_mix = _win_bwd(dpb, dug, u, win_all, h1, g_mix, dh2)
    shapes = [w.shape for w in big_w]
    kinds = ["list", "list", "list", "list", "col", "col", "row", "list", "list", "list"]
    rest_grads = [dwin, [dwglu], [dwco], [dwo], dw2g, dw2u, dw2d]
    rs_rest = _ReduceScatter("rest", [_Grad(a, k, s) for a, k, s in
                                      zip(rest_grads, kinds[n_first:], shapes[n_first:])], pos)
    dh0, dw1g, dw1u, dw1d, dg_ffn1 = _ffn_bwd(dh1, h0, g_ffn1, a1, b1, w1g, w1u, w1d, "ffn1_bwd", chain=rs_rest)
    rs_first = _ReduceScatter("first", [_Grad(a, k, s) for a, k, s in
                                        zip([dw1g, dw1u, dw1d], kinds[:n_first], shapes[:n_first])], pos)
    rs_first.run()
    wmv = list(zip(big_w, big_m, big_v))
    big_out = rs_first.adamw(wmv[:n_first]) + rs_rest.adamw(wmv[n_first:])
    big_out = {nme: tuple(o.reshape((1,) + o.shape) for o in outs) for nme, outs in zip(big_names, big_out)}

    s5_grads = disc_vjp((dmb, dmc, jnp.sum(dlam, axis=1)))
    grad_x = dh0[n_meta:n_meta + seq][None]

    small_names = ["g_ffn1", "g_mix", "b_gate", "ssm_a_re", "ssm_a_im", "ssm_log_dt", "ssm_b_re", "ssm_b_im",
                   "ssm_c_re", "ssm_c_im", "ssm_d", "g_ffn2", "g_final", "meta_tokens", "conv_w"]
    small_w = [g_ffn1, g_mix, b_gate, ssm_a_re, ssm_a_im, ssm_log_dt, ssm_b_re, ssm_b_im, ssm_c_re, ssm_c_im,
               ssm_d, g_ffn2, g_final, meta_tokens, conv_w]
    small_m = [m_g_ffn1, m_g_mix, m_b_gate, m_ssm_a_re, m_ssm_a_im, m_ssm_log_dt, m_ssm_b_re, m_ssm_b_im,
               m_ssm_c_re, m_ssm_c_im, m_ssm_d, m_g_ffn2, m_g_final, m_meta_tokens, m_conv_w]
    small_v = [v_g_ffn1, v_g_mix, v_b_gate, v_ssm_a_re, v_ssm_a_im, v_ssm_log_dt, v_ssm_b_re, v_ssm_b_im,
               v_ssm_c_re, v_ssm_c_im, v_ssm_d, v_g_ffn2, v_g_final, v_meta_tokens, v_conv_w]
    local_small = [dg_ffn1, dg_mix, dbg, *s5_grads, jnp.sum(dd, axis=0), dg_ffn2, dg_final,
                   dh0[:n_meta], dcw[:conv_w.shape[1]]]
    reduced = _unpack(_allreduce_small(_pack(local_small)), local_small)
    reduced[-2] = lax.dynamic_slice_in_dim(reduced[-2], shard * meta_tokens.shape[1], meta_tokens.shape[1], 1)
    reduced[-1] = lax.dynamic_slice_in_dim(reduced[-1], shard * conv_w.shape[3], conv_w.shape[3], 1)
    small_g = [r.reshape(w.shape) for r, w in zip(reduced, small_w)]
    ds_, nm_, nv_ = _adamw_small(_pack(small_w), _pack(small_g), _pack(small_m), _pack(small_v))
    small_out = {nme: o for nme, o in zip(
        small_names, zip(small_g, _unpack(ds_, small_w), _unpack(nm_, small_w), _unpack(nv_, small_w)))}

    loss = lax.psum(loss_part[0, 0], ("x", "y", "c"))
    order = ["meta_tokens", "g_ffn1", "ffn1_w_gate", "ffn1_w_up", "ffn1_w_down", "g_mix", "w_in", "b_gate",
             "ssm_a_re", "ssm_a_im", "ssm_log_dt", "ssm_b_re", "ssm_b_im", "ssm_c_re", "ssm_c_im", "ssm_d",
             "ssm_w_glu", "conv_w", "conv_w_out", "w_o", "g_ffn2", "ffn2_w_gate", "ffn2_w_up", "ffn2_w_down",
             "g_final"]
    res = {**big_out, **small_out}
    return (loss, grad_x, *[res[nme][0] for nme in order], *[res[nme][1] for nme in order],
            *[res[nme][2] for nme in order], *[res[nme][3] for nme in order])
```

```python
import functools
import math

import jax
import jax.numpy as jnp
from jax import lax
from jax.experimental import pallas as pl
from jax.experimental.pallas import tpu as pltpu

F32 = jnp.float32
BF16 = jnp.bfloat16
MESH = pl.DeviceIdType.MESH

RMS_EPS = 1e-6
ADAM_LR = 0.001
ADAM_B1 = 0.9
ADAM_B2 = 0.999
ADAM_EPS = 1e-08
ADAM_WD = 0.01
ADAM_STEP = 10

LANES = 128
SUBLANES = 8
VMEM_LIMIT = 56 * 1024 * 1024

ROW_ALIGN = 256
SCAN_TILE = 256
GROUPS_PER_BLOCK = 8


def _params(sem, vmem=VMEM_LIMIT):
    return pltpu.CompilerParams(dimension_semantics=sem, vmem_limit_bytes=vmem)


def _pick_tile(n, candidates):
    for c in candidates:
        if n % c == 0:
            return c
    raise ValueError(f"no tile for {n}")


def _dot(a, b):
    return jnp.dot(a, b, preferred_element_type=F32)


def _dot_nt(a, b):
    return lax.dot_general(a, b, (((1,), (1,)), ((), ())), preferred_element_type=F32)


def _dot_tn(a, b):
    return lax.dot_general(a, b, (((0,), (0,)), ((), ())), preferred_element_type=F32)


def _sigmoid(x):
    return 1.0 / (1.0 + jnp.exp(-x))


def _rms_stats(h):
    r = lax.rsqrt(jnp.mean(h * h, axis=-1, keepdims=True) + RMS_EPS)
    return h * r, r


def _rms_bwd(xhat, r, g, dn):
    dxh = dn * g
    return r * (dxh - xhat * jnp.mean(dxh * xhat, axis=-1, keepdims=True))


GELU_K = math.sqrt(2.0 / math.pi)
GELU_C = 0.044715


def _gelu(x):
    return 0.5 * x * (1.0 + jnp.tanh(GELU_K * (x + GELU_C * x * x * x)))


def _gelu_grad(x):
    t = jnp.tanh(GELU_K * (x + GELU_C * x * x * x))
    return 0.5 * (1.0 + t) + 0.5 * x * (1.0 - t * t) * GELU_K * (1.0 + 3.0 * GELU_C * x * x)


def _ffn_fwd(h, g, wg, wu, wd, name, final=None, carry=None):
    tp, d = h.shape
    ns, f4, _ = wg.shape
    tm = _pick_tile(tp, (768, 512, 256))
    ni = tp // tm

    def body(*refs):
        refs, phases = split(refs)
        if final is None:
            h_ref, g_ref, wg_ref, wu_ref, wd_ref, ho_ref, a_ref, b_ref, n_scr, acc = refs
        else:
            (h_ref, g_ref, wg_ref, wu_ref, wd_ref, gf_ref, tg_ref,
             ho_ref, a_ref, b_ref, dgf_ref, loss_ref, n_scr, acc) = refs
        i = pl.program_id(0)
        k = pl.program_id(1)
        _run_phases(phases, carry, i * ns + k, ni * ns)

        @pl.when(k == 0)
        def _():
            xhat, _ = _rms_stats(h_ref[...])
            n_scr[...] = (xhat * g_ref[...]).astype(BF16)
            acc[...] = jnp.zeros_like(acc)

        n = n_scr[...]
        a = _dot_nt(n, wg_ref[...])
        b = _dot_nt(n, wu_ref[...])
        a_ref[...] = a.astype(BF16)
        b_ref[...] = b.astype(BF16)
        s = (a * _sigmoid(a) * b).astype(BF16)
        acc[...] += _dot(s, wd_ref[...])

        if final is None:
            @pl.when(k == ns - 1)
            def _():
                ho_ref[...] = h_ref[...] + 0.5 * acc[...]
        else:
            n_meta, seq = final[2], final[3]

            @pl.when((i == 0) & (k == 0))
            def _():
                dgf_ref[...] = jnp.zeros_like(dgf_ref)
                loss_ref[...] = jnp.zeros_like(loss_ref)

            @pl.when(k == ns - 1)
            def _():
                h3 = h_ref[...] + 0.5 * acc[...]
                xhat, r = _rms_stats(h3)
                gf = gf_ref[...]
                row = i * tm + lax.broadcasted_iota(jnp.int32, (tm, d), 0)
                valid = (row >= n_meta) & (row < n_meta + seq)
                diff = jnp.where(valid, xhat * gf - tg_ref[...], 0.0)
                dout = diff * (1.0 / d)
                loss_ref[...] += jnp.full(loss_ref.shape, 0.5 * jnp.sum(diff * diff) * (1.0 / d), F32)
                dgf_ref[...] += jnp.sum(dout * xhat, axis=0, keepdims=True)
                ho_ref[...] = _rms_bwd(xhat, r, gf, dout)

    row_spec = pl.BlockSpec((tm, d), lambda i, k: (i, 0))
    vec_spec = pl.BlockSpec((1, d), lambda i, k: (0, 0))
    in_specs = [row_spec, vec_spec,
                pl.BlockSpec((None, f4, d), lambda i, k: (k, 0, 0)),
                pl.BlockSpec((None, f4, d), lambda i, k: (k, 0, 0)),
                pl.BlockSpec((None, f4, d), lambda i, k: (k, 0, 0))]
    act_spec = pl.BlockSpec((None, tm, f4), lambda i, k: (k, i, 0))
    out_specs = [row_spec, act_spec, act_spec]
    out_shape = [jax.ShapeDtypeStruct((tp, d), F32),
                 jax.ShapeDtypeStruct((ns, tp, f4), BF16),
                 jax.ShapeDtypeStruct((ns, tp, f4), BF16)]
    args = [h, g, wg, wu, wd]
    if final is not None:
        in_specs += [vec_spec, row_spec]
        args += [final[0], final[1]]
        out_specs += [vec_spec, pl.BlockSpec((1, LANES), lambda i, k: (0, 0))]
        out_shape += [jax.ShapeDtypeStruct((1, d), F32), jax.ShapeDtypeStruct((1, LANES), F32)]
    scratch = [pltpu.VMEM((tm, d), BF16), pltpu.VMEM((tm, d), F32)]
    split = _attach_carry(carry, in_specs, args, out_specs, out_shape, scratch)
    return pl.pallas_call(
        body, name=name, grid=(ni, ns), in_specs=in_specs, out_specs=out_specs, out_shape=out_shape,
        scratch_shapes=scratch, compiler_params=_params(("arbitrary", "arbitrary")),
    )(*args)


def _ffn_bwd_shard(k, ns, dn_prev, dh_out, h_in, g, a, b, wg, wu, wd, name, carry=None):
    tp, d = h_in.shape
    f4 = wg.shape[1]
    tm = _pick_tile(tp, (768, 512, 256))
    ni = tp // tm
    first, last = k == 0, k == ns - 1

    def body(*refs):
        refs, phases = split(refs)
        acc_in = None if first else refs.pop(0)
        (dh_ref, h_ref, g_ref, a_ref, b_ref, wg_hbm, wu_hbm, wd_hbm,
         acc_out, dwg_hbm, dwu_hbm, dwd_hbm) = refs[:12]
        rest = refs[12:]
        dg_ref = rest.pop(0) if last else None
        wg_ref, wu_ref, wd_ref, dwg_ref, dwu_ref, dwd_ref = rest
        i = pl.program_id(0)
        _run_phases(phases, carry, i, ni)

        @pl.when(i == 0)
        def _():
            pltpu.sync_copy(wg_hbm.at[k], wg_ref)
            pltpu.sync_copy(wu_hbm.at[k], wu_ref)
            pltpu.sync_copy(wd_hbm.at[k], wd_ref)
            dwg_ref[...] = jnp.zeros_like(dwg_ref)
            dwu_ref[...] = jnp.zeros_like(dwu_ref)
            dwd_ref[...] = jnp.zeros_like(dwd_ref)
            if last:
                dg_ref[...] = jnp.zeros_like(dg_ref)

        xhat, r = _rms_stats(h_ref[...])
        gv = g_ref[...]
        n = (xhat * gv).astype(BF16)
        dy = (0.5 * dh_ref[...]).astype(BF16)
        av = a_ref[...].astype(F32)
        bv = b_ref[...].astype(F32)
        sg = _sigmoid(av)
        silu = av * sg
        ds = _dot_nt(dy, wd_ref[...])
        da = (ds * bv * (sg * (1.0 + av * (1.0 - sg)))).astype(BF16)
        db = (ds * silu).astype(BF16)
        s = (silu * bv).astype(BF16)
        dwd_ref[...] += _dot_tn(s, dy)
        dwg_ref[...] += _dot_tn(da, n)
        dwu_ref[...] += _dot_tn(db, n)
        dn = _dot(da, wg_ref[...]) + _dot(db, wu_ref[...])
        if not first:
            dn = dn + acc_in[...]
        if last:
            dg_ref[...] += jnp.sum(dn * xhat, axis=0, keepdims=True)
            acc_out[...] = dh_ref[...] + _rms_bwd(xhat, r, gv, dn)
        else:
            acc_out[...] = dn

        @pl.when(i == ni - 1)
        def _():
            pltpu.sync_copy(dwg_ref, dwg_hbm)
            pltpu.sync_copy(dwu_ref, dwu_hbm)
            pltpu.sync_copy(dwd_ref, dwd_hbm)

    row_spec = pl.BlockSpec((tm, d), lambda i: (i, 0))
    vec_spec = pl.BlockSpec((1, d), lambda i: (0, 0))
    act_spec = pl.BlockSpec((None, tm, f4), lambda i: (k, i, 0))
    in_specs = [row_spec, row_spec, vec_spec, act_spec, act_spec, ANY, ANY, ANY]
    args = [dh_out, h_in, g, a, b, wg, wu, wd]
    if not first:
        in_specs.insert(0, row_spec)
        args.insert(0, dn_prev)
    out_specs = [row_spec, ANY, ANY, ANY]
    out_shape = [jax.ShapeDtypeStruct((tp, d), F32)] + [jax.ShapeDtypeStruct((f4, d), F32)] * 3
    if last:
        out_specs.append(vec_spec)
        out_shape.append(jax.ShapeDtypeStruct((1, d), F32))
    n_host = len(out_shape)
    scratch = [pltpu.VMEM((f4, d), BF16)] * 3 + [pltpu.VMEM((f4, d), F32)] * 3
    split = _attach_carry(carry, in_specs, args, out_specs, out_shape, scratch)
    outs = pl.pallas_call(
        body, name=f"{name}_{k}", grid=(ni,), in_specs=in_specs, out_specs=out_specs, out_shape=out_shape,
        scratch_shapes=scratch, compiler_params=_params(("arbitrary",)),
    )(*args)
    return outs[:n_host], outs[n_host:]


def _ffn_bwd(dh_out, h_in, g, a, b, wg, wu, wd, name, chain=None):
    ns = wg.shape[0]
    acc, dwg, dwu, dwd, dg = None, [], [], [], None
    for k in range(ns):
        carry = chain.carry() if chain is not None else None
        outs, carried = _ffn_bwd_shard(k, ns, acc, dh_out, h_in, g, a, b, wg, wu, wd, name, carry)
        if chain is not None:
            chain.feed(carried)
        acc = outs[0]
        dwg.append(outs[1])
        dwu.append(outs[2])
        dwd.append(outs[3])
        if k == ns - 1:
            dg = outs[4]
    return acc, dwg, dwu, dwd, dg


def _win_fwd(h, g, w_in, carry=None):
    tp, d = h.shape
    ns = w_in.shape[0]
    tm = _pick_tile(tp, (768, 512, 256))
    ni = tp // tm

    def body(*refs):
        (h_ref, g_ref, w_ref, u_ref, p_ref), phases = split(refs)
        _run_phases(phases, carry, pl.program_id(0) * ns + pl.program_id(1), ni * ns)

        @pl.when(pl.program_id(1) == 0)
        def _():
            xhat, _ = _rms_stats(h_ref[...])
            u_ref[...] = (xhat * g_ref[...]).astype(BF16)

        p_ref[...] = _dot(u_ref[...], w_ref[...]).astype(BF16)

    in_specs = [pl.BlockSpec((tm, d), lambda i, k: (i, 0)),
                pl.BlockSpec((1, d), lambda i, k: (0, 0)),
                pl.BlockSpec((None, d, d), lambda i, k: (k, 0, 0))]
    out_specs = [pl.BlockSpec((tm, d), lambda i, k: (i, 0)),
                 pl.BlockSpec((None, tm, d), lambda i, k: (k, i, 0))]
    out_shape = [jax.ShapeDtypeStruct((tp, d), BF16), jax.ShapeDtypeStruct((ns, tp, d), BF16)]
    args, scratch = [h, g, w_in], []
    split = _attach_carry(carry, in_specs, args, out_specs, out_shape, scratch)
    return pl.pallas_call(
        body, name="win_fwd", grid=(ni, ns), in_specs=in_specs, out_specs=out_specs, out_shape=out_shape,
        scratch_shapes=scratch, compiler_params=_params(("arbitrary", "arbitrary")),
    )(*args)


def _win_bwd_shard(k, ns, du_prev, dpb, dug, u, w_in, h1, g, dh2):
    tp, d = h1.shape
    dh = d // 2
    tm = _pick_tile(tp, (768, 512, 256))
    first, last = k == 0, k == ns - 1

    def body(*refs):
        refs = list(refs)
        acc_in = None if first else refs.pop(0)
        dug_ref = refs.pop(0) if first else None
        dp_ref, u_ref, w_ref = refs[:3]
        refs = refs[3:]
        if last:
            h_ref, g_ref, dh2_ref, acc_out, dw_ref, dg_ref = refs
        else:
            acc_out, dw_ref = refs
        i = pl.program_id(0)

        @pl.when(i == 0)
        def _():
            dw_ref[...] = jnp.zeros_like(dw_ref)
            if last:
                dg_ref[...] = jnp.zeros_like(dg_ref)

        dp = dp_ref[...]
        if first:
            dp = jnp.concatenate([dug_ref[...], dp[:, dh:]], axis=1)
        dw_ref[...] += _dot_tn(u_ref[...], dp)
        du = _dot_nt(dp, w_ref[...])
        if not first:
            du = du + acc_in[...]
        if last:
            xhat, r = _rms_stats(h_ref[...])
            dg_ref[...] += jnp.sum(du * xhat, axis=0, keepdims=True)
            acc_out[...] = dh2_ref[...] + _rms_bwd(xhat, r, g_ref[...], du)
        else:
            acc_out[...] = du

    row_spec = pl.BlockSpec((tm, d), lambda i: (i, 0))
    vec_spec = pl.BlockSpec((1, d), lambda i: (0, 0))
    in_specs = [pl.BlockSpec((None, tm, d), lambda i: (k, i, 0)), row_spec,
                pl.BlockSpec((None, d, d), lambda i: (k, 0, 0))]
    args = [dpb, u, w_in]
    if first:
        in_specs.insert(0, pl.BlockSpec((tm, dh), lambda i: (i, 0)))
        args.insert(0, dug)
    else:
        in_specs.insert(0, row_spec)
        args.insert(0, du_prev)
    out_specs = [row_spec, pl.BlockSpec((d, d), lambda i: (0, 0))]
    out_shape = [jax.ShapeDtypeStruct((tp, d), F32), jax.ShapeDtypeStruct((d, d), F32)]
    if last:
        in_specs += [row_spec, vec_spec, row_spec]
        args += [h1, g, dh2]
        out_specs.append(vec_spec)
        out_shape.append(jax.ShapeDtypeStruct((1, d), F32))
    return pl.pallas_call(
        body, name=f"win_bwd_{k}", grid=(tp // tm,), in_specs=in_specs, out_specs=out_specs,
        out_shape=out_shape, compiler_params=_params(("arbitrary",)),
    )(*args)


def _win_bwd(dpb, dug, u, w_in, h1, g, dh2):
    ns = w_in.shape[0]
    acc, dws, dg = None, [], None
    for k in range(ns):
        outs = _win_bwd_shard(k, ns, acc, dpb, dug, u, w_in, h1, g, dh2)
        acc = outs[0]
        dws.append(outs[1])
        if k == ns - 1:
            dg = outs[2]
    return acc, dws, dg


def _cmul(ar, ai, br, bi):
    return ar * br - ai * bi, ar * bi + ai * br


def _scan_rows(j, sub):
    return pl.ds(j * SUBLANES, SUBLANES)


def _permute_rows(src_ref, dst_ref, sub):
    for j in range(sub):
        dst_ref[pl.ds(j * SUBLANES, SUBLANES), :] = src_ref[pl.ds(j, SUBLANES, stride=sub), :]


def _unpermute_rows(src_ref, dst_ref, sub):
    for j in range(sub):
        dst_ref[pl.ds(j, SUBLANES, stride=sub), :] = src_ref[pl.ds(j * SUBLANES, SUBLANES), :]


def _local_scan(x_ref, lr, li, w, sub, reverse):
    hr = jnp.zeros((SUBLANES, w), F32)
    hi = jnp.zeros((SUBLANES, w), F32)
    order = range(sub - 1, -1, -1) if reverse else range(sub)
    for j in order:
        xr = x_ref[_scan_rows(j, sub), pl.ds(0, w)]
        xi = x_ref[_scan_rows(j, sub), pl.ds(w, w)]
        if reverse:
            hr, hi = lr * hr + li * hi + xr, lr * hi - li * hr + xi
        else:
            hr, hi = lr * hr - li * hi + xr, lr * hi + li * hr + xi
        x_ref[_scan_rows(j, sub), pl.ds(0, w)] = hr
        x_ref[_scan_rows(j, sub), pl.ds(w, w)] = hi
    return hr, hi


def _entering_states(er, ei, fr, fi, pow_ref, w, sub, reverse):
    lane = lax.broadcasted_iota(jnp.int32, (SUBLANES, w), 0)
    if reverse:
        edge, shift1 = SUBLANES - 1, SUBLANES - 1
    else:
        edge, shift1 = 0, 1
    zr = jnp.where(lane == edge, pltpu.roll(fr, shift1, 0), pltpu.roll(er, shift1, 0))
    zi = jnp.where(lane == edge, pltpu.roll(fi, shift1, 0), pltpu.roll(ei, shift1, 0))
    for step, row in ((1, sub - 1), (2, sub), (4, sub + 1)):
        ar = pow_ref[pl.ds(row, 1), pl.ds(0, w)]
        ai = pow_ref[pl.ds(row, 1), pl.ds(w, w)]
        if reverse:
            ai = -ai
            keep = lane < SUBLANES - step
            sr = jnp.where(keep, pltpu.roll(zr, SUBLANES - step, 0), 0.0)
            si = jnp.where(keep, pltpu.roll(zi, SUBLANES - step, 0), 0.0)
        else:
            keep = lane >= step
            sr = jnp.where(keep, pltpu.roll(zr, step, 0), 0.0)
            si = jnp.where(keep, pltpu.roll(zi, step, 0), 0.0)
        pr, pi = _cmul(ar, ai, sr, si)
        zr, zi = zr + pr, zi + pi
    ar = pow_ref[pl.ds(sub - 1, 1), pl.ds(0, w)]
    ai = pow_ref[pl.ds(sub - 1, 1), pl.ds(w, w)]
    if reverse:
        ai = -ai
    pr, pi = _cmul(ar, ai, zr, zi)
    return zr, zi, er + pr, ei + pi


def _scan_fwd(p, mb, mc, powt, dskip):
    _, tp, d = p.shape
    nb, cb, w2 = mb.shape
    w = w2 // 2
    q = SCAN_TILE
    sub = q // SUBLANES
    nt = tp // q
    ds = d // 2

    def body(ug_ref, mb_ref, mc_ref, pow_ref, d_ref, y_ref, bnd_ref, x_scr, carry, nat, perm):
        t = pl.program_id(1)

        @pl.when(t == 0)
        def _():
            carry[...] = jnp.zeros_like(carry)

        ugf = ug_ref[...].astype(F32)
        nat[...] = ugf
        _permute_rows(nat, perm, sub)
        x_scr[...] = _dot(perm[...].astype(BF16), mb_ref[...])
        lr = jnp.broadcast_to(pow_ref[pl.ds(0, 1), pl.ds(0, w)], (SUBLANES, w))
        li = jnp.broadcast_to(pow_ref[pl.ds(0, 1), pl.ds(w, w)], (SUBLANES, w))
        er, ei = _local_scan(x_scr, lr, li, w, sub, False)
        zr, zi, fr, fi = _entering_states(er, ei, carry[:, pl.ds(0, w)], carry[:, pl.ds(w, w)],
                                          pow_ref, w, sub, False)
        carry[:, pl.ds(0, w)] = fr
        carry[:, pl.ds(w, w)] = fi
        bnd_ref[:, pl.ds(0, w)] = fr
        bnd_ref[:, pl.ds(w, w)] = fi
        for j in range(sub):
            pr = pow_ref[pl.ds(j, 1), pl.ds(0, w)]
            pi = pow_ref[pl.ds(j, 1), pl.ds(w, w)]
            cr, ci = _cmul(pr, pi, zr, zi)
            x_scr[_scan_rows(j, sub), pl.ds(0, w)] += cr
            x_scr[_scan_rows(j, sub), pl.ds(w, w)] += ci
        hb = x_scr[...].astype(BF16)
        perm[...] = _dot_nt(hb, mc_ref[...])
        _unpermute_rows(perm, nat, sub)
        y_ref[...] = nat[...] + d_ref[...] * ugf

    in_specs = [pl.BlockSpec((None, q, cb), lambda b, t: (0, t, b)),
                pl.BlockSpec((None, cb, w2), lambda b, t: (b, 0, 0)),
                pl.BlockSpec((None, cb, w2), lambda b, t: (b, 0, 0)),
                pl.BlockSpec((None, powt.shape[1], w2), lambda b, t: (b, 0, 0)),
                pl.BlockSpec((1, cb), lambda b, t: (0, b))]
    out_specs = [pl.BlockSpec((q, cb), lambda b, t: (t, b)),
                 pl.BlockSpec((None, None, SUBLANES, w2), lambda b, t: (b, t, 0, 0))]
    out_shape = [jax.ShapeDtypeStruct((tp, ds), F32), jax.ShapeDtypeStruct((nb, nt, SUBLANES, w2), F32)]
    scratch = [pltpu.VMEM((q, w2), F32), pltpu.VMEM((SUBLANES, w2), F32),
               pltpu.VMEM((q, cb), F32), pltpu.VMEM((q, cb), F32)]
    return pl.pallas_call(
        body, name="s5_scan_fwd", grid=(nb, nt), in_specs=in_specs, out_specs=out_specs,
        out_shape=out_shape, scratch_shapes=scratch, compiler_params=_params(("arbitrary", "arbitrary")),
    )(p, mb, mc, powt, dskip)


def _scan_bwd(p, dy, mb, mc, powt, dskip, bnd):
    _, tp, d = p.shape
    nb, cb, w2 = mb.shape
    w = w2 // 2
    q = SCAN_TILE
    sub = q // SUBLANES
    nt = tp // q
    ds = d // 2

    def body(ug_ref, dy_ref, mb_ref, mc_ref, pow_ref, d_ref, bnd_ref,
             dug_ref, dmb_ref, dmc_ref, dlam_ref, dd_ref, x_scr, y_scr, gcarry, nat, perm):
        t = pl.program_id(1)
        tt = nt - 1 - t

        @pl.when(t == 0)
        def _():
            gcarry[...] = jnp.zeros_like(gcarry)
            dmb_ref[...] = jnp.zeros_like(dmb_ref)
            dmc_ref[...] = jnp.zeros_like(dmc_ref)
            dlam_ref[...] = jnp.zeros_like(dlam_ref)
            dd_ref[...] = jnp.zeros_like(dd_ref)

        ugf = ug_ref[...].astype(F32)
        dyf = dy_ref[...].astype(F32)
        dd_ref[...] += jnp.sum((dyf * ugf).reshape(q // SUBLANES, SUBLANES, cb), axis=0)
        nat[...] = ugf
        _permute_rows(nat, perm, sub)
        ug = perm[...].astype(BF16)
        nat[...] = dyf
        _permute_rows(nat, perm, sub)
        dyb = perm[...].astype(BF16)
        lr = jnp.broadcast_to(pow_ref[pl.ds(0, 1), pl.ds(0, w)], (SUBLANES, w))
        li = jnp.broadcast_to(pow_ref[pl.ds(0, 1), pl.ds(w, w)], (SUBLANES, w))

        x_scr[...] = _dot(ug, mb_ref[...])
        er, ei = _local_scan(x_scr, lr, li, w, sub, False)
        first = tt == 0
        pfr = jnp.where(first, 0.0, bnd_ref[:, pl.ds(0, w)])
        pfi = jnp.where(first, 0.0, bnd_ref[:, pl.ds(w, w)])
        hzr, hzi, _, _ = _entering_states(er, ei, pfr, pfi, pow_ref, w, sub, False)
        for j in range(sub):
            pr = pow_ref[pl.ds(j, 1), pl.ds(0, w)]
            pi = pow_ref[pl.ds(j, 1), pl.ds(w, w)]
            cr, ci = _cmul(pr, pi, hzr, hzi)
            x_scr[_scan_rows(j, sub), pl.ds(0, w)] += cr
            x_scr[_scan_rows(j, sub), pl.ds(w, w)] += ci

        y_scr[...] = _dot(dyb, mc_ref[...])
        er, ei = _local_scan(y_scr, lr, li, w, sub, True)
        gzr, gzi, fr, fi = _entering_states(er, ei, gcarry[:, pl.ds(0, w)], gcarry[:, pl.ds(w, w)],
                                            pow_ref, w, sub, True)
        gcarry[:, pl.ds(0, w)] = fr
        gcarry[:, pl.ds(w, w)] = fi
        accr = jnp.zeros((SUBLANES, w), F32)
        acci = jnp.zeros((SUBLANES, w), F32)
        for j in range(sub):
            pr = pow_ref[pl.ds(sub - 1 - j, 1), pl.ds(0, w)]
            pi = pow_ref[pl.ds(sub - 1 - j, 1), pl.ds(w, w)]
            cr, ci = _cmul(pr, -pi, gzr, gzi)
            gr = y_scr[_scan_rows(j, sub), pl.ds(0, w)] + cr
            gi = y_scr[_scan_rows(j, sub), pl.ds(w, w)] + ci
            y_scr[_scan_rows(j, sub), pl.ds(0, w)] = gr
            y_scr[_scan_rows(j, sub), pl.ds(w, w)] = gi
            if j == 0:
                hpr, hpi = hzr, hzi
            else:
                hpr = x_scr[_scan_rows(j - 1, sub), pl.ds(0, w)]
                hpi = x_scr[_scan_rows(j - 1, sub), pl.ds(w, w)]
            accr += hpr * gr + hpi * gi
            acci += hpr * gi - hpi * gr
        dlam_ref[:, pl.ds(0, w)] += accr
        dlam_ref[:, pl.ds(w, w)] += acci

        hb = x_scr[...].astype(BF16)
        gb = y_scr[...].astype(BF16)
        dmc_ref[...] += _dot_tn(dyb, hb)
        dmb_ref[...] += _dot_tn(ug, gb)
        perm[...] = _dot_nt(gb, mb_ref[...])
        _unpermute_rows(perm, nat, sub)
        dug_ref[...] = (nat[...] + d_ref[...] * dyf).astype(BF16)

    blk = lambda b, t: (b, 0, 0)
    return pl.pallas_call(
        body, name="s5_scan_bwd", grid=(nb, nt),
        in_specs=[pl.BlockSpec((None, q, cb), lambda b, t: (0, nt - 1 - t, b)),
                  pl.BlockSpec((q, cb), lambda b, t: (nt - 1 - t, b)),
                  pl.BlockSpec((None, cb, w2), blk),
                  pl.BlockSpec((None, cb, w2), blk),
                  pl.BlockSpec((None, powt.shape[1], w2), blk),
                  pl.BlockSpec((1, cb), lambda b, t: (0, b)),
                  pl.BlockSpec((None, None, SUBLANES, w2),
                               lambda b, t: (b, jnp.maximum(nt - 2 - t, 0), 0, 0))],
        out_specs=[pl.BlockSpec((q, cb), lambda b, t: (nt - 1 - t, b)),
                   pl.BlockSpec((None, cb, w2), blk),
                   pl.BlockSpec((None, cb, w2), blk),
                   pl.BlockSpec((None, SUBLANES, w2), blk),
                   pl.BlockSpec((SUBLANES, cb), lambda b, t: (0, b))],
        out_shape=[jax.ShapeDtypeStruct((tp, ds), BF16),
                   jax.ShapeDtypeStruct((nb, cb, w2), F32),
                   jax.ShapeDtypeStruct((nb, cb, w2), F32),
                   jax.ShapeDtypeStruct((nb, SUBLANES, w2), F32),
                   jax.ShapeDtypeStruct((SUBLANES, ds), F32)],
        scratch_shapes=[pltpu.VMEM((q, w2), F32), pltpu.VMEM((q, w2), F32),
                        pltpu.VMEM((SUBLANES, w2), F32), pltpu.VMEM((q, cb), F32), pltpu.VMEM((q, cb), F32)],
        compiler_params=_params(("arbitrary", "arbitrary")),
    )(p, dy, mb, mc, powt, dskip, bnd)


HALO = 16


def _mix_tile(ys5, p0, p1, p2, p3, prev_cin, cw, bgate, wglu, wco, d):
    dh = d // 2
    tm = ys5.shape[0]
    v = p0[:, dh:].astype(F32)
    gbr = p1[:, :dh].astype(F32)
    gcr = p1[:, dh:].astype(F32)
    gact = _gelu(ys5).astype(BF16)
    z = _dot(gact, wglu)
    z1, z2 = z[:, :d], z[:, d:]
    sg = _sigmoid(z2)
    y_ssm = z1 * sg
    cin = gcr * v
    ext = jnp.concatenate([cin, prev_cin], axis=0)
    r1 = pltpu.roll(ext, 1, 0)[:tm]
    r2 = pltpu.roll(ext, 2, 0)[:tm]
    cv = cw[2] * cin + cw[1] * r1 + cw[0] * r2
    cg = (gbr * cv).astype(BF16)
    y_conv = _dot(cg, wco)
    g_s = _sigmoid(p2.astype(F32) + bgate[:, :d])
    g_c = _sigmoid(p3.astype(F32) + bgate[:, d:])
    mixed = g_s * y_ssm + g_c * y_conv
    return dict(v=v, gb=gbr, gc=gcr, gact=gact, z1=z1, sg=sg, y_ssm=y_ssm, cin=cin, r1=r1, r2=r2,
                cv=cv, cg=cg, y_conv=y_conv, g_s=g_s, g_c=g_c, mixed=mixed)


def _mix_fwd(h1, ys5, p, cw, bgate, wglu, wco, wo, carry=None):
    tp, d = h1.shape
    dh = d // 2
    tm = ROW_ALIGN
    ni = tp // tm

    def body(*refs):
        refs, phases = split(refs)
        (h_ref, y_ref, p0_ref, p1_ref, p2_ref, p3_ref, cw_ref, bg_ref, wglu_ref, wco_ref, wo_ref,
         o_ref, prev) = refs
        _run_phases(phases, carry, pl.program_id(0), ni)

        @pl.when(pl.program_id(0) == 0)
        def _():
            prev[...] = jnp.zeros_like(prev)

        cw = [cw_ref[pl.ds(t, 1), :] for t in range(3)]
        f = _mix_tile(y_ref[...], p0_ref[...], p1_ref[...], p2_ref[...], p3_ref[...], prev[...],
                      cw, bg_ref[...], wglu_ref[...], wco_ref[...], d)
        prev[...] = f["cin"][tm - HALO:, :]
        o_ref[...] = h_ref[...] + _dot(f["mixed"].astype(BF16), wo_ref[...])

    row = pl.BlockSpec((tm, d), lambda i: (i, 0))
    full = lambda a: pl.BlockSpec(a.shape, lambda i: (0,) * a.ndim)
    pk = lambda k: pl.BlockSpec((None, tm, d), lambda i, k=k: (k, i, 0))
    in_specs = [row, pl.BlockSpec((tm, dh), lambda i: (i, 0)), pk(0), pk(1), pk(2), pk(3),
                full(cw), full(bgate), full(wglu), full(wco), full(wo)]
    out_specs, out_shape = [row], [jax.ShapeDtypeStruct((tp, d), F32)]
    args, scratch = [h1, ys5, p, p, p, p, cw, bgate, wglu, wco, wo], [pltpu.VMEM((HALO, dh), F32)]
    split = _attach_carry(carry, in_specs, args, out_specs, out_shape, scratch)
    return pl.pallas_call(
        body, name="mix_fwd", grid=(ni,), in_specs=in_specs, out_specs=out_specs, out_shape=out_shape,
        scratch_shapes=scratch, compiler_params=_params(("arbitrary",)),
    )(*args)


def _mix_bwd(dh2, ys5, p, cw, bgate, wglu, wco, wo):
    tp, d = dh2.shape
    dh = d // 2
    tm = ROW_ALIGN
    ni = tp // tm
    hb = tm // HALO

    def body(dh_ref, y_ref, p0_ref, p1_ref, p2_ref, p3_ref, h0_ref, h1_ref,
             cw_ref, bg_ref, wglu_ref, wco_ref, wo_ref,
             dys_ref, dpb_ref, dwo_ref, dwglu_ref, dwco_ref, dcw_ref, dbg_ref, nxt):
        i = pl.program_id(0)
        tt = ni - 1 - i

        @pl.when(i == 0)
        def _():
            nxt[...] = jnp.zeros_like(nxt)
            dwo_ref[...] = jnp.zeros_like(dwo_ref)
            dwglu_ref[...] = jnp.zeros_like(dwglu_ref)
            dwco_ref[...] = jnp.zeros_like(dwco_ref)
            dcw_ref[...] = jnp.zeros_like(dcw_ref)
            dbg_ref[...] = jnp.zeros_like(dbg_ref)

        cw = [cw_ref[pl.ds(t, 1), :] for t in range(3)]
        prev_cin = h1_ref[:, dh:].astype(F32) * h0_ref[:, dh:].astype(F32)
        prev_cin = jnp.where(tt == 0, 0.0, prev_cin)
        ys5 = y_ref[...]
        f = _mix_tile(ys5, p0_ref[...], p1_ref[...], p2_ref[...], p3_ref[...], prev_cin,
                      cw, bg_ref[...], wglu_ref[...], wco_ref[...], d)
        dhb = dh_ref[...].astype(BF16)
        dmixed = _dot_nt(dhb, wo_ref[...])
        dwo_ref[...] += _dot_tn(f["mixed"].astype(BF16), dhb)

        g_s, g_c, sg = f["g_s"], f["g_c"], f["sg"]
        dy_ssm = dmixed * g_s
        dy_conv = dmixed * g_c
        dp2 = dmixed * f["y_ssm"] * g_s * (1.0 - g_s)
        dp3 = dmixed * f["y_conv"] * g_c * (1.0 - g_c)
        dbg_ref[:, pl.ds(0, d)] += jnp.sum(dp2, axis=0, keepdims=True)
        dbg_ref[:, pl.ds(d, d)] += jnp.sum(dp3, axis=0, keepdims=True)

        dz = jnp.concatenate([dy_ssm * sg, dy_ssm * f["z1"] * sg * (1.0 - sg)], axis=1).astype(BF16)
        dwglu_ref[...] += _dot_tn(f["gact"], dz)
        dys_ref[...] = (_dot_nt(dz, wglu_ref[...]) * _gelu_grad(ys5)).astype(BF16)

        dycb = dy_conv.astype(BF16)
        dwco_ref[...] += _dot_tn(f["cg"], dycb)
        dcg = _dot_nt(dycb, wco_ref[...])
        dgb = dcg * f["cv"]
        dcv = dcg * f["gb"]
        ext = jnp.concatenate([dcv, nxt[...]], axis=0)
        n1 = pltpu.roll(ext, tm + HALO - 1, 0)[:tm]
        n2 = pltpu.roll(ext, tm + HALO - 2, 0)[:tm]
        nxt[...] = dcv[:HALO, :]
        dcin = cw[2] * dcv + cw[1] * n1 + cw[0] * n2
        dcw_ref[pl.ds(0, 1), :] += jnp.sum(dcv * f["r2"], axis=0, keepdims=True)
        dcw_ref[pl.ds(1, 1), :] += jnp.sum(dcv * f["r1"], axis=0, keepdims=True)
        dcw_ref[pl.ds(2, 1), :] += jnp.sum(dcv * f["cin"], axis=0, keepdims=True)
        dgc = dcin * f["v"]
        dv = dcin * f["gc"]
        dpb_ref[0] = jnp.concatenate([jnp.zeros_like(dv), dv], axis=1).astype(BF16)
        dpb_ref[1] = jnp.concatenate([dgb, dgc], axis=1).astype(BF16)
        dpb_ref[2] = dp2.astype(BF16)
        dpb_ref[3] = dp3.astype(BF16)

    rev = lambda i: ni - 1 - i
    row = pl.BlockSpec((tm, d), lambda i: (rev(i), 0))
    half = pl.BlockSpec((tm, dh), lambda i: (rev(i), 0))
    full = lambda a: pl.BlockSpec(a.shape, lambda i: (0,) * a.ndim)
    pk = lambda k: pl.BlockSpec((None, tm, d), lambda i, k=k: (k, rev(i), 0))
    halo = lambda k: pl.BlockSpec((None, HALO, d), lambda i, k=k: (k, jnp.maximum(rev(i) * hb - 1, 0), 0))
    acc = lambda shape: pl.BlockSpec(shape, lambda i: (0,) * len(shape))
    return pl.pallas_call(
        body, name="mix_bwd", grid=(ni,),
        in_specs=[row, half, pk(0), pk(1), pk(2), pk(3), halo(0), halo(1),
                  full(cw), full(bgate), full(wglu), full(wco), full(wo)],
        out_specs=[half, pl.BlockSpec((4, tm, d), lambda i: (0, rev(i), 0)),
                   acc((d, d)), acc((dh, 2 * d)), acc((dh, d)), acc((SUBLANES, dh)), acc((1, 2 * d))],
        out_shape=[jax.ShapeDtypeStruct((tp, dh), BF16), jax.ShapeDtypeStruct((4, tp, d), BF16),
                   jax.ShapeDtypeStruct((d, d), F32), jax.ShapeDtypeStruct((dh, 2 * d), F32),
                   jax.ShapeDtypeStruct((dh, d), F32), jax.ShapeDtypeStruct((SUBLANES, dh), F32),
                   jax.ShapeDtypeStruct((1, 2 * d), F32)],
        scratch_shapes=[pltpu.VMEM((HALO, dh), F32)],
        compiler_params=_params(("arbitrary",)),
    )(dh2, ys5, p, p, p, p, p, p, cw, bgate, wglu, wco, wo)


ANY = pl.BlockSpec(memory_space=pl.ANY)


def _position():
    return lax.axis_index("x"), lax.axis_index("y"), lax.axis_index("c")


def _remote(src, dst, ssem, rsem, dev):
    return pltpu.make_async_remote_copy(src_ref=src, dst_ref=dst, send_sem=ssem, recv_sem=rsem,
                                        device_id=dev, device_id_type=MESH)


def _cast_piece(w, pos):
    rows, cols = w.shape
    r2 = rows // 2

    def body(pos_ref, w_ref, o_ref):
        o_ref[...] = w_ref[...].astype(BF16)

    return pl.pallas_call(
        body, name="cast_piece",
        grid_spec=pltpu.PrefetchScalarGridSpec(
            num_scalar_prefetch=1, grid=(1,),
            in_specs=[pl.BlockSpec((r2, cols), lambda i, pos: (pos[2], 0))],
            out_specs=pl.BlockSpec((r2, cols), lambda i, pos: (0, 0))),
        out_shape=jax.ShapeDtypeStruct((r2, cols), BF16),
        compiler_params=_params(("arbitrary",)),
    )(pos, w)


class _Carry:
    def __init__(self, name, arrays, out_shapes, nsem, nlsem, make, fracs):
        self.name, self.arrays, self.out_shapes = name, list(arrays), list(out_shapes)
        self.nsem, self.nlsem, self.make, self.fracs = nsem, max(nlsem, 1), make, fracs


def _carry_scratch(carry):
    return [pltpu.SemaphoreType.DMA((carry.nsem,)), pltpu.SemaphoreType.DMA((carry.nsem,)),
            pltpu.SemaphoreType.DMA((carry.nlsem,))]


def _run_carry(carry):
    na, no = len(carry.arrays), len(carry.out_shapes)

    def body(*refs):
        for phase in carry.make(refs[:na], refs[na:na + no], *refs[na + no:]):
            phase()

    return pl.pallas_call(
        body, name=carry.name, in_specs=[ANY] * na, out_specs=[ANY] * no, out_shape=carry.out_shapes,
        scratch_shapes=_carry_scratch(carry),
    )(*carry.arrays)


def _attach_carry(carry, in_specs, args, out_specs, out_shape, scratch):
    nhi, nho, nhs = len(in_specs), len(out_specs), len(scratch)
    if carry is None:
        return lambda refs: (list(refs), [])
    na, no = len(carry.arrays), len(carry.out_shapes)
    in_specs += [ANY] * na
    args += carry.arrays
    out_specs += [ANY] * no
    out_shape += carry.out_shapes
    scratch += _carry_scratch(carry)

    def split(refs):
        refs = list(refs)
        o = nhi + na
        host = refs[:nhi] + refs[o:o + nho] + refs[o + nho + no:o + nho + no + nhs]
        sems = refs[o + nho + no + nhs:]
        return host, carry.make(refs[nhi:o], refs[o + nho:o + nho + no], *sems)

    return split


def _run_phases(phases, carry, step, total):
    for phase, frac in zip(phases, carry.fracs if carry is not None else ()):
        pl.when(step == int(round(frac * (total - 1))))(phase)


def _allgather_carry(name, pieces, smalls):
    n, ns = len(pieces), len(smalls)
    per = 14
    n_big = per * n

    def make(ins, outs, ssem, rsem, lsem):
        pin, sin = ins[:n], ins[n:]
        wall, sall = outs[:n], outs[n:]
        x, y, c = _position()
        xnb, ynb, sib = (1 - x, y, c), (x, 1 - y, c), (x, y, 1 - c)
        chips = [(1 - x, y), (x, 1 - y), (1 - x, 1 - y)]
        r4 = [p.shape[0] // 2 for p in pin]
        own = lambda i, h: pin[i].at[pl.ds(h * r4[i], r4[i]), :]
        slot = lambda i, xx, yy, cc, h: wall[i].at[xx, yy, cc, h]
        cp = lambda src, dst, s, dev: _remote(src, dst, ssem.at[s], rsem.at[s], dev)
        to_sib = lambda i, xx, yy, h: cp(slot(i, xx, yy, c, h), slot(i, xx, yy, c, h),
                                         per * i + 6 + 4 * xx + 2 * yy + h, sib)

        def local():
            cps = [pltpu.make_async_copy(own(i, h), slot(i, x, y, c, h), lsem.at[2 * i + h])
                   for i in range(n) for h in range(2)]
            return cps + [pltpu.make_async_copy(sin[i], sall[i].at[2 * x + y], lsem.at[2 * n + i])
                          for i in range(ns)]

        def small(px, py, j, i, landing):
            s = n_big + j * ns + i
            return cp(sin[i], sall[i].at[landing], s, (px, py, c))

        def first_hop():
            for lc in local():
                lc.start()
            for j, (px, py) in enumerate(chips):
                for i in range(ns):
                    small(px, py, j, i, 2 * x + y).start()
            for i in range(n):
                cp(own(i, 0), slot(i, x, y, c, 0), per * i, xnb).start()
                cp(own(i, 1), slot(i, x, y, c, 1), per * i + 1, ynb).start()
                for h in range(2):
                    cp(own(i, h), slot(i, x, y, c, h), per * i + 6 + 4 * x + 2 * y + h, sib).start()

        def second_hop():
            for lc in local():
                lc.wait()
            for i in range(n):
                cp(slot(i, 1 - x, y, c, 0), slot(i, 1 - x, y, c, 0), per * i, xnb).wait_recv()
                cp(slot(i, x, 1 - y, c, 1), slot(i, x, 1 - y, c, 1), per * i + 1, ynb).wait_recv()
                for j in range(2):
                    cp(slot(i, j, y, c, 0), slot(i, j, y, c, 0), per * i + 2 + j, ynb).start()
                    cp(slot(i, x, j, c, 1), slot(i, x, j, c, 1), per * i + 4 + j, xnb).start()
                to_sib(i, 1 - x, y, 0).start()
                to_sib(i, x, 1 - y, 1).start()

        def last_to_sibling():
            for i in range(n):
                for j in range(2):
                    cp(slot(i, j, 1 - y, c, 0), slot(i, j, 1 - y, c, 0), per * i + 2 + j, ynb).wait_recv()
                    cp(slot(i, 1 - x, j, c, 1), slot(i, 1 - x, j, c, 1), per * i + 4 + j, xnb).wait_recv()
                    to_sib(i, j, 1 - y, 0).start()
                    to_sib(i, 1 - x, j, 1).start()

        def finish():
            for i in range(n):
                for xx in range(2):
                    for yy in range(2):
                        for h in range(2):
                            s = per * i + 6 + 4 * xx + 2 * yy + h
                            cp(slot(i, xx, yy, 1 - c, h), slot(i, xx, yy, 1 - c, h), s, sib).wait_recv()
                            to_sib(i, xx, yy, h).wait_send()
                cp(own(i, 0), slot(i, x, y, c, 0), per * i, xnb).wait_send()
                cp(own(i, 1), slot(i, x, y, c, 1), per * i + 1, ynb).wait_send()
                for j in range(2):
                    cp(slot(i, j, y, c, 0), slot(i, j, y, c, 0), per * i + 2 + j, ynb).wait_send()
                    cp(slot(i, x, j, c, 1), slot(i, x, j, c, 1), per * i + 4 + j, xnb).wait_send()
            for j, (px, py) in enumerate(chips):
                for i in range(ns):
                    small(px, py, j, i, 2 * px + py).wait_recv()
                    small(px, py, j, i, 2 * x + y).wait_send()

        return [first_hop, second_hop, last_to_sibling, finish]

    out_shapes = [jax.ShapeDtypeStruct((2, 2, 2, 2, a.shape[0] // 2, a.shape[1]), a.dtype) for a in pieces]
    out_shapes += [jax.ShapeDtypeStruct((4,) + a.shape, a.dtype) for a in smalls]
    return _Carry(name, list(pieces) + list(smalls), out_shapes, n_big + 3 * ns, 2 * n + ns, make,
                  (0.0, 0.23, 0.73, 1.0))


def _exchange_carry(name, arrays, out_shapes, plan):
    count = plan([None] * len(arrays), [None] * len(out_shapes), None)

    def make(ins, outs, ssem, rsem, lsem):
        def copies():
            return [_remote(src, dst, ssem.at[j], rsem.at[j], peer)
                    for j, (src, dst, peer) in enumerate(plan(ins, outs, _position()))]

        def start():
            for c in copies():
                c.start()

        def wait():
            for c in copies():
                c.wait()

        return [start, wait]

    return _Carry(name, arrays, out_shapes, count, 0, make, (0.0, 1.0))


class _Grad:
    def __init__(self, arrs, kind, shard_shape):
        self.arrs, self.kind = list(arrs), kind
        self.rows, self.cols = shard_shape
        self.r2 = self.rows // 2

    def view(self, refs, k, h):
        r2 = self.r2
        if self.kind == "list":
            return refs[k].at[pl.ds(h * r2, r2), :]
        if self.kind == "stacked":
            return refs[0].at[k, pl.ds(h * r2, r2), :]
        if self.kind == "col":
            return refs[0].at[pl.ds(h * r2, r2), pl.ds(k * self.cols, self.cols)]
        return refs[0].at[pl.ds((2 * k + h) * r2, r2), :]

    def add_half(self, recv, pos):
        r2, cols = self.r2, self.cols
        n_in = len(self.arrs)
        tr = _row_tile(r2, cols)
        nt = r2 // tr

        def body(pos_ref, *refs):
            m_refs, (r_ref, of_ref, ob_ref) = refs[:n_in], refs[n_in:]
            mine = m_refs[0][...]
            for kk in range(1, n_in):
                mine = jnp.where(pl.program_id(1) == kk, m_refs[kk][...], mine)
            s = mine + r_ref[...]
            of_ref[...] = s
            ob_ref[...] = s.astype(BF16)

        row = lambda t, pos: pos[2] * nt + t
        if self.kind == "list":
            specs = [pl.BlockSpec((tr, cols), lambda t, k, pos: (row(t, pos), 0))] * n_in
        elif self.kind == "stacked":
            specs = [pl.BlockSpec((None, tr, cols), lambda t, k, pos: (k, row(t, pos), 0))]
        elif self.kind == "col":
            specs = [pl.BlockSpec((tr, cols), lambda t, k, pos: (row(t, pos), k))]
        else:
            specs = [pl.BlockSpec((tr, cols), lambda t, k, pos: (2 * k * nt + row(t, pos), 0))]
        blk = pl.BlockSpec((None, tr, cols), lambda t, k, pos: (k, t, 0))
        return pl.pallas_call(
            body, name="rs_add_c",
            grid_spec=pltpu.PrefetchScalarGridSpec(
                num_scalar_prefetch=1, grid=(nt, 4), in_specs=specs + [blk], out_specs=[blk, blk]),
            out_shape=[jax.ShapeDtypeStruct((4, r2, cols), F32), jax.ShapeDtypeStruct((4, r2, cols), BF16)],
            compiler_params=_params(("arbitrary", "arbitrary")),
        )(pos, *self.arrs, recv)


def _row_tile(rows, cols):
    fits = [t for t in range(16, rows + 1, 16) if rows % t == 0 and t * cols * 4 <= 2 * 1024 * 1024]
    return max(fits) if fits else rows


def _adamw_math(w, g, m, v):
    m = ADAM_B1 * m + (1.0 - ADAM_B1) * g
    v = ADAM_B2 * v + (1.0 - ADAM_B2) * (g * g)
    m_hat = m / (1.0 - ADAM_B1 ** ADAM_STEP)
    v_hat = v / (1.0 - ADAM_B2 ** ADAM_STEP)
    delta = -ADAM_LR * (m_hat / (jnp.sqrt(v_hat) + ADAM_EPS) + ADAM_WD * w)
    return delta, m, v


def _adamw_big(w, m, v, own, sib, pos):
    rows, cols = w.shape
    r2 = rows // 2

    tr = _row_tile(r2, cols)
    nt = r2 // tr

    def body(pos_ref, w_ref, m_ref, v_ref, own_ref, sib_ref, g_ref, d_ref, nm_ref, nv_ref):
        h = pl.program_id(0)
        g = jnp.where(h == pos_ref[2], own_ref[...], sib_ref[...])
        g_ref[...] = g
        d_ref[...], nm_ref[...], nv_ref[...] = _adamw_math(w_ref[...], g, m_ref[...], v_ref[...])

    half = pl.BlockSpec((tr, cols), lambda h, t, pos: (h * nt + t, 0))
    piece = pl.BlockSpec((tr, cols), lambda h, t, pos: (t, 0))
    out = jax.ShapeDtypeStruct((rows, cols), F32)
    return pl.pallas_call(
        body, name="adamw",
        grid_spec=pltpu.PrefetchScalarGridSpec(
            num_scalar_prefetch=1, grid=(2, nt),
            in_specs=[half, half, half, piece, piece],
            out_specs=[half, half, half, half]),
        out_shape=[out, out, out, out],
        compiler_params=_params(("arbitrary", "arbitrary")),
    )(pos, w, m, v, own, sib)


def _add_hop1(s1f, recv, pos):
    _, _, r4, cols = recv.shape
    s1v = s1f.reshape(4, 2, r4, cols)

    def body(pos_ref, m_ref, r_ref, of_ref, ob_ref):
        s = m_ref[...] + r_ref[...].astype(F32)
        of_ref[...] = s
        ob_ref[...] = s.astype(BF16)

    def mine(h, j, pos):
        return (jnp.where(h == 0, 2 * j + pos[1], 2 * pos[0] + j), h, 0, 0)

    blk = pl.BlockSpec((None, None, r4, cols), lambda h, j, pos: (h, j, 0, 0))
    return pl.pallas_call(
        body, name="rs_add_1",
        grid_spec=pltpu.PrefetchScalarGridSpec(
            num_scalar_prefetch=1, grid=(2, 2),
            in_specs=[pl.BlockSpec((None, None, r4, cols), mine), blk], out_specs=[blk, blk]),
        out_shape=[jax.ShapeDtypeStruct((2, 2, r4, cols), F32), jax.ShapeDtypeStruct((2, 2, r4, cols), BF16)],
        compiler_params=_params(("arbitrary", "arbitrary")),
    )(pos, s1v, recv)


def _own_sum(s2f, recv3, pos):
    _, _, r4, cols = s2f.shape

    def body(pos_ref, s_ref, r_ref, o_ref):
        o_ref[...] = s_ref[...] + r_ref[...].astype(F32)

    blk = pl.BlockSpec((None, r4, cols), lambda h, pos: (h, 0, 0))
    return pl.pallas_call(
        body, name="own_sum",
        grid_spec=pltpu.PrefetchScalarGridSpec(
            num_scalar_prefetch=1, grid=(2,),
            in_specs=[pl.BlockSpec((None, None, r4, cols),
                                   lambda h, pos: (h, jnp.where(h == 0, pos[0], pos[1]), 0, 0)), blk],
            out_specs=blk),
        out_shape=jax.ShapeDtypeStruct((2, r4, cols), F32),
        compiler_params=_params(("arbitrary",)),
    )(pos, s2f, recv3)


def _allreduce_small(buf):
    def body(x_ref, o_ref, recv, ssem, rsem):
        x, y, c = _position()
        o_ref[...] = x_ref[...]
        for s, peer in enumerate([(x, y, 1 - c), (x, 1 - y, c), (1 - x, y, c)]):
            cp = _remote(o_ref, recv.at[s], ssem.at[s], rsem.at[s], peer)
            cp.start()
            cp.wait()
            o_ref[...] = o_ref[...] + recv[s]

    vm = pl.BlockSpec(memory_space=pltpu.VMEM)
    return pl.pallas_call(
        body, name="allreduce_small", in_specs=[vm], out_specs=vm,
        out_shape=jax.ShapeDtypeStruct(buf.shape, F32),
        scratch_shapes=[pltpu.VMEM((3,) + buf.shape, F32),
                        pltpu.SemaphoreType.DMA((3,)), pltpu.SemaphoreType.DMA((3,))],
    )(buf)


def _adamw_small(w, g, m, v):
    def body(w_ref, g_ref, m_ref, v_ref, d_ref, nm_ref, nv_ref):
        d_ref[...], nm_ref[...], nv_ref[...] = _adamw_math(w_ref[...], g_ref[...], m_ref[...], v_ref[...])

    vm = pl.BlockSpec(memory_space=pltpu.VMEM)
    out = jax.ShapeDtypeStruct(w.shape, F32)
    return pl.pallas_call(body, name="adamw_small", in_specs=[vm] * 4, out_specs=[vm] * 3,
                          out_shape=[out, out, out])(w, g, m, v)


class _ReduceScatter:
    def __init__(self, tag, grads, pos):
        self.tag, self.grads, self.pos, self.stage = tag, grads, pos, 0

    def carry(self):
        grads, n = self.grads, len(self.grads)
        r4 = [g.r2 // 2 for g in grads]

        first = [sum(len(g.arrs) for g in grads[:i]) for i in range(n)]

        def plan_c(ins, outs, p):
            if p is None:
                return 4 * n
            x, y, c = p
            mine = lambda i: ins[first[i]:first[i] + len(grads[i].arrs)]
            return [(grads[i].view(mine(i), k, 1 - c), outs[i].at[k], (x, y, 1 - c))
                    for i in range(n) for k in range(4)]

        def plan_1(ins, outs, p):
            if p is None:
                return 4 * n
            x, y, c = p
            copies = []
            for i in range(n):
                for j in range(2):
                    copies.append((ins[i].at[2 * j + (1 - y), pl.ds(0, r4[i]), :], outs[i].at[0, j],
                                   (x, 1 - y, c)))
                    copies.append((ins[i].at[2 * (1 - x) + j, pl.ds(r4[i], r4[i]), :], outs[i].at[1, j],
                                   (1 - x, y, c)))
            return copies

        def plan_2(ins, outs, p):
            if p is None:
                return 2 * n
            x, y, c = p
            copies = []
            for i in range(n):
                copies.append((ins[i].at[0, 1 - x], outs[i].at[0], (1 - x, y, c)))
                copies.append((ins[i].at[1, 1 - y], outs[i].at[1], (x, 1 - y, c)))
            return copies

        def plan_s(ins, outs, p):
            if p is None:
                return n
            x, y, c = p
            return [(ins[i], outs[i], (x, y, 1 - c)) for i in range(n)]

        shape = lambda lead, dt: [jax.ShapeDtypeStruct(lead(g) + (g.cols,), dt) for g in grads]
        stage = self.stage
        if stage == 0:
            return _exchange_carry(f"rs_{self.tag}_exchange_c", [a for g in grads for a in g.arrs],
                                   shape(lambda g: (4, g.r2), F32), plan_c)
        if stage == 1:
            return _exchange_carry(f"rs_{self.tag}_exchange_1", [s[1] for s in self.s1],
                                   shape(lambda g: (2, 2, g.r2 // 2), BF16), plan_1)
        if stage == 2:
            return _exchange_carry(f"rs_{self.tag}_exchange_2", [s[1] for s in self.s2],
                                   shape(lambda g: (2, g.r2 // 2), BF16), plan_2)
        return _exchange_carry(f"rs_{self.tag}_exchange_sibling", self.own, shape(lambda g: (g.r2,), F32), plan_s)

    def feed(self, recv):
        grads, pos = self.grads, self.pos
        if self.stage == 0:
            self.s1 = [g.add_half(r, pos) for g, r in zip(grads, recv)]
        elif self.stage == 1:
            self.s2 = [_add_hop1(s[0], r, pos) for s, r in zip(self.s1, recv)]
        elif self.stage == 2:
            self.own = [_own_sum(s[0], r, pos).reshape(g.r2, g.cols) for g, s, r in zip(grads, self.s2, recv)]
        else:
            self.sib = list(recv)
        self.stage += 1

    def run(self):
        while self.stage < 4:
            self.feed(_run_carry(self.carry()))

    def adamw(self, weights):
        return [_adamw_big(w, m, v, o, sb, self.pos) for (w, m, v), o, sb in zip(weights, self.own, self.sib)]


def _block_diag(t, nb):
    g, c, p = t.shape
    gb = g // nb
    t = t.reshape(nb, gb, c, p)
    eye = jnp.eye(gb, dtype=t.dtype)
    return jnp.einsum("bgcp,gh->bgchp", t, eye).reshape(nb, gb * c, gb * p)


def _s5_discretise(a_re, a_im, log_dt, b_re, b_im, c_re, c_im):
    g, p = a_re.shape
    nb = g // GROUPS_PER_BLOCK
    dt = jnp.exp(log_dt)[:, None]
    mag = jnp.exp(a_re * dt)
    lam_re = mag * jnp.cos(a_im * dt)
    lam_im = mag * jnp.sin(a_im * dt)
    den = a_re * a_re + a_im * a_im
    q_re = ((lam_re - 1.0) * a_re + lam_im * a_im) / den
    q_im = (lam_im * a_re - (lam_re - 1.0) * a_im) / den
    bb_re = q_re[..., None] * b_re - q_im[..., None] * b_im
    bb_im = q_re[..., None] * b_im + q_im[..., None] * b_re
    tr = lambda t: jnp.swapaxes(t, 1, 2)
    mb = jnp.concatenate([_block_diag(tr(bb_re), nb), _block_diag(tr(bb_im), nb)], axis=-1)
    mc = jnp.concatenate([_block_diag(c_re, nb), -_block_diag(c_im, nb)], axis=-1)
    lam = jnp.concatenate([lam_re.reshape(nb, -1), lam_im.reshape(nb, -1)], axis=-1)
    return mb, mc, lam


def _s5_powers(a_re, a_im, log_dt, sub):
    g, p = a_re.shape
    nb = g // GROUPS_PER_BLOCK
    dt = jnp.exp(log_dt)[:, None]
    ns = list(range(1, sub + 1)) + [2 * sub, 4 * sub]
    ns += [0] * (-len(ns) % SUBLANES)
    e = jnp.asarray(ns, F32)[:, None, None]
    mag = jnp.exp(a_re[None] * dt[None] * e)
    ang = a_im[None] * dt[None] * e
    re = (mag * jnp.cos(ang)).reshape(len(ns), nb, -1)
    im = (mag * jnp.sin(ang)).reshape(len(ns), nb, -1)
    return jnp.transpose(jnp.concatenate([re, im], axis=-1), (1, 0, 2))


def _pack(parts):
    flat = jnp.concatenate([a.reshape(-1).astype(F32) for a in parts])
    n = flat.shape[0]
    pad = -n % (SUBLANES * LANES)
    return jnp.pad(flat, (0, pad)).reshape(-1, LANES)


def _unpack(buf, like):
    flat = buf.reshape(-1)
    out, o = [], 0
    for a in like:
        out.append(flat[o:o + a.size].reshape(a.shape))
        o += a.size
    return out


def kernel(x, meta_tokens, g_ffn1, ffn1_w_gate, ffn1_w_up, ffn1_w_down, g_mix, w_in, b_gate, ssm_a_re, ssm_a_im, ssm_log_dt, ssm_b_re, ssm_b_im, ssm_c_re, ssm_c_im, ssm_d, ssm_w_glu, conv_w, conv_w_out, w_o, g_ffn2, ffn2_w_gate, ffn2_w_up, ffn2_w_down, g_final, loss_target, m_meta_tokens, m_g_ffn1, m_ffn1_w_gate, m_ffn1_w_up, m_ffn1_w_down, m_g_mix, m_w_in, m_b_gate, m_ssm_a_re, m_ssm_a_im, m_ssm_log_dt, m_ssm_b_re, m_ssm_b_im, m_ssm_c_re, m_ssm_c_im, m_ssm_d, m_ssm_w_glu, m_conv_w, m_conv_w_out, m_w_o, m_g_ffn2, m_ffn2_w_gate, m_ffn2_w_up, m_ffn2_w_down, m_g_final, v_meta_tokens, v_g_ffn1, v_ffn1_w_gate, v_ffn1_w_up, v_ffn1_w_down, v_g_mix, v_w_in, v_b_gate, v_ssm_a_re, v_ssm_a_im, v_ssm_log_dt, v_ssm_b_re, v_ssm_b_im, v_ssm_c_re, v_ssm_c_im, v_ssm_d, v_ssm_w_glu, v_conv_w, v_conv_w_out, v_w_o, v_g_ffn2, v_ffn2_w_gate, v_ffn2_w_up, v_ffn2_w_down, v_g_final):
    seq, d = x.shape[1], x.shape[2]
    n_meta = meta_tokens.shape[0]
    dh = d // 2
    tp = -(-(n_meta + seq) // ROW_ALIGN) * ROW_ALIGN
    mx, my, mc_ = _position()
    pos = jnp.stack([mx, my, mc_]).astype(jnp.int32)
    shard = 2 * mx + my

    big_names = ["ffn1_w_gate", "ffn1_w_up", "ffn1_w_down", "w_in", "ssm_w_glu", "conv_w_out", "w_o",
                 "ffn2_w_gate", "ffn2_w_up", "ffn2_w_down"]
    transposed = {0, 1, 7, 8}
    drop = lambda arrs: [jnp.swapaxes(a.reshape(a.shape[1:]), 0, 1) if i in transposed else a.reshape(a.shape[1:])
                         for i, a in enumerate(arrs)]
    big_w = drop([ffn1_w_gate, ffn1_w_up, ffn1_w_down, w_in, ssm_w_glu, conv_w_out, w_o,
                  ffn2_w_gate, ffn2_w_up, ffn2_w_down])
    big_m = drop([m_ffn1_w_gate, m_ffn1_w_up, m_ffn1_w_down, m_w_in, m_ssm_w_glu, m_conv_w_out,
                  m_w_o, m_ffn2_w_gate, m_ffn2_w_up, m_ffn2_w_down])
    big_v = drop([v_ffn1_w_gate, v_ffn1_w_up, v_ffn1_w_down, v_w_in, v_ssm_w_glu, v_conv_w_out,
                  v_w_o, v_ffn2_w_gate, v_ffn2_w_up, v_ffn2_w_down])
    pieces = [_cast_piece(w, pos) for w in big_w]
    conv_local = conv_w.reshape(conv_w.shape[1], conv_w.shape[3])
    n_first = 3
    first = _run_carry(_allgather_carry("allgather_first", pieces[:n_first], [meta_tokens, conv_local]))
    smalls = first[n_first:]
    stack4 = lambda wl: wl.reshape((4, -1, wl.shape[-1]))
    w1g, w1u, w1d = [stack4(wl) for wl in first[:n_first]]
    natural_cols = lambda s: jnp.transpose(s, (1, 0, 2)).reshape(s.shape[1], 4 * s.shape[2])
    meta_full = natural_cols(smalls[0])
    cw_full = natural_cols(smalls[1])
    cw_pad = jnp.pad(cw_full, ((0, SUBLANES - cw_full.shape[0]), (0, 0)))

    s5_args = (ssm_a_re[0], ssm_a_im[0], ssm_log_dt[0], ssm_b_re[0], ssm_b_im[0], ssm_c_re[0], ssm_c_im[0])
    (mb, mc, _), disc_vjp = jax.vjp(_s5_discretise, *s5_args)
    powt = _s5_powers(ssm_a_re[0], ssm_a_im[0], ssm_log_dt[0], SCAN_TILE // SUBLANES)
    mb16, mc16 = mb.astype(BF16), mc.astype(BF16)

    pad_rows = tp - n_meta - seq
    h0 = jnp.concatenate([meta_full, x.reshape(seq, d), jnp.zeros((pad_rows, d), F32)], axis=0)
    tgt = jnp.concatenate([jnp.zeros((n_meta, d), F32), loss_target.reshape(seq, d),
                           jnp.zeros((pad_rows, d), F32)], axis=0)
    h1, a1, b1, *mid = _ffn_fwd(h0, g_ffn1, w1g, w1u, w1d, "ffn1_fwd",
                                carry=_allgather_carry("allgather_mixer", pieces[3:7], []))
    win_all, wglu_s, wco_s, wo_s = [stack4(wl) for wl in mid]
    wglu_all = natural_cols(wglu_s)
    wco_all = natural_cols(wco_s)
    wo_all = wo_s.reshape(d, d)
    u, p, w2g, w2u = _win_fwd(h1, g_mix, win_all, carry=_allgather_carry("allgather_ffn2_in", pieces[7:9], []))
    ys5, bnd = _scan_fwd(p, mb16, mc16, powt, ssm_d)
    h2, w2d = _mix_fwd(h1, ys5, p, cw_pad, b_gate, wglu_all, wco_all, wo_all,
                       carry=_allgather_carry("allgather_ffn2_out", pieces[9:], []))
    w2g, w2u, w2d = stack4(w2g), stack4(w2u), stack4(w2d)
    dh3, a2, b2, dg_final, loss_part = _ffn_fwd(
        h2, g_ffn2, w2g, w2u, w2d, "ffn2_fwd_loss", final=(g_final.reshape(1, d), tgt, n_meta, seq))

    dh2, dw2g, dw2u, dw2d, dg_ffn2 = _ffn_bwd(dh3, h2, g_ffn2, a2, b2, w2g, w2u, w2d, "ffn2_bwd")
    dys5, dpb, dwo, dwglu, dwco, dcw, dbg = _mix_bwd(dh2, ys5, p, cw_pad, b_gate, wglu_all, wco_all, wo_all)
    dug, dmb, dmc, dlam, dd = _scan_bwd(p, dys5, mb16, mc16, powt, ssm_d, bnd)
    dh1, dwin, dg_mix = _win_bwd(dpb, dug, u, win_all, h1, g_mix, dh2)
    shapes = [w.shape for w in big_w]
    kinds = ["list", "list", "list", "list", "col", "col", "row", "list", "list", "list"]
    rest_grads = [dwin, [dwglu], [dwco], [dwo], dw2g, dw2u, dw2d]
    rs_rest = _ReduceScatter("rest", [_Grad(a, k, s) for a, k, s in
                                      zip(rest_grads, kinds[n_first:], shapes[n_first:])], pos)
    dh0, dw1g, dw1u, dw1d, dg_ffn1 = _ffn_bwd(dh1, h0, g_ffn1, a1, b1, w1g, w1u, w1d, "ffn1_bwd", chain=rs_rest)
    rs_first = _ReduceScatter("first", [_Grad(a, k, s) for a, k, s in
                                        zip([dw1g, dw1u, dw1d], kinds[:n_first], shapes[:n_first])], pos)
    rs_first.run()
    wmv = list(zip(big_w, big_m, big_v))
    big_out = rs_first.adamw(wmv[:n_first]) + rs_rest.adamw(wmv[n_first:])
    def lead(i, o):
        o = jnp.swapaxes(o, 0, 1) if i in transposed else o
        return o.reshape((1,) + o.shape)

    big_out = {nme: tuple(lead(i, o) for o in outs) for i, (nme, outs) in enumerate(zip(big_names, big_out))}

    s5_grads = disc_vjp((dmb, dmc, jnp.sum(dlam, axis=1)))
    grad_x = dh0[n_meta:n_meta + seq][None]

    small_names = ["g_ffn1", "g_mix", "b_gate", "ssm_a_re", "ssm_a_im", "ssm_log_dt", "ssm_b_re", "ssm_b_im",
                   "ssm_c_re", "ssm_c_im", "ssm_d", "g_ffn2", "g_final", "meta_tokens", "conv_w"]
    small_w = [g_ffn1, g_mix, b_gate, ssm_a_re, ssm_a_im, ssm_log_dt, ssm_b_re, ssm_b_im, ssm_c_re, ssm_c_im,
               ssm_d, g_ffn2, g_final, meta_tokens, conv_w]
    small_m = [m_g_ffn1, m_g_mix, m_b_gate, m_ssm_a_re, m_ssm_a_im, m_ssm_log_dt, m_ssm_b_re, m_ssm_b_im,
               m_ssm_c_re, m_ssm_c_im, m_ssm_d, m_g_ffn2, m_g_final, m_meta_tokens, m_conv_w]
    small_v = [v_g_ffn1, v_g_mix, v_b_gate, v_ssm_a_re, v_ssm_a_im, v_ssm_log_dt, v_ssm_b_re, v_ssm_b_im,
               v_ssm_c_re, v_ssm_c_im, v_ssm_d, v_g_ffn2, v_g_final, v_meta_tokens, v_conv_w]
    local_small = [dg_ffn1, dg_mix, dbg, *s5_grads, jnp.sum(dd, axis=0), dg_ffn2, dg_final,
                   dh0[:n_meta], dcw[:conv_w.shape[1]]]
    reduced = _unpack(_allreduce_small(_pack(local_small)), local_small)
    reduced[-2] = lax.dynamic_slice_in_dim(reduced[-2], shard * meta_tokens.shape[1], meta_tokens.shape[1], 1)
    reduced[-1] = lax.dynamic_slice_in_dim(reduced[-1], shard * conv_w.shape[3], conv_w.shape[3], 1)
    small_g = [r.reshape(w.shape) for r, w in zip(reduced, small_w)]
    ds_, nm_, nv_ = _adamw_small(_pack(small_w), _pack(small_g), _pack(small_m), _pack(small_v))
    small_out = {nme: o for nme, o in zip(
        small_names, zip(small_g, _unpack(ds_, small_w), _unpack(nm_, small_w), _unpack(nv_, small_w)))}

    loss = lax.psum(loss_part[0, 0], ("x", "y", "c"))
    order = ["meta_tokens", "g_ffn1", "ffn1_w_gate", "ffn1_w_up", "ffn1_w_down", "g_mix", "w_in", "b_gate",
             "ssm_a_re", "ssm_a_im", "ssm_log_dt", "ssm_b_re", "ssm_b_im", "ssm_c_re", "ssm_c_im", "ssm_d",
             "ssm_w_glu", "conv_w", "conv_w_out", "w_o", "g_ffn2", "ffn2_w_gate", "ffn2_w_up", "ffn2_w_down",
             "g_final"]
    res = {**big_out, **small_out}
    return (loss, grad_x, *[res[nme][0] for nme in order], *[res[nme][1] for nme in order],
            *[res[nme][2] for nme in order], *[res[nme][3] for nme in order])
```

```python
import functools
import math

import jax
import jax.numpy as jnp
from jax import lax
from jax.experimental import pallas as pl
from jax.experimental.pallas import tpu as pltpu

F32 = jnp.float32
BF16 = jnp.bfloat16
MESH = pl.DeviceIdType.MESH

RMS_EPS = 1e-6
ADAM_LR = 0.001
ADAM_B1 = 0.9
ADAM_B2 = 0.999
ADAM_EPS = 1e-08
ADAM_WD = 0.01
ADAM_STEP = 10

LANES = 128
SUBLANES = 8
VMEM_LIMIT = 56 * 1024 * 1024

ROW_ALIGN = 256
SCAN_TILE = 256
GROUPS_PER_BLOCK = 8


def _params(sem, vmem=VMEM_LIMIT):
    return pltpu.CompilerParams(dimension_semantics=sem, vmem_limit_bytes=vmem)


def _pick_tile(n, candidates):
    for c in candidates:
        if n % c == 0:
            return c
    raise ValueError(f"no tile for {n}")


def _dot(a, b):
    return jnp.dot(a, b, preferred_element_type=F32)


def _dot_nt(a, b):
    return lax.dot_general(a, b, (((1,), (1,)), ((), ())), preferred_element_type=F32)


def _dot_tn(a, b):
    return lax.dot_general(a, b, (((0,), (0,)), ((), ())), preferred_element_type=F32)


def _sigmoid(x):
    return 1.0 / (1.0 + jnp.exp(-x))


def _rms_stats(h):
    r = lax.rsqrt(jnp.mean(h * h, axis=-1, keepdims=True) + RMS_EPS)
    return h * r, r


def _rms_bwd(xhat, r, g, dn):
    dxh = dn * g
    return r * (dxh - xhat * jnp.mean(dxh * xhat, axis=-1, keepdims=True))


GELU_K = math.sqrt(2.0 / math.pi)
GELU_C = 0.044715


def _gelu(x):
    return 0.5 * x * (1.0 + jnp.tanh(GELU_K * (x + GELU_C * x * x * x)))


def _gelu_grad(x):
    t = jnp.tanh(GELU_K * (x + GELU_C * x * x * x))
    return 0.5 * (1.0 + t) + 0.5 * x * (1.0 - t * t) * GELU_K * (1.0 + 3.0 * GELU_C * x * x)


def _ffn_fwd(h, g, wg, wu, wd, name, final=None, carry=None):
    tp, d = h.shape
    ns, f4, _ = wg.shape
    tm = _pick_tile(tp, (768, 512, 256))
    ni = tp // tm

    def body(*refs):
        refs, phases = split(refs)
        if final is None:
            h_ref, g_ref, wg_ref, wu_ref, wd_ref, ho_ref, a_ref, b_ref, n_scr, acc = refs
        else:
            (h_ref, g_ref, wg_ref, wu_ref, wd_ref, gf_ref, tg_ref,
             ho_ref, a_ref, b_ref, n_scr, dgf_ref, loss_ref, dy_ref, acc) = refs
        i = pl.program_id(0)
        k = pl.program_id(1)
        _run_phases(phases, carry, i * ns + k, ni * ns)

        @pl.when(k == 0)
        def _():
            xhat, _ = _rms_stats(h_ref[...])
            n_scr[...] = (xhat * g_ref[...]).astype(BF16)
            acc[...] = jnp.zeros_like(acc)

        n = n_scr[...]
        a = _dot_nt(n, wg_ref[...])
        b = _dot_nt(n, wu_ref[...])
        a_ref[...] = a.astype(BF16)
        b_ref[...] = b.astype(BF16)
        s = (a * _sigmoid(a) * b).astype(BF16)
        acc[...] += _dot(s, wd_ref[...])

        if final is None:
            @pl.when(k == ns - 1)
            def _():
                ho_ref[...] = h_ref[...] + 0.5 * acc[...]
        else:
            n_meta, seq = final[2], final[3]

            @pl.when((i == 0) & (k == 0))
            def _():
                dgf_ref[...] = jnp.zeros_like(dgf_ref)
                loss_ref[...] = jnp.zeros_like(loss_ref)

            @pl.when(k == ns - 1)
            def _():
                h3 = h_ref[...] + 0.5 * acc[...]
                xhat, r = _rms_stats(h3)
                gf = gf_ref[...]
                row = i * tm + lax.broadcasted_iota(jnp.int32, (tm, d), 0)
                valid = (row >= n_meta) & (row < n_meta + seq)
                diff = jnp.where(valid, xhat * gf - tg_ref[...], 0.0)
                dout = diff * (1.0 / d)
                loss_ref[...] += jnp.full(loss_ref.shape, 0.5 * jnp.sum(diff * diff) * (1.0 / d), F32)
                dgf_ref[...] += jnp.sum(dout * xhat, axis=0, keepdims=True)
                dh3 = _rms_bwd(xhat, r, gf, dout)
                ho_ref[...] = dh3
                dy_ref[...] = (0.5 * dh3).astype(BF16)

    row_spec = pl.BlockSpec((tm, d), lambda i, k: (i, 0))
    vec_spec = pl.BlockSpec((1, d), lambda i, k: (0, 0))
    in_specs = [row_spec, vec_spec,
                pl.BlockSpec((None, f4, d), lambda i, k: (k, 0, 0)),
                pl.BlockSpec((None, f4, d), lambda i, k: (k, 0, 0)),
                pl.BlockSpec((None, f4, d), lambda i, k: (k, 0, 0))]
    act_spec = pl.BlockSpec((None, tm, f4), lambda i, k: (k, i, 0))
    out_specs = [row_spec, act_spec, act_spec, row_spec]
    out_shape = [jax.ShapeDtypeStruct((tp, d), F32),
                 jax.ShapeDtypeStruct((ns, tp, f4), BF16),
                 jax.ShapeDtypeStruct((ns, tp, f4), BF16),
                 jax.ShapeDtypeStruct((tp, d), BF16)]
    args = [h, g, wg, wu, wd]
    if final is not None:
        in_specs += [vec_spec, row_spec]
        args += [final[0], final[1]]
        out_specs += [vec_spec, pl.BlockSpec((1, LANES), lambda i, k: (0, 0)), row_spec]
        out_shape += [jax.ShapeDtypeStruct((1, d), F32), jax.ShapeDtypeStruct((1, LANES), F32),
                      jax.ShapeDtypeStruct((tp, d), BF16)]
    scratch = [pltpu.VMEM((tm, d), F32)]
    split = _attach_carry(carry, in_specs, args, out_specs, out_shape, scratch)
    return pl.pallas_call(
        body, name=name, grid=(ni, ns), in_specs=in_specs, out_specs=out_specs, out_shape=out_shape,
        scratch_shapes=scratch, compiler_params=_params(("arbitrary", "arbitrary")),
    )(*args)


def _ffn_bwd_shard(k, ns, dn_prev, dy, n, a, b, wg, wu, wd, tail, name, carry=None):
    tp, d = n.shape
    f4 = wg.shape[1]
    tm = _pick_tile(tp, (768, 512, 256))
    ni = tp // tm
    first, last = k == 0, k == ns - 1

    def body(*refs):
        refs, phases = split(refs)
        acc_in = None if first else refs.pop(0)
        if last:
            dh_ref, h_ref, g_ref = refs[:3]
        else:
            dy_ref, n_ref = refs[:2]
        refs = refs[3 if last else 2:]
        a_ref, b_ref, wg_hbm, wu_hbm, wd_hbm = refs[:5]
        refs = refs[5:]
        acc_out, dwg_hbm, dwu_hbm, dwd_hbm = refs[:4]
        rest = refs[4:]
        dg_ref = rest.pop(0) if last else None
        wg_ref, wu_ref, wd_ref, dwg_ref, dwu_ref, dwd_ref = rest
        i = pl.program_id(0)
        _run_phases(phases, carry, i, ni)

        @pl.when(i == 0)
        def _():
            pltpu.sync_copy(wg_hbm.at[k], wg_ref)
            pltpu.sync_copy(wu_hbm.at[k], wu_ref)
            pltpu.sync_copy(wd_hbm.at[k], wd_ref)
            dwg_ref[...] = jnp.zeros_like(dwg_ref)
            dwu_ref[...] = jnp.zeros_like(dwu_ref)
            dwd_ref[...] = jnp.zeros_like(dwd_ref)
            if last:
                dg_ref[...] = jnp.zeros_like(dg_ref)

        if last:
            xhat, r = _rms_stats(h_ref[...])
            n = (xhat * g_ref[...]).astype(BF16)
            dy = (0.5 * dh_ref[...]).astype(BF16)
        else:
            n = n_ref[...]
            dy = dy_ref[...]
        av = a_ref[...].astype(F32)
        bv = b_ref[...].astype(F32)
        sg = _sigmoid(av)
        silu = av * sg
        ds = _dot_nt(dy, wd_ref[...])
        da = (ds * bv * (sg * (1.0 + av * (1.0 - sg)))).astype(BF16)
        db = (ds * silu).astype(BF16)
        s = (silu * bv).astype(BF16)
        dwd_ref[...] += _dot_tn(s, dy)
        dwg_ref[...] += _dot_tn(da, n)
        dwu_ref[...] += _dot_tn(db, n)
        dn = _dot(da, wg_ref[...]) + _dot(db, wu_ref[...])
        if not first:
            dn = dn + acc_in[...]
        if last:
            dg_ref[...] += jnp.sum(dn * xhat, axis=0, keepdims=True)
            acc_out[...] = dh_ref[...] + _rms_bwd(xhat, r, g_ref[...], dn)
        else:
            acc_out[...] = dn

        @pl.when(i == ni - 1)
        def _():
            for acc_ref, stage_ref, out_hbm in ((dwg_ref, wg_ref, dwg_hbm), (dwu_ref, wu_ref, dwu_hbm),
                                                (dwd_ref, wd_ref, dwd_hbm)):
                stage_ref[...] = acc_ref[...].astype(BF16)
                pltpu.sync_copy(stage_ref, out_hbm)

    row_spec = pl.BlockSpec((tm, d), lambda i: (i, 0))
    vec_spec = pl.BlockSpec((1, d), lambda i: (0, 0))
    act_spec = pl.BlockSpec((None, tm, f4), lambda i: (k, i, 0))
    in_specs = [act_spec, act_spec, ANY, ANY, ANY]
    args = [a, b, wg, wu, wd]
    if last:
        in_specs = [row_spec, row_spec, vec_spec] + in_specs
        args = list(tail) + args
    else:
        in_specs = [row_spec, row_spec] + in_specs
        args = [dy, n] + args
    if not first:
        in_specs.insert(0, row_spec)
        args.insert(0, dn_prev)
    out_specs = [row_spec, ANY, ANY, ANY]
    out_shape = [jax.ShapeDtypeStruct((tp, d), F32)] + [jax.ShapeDtypeStruct((f4, d), BF16)] * 3
    if last:
        out_specs.append(vec_spec)
        out_shape.append(jax.ShapeDtypeStruct((1, d), F32))
    n_host = len(out_shape)
    scratch = [pltpu.VMEM((f4, d), BF16)] * 3 + [pltpu.VMEM((f4, d), F32)] * 3
    split = _attach_carry(carry, in_specs, args, out_specs, out_shape, scratch)
    outs = pl.pallas_call(
        body, name=f"{name}_{k}", grid=(ni,), in_specs=in_specs, out_specs=out_specs, out_shape=out_shape,
        scratch_shapes=scratch, compiler_params=_params(("arbitrary",)),
    )(*args)
    return outs[:n_host], outs[n_host:]


def _ffn_bwd(dh_out, dy, h_in, n, g, a, b, wg, wu, wd, name, chain=None):
    ns = wg.shape[0]
    acc, dwg, dwu, dwd, dg = None, [], [], [], None
    for k in range(ns):
        carry = chain.carry() if chain is not None else None
        outs, carried = _ffn_bwd_shard(k, ns, acc, dy, n, a, b, wg, wu, wd, (dh_out, h_in, g), name, carry)
        if chain is not None:
            chain.feed(carried)
        acc = outs[0]
        dwg.append(outs[1])
        dwu.append(outs[2])
        dwd.append(outs[3])
        if k == ns - 1:
            dg = outs[4]
    return acc, dwg, dwu, dwd, dg


def _win_fwd(h, g, w_in, carry=None):
    tp, d = h.shape
    ns = w_in.shape[0]
    tm = _pick_tile(tp, (768, 512, 256))
    ni = tp // tm

    def body(*refs):
        (h_ref, g_ref, w_ref, u_ref, p_ref), phases = split(refs)
        _run_phases(phases, carry, pl.program_id(0), ni)
        xhat, _ = _rms_stats(h_ref[...])
        u = (xhat * g_ref[...]).astype(BF16)
        u_ref[...] = u
        for k in range(ns):
            p_ref[k] = _dot(u, w_ref[k]).astype(BF16)

    in_specs = [pl.BlockSpec((tm, d), lambda i: (i, 0)),
                pl.BlockSpec((1, d), lambda i: (0, 0)),
                pl.BlockSpec((ns, d, d), lambda i: (0, 0, 0))]
    out_specs = [pl.BlockSpec((tm, d), lambda i: (i, 0)),
                 pl.BlockSpec((ns, tm, d), lambda i: (0, i, 0))]
    out_shape = [jax.ShapeDtypeStruct((tp, d), BF16), jax.ShapeDtypeStruct((ns, tp, d), BF16)]
    args, scratch = [h, g, w_in], []
    split = _attach_carry(carry, in_specs, args, out_specs, out_shape, scratch)
    return pl.pallas_call(
        body, name="win_fwd", grid=(ni,), in_specs=in_specs, out_specs=out_specs, out_shape=out_shape,
        scratch_shapes=scratch, compiler_params=_params(("arbitrary",)),
    )(*args)


def _win_bwd_shard(k, ns, du_prev, dpb, dug, u, w_in, h1, g, dh2):
    tp, d = h1.shape
    dh = d // 2
    tm = _pick_tile(tp, (768, 512, 256))
    first, last = k == 0, k == ns - 1

    def body(*refs):
        refs = list(refs)
        acc_in = None if first else refs.pop(0)
        dug_ref = refs.pop(0) if first else None
        dp_ref, u_ref, w_ref = refs[:3]
        refs = refs[3:]
        if last:
            h_ref, g_ref, dh2_ref, acc_out, dw_ref, dg_ref, dy_ref, dw_acc = refs
        else:
            acc_out, dw_ref, dw_acc = refs
        i = pl.program_id(0)

        @pl.when(i == 0)
        def _():
            dw_acc[...] = jnp.zeros_like(dw_acc)
            if last:
                dg_ref[...] = jnp.zeros_like(dg_ref)

        dp = dp_ref[...]
        if first:
            dp = jnp.concatenate([dug_ref[...], dp[:, dh:]], axis=1)
        dw_acc[...] += _dot_tn(u_ref[...], dp)
        du = _dot_nt(dp, w_ref[...])
        if not first:
            du = du + acc_in[...]
        if last:
            xhat, r = _rms_stats(h_ref[...])
            dg_ref[...] += jnp.sum(du * xhat, axis=0, keepdims=True)
            dh1 = dh2_ref[...] + _rms_bwd(xhat, r, g_ref[...], du)
            acc_out[...] = dh1
            dy_ref[...] = (0.5 * dh1).astype(BF16)
        else:
            acc_out[...] = du

        @pl.when(i == tp // tm - 1)
        def _():
            dw_ref[...] = dw_acc[...].astype(BF16)

    row_spec = pl.BlockSpec((tm, d), lambda i: (i, 0))
    vec_spec = pl.BlockSpec((1, d), lambda i: (0, 0))
    in_specs = [pl.BlockSpec((None, tm, d), lambda i: (k, i, 0)), row_spec,
                pl.BlockSpec((None, d, d), lambda i: (k, 0, 0))]
    args = [dpb, u, w_in]
    if first:
        in_specs.insert(0, pl.BlockSpec((tm, dh), lambda i: (i, 0)))
        args.insert(0, dug)
    else:
        in_specs.insert(0, row_spec)
        args.insert(0, du_prev)
    out_specs = [row_spec, pl.BlockSpec((d, d), lambda i: (0, 0))]
    out_shape = [jax.ShapeDtypeStruct((tp, d), F32), jax.ShapeDtypeStruct((d, d), BF16)]
    if last:
        in_specs += [row_spec, vec_spec, row_spec]
        args += [h1, g, dh2]
        out_specs += [vec_spec, row_spec]
        out_shape += [jax.ShapeDtypeStruct((1, d), F32), jax.ShapeDtypeStruct((tp, d), BF16)]
    return pl.pallas_call(
        body, name=f"win_bwd_{k}", grid=(tp // tm,), in_specs=in_specs, out_specs=out_specs,
        out_shape=out_shape, scratch_shapes=[pltpu.VMEM((d, d), F32)],
        compiler_params=_params(("arbitrary",)),
    )(*args)


def _win_bwd(dpb, dug, u, w_in, h1, g, dh2):
    ns = w_in.shape[0]
    acc, dws = None, []
    for k in range(ns):
        outs = _win_bwd_shard(k, ns, acc, dpb, dug, u, w_in, h1, g, dh2)
        acc = outs[0]
        dws.append(outs[1])
    return acc, dws, outs[2], outs[3]


def _cmul(ar, ai, br, bi):
    return ar * br - ai * bi, ar * bi + ai * br


def _scan_rows(j, sub):
    return pl.ds(j * SUBLANES, SUBLANES)


def _permute_rows(src_ref, dst_ref, sub):
    for j in range(sub):
        dst_ref[pl.ds(j * SUBLANES, SUBLANES), :] = src_ref[pl.ds(j, SUBLANES, stride=sub), :]


def _unpermute_rows(src_ref, dst_ref, sub):
    for j in range(sub):
        dst_ref[pl.ds(j, SUBLANES, stride=sub), :] = src_ref[pl.ds(j * SUBLANES, SUBLANES), :]


def _local_scan(x_ref, lr, li, w, sub, reverse):
    hr = jnp.zeros((SUBLANES, w), F32)
    hi = jnp.zeros((SUBLANES, w), F32)
    order = range(sub - 1, -1, -1) if reverse else range(sub)
    for j in order:
        xr = x_ref[_scan_rows(j, sub), pl.ds(0, w)]
        xi = x_ref[_scan_rows(j, sub), pl.ds(w, w)]
        if reverse:
            hr, hi = lr * hr + li * hi + xr, lr * hi - li * hr + xi
        else:
            hr, hi = lr * hr - li * hi + xr, lr * hi + li * hr + xi
        x_ref[_scan_rows(j, sub), pl.ds(0, w)] = hr
        x_ref[_scan_rows(j, sub), pl.ds(w, w)] = hi
    return hr, hi


def _entering_states(er, ei, fr, fi, pow_ref, w, sub, reverse):
    lane = lax.broadcasted_iota(jnp.int32, (SUBLANES, w), 0)
    if reverse:
        edge, shift1 = SUBLANES - 1, SUBLANES - 1
    else:
        edge, shift1 = 0, 1
    zr = jnp.where(lane == edge, pltpu.roll(fr, shift1, 0), pltpu.roll(er, shift1, 0))
    zi = jnp.where(lane == edge, pltpu.roll(fi, shift1, 0), pltpu.roll(ei, shift1, 0))
    for step, row in ((1, sub - 1), (2, sub), (4, sub + 1)):
        ar = pow_ref[pl.ds(row, 1), pl.ds(0, w)]
        ai = pow_ref[pl.ds(row, 1), pl.ds(w, w)]
        if reverse:
            ai = -ai
            keep = lane < SUBLANES - step
            sr = jnp.where(keep, pltpu.roll(zr, SUBLANES - step, 0), 0.0)
            si = jnp.where(keep, pltpu.roll(zi, SUBLANES - step, 0), 0.0)
        else:
            keep = lane >= step
            sr = jnp.where(keep, pltpu.roll(zr, step, 0), 0.0)
            si = jnp.where(keep, pltpu.roll(zi, step, 0), 0.0)
        pr, pi = _cmul(ar, ai, sr, si)
        zr, zi = zr + pr, zi + pi
    ar = pow_ref[pl.ds(sub - 1, 1), pl.ds(0, w)]
    ai = pow_ref[pl.ds(sub - 1, 1), pl.ds(w, w)]
    if reverse:
        ai = -ai
    pr, pi = _cmul(ar, ai, zr, zi)
    return zr, zi, er + pr, ei + pi


def _scan_fwd(p, mb, mc, powt, dskip):
    _, tp, d = p.shape
    nb, cb, w2 = mb.shape
    w = w2 // 2
    q = SCAN_TILE
    sub = q // SUBLANES
    nt = tp // q
    ds = d // 2

    def body(ug_ref, mb_ref, mc_ref, pow_ref, d_ref, y_ref, bnd_ref, x_scr, carry, nat, perm):
        t = pl.program_id(1)

        @pl.when(t == 0)
        def _():
            carry[...] = jnp.zeros_like(carry)

        ugf = ug_ref[...].astype(F32)
        nat[...] = ugf
        _permute_rows(nat, perm, sub)
        x_scr[...] = _dot(perm[...].astype(BF16), mb_ref[...])
        lr = jnp.broadcast_to(pow_ref[pl.ds(0, 1), pl.ds(0, w)], (SUBLANES, w))
        li = jnp.broadcast_to(pow_ref[pl.ds(0, 1), pl.ds(w, w)], (SUBLANES, w))
        er, ei = _local_scan(x_scr, lr, li, w, sub, False)
        zr, zi, fr, fi = _entering_states(er, ei, carry[:, pl.ds(0, w)], carry[:, pl.ds(w, w)],
                                          pow_ref, w, sub, False)
        carry[:, pl.ds(0, w)] = fr
        carry[:, pl.ds(w, w)] = fi
        bnd_ref[:, pl.ds(0, w)] = fr
        bnd_ref[:, pl.ds(w, w)] = fi
        for j in range(sub):
            pr = pow_ref[pl.ds(j, 1), pl.ds(0, w)]
            pi = pow_ref[pl.ds(j, 1), pl.ds(w, w)]
            cr, ci = _cmul(pr, pi, zr, zi)
            x_scr[_scan_rows(j, sub), pl.ds(0, w)] += cr
            x_scr[_scan_rows(j, sub), pl.ds(w, w)] += ci
        hb = x_scr[...].astype(BF16)
        perm[...] = _dot_nt(hb, mc_ref[...])
        _unpermute_rows(perm, nat, sub)
        y_ref[...] = nat[...] + d_ref[...] * ugf

    in_specs = [pl.BlockSpec((None, q, cb), lambda b, t: (0, t, b)),
                pl.BlockSpec((None, cb, w2), lambda b, t: (b, 0, 0)),
                pl.BlockSpec((None, cb, w2), lambda b, t: (b, 0, 0)),
                pl.BlockSpec((None, powt.shape[1], w2), lambda b, t: (b, 0, 0)),
                pl.BlockSpec((1, cb), lambda b, t: (0, b))]
    out_specs = [pl.BlockSpec((q, cb), lambda b, t: (t, b)),
                 pl.BlockSpec((None, None, SUBLANES, w2), lambda b, t: (b, t, 0, 0))]
    out_shape = [jax.ShapeDtypeStruct((tp, ds), F32), jax.ShapeDtypeStruct((nb, nt, SUBLANES, w2), F32)]
    scratch = [pltpu.VMEM((q, w2), F32), pltpu.VMEM((SUBLANES, w2), F32),
               pltpu.VMEM((q, cb), F32), pltpu.VMEM((q, cb), F32)]
    return pl.pallas_call(
        body, name="s5_scan_fwd", grid=(nb, nt), in_specs=in_specs, out_specs=out_specs,
        out_shape=out_shape, scratch_shapes=scratch, compiler_params=_params(("arbitrary", "arbitrary")),
    )(p, mb, mc, powt, dskip)


def _scan_bwd(p, dy, mb, mc, powt, dskip, bnd):
    _, tp, d = p.shape
    nb, cb, w2 = mb.shape
    w = w2 // 2
    q = SCAN_TILE
    sub = q // SUBLANES
    nt = tp // q
    ds = d // 2

    def body(ug_ref, dy_ref, mb_ref, mc_ref, pow_ref, d_ref, bnd_ref,
             dug_ref, dmb_ref, dmc_ref, dlam_ref, dd_ref, x_scr, y_scr, gcarry, nat, perm):
        t = pl.program_id(1)
        tt = nt - 1 - t

        @pl.when(t == 0)
        def _():
            gcarry[...] = jnp.zeros_like(gcarry)
            dmb_ref[...] = jnp.zeros_like(dmb_ref)
            dmc_ref[...] = jnp.zeros_like(dmc_ref)
            dlam_ref[...] = jnp.zeros_like(dlam_ref)
            dd_ref[...] = jnp.zeros_like(dd_ref)

        ugf = ug_ref[...].astype(F32)
        dyf = dy_ref[...].astype(F32)
        dd_ref[...] += jnp.sum((dyf * ugf).reshape(q // SUBLANES, SUBLANES, cb), axis=0)
        nat[...] = ugf
        _permute_rows(nat, perm, sub)
        ug = perm[...].astype(BF16)
        nat[...] = dyf
        _permute_rows(nat, perm, sub)
        dyb = perm[...].astype(BF16)
        lr = jnp.broadcast_to(pow_ref[pl.ds(0, 1), pl.ds(0, w)], (SUBLANES, w))
        li = jnp.broadcast_to(pow_ref[pl.ds(0, 1), pl.ds(w, w)], (SUBLANES, w))

        x_scr[...] = _dot(ug, mb_ref[...])
        er, ei = _local_scan(x_scr, lr, li, w, sub, False)
        first = tt == 0
        pfr = jnp.where(first, 0.0, bnd_ref[:, pl.ds(0, w)])
        pfi = jnp.where(first, 0.0, bnd_ref[:, pl.ds(w, w)])
        hzr, hzi, _, _ = _entering_states(er, ei, pfr, pfi, pow_ref, w, sub, False)
        for j in range(sub):
            pr = pow_ref[pl.ds(j, 1), pl.ds(0, w)]
            pi = pow_ref[pl.ds(j, 1), pl.ds(w, w)]
            cr, ci = _cmul(pr, pi, hzr, hzi)
            x_scr[_scan_rows(j, sub), pl.ds(0, w)] += cr
            x_scr[_scan_rows(j, sub), pl.ds(w, w)] += ci

        y_scr[...] = _dot(dyb, mc_ref[...])
        er, ei = _local_scan(y_scr, lr, li, w, sub, True)
        gzr, gzi, fr, fi = _entering_states(er, ei, gcarry[:, pl.ds(0, w)], gcarry[:, pl.ds(w, w)],
                                            pow_ref, w, sub, True)
        gcarry[:, pl.ds(0, w)] = fr
        gcarry[:, pl.ds(w, w)] = fi
        accr = jnp.zeros((SUBLANES, w), F32)
        acci = jnp.zeros((SUBLANES, w), F32)
        for j in range(sub):
            pr = pow_ref[pl.ds(sub - 1 - j, 1), pl.ds(0, w)]
            pi = pow_ref[pl.ds(sub - 1 - j, 1), pl.ds(w, w)]
            cr, ci = _cmul(pr, -pi, gzr, gzi)
            gr = y_scr[_scan_rows(j, sub), pl.ds(0, w)] + cr
            gi = y_scr[_scan_rows(j, sub), pl.ds(w, w)] + ci
            y_scr[_scan_rows(j, sub), pl.ds(0, w)] = gr
            y_scr[_scan_rows(j, sub), pl.ds(w, w)] = gi
            if j == 0:
                hpr, hpi = hzr, hzi
            else:
                hpr = x_scr[_scan_rows(j - 1, sub), pl.ds(0, w)]
                hpi = x_scr[_scan_rows(j - 1, sub), pl.ds(w, w)]
            accr += hpr * gr + hpi * gi
            acci += hpr * gi - hpi * gr
        dlam_ref[:, pl.ds(0, w)] += accr
        dlam_ref[:, pl.ds(w, w)] += acci

        hb = x_scr[...].astype(BF16)
        gb = y_scr[...].astype(BF16)
        dmc_ref[...] += _dot_tn(dyb, hb)
        dmb_ref[...] += _dot_tn(ug, gb)
        perm[...] = _dot_nt(gb, mb_ref[...])
        _unpermute_rows(perm, nat, sub)
        dug_ref[...] = (nat[...] + d_ref[...] * dyf).astype(BF16)

    blk = lambda b, t: (b, 0, 0)
    return pl.pallas_call(
        body, name="s5_scan_bwd", grid=(nb, nt),
        in_specs=[pl.BlockSpec((None, q, cb), lambda b, t: (0, nt - 1 - t, b)),
                  pl.BlockSpec((q, cb), lambda b, t: (nt - 1 - t, b)),
                  pl.BlockSpec((None, cb, w2), blk),
                  pl.BlockSpec((None, cb, w2), blk),
                  pl.BlockSpec((None, powt.shape[1], w2), blk),
                  pl.BlockSpec((1, cb), lambda b, t: (0, b)),
                  pl.BlockSpec((None, None, SUBLANES, w2),
                               lambda b, t: (b, jnp.maximum(nt - 2 - t, 0), 0, 0))],
        out_specs=[pl.BlockSpec((q, cb), lambda b, t: (nt - 1 - t, b)),
                   pl.BlockSpec((None, cb, w2), blk),
                   pl.BlockSpec((None, cb, w2), blk),
                   pl.BlockSpec((None, SUBLANES, w2), blk),
                   pl.BlockSpec((SUBLANES, cb), lambda b, t: (0, b))],
        out_shape=[jax.ShapeDtypeStruct((tp, ds), BF16),
                   jax.ShapeDtypeStruct((nb, cb, w2), F32),
                   jax.ShapeDtypeStruct((nb, cb, w2), F32),
                   jax.ShapeDtypeStruct((nb, SUBLANES, w2), F32),
                   jax.ShapeDtypeStruct((SUBLANES, ds), F32)],
        scratch_shapes=[pltpu.VMEM((q, w2), F32), pltpu.VMEM((q, w2), F32),
                        pltpu.VMEM((SUBLANES, w2), F32), pltpu.VMEM((q, cb), F32), pltpu.VMEM((q, cb), F32)],
        compiler_params=_params(("arbitrary", "arbitrary")),
    )(p, dy, mb, mc, powt, dskip, bnd)


HALO = 16


def _mix_tile(ys5, p0, p1, p2, p3, prev_cin, cw, bgate, wglu, wco, d):
    dh = d // 2
    tm = ys5.shape[0]
    v = p0[:, dh:].astype(F32)
    gbr = p1[:, :dh].astype(F32)
    gcr = p1[:, dh:].astype(F32)
    gact = _gelu(ys5).astype(BF16)
    z = _dot(gact, wglu)
    z1, z2 = z[:, :d], z[:, d:]
    sg = _sigmoid(z2)
    y_ssm = z1 * sg
    cin = gcr * v
    ext = jnp.concatenate([cin, prev_cin], axis=0)
    r1 = pltpu.roll(ext, 1, 0)[:tm]
    r2 = pltpu.roll(ext, 2, 0)[:tm]
    cv = cw[2] * cin + cw[1] * r1 + cw[0] * r2
    cg = (gbr * cv).astype(BF16)
    y_conv = _dot(cg, wco)
    g_s = _sigmoid(p2.astype(F32) + bgate[:, :d])
    g_c = _sigmoid(p3.astype(F32) + bgate[:, d:])
    mixed = g_s * y_ssm + g_c * y_conv
    return dict(v=v, gb=gbr, gc=gcr, gact=gact, z1=z1, sg=sg, y_ssm=y_ssm, cin=cin, r1=r1, r2=r2,
                cv=cv, cg=cg, y_conv=y_conv, g_s=g_s, g_c=g_c, mixed=mixed)


def _mix_fwd(h1, ys5, p, cw, bgate, wglu, wco, wo, carry=None):
    tp, d = h1.shape
    dh = d // 2
    tm = ROW_ALIGN
    ni = tp // tm

    def body(*refs):
        refs, phases = split(refs)
        (h_ref, y_ref, p0_ref, p1_ref, p2_ref, p3_ref, cw_ref, bg_ref, wglu_ref, wco_ref, wo_ref,
         o_ref, prev) = refs
        _run_phases(phases, carry, pl.program_id(0), ni)

        @pl.when(pl.program_id(0) == 0)
        def _():
            prev[...] = jnp.zeros_like(prev)

        cw = [cw_ref[pl.ds(t, 1), :] for t in range(3)]
        f = _mix_tile(y_ref[...], p0_ref[...], p1_ref[...], p2_ref[...], p3_ref[...], prev[...],
                      cw, bg_ref[...], wglu_ref[...], wco_ref[...], d)
        prev[...] = f["cin"][tm - HALO:, :]
        o_ref[...] = h_ref[...] + _dot(f["mixed"].astype(BF16), wo_ref[...])

    row = pl.BlockSpec((tm, d), lambda i: (i, 0))
    full = lambda a: pl.BlockSpec(a.shape, lambda i: (0,) * a.ndim)
    pk = lambda k: pl.BlockSpec((None, tm, d), lambda i, k=k: (k, i, 0))
    in_specs = [row, pl.BlockSpec((tm, dh), lambda i: (i, 0)), pk(0), pk(1), pk(2), pk(3),
                full(cw), full(bgate), full(wglu), full(wco), full(wo)]
    out_specs, out_shape = [row], [jax.ShapeDtypeStruct((tp, d), F32)]
    args, scratch = [h1, ys5, p, p, p, p, cw, bgate, wglu, wco, wo], [pltpu.VMEM((HALO, dh), F32)]
    split = _attach_carry(carry, in_specs, args, out_specs, out_shape, scratch)
    return pl.pallas_call(
        body, name="mix_fwd", grid=(ni,), in_specs=in_specs, out_specs=out_specs, out_shape=out_shape,
        scratch_shapes=scratch, compiler_params=_params(("arbitrary",)),
    )(*args)


def _mix_bwd(dh2, ys5, p, cw, bgate, wglu, wco, wo):
    tp, d = dh2.shape
    dh = d // 2
    tm = ROW_ALIGN
    ni = tp // tm
    hb = tm // HALO

    def body(dh_ref, y_ref, p0_ref, p1_ref, p2_ref, p3_ref, h0_ref, h1_ref,
             cw_ref, bg_ref, wglu_ref, wco_ref, wo_ref,
             dys_ref, dpb_ref, dwo_ref, dwglu_ref, dwco_ref, dcw_ref, dbg_ref, nxt):
        i = pl.program_id(0)
        tt = ni - 1 - i

        @pl.when(i == 0)
        def _():
            nxt[...] = jnp.zeros_like(nxt)
            dwo_ref[...] = jnp.zeros_like(dwo_ref)
            dwglu_ref[...] = jnp.zeros_like(dwglu_ref)
            dwco_ref[...] = jnp.zeros_like(dwco_ref)
            dcw_ref[...] = jnp.zeros_like(dcw_ref)
            dbg_ref[...] = jnp.zeros_like(dbg_ref)

        cw = [cw_ref[pl.ds(t, 1), :] for t in range(3)]
        prev_cin = h1_ref[:, dh:].astype(F32) * h0_ref[:, dh:].astype(F32)
        prev_cin = jnp.where(tt == 0, 0.0, prev_cin)
        ys5 = y_ref[...]
        f = _mix_tile(ys5, p0_ref[...], p1_ref[...], p2_ref[...], p3_ref[...], prev_cin,
                      cw, bg_ref[...], wglu_ref[...], wco_ref[...], d)
        dhb = dh_ref[...].astype(BF16)
        dmixed = _dot_nt(dhb, wo_ref[...])
        dwo_ref[...] += _dot_tn(f["mixed"].astype(BF16), dhb)

        g_s, g_c, sg = f["g_s"], f["g_c"], f["sg"]
        dy_ssm = dmixed * g_s
        dy_conv = dmixed * g_c
        dp2 = dmixed * f["y_ssm"] * g_s * (1.0 - g_s)
        dp3 = dmixed * f["y_conv"] * g_c * (1.0 - g_c)
        dbg_ref[:, pl.ds(0, d)] += jnp.sum(dp2, axis=0, keepdims=True)
        dbg_ref[:, pl.ds(d, d)] += jnp.sum(dp3, axis=0, keepdims=True)

        dz = jnp.concatenate([dy_ssm * sg, dy_ssm * f["z1"] * sg * (1.0 - sg)], axis=1).astype(BF16)
        dwglu_ref[...] += _dot_tn(f["gact"], dz)
        dys_ref[...] = (_dot_nt(dz, wglu_ref[...]) * _gelu_grad(ys5)).astype(BF16)

        dycb = dy_conv.astype(BF16)
        dwco_ref[...] += _dot_tn(f["cg"], dycb)
        dcg = _dot_nt(dycb, wco_ref[...])
        dgb = dcg * f["cv"]
        dcv = dcg * f["gb"]
        ext = jnp.concatenate([dcv, nxt[...]], axis=0)
        n1 = pltpu.roll(ext, tm + HALO - 1, 0)[:tm]
        n2 = pltpu.roll(ext, tm + HALO - 2, 0)[:tm]
        nxt[...] = dcv[:HALO, :]
        dcin = cw[2] * dcv + cw[1] * n1 + cw[0] * n2
        dcw_ref[pl.ds(0, 1), :] += jnp.sum(dcv * f["r2"], axis=0, keepdims=True)
        dcw_ref[pl.ds(1, 1), :] += jnp.sum(dcv * f["r1"], axis=0, keepdims=True)
        dcw_ref[pl.ds(2, 1), :] += jnp.sum(dcv * f["cin"], axis=0, keepdims=True)
        dgc = dcin * f["v"]
        dv = dcin * f["gc"]
        dpb_ref[0] = jnp.concatenate([jnp.zeros_like(dv), dv], axis=1).astype(BF16)
        dpb_ref[1] = jnp.concatenate([dgb, dgc], axis=1).astype(BF16)
        dpb_ref[2] = dp2.astype(BF16)
        dpb_ref[3] = dp3.astype(BF16)

    rev = lambda i: ni - 1 - i
    row = pl.BlockSpec((tm, d), lambda i: (rev(i), 0))
    half = pl.BlockSpec((tm, dh), lambda i: (rev(i), 0))
    full = lambda a: pl.BlockSpec(a.shape, lambda i: (0,) * a.ndim)
    pk = lambda k: pl.BlockSpec((None, tm, d), lambda i, k=k: (k, rev(i), 0))
    halo = lambda k: pl.BlockSpec((None, HALO, d), lambda i, k=k: (k, jnp.maximum(rev(i) * hb - 1, 0), 0))
    acc = lambda shape: pl.BlockSpec(shape, lambda i: (0,) * len(shape))
    return pl.pallas_call(
        body, name="mix_bwd", grid=(ni,),
        in_specs=[row, half, pk(0), pk(1), pk(2), pk(3), halo(0), halo(1),
                  full(cw), full(bgate), full(wglu), full(wco), full(wo)],
        out_specs=[half, pl.BlockSpec((4, tm, d), lambda i: (0, rev(i), 0)),
                   acc((d, d)), acc((dh, 2 * d)), acc((dh, d)), acc((SUBLANES, dh)), acc((1, 2 * d))],
        out_shape=[jax.ShapeDtypeStruct((tp, dh), BF16), jax.ShapeDtypeStruct((4, tp, d), BF16),
                   jax.ShapeDtypeStruct((d, d), F32), jax.ShapeDtypeStruct((dh, 2 * d), F32),
                   jax.ShapeDtypeStruct((dh, d), F32), jax.ShapeDtypeStruct((SUBLANES, dh), F32),
                   jax.ShapeDtypeStruct((1, 2 * d), F32)],
        scratch_shapes=[pltpu.VMEM((HALO, dh), F32)],
        compiler_params=_params(("arbitrary",)),
    )(dh2, ys5, p, p, p, p, p, p, cw, bgate, wglu, wco, wo)


ANY = pl.BlockSpec(memory_space=pl.ANY)


def _position():
    return lax.axis_index("x"), lax.axis_index("y"), lax.axis_index("c")


def _remote(src, dst, ssem, rsem, dev):
    return pltpu.make_async_remote_copy(src_ref=src, dst_ref=dst, send_sem=ssem, recv_sem=rsem,
                                        device_id=dev, device_id_type=MESH)


def _cast_piece(w, pos):
    rows, cols = w.shape
    r2 = rows // 2

    def body(pos_ref, w_ref, o_ref):
        o_ref[...] = w_ref[...].astype(BF16)

    return pl.pallas_call(
        body, name="cast_piece",
        grid_spec=pltpu.PrefetchScalarGridSpec(
            num_scalar_prefetch=1, grid=(1,),
            in_specs=[pl.BlockSpec((r2, cols), lambda i, pos: (pos[2], 0))],
            out_specs=pl.BlockSpec((r2, cols), lambda i, pos: (0, 0))),
        out_shape=jax.ShapeDtypeStruct((r2, cols), BF16),
        compiler_params=_params(("arbitrary",)),
    )(pos, w)


class _Carry:
    def __init__(self, name, arrays, out_shapes, nsem, nlsem, make, fracs):
        self.name, self.arrays, self.out_shapes = name, list(arrays), list(out_shapes)
        self.nsem, self.nlsem, self.make, self.fracs = nsem, max(nlsem, 1), make, fracs


def _carry_scratch(carry):
    return [pltpu.SemaphoreType.DMA((carry.nsem,)), pltpu.SemaphoreType.DMA((carry.nsem,)),
            pltpu.SemaphoreType.DMA((carry.nlsem,))]


def _run_carry(carry):
    na, no = len(carry.arrays), len(carry.out_shapes)

    def body(*refs):
        for phase in carry.make(refs[:na], refs[na:na + no], *refs[na + no:]):
            phase()

    return pl.pallas_call(
        body, name=carry.name, in_specs=[ANY] * na, out_specs=[ANY] * no, out_shape=carry.out_shapes,
        scratch_shapes=_carry_scratch(carry),
    )(*carry.arrays)


def _attach_carry(carry, in_specs, args, out_specs, out_shape, scratch):
    nhi, nho, nhs = len(in_specs), len(out_specs), len(scratch)
    if carry is None:
        return lambda refs: (list(refs), [])
    na, no = len(carry.arrays), len(carry.out_shapes)
    in_specs += [ANY] * na
    args += carry.arrays
    out_specs += [ANY] * no
    out_shape += carry.out_shapes
    scratch += _carry_scratch(carry)

    def split(refs):
        refs = list(refs)
        o = nhi + na
        host = refs[:nhi] + refs[o:o + nho] + refs[o + nho + no:o + nho + no + nhs]
        sems = refs[o + nho + no + nhs:]
        return host, carry.make(refs[nhi:o], refs[o + nho:o + nho + no], *sems)

    return split


def _run_phases(phases, carry, step, total):
    for phase, frac in zip(phases, carry.fracs if carry is not None else ()):
        pl.when(step == int(round(frac * (total - 1))))(phase)


def _allgather_carry(name, pieces, smalls):
    n, ns = len(pieces), len(smalls)
    per = 14
    n_big = per * n

    def make(ins, outs, ssem, rsem, lsem):
        pin, sin = ins[:n], ins[n:]
        wall, sall = outs[:n], outs[n:]
        x, y, c = _position()
        xnb, ynb, sib = (1 - x, y, c), (x, 1 - y, c), (x, y, 1 - c)
        chips = [(1 - x, y), (x, 1 - y), (1 - x, 1 - y)]
        r4 = [p.shape[0] // 2 for p in pin]
        own = lambda i, h: pin[i].at[pl.ds(h * r4[i], r4[i]), :]
        slot = lambda i, xx, yy, cc, h: wall[i].at[xx, yy, cc, h]
        cp = lambda src, dst, s, dev: _remote(src, dst, ssem.at[s], rsem.at[s], dev)
        to_sib = lambda i, xx, yy, h: cp(slot(i, xx, yy, c, h), slot(i, xx, yy, c, h),
                                         per * i + 6 + 4 * xx + 2 * yy + h, sib)

        def local():
            cps = [pltpu.make_async_copy(own(i, h), slot(i, x, y, c, h), lsem.at[2 * i + h])
                   for i in range(n) for h in range(2)]
            return cps + [pltpu.make_async_copy(sin[i], sall[i].at[2 * x + y], lsem.at[2 * n + i])
                          for i in range(ns)]

        def small(px, py, j, i, landing):
            s = n_big + j * ns + i
            return cp(sin[i], sall[i].at[landing], s, (px, py, c))

        def first_hop():
            for lc in local():
                lc.start()
            for j, (px, py) in enumerate(chips):
                for i in range(ns):
                    small(px, py, j, i, 2 * x + y).start()
            for i in range(n):
                cp(own(i, 0), slot(i, x, y, c, 0), per * i, xnb).start()
                cp(own(i, 1), slot(i, x, y, c, 1), per * i + 1, ynb).start()
                for h in range(2):
                    cp(own(i, h), slot(i, x, y, c, h), per * i + 6 + 4 * x + 2 * y + h, sib).start()

        def second_hop():
            for lc in local():
                lc.wait()
            for i in range(n):
                cp(slot(i, 1 - x, y, c, 0), slot(i, 1 - x, y, c, 0), per * i, xnb).wait_recv()
                cp(slot(i, x, 1 - y, c, 1), slot(i, x, 1 - y, c, 1), per * i + 1, ynb).wait_recv()
                for j in range(2):
                    cp(slot(i, j, y, c, 0), slot(i, j, y, c, 0), per * i + 2 + j, ynb).start()
                    cp(slot(i, x, j, c, 1), slot(i, x, j, c, 1), per * i + 4 + j, xnb).start()
                to_sib(i, 1 - x, y, 0).start()
                to_sib(i, x, 1 - y, 1).start()

        def last_to_sibling():
            for i in range(n):
                for j in range(2):
                    cp(slot(i, j, 1 - y, c, 0), slot(i, j, 1 - y, c, 0), per * i + 2 + j, ynb).wait_recv()
                    cp(slot(i, 1 - x, j, c, 1), slot(i, 1 - x, j, c, 1), per * i + 4 + j, xnb).wait_recv()
                    to_sib(i, j, 1 - y, 0).start()
                    to_sib(i, 1 - x, j, 1).start()

        def finish():
            for i in range(n):
                for xx in range(2):
                    for yy in range(2):
                        for h in range(2):
                            s = per * i + 6 + 4 * xx + 2 * yy + h
                            cp(slot(i, xx, yy, 1 - c, h), slot(i, xx, yy, 1 - c, h), s, sib).wait_recv()
                            to_sib(i, xx, yy, h).wait_send()
                cp(own(i, 0), slot(i, x, y, c, 0), per * i, xnb).wait_send()
                cp(own(i, 1), slot(i, x, y, c, 1), per * i + 1, ynb).wait_send()
                for j in range(2):
                    cp(slot(i, j, y, c, 0), slot(i, j, y, c, 0), per * i + 2 + j, ynb).wait_send()
                    cp(slot(i, x, j, c, 1), slot(i, x, j, c, 1), per * i + 4 + j, xnb).wait_send()
            for j, (px, py) in enumerate(chips):
                for i in range(ns):
                    small(px, py, j, i, 2 * px + py).wait_recv()
                    small(px, py, j, i, 2 * x + y).wait_send()

        return [first_hop, second_hop, last_to_sibling, finish]

    out_shapes = [jax.ShapeDtypeStruct((2, 2, 2, 2, a.shape[0] // 2, a.shape[1]), a.dtype) for a in pieces]
    out_shapes += [jax.ShapeDtypeStruct((4,) + a.shape, a.dtype) for a in smalls]
    return _Carry(name, list(pieces) + list(smalls), out_shapes, n_big + 3 * ns, 2 * n + ns, make,
                  (0.0, 0.23, 0.73, 1.0))


def _exchange_carry(name, arrays, out_shapes, plan):
    count = plan([None] * len(arrays), [None] * len(out_shapes), None)

    def make(ins, outs, ssem, rsem, lsem):
        def copies():
            return [_remote(src, dst, ssem.at[j], rsem.at[j], peer)
                    for j, (src, dst, peer) in enumerate(plan(ins, outs, _position()))]

        def start():
            for c in copies():
                c.start()

        def wait():
            for c in copies():
                c.wait()

        return [start, wait]

    return _Carry(name, arrays, out_shapes, count, 0, make, (0.0, 1.0))


class _Grad:
    def __init__(self, arrs, kind, shard_shape):
        self.arrs, self.kind = list(arrs), kind
        self.rows, self.cols = shard_shape
        self.r2 = self.rows // 2

    def view(self, refs, k, h):
        r2 = self.r2
        if self.kind == "list":
            return refs[k].at[pl.ds(h * r2, r2), :]
        if self.kind == "stacked":
            return refs[0].at[k, pl.ds(h * r2, r2), :]
        if self.kind == "col":
            return refs[0].at[pl.ds(h * r2, r2), pl.ds(k * self.cols, self.cols)]
        return refs[0].at[pl.ds((2 * k + h) * r2, r2), :]

    def add_half(self, recv, pos):
        r2, cols = self.r2, self.cols
        n_in = len(self.arrs)
        tr = _row_tile(r2, cols)
        nt = r2 // tr

        def body(pos_ref, *refs):
            m_refs, (r_ref, of_ref, ob_ref) = refs[:n_in], refs[n_in:]
            mine = m_refs[0][...]
            for kk in range(1, n_in):
                mine = jnp.where(pl.program_id(1) == kk, m_refs[kk][...], mine)
            s = mine.astype(F32) + r_ref[...].astype(F32)
            of_ref[...] = s
            ob_ref[...] = s.astype(BF16)

        row = lambda t, pos: pos[2] * nt + t
        if self.kind == "list":
            specs = [pl.BlockSpec((tr, cols), lambda t, k, pos: (row(t, pos), 0))] * n_in
        elif self.kind == "stacked":
            specs = [pl.BlockSpec((None, tr, cols), lambda t, k, pos: (k, row(t, pos), 0))]
        elif self.kind == "col":
            specs = [pl.BlockSpec((tr, cols), lambda t, k, pos: (row(t, pos), k))]
        else:
            specs = [pl.BlockSpec((tr, cols), lambda t, k, pos: (2 * k * nt + row(t, pos), 0))]
        blk = pl.BlockSpec((None, tr, cols), lambda t, k, pos: (k, t, 0))
        return pl.pallas_call(
            body, name="rs_add_c",
            grid_spec=pltpu.PrefetchScalarGridSpec(
                num_scalar_prefetch=1, grid=(nt, 4), in_specs=specs + [blk], out_specs=[blk, blk]),
            out_shape=[jax.ShapeDtypeStruct((4, r2, cols), F32), jax.ShapeDtypeStruct((4, r2, cols), BF16)],
            compiler_params=_params(("arbitrary", "arbitrary")),
        )(pos, *self.arrs, recv)


def _row_tile(rows, cols):
    fits = [t for t in range(16, rows + 1, 16) if rows % t == 0 and t * cols * 4 <= 2 * 1024 * 1024]
    return max(fits) if fits else rows


def _adamw_math(w, g, m, v):
    m = ADAM_B1 * m + (1.0 - ADAM_B1) * g
    v = ADAM_B2 * v + (1.0 - ADAM_B2) * (g * g)
    m_hat = m / (1.0 - ADAM_B1 ** ADAM_STEP)
    v_hat = v / (1.0 - ADAM_B2 ** ADAM_STEP)
    delta = -ADAM_LR * (m_hat / (jnp.sqrt(v_hat) + ADAM_EPS) + ADAM_WD * w)
    return delta, m, v


def _adamw_big(w, m, v, own, sib, pos):
    rows, cols = w.shape
    r2 = rows // 2

    tr = _row_tile(r2, cols)
    nt = r2 // tr

    def body(pos_ref, w_ref, m_ref, v_ref, own_ref, sib_ref, g_ref, d_ref, nm_ref, nv_ref):
        h = pl.program_id(0)
        g = jnp.where(h == pos_ref[2], own_ref[...], sib_ref[...])
        g_ref[...] = g
        d_ref[...], nm_ref[...], nv_ref[...] = _adamw_math(w_ref[...], g, m_ref[...], v_ref[...])

    half = pl.BlockSpec((tr, cols), lambda h, t, pos: (h * nt + t, 0))
    piece = pl.BlockSpec((tr, cols), lambda h, t, pos: (t, 0))
    out = jax.ShapeDtypeStruct((rows, cols), F32)
    return pl.pallas_call(
        body, name="adamw",
        grid_spec=pltpu.PrefetchScalarGridSpec(
            num_scalar_prefetch=1, grid=(2, nt),
            in_specs=[half, half, half, piece, piece],
            out_specs=[half, half, half, half]),
        out_shape=[out, out, out, out],
        compiler_params=_params(("arbitrary", "arbitrary")),
    )(pos, w, m, v, own, sib)


def _add_hop1(s1f, recv, pos):
    _, _, r4, cols = recv.shape
    s1v = s1f.reshape(4, 2, r4, cols)

    def body(pos_ref, m_ref, r_ref, of_ref, ob_ref):
        s = m_ref[...] + r_ref[...].astype(F32)
        of_ref[...] = s
        ob_ref[...] = s.astype(BF16)

    def mine(h, j, pos):
        return (jnp.where(h == 0, 2 * j + pos[1], 2 * pos[0] + j), h, 0, 0)

    blk = pl.BlockSpec((None, None, r4, cols), lambda h, j, pos: (h, j, 0, 0))
    return pl.pallas_call(
        body, name="rs_add_1",
        grid_spec=pltpu.PrefetchScalarGridSpec(
            num_scalar_prefetch=1, grid=(2, 2),
            in_specs=[pl.BlockSpec((None, None, r4, cols), mine), blk], out_specs=[blk, blk]),
        out_shape=[jax.ShapeDtypeStruct((2, 2, r4, cols), F32), jax.ShapeDtypeStruct((2, 2, r4, cols), BF16)],
        compiler_params=_params(("arbitrary", "arbitrary")),
    )(pos, s1v, recv)


def _own_sum(s2f, recv3, pos):
    _, _, r4, cols = s2f.shape

    def body(pos_ref, s_ref, r_ref, o_ref):
        o_ref[...] = s_ref[...] + r_ref[...].astype(F32)

    blk = pl.BlockSpec((None, r4, cols), lambda h, pos: (h, 0, 0))
    return pl.pallas_call(
        body, name="own_sum",
        grid_spec=pltpu.PrefetchScalarGridSpec(
            num_scalar_prefetch=1, grid=(2,),
            in_specs=[pl.BlockSpec((None, None, r4, cols),
                                   lambda h, pos: (h, jnp.where(h == 0, pos[0], pos[1]), 0, 0)), blk],
            out_specs=blk),
        out_shape=jax.ShapeDtypeStruct((2, r4, cols), F32),
        compiler_params=_params(("arbitrary",)),
    )(pos, s2f, recv3)


def _allreduce_small(buf):
    def body(x_ref, o_ref, recv, ssem, rsem):
        x, y, c = _position()
        o_ref[...] = x_ref[...]
        for s, peer in enumerate([(x, y, 1 - c), (x, 1 - y, c), (1 - x, y, c)]):
            cp = _remote(o_ref, recv.at[s], ssem.at[s], rsem.at[s], peer)
            cp.start()
            cp.wait()
            o_ref[...] = o_ref[...] + recv[s]

    vm = pl.BlockSpec(memory_space=pltpu.VMEM)
    return pl.pallas_call(
        body, name="allreduce_small", in_specs=[vm], out_specs=vm,
        out_shape=jax.ShapeDtypeStruct(buf.shape, F32),
        scratch_shapes=[pltpu.VMEM((3,) + buf.shape, F32),
                        pltpu.SemaphoreType.DMA((3,)), pltpu.SemaphoreType.DMA((3,))],
    )(buf)


def _adamw_small(w, g, m, v):
    def body(w_ref, g_ref, m_ref, v_ref, d_ref, nm_ref, nv_ref):
        d_ref[...], nm_ref[...], nv_ref[...] = _adamw_math(w_ref[...], g_ref[...], m_ref[...], v_ref[...])

    vm = pl.BlockSpec(memory_space=pltpu.VMEM)
    out = jax.ShapeDtypeStruct(w.shape, F32)
    return pl.pallas_call(body, name="adamw_small", in_specs=[vm] * 4, out_specs=[vm] * 3,
                          out_shape=[out, out, out])(w, g, m, v)


class _ReduceScatter:
    def __init__(self, tag, grads, pos):
        self.tag, self.grads, self.pos, self.stage = tag, grads, pos, 0

    def carry(self):
        grads, n = self.grads, len(self.grads)
        r4 = [g.r2 // 2 for g in grads]

        first = [sum(len(g.arrs) for g in grads[:i]) for i in range(n)]

        def plan_c(ins, outs, p):
            if p is None:
                return 4 * n
            x, y, c = p
            mine = lambda i: ins[first[i]:first[i] + len(grads[i].arrs)]
            return [(grads[i].view(mine(i), k, 1 - c), outs[i].at[k], (x, y, 1 - c))
                    for i in range(n) for k in range(4)]

        def plan_1(ins, outs, p):
            if p is None:
                return 4 * n
            x, y, c = p
            copies = []
            for i in range(n):
                for j in range(2):
                    copies.append((ins[i].at[2 * j + (1 - y), pl.ds(0, r4[i]), :], outs[i].at[0, j],
                                   (x, 1 - y, c)))
                    copies.append((ins[i].at[2 * (1 - x) + j, pl.ds(r4[i], r4[i]), :], outs[i].at[1, j],
                                   (1 - x, y, c)))
            return copies

        def plan_2(ins, outs, p):
            if p is None:
                return 2 * n
            x, y, c = p
            copies = []
            for i in range(n):
                copies.append((ins[i].at[0, 1 - x], outs[i].at[0], (1 - x, y, c)))
                copies.append((ins[i].at[1, 1 - y], outs[i].at[1], (x, 1 - y, c)))
            return copies

        def plan_s(ins, outs, p):
            if p is None:
                return n
            x, y, c = p
            return [(ins[i], outs[i], (x, y, 1 - c)) for i in range(n)]

        shape = lambda lead, dt: [jax.ShapeDtypeStruct(lead(g) + (g.cols,), dt) for g in grads]
        stage = self.stage
        if stage == 0:
            return _exchange_carry(f"rs_{self.tag}_exchange_c", [a for g in grads for a in g.arrs],
                                   [jax.ShapeDtypeStruct((4, g.r2, g.cols), g.arrs[0].dtype) for g in grads],
                                   plan_c)
        if stage == 1:
            return _exchange_carry(f"rs_{self.tag}_exchange_1", [s[1] for s in self.s1],
                                   shape(lambda g: (2, 2, g.r2 // 2), BF16), plan_1)
        if stage == 2:
            return _exchange_carry(f"rs_{self.tag}_exchange_2", [s[1] for s in self.s2],
                                   shape(lambda g: (2, g.r2 // 2), BF16), plan_2)
        return _exchange_carry(f"rs_{self.tag}_exchange_sibling", self.own, shape(lambda g: (g.r2,), F32), plan_s)

    def feed(self, recv):
        grads, pos = self.grads, self.pos
        if self.stage == 0:
            self.s1 = [g.add_half(r, pos) for g, r in zip(grads, recv)]
        elif self.stage == 1:
            self.s2 = [_add_hop1(s[0], r, pos) for s, r in zip(self.s1, recv)]
        elif self.stage == 2:
            self.own = [_own_sum(s[0], r, pos).reshape(g.r2, g.cols) for g, s, r in zip(grads, self.s2, recv)]
        else:
            self.sib = list(recv)
        self.stage += 1

    def run(self):
        while self.stage < 4:
            self.feed(_run_carry(self.carry()))

    def adamw(self, weights):
        return [_adamw_big(w, m, v, o, sb, self.pos) for (w, m, v), o, sb in zip(weights, self.own, self.sib)]


def _block_diag(t, nb):
    g, c, p = t.shape
    gb = g // nb
    t = t.reshape(nb, gb, c, p)
    eye = jnp.eye(gb, dtype=t.dtype)
    return jnp.einsum("bgcp,gh->bgchp", t, eye).reshape(nb, gb * c, gb * p)


def _s5_discretise(a_re, a_im, log_dt, b_re, b_im, c_re, c_im):
    g, p = a_re.shape
    nb = g // GROUPS_PER_BLOCK
    dt = jnp.exp(log_dt)[:, None]
    mag = jnp.exp(a_re * dt)
    lam_re = mag * jnp.cos(a_im * dt)
    lam_im = mag * jnp.sin(a_im * dt)
    den = a_re * a_re + a_im * a_im
    q_re = ((lam_re - 1.0) * a_re + lam_im * a_im) / den
    q_im = (lam_im * a_re - (lam_re - 1.0) * a_im) / den
    bb_re = q_re[..., None] * b_re - q_im[..., None] * b_im
    bb_im = q_re[..., None] * b_im + q_im[..., None] * b_re
    tr = lambda t: jnp.swapaxes(t, 1, 2)
    mb = jnp.concatenate([_block_diag(tr(bb_re), nb), _block_diag(tr(bb_im), nb)], axis=-1)
    mc = jnp.concatenate([_block_diag(c_re, nb), -_block_diag(c_im, nb)], axis=-1)
    lam = jnp.concatenate([lam_re.reshape(nb, -1), lam_im.reshape(nb, -1)], axis=-1)
    return mb, mc, lam


def _s5_powers(a_re, a_im, log_dt, sub):
    g, p = a_re.shape
    nb = g // GROUPS_PER_BLOCK
    dt = jnp.exp(log_dt)[:, None]
    ns = list(range(1, sub + 1)) + [2 * sub, 4 * sub]
    ns += [0] * (-len(ns) % SUBLANES)
    e = jnp.asarray(ns, F32)[:, None, None]
    mag = jnp.exp(a_re[None] * dt[None] * e)
    ang = a_im[None] * dt[None] * e
    re = (mag * jnp.cos(ang)).reshape(len(ns), nb, -1)
    im = (mag * jnp.sin(ang)).reshape(len(ns), nb, -1)
    return jnp.transpose(jnp.concatenate([re, im], axis=-1), (1, 0, 2))


def _pack(parts):
    flat = jnp.concatenate([a.reshape(-1).astype(F32) for a in parts])
    n = flat.shape[0]
    pad = -n % (SUBLANES * LANES)
    return jnp.pad(flat, (0, pad)).reshape(-1, LANES)


def _unpack(buf, like):
    flat = buf.reshape(-1)
    out, o = [], 0
    for a in like:
        out.append(flat[o:o + a.size].reshape(a.shape))
        o += a.size
    return out


def kernel(x, meta_tokens, g_ffn1, ffn1_w_gate, ffn1_w_up, ffn1_w_down, g_mix, w_in, b_gate, ssm_a_re, ssm_a_im, ssm_log_dt, ssm_b_re, ssm_b_im, ssm_c_re, ssm_c_im, ssm_d, ssm_w_glu, conv_w, conv_w_out, w_o, g_ffn2, ffn2_w_gate, ffn2_w_up, ffn2_w_down, g_final, loss_target, m_meta_tokens, m_g_ffn1, m_ffn1_w_gate, m_ffn1_w_up, m_ffn1_w_down, m_g_mix, m_w_in, m_b_gate, m_ssm_a_re, m_ssm_a_im, m_ssm_log_dt, m_ssm_b_re, m_ssm_b_im, m_ssm_c_re, m_ssm_c_im, m_ssm_d, m_ssm_w_glu, m_conv_w, m_conv_w_out, m_w_o, m_g_ffn2, m_ffn2_w_gate, m_ffn2_w_up, m_ffn2_w_down, m_g_final, v_meta_tokens, v_g_ffn1, v_ffn1_w_gate, v_ffn1_w_up, v_ffn1_w_down, v_g_mix, v_w_in, v_b_gate, v_ssm_a_re, v_ssm_a_im, v_ssm_log_dt, v_ssm_b_re, v_ssm_b_im, v_ssm_c_re, v_ssm_c_im, v_ssm_d, v_ssm_w_glu, v_conv_w, v_conv_w_out, v_w_o, v_g_ffn2, v_ffn2_w_gate, v_ffn2_w_up, v_ffn2_w_down, v_g_final):
    seq, d = x.shape[1], x.shape[2]
    n_meta = meta_tokens.shape[0]
    dh = d // 2
    tp = -(-(n_meta + seq) // ROW_ALIGN) * ROW_ALIGN
    mx, my, mc_ = _position()
    pos = jnp.stack([mx, my, mc_]).astype(jnp.int32)
    shard = 2 * mx + my

    big_names = ["ffn1_w_gate", "ffn1_w_up", "ffn1_w_down", "w_in", "ssm_w_glu", "conv_w_out", "w_o",
                 "ffn2_w_gate", "ffn2_w_up", "ffn2_w_down"]
    transposed = {0, 1, 7, 8}
    drop = lambda arrs: [jnp.swapaxes(a.reshape(a.shape[1:]), 0, 1) if i in transposed else a.reshape(a.shape[1:])
                         for i, a in enumerate(arrs)]
    big_w = drop([ffn1_w_gate, ffn1_w_up, ffn1_w_down, w_in, ssm_w_glu, conv_w_out, w_o,
                  ffn2_w_gate, ffn2_w_up, ffn2_w_down])
    big_m = drop([m_ffn1_w_gate, m_ffn1_w_up, m_ffn1_w_down, m_w_in, m_ssm_w_glu, m_conv_w_out,
                  m_w_o, m_ffn2_w_gate, m_ffn2_w_up, m_ffn2_w_down])
    big_v = drop([v_ffn1_w_gate, v_ffn1_w_up, v_ffn1_w_down, v_w_in, v_ssm_w_glu, v_conv_w_out,
                  v_w_o, v_ffn2_w_gate, v_ffn2_w_up, v_ffn2_w_down])
    pieces = [_cast_piece(w, pos) for w in big_w]
    conv_local = conv_w.reshape(conv_w.shape[1], conv_w.shape[3])
    n_first = 3
    first = _run_carry(_allgather_carry("allgather_first", pieces[:n_first], [meta_tokens, conv_local]))
    smalls = first[n_first:]
    stack4 = lambda wl: wl.reshape((4, -1, wl.shape[-1]))
    w1g, w1u, w1d = [stack4(wl) for wl in first[:n_first]]
    natural_cols = lambda s: jnp.transpose(s, (1, 0, 2)).reshape(s.shape[1], 4 * s.shape[2])
    meta_full = natural_cols(smalls[0])
    cw_full = natural_cols(smalls[1])
    cw_pad = jnp.pad(cw_full, ((0, SUBLANES - cw_full.shape[0]), (0, 0)))

    s5_args = (ssm_a_re[0], ssm_a_im[0], ssm_log_dt[0], ssm_b_re[0], ssm_b_im[0], ssm_c_re[0], ssm_c_im[0])
    (mb, mc, _), disc_vjp = jax.vjp(_s5_discretise, *s5_args)
    powt = _s5_powers(ssm_a_re[0], ssm_a_im[0], ssm_log_dt[0], SCAN_TILE // SUBLANES)
    mb16, mc16 = mb.astype(BF16), mc.astype(BF16)

    pad_rows = tp - n_meta - seq
    h0 = jnp.concatenate([meta_full, x.reshape(seq, d), jnp.zeros((pad_rows, d), F32)], axis=0)
    tgt = jnp.concatenate([jnp.zeros((n_meta, d), F32), loss_target.reshape(seq, d),
                           jnp.zeros((pad_rows, d), F32)], axis=0)
    h1, a1, b1, n1, *mid = _ffn_fwd(h0, g_ffn1, w1g, w1u, w1d, "ffn1_fwd",
                                    carry=_allgather_carry("allgather_mixer", pieces[3:7], []))
    win_all, wglu_s, wco_s, wo_s = [stack4(wl) for wl in mid]
    wglu_all = natural_cols(wglu_s)
    wco_all = natural_cols(wco_s)
    wo_all = wo_s.reshape(d, d)
    u, p, w2g, w2u = _win_fwd(h1, g_mix, win_all, carry=_allgather_carry("allgather_ffn2_in", pieces[7:9], []))
    ys5, bnd = _scan_fwd(p, mb16, mc16, powt, ssm_d)
    h2, w2d = _mix_fwd(h1, ys5, p, cw_pad, b_gate, wglu_all, wco_all, wo_all,
                       carry=_allgather_carry("allgather_ffn2_out", pieces[9:], []))
    w2g, w2u, w2d = stack4(w2g), stack4(w2u), stack4(w2d)
    dh3, a2, b2, n2, dg_final, loss_part, dy3 = _ffn_fwd(
        h2, g_ffn2, w2g, w2u, w2d, "ffn2_fwd_loss", final=(g_final.reshape(1, d), tgt, n_meta, seq))

    dh2, dw2g, dw2u, dw2d, dg_ffn2 = _ffn_bwd(dh3, dy3, h2, n2, g_ffn2, a2, b2, w2g, w2u, w2d, "ffn2_bwd")
    dys5, dpb, dwo, dwglu, dwco, dcw, dbg = _mix_bwd(dh2, ys5, p, cw_pad, b_gate, wglu_all, wco_all, wo_all)
    dug, dmb, dmc, dlam, dd = _scan_bwd(p, dys5, mb16, mc16, powt, ssm_d, bnd)
    dh1, dwin, dg_mix, dy1 = _win_bwd(dpb, dug, u, win_all, h1, g_mix, dh2)
    shapes = [w.shape for w in big_w]
    kinds = ["list", "list", "list", "list", "col", "col", "row", "list", "list", "list"]
    rest_grads = [dwin, [dwglu], [dwco], [dwo], dw2g, dw2u, dw2d]
    rs_rest = _ReduceScatter("rest", [_Grad(a, k, s) for a, k, s in
                                      zip(rest_grads, kinds[n_first:], shapes[n_first:])], pos)
    dh0, dw1g, dw1u, dw1d, dg_ffn1 = _ffn_bwd(dh1, dy1, h0, n1, g_ffn1, a1, b1, w1g, w1u, w1d, "ffn1_bwd",
                                              chain=rs_rest)
    rs_first = _ReduceScatter("first", [_Grad(a, k, s) for a, k, s in
                                        zip([dw1g, dw1u, dw1d], kinds[:n_first], shapes[:n_first])], pos)
    rs_first.run()
    wmv = list(zip(big_w, big_m, big_v))
    big_out = rs_first.adamw(wmv[:n_first]) + rs_rest.adamw(wmv[n_first:])
    def lead(i, o):
        o = jnp.swapaxes(o, 0, 1) if i in transposed else o
        return o.reshape((1,) + o.shape)

    big_out = {nme: tuple(lead(i, o) for o in outs) for i, (nme, outs) in enumerate(zip(big_names, big_out))}

    s5_grads = disc_vjp((dmb, dmc, jnp.sum(dlam, axis=1)))
    grad_x = dh0[n_meta:n_meta + seq][None]

    small_names = ["g_ffn1", "g_mix", "b_gate", "ssm_a_re", "ssm_a_im", "ssm_log_dt", "ssm_b_re", "ssm_b_im",
                   "ssm_c_re", "ssm_c_im", "ssm_d", "g_ffn2", "g_final", "meta_tokens", "conv_w"]
    small_w = [g_ffn1, g_mix, b_gate, ssm_a_re, ssm_a_im, ssm_log_dt, ssm_b_re, ssm_b_im, ssm_c_re, ssm_c_im,
               ssm_d, g_ffn2, g_final, meta_tokens, conv_w]
    small_m = [m_g_ffn1, m_g_mix, m_b_gate, m_ssm_a_re, m_ssm_a_im, m_ssm_log_dt, m_ssm_b_re, m_ssm_b_im,
               m_ssm_c_re, m_ssm_c_im, m_ssm_d, m_g_ffn2, m_g_final, m_meta_tokens, m_conv_w]
    small_v = [v_g_ffn1, v_g_mix, v_b_gate, v_ssm_a_re, v_ssm_a_im, v_ssm_log_dt, v_ssm_b_re, v_ssm_b_im,
               v_ssm_c_re, v_ssm_c_im, v_ssm_d, v_g_ffn2, v_g_final, v_meta_tokens, v_conv_w]
    local_small = [dg_ffn1, dg_mix, dbg, *s5_grads, jnp.sum(dd, axis=0), dg_ffn2, dg_final,
                   dh0[:n_meta], dcw[:conv_w.shape[1]]]
    reduced = _unpack(_allreduce_small(_pack(local_small)), local_small)
    reduced[-2] = lax.dynamic_slice_in_dim(reduced[-2], shard * meta_tokens.shape[1], meta_tokens.shape[1], 1)
    reduced[-1] = lax.dynamic_slice_in_dim(reduced[-1], shard * conv_w.shape[3], conv_w.shape[3], 1)
    small_g = [r.reshape(w.shape) for r, w in zip(reduced, small_w)]
    ds_, nm_, nv_ = _adamw_small(_pack(small_w), _pack(small_g), _pack(small_m), _pack(small_v))
    small_out = {nme: o for nme, o in zip(
        small_names, zip(small_g, _unpack(ds_, small_w), _unpack(nm_, small_w), _unpack(nv_, small_w)))}

    loss = lax.psum(loss_part[0, 0], ("x", "y", "c"))
    order = ["meta_tokens", "g_ffn1", "ffn1_w_gate", "ffn1_w_up", "ffn1_w_down", "g_mix", "w_in", "b_gate",
             "ssm_a_re", "ssm_a_im", "ssm_log_dt", "ssm_b_re", "ssm_b_im", "ssm_c_re", "ssm_c_im", "ssm_d",
             "ssm_w_glu", "conv_w", "conv_w_out", "w_o", "g_ffn2", "ffn2_w_gate", "ffn2_w_up", "ffn2_w_down",
             "g_final"]
    res = {**big_out, **small_out}
    return (loss, grad_x, *[res[nme][0] for nme in order], *[res[nme][1] for nme in order],
            *[res[nme][2] for nme in order], *[res[nme][3] for nme in order])
```

```python
import functools
import math

import jax
import jax.numpy as jnp
from jax import lax
from jax.experimental import pallas as pl
from jax.experimental.pallas import tpu as pltpu

F32 = jnp.float32
BF16 = jnp.bfloat16
MESH = pl.DeviceIdType.MESH

RMS_EPS = 1e-6
ADAM_LR = 0.001
ADAM_B1 = 0.9
ADAM_B2 = 0.999
ADAM_EPS = 1e-08
ADAM_WD = 0.01
ADAM_STEP = 10

LANES = 128
SUBLANES = 8
VMEM_LIMIT = 56 * 1024 * 1024

ROW_ALIGN = 256
SCAN_TILE = 256
SCAN_SEQS = 16
GROUPS_PER_BLOCK = 8


def _params(sem, vmem=VMEM_LIMIT):
    return pltpu.CompilerParams(dimension_semantics=sem, vmem_limit_bytes=vmem)


def _pick_tile(n, candidates):
    for c in candidates:
        if n % c == 0:
            return c
    raise ValueError(f"no tile for {n}")


def _dot(a, b):
    return jnp.dot(a, b, preferred_element_type=F32)


def _dot_nt(a, b):
    return lax.dot_general(a, b, (((1,), (1,)), ((), ())), preferred_element_type=F32)


def _dot_tn(a, b):
    return lax.dot_general(a, b, (((0,), (0,)), ((), ())), preferred_element_type=F32)


def _sigmoid(x):
    return 1.0 / (1.0 + jnp.exp(-x))


def _rms_stats(h):
    r = lax.rsqrt(jnp.mean(h * h, axis=-1, keepdims=True) + RMS_EPS)
    return h * r, r


def _rms_bwd(xhat, r, g, dn):
    dxh = dn * g
    return r * (dxh - xhat * jnp.mean(dxh * xhat, axis=-1, keepdims=True))


GELU_K = math.sqrt(2.0 / math.pi)
GELU_C = 0.044715


def _gelu(x):
    return 0.5 * x * (1.0 + jnp.tanh(GELU_K * (x + GELU_C * x * x * x)))


def _gelu_grad(x):
    t = jnp.tanh(GELU_K * (x + GELU_C * x * x * x))
    return 0.5 * (1.0 + t) + 0.5 * x * (1.0 - t * t) * GELU_K * (1.0 + 3.0 * GELU_C * x * x)


def _for_tile_rows(i, ni, tm, n_meta, seq, fn):
    pl.when(i == 0)(lambda: fn(0, min(tm - n_meta, seq), n_meta))
    if ni > 1:
        last_lo = (ni - 1) * tm - n_meta
        pl.when(i == ni - 1)(lambda: fn(last_lo, min(seq - last_lo, tm), 0))
    if ni > 2:
        pl.when((i > 0) & (i < ni - 1))(lambda: fn(pl.multiple_of(i * tm - n_meta, SUBLANES), tm, 0))


def _ffn_fwd(h, g, wg, wu, wd, name, final=None, carry=None):
    tp, d = h.shape
    ns, f4, _ = wg.shape
    tm = _pick_tile(tp, (768, 512, 256))
    ni = tp // tm

    def body(*refs):
        refs, phases = split(refs)
        if final is None:
            h_ref, g_ref, wg_ref, wu_ref, wd_ref, ho_ref, a_ref, b_ref, n_scr, acc = refs
        else:
            (h_ref, g_ref, wg_ref, wu_ref, wd_ref, gf_ref, tg_hbm,
             ho_ref, a_ref, b_ref, n_scr, dgf_ref, loss_ref, dy_ref, acc, tg_ref, tg_sem) = refs
        i = pl.program_id(0)
        k = pl.program_id(1)
        _run_phases(phases, carry, i * ns + k, ni * ns)

        if final is not None:
            def target_rows(lo, n, at):
                return pltpu.make_async_copy(tg_hbm.at[pl.ds(lo, n), :], tg_ref.at[pl.ds(at, n), :], tg_sem)

            def fetch_target(lo, n, at):
                if at > 0:
                    tg_ref[pl.ds(0, at), :] = jnp.zeros((at, d), F32)
                if at + n < tm:
                    tg_ref[pl.ds(at + n, tm - at - n), :] = jnp.zeros((tm - at - n, d), F32)
                target_rows(lo, n, at).start()

            pl.when(k == 0)(lambda: _for_tile_rows(i, ni, tm, final[2], final[3], fetch_target))

        @pl.when(k == 0)
        def _():
            xhat, _ = _rms_stats(h_ref[...])
            n_scr[...] = (xhat * g_ref[...]).astype(BF16)
            acc[...] = jnp.zeros_like(acc)

        n = n_scr[...]
        a = _dot_nt(n, wg_ref[...])
        b = _dot_nt(n, wu_ref[...])
        a_ref[...] = a.astype(BF16)
        b_ref[...] = b.astype(BF16)
        s = (a * _sigmoid(a) * b).astype(BF16)
        acc[...] += _dot(s, wd_ref[...])

        if final is None:
            @pl.when(k == ns - 1)
            def _():
                ho_ref[...] = h_ref[...] + 0.5 * acc[...]
        else:
            n_meta, seq = final[2], final[3]

            @pl.when((i == 0) & (k == 0))
            def _():
                dgf_ref[...] = jnp.zeros_like(dgf_ref)
                loss_ref[...] = jnp.zeros_like(loss_ref)

            @pl.when(k == ns - 1)
            def _():
                _for_tile_rows(i, ni, tm, n_meta, seq, lambda lo, n, at: target_rows(lo, n, at).wait())
                h3 = h_ref[...] + 0.5 * acc[...]
                xhat, r = _rms_stats(h3)
                gf = gf_ref[...]
                row = i * tm + lax.broadcasted_iota(jnp.int32, (tm, d), 0)
                valid = (row >= n_meta) & (row < n_meta + seq)
                diff = jnp.where(valid, xhat * gf - tg_ref[...], 0.0)
                dout = diff * (1.0 / d)
                loss_ref[...] += jnp.full(loss_ref.shape, 0.5 * jnp.sum(diff * diff) * (1.0 / d), F32)
                dgf_ref[...] += jnp.sum(dout * xhat, axis=0, keepdims=True)
                dh3 = _rms_bwd(xhat, r, gf, dout)
                ho_ref[...] = dh3
                dy_ref[...] = (0.5 * dh3).astype(BF16)

    row_spec = pl.BlockSpec((tm, d), lambda i, k: (i, 0))
    vec_spec = pl.BlockSpec((1, d), lambda i, k: (0, 0))
    in_specs = [row_spec, vec_spec,
                pl.BlockSpec((None, f4, d), lambda i, k: (k, 0, 0)),
                pl.BlockSpec((None, f4, d), lambda i, k: (k, 0, 0)),
                pl.BlockSpec((None, f4, d), lambda i, k: (k, 0, 0))]
    act_spec = pl.BlockSpec((None, tm, f4), lambda i, k: (k, i, 0))
    out_specs = [row_spec, act_spec, act_spec, row_spec]
    out_shape = [jax.ShapeDtypeStruct((tp, d), F32),
                 jax.ShapeDtypeStruct((ns, tp, f4), BF16),
                 jax.ShapeDtypeStruct((ns, tp, f4), BF16),
                 jax.ShapeDtypeStruct((tp, d), BF16)]
    args = [h, g, wg, wu, wd]
    scratch = [pltpu.VMEM((tm, d), F32)]
    if final is not None:
        in_specs += [vec_spec, ANY]
        args += [final[0], final[1]]
        out_specs += [vec_spec, pl.BlockSpec((1, LANES), lambda i, k: (0, 0)), row_spec]
        out_shape += [jax.ShapeDtypeStruct((1, d), F32), jax.ShapeDtypeStruct((1, LANES), F32),
                      jax.ShapeDtypeStruct((tp, d), BF16)]
        scratch += [pltpu.VMEM((tm, d), F32), pltpu.SemaphoreType.DMA(())]
    split = _attach_carry(carry, in_specs, args, out_specs, out_shape, scratch)
    return pl.pallas_call(
        body, name=name, grid=(ni, ns), in_specs=in_specs, out_specs=out_specs, out_shape=out_shape,
        scratch_shapes=scratch, compiler_params=_params(("arbitrary", "arbitrary")),
    )(*args)


def _ffn_bwd_shard(k, ns, dn_prev, dy, n, a, b, wg, wu, wd, tail, name, carry=None, unpad=None):
    tp, d = n.shape
    f4 = wg.shape[1]
    tm = _pick_tile(tp, (768, 512, 256))
    ni = tp // tm
    first, last = k == 0, k == ns - 1
    unpad = unpad if last else None

    def body(*refs):
        refs, phases = split(refs)
        acc_in = None if first else refs.pop(0)
        if last:
            dh_ref, h_ref, g_ref = refs[:3]
        else:
            dy_ref, n_ref = refs[:2]
        refs = refs[3 if last else 2:]
        a_ref, b_ref, wg_hbm, wu_hbm, wd_hbm = refs[:5]
        refs = refs[5:]
        acc_out, dwg_hbm, dwu_hbm, dwd_hbm = refs[:4]
        rest = refs[4:]
        dg_ref = rest.pop(0) if last else None
        head_ref = rest.pop(0) if unpad else None
        wg_ref, wu_ref, wd_ref, dwg_ref, dwu_ref, dwd_ref = rest[:6]
        i = pl.program_id(0)
        _run_phases(phases, carry, i, ni)
        if unpad:
            res_ref, res_sem = rest[6:]

            def real_rows(lo, cnt, at):
                return pltpu.make_async_copy(res_ref.at[pl.ds(at, cnt), :], acc_out.at[pl.ds(lo, cnt), :], res_sem)

            def wait_tile(tile):
                _for_tile_rows(tile, ni, tm, *unpad, lambda lo, cnt, at: real_rows(lo, cnt, at).wait())

        @pl.when(i == 0)
        def _():
            pltpu.sync_copy(wg_hbm.at[k], wg_ref)
            pltpu.sync_copy(wu_hbm.at[k], wu_ref)
            pltpu.sync_copy(wd_hbm.at[k], wd_ref)
            dwg_ref[...] = jnp.zeros_like(dwg_ref)
            dwu_ref[...] = jnp.zeros_like(dwu_ref)
            dwd_ref[...] = jnp.zeros_like(dwd_ref)
            if last:
                dg_ref[...] = jnp.zeros_like(dg_ref)

        if last:
            xhat, r = _rms_stats(h_ref[...])
            n = (xhat * g_ref[...]).astype(BF16)
            dy = (0.5 * dh_ref[...]).astype(BF16)
        else:
            n = n_ref[...]
            dy = dy_ref[...]
        av = a_ref[...].astype(F32)
        bv = b_ref[...].astype(F32)
        sg = _sigmoid(av)
        silu = av * sg
        ds = _dot_nt(dy, wd_ref[...])
        da = (ds * bv * (sg * (1.0 + av * (1.0 - sg)))).astype(BF16)
        db = (ds * silu).astype(BF16)
        s = (silu * bv).astype(BF16)
        dwd_ref[...] += _dot_tn(s, dy)
        dwg_ref[...] += _dot_tn(da, n)
        dwu_ref[...] += _dot_tn(db, n)
        dn = _dot(da, wg_ref[...]) + _dot(db, wu_ref[...])
        if not first:
            dn = dn + acc_in[...]
        if last:
            dg_ref[...] += jnp.sum(dn * xhat, axis=0, keepdims=True)
            dh_in = dh_ref[...] + _rms_bwd(xhat, r, g_ref[...], dn)
            if unpad:
                pl.when(i > 0)(lambda: wait_tile(i - 1))
                res_ref[...] = dh_in

                @pl.when(i == 0)
                def _():
                    head_ref[...] = res_ref[pl.ds(0, unpad[0]), :]

                _for_tile_rows(i, ni, tm, *unpad, lambda lo, cnt, at: real_rows(lo, cnt, at).start())
                pl.when(i == ni - 1)(lambda: wait_tile(i))
            else:
                acc_out[...] = dh_in
        else:
            acc_out[...] = dn

        @pl.when(i == ni - 1)
        def _():
            for acc_ref, stage_ref, out_hbm in ((dwg_ref, wg_ref, dwg_hbm), (dwu_ref, wu_ref, dwu_hbm),
                                                (dwd_ref, wd_ref, dwd_hbm)):
                stage_ref[...] = acc_ref[...].astype(BF16)
                pltpu.sync_copy(stage_ref, out_hbm)

    row_spec = pl.BlockSpec((tm, d), lambda i: (i, 0))
    vec_spec = pl.BlockSpec((1, d), lambda i: (0, 0))
    act_spec = pl.BlockSpec((None, tm, f4), lambda i: (k, i, 0))
    in_specs = [act_spec, act_spec, ANY, ANY, ANY]
    args = [a, b, wg, wu, wd]
    if last:
        in_specs = [row_spec, row_spec, vec_spec] + in_specs
        args = list(tail) + args
    else:
        in_specs = [row_spec, row_spec] + in_specs
        args = [dy, n] + args
    if not first:
        in_specs.insert(0, row_spec)
        args.insert(0, dn_prev)
    out_specs = [row_spec, ANY, ANY, ANY]
    out_shape = [jax.ShapeDtypeStruct((tp, d), F32)] + [jax.ShapeDtypeStruct((f4, d), BF16)] * 3
    scratch = [pltpu.VMEM((f4, d), BF16)] * 3 + [pltpu.VMEM((f4, d), F32)] * 3
    if last:
        out_specs.append(vec_spec)
        out_shape.append(jax.ShapeDtypeStruct((1, d), F32))
    if unpad:
        out_specs[0] = ANY
        out_shape[0] = jax.ShapeDtypeStruct((unpad[1], d), F32)
        out_specs.append(pl.BlockSpec((unpad[0], d), lambda i: (0, 0)))
        out_shape.append(jax.ShapeDtypeStruct((unpad[0], d), F32))
        scratch += [pltpu.VMEM((tm, d), F32), pltpu.SemaphoreType.DMA(())]
    n_host = len(out_shape)
    split = _attach_carry(carry, in_specs, args, out_specs, out_shape, scratch)
    outs = pl.pallas_call(
        body, name=f"{name}_{k}", grid=(ni,), in_specs=in_specs, out_specs=out_specs, out_shape=out_shape,
        scratch_shapes=scratch, compiler_params=_params(("arbitrary",)),
    )(*args)
    return outs[:n_host], outs[n_host:]


def _ffn_bwd(dh_out, dy, h_in, n, g, a, b, wg, wu, wd, name, chain=None, unpad=None):
    ns = wg.shape[0]
    acc, dwg, dwu, dwd = None, [], [], []
    for k in range(ns):
        carry = chain.carry() if chain is not None else None
        outs, carried = _ffn_bwd_shard(k, ns, acc, dy, n, a, b, wg, wu, wd, (dh_out, h_in, g), name, carry, unpad)
        if chain is not None:
            chain.feed(carried)
        acc = outs[0]
        dwg.append(outs[1])
        dwu.append(outs[2])
        dwd.append(outs[3])
    return (acc, dwg, dwu, dwd) + tuple(outs[4:])


def _win_fwd(h, g, w_in, carry=None):
    tp, d = h.shape
    ns = w_in.shape[0]
    tm = _pick_tile(tp, (768, 512, 256))
    ni = tp // tm

    def body(*refs):
        (h_ref, g_ref, w_ref, u_ref, p_ref), phases = split(refs)
        _run_phases(phases, carry, pl.program_id(0), ni)
        xhat, _ = _rms_stats(h_ref[...])
        u = (xhat * g_ref[...]).astype(BF16)
        u_ref[...] = u
        for k in range(ns):
            p_ref[k] = _dot(u, w_ref[k]).astype(BF16)

    in_specs = [pl.BlockSpec((tm, d), lambda i: (i, 0)),
                pl.BlockSpec((1, d), lambda i: (0, 0)),
                pl.BlockSpec((ns, d, d), lambda i: (0, 0, 0))]
    out_specs = [pl.BlockSpec((tm, d), lambda i: (i, 0)),
                 pl.BlockSpec((ns, tm, d), lambda i: (0, i, 0))]
    out_shape = [jax.ShapeDtypeStruct((tp, d), BF16), jax.ShapeDtypeStruct((ns, tp, d), BF16)]
    args, scratch = [h, g, w_in], []
    split = _attach_carry(carry, in_specs, args, out_specs, out_shape, scratch)
    return pl.pallas_call(
        body, name="win_fwd", grid=(ni,), in_specs=in_specs, out_specs=out_specs, out_shape=out_shape,
        scratch_shapes=scratch, compiler_params=_params(("arbitrary",)),
    )(*args)


def _win_bwd_shard(k, ns, du_prev, dpb, dug, u, w_in, h1, g, dh2):
    tp, d = h1.shape
    dh = d // 2
    tm = _pick_tile(tp, (768, 512, 256))
    first, last = k == 0, k == ns - 1

    def body(*refs):
        refs = list(refs)
        acc_in = None if first else refs.pop(0)
        dug_ref = refs.pop(0) if first else None
        dp_ref, u_ref, w_ref = refs[:3]
        refs = refs[3:]
        if last:
            h_ref, g_ref, dh2_ref, acc_out, dw_ref, dg_ref, dy_ref, dw_acc = refs
        else:
            acc_out, dw_ref, dw_acc = refs
        i = pl.program_id(0)

        @pl.when(i == 0)
        def _():
            dw_acc[...] = jnp.zeros_like(dw_acc)
            if last:
                dg_ref[...] = jnp.zeros_like(dg_ref)

        dp = dp_ref[...]
        if first:
            dp = jnp.concatenate([dug_ref[...], dp[:, dh:]], axis=1)
        dw_acc[...] += _dot_tn(u_ref[...], dp)
        du = _dot_nt(dp, w_ref[...])
        if not first:
            du = du + acc_in[...]
        if last:
            xhat, r = _rms_stats(h_ref[...])
            dg_ref[...] += jnp.sum(du * xhat, axis=0, keepdims=True)
            dh1 = dh2_ref[...] + _rms_bwd(xhat, r, g_ref[...], du)
            acc_out[...] = dh1
            dy_ref[...] = (0.5 * dh1).astype(BF16)
        else:
            acc_out[...] = du

        @pl.when(i == tp // tm - 1)
        def _():
            dw_ref[...] = dw_acc[...].astype(BF16)

    row_spec = pl.BlockSpec((tm, d), lambda i: (i, 0))
    vec_spec = pl.BlockSpec((1, d), lambda i: (0, 0))
    in_specs = [pl.BlockSpec((None, tm, d), lambda i: (k, i, 0)), row_spec,
                pl.BlockSpec((None, d, d), lambda i: (k, 0, 0))]
    args = [dpb, u, w_in]
    if first:
        in_specs.insert(0, pl.BlockSpec((tm, dh), lambda i: (i, 0)))
        args.insert(0, dug)
    else:
        in_specs.insert(0, row_spec)
        args.insert(0, du_prev)
    out_specs = [row_spec, pl.BlockSpec((d, d), lambda i: (0, 0))]
    out_shape = [jax.ShapeDtypeStruct((tp, d), F32), jax.ShapeDtypeStruct((d, d), BF16)]
    if last:
        in_specs += [row_spec, vec_spec, row_spec]
        args += [h1, g, dh2]
        out_specs += [vec_spec, row_spec]
        out_shape += [jax.ShapeDtypeStruct((1, d), F32), jax.ShapeDtypeStruct((tp, d), BF16)]
    return pl.pallas_call(
        body, name=f"win_bwd_{k}", grid=(tp // tm,), in_specs=in_specs, out_specs=out_specs,
        out_shape=out_shape, scratch_shapes=[pltpu.VMEM((d, d), F32)],
        compiler_params=_params(("arbitrary",)),
    )(*args)


def _win_bwd(dpb, dug, u, w_in, h1, g, dh2):
    ns = w_in.shape[0]
    acc, dws = None, []
    for k in range(ns):
        outs = _win_bwd_shard(k, ns, acc, dpb, dug, u, w_in, h1, g, dh2)
        acc = outs[0]
        dws.append(outs[1])
    return acc, dws, outs[2], outs[3]


def _cmul(ar, ai, br, bi):
    return ar * br - ai * bi, ar * bi + ai * br


def _scan_rows(j, sub):
    return pl.ds(j * SCAN_SEQS, SCAN_SEQS)


def _permute_rows(src_ref, dst_ref, sub):
    for j in range(sub):
        dst_ref[pl.ds(j * SCAN_SEQS, SCAN_SEQS), :] = src_ref[pl.ds(j, SCAN_SEQS, stride=sub), :]


def _unpermute_rows(src_ref, dst_ref, sub):
    for j in range(sub):
        dst_ref[pl.ds(j, SCAN_SEQS, stride=sub), :] = src_ref[pl.ds(j * SCAN_SEQS, SCAN_SEQS), :]


def _local_scan(x_ref, lr, li, w, sub, reverse):
    hr = jnp.zeros((SCAN_SEQS, w), F32)
    hi = jnp.zeros((SCAN_SEQS, w), F32)
    order = range(sub - 1, -1, -1) if reverse else range(sub)
    for j in order:
        xr = x_ref[_scan_rows(j, sub), pl.ds(0, w)]
        xi = x_ref[_scan_rows(j, sub), pl.ds(w, w)]
        if reverse:
            hr, hi = lr * hr + li * hi + xr, lr * hi - li * hr + xi
        else:
            hr, hi = lr * hr - li * hi + xr, lr * hi + li * hr + xi
        x_ref[_scan_rows(j, sub), pl.ds(0, w)] = hr
        x_ref[_scan_rows(j, sub), pl.ds(w, w)] = hi
    return hr, hi


def _entering_states(er, ei, fr, fi, pow_ref, w, sub, reverse):
    lane = lax.broadcasted_iota(jnp.int32, (SCAN_SEQS, w), 0)
    if reverse:
        edge, shift1 = SCAN_SEQS - 1, SCAN_SEQS - 1
    else:
        edge, shift1 = 0, 1
    zr = jnp.where(lane == edge, pltpu.roll(fr, shift1, 0), pltpu.roll(er, shift1, 0))
    zi = jnp.where(lane == edge, pltpu.roll(fi, shift1, 0), pltpu.roll(ei, shift1, 0))
    for m in range(SCAN_SEQS.bit_length() - 1):
        step, row = 1 << m, sub - 1 + m
        ar = pow_ref[pl.ds(row, 1), pl.ds(0, w)]
        ai = pow_ref[pl.ds(row, 1), pl.ds(w, w)]
        if reverse:
            ai = -ai
            keep = lane < SCAN_SEQS - step
            sr = jnp.where(keep, pltpu.roll(zr, SCAN_SEQS - step, 0), 0.0)
            si = jnp.where(keep, pltpu.roll(zi, SCAN_SEQS - step, 0), 0.0)
        else:
            keep = lane >= step
            sr = jnp.where(keep, pltpu.roll(zr, step, 0), 0.0)
            si = jnp.where(keep, pltpu.roll(zi, step, 0), 0.0)
        pr, pi = _cmul(ar, ai, sr, si)
        zr, zi = zr + pr, zi + pi
    ar = pow_ref[pl.ds(sub - 1, 1), pl.ds(0, w)]
    ai = pow_ref[pl.ds(sub - 1, 1), pl.ds(w, w)]
    if reverse:
        ai = -ai
    pr, pi = _cmul(ar, ai, zr, zi)
    return zr, zi, er + pr, ei + pi


def _scan_fwd(p, mb, mc, powt, dskip):
    _, tp, d = p.shape
    nb, cb, w2 = mb.shape
    w = w2 // 2
    q = SCAN_TILE
    sub = q // SCAN_SEQS
    nt = tp // q
    ds = d // 2

    def body(ug_ref, mb_ref, mc_ref, pow_ref, d_ref, y_ref, bnd_ref, x_scr, carry, nat, perm):
        t = pl.program_id(1)

        @pl.when(t == 0)
        def _():
            carry[...] = jnp.zeros_like(carry)

        ugf = ug_ref[...].astype(F32)
        nat[...] = ugf
        _permute_rows(nat, perm, sub)
        x_scr[...] = _dot(perm[...].astype(BF16), mb_ref[...])
        lr = jnp.broadcast_to(pow_ref[pl.ds(0, 1), pl.ds(0, w)], (SCAN_SEQS, w))
        li = jnp.broadcast_to(pow_ref[pl.ds(0, 1), pl.ds(w, w)], (SCAN_SEQS, w))
        er, ei = _local_scan(x_scr, lr, li, w, sub, False)
        zr, zi, fr, fi = _entering_states(er, ei, carry[:, pl.ds(0, w)], carry[:, pl.ds(w, w)],
                                          pow_ref, w, sub, False)
        carry[:, pl.ds(0, w)] = fr
        carry[:, pl.ds(w, w)] = fi
        bnd_ref[:, pl.ds(0, w)] = fr
        bnd_ref[:, pl.ds(w, w)] = fi
        for j in range(sub):
            pr = pow_ref[pl.ds(j, 1), pl.ds(0, w)]
            pi = pow_ref[pl.ds(j, 1), pl.ds(w, w)]
            cr, ci = _cmul(pr, pi, zr, zi)
            x_scr[_scan_rows(j, sub), pl.ds(0, w)] += cr
            x_scr[_scan_rows(j, sub), pl.ds(w, w)] += ci
        hb = x_scr[...].astype(BF16)
        perm[...] = _dot_nt(hb, mc_ref[...])
        _unpermute_rows(perm, nat, sub)
        y_ref[...] = nat[...] + d_ref[...] * ugf

    in_specs = [pl.BlockSpec((None, q, cb), lambda b, t: (0, t, b)),
                pl.BlockSpec((None, cb, w2), lambda b, t: (b, 0, 0)),
                pl.BlockSpec((None, cb, w2), lambda b, t: (b, 0, 0)),
                pl.BlockSpec((None, powt.shape[1], w2), lambda b, t: (b, 0, 0)),
                pl.BlockSpec((1, cb), lambda b, t: (0, b))]
    out_specs = [pl.BlockSpec((q, cb), lambda b, t: (t, b)),
                 pl.BlockSpec((None, None, SCAN_SEQS, w2), lambda b, t: (b, t, 0, 0))]
    out_shape = [jax.ShapeDtypeStruct((tp, ds), F32), jax.ShapeDtypeStruct((nb, nt, SCAN_SEQS, w2), F32)]
    scratch = [pltpu.VMEM((q, w2), F32), pltpu.VMEM((SCAN_SEQS, w2), F32),
               pltpu.VMEM((q, cb), F32), pltpu.VMEM((q, cb), F32)]
    return pl.pallas_call(
        body, name="s5_scan_fwd", grid=(nb, nt), in_specs=in_specs, out_specs=out_specs,
        out_shape=out_shape, scratch_shapes=scratch, compiler_params=_params(("arbitrary", "arbitrary")),
    )(p, mb, mc, powt, dskip)


def _scan_bwd(p, dy, mb, mc, powt, dskip, bnd):
    _, tp, d = p.shape
    nb, cb, w2 = mb.shape
    w = w2 // 2
    q = SCAN_TILE
    sub = q // SCAN_SEQS
    nt = tp // q
    ds = d // 2

    def body(ug_ref, dy_ref, mb_ref, mc_ref, pow_ref, d_ref, bnd_ref,
             dug_ref, dmb_ref, dmc_ref, dlam_ref, dd_ref, x_scr, y_scr, gcarry, nat, perm):
        t = pl.program_id(1)
        tt = nt - 1 - t

        @pl.when(t == 0)
        def _():
            gcarry[...] = jnp.zeros_like(gcarry)
            dmb_ref[...] = jnp.zeros_like(dmb_ref)
            dmc_ref[...] = jnp.zeros_like(dmc_ref)
            dlam_ref[...] = jnp.zeros_like(dlam_ref)
            dd_ref[...] = jnp.zeros_like(dd_ref)

        ugf = ug_ref[...].astype(F32)
        dyf = dy_ref[...].astype(F32)
        dd_ref[...] += jnp.sum((dyf * ugf).reshape(q // SUBLANES, SUBLANES, cb), axis=0)
        nat[...] = ugf
        _permute_rows(nat, perm, sub)
        ug = perm[...].astype(BF16)
        nat[...] = dyf
        _permute_rows(nat, perm, sub)
        dyb = perm[...].astype(BF16)
        lr = jnp.broadcast_to(pow_ref[pl.ds(0, 1), pl.ds(0, w)], (SCAN_SEQS, w))
        li = jnp.broadcast_to(pow_ref[pl.ds(0, 1), pl.ds(w, w)], (SCAN_SEQS, w))

        x_scr[...] = _dot(ug, mb_ref[...])
        er, ei = _local_scan(x_scr, lr, li, w, sub, False)
        first = tt == 0
        pfr = jnp.where(first, 0.0, bnd_ref[:, pl.ds(0, w)])
        pfi = jnp.where(first, 0.0, bnd_ref[:, pl.ds(w, w)])
        hzr, hzi, _, _ = _entering_states(er, ei, pfr, pfi, pow_ref, w, sub, False)
        for j in range(sub):
            pr = pow_ref[pl.ds(j, 1), pl.ds(0, w)]
            pi = pow_ref[pl.ds(j, 1), pl.ds(w, w)]
            cr, ci = _cmul(pr, pi, hzr, hzi)
            x_scr[_scan_rows(j, sub), pl.ds(0, w)] += cr
            x_scr[_scan_rows(j, sub), pl.ds(w, w)] += ci

        y_scr[...] = _dot(dyb, mc_ref[...])
        er, ei = _local_scan(y_scr, lr, li, w, sub, True)
        gzr, gzi, fr, fi = _entering_states(er, ei, gcarry[:, pl.ds(0, w)], gcarry[:, pl.ds(w, w)],
                                            pow_ref, w, sub, True)
        gcarry[:, pl.ds(0, w)] = fr
        gcarry[:, pl.ds(w, w)] = fi
        accr = jnp.zeros((SCAN_SEQS, w), F32)
        acci = jnp.zeros((SCAN_SEQS, w), F32)
        for j in range(sub):
            pr = pow_ref[pl.ds(sub - 1 - j, 1), pl.ds(0, w)]
            pi = pow_ref[pl.ds(sub - 1 - j, 1), pl.ds(w, w)]
            cr, ci = _cmul(pr, -pi, gzr, gzi)
            gr = y_scr[_scan_rows(j, sub), pl.ds(0, w)] + cr
            gi = y_scr[_scan_rows(j, sub), pl.ds(w, w)] + ci
            y_scr[_scan_rows(j, sub), pl.ds(0, w)] = gr
            y_scr[_scan_rows(j, sub), pl.ds(w, w)] = gi
            if j == 0:
                hpr, hpi = hzr, hzi
            else:
                hpr = x_scr[_scan_rows(j - 1, sub), pl.ds(0, w)]
                hpi = x_scr[_scan_rows(j - 1, sub), pl.ds(w, w)]
            accr += hpr * gr + hpi * gi
            acci += hpr * gi - hpi * gr
        dlam_ref[:, pl.ds(0, w)] += accr
        dlam_ref[:, pl.ds(w, w)] += acci

        hb = x_scr[...].astype(BF16)
        gb = y_scr[...].astype(BF16)
        dmc_ref[...] += _dot_tn(dyb, hb)
        dmb_ref[...] += _dot_tn(ug, gb)
        perm[...] = _dot_nt(gb, mb_ref[...])
        _unpermute_rows(perm, nat, sub)
        dug_ref[...] = (nat[...] + d_ref[...] * dyf).astype(BF16)

    blk = lambda b, t: (b, 0, 0)
    return pl.pallas_call(
        body, name="s5_scan_bwd", grid=(nb, nt),
        in_specs=[pl.BlockSpec((None, q, cb), lambda b, t: (0, nt - 1 - t, b)),
                  pl.BlockSpec((q, cb), lambda b, t: (nt - 1 - t, b)),
                  pl.BlockSpec((None, cb, w2), blk),
                  pl.BlockSpec((None, cb, w2), blk),
                  pl.BlockSpec((None, powt.shape[1], w2), blk),
                  pl.BlockSpec((1, cb), lambda b, t: (0, b)),
                  pl.BlockSpec((None, None, SCAN_SEQS, w2),
                               lambda b, t: (b, jnp.maximum(nt - 2 - t, 0), 0, 0))],
        out_specs=[pl.BlockSpec((q, cb), lambda b, t: (nt - 1 - t, b)),
                   pl.BlockSpec((None, cb, w2), blk),
                   pl.BlockSpec((None, cb, w2), blk),
                   pl.BlockSpec((None, SCAN_SEQS, w2), blk),
                   pl.BlockSpec((SUBLANES, cb), lambda b, t: (0, b))],
        out_shape=[jax.ShapeDtypeStruct((tp, ds), BF16),
                   jax.ShapeDtypeStruct((nb, cb, w2), F32),
                   jax.ShapeDtypeStruct((nb, cb, w2), F32),
                   jax.ShapeDtypeStruct((nb, SCAN_SEQS, w2), F32),
                   jax.ShapeDtypeStruct((SUBLANES, ds), F32)],
        scratch_shapes=[pltpu.VMEM((q, w2), F32), pltpu.VMEM((q, w2), F32),
                        pltpu.VMEM((SCAN_SEQS, w2), F32), pltpu.VMEM((q, cb), F32), pltpu.VMEM((q, cb), F32)],
        compiler_params=_params(("arbitrary", "arbitrary")),
    )(p, dy, mb, mc, powt, dskip, bnd)


HALO = 16


def _mix_tile(ys5, p0, p1, p2, p3, prev_cin, cw, bgate, wglu, wco, d):
    dh = d // 2
    tm = ys5.shape[0]
    v = p0[:, dh:].astype(F32)
    gbr = p1[:, :dh].astype(F32)
    gcr = p1[:, dh:].astype(F32)
    gact = _gelu(ys5).astype(BF16)
    z = _dot(gact, wglu)
    z1, z2 = z[:, :d], z[:, d:]
    sg = _sigmoid(z2)
    y_ssm = z1 * sg
    cin = gcr * v
    ext = jnp.concatenate([cin, prev_cin], axis=0)
    r1 = pltpu.roll(ext, 1, 0)[:tm]
    r2 = pltpu.roll(ext, 2, 0)[:tm]
    cv = cw[2] * cin + cw[1] * r1 + cw[0] * r2
    cg = (gbr * cv).astype(BF16)
    y_conv = _dot(cg, wco)
    g_s = _sigmoid(p2.astype(F32) + bgate[:, :d])
    g_c = _sigmoid(p3.astype(F32) + bgate[:, d:])
    mixed = g_s * y_ssm + g_c * y_conv
    return dict(v=v, gb=gbr, gc=gcr, gact=gact, z1=z1, sg=sg, y_ssm=y_ssm, cin=cin, r1=r1, r2=r2,
                cv=cv, cg=cg, y_conv=y_conv, g_s=g_s, g_c=g_c, mixed=mixed)


def _mix_fwd(h1, ys5, p, cw, bgate, wglu, wco, wo, carry=None):
    tp, d = h1.shape
    dh = d // 2
    tm = ROW_ALIGN
    ni = tp // tm

    def body(*refs):
        refs, phases = split(refs)
        (h_ref, y_ref, p0_ref, p1_ref, p2_ref, p3_ref, cw_ref, bg_ref, wglu_ref, wco_ref, wo_ref,
         o_ref, prev) = refs
        _run_phases(phases, carry, pl.program_id(0), ni)

        @pl.when(pl.program_id(0) == 0)
        def _():
            prev[...] = jnp.zeros_like(prev)

        cw = [cw_ref[pl.ds(t, 1), :] for t in range(3)]
        f = _mix_tile(y_ref[...], p0_ref[...], p1_ref[...], p2_ref[...], p3_ref[...], prev[...],
                      cw, bg_ref[...], wglu_ref[...], wco_ref[...], d)
        prev[...] = f["cin"][tm - HALO:, :]
        o_ref[...] = h_ref[...] + _dot(f["mixed"].astype(BF16), wo_ref[...])

    row = pl.BlockSpec((tm, d), lambda i: (i, 0))
    full = lambda a: pl.BlockSpec(a.shape, lambda i: (0,) * a.ndim)
    pk = lambda k: pl.BlockSpec((None, tm, d), lambda i, k=k: (k, i, 0))
    in_specs = [row, pl.BlockSpec((tm, dh), lambda i: (i, 0)), pk(0), pk(1), pk(2), pk(3),
                full(cw), full(bgate), full(wglu), full(wco), full(wo)]
    out_specs, out_shape = [row], [jax.ShapeDtypeStruct((tp, d), F32)]
    args, scratch = [h1, ys5, p, p, p, p, cw, bgate, wglu, wco, wo], [pltpu.VMEM((HALO, dh), F32)]
    split = _attach_carry(carry, in_specs, args, out_specs, out_shape, scratch)
    return pl.pallas_call(
        body, name="mix_fwd", grid=(ni,), in_specs=in_specs, out_specs=out_specs, out_shape=out_shape,
        scratch_shapes=scratch, compiler_params=_params(("arbitrary",)),
    )(*args)


def _mix_bwd(dh2, ys5, p, cw, bgate, wglu, wco, wo):
    tp, d = dh2.shape
    dh = d // 2
    tm = ROW_ALIGN
    ni = tp // tm
    hb = tm // HALO

    def body(dh_ref, y_ref, p0_ref, p1_ref, p2_ref, p3_ref, h0_ref, h1_ref,
             cw_ref, bg_ref, wglu_ref, wco_ref, wo_ref,
             dys_ref, dpb_ref, dwo_ref, dwglu_ref, dwco_ref, dcw_ref, dbg_ref, nxt):
        i = pl.program_id(0)
        tt = ni - 1 - i

        @pl.when(i == 0)
        def _():
            nxt[...] = jnp.zeros_like(nxt)
            dwo_ref[...] = jnp.zeros_like(dwo_ref)
            dwglu_ref[...] = jnp.zeros_like(dwglu_ref)
            dwco_ref[...] = jnp.zeros_like(dwco_ref)
            dcw_ref[...] = jnp.zeros_like(dcw_ref)
            dbg_ref[...] = jnp.zeros_like(dbg_ref)

        cw = [cw_ref[pl.ds(t, 1), :] for t in range(3)]
        prev_cin = h1_ref[:, dh:].astype(F32) * h0_ref[:, dh:].astype(F32)
        prev_cin = jnp.where(tt == 0, 0.0, prev_cin)
        ys5 = y_ref[...]
        f = _mix_tile(ys5, p0_ref[...], p1_ref[...], p2_ref[...], p3_ref[...], prev_cin,
                      cw, bg_ref[...], wglu_ref[...], wco_ref[...], d)
        dhb = dh_ref[...].astype(BF16)
        dmixed = _dot_nt(dhb, wo_ref[...])
        dwo_ref[...] += _dot_tn(f["mixed"].astype(BF16), dhb)

        g_s, g_c, sg = f["g_s"], f["g_c"], f["sg"]
        dy_ssm = dmixed * g_s
        dy_conv = dmixed * g_c
        dp2 = dmixed * f["y_ssm"] * g_s * (1.0 - g_s)
        dp3 = dmixed * f["y_conv"] * g_c * (1.0 - g_c)
        dbg_ref[:, pl.ds(0, d)] += jnp.sum(dp2, axis=0, keepdims=True)
        dbg_ref[:, pl.ds(d, d)] += jnp.sum(dp3, axis=0, keepdims=True)

        dz = jnp.concatenate([dy_ssm * sg, dy_ssm * f["z1"] * sg * (1.0 - sg)], axis=1).astype(BF16)
        dwglu_ref[...] += _dot_tn(f["gact"], dz)
        dys_ref[...] = (_dot_nt(dz, wglu_ref[...]) * _gelu_grad(ys5)).astype(BF16)

        dycb = dy_conv.astype(BF16)
        dwco_ref[...] += _dot_tn(f["cg"], dycb)
        dcg = _dot_nt(dycb, wco_ref[...])
        dgb = dcg * f["cv"]
        dcv = dcg * f["gb"]
        ext = jnp.concatenate([dcv, nxt[...]], axis=0)
        n1 = pltpu.roll(ext, tm + HALO - 1, 0)[:tm]
        n2 = pltpu.roll(ext, tm + HALO - 2, 0)[:tm]
        nxt[...] = dcv[:HALO, :]
        dcin = cw[2] * dcv + cw[1] * n1 + cw[0] * n2
        dcw_ref[pl.ds(0, 1), :] += jnp.sum(dcv * f["r2"], axis=0, keepdims=True)
        dcw_ref[pl.ds(1, 1), :] += jnp.sum(dcv * f["r1"], axis=0, keepdims=True)
        dcw_ref[pl.ds(2, 1), :] += jnp.sum(dcv * f["cin"], axis=0, keepdims=True)
        dgc = dcin * f["v"]
        dv = dcin * f["gc"]
        dpb_ref[0] = jnp.concatenate([jnp.zeros_like(dv), dv], axis=1).astype(BF16)
        dpb_ref[1] = jnp.concatenate([dgb, dgc], axis=1).astype(BF16)
        dpb_ref[2] = dp2.astype(BF16)
        dpb_ref[3] = dp3.astype(BF16)

    rev = lambda i: ni - 1 - i
    row = pl.BlockSpec((tm, d), lambda i: (rev(i), 0))
    half = pl.BlockSpec((tm, dh), lambda i: (rev(i), 0))
    full = lambda a: pl.BlockSpec(a.shape, lambda i: (0,) * a.ndim)
    pk = lambda k: pl.BlockSpec((None, tm, d), lambda i, k=k: (k, rev(i), 0))
    halo = lambda k: pl.BlockSpec((None, HALO, d), lambda i, k=k: (k, jnp.maximum(rev(i) * hb - 1, 0), 0))
    acc = lambda shape: pl.BlockSpec(shape, lambda i: (0,) * len(shape))
    return pl.pallas_call(
        body, name="mix_bwd", grid=(ni,),
        in_specs=[row, half, pk(0), pk(1), pk(2), pk(3), halo(0), halo(1),
                  full(cw), full(bgate), full(wglu), full(wco), full(wo)],
        out_specs=[half, pl.BlockSpec((4, tm, d), lambda i: (0, rev(i), 0)),
                   acc((d, d)), acc((dh, 2 * d)), acc((dh, d)), acc((SUBLANES, dh)), acc((1, 2 * d))],
        out_shape=[jax.ShapeDtypeStruct((tp, dh), BF16), jax.ShapeDtypeStruct((4, tp, d), BF16),
                   jax.ShapeDtypeStruct((d, d), F32), jax.ShapeDtypeStruct((dh, 2 * d), F32),
                   jax.ShapeDtypeStruct((dh, d), F32), jax.ShapeDtypeStruct((SUBLANES, dh), F32),
                   jax.ShapeDtypeStruct((1, 2 * d), F32)],
        scratch_shapes=[pltpu.VMEM((HALO, dh), F32)],
        compiler_params=_params(("arbitrary",)),
    )(dh2, ys5, p, p, p, p, p, p, cw, bgate, wglu, wco, wo)


ANY = pl.BlockSpec(memory_space=pl.ANY)


def _position():
    return lax.axis_index("x"), lax.axis_index("y"), lax.axis_index("c")


def _remote(src, dst, ssem, rsem, dev):
    return pltpu.make_async_remote_copy(src_ref=src, dst_ref=dst, send_sem=ssem, recv_sem=rsem,
                                        device_id=dev, device_id_type=MESH)


def _cast_piece(w, pos):
    rows, cols = w.shape
    r2 = rows // 2

    def body(pos_ref, w_ref, o_ref):
        o_ref[...] = w_ref[...].astype(BF16)

    return pl.pallas_call(
        body, name="cast_piece",
        grid_spec=pltpu.PrefetchScalarGridSpec(
            num_scalar_prefetch=1, grid=(1,),
            in_specs=[pl.BlockSpec((r2, cols), lambda i, pos: (pos[2], 0))],
            out_specs=pl.BlockSpec((r2, cols), lambda i, pos: (0, 0))),
        out_shape=jax.ShapeDtypeStruct((r2, cols), BF16),
        compiler_params=_params(("arbitrary",)),
    )(pos, w)


class _Carry:
    def __init__(self, name, arrays, out_shapes, nsem, nlsem, make, fracs):
        self.name, self.arrays, self.out_shapes = name, list(arrays), list(out_shapes)
        self.nsem, self.nlsem, self.make, self.fracs = nsem, max(nlsem, 1), make, fracs


def _carry_scratch(carry):
    return [pltpu.SemaphoreType.DMA((carry.nsem,)), pltpu.SemaphoreType.DMA((carry.nsem,)),
            pltpu.SemaphoreType.DMA((carry.nlsem,))]


def _run_carry(carry):
    na, no = len(carry.arrays), len(carry.out_shapes)

    def body(*refs):
        for phase in carry.make(refs[:na], refs[na:na + no], *refs[na + no:]):
            phase()

    return pl.pallas_call(
        body, name=carry.name, in_specs=[ANY] * na, out_specs=[ANY] * no, out_shape=carry.out_shapes,
        scratch_shapes=_carry_scratch(carry),
    )(*carry.arrays)


def _attach_carry(carry, in_specs, args, out_specs, out_shape, scratch):
    nhi, nho, nhs = len(in_specs), len(out_specs), len(scratch)
    if carry is None:
        return lambda refs: (list(refs), [])
    na, no = len(carry.arrays), len(carry.out_shapes)
    in_specs += [ANY] * na
    args += carry.arrays
    out_specs += [ANY] * no
    out_shape += carry.out_shapes
    scratch += _carry_scratch(carry)

    def split(refs):
        refs = list(refs)
        o = nhi + na
        host = refs[:nhi] + refs[o:o + nho] + refs[o + nho + no:o + nho + no + nhs]
        sems = refs[o + nho + no + nhs:]
        return host, carry.make(refs[nhi:o], refs[o + nho:o + nho + no], *sems)

    return split


def _run_phases(phases, carry, step, total):
    for phase, frac in zip(phases, carry.fracs if carry is not None else ()):
        pl.when(step == int(round(frac * (total - 1))))(phase)


def _allgather_carry(name, pieces, smalls):
    n, ns = len(pieces), len(smalls)
    per = 14
    n_big = per * n

    def make(ins, outs, ssem, rsem, lsem):
        pin, sin = ins[:n], ins[n:]
        wall, sall = outs[:n], outs[n:]
        x, y, c = _position()
        xnb, ynb, sib = (1 - x, y, c), (x, 1 - y, c), (x, y, 1 - c)
        chips = [(1 - x, y), (x, 1 - y), (1 - x, 1 - y)]
        r4 = [p.shape[0] // 2 for p in pin]
        own = lambda i, h: pin[i].at[pl.ds(h * r4[i], r4[i]), :]
        slot = lambda i, xx, yy, cc, h: wall[i].at[xx, yy, cc, h]
        cp = lambda src, dst, s, dev: _remote(src, dst, ssem.at[s], rsem.at[s], dev)
        to_sib = lambda i, xx, yy, h: cp(slot(i, xx, yy, c, h), slot(i, xx, yy, c, h),
                                         per * i + 6 + 4 * xx + 2 * yy + h, sib)

        def local():
            cps = [pltpu.make_async_copy(own(i, h), slot(i, x, y, c, h), lsem.at[2 * i + h])
                   for i in range(n) for h in range(2)]
            return cps + [pltpu.make_async_copy(sin[i], sall[i].at[2 * x + y], lsem.at[2 * n + i])
                          for i in range(ns)]

        def small(px, py, j, i, landing):
            s = n_big + j * ns + i
            return cp(sin[i], sall[i].at[landing], s, (px, py, c))

        def first_hop():
            for lc in local():
                lc.start()
            for j, (px, py) in enumerate(chips):
                for i in range(ns):
                    small(px, py, j, i, 2 * x + y).start()
            for i in range(n):
                cp(own(i, 0), slot(i, x, y, c, 0), per * i, xnb).start()
                cp(own(i, 1), slot(i, x, y, c, 1), per * i + 1, ynb).start()
                for h in range(2):
                    cp(own(i, h), slot(i, x, y, c, h), per * i + 6 + 4 * x + 2 * y + h, sib).start()

        def second_hop():
            for lc in local():
                lc.wait()
            for i in range(n):
                cp(slot(i, 1 - x, y, c, 0), slot(i, 1 - x, y, c, 0), per * i, xnb).wait_recv()
                cp(slot(i, x, 1 - y, c, 1), slot(i, x, 1 - y, c, 1), per * i + 1, ynb).wait_recv()
                for j in range(2):
                    cp(slot(i, j, y, c, 0), slot(i, j, y, c, 0), per * i + 2 + j, ynb).start()
                    cp(slot(i, x, j, c, 1), slot(i, x, j, c, 1), per * i + 4 + j, xnb).start()
                to_sib(i, 1 - x, y, 0).start()
                to_sib(i, x, 1 - y, 1).start()

        def last_to_sibling():
            for i in range(n):
                for j in range(2):
                    cp(slot(i, j, 1 - y, c, 0), slot(i, j, 1 - y, c, 0), per * i + 2 + j, ynb).wait_recv()
                    cp(slot(i, 1 - x, j, c, 1), slot(i, 1 - x, j, c, 1), per * i + 4 + j, xnb).wait_recv()
                    to_sib(i, j, 1 - y, 0).start()
                    to_sib(i, 1 - x, j, 1).start()

        def finish():
            for i in range(n):
                for xx in range(2):
                    for yy in range(2):
                        for h in range(2):
                            s = per * i + 6 + 4 * xx + 2 * yy + h
                            cp(slot(i, xx, yy, 1 - c, h), slot(i, xx, yy, 1 - c, h), s, sib).wait_recv()
                            to_sib(i, xx, yy, h).wait_send()
                cp(own(i, 0), slot(i, x, y, c, 0), per * i, xnb).wait_send()
                cp(own(i, 1), slot(i, x, y, c, 1), per * i + 1, ynb).wait_send()
                for j in range(2):
                    cp(slot(i, j, y, c, 0), slot(i, j, y, c, 0), per * i + 2 + j, ynb).wait_send()
                    cp(slot(i, x, j, c, 1), slot(i, x, j, c, 1), per * i + 4 + j, xnb).wait_send()
            for j, (px, py) in enumerate(chips):
                for i in range(ns):
                    small(px, py, j, i, 2 * px + py).wait_recv()
                    small(px, py, j, i, 2 * x + y).wait_send()

        return [first_hop, second_hop, last_to_sibling, finish]

    out_shapes = [jax.ShapeDtypeStruct((2, 2, 2, 2, a.shape[0] // 2, a.shape[1]), a.dtype) for a in pieces]
    out_shapes += [jax.ShapeDtypeStruct((4,) + a.shape, a.dtype) for a in smalls]
    return _Carry(name, list(pieces) + list(smalls), out_shapes, n_big + 3 * ns, 2 * n + ns, make,
                  (0.0, 0.23, 0.73, 1.0))


def _exchange_carry(name, arrays, out_shapes, plan):
    count = plan([None] * len(arrays), [None] * len(out_shapes), None)

    def make(ins, outs, ssem, rsem, lsem):
        def copies():
            return [_remote(src, dst, ssem.at[j], rsem.at[j], peer)
                    for j, (src, dst, peer) in enumerate(plan(ins, outs, _position()))]

        def start():
            for c in copies():
                c.start()

        def wait():
            for c in copies():
                c.wait()

        return [start, wait]

    return _Carry(name, arrays, out_shapes, count, 0, make, (0.0, 1.0))


class _Grad:
    def __init__(self, arrs, kind, shard_shape):
        self.arrs, self.kind = list(arrs), kind
        self.rows, self.cols = shard_shape
        self.r2 = self.rows // 2

    def view(self, refs, k, h):
        r2 = self.r2
        if self.kind == "list":
            return refs[k].at[pl.ds(h * r2, r2), :]
        if self.kind == "stacked":
            return refs[0].at[k, pl.ds(h * r2, r2), :]
        if self.kind == "col":
            return refs[0].at[pl.ds(h * r2, r2), pl.ds(k * self.cols, self.cols)]
        return refs[0].at[pl.ds((2 * k + h) * r2, r2), :]

    def add_half(self, recv, pos):
        r2, cols = self.r2, self.cols
        n_in = len(self.arrs)
        tr = _row_tile(r2, cols)
        nt = r2 // tr

        def body(pos_ref, *refs):
            m_refs, (r_ref, of_ref, ob_ref) = refs[:n_in], refs[n_in:]
            mine = m_refs[0][...]
            for kk in range(1, n_in):
                mine = jnp.where(pl.program_id(1) == kk, m_refs[kk][...], mine)
            s = mine.astype(F32) + r_ref[...].astype(F32)
            of_ref[...] = s
            ob_ref[...] = s.astype(BF16)

        row = lambda t, pos: pos[2] * nt + t
        if self.kind == "list":
            specs = [pl.BlockSpec((tr, cols), lambda t, k, pos: (row(t, pos), 0))] * n_in
        elif self.kind == "stacked":
            specs = [pl.BlockSpec((None, tr, cols), lambda t, k, pos: (k, row(t, pos), 0))]
        elif self.kind == "col":
            specs = [pl.BlockSpec((tr, cols), lambda t, k, pos: (row(t, pos), k))]
        else:
            specs = [pl.BlockSpec((tr, cols), lambda t, k, pos: (2 * k * nt + row(t, pos), 0))]
        blk = pl.BlockSpec((None, tr, cols), lambda t, k, pos: (k, t, 0))
        return pl.pallas_call(
            body, name="rs_add_c",
            grid_spec=pltpu.PrefetchScalarGridSpec(
                num_scalar_prefetch=1, grid=(nt, 4), in_specs=specs + [blk], out_specs=[blk, blk]),
            out_shape=[jax.ShapeDtypeStruct((4, r2, cols), F32), jax.ShapeDtypeStruct((4, r2, cols), BF16)],
            compiler_params=_params(("arbitrary", "arbitrary")),
        )(pos, *self.arrs, recv)


def _row_tile(rows, cols):
    fits = [t for t in range(16, rows + 1, 16) if rows % t == 0 and t * cols * 4 <= 2 * 1024 * 1024]
    return max(fits) if fits else rows


def _adamw_math(w, g, m, v):
    m = ADAM_B1 * m + (1.0 - ADAM_B1) * g
    v = ADAM_B2 * v + (1.0 - ADAM_B2) * (g * g)
    m_hat = m / (1.0 - ADAM_B1 ** ADAM_STEP)
    v_hat = v / (1.0 - ADAM_B2 ** ADAM_STEP)
    delta = -ADAM_LR * (m_hat / (jnp.sqrt(v_hat) + ADAM_EPS) + ADAM_WD * w)
    return delta, m, v


def _adamw_big(w, m, v, own, sib, pos):
    rows, cols = w.shape
    r2 = rows // 2

    tr = _row_tile(r2, cols)
    nt = r2 // tr

    def body(pos_ref, w_ref, m_ref, v_ref, own_ref, sib_ref, g_ref, d_ref, nm_ref, nv_ref):
        h = pl.program_id(0)
        g = jnp.where(h == pos_ref[2], own_ref[...], sib_ref[...])
        g_ref[...] = g
        d_ref[...], nm_ref[...], nv_ref[...] = _adamw_math(w_ref[...], g, m_ref[...], v_ref[...])

    half = pl.BlockSpec((tr, cols), lambda h, t, pos: (h * nt + t, 0))
    piece = pl.BlockSpec((tr, cols), lambda h, t, pos: (t, 0))
    out = jax.ShapeDtypeStruct((rows, cols), F32)
    return pl.pallas_call(
        body, name="adamw",
        grid_spec=pltpu.PrefetchScalarGridSpec(
            num_scalar_prefetch=1, grid=(2, nt),
            in_specs=[half, half, half, piece, piece],
            out_specs=[half, half, half, half]),
        out_shape=[out, out, out, out],
        compiler_params=_params(("arbitrary", "arbitrary")),
    )(pos, w, m, v, own, sib)


def _add_hop1(s1f, recv, pos):
    _, _, r4, cols = recv.shape
    s1v = s1f.reshape(4, 2, r4, cols)

    def body(pos_ref, m_ref, r_ref, of_ref, ob_ref):
        s = m_ref[...] + r_ref[...].astype(F32)
        of_ref[...] = s
        ob_ref[...] = s.astype(BF16)

    def mine(h, j, pos):
        return (jnp.where(h == 0, 2 * j + pos[1], 2 * pos[0] + j), h, 0, 0)

    blk = pl.BlockSpec((None, None, r4, cols), lambda h, j, pos: (h, j, 0, 0))
    return pl.pallas_call(
        body, name="rs_add_1",
        grid_spec=pltpu.PrefetchScalarGridSpec(
            num_scalar_prefetch=1, grid=(2, 2),
            in_specs=[pl.BlockSpec((None, None, r4, cols), mine), blk], out_specs=[blk, blk]),
        out_shape=[jax.ShapeDtypeStruct((2, 2, r4, cols), F32), jax.ShapeDtypeStruct((2, 2, r4, cols), BF16)],
        compiler_params=_params(("arbitrary", "arbitrary")),
    )(pos, s1v, recv)


def _own_sum(s2f, recv3, pos):
    _, _, r4, cols = s2f.shape

    def body(pos_ref, s_ref, r_ref, o_ref):
        o_ref[...] = s_ref[...] + r_ref[...].astype(F32)

    blk = pl.BlockSpec((None, r4, cols), lambda h, pos: (h, 0, 0))
    return pl.pallas_call(
        body, name="own_sum",
        grid_spec=pltpu.PrefetchScalarGridSpec(
            num_scalar_prefetch=1, grid=(2,),
            in_specs=[pl.BlockSpec((None, None, r4, cols),
                                   lambda h, pos: (h, jnp.where(h == 0, pos[0], pos[1]), 0, 0)), blk],
            out_specs=blk),
        out_shape=jax.ShapeDtypeStruct((2, r4, cols), F32),
        compiler_params=_params(("arbitrary",)),
    )(pos, s2f, recv3)


def _allreduce_small(buf):
    def body(x_ref, o_ref, recv, ssem, rsem):
        x, y, c = _position()
        o_ref[...] = x_ref[...]
        for s, peer in enumerate([(x, y, 1 - c), (x, 1 - y, c), (1 - x, y, c)]):
            cp = _remote(o_ref, recv.at[s], ssem.at[s], rsem.at[s], peer)
            cp.start()
            cp.wait()
            o_ref[...] = o_ref[...] + recv[s]

    vm = pl.BlockSpec(memory_space=pltpu.VMEM)
    return pl.pallas_call(
        body, name="allreduce_small", in_specs=[vm], out_specs=vm,
        out_shape=jax.ShapeDtypeStruct(buf.shape, F32),
        scratch_shapes=[pltpu.VMEM((3,) + buf.shape, F32),
                        pltpu.SemaphoreType.DMA((3,)), pltpu.SemaphoreType.DMA((3,))],
    )(buf)


def _adamw_small(w, g, m, v):
    def body(w_ref, g_ref, m_ref, v_ref, d_ref, nm_ref, nv_ref):
        d_ref[...], nm_ref[...], nv_ref[...] = _adamw_math(w_ref[...], g_ref[...], m_ref[...], v_ref[...])

    vm = pl.BlockSpec(memory_space=pltpu.VMEM)
    out = jax.ShapeDtypeStruct(w.shape, F32)
    return pl.pallas_call(body, name="adamw_small", in_specs=[vm] * 4, out_specs=[vm] * 3,
                          out_shape=[out, out, out])(w, g, m, v)


class _ReduceScatter:
    def __init__(self, tag, grads, pos):
        self.tag, self.grads, self.pos, self.stage = tag, grads, pos, 0

    def carry(self):
        grads, n = self.grads, len(self.grads)
        r4 = [g.r2 // 2 for g in grads]

        first = [sum(len(g.arrs) for g in grads[:i]) for i in range(n)]

        def plan_c(ins, outs, p):
            if p is None:
                return 4 * n
            x, y, c = p
            mine = lambda i: ins[first[i]:first[i] + len(grads[i].arrs)]
            return [(grads[i].view(mine(i), k, 1 - c), outs[i].at[k], (x, y, 1 - c))
                    for i in range(n) for k in range(4)]

        def plan_1(ins, outs, p):
            if p is None:
                return 4 * n
            x, y, c = p
            copies = []
            for i in range(n):
                for j in range(2):
                    copies.append((ins[i].at[2 * j + (1 - y), pl.ds(0, r4[i]), :], outs[i].at[0, j],
                                   (x, 1 - y, c)))
                    copies.append((ins[i].at[2 * (1 - x) + j, pl.ds(r4[i], r4[i]), :], outs[i].at[1, j],
                                   (1 - x, y, c)))
            return copies

        def plan_2(ins, outs, p):
            if p is None:
                return 2 * n
            x, y, c = p
            copies = []
            for i in range(n):
                copies.append((ins[i].at[0, 1 - x], outs[i].at[0], (1 - x, y, c)))
                copies.append((ins[i].at[1, 1 - y], outs[i].at[1], (x, 1 - y, c)))
            return copies

        def plan_s(ins, outs, p):
            if p is None:
                return n
            x, y, c = p
            return [(ins[i], outs[i], (x, y, 1 - c)) for i in range(n)]

        shape = lambda lead, dt: [jax.ShapeDtypeStruct(lead(g) + (g.cols,), dt) for g in grads]
        stage = self.stage
        if stage == 0:
            return _exchange_carry(f"rs_{self.tag}_exchange_c", [a for g in grads for a in g.arrs],
                                   [jax.ShapeDtypeStruct((4, g.r2, g.cols), g.arrs[0].dtype) for g in grads],
                                   plan_c)
        if stage == 1:
            return _exchange_carry(f"rs_{self.tag}_exchange_1", [s[1] for s in self.s1],
                                   shape(lambda g: (2, 2, g.r2 // 2), BF16), plan_1)
        if stage == 2:
            return _exchange_carry(f"rs_{self.tag}_exchange_2", [s[1] for s in self.s2],
                                   shape(lambda g: (2, g.r2 // 2), BF16), plan_2)
        return _exchange_carry(f"rs_{self.tag}_exchange_sibling", self.own, shape(lambda g: (g.r2,), F32), plan_s)

    def feed(self, recv):
        grads, pos = self.grads, self.pos
        if self.stage == 0:
            self.s1 = [g.add_half(r, pos) for g, r in zip(grads, recv)]
        elif self.stage == 1:
            self.s2 = [_add_hop1(s[0], r, pos) for s, r in zip(self.s1, recv)]
        elif self.stage == 2:
            self.own = [_own_sum(s[0], r, pos).reshape(g.r2, g.cols) for g, s, r in zip(grads, self.s2, recv)]
        else:
            self.sib = list(recv)
        self.stage += 1

    def run(self):
        while self.stage < 4:
            self.feed(_run_carry(self.carry()))

    def adamw(self, weights):
        return [_adamw_big(w, m, v, o, sb, self.pos) for (w, m, v), o, sb in zip(weights, self.own, self.sib)]


def _block_diag(t, nb):
    g, c, p = t.shape
    gb = g // nb
    t = t.reshape(nb, gb, c, p)
    eye = jnp.eye(gb, dtype=t.dtype)
    return jnp.einsum("bgcp,gh->bgchp", t, eye).reshape(nb, gb * c, gb * p)


def _s5_discretise(a_re, a_im, log_dt, b_re, b_im, c_re, c_im):
    g, p = a_re.shape
    nb = g // GROUPS_PER_BLOCK
    dt = jnp.exp(log_dt)[:, None]
    mag = jnp.exp(a_re * dt)
    lam_re = mag * jnp.cos(a_im * dt)
    lam_im = mag * jnp.sin(a_im * dt)
    den = a_re * a_re + a_im * a_im
    q_re = ((lam_re - 1.0) * a_re + lam_im * a_im) / den
    q_im = (lam_im * a_re - (lam_re - 1.0) * a_im) / den
    bb_re = q_re[..., None] * b_re - q_im[..., None] * b_im
    bb_im = q_re[..., None] * b_im + q_im[..., None] * b_re
    tr = lambda t: jnp.swapaxes(t, 1, 2)
    mb = jnp.concatenate([_block_diag(tr(bb_re), nb), _block_diag(tr(bb_im), nb)], axis=-1)
    mc = jnp.concatenate([_block_diag(c_re, nb), -_block_diag(c_im, nb)], axis=-1)
    lam = jnp.concatenate([lam_re.reshape(nb, -1), lam_im.reshape(nb, -1)], axis=-1)
    return mb, mc, lam


def _s5_powers(a_re, a_im, log_dt, sub):
    g, p = a_re.shape
    nb = g // GROUPS_PER_BLOCK
    dt = jnp.exp(log_dt)[:, None]
    ns = list(range(1, sub + 1)) + [sub << m for m in range(1, SCAN_SEQS.bit_length() - 1)]
    ns += [0] * (-len(ns) % SUBLANES)
    e = jnp.asarray(ns, F32)[:, None, None]
    mag = jnp.exp(a_re[None] * dt[None] * e)
    ang = a_im[None] * dt[None] * e
    re = (mag * jnp.cos(ang)).reshape(len(ns), nb, -1)
    im = (mag * jnp.sin(ang)).reshape(len(ns), nb, -1)
    return jnp.transpose(jnp.concatenate([re, im], axis=-1), (1, 0, 2))


def _pack(parts):
    flat = jnp.concatenate([a.reshape(-1).astype(F32) for a in parts])
    n = flat.shape[0]
    pad = -n % (SUBLANES * LANES)
    return jnp.pad(flat, (0, pad)).reshape(-1, LANES)


def _unpack(buf, like):
    flat = buf.reshape(-1)
    out, o = [], 0
    for a in like:
        out.append(flat[o:o + a.size].reshape(a.shape))
        o += a.size
    return out


def kernel(x, meta_tokens, g_ffn1, ffn1_w_gate, ffn1_w_up, ffn1_w_down, g_mix, w_in, b_gate, ssm_a_re, ssm_a_im, ssm_log_dt, ssm_b_re, ssm_b_im, ssm_c_re, ssm_c_im, ssm_d, ssm_w_glu, conv_w, conv_w_out, w_o, g_ffn2, ffn2_w_gate, ffn2_w_up, ffn2_w_down, g_final, loss_target, m_meta_tokens, m_g_ffn1, m_ffn1_w_gate, m_ffn1_w_up, m_ffn1_w_down, m_g_mix, m_w_in, m_b_gate, m_ssm_a_re, m_ssm_a_im, m_ssm_log_dt, m_ssm_b_re, m_ssm_b_im, m_ssm_c_re, m_ssm_c_im, m_ssm_d, m_ssm_w_glu, m_conv_w, m_conv_w_out, m_w_o, m_g_ffn2, m_ffn2_w_gate, m_ffn2_w_up, m_ffn2_w_down, m_g_final, v_meta_tokens, v_g_ffn1, v_ffn1_w_gate, v_ffn1_w_up, v_ffn1_w_down, v_g_mix, v_w_in, v_b_gate, v_ssm_a_re, v_ssm_a_im, v_ssm_log_dt, v_ssm_b_re, v_ssm_b_im, v_ssm_c_re, v_ssm_c_im, v_ssm_d, v_ssm_w_glu, v_conv_w, v_conv_w_out, v_w_o, v_g_ffn2, v_ffn2_w_gate, v_ffn2_w_up, v_ffn2_w_down, v_g_final):
    seq, d = x.shape[1], x.shape[2]
    n_meta = meta_tokens.shape[0]
    dh = d // 2
    tp = -(-(n_meta + seq) // ROW_ALIGN) * ROW_ALIGN
    mx, my, mc_ = _position()
    pos = jnp.stack([mx, my, mc_]).astype(jnp.int32)
    shard = 2 * mx + my

    big_names = ["ffn1_w_gate", "ffn1_w_up", "ffn1_w_down", "w_in", "ssm_w_glu", "conv_w_out", "w_o",
                 "ffn2_w_gate", "ffn2_w_up", "ffn2_w_down"]
    transposed = {0, 1, 7, 8}
    drop = lambda arrs: [jnp.swapaxes(a.reshape(a.shape[1:]), 0, 1) if i in transposed else a.reshape(a.shape[1:])
                         for i, a in enumerate(arrs)]
    big_w = drop([ffn1_w_gate, ffn1_w_up, ffn1_w_down, w_in, ssm_w_glu, conv_w_out, w_o,
                  ffn2_w_gate, ffn2_w_up, ffn2_w_down])
    big_m = drop([m_ffn1_w_gate, m_ffn1_w_up, m_ffn1_w_down, m_w_in, m_ssm_w_glu, m_conv_w_out,
                  m_w_o, m_ffn2_w_gate, m_ffn2_w_up, m_ffn2_w_down])
    big_v = drop([v_ffn1_w_gate, v_ffn1_w_up, v_ffn1_w_down, v_w_in, v_ssm_w_glu, v_conv_w_out,
                  v_w_o, v_ffn2_w_gate, v_ffn2_w_up, v_ffn2_w_down])
    pieces = [_cast_piece(w, pos) for w in big_w]
    conv_local = conv_w.reshape(conv_w.shape[1], conv_w.shape[3])
    n_first = 3
    first = _run_carry(_allgather_carry("allgather_first", pieces[:n_first], [meta_tokens, conv_local]))
    smalls = first[n_first:]
    stack4 = lambda wl: wl.reshape((4, -1, wl.shape[-1]))
    w1g, w1u, w1d = [stack4(wl) for wl in first[:n_first]]
    natural_cols = lambda s: jnp.transpose(s, (1, 0, 2)).reshape(s.shape[1], 4 * s.shape[2])
    meta_full = natural_cols(smalls[0])
    cw_full = natural_cols(smalls[1])
    cw_pad = jnp.pad(cw_full, ((0, SUBLANES - cw_full.shape[0]), (0, 0)))

    s5_args = (ssm_a_re[0], ssm_a_im[0], ssm_log_dt[0], ssm_b_re[0], ssm_b_im[0], ssm_c_re[0], ssm_c_im[0])
    (mb, mc, _), disc_vjp = jax.vjp(_s5_discretise, *s5_args)
    powt = _s5_powers(ssm_a_re[0], ssm_a_im[0], ssm_log_dt[0], SCAN_TILE // SCAN_SEQS)
    mb16, mc16 = mb.astype(BF16), mc.astype(BF16)

    pad_rows = tp - n_meta - seq
    h0 = jnp.concatenate([meta_full, x.reshape(seq, d), jnp.zeros((pad_rows, d), F32)], axis=0)
    h1, a1, b1, n1, *mid = _ffn_fwd(h0, g_ffn1, w1g, w1u, w1d, "ffn1_fwd",
                                    carry=_allgather_carry("allgather_mixer", pieces[3:7], []))
    win_all, wglu_s, wco_s, wo_s = [stack4(wl) for wl in mid]
    wglu_all = natural_cols(wglu_s)
    wco_all = natural_cols(wco_s)
    wo_all = wo_s.reshape(d, d)
    u, p, w2g, w2u = _win_fwd(h1, g_mix, win_all, carry=_allgather_carry("allgather_ffn2_in", pieces[7:9], []))
    ys5, bnd = _scan_fwd(p, mb16, mc16, powt, ssm_d)
    h2, w2d = _mix_fwd(h1, ys5, p, cw_pad, b_gate, wglu_all, wco_all, wo_all,
                       carry=_allgather_carry("allgather_ffn2_out", pieces[9:], []))
    w2g, w2u, w2d = stack4(w2g), stack4(w2u), stack4(w2d)
    dh3, a2, b2, n2, dg_final, loss_part, dy3 = _ffn_fwd(
        h2, g_ffn2, w2g, w2u, w2d, "ffn2_fwd_loss",
        final=(g_final.reshape(1, d), loss_target.reshape(seq, d), n_meta, seq))

    dh2, dw2g, dw2u, dw2d, dg_ffn2 = _ffn_bwd(dh3, dy3, h2, n2, g_ffn2, a2, b2, w2g, w2u, w2d, "ffn2_bwd")
    dys5, dpb, dwo, dwglu, dwco, dcw, dbg = _mix_bwd(dh2, ys5, p, cw_pad, b_gate, wglu_all, wco_all, wo_all)
    dug, dmb, dmc, dlam, dd = _scan_bwd(p, dys5, mb16, mc16, powt, ssm_d, bnd)
    dh1, dwin, dg_mix, dy1 = _win_bwd(dpb, dug, u, win_all, h1, g_mix, dh2)
    shapes = [w.shape for w in big_w]
    kinds = ["list", "list", "list", "list", "col", "col", "row", "list", "list", "list"]
    rest_grads = [dwin, [dwglu], [dwco], [dwo], dw2g, dw2u, dw2d]
    rs_rest = _ReduceScatter("rest", [_Grad(a, k, s) for a, k, s in
                                      zip(rest_grads, kinds[n_first:], shapes[n_first:])], pos)
    grad_x, dw1g, dw1u, dw1d, dg_ffn1, grad_meta = _ffn_bwd(
        dh1, dy1, h0, n1, g_ffn1, a1, b1, w1g, w1u, w1d, "ffn1_bwd", chain=rs_rest, unpad=(n_meta, seq))
    rs_first = _ReduceScatter("first", [_Grad(a, k, s) for a, k, s in
                                        zip([dw1g, dw1u, dw1d], kinds[:n_first], shapes[:n_first])], pos)
    rs_first.run()
    wmv = list(zip(big_w, big_m, big_v))
    big_out = rs_first.adamw(wmv[:n_first]) + rs_rest.adamw(wmv[n_first:])
    def lead(i, o):
        o = jnp.swapaxes(o, 0, 1) if i in transposed else o
        return o.reshape((1,) + o.shape)

    big_out = {nme: tuple(lead(i, o) for o in outs) for i, (nme, outs) in enumerate(zip(big_names, big_out))}

    s5_grads = disc_vjp((dmb, dmc, jnp.sum(dlam, axis=1)))
    grad_x = grad_x.reshape(1, seq, d)

    small_names = ["g_ffn1", "g_mix", "b_gate", "ssm_a_re", "ssm_a_im", "ssm_log_dt", "ssm_b_re", "ssm_b_im",
                   "ssm_c_re", "ssm_c_im", "ssm_d", "g_ffn2", "g_final", "meta_tokens", "conv_w"]
    small_w = [g_ffn1, g_mix, b_gate, ssm_a_re, ssm_a_im, ssm_log_dt, ssm_b_re, ssm_b_im, ssm_c_re, ssm_c_im,
               ssm_d, g_ffn2, g_final, meta_tokens, conv_w]
    small_m = [m_g_ffn1, m_g_mix, m_b_gate, m_ssm_a_re, m_ssm_a_im, m_ssm_log_dt, m_ssm_b_re, m_ssm_b_im,
               m_ssm_c_re, m_ssm_c_im, m_ssm_d, m_g_ffn2, m_g_final, m_meta_tokens, m_conv_w]
    small_v = [v_g_ffn1, v_g_mix, v_b_gate, v_ssm_a_re, v_ssm_a_im, v_ssm_log_dt, v_ssm_b_re, v_ssm_b_im,
               v_ssm_c_re, v_ssm_c_im, v_ssm_d, v_g_ffn2, v_g_final, v_meta_tokens, v_conv_w]
    local_small = [dg_ffn1, dg_mix, dbg, *s5_grads, jnp.sum(dd, axis=0), dg_ffn2, dg_final,
                   grad_meta, dcw[:conv_w.shape[1]]]
    reduced = _unpack(_allreduce_small(_pack(local_small)), local_small)
    reduced[-2] = lax.dynamic_slice_in_dim(reduced[-2], shard * meta_tokens.shape[1], meta_tokens.shape[1], 1)
    reduced[-1] = lax.dynamic_slice_in_dim(reduced[-1], shard * conv_w.shape[3], conv_w.shape[3], 1)
    small_g = [r.reshape(w.shape) for r, w in zip(reduced, small_w)]
    ds_, nm_, nv_ = _adamw_small(_pack(small_w), _pack(small_g), _pack(small_m), _pack(small_v))
    small_out = {nme: o for nme, o in zip(
        small_names, zip(small_g, _unpack(ds_, small_w), _unpack(nm_, small_w), _unpack(nv_, small_w)))}

    loss = lax.psum(loss_part[0, 0], ("x", "y", "c"))
    order = ["meta_tokens", "g_ffn1", "ffn1_w_gate", "ffn1_w_up", "ffn1_w_down", "g_mix", "w_in", "b_gate",
             "ssm_a_re", "ssm_a_im", "ssm_log_dt", "ssm_b_re", "ssm_b_im", "ssm_c_re", "ssm_c_im", "ssm_d",
             "ssm_w_glu", "conv_w", "conv_w_out", "w_o", "g_ffn2", "ffn2_w_gate", "ffn2_w_up", "ffn2_w_down",
             "g_final"]
    res = {**big_out, **small_out}
    return (loss, grad_x, *[res[nme][0] for nme in order], *[res[nme][1] for nme in order],
            *[res[nme][2] for nme in order], *[res[nme][3] for nme in order])
```

```python
import functools
import math

import jax
import jax.numpy as jnp
from jax import lax
from jax.experimental import pallas as pl
from jax.experimental.pallas import tpu as pltpu

F32 = jnp.float32
BF16 = jnp.bfloat16
MESH = pl.DeviceIdType.MESH

RMS_EPS = 1e-6
ADAM_LR = 0.001
ADAM_B1 = 0.9
ADAM_B2 = 0.999
ADAM_EPS = 1e-08
ADAM_WD = 0.01
ADAM_STEP = 10

LANES = 128
SUBLANES = 8
VMEM_LIMIT = 56 * 1024 * 1024

ROW_ALIGN = 256
SCAN_TILE = 256
SCAN_SEQS = 16
GROUPS_PER_BLOCK = 8


def _params(sem, vmem=VMEM_LIMIT):
    return pltpu.CompilerParams(dimension_semantics=sem, vmem_limit_bytes=vmem)


def _pick_tile(n, candidates):
    for c in candidates:
        if n % c == 0:
            return c
    raise ValueError(f"no tile for {n}")


def _dot(a, b):
    return jnp.dot(a, b, preferred_element_type=F32)


def _dot_nt(a, b):
    return lax.dot_general(a, b, (((1,), (1,)), ((), ())), preferred_element_type=F32)


def _dot_tn(a, b):
    return lax.dot_general(a, b, (((0,), (0,)), ((), ())), preferred_element_type=F32)


def _sigmoid(x):
    return pl.reciprocal(1.0 + jnp.exp(-x), approx=True)


def _rms_stats(h):
    r = lax.rsqrt(jnp.mean(h * h, axis=-1, keepdims=True) + RMS_EPS)
    return h * r, r


def _rms_bwd(xhat, r, g, dn):
    dxh = dn * g
    return r * (dxh - xhat * jnp.mean(dxh * xhat, axis=-1, keepdims=True))


GELU_K = math.sqrt(2.0 / math.pi)
GELU_C = 0.044715


def _gelu(x):
    return 0.5 * x * (1.0 + jnp.tanh(GELU_K * (x + GELU_C * x * x * x)))


def _gelu_grad(x):
    t = jnp.tanh(GELU_K * (x + GELU_C * x * x * x))
    return 0.5 * (1.0 + t) + 0.5 * x * (1.0 - t * t) * GELU_K * (1.0 + 3.0 * GELU_C * x * x)


def _for_tile_rows(i, ni, tm, n_meta, seq, fn):
    pl.when(i == 0)(lambda: fn(0, min(tm - n_meta, seq), n_meta))
    if ni > 1:
        last_lo = (ni - 1) * tm - n_meta
        pl.when(i == ni - 1)(lambda: fn(last_lo, min(seq - last_lo, tm), 0))
    if ni > 2:
        pl.when((i > 0) & (i < ni - 1))(lambda: fn(pl.multiple_of(i * tm - n_meta, SUBLANES), tm, 0))


def _ffn_fwd(h, g, wg, wu, wd, name, final=None, carry=None):
    tp, d = h.shape
    ns, f4, _ = wg.shape
    tm = _pick_tile(tp, (768, 512, 256))
    ni = tp // tm

    def body(*refs):
        refs, phases = split(refs)
        if final is None:
            h_ref, g_ref, wg_ref, wu_ref, wd_ref, ho_ref, a_ref, b_ref, n_scr, acc = refs
        else:
            (h_ref, g_ref, wg_ref, wu_ref, wd_ref, gf_ref, tg_hbm,
             ho_ref, a_ref, b_ref, n_scr, dgf_ref, loss_ref, dy_ref, acc, tg_ref, tg_sem) = refs
        i = pl.program_id(0)
        k = pl.program_id(1)
        _run_phases(phases, carry, i * ns + k, ni * ns)

        if final is not None:
            def target_rows(lo, n, at):
                return pltpu.make_async_copy(tg_hbm.at[pl.ds(lo, n), :], tg_ref.at[pl.ds(at, n), :], tg_sem)

            def fetch_target(lo, n, at):
                if at > 0:
                    tg_ref[pl.ds(0, at), :] = jnp.zeros((at, d), F32)
                if at + n < tm:
                    tg_ref[pl.ds(at + n, tm - at - n), :] = jnp.zeros((tm - at - n, d), F32)
                target_rows(lo, n, at).start()

            pl.when(k == 0)(lambda: _for_tile_rows(i, ni, tm, final[2], final[3], fetch_target))

        @pl.when(k == 0)
        def _():
            xhat, _ = _rms_stats(h_ref[...])
            n_scr[...] = (xhat * g_ref[...]).astype(BF16)
            acc[...] = jnp.zeros_like(acc)

        n = n_scr[...]
        a = _dot_nt(n, wg_ref[...])
        b = _dot_nt(n, wu_ref[...])
        a_ref[...] = a.astype(BF16)
        b_ref[...] = b.astype(BF16)
        s = (a * _sigmoid(a) * b).astype(BF16)
        acc[...] += _dot(s, wd_ref[...])

        if final is None:
            @pl.when(k == ns - 1)
            def _():
                ho_ref[...] = h_ref[...] + 0.5 * acc[...]
        else:
            n_meta, seq = final[2], final[3]

            @pl.when((i == 0) & (k == 0))
            def _():
                dgf_ref[...] = jnp.zeros_like(dgf_ref)
                loss_ref[...] = jnp.zeros_like(loss_ref)

            @pl.when(k == ns - 1)
            def _():
                _for_tile_rows(i, ni, tm, n_meta, seq, lambda lo, n, at: target_rows(lo, n, at).wait())
                h3 = h_ref[...] + 0.5 * acc[...]
                xhat, r = _rms_stats(h3)
                gf = gf_ref[...]
                row = i * tm + lax.broadcasted_iota(jnp.int32, (tm, d), 0)
                valid = (row >= n_meta) & (row < n_meta + seq)
                diff = jnp.where(valid, xhat * gf - tg_ref[...], 0.0)
                dout = diff * (1.0 / d)
                loss_ref[...] += jnp.full(loss_ref.shape, 0.5 * jnp.sum(diff * diff) * (1.0 / d), F32)
                dgf_ref[...] += jnp.sum(dout * xhat, axis=0, keepdims=True)
                dh3 = _rms_bwd(xhat, r, gf, dout)
                ho_ref[...] = dh3
                dy_ref[...] = (0.5 * dh3).astype(BF16)

    row_spec = pl.BlockSpec((tm, d), lambda i, k: (i, 0))
    vec_spec = pl.BlockSpec((1, d), lambda i, k: (0, 0))
    in_specs = [row_spec, vec_spec,
                pl.BlockSpec((None, f4, d), lambda i, k: (k, 0, 0)),
                pl.BlockSpec((None, f4, d), lambda i, k: (k, 0, 0)),
                pl.BlockSpec((None, f4, d), lambda i, k: (k, 0, 0))]
    act_spec = pl.BlockSpec((None, tm, f4), lambda i, k: (k, i, 0))
    out_specs = [row_spec, act_spec, act_spec, row_spec]
    out_shape = [jax.ShapeDtypeStruct((tp, d), F32),
                 jax.ShapeDtypeStruct((ns, tp, f4), BF16),
                 jax.ShapeDtypeStruct((ns, tp, f4), BF16),
                 jax.ShapeDtypeStruct((tp, d), BF16)]
    args = [h, g, wg, wu, wd]
    scratch = [pltpu.VMEM((tm, d), F32)]
    if final is not None:
        in_specs += [vec_spec, ANY]
        args += [final[0], final[1]]
        out_specs += [vec_spec, pl.BlockSpec((1, LANES), lambda i, k: (0, 0)), row_spec]
        out_shape += [jax.ShapeDtypeStruct((1, d), F32), jax.ShapeDtypeStruct((1, LANES), F32),
                      jax.ShapeDtypeStruct((tp, d), BF16)]
        scratch += [pltpu.VMEM((tm, d), F32), pltpu.SemaphoreType.DMA(())]
    split = _attach_carry(carry, in_specs, args, out_specs, out_shape, scratch)
    return pl.pallas_call(
        body, name=name, grid=(ni, ns), in_specs=in_specs, out_specs=out_specs, out_shape=out_shape,
        scratch_shapes=scratch, compiler_params=_params(("arbitrary", "arbitrary")),
    )(*args)


def _ffn_bwd_shard(k, ns, dn_prev, dy, n, a, b, wg, wu, wd, tail, name, carry=None, unpad=None):
    tp, d = n.shape
    f4 = wg.shape[1]
    tm = _pick_tile(tp, (768, 512, 256))
    ni = tp // tm
    first, last = k == 0, k == ns - 1
    unpad = unpad if last else None

    def body(*refs):
        refs, phases = split(refs)
        acc_in = None if first else refs.pop(0)
        if last:
            dh_ref, h_ref, g_ref = refs[:3]
        else:
            dy_ref, n_ref = refs[:2]
        refs = refs[3 if last else 2:]
        a_ref, b_ref, wg_hbm, wu_hbm, wd_hbm = refs[:5]
        refs = refs[5:]
        acc_out, dwg_hbm, dwu_hbm, dwd_hbm = refs[:4]
        rest = refs[4:]
        dg_ref = rest.pop(0) if last else None
        head_ref = rest.pop(0) if unpad else None
        wg_ref, wu_ref, wd_ref, dwg_ref, dwu_ref, dwd_ref = rest[:6]
        i = pl.program_id(0)
        _run_phases(phases, carry, i, ni)
        if unpad:
            res_ref, res_sem = rest[6:]

            def real_rows(lo, cnt, at):
                return pltpu.make_async_copy(res_ref.at[pl.ds(at, cnt), :], acc_out.at[pl.ds(lo, cnt), :], res_sem)

            def wait_tile(tile):
                _for_tile_rows(tile, ni, tm, *unpad, lambda lo, cnt, at: real_rows(lo, cnt, at).wait())

        @pl.when(i == 0)
        def _():
            pltpu.sync_copy(wg_hbm.at[k], wg_ref)
            pltpu.sync_copy(wu_hbm.at[k], wu_ref)
            pltpu.sync_copy(wd_hbm.at[k], wd_ref)
            dwg_ref[...] = jnp.zeros_like(dwg_ref)
            dwu_ref[...] = jnp.zeros_like(dwu_ref)
            dwd_ref[...] = jnp.zeros_like(dwd_ref)
            if last:
                dg_ref[...] = jnp.zeros_like(dg_ref)

        if last:
            xhat, r = _rms_stats(h_ref[...])
            n = (xhat * g_ref[...]).astype(BF16)
            dy = (0.5 * dh_ref[...]).astype(BF16)
        else:
            n = n_ref[...]
            dy = dy_ref[...]
        av = a_ref[...].astype(F32)
        bv = b_ref[...].astype(F32)
        sg = _sigmoid(av)
        silu = av * sg
        ds = _dot_nt(dy, wd_ref[...])
        da = (ds * bv * (sg * (1.0 + av * (1.0 - sg)))).astype(BF16)
        db = (ds * silu).astype(BF16)
        s = (silu * bv).astype(BF16)
        dwd_ref[...] += _dot_tn(s, dy)
        dwg_ref[...] += _dot_tn(da, n)
        dwu_ref[...] += _dot_tn(db, n)
        dn = _dot(da, wg_ref[...]) + _dot(db, wu_ref[...])
        if not first:
            dn = dn + acc_in[...]
        if last:
            dg_ref[...] += jnp.sum(dn * xhat, axis=0, keepdims=True)
            dh_in = dh_ref[...] + _rms_bwd(xhat, r, g_ref[...], dn)
            if unpad:
                pl.when(i > 0)(lambda: wait_tile(i - 1))
                res_ref[...] = dh_in

                @pl.when(i == 0)
                def _():
                    head_ref[...] = res_ref[pl.ds(0, unpad[0]), :]

                _for_tile_rows(i, ni, tm, *unpad, lambda lo, cnt, at: real_rows(lo, cnt, at).start())
                pl.when(i == ni - 1)(lambda: wait_tile(i))
            else:
                acc_out[...] = dh_in
        else:
            acc_out[...] = dn

        @pl.when(i == ni - 1)
        def _():
            for acc_ref, stage_ref, out_hbm in ((dwg_ref, wg_ref, dwg_hbm), (dwu_ref, wu_ref, dwu_hbm),
                                                (dwd_ref, wd_ref, dwd_hbm)):
                stage_ref[...] = acc_ref[...].astype(BF16)
                pltpu.sync_copy(stage_ref, out_hbm)

    row_spec = pl.BlockSpec((tm, d), lambda i: (i, 0))
    vec_spec = pl.BlockSpec((1, d), lambda i: (0, 0))
    act_spec = pl.BlockSpec((None, tm, f4), lambda i: (k, i, 0))
    in_specs = [act_spec, act_spec, ANY, ANY, ANY]
    args = [a, b, wg, wu, wd]
    if last:
        in_specs = [row_spec, row_spec, vec_spec] + in_specs
        args = list(tail) + args
    else:
        in_specs = [row_spec, row_spec] + in_specs
        args = [dy, n] + args
    if not first:
        in_specs.insert(0, row_spec)
        args.insert(0, dn_prev)
    out_specs = [row_spec, ANY, ANY, ANY]
    out_shape = [jax.ShapeDtypeStruct((tp, d), F32)] + [jax.ShapeDtypeStruct((f4, d), BF16)] * 3
    scratch = [pltpu.VMEM((f4, d), BF16)] * 3 + [pltpu.VMEM((f4, d), F32)] * 3
    if last:
        out_specs.append(vec_spec)
        out_shape.append(jax.ShapeDtypeStruct((1, d), F32))
    if unpad:
        out_specs[0] = ANY
        out_shape[0] = jax.ShapeDtypeStruct((unpad[1], d), F32)
        out_specs.append(pl.BlockSpec((unpad[0], d), lambda i: (0, 0)))
        out_shape.append(jax.ShapeDtypeStruct((unpad[0], d), F32))
        scratch += [pltpu.VMEM((tm, d), F32), pltpu.SemaphoreType.DMA(())]
    n_host = len(out_shape)
    split = _attach_carry(carry, in_specs, args, out_specs, out_shape, scratch)
    outs = pl.pallas_call(
        body, name=f"{name}_{k}", grid=(ni,), in_specs=in_specs, out_specs=out_specs, out_shape=out_shape,
        scratch_shapes=scratch, compiler_params=_params(("arbitrary",)),
    )(*args)
    return outs[:n_host], outs[n_host:]


def _ffn_bwd(dh_out, dy, h_in, n, g, a, b, wg, wu, wd, name, chain=None, unpad=None):
    ns = wg.shape[0]
    acc, dwg, dwu, dwd = None, [], [], []
    for k in range(ns):
        carry = chain.carry() if chain is not None else None
        outs, carried = _ffn_bwd_shard(k, ns, acc, dy, n, a, b, wg, wu, wd, (dh_out, h_in, g), name, carry, unpad)
        if chain is not None:
            chain.feed(carried)
        acc = outs[0]
        dwg.append(outs[1])
        dwu.append(outs[2])
        dwd.append(outs[3])
    return (acc, dwg, dwu, dwd) + tuple(outs[4:])


def _win_fwd(h, g, w_in, carry=None):
    tp, d = h.shape
    ns = w_in.shape[0]
    tm = _pick_tile(tp, (768, 512, 256))
    ni = tp // tm

    def body(*refs):
        (h_ref, g_ref, w_ref, u_ref, p_ref), phases = split(refs)
        _run_phases(phases, carry, pl.program_id(0), ni)
        xhat, _ = _rms_stats(h_ref[...])
        u = (xhat * g_ref[...]).astype(BF16)
        u_ref[...] = u
        for k in range(ns):
            p_ref[k] = _dot(u, w_ref[k]).astype(BF16)

    in_specs = [pl.BlockSpec((tm, d), lambda i: (i, 0)),
                pl.BlockSpec((1, d), lambda i: (0, 0)),
                pl.BlockSpec((ns, d, d), lambda i: (0, 0, 0))]
    out_specs = [pl.BlockSpec((tm, d), lambda i: (i, 0)),
                 pl.BlockSpec((ns, tm, d), lambda i: (0, i, 0))]
    out_shape = [jax.ShapeDtypeStruct((tp, d), BF16), jax.ShapeDtypeStruct((ns, tp, d), BF16)]
    args, scratch = [h, g, w_in], []
    split = _attach_carry(carry, in_specs, args, out_specs, out_shape, scratch)
    return pl.pallas_call(
        body, name="win_fwd", grid=(ni,), in_specs=in_specs, out_specs=out_specs, out_shape=out_shape,
        scratch_shapes=scratch, compiler_params=_params(("arbitrary",)),
    )(*args)


def _win_bwd_shard(k, ns, du_prev, dpb, dug, u, w_in, h1, g, dh2):
    tp, d = h1.shape
    dh = d // 2
    tm = _pick_tile(tp, (768, 512, 256))
    first, last = k == 0, k == ns - 1

    def body(*refs):
        refs = list(refs)
        acc_in = None if first else refs.pop(0)
        dug_ref = refs.pop(0) if first else None
        dp_ref, u_ref, w_ref = refs[:3]
        refs = refs[3:]
        if last:
            h_ref, g_ref, dh2_ref, acc_out, dw_ref, dg_ref, dy_ref, dw_acc = refs
        else:
            acc_out, dw_ref, dw_acc = refs
        i = pl.program_id(0)

        @pl.when(i == 0)
        def _():
            dw_acc[...] = jnp.zeros_like(dw_acc)
            if last:
                dg_ref[...] = jnp.zeros_like(dg_ref)

        dp = dp_ref[...]
        if first:
            dp = jnp.concatenate([dug_ref[...], dp[:, dh:]], axis=1)
        dw_acc[...] += _dot_tn(u_ref[...], dp)
        du = _dot_nt(dp, w_ref[...])
        if not first:
            du = du + acc_in[...]
        if last:
            xhat, r = _rms_stats(h_ref[...])
            dg_ref[...] += jnp.sum(du * xhat, axis=0, keepdims=True)
            dh1 = dh2_ref[...] + _rms_bwd(xhat, r, g_ref[...], du)
            acc_out[...] = dh1
            dy_ref[...] = (0.5 * dh1).astype(BF16)
        else:
            acc_out[...] = du

        @pl.when(i == tp // tm - 1)
        def _():
            dw_ref[...] = dw_acc[...].astype(BF16)

    row_spec = pl.BlockSpec((tm, d), lambda i: (i, 0))
    vec_spec = pl.BlockSpec((1, d), lambda i: (0, 0))
    in_specs = [pl.BlockSpec((None, tm, d), lambda i: (k, i, 0)), row_spec,
                pl.BlockSpec((None, d, d), lambda i: (k, 0, 0))]
    args = [dpb, u, w_in]
    if first:
        in_specs.insert(0, pl.BlockSpec((tm, dh), lambda i: (i, 0)))
        args.insert(0, dug)
    else:
        in_specs.insert(0, row_spec)
        args.insert(0, du_prev)
    out_specs = [row_spec, pl.BlockSpec((d, d), lambda i: (0, 0))]
    out_shape = [jax.ShapeDtypeStruct((tp, d), F32), jax.ShapeDtypeStruct((d, d), BF16)]
    if last:
        in_specs += [row_spec, vec_spec, row_spec]
        args += [h1, g, dh2]
        out_specs += [vec_spec, row_spec]
        out_shape += [jax.ShapeDtypeStruct((1, d), F32), jax.ShapeDtypeStruct((tp, d), BF16)]
    return pl.pallas_call(
        body, name=f"win_bwd_{k}", grid=(tp // tm,), in_specs=in_specs, out_specs=out_specs,
        out_shape=out_shape, scratch_shapes=[pltpu.VMEM((d, d), F32)],
        compiler_params=_params(("arbitrary",)),
    )(*args)


def _win_bwd(dpb, dug, u, w_in, h1, g, dh2):
    ns = w_in.shape[0]
    acc, dws = None, []
    for k in range(ns):
        outs = _win_bwd_shard(k, ns, acc, dpb, dug, u, w_in, h1, g, dh2)
        acc = outs[0]
        dws.append(outs[1])
    return acc, dws, outs[2], outs[3]


def _cmul(ar, ai, br, bi):
    return ar * br - ai * bi, ar * bi + ai * br


def _scan_rows(j, sub):
    return pl.ds(j * SCAN_SEQS, SCAN_SEQS)


def _permute_rows(src_ref, dst_ref, sub):
    for j in range(sub):
        dst_ref[pl.ds(j * SCAN_SEQS, SCAN_SEQS), :] = src_ref[pl.ds(j, SCAN_SEQS, stride=sub), :]


def _unpermute_rows(src_ref, dst_ref, sub):
    for j in range(sub):
        dst_ref[pl.ds(j, SCAN_SEQS, stride=sub), :] = src_ref[pl.ds(j * SCAN_SEQS, SCAN_SEQS), :]


def _local_scan(x_ref, lr, li, w, sub, reverse):
    hr = jnp.zeros((SCAN_SEQS, w), F32)
    hi = jnp.zeros((SCAN_SEQS, w), F32)
    order = range(sub - 1, -1, -1) if reverse else range(sub)
    for j in order:
        xr = x_ref[_scan_rows(j, sub), pl.ds(0, w)]
        xi = x_ref[_scan_rows(j, sub), pl.ds(w, w)]
        if reverse:
            hr, hi = lr * hr + li * hi + xr, lr * hi - li * hr + xi
        else:
            hr, hi = lr * hr - li * hi + xr, lr * hi + li * hr + xi
        x_ref[_scan_rows(j, sub), pl.ds(0, w)] = hr
        x_ref[_scan_rows(j, sub), pl.ds(w, w)] = hi
    return hr, hi


def _entering_states(er, ei, fr, fi, pow_ref, w, sub, reverse):
    lane = lax.broadcasted_iota(jnp.int32, (SCAN_SEQS, w), 0)
    if reverse:
        edge, shift1 = SCAN_SEQS - 1, SCAN_SEQS - 1
    else:
        edge, shift1 = 0, 1
    zr = jnp.where(lane == edge, pltpu.roll(fr, shift1, 0), pltpu.roll(er, shift1, 0))
    zi = jnp.where(lane == edge, pltpu.roll(fi, shift1, 0), pltpu.roll(ei, shift1, 0))
    for m in range(SCAN_SEQS.bit_length() - 1):
        step, row = 1 << m, sub - 1 + m
        ar = pow_ref[pl.ds(row, 1), pl.ds(0, w)]
        ai = pow_ref[pl.ds(row, 1), pl.ds(w, w)]
        if reverse:
            ai = -ai
            keep = lane < SCAN_SEQS - step
            sr = jnp.where(keep, pltpu.roll(zr, SCAN_SEQS - step, 0), 0.0)
            si = jnp.where(keep, pltpu.roll(zi, SCAN_SEQS - step, 0), 0.0)
        else:
            keep = lane >= step
            sr = jnp.where(keep, pltpu.roll(zr, step, 0), 0.0)
            si = jnp.where(keep, pltpu.roll(zi, step, 0), 0.0)
        pr, pi = _cmul(ar, ai, sr, si)
        zr, zi = zr + pr, zi + pi
    ar = pow_ref[pl.ds(sub - 1, 1), pl.ds(0, w)]
    ai = pow_ref[pl.ds(sub - 1, 1), pl.ds(w, w)]
    if reverse:
        ai = -ai
    pr, pi = _cmul(ar, ai, zr, zi)
    return zr, zi, er + pr, ei + pi


def _scan_fwd(p, mb, mc, powt, dskip):
    _, tp, d = p.shape
    nb, cb, w2 = mb.shape
    w = w2 // 2
    q = SCAN_TILE
    sub = q // SCAN_SEQS
    nt = tp // q
    ds = d // 2

    def body(ug_ref, mb_ref, mc_ref, pow_ref, d_ref, y_ref, bnd_ref, x_scr, carry, nat, perm):
        t = pl.program_id(1)

        @pl.when(t == 0)
        def _():
            carry[...] = jnp.zeros_like(carry)

        ugf = ug_ref[...].astype(F32)
        nat[...] = ugf
        _permute_rows(nat, perm, sub)
        x_scr[...] = _dot(perm[...].astype(BF16), mb_ref[...])
        lr = jnp.broadcast_to(pow_ref[pl.ds(0, 1), pl.ds(0, w)], (SCAN_SEQS, w))
        li = jnp.broadcast_to(pow_ref[pl.ds(0, 1), pl.ds(w, w)], (SCAN_SEQS, w))
        er, ei = _local_scan(x_scr, lr, li, w, sub, False)
        zr, zi, fr, fi = _entering_states(er, ei, carry[:, pl.ds(0, w)], carry[:, pl.ds(w, w)],
                                          pow_ref, w, sub, False)
        carry[:, pl.ds(0, w)] = fr
        carry[:, pl.ds(w, w)] = fi
        bnd_ref[:, pl.ds(0, w)] = fr
        bnd_ref[:, pl.ds(w, w)] = fi
        for j in range(sub):
            pr = pow_ref[pl.ds(j, 1), pl.ds(0, w)]
            pi = pow_ref[pl.ds(j, 1), pl.ds(w, w)]
            cr, ci = _cmul(pr, pi, zr, zi)
            x_scr[_scan_rows(j, sub), pl.ds(0, w)] += cr
            x_scr[_scan_rows(j, sub), pl.ds(w, w)] += ci
        hb = x_scr[...].astype(BF16)
        perm[...] = _dot_nt(hb, mc_ref[...])
        _unpermute_rows(perm, nat, sub)
        y_ref[...] = nat[...] + d_ref[...] * ugf

    in_specs = [pl.BlockSpec((None, q, cb), lambda b, t: (0, t, b)),
                pl.BlockSpec((None, cb, w2), lambda b, t: (b, 0, 0)),
                pl.BlockSpec((None, cb, w2), lambda b, t: (b, 0, 0)),
                pl.BlockSpec((None, powt.shape[1], w2), lambda b, t: (b, 0, 0)),
                pl.BlockSpec((1, cb), lambda b, t: (0, b))]
    out_specs = [pl.BlockSpec((q, cb), lambda b, t: (t, b)),
                 pl.BlockSpec((None, None, SCAN_SEQS, w2), lambda b, t: (b, t, 0, 0))]
    out_shape = [jax.ShapeDtypeStruct((tp, ds), F32), jax.ShapeDtypeStruct((nb, nt, SCAN_SEQS, w2), F32)]
    scratch = [pltpu.VMEM((q, w2), F32), pltpu.VMEM((SCAN_SEQS, w2), F32),
               pltpu.VMEM((q, cb), F32), pltpu.VMEM((q, cb), F32)]
    return pl.pallas_call(
        body, name="s5_scan_fwd", grid=(nb, nt), in_specs=in_specs, out_specs=out_specs,
        out_shape=out_shape, scratch_shapes=scratch, compiler_params=_params(("arbitrary", "arbitrary")),
    )(p, mb, mc, powt, dskip)


def _scan_bwd(p, dy, mb, mc, powt, dskip, bnd):
    _, tp, d = p.shape
    nb, cb, w2 = mb.shape
    w = w2 // 2
    q = SCAN_TILE
    sub = q // SCAN_SEQS
    nt = tp // q
    ds = d // 2

    def body(ug_ref, dy_ref, mb_ref, mc_ref, pow_ref, d_ref, bnd_ref,
             dug_ref, dmb_ref, dmc_ref, dlam_ref, dd_ref, x_scr, y_scr, gcarry, nat, perm):
        t = pl.program_id(1)
        tt = nt - 1 - t

        @pl.when(t == 0)
        def _():
            gcarry[...] = jnp.zeros_like(gcarry)
            dmb_ref[...] = jnp.zeros_like(dmb_ref)
            dmc_ref[...] = jnp.zeros_like(dmc_ref)
            dlam_ref[...] = jnp.zeros_like(dlam_ref)
            dd_ref[...] = jnp.zeros_like(dd_ref)

        ugf = ug_ref[...].astype(F32)
        dyf = dy_ref[...].astype(F32)
        dd_ref[...] += jnp.sum((dyf * ugf).reshape(q // SUBLANES, SUBLANES, cb), axis=0)
        nat[...] = ugf
        _permute_rows(nat, perm, sub)
        ug = perm[...].astype(BF16)
        nat[...] = dyf
        _permute_rows(nat, perm, sub)
        dyb = perm[...].astype(BF16)
        lr = jnp.broadcast_to(pow_ref[pl.ds(0, 1), pl.ds(0, w)], (SCAN_SEQS, w))
        li = jnp.broadcast_to(pow_ref[pl.ds(0, 1), pl.ds(w, w)], (SCAN_SEQS, w))

        x_scr[...] = _dot(ug, mb_ref[...])
        er, ei = _local_scan(x_scr, lr, li, w, sub, False)
        first = tt == 0
        pfr = jnp.where(first, 0.0, bnd_ref[:, pl.ds(0, w)])
        pfi = jnp.where(first, 0.0, bnd_ref[:, pl.ds(w, w)])
        hzr, hzi, _, _ = _entering_states(er, ei, pfr, pfi, pow_ref, w, sub, False)
        for j in range(sub):
            pr = pow_ref[pl.ds(j, 1), pl.ds(0, w)]
            pi = pow_ref[pl.ds(j, 1), pl.ds(w, w)]
            cr, ci = _cmul(pr, pi, hzr, hzi)
            x_scr[_scan_rows(j, sub), pl.ds(0, w)] += cr
            x_scr[_scan_rows(j, sub), pl.ds(w, w)] += ci

        y_scr[...] = _dot(dyb, mc_ref[...])
        er, ei = _local_scan(y_scr, lr, li, w, sub, True)
        gzr, gzi, fr, fi = _entering_states(er, ei, gcarry[:, pl.ds(0, w)], gcarry[:, pl.ds(w, w)],
                                            pow_ref, w, sub, True)
        gcarry[:, pl.ds(0, w)] = fr
        gcarry[:, pl.ds(w, w)] = fi
        accr = jnp.zeros((SCAN_SEQS, w), F32)
        acci = jnp.zeros((SCAN_SEQS, w), F32)
        for j in range(sub):
            pr = pow_ref[pl.ds(sub - 1 - j, 1), pl.ds(0, w)]
            pi = pow_ref[pl.ds(sub - 1 - j, 1), pl.ds(w, w)]
            cr, ci = _cmul(pr, -pi, gzr, gzi)
            gr = y_scr[_scan_rows(j, sub), pl.ds(0, w)] + cr
            gi = y_scr[_scan_rows(j, sub), pl.ds(w, w)] + ci
            y_scr[_scan_rows(j, sub), pl.ds(0, w)] = gr
            y_scr[_scan_rows(j, sub), pl.ds(w, w)] = gi
            if j == 0:
                hpr, hpi = hzr, hzi
            else:
                hpr = x_scr[_scan_rows(j - 1, sub), pl.ds(0, w)]
                hpi = x_scr[_scan_rows(j - 1, sub), pl.ds(w, w)]
            accr += hpr * gr + hpi * gi
            acci += hpr * gi - hpi * gr
        dlam_ref[:, pl.ds(0, w)] += accr
        dlam_ref[:, pl.ds(w, w)] += acci

        hb = x_scr[...].astype(BF16)
        gb = y_scr[...].astype(BF16)
        dmc_ref[...] += _dot_tn(dyb, hb)
        dmb_ref[...] += _dot_tn(ug, gb)
        perm[...] = _dot_nt(gb, mb_ref[...])
        _unpermute_rows(perm, nat, sub)
        dug_ref[...] = (nat[...] + d_ref[...] * dyf).astype(BF16)

    blk = lambda b, t: (b, 0, 0)
    return pl.pallas_call(
        body, name="s5_scan_bwd", grid=(nb, nt),
        in_specs=[pl.BlockSpec((None, q, cb), lambda b, t: (0, nt - 1 - t, b)),
                  pl.BlockSpec((q, cb), lambda b, t: (nt - 1 - t, b)),
                  pl.BlockSpec((None, cb, w2), blk),
                  pl.BlockSpec((None, cb, w2), blk),
                  pl.BlockSpec((None, powt.shape[1], w2), blk),
                  pl.BlockSpec((1, cb), lambda b, t: (0, b)),
                  pl.BlockSpec((None, None, SCAN_SEQS, w2),
                               lambda b, t: (b, jnp.maximum(nt - 2 - t, 0), 0, 0))],
        out_specs=[pl.BlockSpec((q, cb), lambda b, t: (nt - 1 - t, b)),
                   pl.BlockSpec((None, cb, w2), blk),
                   pl.BlockSpec((None, cb, w2), blk),
                   pl.BlockSpec((None, SCAN_SEQS, w2), blk),
                   pl.BlockSpec((SUBLANES, cb), lambda b, t: (0, b))],
        out_shape=[jax.ShapeDtypeStruct((tp, ds), BF16),
                   jax.ShapeDtypeStruct((nb, cb, w2), F32),
                   jax.ShapeDtypeStruct((nb, cb, w2), F32),
                   jax.ShapeDtypeStruct((nb, SCAN_SEQS, w2), F32),
                   jax.ShapeDtypeStruct((SUBLANES, ds), F32)],
        scratch_shapes=[pltpu.VMEM((q, w2), F32), pltpu.VMEM((q, w2), F32),
                        pltpu.VMEM((SCAN_SEQS, w2), F32), pltpu.VMEM((q, cb), F32), pltpu.VMEM((q, cb), F32)],
        compiler_params=_params(("arbitrary", "arbitrary")),
    )(p, dy, mb, mc, powt, dskip, bnd)


HALO = 16


def _mix_tile(ys5, p0, p1, p2, p3, prev_cin, cw, bgate, wglu, wco, d):
    dh = d // 2
    tm = ys5.shape[0]
    v = p0[:, dh:].astype(F32)
    gbr = p1[:, :dh].astype(F32)
    gcr = p1[:, dh:].astype(F32)
    gact = _gelu(ys5).astype(BF16)
    z = _dot(gact, wglu)
    z1, z2 = z[:, :d], z[:, d:]
    sg = _sigmoid(z2)
    y_ssm = z1 * sg
    cin = gcr * v
    ext = jnp.concatenate([cin, prev_cin], axis=0)
    r1 = pltpu.roll(ext, 1, 0)[:tm]
    r2 = pltpu.roll(ext, 2, 0)[:tm]
    cv = cw[2] * cin + cw[1] * r1 + cw[0] * r2
    cg = (gbr * cv).astype(BF16)
    y_conv = _dot(cg, wco)
    g_s = _sigmoid(p2.astype(F32) + bgate[:, :d])
    g_c = _sigmoid(p3.astype(F32) + bgate[:, d:])
    mixed = g_s * y_ssm + g_c * y_conv
    return dict(v=v, gb=gbr, gc=gcr, gact=gact, z1=z1, sg=sg, y_ssm=y_ssm, cin=cin, r1=r1, r2=r2,
                cv=cv, cg=cg, y_conv=y_conv, g_s=g_s, g_c=g_c, mixed=mixed)


def _mix_fwd(h1, ys5, p, cw, bgate, wglu, wco, wo, carry=None):
    tp, d = h1.shape
    dh = d // 2
    tm = ROW_ALIGN
    ni = tp // tm

    def body(*refs):
        refs, phases = split(refs)
        (h_ref, y_ref, p0_ref, p1_ref, p2_ref, p3_ref, cw_ref, bg_ref, wglu_ref, wco_ref, wo_ref,
         o_ref, prev) = refs
        _run_phases(phases, carry, pl.program_id(0), ni)

        @pl.when(pl.program_id(0) == 0)
        def _():
            prev[...] = jnp.zeros_like(prev)

        cw = [cw_ref[pl.ds(t, 1), :] for t in range(3)]
        f = _mix_tile(y_ref[...], p0_ref[...], p1_ref[...], p2_ref[...], p3_ref[...], prev[...],
                      cw, bg_ref[...], wglu_ref[...], wco_ref[...], d)
        prev[...] = f["cin"][tm - HALO:, :]
        o_ref[...] = h_ref[...] + _dot(f["mixed"].astype(BF16), wo_ref[...])

    row = pl.BlockSpec((tm, d), lambda i: (i, 0))
    full = lambda a: pl.BlockSpec(a.shape, lambda i: (0,) * a.ndim)
    pk = lambda k: pl.BlockSpec((None, tm, d), lambda i, k=k: (k, i, 0))
    in_specs = [row, pl.BlockSpec((tm, dh), lambda i: (i, 0)), pk(0), pk(1), pk(2), pk(3),
                full(cw), full(bgate), full(wglu), full(wco), full(wo)]
    out_specs, out_shape = [row], [jax.ShapeDtypeStruct((tp, d), F32)]
    args, scratch = [h1, ys5, p, p, p, p, cw, bgate, wglu, wco, wo], [pltpu.VMEM((HALO, dh), F32)]
    split = _attach_carry(carry, in_specs, args, out_specs, out_shape, scratch)
    return pl.pallas_call(
        body, name="mix_fwd", grid=(ni,), in_specs=in_specs, out_specs=out_specs, out_shape=out_shape,
        scratch_shapes=scratch, compiler_params=_params(("arbitrary",)),
    )(*args)


def _mix_bwd(dh2, ys5, p, cw, bgate, wglu, wco, wo):
    tp, d = dh2.shape
    dh = d // 2
    tm = ROW_ALIGN
    ni = tp // tm
    hb = tm // HALO

    def body(dh_ref, y_ref, p0_ref, p1_ref, p2_ref, p3_ref, h0_ref, h1_ref,
             cw_ref, bg_ref, wglu_ref, wco_ref, wo_ref,
             dys_ref, dpb_ref, dwo_ref, dwglu_ref, dwco_ref, dcw_ref, dbg_ref, nxt):
        i = pl.program_id(0)
        tt = ni - 1 - i

        @pl.when(i == 0)
        def _():
            nxt[...] = jnp.zeros_like(nxt)
            dwo_ref[...] = jnp.zeros_like(dwo_ref)
            dwglu_ref[...] = jnp.zeros_like(dwglu_ref)
            dwco_ref[...] = jnp.zeros_like(dwco_ref)
            dcw_ref[...] = jnp.zeros_like(dcw_ref)
            dbg_ref[...] = jnp.zeros_like(dbg_ref)

        cw = [cw_ref[pl.ds(t, 1), :] for t in range(3)]
        prev_cin = h1_ref[:, dh:].astype(F32) * h0_ref[:, dh:].astype(F32)
        prev_cin = jnp.where(tt == 0, 0.0, prev_cin)
        ys5 = y_ref[...]
        f = _mix_tile(ys5, p0_ref[...], p1_ref[...], p2_ref[...], p3_ref[...], prev_cin,
                      cw, bg_ref[...], wglu_ref[...], wco_ref[...], d)
        dhb = dh_ref[...].astype(BF16)
        dmixed = _dot_nt(dhb, wo_ref[...])
        dwo_ref[...] += _dot_tn(f["mixed"].astype(BF16), dhb)

        g_s, g_c, sg = f["g_s"], f["g_c"], f["sg"]
        dy_ssm = dmixed * g_s
        dy_conv = dmixed * g_c
        dp2 = dmixed * f["y_ssm"] * g_s * (1.0 - g_s)
        dp3 = dmixed * f["y_conv"] * g_c * (1.0 - g_c)
        dbg_ref[:, pl.ds(0, d)] += jnp.sum(dp2, axis=0, keepdims=True)
        dbg_ref[:, pl.ds(d, d)] += jnp.sum(dp3, axis=0, keepdims=True)

        dz = jnp.concatenate([dy_ssm * sg, dy_ssm * f["z1"] * sg * (1.0 - sg)], axis=1).astype(BF16)
        dwglu_ref[...] += _dot_tn(f["gact"], dz)
        dys_ref[...] = (_dot_nt(dz, wglu_ref[...]) * _gelu_grad(ys5)).astype(BF16)

        dycb = dy_conv.astype(BF16)
        dwco_ref[...] += _dot_tn(f["cg"], dycb)
        dcg = _dot_nt(dycb, wco_ref[...])
        dgb = dcg * f["cv"]
        dcv = dcg * f["gb"]
        ext = jnp.concatenate([dcv, nxt[...]], axis=0)
        n1 = pltpu.roll(ext, tm + HALO - 1, 0)[:tm]
        n2 = pltpu.roll(ext, tm + HALO - 2, 0)[:tm]
        nxt[...] = dcv[:HALO, :]
        dcin = cw[2] * dcv + cw[1] * n1 + cw[0] * n2
        dcw_ref[pl.ds(0, 1), :] += jnp.sum(dcv * f["r2"], axis=0, keepdims=True)
        dcw_ref[pl.ds(1, 1), :] += jnp.sum(dcv * f["r1"], axis=0, keepdims=True)
        dcw_ref[pl.ds(2, 1), :] += jnp.sum(dcv * f["cin"], axis=0, keepdims=True)
        dgc = dcin * f["v"]
        dv = dcin * f["gc"]
        dpb_ref[0] = jnp.concatenate([jnp.zeros_like(dv), dv], axis=1).astype(BF16)
        dpb_ref[1] = jnp.concatenate([dgb, dgc], axis=1).astype(BF16)
        dpb_ref[2] = dp2.astype(BF16)
        dpb_ref[3] = dp3.astype(BF16)

    rev = lambda i: ni - 1 - i
    row = pl.BlockSpec((tm, d), lambda i: (rev(i), 0))
    half = pl.BlockSpec((tm, dh), lambda i: (rev(i), 0))
    full = lambda a: pl.BlockSpec(a.shape, lambda i: (0,) * a.ndim)
    pk = lambda k: pl.BlockSpec((None, tm, d), lambda i, k=k: (k, rev(i), 0))
    halo = lambda k: pl.BlockSpec((None, HALO, d), lambda i, k=k: (k, jnp.maximum(rev(i) * hb - 1, 0), 0))
    acc = lambda shape: pl.BlockSpec(shape, lambda i: (0,) * len(shape))
    return pl.pallas_call(
        body, name="mix_bwd", grid=(ni,),
        in_specs=[row, half, pk(0), pk(1), pk(2), pk(3), halo(0), halo(1),
                  full(cw), full(bgate), full(wglu), full(wco), full(wo)],
        out_specs=[half, pl.BlockSpec((4, tm, d), lambda i: (0, rev(i), 0)),
                   acc((d, d)), acc((dh, 2 * d)), acc((dh, d)), acc((SUBLANES, dh)), acc((1, 2 * d))],
        out_shape=[jax.ShapeDtypeStruct((tp, dh), BF16), jax.ShapeDtypeStruct((4, tp, d), BF16),
                   jax.ShapeDtypeStruct((d, d), F32), jax.ShapeDtypeStruct((dh, 2 * d), F32),
                   jax.ShapeDtypeStruct((dh, d), F32), jax.ShapeDtypeStruct((SUBLANES, dh), F32),
                   jax.ShapeDtypeStruct((1, 2 * d), F32)],
        scratch_shapes=[pltpu.VMEM((HALO, dh), F32)],
        compiler_params=_params(("arbitrary",)),
    )(dh2, ys5, p, p, p, p, p, p, cw, bgate, wglu, wco, wo)


ANY = pl.BlockSpec(memory_space=pl.ANY)


def _position():
    return lax.axis_index("x"), lax.axis_index("y"), lax.axis_index("c")


def _remote(src, dst, ssem, rsem, dev):
    return pltpu.make_async_remote_copy(src_ref=src, dst_ref=dst, send_sem=ssem, recv_sem=rsem,
                                        device_id=dev, device_id_type=MESH)


def _cast_pieces(ws, pos):
    n = len(ws)

    def body(pos_ref, *refs):
        for w_ref, o_ref in zip(refs[:n], refs[n:]):
            o_ref[...] = w_ref[...].astype(BF16)

    halves = [(w.shape[0] // 2, w.shape[1]) for w in ws]
    return pl.pallas_call(
        body, name="cast_pieces",
        grid_spec=pltpu.PrefetchScalarGridSpec(
            num_scalar_prefetch=1, grid=(1,),
            in_specs=[pl.BlockSpec(hs, lambda i, pos: (pos[2], 0)) for hs in halves],
            out_specs=[pl.BlockSpec(hs, lambda i, pos: (0, 0)) for hs in halves]),
        out_shape=[jax.ShapeDtypeStruct(hs, BF16) for hs in halves],
        compiler_params=_params(("arbitrary",)),
    )(pos, *ws)


class _Carry:
    def __init__(self, name, arrays, out_shapes, nsem, nlsem, make, fracs):
        self.name, self.arrays, self.out_shapes = name, list(arrays), list(out_shapes)
        self.nsem, self.nlsem, self.make, self.fracs = nsem, max(nlsem, 1), make, fracs


def _carry_scratch(carry):
    return [pltpu.SemaphoreType.DMA((carry.nsem,)), pltpu.SemaphoreType.DMA((carry.nsem,)),
            pltpu.SemaphoreType.DMA((carry.nlsem,))]


def _run_carry(carry):
    na, no = len(carry.arrays), len(carry.out_shapes)

    def body(*refs):
        for phase in carry.make(refs[:na], refs[na:na + no], *refs[na + no:]):
            phase()

    return pl.pallas_call(
        body, name=carry.name, in_specs=[ANY] * na, out_specs=[ANY] * no, out_shape=carry.out_shapes,
        scratch_shapes=_carry_scratch(carry),
    )(*carry.arrays)


def _attach_carry(carry, in_specs, args, out_specs, out_shape, scratch):
    nhi, nho, nhs = len(in_specs), len(out_specs), len(scratch)
    if carry is None:
        return lambda refs: (list(refs), [])
    na, no = len(carry.arrays), len(carry.out_shapes)
    in_specs += [ANY] * na
    args += carry.arrays
    out_specs += [ANY] * no
    out_shape += carry.out_shapes
    scratch += _carry_scratch(carry)

    def split(refs):
        refs = list(refs)
        o = nhi + na
        host = refs[:nhi] + refs[o:o + nho] + refs[o + nho + no:o + nho + no + nhs]
        sems = refs[o + nho + no + nhs:]
        return host, carry.make(refs[nhi:o], refs[o + nho:o + nho + no], *sems)

    return split


def _run_phases(phases, carry, step, total):
    for phase, frac in zip(phases, carry.fracs if carry is not None else ()):
        pl.when(step == int(round(frac * (total - 1))))(phase)


def _allgather_carry(name, pieces, smalls):
    n, ns = len(pieces), len(smalls)
    per = 14
    n_big = per * n

    def make(ins, outs, ssem, rsem, lsem):
        pin, sin = ins[:n], ins[n:]
        wall, sall = outs[:n], outs[n:]
        x, y, c = _position()
        xnb, ynb, sib = (1 - x, y, c), (x, 1 - y, c), (x, y, 1 - c)
        chips = [(1 - x, y), (x, 1 - y), (1 - x, 1 - y)]
        r4 = [p.shape[0] // 2 for p in pin]
        own = lambda i, h: pin[i].at[pl.ds(h * r4[i], r4[i]), :]
        slot = lambda i, xx, yy, cc, h: wall[i].at[xx, yy, cc, h]
        cp = lambda src, dst, s, dev: _remote(src, dst, ssem.at[s], rsem.at[s], dev)
        to_sib = lambda i, xx, yy, h: cp(slot(i, xx, yy, c, h), slot(i, xx, yy, c, h),
                                         per * i + 6 + 4 * xx + 2 * yy + h, sib)

        def local():
            cps = [pltpu.make_async_copy(own(i, h), slot(i, x, y, c, h), lsem.at[2 * i + h])
                   for i in range(n) for h in range(2)]
            return cps + [pltpu.make_async_copy(sin[i], sall[i].at[2 * x + y], lsem.at[2 * n + i])
                          for i in range(ns)]

        def small(px, py, j, i, landing):
            s = n_big + j * ns + i
            return cp(sin[i], sall[i].at[landing], s, (px, py, c))

        def first_hop():
            for lc in local():
                lc.start()
            for j, (px, py) in enumerate(chips):
                for i in range(ns):
                    small(px, py, j, i, 2 * x + y).start()
            for i in range(n):
                cp(own(i, 0), slot(i, x, y, c, 0), per * i, xnb).start()
                cp(own(i, 1), slot(i, x, y, c, 1), per * i + 1, ynb).start()
                for h in range(2):
                    cp(own(i, h), slot(i, x, y, c, h), per * i + 6 + 4 * x + 2 * y + h, sib).start()

        def second_hop():
            for lc in local():
                lc.wait()
            for i in range(n):
                cp(slot(i, 1 - x, y, c, 0), slot(i, 1 - x, y, c, 0), per * i, xnb).wait_recv()
                cp(slot(i, x, 1 - y, c, 1), slot(i, x, 1 - y, c, 1), per * i + 1, ynb).wait_recv()
                for j in range(2):
                    cp(slot(i, j, y, c, 0), slot(i, j, y, c, 0), per * i + 2 + j, ynb).start()
                    cp(slot(i, x, j, c, 1), slot(i, x, j, c, 1), per * i + 4 + j, xnb).start()
                to_sib(i, 1 - x, y, 0).start()
                to_sib(i, x, 1 - y, 1).start()

        def last_to_sibling():
            for i in range(n):
                for j in range(2):
                    cp(slot(i, j, 1 - y, c, 0), slot(i, j, 1 - y, c, 0), per * i + 2 + j, ynb).wait_recv()
                    cp(slot(i, 1 - x, j, c, 1), slot(i, 1 - x, j, c, 1), per * i + 4 + j, xnb).wait_recv()
                    to_sib(i, j, 1 - y, 0).start()
                    to_sib(i, 1 - x, j, 1).start()

        def finish():
            for i in range(n):
                for xx in range(2):
                    for yy in range(2):
                        for h in range(2):
                            s = per * i + 6 + 4 * xx + 2 * yy + h
                            cp(slot(i, xx, yy, 1 - c, h), slot(i, xx, yy, 1 - c, h), s, sib).wait_recv()
                            to_sib(i, xx, yy, h).wait_send()
                cp(own(i, 0), slot(i, x, y, c, 0), per * i, xnb).wait_send()
                cp(own(i, 1), slot(i, x, y, c, 1), per * i + 1, ynb).wait_send()
                for j in range(2):
                    cp(slot(i, j, y, c, 0), slot(i, j, y, c, 0), per * i + 2 + j, ynb).wait_send()
                    cp(slot(i, x, j, c, 1), slot(i, x, j, c, 1), per * i + 4 + j, xnb).wait_send()
            for j, (px, py) in enumerate(chips):
                for i in range(ns):
                    small(px, py, j, i, 2 * px + py).wait_recv()
                    small(px, py, j, i, 2 * x + y).wait_send()

        return [first_hop, second_hop, last_to_sibling, finish]

    out_shapes = [jax.ShapeDtypeStruct((2, 2, 2, 2, a.shape[0] // 2, a.shape[1]), a.dtype) for a in pieces]
    out_shapes += [jax.ShapeDtypeStruct((4,) + a.shape, a.dtype) for a in smalls]
    return _Carry(name, list(pieces) + list(smalls), out_shapes, n_big + 3 * ns, 2 * n + ns, make,
                  (0.0, 0.23, 0.73, 1.0))


def _exchange_carry(name, arrays, out_shapes, plan):
    count = plan([None] * len(arrays), [None] * len(out_shapes), None)

    def make(ins, outs, ssem, rsem, lsem):
        def copies():
            return [_remote(src, dst, ssem.at[j], rsem.at[j], peer)
                    for j, (src, dst, peer) in enumerate(plan(ins, outs, _position()))]

        def start():
            for c in copies():
                c.start()

        def wait():
            for c in copies():
                c.wait()

        return [start, wait]

    return _Carry(name, arrays, out_shapes, count, 0, make, (0.0, 1.0))


class _Grad:
    def __init__(self, arrs, kind, shard_shape):
        self.arrs, self.kind = list(arrs), kind
        self.rows, self.cols = shard_shape
        self.r2 = self.rows // 2

    def view(self, refs, k, h):
        r2 = self.r2
        if self.kind == "list":
            return refs[k].at[pl.ds(h * r2, r2), :]
        if self.kind == "stacked":
            return refs[0].at[k, pl.ds(h * r2, r2), :]
        if self.kind == "col":
            return refs[0].at[pl.ds(h * r2, r2), pl.ds(k * self.cols, self.cols)]
        return refs[0].at[pl.ds((2 * k + h) * r2, r2), :]

    def add_half(self, recv, pos):
        r2, cols = self.r2, self.cols
        n_in = len(self.arrs)
        tr = _row_tile(r2, cols)
        nt = r2 // tr

        def body(pos_ref, *refs):
            m_refs, (r_ref, of_ref, ob_ref) = refs[:n_in], refs[n_in:]
            mine = m_refs[0][...]
            for kk in range(1, n_in):
                mine = jnp.where(pl.program_id(1) == kk, m_refs[kk][...], mine)
            s = mine.astype(F32) + r_ref[...].astype(F32)
            of_ref[...] = s
            ob_ref[...] = s.astype(BF16)

        row = lambda t, pos: pos[2] * nt + t
        if self.kind == "list":
            specs = [pl.BlockSpec((tr, cols), lambda t, k, pos: (row(t, pos), 0))] * n_in
        elif self.kind == "stacked":
            specs = [pl.BlockSpec((None, tr, cols), lambda t, k, pos: (k, row(t, pos), 0))]
        elif self.kind == "col":
            specs = [pl.BlockSpec((tr, cols), lambda t, k, pos: (row(t, pos), k))]
        else:
            specs = [pl.BlockSpec((tr, cols), lambda t, k, pos: (2 * k * nt + row(t, pos), 0))]
        blk = pl.BlockSpec((None, tr, cols), lambda t, k, pos: (k, t, 0))
        return pl.pallas_call(
            body, name="rs_add_c",
            grid_spec=pltpu.PrefetchScalarGridSpec(
                num_scalar_prefetch=1, grid=(nt, 4), in_specs=specs + [blk], out_specs=[blk, blk]),
            out_shape=[jax.ShapeDtypeStruct((4, r2, cols), F32), jax.ShapeDtypeStruct((4, r2, cols), BF16)],
            compiler_params=_params(("arbitrary", "arbitrary")),
        )(pos, *self.arrs, recv)


def _row_tile(rows, cols):
    fits = [t for t in range(16, rows + 1, 16) if rows % t == 0 and t * cols * 4 <= 2 * 1024 * 1024]
    return max(fits) if fits else rows


def _adamw_math(w, g, m, v):
    m = ADAM_B1 * m + (1.0 - ADAM_B1) * g
    v = ADAM_B2 * v + (1.0 - ADAM_B2) * (g * g)
    m_hat = m / (1.0 - ADAM_B1 ** ADAM_STEP)
    v_hat = v / (1.0 - ADAM_B2 ** ADAM_STEP)
    delta = -ADAM_LR * (m_hat / (jnp.sqrt(v_hat) + ADAM_EPS) + ADAM_WD * w)
    return delta, m, v


def _adamw_big(w, m, v, own, sib, pos):
    rows, cols = w.shape
    r2 = rows // 2

    tr = _row_tile(r2, cols)
    nt = r2 // tr

    def body(pos_ref, w_ref, m_ref, v_ref, own_ref, sib_ref, g_ref, d_ref, nm_ref, nv_ref):
        h = pl.program_id(0)
        g = jnp.where(h == pos_ref[2], own_ref[...], sib_ref[...])
        g_ref[...] = g
        d_ref[...], nm_ref[...], nv_ref[...] = _adamw_math(w_ref[...], g, m_ref[...], v_ref[...])

    half = pl.BlockSpec((tr, cols), lambda h, t, pos: (h * nt + t, 0))
    piece = pl.BlockSpec((tr, cols), lambda h, t, pos: (t, 0))
    out = jax.ShapeDtypeStruct((rows, cols), F32)
    return pl.pallas_call(
        body, name="adamw",
        grid_spec=pltpu.PrefetchScalarGridSpec(
            num_scalar_prefetch=1, grid=(2, nt),
            in_specs=[half, half, half, piece, piece],
            out_specs=[half, half, half, half]),
        out_shape=[out, out, out, out],
        compiler_params=_params(("arbitrary", "arbitrary")),
    )(pos, w, m, v, own, sib)


def _add_hop1(s1fs, recvs, pos):
    n = len(s1fs)
    s1vs = [s.reshape((4, 2) + r.shape[2:]) for s, r in zip(s1fs, recvs)]

    def body(pos_ref, *refs):
        for m_ref, r_ref, of_ref, ob_ref in zip(refs[:n], refs[n:2 * n], refs[2 * n:3 * n], refs[3 * n:]):
            s = m_ref[...] + r_ref[...].astype(F32)
            of_ref[...] = s
            ob_ref[...] = s.astype(BF16)

    def mine(h, j, pos):
        return (jnp.where(h == 0, 2 * j + pos[1], 2 * pos[0] + j), h, 0, 0)

    tile = lambda r: (None, None) + r.shape[2:]
    blks = [pl.BlockSpec(tile(r), lambda h, j, pos: (h, j, 0, 0)) for r in recvs]
    outs = pl.pallas_call(
        body, name="rs_add_1",
        grid_spec=pltpu.PrefetchScalarGridSpec(
            num_scalar_prefetch=1, grid=(2, 2),
            in_specs=[pl.BlockSpec(tile(r), mine) for r in recvs] + blks, out_specs=blks + blks),
        out_shape=[jax.ShapeDtypeStruct(r.shape, F32) for r in recvs]
        + [jax.ShapeDtypeStruct(r.shape, BF16) for r in recvs],
        compiler_params=_params(("arbitrary", "arbitrary")),
    )(pos, *s1vs, *recvs)
    return list(zip(outs[:n], outs[n:]))


def _own_sum(s2fs, recvs, pos):
    n = len(s2fs)

    def body(pos_ref, *refs):
        for s_ref, r_ref, o_ref in zip(refs[:n], refs[n:2 * n], refs[2 * n:]):
            o_ref[...] = s_ref[...] + r_ref[...].astype(F32)

    blks = [pl.BlockSpec((None,) + r.shape[1:], lambda h, pos: (h, 0, 0)) for r in recvs]
    return pl.pallas_call(
        body, name="own_sum",
        grid_spec=pltpu.PrefetchScalarGridSpec(
            num_scalar_prefetch=1, grid=(2,),
            in_specs=[pl.BlockSpec((None, None) + r.shape[1:],
                                   lambda h, pos: (h, jnp.where(h == 0, pos[0], pos[1]), 0, 0)) for r in recvs]
            + blks, out_specs=blks),
        out_shape=[jax.ShapeDtypeStruct(r.shape, F32) for r in recvs],
        compiler_params=_params(("arbitrary",)),
    )(pos, *s2fs, *recvs)


def _allreduce_small(buf):
    def body(x_ref, o_ref, recv, ssem, rsem):
        x, y, c = _position()
        o_ref[...] = x_ref[...]
        for s, peer in enumerate([(x, y, 1 - c), (x, 1 - y, c), (1 - x, y, c)]):
            cp = _remote(o_ref, recv.at[s], ssem.at[s], rsem.at[s], peer)
            cp.start()
            cp.wait()
            o_ref[...] = o_ref[...] + recv[s]

    vm = pl.BlockSpec(memory_space=pltpu.VMEM)
    return pl.pallas_call(
        body, name="allreduce_small", in_specs=[vm], out_specs=vm,
        out_shape=jax.ShapeDtypeStruct(buf.shape, F32),
        scratch_shapes=[pltpu.VMEM((3,) + buf.shape, F32),
                        pltpu.SemaphoreType.DMA((3,)), pltpu.SemaphoreType.DMA((3,))],
    )(buf)


def _adamw_small(w, g, m, v):
    def body(w_ref, g_ref, m_ref, v_ref, d_ref, nm_ref, nv_ref):
        d_ref[...], nm_ref[...], nv_ref[...] = _adamw_math(w_ref[...], g_ref[...], m_ref[...], v_ref[...])

    vm = pl.BlockSpec(memory_space=pltpu.VMEM)
    out = jax.ShapeDtypeStruct(w.shape, F32)
    return pl.pallas_call(body, name="adamw_small", in_specs=[vm] * 4, out_specs=[vm] * 3,
                          out_shape=[out, out, out])(w, g, m, v)


class _ReduceScatter:
    def __init__(self, tag, grads, pos):
        self.tag, self.grads, self.pos, self.stage = tag, grads, pos, 0

    def carry(self):
        grads, n = self.grads, len(self.grads)
        r4 = [g.r2 // 2 for g in grads]

        first = [sum(len(g.arrs) for g in grads[:i]) for i in range(n)]

        def plan_c(ins, outs, p):
            if p is None:
                return 4 * n
            x, y, c = p
            mine = lambda i: ins[first[i]:first[i] + len(grads[i].arrs)]
            return [(grads[i].view(mine(i), k, 1 - c), outs[i].at[k], (x, y, 1 - c))
                    for i in range(n) for k in range(4)]

        def plan_1(ins, outs, p):
            if p is None:
                return 4 * n
            x, y, c = p
            copies = []
            for i in range(n):
                for j in range(2):
                    copies.append((ins[i].at[2 * j + (1 - y), pl.ds(0, r4[i]), :], outs[i].at[0, j],
                                   (x, 1 - y, c)))
                    copies.append((ins[i].at[2 * (1 - x) + j, pl.ds(r4[i], r4[i]), :], outs[i].at[1, j],
                                   (1 - x, y, c)))
            return copies

        def plan_2(ins, outs, p):
            if p is None:
                return 2 * n
            x, y, c = p
            copies = []
            for i in range(n):
                copies.append((ins[i].at[0, 1 - x], outs[i].at[0], (1 - x, y, c)))
                copies.append((ins[i].at[1, 1 - y], outs[i].at[1], (x, 1 - y, c)))
            return copies

        def plan_s(ins, outs, p):
            if p is None:
                return n
            x, y, c = p
            return [(ins[i], outs[i], (x, y, 1 - c)) for i in range(n)]

        shape = lambda lead, dt: [jax.ShapeDtypeStruct(lead(g) + (g.cols,), dt) for g in grads]
        stage = self.stage
        if stage == 0:
            return _exchange_carry(f"rs_{self.tag}_exchange_c", [a for g in grads for a in g.arrs],
                                   [jax.ShapeDtypeStruct((4, g.r2, g.cols), g.arrs[0].dtype) for g in grads],
                                   plan_c)
        if stage == 1:
            return _exchange_carry(f"rs_{self.tag}_exchange_1", [s[1] for s in self.s1],
                                   shape(lambda g: (2, 2, g.r2 // 2), BF16), plan_1)
        if stage == 2:
            return _exchange_carry(f"rs_{self.tag}_exchange_2", [s[1] for s in self.s2],
                                   shape(lambda g: (2, g.r2 // 2), BF16), plan_2)
        return _exchange_carry(f"rs_{self.tag}_exchange_sibling", self.own, shape(lambda g: (g.r2,), F32), plan_s)

    def feed(self, recv):
        grads, pos = self.grads, self.pos
        if self.stage == 0:
            self.s1 = [g.add_half(r, pos) for g, r in zip(grads, recv)]
        elif self.stage == 1:
            self.s2 = _add_hop1([s[0] for s in self.s1], list(recv), pos)
        elif self.stage == 2:
            own = _own_sum([s[0] for s in self.s2], list(recv), pos)
            self.own = [o.reshape(g.r2, g.cols) for g, o in zip(grads, own)]
        else:
            self.sib = list(recv)
        self.stage += 1

    def run(self):
        while self.stage < 4:
            self.feed(_run_carry(self.carry()))

    def adamw(self, weights):
        return [_adamw_big(w, m, v, o, sb, self.pos) for (w, m, v), o, sb in zip(weights, self.own, self.sib)]


def _block_diag(t, nb):
    g, c, p = t.shape
    gb = g // nb
    t = t.reshape(nb, gb, c, p)
    eye = jnp.eye(gb, dtype=t.dtype)
    return jnp.einsum("bgcp,gh->bgchp", t, eye).reshape(nb, gb * c, gb * p)


def _s5_discretise(a_re, a_im, log_dt, b_re, b_im, c_re, c_im):
    g, p = a_re.shape
    nb = g // GROUPS_PER_BLOCK
    dt = jnp.exp(log_dt)[:, None]
    mag = jnp.exp(a_re * dt)
    lam_re = mag * jnp.cos(a_im * dt)
    lam_im = mag * jnp.sin(a_im * dt)
    den = a_re * a_re + a_im * a_im
    q_re = ((lam_re - 1.0) * a_re + lam_im * a_im) / den
    q_im = (lam_im * a_re - (lam_re - 1.0) * a_im) / den
    bb_re = q_re[..., None] * b_re - q_im[..., None] * b_im
    bb_im = q_re[..., None] * b_im + q_im[..., None] * b_re
    tr = lambda t: jnp.swapaxes(t, 1, 2)
    mb = jnp.concatenate([_block_diag(tr(bb_re), nb), _block_diag(tr(bb_im), nb)], axis=-1)
    mc = jnp.concatenate([_block_diag(c_re, nb), -_block_diag(c_im, nb)], axis=-1)
    lam = jnp.concatenate([lam_re.reshape(nb, -1), lam_im.reshape(nb, -1)], axis=-1)
    return mb, mc, lam


def _s5_powers(a_re, a_im, log_dt, sub):
    g, p = a_re.shape
    nb = g // GROUPS_PER_BLOCK
    dt = jnp.exp(log_dt)[:, None]
    ns = list(range(1, sub + 1)) + [sub << m for m in range(1, SCAN_SEQS.bit_length() - 1)]
    ns += [0] * (-len(ns) % SUBLANES)
    e = jnp.asarray(ns, F32)[:, None, None]
    mag = jnp.exp(a_re[None] * dt[None] * e)
    ang = a_im[None] * dt[None] * e
    re = (mag * jnp.cos(ang)).reshape(len(ns), nb, -1)
    im = (mag * jnp.sin(ang)).reshape(len(ns), nb, -1)
    return jnp.transpose(jnp.concatenate([re, im], axis=-1), (1, 0, 2))


def _pack(parts):
    flat = jnp.concatenate([a.reshape(-1).astype(F32) for a in parts])
    n = flat.shape[0]
    pad = -n % (SUBLANES * LANES)
    return jnp.pad(flat, (0, pad)).reshape(-1, LANES)


def _unpack(buf, like):
    flat = buf.reshape(-1)
    out, o = [], 0
    for a in like:
        out.append(flat[o:o + a.size].reshape(a.shape))
        o += a.size
    return out


def kernel(x, meta_tokens, g_ffn1, ffn1_w_gate, ffn1_w_up, ffn1_w_down, g_mix, w_in, b_gate, ssm_a_re, ssm_a_im, ssm_log_dt, ssm_b_re, ssm_b_im, ssm_c_re, ssm_c_im, ssm_d, ssm_w_glu, conv_w, conv_w_out, w_o, g_ffn2, ffn2_w_gate, ffn2_w_up, ffn2_w_down, g_final, loss_target, m_meta_tokens, m_g_ffn1, m_ffn1_w_gate, m_ffn1_w_up, m_ffn1_w_down, m_g_mix, m_w_in, m_b_gate, m_ssm_a_re, m_ssm_a_im, m_ssm_log_dt, m_ssm_b_re, m_ssm_b_im, m_ssm_c_re, m_ssm_c_im, m_ssm_d, m_ssm_w_glu, m_conv_w, m_conv_w_out, m_w_o, m_g_ffn2, m_ffn2_w_gate, m_ffn2_w_up, m_ffn2_w_down, m_g_final, v_meta_tokens, v_g_ffn1, v_ffn1_w_gate, v_ffn1_w_up, v_ffn1_w_down, v_g_mix, v_w_in, v_b_gate, v_ssm_a_re, v_ssm_a_im, v_ssm_log_dt, v_ssm_b_re, v_ssm_b_im, v_ssm_c_re, v_ssm_c_im, v_ssm_d, v_ssm_w_glu, v_conv_w, v_conv_w_out, v_w_o, v_g_ffn2, v_ffn2_w_gate, v_ffn2_w_up, v_ffn2_w_down, v_g_final):
    seq, d = x.shape[1], x.shape[2]
    n_meta = meta_tokens.shape[0]
    dh = d // 2
    tp = -(-(n_meta + seq) // ROW_ALIGN) * ROW_ALIGN
    mx, my, mc_ = _position()
    pos = jnp.stack([mx, my, mc_]).astype(jnp.int32)
    shard = 2 * mx + my

    big_names = ["ffn1_w_gate", "ffn1_w_up", "ffn1_w_down", "w_in", "ssm_w_glu", "conv_w_out", "w_o",
                 "ffn2_w_gate", "ffn2_w_up", "ffn2_w_down"]
    transposed = {0, 1, 7, 8}
    drop = lambda arrs: [jnp.swapaxes(a.reshape(a.shape[1:]), 0, 1) if i in transposed else a.reshape(a.shape[1:])
                         for i, a in enumerate(arrs)]
    big_w = drop([ffn1_w_gate, ffn1_w_up, ffn1_w_down, w_in, ssm_w_glu, conv_w_out, w_o,
                  ffn2_w_gate, ffn2_w_up, ffn2_w_down])
    big_m = drop([m_ffn1_w_gate, m_ffn1_w_up, m_ffn1_w_down, m_w_in, m_ssm_w_glu, m_conv_w_out,
                  m_w_o, m_ffn2_w_gate, m_ffn2_w_up, m_ffn2_w_down])
    big_v = drop([v_ffn1_w_gate, v_ffn1_w_up, v_ffn1_w_down, v_w_in, v_ssm_w_glu, v_conv_w_out,
                  v_w_o, v_ffn2_w_gate, v_ffn2_w_up, v_ffn2_w_down])
    pieces = _cast_pieces(big_w[:3], pos) + _cast_pieces(big_w[3:], pos)
    conv_local = conv_w.reshape(conv_w.shape[1], conv_w.shape[3])
    n_first = 3
    first = _run_carry(_allgather_carry("allgather_first", pieces[:n_first], [meta_tokens, conv_local]))
    smalls = first[n_first:]
    stack4 = lambda wl: wl.reshape((4, -1, wl.shape[-1]))
    w1g, w1u, w1d = [stack4(wl) for wl in first[:n_first]]
    natural_cols = lambda s: jnp.transpose(s, (1, 0, 2)).reshape(s.shape[1], 4 * s.shape[2])
    meta_full = natural_cols(smalls[0])
    cw_full = natural_cols(smalls[1])
    cw_pad = jnp.pad(cw_full, ((0, SUBLANES - cw_full.shape[0]), (0, 0)))

    s5_args = (ssm_a_re[0], ssm_a_im[0], ssm_log_dt[0], ssm_b_re[0], ssm_b_im[0], ssm_c_re[0], ssm_c_im[0])
    (mb, mc, _), disc_vjp = jax.vjp(_s5_discretise, *s5_args)
    powt = _s5_powers(ssm_a_re[0], ssm_a_im[0], ssm_log_dt[0], SCAN_TILE // SCAN_SEQS)
    mb16, mc16 = mb.astype(BF16), mc.astype(BF16)

    pad_rows = tp - n_meta - seq
    h0 = jnp.concatenate([meta_full, x.reshape(seq, d), jnp.zeros((pad_rows, d), F32)], axis=0)
    h1, a1, b1, n1, *mid = _ffn_fwd(h0, g_ffn1, w1g, w1u, w1d, "ffn1_fwd",
                                    carry=_allgather_carry("allgather_mixer", pieces[3:7], []))
    win_all, wglu_s, wco_s, wo_s = [stack4(wl) for wl in mid]
    wglu_all = natural_cols(wglu_s)
    wco_all = natural_cols(wco_s)
    wo_all = wo_s.reshape(d, d)
    u, p, w2g, w2u = _win_fwd(h1, g_mix, win_all, carry=_allgather_carry("allgather_ffn2_in", pieces[7:9], []))
    ys5, bnd = _scan_fwd(p, mb16, mc16, powt, ssm_d)
    h2, w2d = _mix_fwd(h1, ys5, p, cw_pad, b_gate, wglu_all, wco_all, wo_all,
                       carry=_allgather_carry("allgather_ffn2_out", pieces[9:], []))
    w2g, w2u, w2d = stack4(w2g), stack4(w2u), stack4(w2d)
    dh3, a2, b2, n2, dg_final, loss_part, dy3 = _ffn_fwd(
        h2, g_ffn2, w2g, w2u, w2d, "ffn2_fwd_loss",
        final=(g_final.reshape(1, d), loss_target.reshape(seq, d), n_meta, seq))

    dh2, dw2g, dw2u, dw2d, dg_ffn2 = _ffn_bwd(dh3, dy3, h2, n2, g_ffn2, a2, b2, w2g, w2u, w2d, "ffn2_bwd")
    dys5, dpb, dwo, dwglu, dwco, dcw, dbg = _mix_bwd(dh2, ys5, p, cw_pad, b_gate, wglu_all, wco_all, wo_all)
    dug, dmb, dmc, dlam, dd = _scan_bwd(p, dys5, mb16, mc16, powt, ssm_d, bnd)
    dh1, dwin, dg_mix, dy1 = _win_bwd(dpb, dug, u, win_all, h1, g_mix, dh2)
    shapes = [w.shape for w in big_w]
    kinds = ["list", "list", "list", "list", "col", "col", "row", "list", "list", "list"]
    rest_grads = [dwin, [dwglu], [dwco], [dwo], dw2g, dw2u, dw2d]
    rs_rest = _ReduceScatter("rest", [_Grad(a, k, s) for a, k, s in
                                      zip(rest_grads, kinds[n_first:], shapes[n_first:])], pos)
    grad_x, dw1g, dw1u, dw1d, dg_ffn1, grad_meta = _ffn_bwd(
        dh1, dy1, h0, n1, g_ffn1, a1, b1, w1g, w1u, w1d, "ffn1_bwd", chain=rs_rest, unpad=(n_meta, seq))
    rs_first = _ReduceScatter("first", [_Grad(a, k, s) for a, k, s in
                                        zip([dw1g, dw1u, dw1d], kinds[:n_first], shapes[:n_first])], pos)
    rs_first.run()
    wmv = list(zip(big_w, big_m, big_v))
    big_out = rs_first.adamw(wmv[:n_first]) + rs_rest.adamw(wmv[n_first:])
    def lead(i, o):
        o = jnp.swapaxes(o, 0, 1) if i in transposed else o
        return o.reshape((1,) + o.shape)

    big_out = {nme: tuple(lead(i, o) for o in outs) for i, (nme, outs) in enumerate(zip(big_names, big_out))}

    s5_grads = disc_vjp((dmb, dmc, jnp.sum(dlam, axis=1)))
    grad_x = grad_x.reshape(1, seq, d)

    small_names = ["g_ffn1", "g_mix", "b_gate", "ssm_a_re", "ssm_a_im", "ssm_log_dt", "ssm_b_re", "ssm_b_im",
                   "ssm_c_re", "ssm_c_im", "ssm_d", "g_ffn2", "g_final", "meta_tokens", "conv_w"]
    small_w = [g_ffn1, g_mix, b_gate, ssm_a_re, ssm_a_im, ssm_log_dt, ssm_b_re, ssm_b_im, ssm_c_re, ssm_c_im,
               ssm_d, g_ffn2, g_final, meta_tokens, conv_w]
    small_m = [m_g_ffn1, m_g_mix, m_b_gate, m_ssm_a_re, m_ssm_a_im, m_ssm_log_dt, m_ssm_b_re, m_ssm_b_im,
               m_ssm_c_re, m_ssm_c_im, m_ssm_d, m_g_ffn2, m_g_final, m_meta_tokens, m_conv_w]
    small_v = [v_g_ffn1, v_g_mix, v_b_gate, v_ssm_a_re, v_ssm_a_im, v_ssm_log_dt, v_ssm_b_re, v_ssm_b_im,
               v_ssm_c_re, v_ssm_c_im, v_ssm_d, v_g_ffn2, v_g_final, v_meta_tokens, v_conv_w]
    local_small = [dg_ffn1, dg_mix, dbg, *s5_grads, jnp.sum(dd, axis=0), dg_ffn2, dg_final,
                   grad_meta, dcw[:conv_w.shape[1]]]
    reduced = _unpack(_allreduce_small(_pack(local_small)), local_small)
    reduced[-2] = lax.dynamic_slice_in_dim(reduced[-2], shard * meta_tokens.shape[1], meta_tokens.shape[1], 1)
    reduced[-1] = lax.dynamic_slice_in_dim(reduced[-1], shard * conv_w.shape[3], conv_w.shape[3], 1)
    small_g = [r.reshape(w.shape) for r, w in zip(reduced, small_w)]
    ds_, nm_, nv_ = _adamw_small(_pack(small_w), _pack(small_g), _pack(small_m), _pack(small_v))
    small_out = {nme: o for nme, o in zip(
        small_names, zip(small_g, _unpack(ds_, small_w), _unpack(nm_, small_w), _unpack(nv_, small_w)))}

    loss = lax.psum(loss_part[0, 0], ("x", "y", "c"))
    order = ["meta_tokens", "g_ffn1", "ffn1_w_gate", "ffn1_w_up", "ffn1_w_down", "g_mix", "w_in", "b_gate",
             "ssm_a_re", "ssm_a_im", "ssm_log_dt", "ssm_b_re", "ssm_b_im", "ssm_c_re", "ssm_c_im", "ssm_d",
             "ssm_w_glu", "conv_w", "conv_w_out", "w_o", "g_ffn2", "ffn2_w_gate", "ffn2_w_up", "ffn2_w_down",
             "g_final"]
    res = {**big_out, **small_out}
    return (loss, grad_x, *[res[nme][0] for nme in order], *[res[nme][1] for nme in order],
            *[res[nme][2] for nme in order], *[res[nme][3] for nme in order])
```

```python
import functools
import math

import jax
import jax.numpy as jnp
from jax import lax
from jax.experimental import pallas as pl
from jax.experimental.pallas import tpu as pltpu

F32 = jnp.float32
BF16 = jnp.bfloat16
MESH = pl.DeviceIdType.MESH

RMS_EPS = 1e-6
ADAM_LR = 0.001
ADAM_B1 = 0.9
ADAM_B2 = 0.999
ADAM_EPS = 1e-08
ADAM_WD = 0.01
ADAM_STEP = 10

LANES = 128
SUBLANES = 8
VMEM_LIMIT = 56 * 1024 * 1024

ROW_ALIGN = 256
SCAN_TILE = 256
SCAN_SEQS = 16
GROUPS_PER_BLOCK = 8


def _params(sem, vmem=VMEM_LIMIT):
    return pltpu.CompilerParams(dimension_semantics=sem, vmem_limit_bytes=vmem)


def _pick_tile(n, candidates):
    for c in candidates:
        if n % c == 0:
            return c
    raise ValueError(f"no tile for {n}")


def _dot(a, b):
    return jnp.dot(a, b, preferred_element_type=F32)


def _dot_nt(a, b):
    return lax.dot_general(a, b, (((1,), (1,)), ((), ())), preferred_element_type=F32)


def _dot_tn(a, b):
    return lax.dot_general(a, b, (((0,), (0,)), ((), ())), preferred_element_type=F32)


def _sigmoid(x):
    return pl.reciprocal(1.0 + jnp.exp(-x), approx=True)


def _rms_stats(h):
    r = lax.rsqrt(jnp.mean(h * h, axis=-1, keepdims=True) + RMS_EPS)
    return h * r, r


def _rms_bwd(xhat, r, g, dn):
    dxh = dn * g
    return r * (dxh - xhat * jnp.mean(dxh * xhat, axis=-1, keepdims=True))


GELU_K = math.sqrt(2.0 / math.pi)
GELU_C = 0.044715


def _gelu(x):
    return 0.5 * x * (1.0 + jnp.tanh(GELU_K * (x + GELU_C * x * x * x)))


def _gelu_grad(x):
    t = jnp.tanh(GELU_K * (x + GELU_C * x * x * x))
    return 0.5 * (1.0 + t) + 0.5 * x * (1.0 - t * t) * GELU_K * (1.0 + 3.0 * GELU_C * x * x)


def _for_tile_rows(i, ni, tm, n_meta, seq, fn):
    pl.when(i == 0)(lambda: fn(0, min(tm - n_meta, seq), n_meta))
    if ni > 1:
        last_lo = (ni - 1) * tm - n_meta
        pl.when(i == ni - 1)(lambda: fn(last_lo, min(seq - last_lo, tm), 0))
    if ni > 2:
        pl.when((i > 0) & (i < ni - 1))(lambda: fn(pl.multiple_of(i * tm - n_meta, SUBLANES), tm, 0))


def _ffn_fwd(h, g, wg, wu, wd, name, final=None, carry=None):
    tp, d = h.shape
    ns, f4, _ = wg.shape
    tm = _pick_tile(tp, (768, 512, 256))
    ni = tp // tm

    def body(*refs):
        refs, phases = split(refs)
        if final is None:
            h_ref, g_ref, wg_ref, wu_ref, wd_ref, ho_ref, a_ref, b_ref, n_scr, acc = refs
        else:
            (h_ref, g_ref, wg_ref, wu_ref, wd_ref, gf_ref, tg_hbm,
             ho_ref, a_ref, b_ref, n_scr, dgf_ref, loss_ref, dy_ref, acc, tg_ref, tg_sem) = refs
        i = pl.program_id(0)
        k = pl.program_id(1)
        _run_phases(phases, carry, i * ns + k, ni * ns)

        if final is not None:
            def target_rows(lo, n, at):
                return pltpu.make_async_copy(tg_hbm.at[pl.ds(lo, n), :], tg_ref.at[pl.ds(at, n), :], tg_sem)

            def fetch_target(lo, n, at):
                if at > 0:
                    tg_ref[pl.ds(0, at), :] = jnp.zeros((at, d), F32)
                if at + n < tm:
                    tg_ref[pl.ds(at + n, tm - at - n), :] = jnp.zeros((tm - at - n, d), F32)
                target_rows(lo, n, at).start()

            pl.when(k == 0)(lambda: _for_tile_rows(i, ni, tm, final[2], final[3], fetch_target))

        @pl.when(k == 0)
        def _():
            xhat, _ = _rms_stats(h_ref[...])
            n_scr[...] = (xhat * g_ref[...]).astype(BF16)
            acc[...] = jnp.zeros_like(acc)

        n = n_scr[...]
        a = _dot_nt(n, wg_ref[...])
        b = _dot_nt(n, wu_ref[...])
        a_ref[...] = a.astype(BF16)
        b_ref[...] = b.astype(BF16)
        s = (a * _sigmoid(a) * b).astype(BF16)
        acc[...] += _dot(s, wd_ref[...])

        if final is None:
            @pl.when(k == ns - 1)
            def _():
                ho_ref[...] = h_ref[...] + 0.5 * acc[...]
        else:
            n_meta, seq = final[2], final[3]

            @pl.when((i == 0) & (k == 0))
            def _():
                dgf_ref[...] = jnp.zeros_like(dgf_ref)
                loss_ref[...] = jnp.zeros_like(loss_ref)

            @pl.when(k == ns - 1)
            def _():
                _for_tile_rows(i, ni, tm, n_meta, seq, lambda lo, n, at: target_rows(lo, n, at).wait())
                h3 = h_ref[...] + 0.5 * acc[...]
                xhat, r = _rms_stats(h3)
                gf = gf_ref[...]
                row = i * tm + lax.broadcasted_iota(jnp.int32, (tm, d), 0)
                valid = (row >= n_meta) & (row < n_meta + seq)
                diff = jnp.where(valid, xhat * gf - tg_ref[...], 0.0)
                dout = diff * (1.0 / d)
                loss_ref[...] += jnp.full(loss_ref.shape, 0.5 * jnp.sum(diff * diff) * (1.0 / d), F32)
                dgf_ref[...] += jnp.sum(dout * xhat, axis=0, keepdims=True)
                dh3 = _rms_bwd(xhat, r, gf, dout)
                ho_ref[...] = dh3
                dy_ref[...] = (0.5 * dh3).astype(BF16)

    row_spec = pl.BlockSpec((tm, d), lambda i, k: (i, 0))
    vec_spec = pl.BlockSpec((1, d), lambda i, k: (0, 0))
    in_specs = [row_spec, vec_spec,
                pl.BlockSpec((None, f4, d), lambda i, k: (k, 0, 0)),
                pl.BlockSpec((None, f4, d), lambda i, k: (k, 0, 0)),
                pl.BlockSpec((None, f4, d), lambda i, k: (k, 0, 0))]
    act_spec = pl.BlockSpec((None, tm, f4), lambda i, k: (k, i, 0))
    out_specs = [row_spec, act_spec, act_spec, row_spec]
    out_shape = [jax.ShapeDtypeStruct((tp, d), F32),
                 jax.ShapeDtypeStruct((ns, tp, f4), BF16),
                 jax.ShapeDtypeStruct((ns, tp, f4), BF16),
                 jax.ShapeDtypeStruct((tp, d), BF16)]
    args = [h, g, wg, wu, wd]
    scratch = [pltpu.VMEM((tm, d), F32)]
    if final is not None:
        in_specs += [vec_spec, ANY]
        args += [final[0], final[1]]
        out_specs += [vec_spec, pl.BlockSpec((1, LANES), lambda i, k: (0, 0)), row_spec]
        out_shape += [jax.ShapeDtypeStruct((1, d), F32), jax.ShapeDtypeStruct((1, LANES), F32),
                      jax.ShapeDtypeStruct((tp, d), BF16)]
        scratch += [pltpu.VMEM((tm, d), F32), pltpu.SemaphoreType.DMA(())]
    split = _attach_carry(carry, in_specs, args, out_specs, out_shape, scratch)
    return pl.pallas_call(
        body, name=name, grid=(ni, ns), in_specs=in_specs, out_specs=out_specs, out_shape=out_shape,
        scratch_shapes=scratch, compiler_params=_params(("arbitrary", "arbitrary")),
    )(*args)


def _ffn_bwd_shard(k, ns, dn_prev, dy, n, a, b, wg, wu, wd, tail, name, carry=None, unpad=None):
    tp, d = n.shape
    f4 = wg.shape[1]
    tm = _pick_tile(tp, (768, 512, 256))
    ni = tp // tm
    first, last = k == 0, k == ns - 1
    unpad = unpad if last else None

    def body(*refs):
        refs, phases = split(refs)
        acc_in = None if first else refs.pop(0)
        if last:
            dh_ref, h_ref, g_ref = refs[:3]
        else:
            dy_ref, n_ref = refs[:2]
        refs = refs[3 if last else 2:]
        a_ref, b_ref, wg_hbm, wu_hbm, wd_hbm = refs[:5]
        refs = refs[5:]
        acc_out, dwg_hbm, dwu_hbm, dwd_hbm = refs[:4]
        rest = refs[4:]
        dg_ref = rest.pop(0) if last else None
        head_ref = rest.pop(0) if unpad else None
        wg_ref, wu_ref, wd_ref, dwg_ref, dwu_ref, dwd_ref, wsem = rest[:7]
        i = pl.program_id(0)
        _run_phases(phases, carry, i, ni)
        if unpad:
            res_ref, res_sem = rest[7:]

            def real_rows(lo, cnt, at):
                return pltpu.make_async_copy(res_ref.at[pl.ds(at, cnt), :], acc_out.at[pl.ds(lo, cnt), :], res_sem)

            def wait_tile(tile):
                _for_tile_rows(tile, ni, tm, *unpad, lambda lo, cnt, at: real_rows(lo, cnt, at).wait())

        @pl.when(i == 0)
        def _():
            loads = [pltpu.make_async_copy(src.at[k], dst, wsem.at[j])
                     for j, (src, dst) in enumerate(((wg_hbm, wg_ref), (wu_hbm, wu_ref), (wd_hbm, wd_ref)))]
            for cp in loads:
                cp.start()
            dwg_ref[...] = jnp.zeros_like(dwg_ref)
            dwu_ref[...] = jnp.zeros_like(dwu_ref)
            dwd_ref[...] = jnp.zeros_like(dwd_ref)
            if last:
                dg_ref[...] = jnp.zeros_like(dg_ref)
            for cp in loads:
                cp.wait()

        if last:
            xhat, r = _rms_stats(h_ref[...])
            n = (xhat * g_ref[...]).astype(BF16)
            dy = (0.5 * dh_ref[...]).astype(BF16)
        else:
            n = n_ref[...]
            dy = dy_ref[...]
        av = a_ref[...].astype(F32)
        bv = b_ref[...].astype(F32)
        sg = _sigmoid(av)
        silu = av * sg
        ds = _dot_nt(dy, wd_ref[...])
        da = (ds * bv * (sg * (1.0 + av * (1.0 - sg)))).astype(BF16)
        db = (ds * silu).astype(BF16)
        s = (silu * bv).astype(BF16)
        dwd_ref[...] += _dot_tn(s, dy)
        dwg_ref[...] += _dot_tn(da, n)
        dwu_ref[...] += _dot_tn(db, n)
        dn = _dot(da, wg_ref[...]) + _dot(db, wu_ref[...])
        if not first:
            dn = dn + acc_in[...]
        if last:
            dg_ref[...] += jnp.sum(dn * xhat, axis=0, keepdims=True)
            dh_in = dh_ref[...] + _rms_bwd(xhat, r, g_ref[...], dn)
            if unpad:
                pl.when(i > 0)(lambda: wait_tile(i - 1))
                res_ref[...] = dh_in

                @pl.when(i == 0)
                def _():
                    head_ref[...] = res_ref[pl.ds(0, unpad[0]), :]

                _for_tile_rows(i, ni, tm, *unpad, lambda lo, cnt, at: real_rows(lo, cnt, at).start())
                pl.when(i == ni - 1)(lambda: wait_tile(i))
            else:
                acc_out[...] = dh_in
        else:
            acc_out[...] = dn

        @pl.when(i == ni - 1)
        def _():
            stores = []
            for j, (acc_ref, stage_ref, out_hbm) in enumerate(((dwg_ref, wg_ref, dwg_hbm), (dwu_ref, wu_ref, dwu_hbm),
                                                              (dwd_ref, wd_ref, dwd_hbm))):
                stage_ref[...] = acc_ref[...].astype(BF16)
                stores.append(pltpu.make_async_copy(stage_ref, out_hbm, wsem.at[j]))
                stores[-1].start()
            for cp in stores:
                cp.wait()

    row_spec = pl.BlockSpec((tm, d), lambda i: (i, 0))
    vec_spec = pl.BlockSpec((1, d), lambda i: (0, 0))
    act_spec = pl.BlockSpec((None, tm, f4), lambda i: (k, i, 0))
    in_specs = [act_spec, act_spec, ANY, ANY, ANY]
    args = [a, b, wg, wu, wd]
    if last:
        in_specs = [row_spec, row_spec, vec_spec] + in_specs
        args = list(tail) + args
    else:
        in_specs = [row_spec, row_spec] + in_specs
        args = [dy, n] + args
    if not first:
        in_specs.insert(0, row_spec)
        args.insert(0, dn_prev)
    out_specs = [row_spec, ANY, ANY, ANY]
    out_shape = [jax.ShapeDtypeStruct((tp, d), F32)] + [jax.ShapeDtypeStruct((f4, d), BF16)] * 3
    scratch = [pltpu.VMEM((f4, d), BF16)] * 3 + [pltpu.VMEM((f4, d), F32)] * 3 + [pltpu.SemaphoreType.DMA((3,))]
    if last:
        out_specs.append(vec_spec)
        out_shape.append(jax.ShapeDtypeStruct((1, d), F32))
    if unpad:
        out_specs[0] = ANY
        out_shape[0] = jax.ShapeDtypeStruct((unpad[1], d), F32)
        out_specs.append(pl.BlockSpec((unpad[0], d), lambda i: (0, 0)))
        out_shape.append(jax.ShapeDtypeStruct((unpad[0], d), F32))
        scratch += [pltpu.VMEM((tm, d), F32), pltpu.SemaphoreType.DMA(())]
    n_host = len(out_shape)
    split = _attach_carry(carry, in_specs, args, out_specs, out_shape, scratch)
    outs = pl.pallas_call(
        body, name=f"{name}_{k}", grid=(ni,), in_specs=in_specs, out_specs=out_specs, out_shape=out_shape,
        scratch_shapes=scratch, compiler_params=_params(("arbitrary",)),
    )(*args)
    return outs[:n_host], outs[n_host:]


def _ffn_bwd(dh_out, dy, h_in, n, g, a, b, wg, wu, wd, name, chain=None, unpad=None):
    ns = wg.shape[0]
    acc, dwg, dwu, dwd = None, [], [], []
    for k in range(ns):
        carry = chain.carry() if chain is not None else None
        outs, carried = _ffn_bwd_shard(k, ns, acc, dy, n, a, b, wg, wu, wd, (dh_out, h_in, g), name, carry, unpad)
        if chain is not None:
            chain.feed(carried)
        acc = outs[0]
        dwg.append(outs[1])
        dwu.append(outs[2])
        dwd.append(outs[3])
    return (acc, dwg, dwu, dwd) + tuple(outs[4:])


def _win_fwd(h, g, w_in, carry=None):
    tp, d = h.shape
    ns = w_in.shape[0]
    tm = _pick_tile(tp, (768, 512, 256))
    ni = tp // tm

    def body(*refs):
        (h_ref, g_ref, w_ref, u_ref, p_ref), phases = split(refs)
        _run_phases(phases, carry, pl.program_id(0), ni)
        xhat, _ = _rms_stats(h_ref[...])
        u = (xhat * g_ref[...]).astype(BF16)
        u_ref[...] = u
        for k in range(ns):
            p_ref[k] = _dot(u, w_ref[k]).astype(BF16)

    in_specs = [pl.BlockSpec((tm, d), lambda i: (i, 0)),
                pl.BlockSpec((1, d), lambda i: (0, 0)),
                pl.BlockSpec((ns, d, d), lambda i: (0, 0, 0))]
    out_specs = [pl.BlockSpec((tm, d), lambda i: (i, 0)),
                 pl.BlockSpec((ns, tm, d), lambda i: (0, i, 0))]
    out_shape = [jax.ShapeDtypeStruct((tp, d), BF16), jax.ShapeDtypeStruct((ns, tp, d), BF16)]
    args, scratch = [h, g, w_in], []
    split = _attach_carry(carry, in_specs, args, out_specs, out_shape, scratch)
    return pl.pallas_call(
        body, name="win_fwd", grid=(ni,), in_specs=in_specs, out_specs=out_specs, out_shape=out_shape,
        scratch_shapes=scratch, compiler_params=_params(("arbitrary",)),
    )(*args)


def _win_bwd_shard(k, ns, du_prev, dpb, dug, u, w_in, h1, g, dh2):
    tp, d = h1.shape
    dh = d // 2
    tm = _pick_tile(tp, (768, 512, 256))
    first, last = k == 0, k == ns - 1

    def body(*refs):
        refs = list(refs)
        acc_in = None if first else refs.pop(0)
        dug_ref = refs.pop(0) if first else None
        dp_ref, u_ref, w_ref = refs[:3]
        refs = refs[3:]
        if last:
            h_ref, g_ref, dh2_ref, acc_out, dw_ref, dg_ref, dy_ref, dw_acc = refs
        else:
            acc_out, dw_ref, dw_acc = refs
        i = pl.program_id(0)

        @pl.when(i == 0)
        def _():
            dw_acc[...] = jnp.zeros_like(dw_acc)
            if last:
                dg_ref[...] = jnp.zeros_like(dg_ref)

        dp = dp_ref[...]
        if first:
            dp = jnp.concatenate([dug_ref[...], dp[:, dh:]], axis=1)
        dw_acc[...] += _dot_tn(u_ref[...], dp)
        du = _dot_nt(dp, w_ref[...])
        if not first:
            du = du + acc_in[...]
        if last:
            xhat, r = _rms_stats(h_ref[...])
            dg_ref[...] += jnp.sum(du * xhat, axis=0, keepdims=True)
            dh1 = dh2_ref[...] + _rms_bwd(xhat, r, g_ref[...], du)
            acc_out[...] = dh1
            dy_ref[...] = (0.5 * dh1).astype(BF16)
        else:
            acc_out[...] = du

        @pl.when(i == tp // tm - 1)
        def _():
            dw_ref[...] = dw_acc[...].astype(BF16)

    row_spec = pl.BlockSpec((tm, d), lambda i: (i, 0))
    vec_spec = pl.BlockSpec((1, d), lambda i: (0, 0))
    in_specs = [pl.BlockSpec((None, tm, d), lambda i: (k, i, 0)), row_spec,
                pl.BlockSpec((None, d, d), lambda i: (k, 0, 0))]
    args = [dpb, u, w_in]
    if first:
        in_specs.insert(0, pl.BlockSpec((tm, dh), lambda i: (i, 0)))
        args.insert(0, dug)
    else:
        in_specs.insert(0, row_spec)
        args.insert(0, du_prev)
    out_specs = [row_spec, pl.BlockSpec((d, d), lambda i: (0, 0))]
    out_shape = [jax.ShapeDtypeStruct((tp, d), F32), jax.ShapeDtypeStruct((d, d), BF16)]
    if last:
        in_specs += [row_spec, vec_spec, row_spec]
        args += [h1, g, dh2]
        out_specs += [vec_spec, row_spec]
        out_shape += [jax.ShapeDtypeStruct((1, d), F32), jax.ShapeDtypeStruct((tp, d), BF16)]
    return pl.pallas_call(
        body, name=f"win_bwd_{k}", grid=(tp // tm,), in_specs=in_specs, out_specs=out_specs,
        out_shape=out_shape, scratch_shapes=[pltpu.VMEM((d, d), F32)],
        compiler_params=_params(("arbitrary",)),
    )(*args)


def _win_bwd(dpb, dug, u, w_in, h1, g, dh2):
    ns = w_in.shape[0]
    acc, dws = None, []
    for k in range(ns):
        outs = _win_bwd_shard(k, ns, acc, dpb, dug, u, w_in, h1, g, dh2)
        acc = outs[0]
        dws.append(outs[1])
    return acc, dws, outs[2], outs[3]


def _cmul(ar, ai, br, bi):
    return ar * br - ai * bi, ar * bi + ai * br


def _scan_rows(j, sub):
    return pl.ds(j * SCAN_SEQS, SCAN_SEQS)


def _permute_rows(src_ref, dst_ref, sub):
    for j in range(sub):
        dst_ref[pl.ds(j * SCAN_SEQS, SCAN_SEQS), :] = src_ref[pl.ds(j, SCAN_SEQS, stride=sub), :]


def _unpermute_rows(src_ref, dst_ref, sub):
    for j in range(sub):
        dst_ref[pl.ds(j, SCAN_SEQS, stride=sub), :] = src_ref[pl.ds(j * SCAN_SEQS, SCAN_SEQS), :]


def _local_scan(x_ref, lr, li, w, sub, reverse):
    hr = jnp.zeros((SCAN_SEQS, w), F32)
    hi = jnp.zeros((SCAN_SEQS, w), F32)
    order = range(sub - 1, -1, -1) if reverse else range(sub)
    for j in order:
        xr = x_ref[_scan_rows(j, sub), pl.ds(0, w)]
        xi = x_ref[_scan_rows(j, sub), pl.ds(w, w)]
        if reverse:
            hr, hi = lr * hr + li * hi + xr, lr * hi - li * hr + xi
        else:
            hr, hi = lr * hr - li * hi + xr, lr * hi + li * hr + xi
        x_ref[_scan_rows(j, sub), pl.ds(0, w)] = hr
        x_ref[_scan_rows(j, sub), pl.ds(w, w)] = hi
    return hr, hi


def _entering_states(er, ei, fr, fi, pow_ref, w, sub, reverse):
    lane = lax.broadcasted_iota(jnp.int32, (SCAN_SEQS, w), 0)
    if reverse:
        edge, shift1 = SCAN_SEQS - 1, SCAN_SEQS - 1
    else:
        edge, shift1 = 0, 1
    zr = jnp.where(lane == edge, pltpu.roll(fr, shift1, 0), pltpu.roll(er, shift1, 0))
    zi = jnp.where(lane == edge, pltpu.roll(fi, shift1, 0), pltpu.roll(ei, shift1, 0))
    for m in range(SCAN_SEQS.bit_length() - 1):
        step, row = 1 << m, sub - 1 + m
        ar = pow_ref[pl.ds(row, 1), pl.ds(0, w)]
        ai = pow_ref[pl.ds(row, 1), pl.ds(w, w)]
        if reverse:
            ai = -ai
            keep = lane < SCAN_SEQS - step
            sr = jnp.where(keep, pltpu.roll(zr, SCAN_SEQS - step, 0), 0.0)
            si = jnp.where(keep, pltpu.roll(zi, SCAN_SEQS - step, 0), 0.0)
        else:
            keep = lane >= step
            sr = jnp.where(keep, pltpu.roll(zr, step, 0), 0.0)
            si = jnp.where(keep, pltpu.roll(zi, step, 0), 0.0)
        pr, pi = _cmul(ar, ai, sr, si)
        zr, zi = zr + pr, zi + pi
    ar = pow_ref[pl.ds(sub - 1, 1), pl.ds(0, w)]
    ai = pow_ref[pl.ds(sub - 1, 1), pl.ds(w, w)]
    if reverse:
        ai = -ai
    pr, pi = _cmul(ar, ai, zr, zi)
    return zr, zi, er + pr, ei + pi


def _scan_fwd(p, mb, mc, powt, dskip):
    _, tp, d = p.shape
    nb, cb, w2 = mb.shape
    w = w2 // 2
    q = SCAN_TILE
    sub = q // SCAN_SEQS
    nt = tp // q
    ds = d // 2

    def body(ug_ref, mb_ref, mc_ref, pow_ref, d_ref, y_ref, bnd_ref, x_scr, carry, nat, perm):
        t = pl.program_id(1)

        @pl.when(t == 0)
        def _():
            carry[...] = jnp.zeros_like(carry)

        ugf = ug_ref[...].astype(F32)
        nat[...] = ugf
        _permute_rows(nat, perm, sub)
        x_scr[...] = _dot(perm[...].astype(BF16), mb_ref[...])
        lr = jnp.broadcast_to(pow_ref[pl.ds(0, 1), pl.ds(0, w)], (SCAN_SEQS, w))
        li = jnp.broadcast_to(pow_ref[pl.ds(0, 1), pl.ds(w, w)], (SCAN_SEQS, w))
        er, ei = _local_scan(x_scr, lr, li, w, sub, False)
        zr, zi, fr, fi = _entering_states(er, ei, carry[:, pl.ds(0, w)], carry[:, pl.ds(w, w)],
                                          pow_ref, w, sub, False)
        carry[:, pl.ds(0, w)] = fr
        carry[:, pl.ds(w, w)] = fi
        bnd_ref[:, pl.ds(0, w)] = fr
        bnd_ref[:, pl.ds(w, w)] = fi
        for j in range(sub):
            pr = pow_ref[pl.ds(j, 1), pl.ds(0, w)]
            pi = pow_ref[pl.ds(j, 1), pl.ds(w, w)]
            cr, ci = _cmul(pr, pi, zr, zi)
            x_scr[_scan_rows(j, sub), pl.ds(0, w)] += cr
            x_scr[_scan_rows(j, sub), pl.ds(w, w)] += ci
        hb = x_scr[...].astype(BF16)
        perm[...] = _dot_nt(hb, mc_ref[...])
        _unpermute_rows(perm, nat, sub)
        y_ref[...] = nat[...] + d_ref[...] * ugf

    in_specs = [pl.BlockSpec((None, q, cb), lambda b, t: (0, t, b)),
                pl.BlockSpec((None, cb, w2), lambda b, t: (b, 0, 0)),
                pl.BlockSpec((None, cb, w2), lambda b, t: (b, 0, 0)),
                pl.BlockSpec((None, powt.shape[1], w2), lambda b, t: (b, 0, 0)),
                pl.BlockSpec((1, cb), lambda b, t: (0, b))]
    out_specs = [pl.BlockSpec((q, cb), lambda b, t: (t, b)),
                 pl.BlockSpec((None, None, SCAN_SEQS, w2), lambda b, t: (b, t, 0, 0))]
    out_shape = [jax.ShapeDtypeStruct((tp, ds), F32), jax.ShapeDtypeStruct((nb, nt, SCAN_SEQS, w2), F32)]
    scratch = [pltpu.VMEM((q, w2), F32), pltpu.VMEM((SCAN_SEQS, w2), F32),
               pltpu.VMEM((q, cb), F32), pltpu.VMEM((q, cb), F32)]
    return pl.pallas_call(
        body, name="s5_scan_fwd", grid=(nb, nt), in_specs=in_specs, out_specs=out_specs,
        out_shape=out_shape, scratch_shapes=scratch, compiler_params=_params(("arbitrary", "arbitrary")),
    )(p, mb, mc, powt, dskip)


def _scan_bwd(p, dy, mb, mc, powt, dskip, bnd):
    _, tp, d = p.shape
    nb, cb, w2 = mb.shape
    w = w2 // 2
    q = SCAN_TILE
    sub = q // SCAN_SEQS
    nt = tp // q
    ds = d // 2

    def body(ug_ref, dy_ref, mb_ref, mc_ref, pow_ref, d_ref, bnd_ref,
             dug_ref, dmb_ref, dmc_ref, dlam_ref, dd_ref, x_scr, y_scr, gcarry, nat, perm):
        t = pl.program_id(1)
        tt = nt - 1 - t

        @pl.when(t == 0)
        def _():
            gcarry[...] = jnp.zeros_like(gcarry)
            dmb_ref[...] = jnp.zeros_like(dmb_ref)
            dmc_ref[...] = jnp.zeros_like(dmc_ref)
            dlam_ref[...] = jnp.zeros_like(dlam_ref)
            dd_ref[...] = jnp.zeros_like(dd_ref)

        ugf = ug_ref[...].astype(F32)
        dyf = dy_ref[...].astype(F32)
        dd_ref[...] += jnp.sum((dyf * ugf).reshape(q // SUBLANES, SUBLANES, cb), axis=0)
        nat[...] = ugf
        _permute_rows(nat, perm, sub)
        ug = perm[...].astype(BF16)
        nat[...] = dyf
        _permute_rows(nat, perm, sub)
        dyb = perm[...].astype(BF16)
        lr = jnp.broadcast_to(pow_ref[pl.ds(0, 1), pl.ds(0, w)], (SCAN_SEQS, w))
        li = jnp.broadcast_to(pow_ref[pl.ds(0, 1), pl.ds(w, w)], (SCAN_SEQS, w))

        x_scr[...] = _dot(ug, mb_ref[...])
        er, ei = _local_scan(x_scr, lr, li, w, sub, False)
        first = tt == 0
        pfr = jnp.where(first, 0.0, bnd_ref[:, pl.ds(0, w)])
        pfi = jnp.where(first, 0.0, bnd_ref[:, pl.ds(w, w)])
        hzr, hzi, _, _ = _entering_states(er, ei, pfr, pfi, pow_ref, w, sub, False)
        for j in range(sub):
            pr = pow_ref[pl.ds(j, 1), pl.ds(0, w)]
            pi = pow_ref[pl.ds(j, 1), pl.ds(w, w)]
            cr, ci = _cmul(pr, pi, hzr, hzi)
            x_scr[_scan_rows(j, sub), pl.ds(0, w)] += cr
            x_scr[_scan_rows(j, sub), pl.ds(w, w)] += ci

        y_scr[...] = _dot(dyb, mc_ref[...])
        er, ei = _local_scan(y_scr, lr, li, w, sub, True)
        gzr, gzi, fr, fi = _entering_states(er, ei, gcarry[:, pl.ds(0, w)], gcarry[:, pl.ds(w, w)],
                                            pow_ref, w, sub, True)
        gcarry[:, pl.ds(0, w)] = fr
        gcarry[:, pl.ds(w, w)] = fi
        accr = jnp.zeros((SCAN_SEQS, w), F32)
        acci = jnp.zeros((SCAN_SEQS, w), F32)
        for j in range(sub):
            pr = pow_ref[pl.ds(sub - 1 - j, 1), pl.ds(0, w)]
            pi = pow_ref[pl.ds(sub - 1 - j, 1), pl.ds(w, w)]
            cr, ci = _cmul(pr, -pi, gzr, gzi)
            gr = y_scr[_scan_rows(j, sub), pl.ds(0, w)] + cr
            gi = y_scr[_scan_rows(j, sub), pl.ds(w, w)] + ci
            y_scr[_scan_rows(j, sub), pl.ds(0, w)] = gr
            y_scr[_scan_rows(j, sub), pl.ds(w, w)] = gi
            if j == 0:
                hpr, hpi = hzr, hzi
            else:
                hpr = x_scr[_scan_rows(j - 1, sub), pl.ds(0, w)]
                hpi = x_scr[_scan_rows(j - 1, sub), pl.ds(w, w)]
            accr += hpr * gr + hpi * gi
            acci += hpr * gi - hpi * gr
        dlam_ref[:, pl.ds(0, w)] += accr
        dlam_ref[:, pl.ds(w, w)] += acci

        hb = x_scr[...].astype(BF16)
        gb = y_scr[...].astype(BF16)
        dmc_ref[...] += _dot_tn(dyb, hb)
        dmb_ref[...] += _dot_tn(ug, gb)
        perm[...] = _dot_nt(gb, mb_ref[...])
        _unpermute_rows(perm, nat, sub)
        dug_ref[...] = (nat[...] + d_ref[...] * dyf).astype(BF16)

    blk = lambda b, t: (b, 0, 0)
    return pl.pallas_call(
        body, name="s5_scan_bwd", grid=(nb, nt),
        in_specs=[pl.BlockSpec((None, q, cb), lambda b, t: (0, nt - 1 - t, b)),
                  pl.BlockSpec((q, cb), lambda b, t: (nt - 1 - t, b)),
                  pl.BlockSpec((None, cb, w2), blk),
                  pl.BlockSpec((None, cb, w2), blk),
                  pl.BlockSpec((None, powt.shape[1], w2), blk),
                  pl.BlockSpec((1, cb), lambda b, t: (0, b)),
                  pl.BlockSpec((None, None, SCAN_SEQS, w2),
                               lambda b, t: (b, jnp.maximum(nt - 2 - t, 0), 0, 0))],
        out_specs=[pl.BlockSpec((q, cb), lambda b, t: (nt - 1 - t, b)),
                   pl.BlockSpec((None, cb, w2), blk),
                   pl.BlockSpec((None, cb, w2), blk),
                   pl.BlockSpec((None, SCAN_SEQS, w2), blk),
                   pl.BlockSpec((SUBLANES, cb), lambda b, t: (0, b))],
        out_shape=[jax.ShapeDtypeStruct((tp, ds), BF16),
                   jax.ShapeDtypeStruct((nb, cb, w2), F32),
                   jax.ShapeDtypeStruct((nb, cb, w2), F32),
                   jax.ShapeDtypeStruct((nb, SCAN_SEQS, w2), F32),
                   jax.ShapeDtypeStruct((SUBLANES, ds), F32)],
        scratch_shapes=[pltpu.VMEM((q, w2), F32), pltpu.VMEM((q, w2), F32),
                        pltpu.VMEM((SCAN_SEQS, w2), F32), pltpu.VMEM((q, cb), F32), pltpu.VMEM((q, cb), F32)],
        compiler_params=_params(("arbitrary", "arbitrary")),
    )(p, dy, mb, mc, powt, dskip, bnd)


HALO = 16


def _mix_tile(ys5, p0, p1, p2, p3, prev_cin, cw, bgate, wglu, wco, d):
    dh = d // 2
    tm = ys5.shape[0]
    v = p0[:, dh:].astype(F32)
    gbr = p1[:, :dh].astype(F32)
    gcr = p1[:, dh:].astype(F32)
    gact = _gelu(ys5).astype(BF16)
    z = _dot(gact, wglu)
    z1, z2 = z[:, :d], z[:, d:]
    sg = _sigmoid(z2)
    y_ssm = z1 * sg
    cin = gcr * v
    ext = jnp.concatenate([cin, prev_cin], axis=0)
    r1 = pltpu.roll(ext, 1, 0)[:tm]
    r2 = pltpu.roll(ext, 2, 0)[:tm]
    cv = cw[2] * cin + cw[1] * r1 + cw[0] * r2
    cg = (gbr * cv).astype(BF16)
    y_conv = _dot(cg, wco)
    g_s = _sigmoid(p2.astype(F32) + bgate[:, :d])
    g_c = _sigmoid(p3.astype(F32) + bgate[:, d:])
    mixed = g_s * y_ssm + g_c * y_conv
    return dict(v=v, gb=gbr, gc=gcr, gact=gact, z1=z1, sg=sg, y_ssm=y_ssm, cin=cin, r1=r1, r2=r2,
                cv=cv, cg=cg, y_conv=y_conv, g_s=g_s, g_c=g_c, mixed=mixed)


def _mix_fwd(h1, ys5, p, cw, bgate, wglu, wco, wo, carry=None):
    tp, d = h1.shape
    dh = d // 2
    tm = ROW_ALIGN
    ni = tp // tm

    def body(*refs):
        refs, phases = split(refs)
        (h_ref, y_ref, p0_ref, p1_ref, p2_ref, p3_ref, cw_ref, bg_ref, wglu_ref, wco_ref, wo_ref,
         o_ref, prev) = refs
        _run_phases(phases, carry, pl.program_id(0), ni)

        @pl.when(pl.program_id(0) == 0)
        def _():
            prev[...] = jnp.zeros_like(prev)

        cw = [cw_ref[pl.ds(t, 1), :] for t in range(3)]
        f = _mix_tile(y_ref[...], p0_ref[...], p1_ref[...], p2_ref[...], p3_ref[...], prev[...],
                      cw, bg_ref[...], wglu_ref[...], wco_ref[...], d)
        prev[...] = f["cin"][tm - HALO:, :]
        o_ref[...] = h_ref[...] + _dot(f["mixed"].astype(BF16), wo_ref[...])

    row = pl.BlockSpec((tm, d), lambda i: (i, 0))
    full = lambda a: pl.BlockSpec(a.shape, lambda i: (0,) * a.ndim)
    pk = lambda k: pl.BlockSpec((None, tm, d), lambda i, k=k: (k, i, 0))
    in_specs = [row, pl.BlockSpec((tm, dh), lambda i: (i, 0)), pk(0), pk(1), pk(2), pk(3),
                full(cw), full(bgate), full(wglu), full(wco), full(wo)]
    out_specs, out_shape = [row], [jax.ShapeDtypeStruct((tp, d), F32)]
    args, scratch = [h1, ys5, p, p, p, p, cw, bgate, wglu, wco, wo], [pltpu.VMEM((HALO, dh), F32)]
    split = _attach_carry(carry, in_specs, args, out_specs, out_shape, scratch)
    return pl.pallas_call(
        body, name="mix_fwd", grid=(ni,), in_specs=in_specs, out_specs=out_specs, out_shape=out_shape,
        scratch_shapes=scratch, compiler_params=_params(("arbitrary",)),
    )(*args)


def _mix_bwd(dh2, ys5, p, cw, bgate, wglu, wco, wo):
    tp, d = dh2.shape
    dh = d // 2
    tm = ROW_ALIGN
    ni = tp // tm
    hb = tm // HALO

    def body(dh_ref, y_ref, p0_ref, p1_ref, p2_ref, p3_ref, h0_ref, h1_ref,
             cw_ref, bg_ref, wglu_ref, wco_ref, wo_ref,
             dys_ref, dpb_ref, dwo_ref, dwglu_ref, dwco_ref, dcw_ref, dbg_ref, nxt):
        i = pl.program_id(0)
        tt = ni - 1 - i

        @pl.when(i == 0)
        def _():
            nxt[...] = jnp.zeros_like(nxt)
            dwo_ref[...] = jnp.zeros_like(dwo_ref)
            dwglu_ref[...] = jnp.zeros_like(dwglu_ref)
            dwco_ref[...] = jnp.zeros_like(dwco_ref)
            dcw_ref[...] = jnp.zeros_like(dcw_ref)
            dbg_ref[...] = jnp.zeros_like(dbg_ref)

        cw = [cw_ref[pl.ds(t, 1), :] for t in range(3)]
        prev_cin = h1_ref[:, dh:].astype(F32) * h0_ref[:, dh:].astype(F32)
        prev_cin = jnp.where(tt == 0, 0.0, prev_cin)
        ys5 = y_ref[...]
        f = _mix_tile(ys5, p0_ref[...], p1_ref[...], p2_ref[...], p3_ref[...], prev_cin,
                      cw, bg_ref[...], wglu_ref[...], wco_ref[...], d)
        dhb = dh_ref[...].astype(BF16)
        dmixed = _dot_nt(dhb, wo_ref[...])
        dwo_ref[...] += _dot_tn(f["mixed"].astype(BF16), dhb)

        g_s, g_c, sg = f["g_s"], f["g_c"], f["sg"]
        dy_ssm = dmixed * g_s
        dy_conv = dmixed * g_c
        dp2 = dmixed * f["y_ssm"] * g_s * (1.0 - g_s)
        dp3 = dmixed * f["y_conv"] * g_c * (1.0 - g_c)
        dbg_ref[:, pl.ds(0, d)] += jnp.sum(dp2, axis=0, keepdims=True)
        dbg_ref[:, pl.ds(d, d)] += jnp.sum(dp3, axis=0, keepdims=True)

        dz = jnp.concatenate([dy_ssm * sg, dy_ssm * f["z1"] * sg * (1.0 - sg)], axis=1).astype(BF16)
        dwglu_ref[...] += _dot_tn(f["gact"], dz)
        dys_ref[...] = (_dot_nt(dz, wglu_ref[...]) * _gelu_grad(ys5)).astype(BF16)

        dycb = dy_conv.astype(BF16)
        dwco_ref[...] += _dot_tn(f["cg"], dycb)
        dcg = _dot_nt(dycb, wco_ref[...])
        dgb = dcg * f["cv"]
        dcv = dcg * f["gb"]
        ext = jnp.concatenate([dcv, nxt[...]], axis=0)
        n1 = pltpu.roll(ext, tm + HALO - 1, 0)[:tm]
        n2 = pltpu.roll(ext, tm + HALO - 2, 0)[:tm]
        nxt[...] = dcv[:HALO, :]
        dcin = cw[2] * dcv + cw[1] * n1 + cw[0] * n2
        dcw_ref[pl.ds(0, 1), :] += jnp.sum(dcv * f["r2"], axis=0, keepdims=True)
        dcw_ref[pl.ds(1, 1), :] += jnp.sum(dcv * f["r1"], axis=0, keepdims=True)
        dcw_ref[pl.ds(2, 1), :] += jnp.sum(dcv * f["cin"], axis=0, keepdims=True)
        dgc = dcin * f["v"]
        dv = dcin * f["gc"]
        dpb_ref[0] = jnp.concatenate([jnp.zeros_like(dv), dv], axis=1).astype(BF16)
        dpb_ref[1] = jnp.concatenate([dgb, dgc], axis=1).astype(BF16)
        dpb_ref[2] = dp2.astype(BF16)
        dpb_ref[3] = dp3.astype(BF16)

    rev = lambda i: ni - 1 - i
    row = pl.BlockSpec((tm, d), lambda i: (rev(i), 0))
    half = pl.BlockSpec((tm, dh), lambda i: (rev(i), 0))
    full = lambda a: pl.BlockSpec(a.shape, lambda i: (0,) * a.ndim)
    pk = lambda k: pl.BlockSpec((None, tm, d), lambda i, k=k: (k, rev(i), 0))
    halo = lambda k: pl.BlockSpec((None, HALO, d), lambda i, k=k: (k, jnp.maximum(rev(i) * hb - 1, 0), 0))
    acc = lambda shape: pl.BlockSpec(shape, lambda i: (0,) * len(shape))
    return pl.pallas_call(
        body, name="mix_bwd", grid=(ni,),
        in_specs=[row, half, pk(0), pk(1), pk(2), pk(3), halo(0), halo(1),
                  full(cw), full(bgate), full(wglu), full(wco), full(wo)],
        out_specs=[half, pl.BlockSpec((4, tm, d), lambda i: (0, rev(i), 0)),
                   acc((d, d)), acc((dh, 2 * d)), acc((dh, d)), acc((SUBLANES, dh)), acc((1, 2 * d))],
        out_shape=[jax.ShapeDtypeStruct((tp, dh), BF16), jax.ShapeDtypeStruct((4, tp, d), BF16),
                   jax.ShapeDtypeStruct((d, d), F32), jax.ShapeDtypeStruct((dh, 2 * d), F32),
                   jax.ShapeDtypeStruct((dh, d), F32), jax.ShapeDtypeStruct((SUBLANES, dh), F32),
                   jax.ShapeDtypeStruct((1, 2 * d), F32)],
        scratch_shapes=[pltpu.VMEM((HALO, dh), F32)],
        compiler_params=_params(("arbitrary",)),
    )(dh2, ys5, p, p, p, p, p, p, cw, bgate, wglu, wco, wo)


ANY = pl.BlockSpec(memory_space=pl.ANY)


def _position():
    return lax.axis_index("x"), lax.axis_index("y"), lax.axis_index("c")


def _remote(src, dst, ssem, rsem, dev):
    return pltpu.make_async_remote_copy(src_ref=src, dst_ref=dst, send_sem=ssem, recv_sem=rsem,
                                        device_id=dev, device_id_type=MESH)


def _cast_pieces(ws, pos):
    n = len(ws)

    def body(pos_ref, *refs):
        for w_ref, o_ref in zip(refs[:n], refs[n:]):
            o_ref[...] = w_ref[...].astype(BF16)

    halves = [(w.shape[0] // 2, w.shape[1]) for w in ws]
    return pl.pallas_call(
        body, name="cast_pieces",
        grid_spec=pltpu.PrefetchScalarGridSpec(
            num_scalar_prefetch=1, grid=(1,),
            in_specs=[pl.BlockSpec(hs, lambda i, pos: (pos[2], 0)) for hs in halves],
            out_specs=[pl.BlockSpec(hs, lambda i, pos: (0, 0)) for hs in halves]),
        out_shape=[jax.ShapeDtypeStruct(hs, BF16) for hs in halves],
        compiler_params=_params(("arbitrary",)),
    )(pos, *ws)


class _Carry:
    def __init__(self, name, arrays, out_shapes, nsem, nlsem, make, fracs):
        self.name, self.arrays, self.out_shapes = name, list(arrays), list(out_shapes)
        self.nsem, self.nlsem, self.make, self.fracs = nsem, max(nlsem, 1), make, fracs


def _carry_scratch(carry):
    return [pltpu.SemaphoreType.DMA((carry.nsem,)), pltpu.SemaphoreType.DMA((carry.nsem,)),
            pltpu.SemaphoreType.DMA((carry.nlsem,))]


def _run_carry(carry):
    na, no = len(carry.arrays), len(carry.out_shapes)

    def body(*refs):
        for phase in carry.make(refs[:na], refs[na:na + no], *refs[na + no:]):
            phase()

    return pl.pallas_call(
        body, name=carry.name, in_specs=[ANY] * na, out_specs=[ANY] * no, out_shape=carry.out_shapes,
        scratch_shapes=_carry_scratch(carry),
    )(*carry.arrays)


def _attach_carry(carry, in_specs, args, out_specs, out_shape, scratch):
    nhi, nho, nhs = len(in_specs), len(out_specs), len(scratch)
    if carry is None:
        return lambda refs: (list(refs), [])
    na, no = len(carry.arrays), len(carry.out_shapes)
    in_specs += [ANY] * na
    args += carry.arrays
    out_specs += [ANY] * no
    out_shape += carry.out_shapes
    scratch += _carry_scratch(carry)

    def split(refs):
        refs = list(refs)
        o = nhi + na
        host = refs[:nhi] + refs[o:o + nho] + refs[o + nho + no:o + nho + no + nhs]
        sems = refs[o + nho + no + nhs:]
        return host, carry.make(refs[nhi:o], refs[o + nho:o + nho + no], *sems)

    return split


def _run_phases(phases, carry, step, total):
    for phase, frac in zip(phases, carry.fracs if carry is not None else ()):
        pl.when(step == int(round(frac * (total - 1))))(phase)


def _allgather_carry(name, pieces, smalls):
    n, ns = len(pieces), len(smalls)
    per = 14
    n_big = per * n

    def make(ins, outs, ssem, rsem, lsem):
        pin, sin = ins[:n], ins[n:]
        wall, sall = outs[:n], outs[n:]
        x, y, c = _position()
        xnb, ynb, sib = (1 - x, y, c), (x, 1 - y, c), (x, y, 1 - c)
        chips = [(1 - x, y), (x, 1 - y), (1 - x, 1 - y)]
        r4 = [p.shape[0] // 2 for p in pin]
        own = lambda i, h: pin[i].at[pl.ds(h * r4[i], r4[i]), :]
        slot = lambda i, xx, yy, cc, h: wall[i].at[xx, yy, cc, h]
        cp = lambda src, dst, s, dev: _remote(src, dst, ssem.at[s], rsem.at[s], dev)
        to_sib = lambda i, xx, yy, h: cp(slot(i, xx, yy, c, h), slot(i, xx, yy, c, h),
                                         per * i + 6 + 4 * xx + 2 * yy + h, sib)

        def local():
            cps = [pltpu.make_async_copy(own(i, h), slot(i, x, y, c, h), lsem.at[2 * i + h])
                   for i in range(n) for h in range(2)]
            return cps + [pltpu.make_async_copy(sin[i], sall[i].at[2 * x + y], lsem.at[2 * n + i])
                          for i in range(ns)]

        def small(px, py, j, i, landing):
            s = n_big + j * ns + i
            return cp(sin[i], sall[i].at[landing], s, (px, py, c))

        def first_hop():
            for lc in local():
                lc.start()
            for j, (px, py) in enumerate(chips):
                for i in range(ns):
                    small(px, py, j, i, 2 * x + y).start()
            for i in range(n):
                cp(own(i, 0), slot(i, x, y, c, 0), per * i, xnb).start()
                cp(own(i, 1), slot(i, x, y, c, 1), per * i + 1, ynb).start()
                for h in range(2):
                    cp(own(i, h), slot(i, x, y, c, h), per * i + 6 + 4 * x + 2 * y + h, sib).start()

        def second_hop():
            for lc in local():
                lc.wait()
            for i in range(n):
                cp(slot(i, 1 - x, y, c, 0), slot(i, 1 - x, y, c, 0), per * i, xnb).wait_recv()
                cp(slot(i, x, 1 - y, c, 1), slot(i, x, 1 - y, c, 1), per * i + 1, ynb).wait_recv()
                for j in range(2):
                    cp(slot(i, j, y, c, 0), slot(i, j, y, c, 0), per * i + 2 + j, ynb).start()
                    cp(slot(i, x, j, c, 1), slot(i, x, j, c, 1), per * i + 4 + j, xnb).start()
                to_sib(i, 1 - x, y, 0).start()
                to_sib(i, x, 1 - y, 1).start()

        def last_to_sibling():
            for i in range(n):
                for j in range(2):
                    cp(slot(i, j, 1 - y, c, 0), slot(i, j, 1 - y, c, 0), per * i + 2 + j, ynb).wait_recv()
                    cp(slot(i, 1 - x, j, c, 1), slot(i, 1 - x, j, c, 1), per * i + 4 + j, xnb).wait_recv()
                    to_sib(i, j, 1 - y, 0).start()
                    to_sib(i, 1 - x, j, 1).start()

        def finish():
            for i in range(n):
                for xx in range(2):
                    for yy in range(2):
                        for h in range(2):
                            s = per * i + 6 + 4 * xx + 2 * yy + h
                            cp(slot(i, xx, yy, 1 - c, h), slot(i, xx, yy, 1 - c, h), s, sib).wait_recv()
                            to_sib(i, xx, yy, h).wait_send()
                cp(own(i, 0), slot(i, x, y, c, 0), per * i, xnb).wait_send()
                cp(own(i, 1), slot(i, x, y, c, 1), per * i + 1, ynb).wait_send()
                for j in range(2):
                    cp(slot(i, j, y, c, 0), slot(i, j, y, c, 0), per * i + 2 + j, ynb).wait_send()
                    cp(slot(i, x, j, c, 1), slot(i, x, j, c, 1), per * i + 4 + j, xnb).wait_send()
            for j, (px, py) in enumerate(chips):
                for i in range(ns):
                    small(px, py, j, i, 2 * px + py).wait_recv()
                    small(px, py, j, i, 2 * x + y).wait_send()

        return [first_hop, second_hop, last_to_sibling, finish]

    out_shapes = [jax.ShapeDtypeStruct((2, 2, 2, 2, a.shape[0] // 2, a.shape[1]), a.dtype) for a in pieces]
    out_shapes += [jax.ShapeDtypeStruct((4,) + a.shape, a.dtype) for a in smalls]
    return _Carry(name, list(pieces) + list(smalls), out_shapes, n_big + 3 * ns, 2 * n + ns, make,
                  (0.0, 0.23, 0.73, 1.0))


def _exchange_carry(name, arrays, out_shapes, plan):
    count = plan([None] * len(arrays), [None] * len(out_shapes), None)

    def make(ins, outs, ssem, rsem, lsem):
        def copies():
            return [_remote(src, dst, ssem.at[j], rsem.at[j], peer)
                    for j, (src, dst, peer) in enumerate(plan(ins, outs, _position()))]

        def start():
            for c in copies():
                c.start()

        def wait():
            for c in copies():
                c.wait()

        return [start, wait]

    return _Carry(name, arrays, out_shapes, count, 0, make, (0.0, 1.0))


class _Grad:
    def __init__(self, arrs, kind, shard_shape):
        self.arrs, self.kind = list(arrs), kind
        self.rows, self.cols = shard_shape
        self.r2 = self.rows // 2

    def view(self, refs, k, h):
        r2 = self.r2
        if self.kind == "list":
            return refs[k].at[pl.ds(h * r2, r2), :]
        if self.kind == "stacked":
            return refs[0].at[k, pl.ds(h * r2, r2), :]
        if self.kind == "col":
            return refs[0].at[pl.ds(h * r2, r2), pl.ds(k * self.cols, self.cols)]
        return refs[0].at[pl.ds((2 * k + h) * r2, r2), :]

    def add_half(self, recv, pos):
        r2, cols = self.r2, self.cols
        n_in = len(self.arrs)
        tr = _row_tile(r2, cols)
        nt = r2 // tr

        def body(pos_ref, *refs):
            m_refs, (r_ref, of_ref, ob_ref) = refs[:n_in], refs[n_in:]
            mine = m_refs[0][...]
            for kk in range(1, n_in):
                mine = jnp.where(pl.program_id(1) == kk, m_refs[kk][...], mine)
            s = mine.astype(F32) + r_ref[...].astype(F32)
            of_ref[...] = s
            ob_ref[...] = s.astype(BF16)

        row = lambda t, pos: pos[2] * nt + t
        if self.kind == "list":
            specs = [pl.BlockSpec((tr, cols), lambda t, k, pos: (row(t, pos), 0))] * n_in
        elif self.kind == "stacked":
            specs = [pl.BlockSpec((None, tr, cols), lambda t, k, pos: (k, row(t, pos), 0))]
        elif self.kind == "col":
            specs = [pl.BlockSpec((tr, cols), lambda t, k, pos: (row(t, pos), k))]
        else:
            specs = [pl.BlockSpec((tr, cols), lambda t, k, pos: (2 * k * nt + row(t, pos), 0))]
        blk = pl.BlockSpec((None, tr, cols), lambda t, k, pos: (k, t, 0))
        return pl.pallas_call(
            body, name="rs_add_c",
            grid_spec=pltpu.PrefetchScalarGridSpec(
                num_scalar_prefetch=1, grid=(nt, 4), in_specs=specs + [blk], out_specs=[blk, blk]),
            out_shape=[jax.ShapeDtypeStruct((4, r2, cols), F32), jax.ShapeDtypeStruct((4, r2, cols), BF16)],
            compiler_params=_params(("arbitrary", "arbitrary")),
        )(pos, *self.arrs, recv)


def _row_tile(rows, cols):
    fits = [t for t in range(16, rows + 1, 16) if rows % t == 0 and t * cols * 4 <= 2 * 1024 * 1024]
    return max(fits) if fits else rows


def _adamw_math(w, g, m, v):
    m = ADAM_B1 * m + (1.0 - ADAM_B1) * g
    v = ADAM_B2 * v + (1.0 - ADAM_B2) * (g * g)
    m_hat = m / (1.0 - ADAM_B1 ** ADAM_STEP)
    v_hat = v / (1.0 - ADAM_B2 ** ADAM_STEP)
    delta = -ADAM_LR * (m_hat / (jnp.sqrt(v_hat) + ADAM_EPS) + ADAM_WD * w)
    return delta, m, v


def _adamw_big(w, m, v, own, sib, pos):
    rows, cols = w.shape
    r2 = rows // 2

    tr = _row_tile(r2, cols)
    nt = r2 // tr

    def body(pos_ref, w_ref, m_ref, v_ref, own_ref, sib_ref, g_ref, d_ref, nm_ref, nv_ref):
        h = pl.program_id(0)
        g = jnp.where(h == pos_ref[2], own_ref[...], sib_ref[...])
        g_ref[...] = g
        d_ref[...], nm_ref[...], nv_ref[...] = _adamw_math(w_ref[...], g, m_ref[...], v_ref[...])

    half = pl.BlockSpec((tr, cols), lambda h, t, pos: (h * nt + t, 0))
    piece = pl.BlockSpec((tr, cols), lambda h, t, pos: (t, 0))
    out = jax.ShapeDtypeStruct((rows, cols), F32)
    return pl.pallas_call(
        body, name="adamw",
        grid_spec=pltpu.PrefetchScalarGridSpec(
            num_scalar_prefetch=1, grid=(2, nt),
            in_specs=[half, half, half, piece, piece],
            out_specs=[half, half, half, half]),
        out_shape=[out, out, out, out],
        compiler_params=_params(("arbitrary", "arbitrary")),
    )(pos, w, m, v, own, sib)


def _add_hop1(s1fs, recvs, pos):
    n = len(s1fs)
    s1vs = [s.reshape((4, 2) + r.shape[2:]) for s, r in zip(s1fs, recvs)]

    def body(pos_ref, *refs):
        for m_ref, r_ref, of_ref, ob_ref in zip(refs[:n], refs[n:2 * n], refs[2 * n:3 * n], refs[3 * n:]):
            s = m_ref[...] + r_ref[...].astype(F32)
            of_ref[...] = s
            ob_ref[...] = s.astype(BF16)

    def mine(h, j, pos):
        return (jnp.where(h == 0, 2 * j + pos[1], 2 * pos[0] + j), h, 0, 0)

    tile = lambda r: (None, None) + r.shape[2:]
    blks = [pl.BlockSpec(tile(r), lambda h, j, pos: (h, j, 0, 0)) for r in recvs]
    outs = pl.pallas_call(
        body, name="rs_add_1",
        grid_spec=pltpu.PrefetchScalarGridSpec(
            num_scalar_prefetch=1, grid=(2, 2),
            in_specs=[pl.BlockSpec(tile(r), mine) for r in recvs] + blks, out_specs=blks + blks),
        out_shape=[jax.ShapeDtypeStruct(r.shape, F32) for r in recvs]
        + [jax.ShapeDtypeStruct(r.shape, BF16) for r in recvs],
        compiler_params=_params(("arbitrary", "arbitrary")),
    )(pos, *s1vs, *recvs)
    return list(zip(outs[:n], outs[n:]))


def _own_sum(s2fs, recvs, pos):
    n = len(s2fs)

    def body(pos_ref, *refs):
        for s_ref, r_ref, o_ref in zip(refs[:n], refs[n:2 * n], refs[2 * n:]):
            o_ref[...] = s_ref[...] + r_ref[...].astype(F32)

    blks = [pl.BlockSpec((None,) + r.shape[1:], lambda h, pos: (h, 0, 0)) for r in recvs]
    return pl.pallas_call(
        body, name="own_sum",
        grid_spec=pltpu.PrefetchScalarGridSpec(
            num_scalar_prefetch=1, grid=(2,),
            in_specs=[pl.BlockSpec((None, None) + r.shape[1:],
                                   lambda h, pos: (h, jnp.where(h == 0, pos[0], pos[1]), 0, 0)) for r in recvs]
            + blks, out_specs=blks),
        out_shape=[jax.ShapeDtypeStruct(r.shape, F32) for r in recvs],
        compiler_params=_params(("arbitrary",)),
    )(pos, *s2fs, *recvs)


def _add_small(a, b):
    def body(a_ref, b_ref, o_ref):
        o_ref[...] = a_ref[...] + b_ref[...]

    vm = pl.BlockSpec(memory_space=pltpu.VMEM)
    return pl.pallas_call(body, name="add_small", in_specs=[vm, vm], out_specs=vm,
                          out_shape=jax.ShapeDtypeStruct(a.shape, F32))(a, b)


def _adamw_small(w, g, m, v):
    def body(w_ref, g_ref, m_ref, v_ref, d_ref, nm_ref, nv_ref):
        d_ref[...], nm_ref[...], nv_ref[...] = _adamw_math(w_ref[...], g_ref[...], m_ref[...], v_ref[...])

    vm = pl.BlockSpec(memory_space=pltpu.VMEM)
    out = jax.ShapeDtypeStruct(w.shape, F32)
    return pl.pallas_call(body, name="adamw_small", in_specs=[vm] * 4, out_specs=[vm] * 3,
                          out_shape=[out, out, out])(w, g, m, v)


class _ReduceScatter:
    def __init__(self, tag, grads, pos, extra=None):
        self.tag, self.grads, self.pos, self.stage, self.extra = tag, grads, pos, 0, extra

    def carry(self):
        grads, n = self.grads, len(self.grads)
        r4 = [g.r2 // 2 for g in grads]

        first = [sum(len(g.arrs) for g in grads[:i]) for i in range(n)]

        def plan_c(ins, outs, p):
            if p is None:
                return 4 * n
            x, y, c = p
            mine = lambda i: ins[first[i]:first[i] + len(grads[i].arrs)]
            return [(grads[i].view(mine(i), k, 1 - c), outs[i].at[k], (x, y, 1 - c))
                    for i in range(n) for k in range(4)]

        def plan_1(ins, outs, p):
            if p is None:
                return 4 * n
            x, y, c = p
            copies = []
            for i in range(n):
                for j in range(2):
                    copies.append((ins[i].at[2 * j + (1 - y), pl.ds(0, r4[i]), :], outs[i].at[0, j],
                                   (x, 1 - y, c)))
                    copies.append((ins[i].at[2 * (1 - x) + j, pl.ds(r4[i], r4[i]), :], outs[i].at[1, j],
                                   (1 - x, y, c)))
            return copies

        def plan_2(ins, outs, p):
            if p is None:
                return 2 * n
            x, y, c = p
            copies = []
            for i in range(n):
                copies.append((ins[i].at[0, 1 - x], outs[i].at[0], (1 - x, y, c)))
                copies.append((ins[i].at[1, 1 - y], outs[i].at[1], (x, 1 - y, c)))
            return copies

        def plan_s(ins, outs, p):
            if p is None:
                return n
            x, y, c = p
            return [(ins[i], outs[i], (x, y, 1 - c)) for i in range(n)]

        shape = lambda lead, dt: [jax.ShapeDtypeStruct(lead(g) + (g.cols,), dt) for g in grads]
        stage = self.stage
        if stage == 0:
            name, arrays, plan = "exchange_c", [a for g in grads for a in g.arrs], plan_c
            shapes = [jax.ShapeDtypeStruct((4, g.r2, g.cols), g.arrs[0].dtype) for g in grads]
        elif stage == 1:
            name, arrays, plan = "exchange_1", [s[1] for s in self.s1], plan_1
            shapes = shape(lambda g: (2, 2, g.r2 // 2), BF16)
        elif stage == 2:
            name, arrays, plan = "exchange_2", [s[1] for s in self.s2], plan_2
            shapes = shape(lambda g: (2, g.r2 // 2), BF16)
        else:
            name, arrays, plan, shapes = "exchange_sibling", self.own, plan_s, shape(lambda g: (g.r2,), F32)
        if self.extra is not None and stage < 3:
            def with_extra(ins, outs, p, plan=plan):
                if p is None:
                    return plan(ins[:-1], outs[:-1], None) + 1
                x, y, c = p
                peer = [(x, y, 1 - c), (x, 1 - y, c), (1 - x, y, c)][stage]
                return plan(ins[:-1], outs[:-1], p) + [(ins[-1], outs[-1], peer)]

            arrays = arrays + [self.extra]
            shapes = shapes + [jax.ShapeDtypeStruct(self.extra.shape, F32)]
            plan = with_extra
        return _exchange_carry(f"rs_{self.tag}_{name}", arrays, shapes, plan)

    def feed(self, recv):
        grads, pos = self.grads, self.pos
        recv = list(recv)
        if self.extra is not None and self.stage < 3:
            self.extra = _add_small(self.extra, recv.pop())
        if self.stage == 0:
            self.s1 = [g.add_half(r, pos) for g, r in zip(grads, recv)]
        elif self.stage == 1:
            self.s2 = _add_hop1([s[0] for s in self.s1], list(recv), pos)
        elif self.stage == 2:
            own = _own_sum([s[0] for s in self.s2], list(recv), pos)
            self.own = [o.reshape(g.r2, g.cols) for g, o in zip(grads, own)]
        else:
            self.sib = list(recv)
        self.stage += 1

    def run(self):
        while self.stage < 4:
            self.feed(_run_carry(self.carry()))

    def adamw(self, weights):
        return [_adamw_big(w, m, v, o, sb, self.pos) for (w, m, v), o, sb in zip(weights, self.own, self.sib)]


def _block_diag(t, nb):
    g, c, p = t.shape
    gb = g // nb
    t = t.reshape(nb, gb, c, p)
    eye = jnp.eye(gb, dtype=t.dtype)
    return jnp.einsum("bgcp,gh->bgchp", t, eye).reshape(nb, gb * c, gb * p)


def _s5_discretise(a_re, a_im, log_dt, b_re, b_im, c_re, c_im):
    g, p = a_re.shape
    nb = g // GROUPS_PER_BLOCK
    dt = jnp.exp(log_dt)[:, None]
    mag = jnp.exp(a_re * dt)
    lam_re = mag * jnp.cos(a_im * dt)
    lam_im = mag * jnp.sin(a_im * dt)
    den = a_re * a_re + a_im * a_im
    q_re = ((lam_re - 1.0) * a_re + lam_im * a_im) / den
    q_im = (lam_im * a_re - (lam_re - 1.0) * a_im) / den
    bb_re = q_re[..., None] * b_re - q_im[..., None] * b_im
    bb_im = q_re[..., None] * b_im + q_im[..., None] * b_re
    tr = lambda t: jnp.swapaxes(t, 1, 2)
    mb = jnp.concatenate([_block_diag(tr(bb_re), nb), _block_diag(tr(bb_im), nb)], axis=-1)
    mc = jnp.concatenate([_block_diag(c_re, nb), -_block_diag(c_im, nb)], axis=-1)
    lam = jnp.concatenate([lam_re.reshape(nb, -1), lam_im.reshape(nb, -1)], axis=-1)
    return mb, mc, lam


def _s5_powers(a_re, a_im, log_dt, sub):
    g, p = a_re.shape
    nb = g // GROUPS_PER_BLOCK
    dt = jnp.exp(log_dt)[:, None]
    ns = list(range(1, sub + 1)) + [sub << m for m in range(1, SCAN_SEQS.bit_length() - 1)]
    ns += [0] * (-len(ns) % SUBLANES)
    e = jnp.asarray(ns, F32)[:, None, None]
    mag = jnp.exp(a_re[None] * dt[None] * e)
    ang = a_im[None] * dt[None] * e
    re = (mag * jnp.cos(ang)).reshape(len(ns), nb, -1)
    im = (mag * jnp.sin(ang)).reshape(len(ns), nb, -1)
    return jnp.transpose(jnp.concatenate([re, im], axis=-1), (1, 0, 2))


def _pack(parts):
    flat = jnp.concatenate([a.reshape(-1).astype(F32) for a in parts])
    n = flat.shape[0]
    pad = -n % (SUBLANES * LANES)
    return jnp.pad(flat, (0, pad)).reshape(-1, LANES)


def _unpack(buf, like):
    flat = buf.reshape(-1)
    out, o = [], 0
    for a in like:
        out.append(flat[o:o + a.size].reshape(a.shape))
        o += a.size
    return out


def kernel(x, meta_tokens, g_ffn1, ffn1_w_gate, ffn1_w_up, ffn1_w_down, g_mix, w_in, b_gate, ssm_a_re, ssm_a_im, ssm_log_dt, ssm_b_re, ssm_b_im, ssm_c_re, ssm_c_im, ssm_d, ssm_w_glu, conv_w, conv_w_out, w_o, g_ffn2, ffn2_w_gate, ffn2_w_up, ffn2_w_down, g_final, loss_target, m_meta_tokens, m_g_ffn1, m_ffn1_w_gate, m_ffn1_w_up, m_ffn1_w_down, m_g_mix, m_w_in, m_b_gate, m_ssm_a_re, m_ssm_a_im, m_ssm_log_dt, m_ssm_b_re, m_ssm_b_im, m_ssm_c_re, m_ssm_c_im, m_ssm_d, m_ssm_w_glu, m_conv_w, m_conv_w_out, m_w_o, m_g_ffn2, m_ffn2_w_gate, m_ffn2_w_up, m_ffn2_w_down, m_g_final, v_meta_tokens, v_g_ffn1, v_ffn1_w_gate, v_ffn1_w_up, v_ffn1_w_down, v_g_mix, v_w_in, v_b_gate, v_ssm_a_re, v_ssm_a_im, v_ssm_log_dt, v_ssm_b_re, v_ssm_b_im, v_ssm_c_re, v_ssm_c_im, v_ssm_d, v_ssm_w_glu, v_conv_w, v_conv_w_out, v_w_o, v_g_ffn2, v_ffn2_w_gate, v_ffn2_w_up, v_ffn2_w_down, v_g_final):
    seq, d = x.shape[1], x.shape[2]
    n_meta = meta_tokens.shape[0]
    dh = d // 2
    tp = -(-(n_meta + seq) // ROW_ALIGN) * ROW_ALIGN
    mx, my, mc_ = _position()
    pos = jnp.stack([mx, my, mc_]).astype(jnp.int32)
    shard = 2 * mx + my

    big_names = ["ffn1_w_gate", "ffn1_w_up", "ffn1_w_down", "w_in", "ssm_w_glu", "conv_w_out", "w_o",
                 "ffn2_w_gate", "ffn2_w_up", "ffn2_w_down"]
    transposed = {0, 1, 7, 8}
    drop = lambda arrs: [jnp.swapaxes(a.reshape(a.shape[1:]), 0, 1) if i in transposed else a.reshape(a.shape[1:])
                         for i, a in enumerate(arrs)]
    big_w = drop([ffn1_w_gate, ffn1_w_up, ffn1_w_down, w_in, ssm_w_glu, conv_w_out, w_o,
                  ffn2_w_gate, ffn2_w_up, ffn2_w_down])
    big_m = drop([m_ffn1_w_gate, m_ffn1_w_up, m_ffn1_w_down, m_w_in, m_ssm_w_glu, m_conv_w_out,
                  m_w_o, m_ffn2_w_gate, m_ffn2_w_up, m_ffn2_w_down])
    big_v = drop([v_ffn1_w_gate, v_ffn1_w_up, v_ffn1_w_down, v_w_in, v_ssm_w_glu, v_conv_w_out,
                  v_w_o, v_ffn2_w_gate, v_ffn2_w_up, v_ffn2_w_down])
    pieces = _cast_pieces(big_w[:3], pos) + _cast_pieces(big_w[3:], pos)
    conv_local = conv_w.reshape(conv_w.shape[1], conv_w.shape[3])
    n_first = 3
    first = _run_carry(_allgather_carry("allgather_first", pieces[:n_first], [meta_tokens, conv_local]))
    smalls = first[n_first:]
    stack4 = lambda wl: wl.reshape((4, -1, wl.shape[-1]))
    w1g, w1u, w1d = [stack4(wl) for wl in first[:n_first]]
    natural_cols = lambda s: jnp.transpose(s, (1, 0, 2)).reshape(s.shape[1], 4 * s.shape[2])
    meta_full = natural_cols(smalls[0])
    cw_full = natural_cols(smalls[1])
    cw_pad = jnp.pad(cw_full, ((0, SUBLANES - cw_full.shape[0]), (0, 0)))

    s5_args = (ssm_a_re[0], ssm_a_im[0], ssm_log_dt[0], ssm_b_re[0], ssm_b_im[0], ssm_c_re[0], ssm_c_im[0])
    (mb, mc, _), disc_vjp = jax.vjp(_s5_discretise, *s5_args)
    powt = _s5_powers(ssm_a_re[0], ssm_a_im[0], ssm_log_dt[0], SCAN_TILE // SCAN_SEQS)
    mb16, mc16 = mb.astype(BF16), mc.astype(BF16)

    pad_rows = tp - n_meta - seq
    h0 = jnp.concatenate([meta_full, x.reshape(seq, d), jnp.zeros((pad_rows, d), F32)], axis=0)
    h1, a1, b1, n1, *mid = _ffn_fwd(h0, g_ffn1, w1g, w1u, w1d, "ffn1_fwd",
                                    carry=_allgather_carry("allgather_mixer", pieces[3:7], []))
    win_all, wglu_s, wco_s, wo_s = [stack4(wl) for wl in mid]
    wglu_all = natural_cols(wglu_s)
    wco_all = natural_cols(wco_s)
    wo_all = wo_s.reshape(d, d)
    u, p, w2g, w2u = _win_fwd(h1, g_mix, win_all, carry=_allgather_carry("allgather_ffn2_in", pieces[7:9], []))
    ys5, bnd = _scan_fwd(p, mb16, mc16, powt, ssm_d)
    h2, w2d = _mix_fwd(h1, ys5, p, cw_pad, b_gate, wglu_all, wco_all, wo_all,
                       carry=_allgather_carry("allgather_ffn2_out", pieces[9:], []))
    w2g, w2u, w2d = stack4(w2g), stack4(w2u), stack4(w2d)
    dh3, a2, b2, n2, dg_final, loss_part, dy3 = _ffn_fwd(
        h2, g_ffn2, w2g, w2u, w2d, "ffn2_fwd_loss",
        final=(g_final.reshape(1, d), loss_target.reshape(seq, d), n_meta, seq))

    dh2, dw2g, dw2u, dw2d, dg_ffn2 = _ffn_bwd(dh3, dy3, h2, n2, g_ffn2, a2, b2, w2g, w2u, w2d, "ffn2_bwd")
    dys5, dpb, dwo, dwglu, dwco, dcw, dbg = _mix_bwd(dh2, ys5, p, cw_pad, b_gate, wglu_all, wco_all, wo_all)
    dug, dmb, dmc, dlam, dd = _scan_bwd(p, dys5, mb16, mc16, powt, ssm_d, bnd)
    dh1, dwin, dg_mix, dy1 = _win_bwd(dpb, dug, u, win_all, h1, g_mix, dh2)
    shapes = [w.shape for w in big_w]
    kinds = ["list", "list", "list", "list", "col", "col", "row", "list", "list", "list"]
    rest_grads = [dwin, [dwglu], [dwco], [dwo], dw2g, dw2u, dw2d]
    rs_rest = _ReduceScatter("rest", [_Grad(a, k, s) for a, k, s in
                                      zip(rest_grads, kinds[n_first:], shapes[n_first:])], pos)
    grad_x, dw1g, dw1u, dw1d, dg_ffn1, grad_meta = _ffn_bwd(
        dh1, dy1, h0, n1, g_ffn1, a1, b1, w1g, w1u, w1d, "ffn1_bwd", chain=rs_rest, unpad=(n_meta, seq))
    s5_grads = disc_vjp((dmb, dmc, jnp.sum(dlam, axis=1)))
    local_small = [dg_ffn1, dg_mix, dbg, *s5_grads, jnp.sum(dd, axis=0), dg_ffn2, dg_final,
                   grad_meta, dcw[:conv_w.shape[1]]]
    rs_first = _ReduceScatter("first", [_Grad(a, k, s) for a, k, s in
                                        zip([dw1g, dw1u, dw1d], kinds[:n_first], shapes[:n_first])], pos,
                              extra=_pack(local_small))
    rs_first.run()
    wmv = list(zip(big_w, big_m, big_v))
    big_out = rs_first.adamw(wmv[:n_first]) + rs_rest.adamw(wmv[n_first:])
    def lead(i, o):
        o = jnp.swapaxes(o, 0, 1) if i in transposed else o
        return o.reshape((1,) + o.shape)

    big_out = {nme: tuple(lead(i, o) for o in outs) for i, (nme, outs) in enumerate(zip(big_names, big_out))}

    grad_x = grad_x.reshape(1, seq, d)

    small_names = ["g_ffn1", "g_mix", "b_gate", "ssm_a_re", "ssm_a_im", "ssm_log_dt", "ssm_b_re", "ssm_b_im",
                   "ssm_c_re", "ssm_c_im", "ssm_d", "g_ffn2", "g_final", "meta_tokens", "conv_w"]
    small_w = [g_ffn1, g_mix, b_gate, ssm_a_re, ssm_a_im, ssm_log_dt, ssm_b_re, ssm_b_im, ssm_c_re, ssm_c_im,
               ssm_d, g_ffn2, g_final, meta_tokens, conv_w]
    small_m = [m_g_ffn1, m_g_mix, m_b_gate, m_ssm_a_re, m_ssm_a_im, m_ssm_log_dt, m_ssm_b_re, m_ssm_b_im,
               m_ssm_c_re, m_ssm_c_im, m_ssm_d, m_g_ffn2, m_g_final, m_meta_tokens, m_conv_w]
    small_v = [v_g_ffn1, v_g_mix, v_b_gate, v_ssm_a_re, v_ssm_a_im, v_ssm_log_dt, v_ssm_b_re, v_ssm_b_im,
               v_ssm_c_re, v_ssm_c_im, v_ssm_d, v_g_ffn2, v_g_final, v_meta_tokens, v_conv_w]
    reduced = _unpack(rs_first.extra, local_small)
    reduced[-2] = lax.dynamic_slice_in_dim(reduced[-2], shard * meta_tokens.shape[1], meta_tokens.shape[1], 1)
    reduced[-1] = lax.dynamic_slice_in_dim(reduced[-1], shard * conv_w.shape[3], conv_w.shape[3], 1)
    small_g = [r.reshape(w.shape) for r, w in zip(reduced, small_w)]
    ds_, nm_, nv_ = _adamw_small(_pack(small_w), _pack(small_g), _pack(small_m), _pack(small_v))
    small_out = {nme: o for nme, o in zip(
        small_names, zip(small_g, _unpack(ds_, small_w), _unpack(nm_, small_w), _unpack(nv_, small_w)))}

    loss = lax.psum(loss_part[0, 0], ("x", "y", "c"))
    order = ["meta_tokens", "g_ffn1", "ffn1_w_gate", "ffn1_w_up", "ffn1_w_down", "g_mix", "w_in", "b_gate",
             "ssm_a_re", "ssm_a_im", "ssm_log_dt", "ssm_b_re", "ssm_b_im", "ssm_c_re", "ssm_c_im", "ssm_d",
             "ssm_w_glu", "conv_w", "conv_w_out", "w_o", "g_ffn2", "ffn2_w_gate", "ffn2_w_up", "ffn2_w_down",
             "g_final"]
    res = {**big_out, **small_out}
    return (loss, grad_x, *[res[nme][0] for nme in order], *[res[nme][1] for nme in order],
            *[res[nme][2] for nme in order], *[res[nme][3] for nme in order])
```

```python
import functools
import math

import jax
import jax.numpy as jnp
from jax import lax
from jax.experimental import pallas as pl
from jax.experimental.pallas import tpu as pltpu

F32 = jnp.float32
BF16 = jnp.bfloat16
MESH = pl.DeviceIdType.MESH

RMS_EPS = 1e-6
ADAM_LR = 0.001
ADAM_B1 = 0.9
ADAM_B2 = 0.999
ADAM_EPS = 1e-08
ADAM_WD = 0.01
ADAM_STEP = 10

LANES = 128
SUBLANES = 8
VMEM_LIMIT = 56 * 1024 * 1024

ROW_ALIGN = 256
SCAN_TILE = 256
SCAN_SEQS = 16
GROUPS_PER_BLOCK = 8


def _params(sem, vmem=VMEM_LIMIT):
    return pltpu.CompilerParams(dimension_semantics=sem, vmem_limit_bytes=vmem)


def _pick_tile(n, candidates):
    for c in candidates:
        if n % c == 0:
            return c
    raise ValueError(f"no tile for {n}")


def _dot(a, b):
    return jnp.dot(a, b, preferred_element_type=F32)


def _dot_nt(a, b):
    return lax.dot_general(a, b, (((1,), (1,)), ((), ())), preferred_element_type=F32)


def _dot_tn(a, b):
    return lax.dot_general(a, b, (((0,), (0,)), ((), ())), preferred_element_type=F32)


def _sigmoid(x):
    return pl.reciprocal(1.0 + jnp.exp(-x), approx=True)


def _rms_stats(h):
    r = lax.rsqrt(jnp.mean(h * h, axis=-1, keepdims=True) + RMS_EPS)
    return h * r, r


def _rms_bwd(xhat, r, g, dn):
    dxh = dn * g
    return r * (dxh - xhat * jnp.mean(dxh * xhat, axis=-1, keepdims=True))


GELU_K = math.sqrt(2.0 / math.pi)
GELU_C = 0.044715


def _gelu(x):
    return 0.5 * x * (1.0 + jnp.tanh(GELU_K * (x + GELU_C * x * x * x)))


def _gelu_grad(x):
    t = jnp.tanh(GELU_K * (x + GELU_C * x * x * x))
    return 0.5 * (1.0 + t) + 0.5 * x * (1.0 - t * t) * GELU_K * (1.0 + 3.0 * GELU_C * x * x)


def _for_tile_rows(i, ni, tm, n_meta, seq, fn):
    pl.when(i == 0)(lambda: fn(0, min(tm - n_meta, seq), n_meta))
    if ni > 1:
        last_lo = (ni - 1) * tm - n_meta
        pl.when(i == ni - 1)(lambda: fn(last_lo, min(seq - last_lo, tm), 0))
    if ni > 2:
        pl.when((i > 0) & (i < ni - 1))(lambda: fn(pl.multiple_of(i * tm - n_meta, SUBLANES), tm, 0))


def _ffn_fwd(h, g, wg, wu, wd, name, final=None, carry=None):
    tp, d = h.shape
    ns, f4, _ = wg.shape
    tm = _pick_tile(tp, (768, 512, 256))
    ni = tp // tm

    def body(*refs):
        refs, phases = split(refs)
        if final is None:
            h_ref, g_ref, wg_ref, wu_ref, wd_ref, ho_ref, a_ref, b_ref, n_scr, acc = refs
        else:
            (h_ref, g_ref, wg_ref, wu_ref, wd_ref, gf_ref, tg_hbm,
             ho_ref, a_ref, b_ref, n_scr, dgf_ref, loss_ref, dy_ref, acc, tg_ref, tg_sem) = refs
        i = pl.program_id(0)
        k = pl.program_id(1)
        _run_phases(phases, carry, i * ns + k, ni * ns)

        if final is not None:
            def target_rows(lo, n, at):
                return pltpu.make_async_copy(tg_hbm.at[pl.ds(lo, n), :], tg_ref.at[pl.ds(at, n), :], tg_sem)

            def fetch_target(lo, n, at):
                if at > 0:
                    tg_ref[pl.ds(0, at), :] = jnp.zeros((at, d), F32)
                if at + n < tm:
                    tg_ref[pl.ds(at + n, tm - at - n), :] = jnp.zeros((tm - at - n, d), F32)
                target_rows(lo, n, at).start()

            pl.when(k == 0)(lambda: _for_tile_rows(i, ni, tm, final[2], final[3], fetch_target))

        @pl.when(k == 0)
        def _():
            xhat, _ = _rms_stats(h_ref[...])
            n_scr[...] = (xhat * g_ref[...]).astype(BF16)
            acc[...] = jnp.zeros_like(acc)

        n = n_scr[...]
        a = _dot_nt(n, wg_ref[...])
        b = _dot_nt(n, wu_ref[...])
        a_ref[...] = a.astype(BF16)
        b_ref[...] = b.astype(BF16)
        s = (a * _sigmoid(a) * b).astype(BF16)
        acc[...] += _dot(s, wd_ref[...])

        if final is None:
            @pl.when(k == ns - 1)
            def _():
                ho_ref[...] = h_ref[...] + 0.5 * acc[...]
        else:
            n_meta, seq = final[2], final[3]

            @pl.when((i == 0) & (k == 0))
            def _():
                dgf_ref[...] = jnp.zeros_like(dgf_ref)
                loss_ref[...] = jnp.zeros_like(loss_ref)

            @pl.when(k == ns - 1)
            def _():
                _for_tile_rows(i, ni, tm, n_meta, seq, lambda lo, n, at: target_rows(lo, n, at).wait())
                h3 = h_ref[...] + 0.5 * acc[...]
                xhat, r = _rms_stats(h3)
                gf = gf_ref[...]
                row = i * tm + lax.broadcasted_iota(jnp.int32, (tm, d), 0)
                valid = (row >= n_meta) & (row < n_meta + seq)
                diff = jnp.where(valid, xhat * gf - tg_ref[...], 0.0)
                dout = diff * (1.0 / d)
                loss_ref[...] += jnp.full(loss_ref.shape, 0.5 * jnp.sum(diff * diff) * (1.0 / d), F32)
                dgf_ref[...] += jnp.sum(dout * xhat, axis=0, keepdims=True)
                dh3 = _rms_bwd(xhat, r, gf, dout)
                ho_ref[...] = dh3
                dy_ref[...] = (0.5 * dh3).astype(BF16)

    row_spec = pl.BlockSpec((tm, d), lambda i, k: (i, 0))
    vec_spec = pl.BlockSpec((1, d), lambda i, k: (0, 0))
    in_specs = [row_spec, vec_spec,
                pl.BlockSpec((None, f4, d), lambda i, k: (k, 0, 0)),
                pl.BlockSpec((None, f4, d), lambda i, k: (k, 0, 0)),
                pl.BlockSpec((None, f4, d), lambda i, k: (k, 0, 0))]
    act_spec = pl.BlockSpec((None, tm, f4), lambda i, k: (k, i, 0))
    out_specs = [row_spec, act_spec, act_spec, row_spec]
    out_shape = [jax.ShapeDtypeStruct((tp, d), F32),
                 jax.ShapeDtypeStruct((ns, tp, f4), BF16),
                 jax.ShapeDtypeStruct((ns, tp, f4), BF16),
                 jax.ShapeDtypeStruct((tp, d), BF16)]
    args = [h, g, wg, wu, wd]
    scratch = [pltpu.VMEM((tm, d), F32)]
    if final is not None:
        in_specs += [vec_spec, ANY]
        args += [final[0], final[1]]
        out_specs += [vec_spec, pl.BlockSpec((1, LANES), lambda i, k: (0, 0)), row_spec]
        out_shape += [jax.ShapeDtypeStruct((1, d), F32), jax.ShapeDtypeStruct((1, LANES), F32),
                      jax.ShapeDtypeStruct((tp, d), BF16)]
        scratch += [pltpu.VMEM((tm, d), F32), pltpu.SemaphoreType.DMA(())]
    split = _attach_carry(carry, in_specs, args, out_specs, out_shape, scratch)
    return pl.pallas_call(
        body, name=name, grid=(ni, ns), in_specs=in_specs, out_specs=out_specs, out_shape=out_shape,
        scratch_shapes=scratch, compiler_params=_params(("arbitrary", "arbitrary")),
    )(*args)


def _ffn_bwd_shard(k, ns, dn_prev, dy, n, a, b, wg, wu, wd, tail, name, carry=None, unpad=None):
    tp, d = n.shape
    f4 = wg.shape[1]
    tm = _pick_tile(tp, (768, 512, 256))
    ni = tp // tm
    first, last = k == 0, k == ns - 1
    unpad = unpad if last else None

    def body(*refs):
        refs, phases = split(refs)
        acc_in = None if first else refs.pop(0)
        if last:
            dh_ref, h_ref, g_ref = refs[:3]
        else:
            dy_ref, n_ref = refs[:2]
        refs = refs[3 if last else 2:]
        a_ref, b_ref, wg_hbm, wu_hbm, wd_hbm = refs[:5]
        refs = refs[5:]
        acc_out, dwg_hbm, dwu_hbm, dwd_hbm = refs[:4]
        rest = refs[4:]
        dg_ref = rest.pop(0) if last else None
        head_ref = rest.pop(0) if unpad else None
        wg_ref, wu_ref, wd_ref, dwg_ref, dwu_ref, dwd_ref, wsem = rest[:7]
        i = pl.program_id(0)
        _run_phases(phases, carry, i, ni)
        if unpad:
            res_ref, res_sem = rest[7:]

            def real_rows(lo, cnt, at):
                return pltpu.make_async_copy(res_ref.at[pl.ds(at, cnt), :], acc_out.at[pl.ds(lo, cnt), :], res_sem)

            def wait_tile(tile):
                _for_tile_rows(tile, ni, tm, *unpad, lambda lo, cnt, at: real_rows(lo, cnt, at).wait())

        @pl.when(i == 0)
        def _():
            loads = [pltpu.make_async_copy(src.at[k], dst, wsem.at[j])
                     for j, (src, dst) in enumerate(((wg_hbm, wg_ref), (wu_hbm, wu_ref), (wd_hbm, wd_ref)))]
            for cp in loads:
                cp.start()
            dwg_ref[...] = jnp.zeros_like(dwg_ref)
            dwu_ref[...] = jnp.zeros_like(dwu_ref)
            dwd_ref[...] = jnp.zeros_like(dwd_ref)
            if last:
                dg_ref[...] = jnp.zeros_like(dg_ref)
            for cp in loads:
                cp.wait()

        if last:
            xhat, r = _rms_stats(h_ref[...])
            n = (xhat * g_ref[...]).astype(BF16)
            dy = (0.5 * dh_ref[...]).astype(BF16)
        else:
            n = n_ref[...]
            dy = dy_ref[...]
        av = a_ref[...].astype(F32)
        bv = b_ref[...].astype(F32)
        sg = _sigmoid(av)
        silu = av * sg
        ds = _dot_nt(dy, wd_ref[...])
        da = (ds * bv * (sg * (1.0 + av * (1.0 - sg)))).astype(BF16)
        db = (ds * silu).astype(BF16)
        s = (silu * bv).astype(BF16)
        dwd_ref[...] += _dot_tn(s, dy)
        dwg_ref[...] += _dot_tn(da, n)
        dwu_ref[...] += _dot_tn(db, n)
        dn = _dot(da, wg_ref[...]) + _dot(db, wu_ref[...])
        if not first:
            dn = dn + acc_in[...]
        if last:
            dg_ref[...] += jnp.sum(dn * xhat, axis=0, keepdims=True)
            dh_in = dh_ref[...] + _rms_bwd(xhat, r, g_ref[...], dn)
            if unpad:
                pl.when(i > 0)(lambda: wait_tile(i - 1))
                res_ref[...] = dh_in

                @pl.when(i == 0)
                def _():
                    head_ref[...] = res_ref[pl.ds(0, unpad[0]), :]

                _for_tile_rows(i, ni, tm, *unpad, lambda lo, cnt, at: real_rows(lo, cnt, at).start())
                pl.when(i == ni - 1)(lambda: wait_tile(i))
            else:
                acc_out[...] = dh_in
        else:
            acc_out[...] = dn

        @pl.when(i == ni - 1)
        def _():
            stores = []
            for j, (acc_ref, stage_ref, out_hbm) in enumerate(((dwg_ref, wg_ref, dwg_hbm), (dwu_ref, wu_ref, dwu_hbm),
                                                              (dwd_ref, wd_ref, dwd_hbm))):
                stage_ref[...] = acc_ref[...].astype(BF16)
                stores.append(pltpu.make_async_copy(stage_ref, out_hbm, wsem.at[j]))
                stores[-1].start()
            for cp in stores:
                cp.wait()

    row_spec = pl.BlockSpec((tm, d), lambda i: (i, 0))
    vec_spec = pl.BlockSpec((1, d), lambda i: (0, 0))
    act_spec = pl.BlockSpec((None, tm, f4), lambda i: (k, i, 0))
    in_specs = [act_spec, act_spec, ANY, ANY, ANY]
    args = [a, b, wg, wu, wd]
    if last:
        in_specs = [row_spec, row_spec, vec_spec] + in_specs
        args = list(tail) + args
    else:
        in_specs = [row_spec, row_spec] + in_specs
        args = [dy, n] + args
    if not first:
        in_specs.insert(0, row_spec)
        args.insert(0, dn_prev)
    out_specs = [row_spec, ANY, ANY, ANY]
    out_shape = [jax.ShapeDtypeStruct((tp, d), F32)] + [jax.ShapeDtypeStruct((f4, d), BF16)] * 3
    scratch = [pltpu.VMEM((f4, d), BF16)] * 3 + [pltpu.VMEM((f4, d), F32)] * 3 + [pltpu.SemaphoreType.DMA((3,))]
    if last:
        out_specs.append(vec_spec)
        out_shape.append(jax.ShapeDtypeStruct((1, d), F32))
    if unpad:
        out_specs[0] = ANY
        out_shape[0] = jax.ShapeDtypeStruct((unpad[1], d), F32)
        out_specs.append(pl.BlockSpec((unpad[0], d), lambda i: (0, 0)))
        out_shape.append(jax.ShapeDtypeStruct((unpad[0], d), F32))
        scratch += [pltpu.VMEM((tm, d), F32), pltpu.SemaphoreType.DMA(())]
    n_host = len(out_shape)
    split = _attach_carry(carry, in_specs, args, out_specs, out_shape, scratch)
    outs = pl.pallas_call(
        body, name=f"{name}_{k}", grid=(ni,), in_specs=in_specs, out_specs=out_specs, out_shape=out_shape,
        scratch_shapes=scratch, compiler_params=_params(("arbitrary",)),
    )(*args)
    return outs[:n_host], outs[n_host:]


def _ffn_bwd(dh_out, dy, h_in, n, g, a, b, wg, wu, wd, name, chain=None, unpad=None):
    ns = wg.shape[0]
    acc, dwg, dwu, dwd = None, [], [], []
    for k in range(ns):
        carry = chain.carry() if chain is not None else None
        outs, carried = _ffn_bwd_shard(k, ns, acc, dy, n, a, b, wg, wu, wd, (dh_out, h_in, g), name, carry, unpad)
        if chain is not None:
            chain.feed(carried)
        acc = outs[0]
        dwg.append(outs[1])
        dwu.append(outs[2])
        dwd.append(outs[3])
    return (acc, dwg, dwu, dwd) + tuple(outs[4:])


def _win_fwd(h, g, w_in, carry=None):
    tp, d = h.shape
    ns = w_in.shape[0]
    tm = _pick_tile(tp, (768, 512, 256))
    ni = tp // tm

    def body(*refs):
        (h_ref, g_ref, w_ref, u_ref, p_ref), phases = split(refs)
        _run_phases(phases, carry, pl.program_id(0), ni)
        xhat, _ = _rms_stats(h_ref[...])
        u = (xhat * g_ref[...]).astype(BF16)
        u_ref[...] = u
        for k in range(ns):
            p_ref[k] = _dot(u, w_ref[k]).astype(BF16)

    in_specs = [pl.BlockSpec((tm, d), lambda i: (i, 0)),
                pl.BlockSpec((1, d), lambda i: (0, 0)),
                pl.BlockSpec((ns, d, d), lambda i: (0, 0, 0))]
    out_specs = [pl.BlockSpec((tm, d), lambda i: (i, 0)),
                 pl.BlockSpec((ns, tm, d), lambda i: (0, i, 0))]
    out_shape = [jax.ShapeDtypeStruct((tp, d), BF16), jax.ShapeDtypeStruct((ns, tp, d), BF16)]
    args, scratch = [h, g, w_in], []
    split = _attach_carry(carry, in_specs, args, out_specs, out_shape, scratch)
    return pl.pallas_call(
        body, name="win_fwd", grid=(ni,), in_specs=in_specs, out_specs=out_specs, out_shape=out_shape,
        scratch_shapes=scratch, compiler_params=_params(("arbitrary",)),
    )(*args)


def _win_bwd_shard(k, ns, du_prev, dpb, dug, u, w_in, h1, g, dh2):
    tp, d = h1.shape
    dh = d // 2
    tm = _pick_tile(tp, (768, 512, 256))
    first, last = k == 0, k == ns - 1

    def body(*refs):
        refs = list(refs)
        acc_in = None if first else refs.pop(0)
        dug_ref = refs.pop(0) if first else None
        dp_ref, u_ref, w_ref = refs[:3]
        refs = refs[3:]
        if last:
            h_ref, g_ref, dh2_ref, acc_out, dw_ref, dg_ref, dy_ref, dw_acc = refs
        else:
            acc_out, dw_ref, dw_acc = refs
        i = pl.program_id(0)

        @pl.when(i == 0)
        def _():
            dw_acc[...] = jnp.zeros_like(dw_acc)
            if last:
                dg_ref[...] = jnp.zeros_like(dg_ref)

        dp = dp_ref[...]
        if first:
            dp = jnp.concatenate([dug_ref[...], dp[:, dh:]], axis=1)
        dw_acc[...] += _dot_tn(u_ref[...], dp)
        du = _dot_nt(dp, w_ref[...])
        if not first:
            du = du + acc_in[...]
        if last:
            xhat, r = _rms_stats(h_ref[...])
            dg_ref[...] += jnp.sum(du * xhat, axis=0, keepdims=True)
            dh1 = dh2_ref[...] + _rms_bwd(xhat, r, g_ref[...], du)
            acc_out[...] = dh1
            dy_ref[...] = (0.5 * dh1).astype(BF16)
        else:
            acc_out[...] = du

        @pl.when(i == tp // tm - 1)
        def _():
            dw_ref[...] = dw_acc[...].astype(BF16)

    row_spec = pl.BlockSpec((tm, d), lambda i: (i, 0))
    vec_spec = pl.BlockSpec((1, d), lambda i: (0, 0))
    in_specs = [pl.BlockSpec((None, tm, d), lambda i: (k, i, 0)), row_spec,
                pl.BlockSpec((None, d, d), lambda i: (k, 0, 0))]
    args = [dpb, u, w_in]
    if first:
        in_specs.insert(0, pl.BlockSpec((tm, dh), lambda i: (i, 0)))
        args.insert(0, dug)
    else:
        in_specs.insert(0, row_spec)
        args.insert(0, du_prev)
    out_specs = [row_spec, pl.BlockSpec((d, d), lambda i: (0, 0))]
    out_shape = [jax.ShapeDtypeStruct((tp, d), F32), jax.ShapeDtypeStruct((d, d), BF16)]
    if last:
        in_specs += [row_spec, vec_spec, row_spec]
        args += [h1, g, dh2]
        out_specs += [vec_spec, row_spec]
        out_shape += [jax.ShapeDtypeStruct((1, d), F32), jax.ShapeDtypeStruct((tp, d), BF16)]
    return pl.pallas_call(
        body, name=f"win_bwd_{k}", grid=(tp // tm,), in_specs=in_specs, out_specs=out_specs,
        out_shape=out_shape, scratch_shapes=[pltpu.VMEM((d, d), F32)],
        compiler_params=_params(("arbitrary",)),
    )(*args)


def _win_bwd(dpb, dug, u, w_in, h1, g, dh2):
    ns = w_in.shape[0]
    acc, dws = None, []
    for k in range(ns):
        outs = _win_bwd_shard(k, ns, acc, dpb, dug, u, w_in, h1, g, dh2)
        acc = outs[0]
        dws.append(outs[1])
    return acc, dws, outs[2], outs[3]


def _cmul(ar, ai, br, bi):
    return ar * br - ai * bi, ar * bi + ai * br


def _scan_rows(j, sub):
    return pl.ds(j * SCAN_SEQS, SCAN_SEQS)


def _permute_rows(src_ref, dst_ref, sub):
    for j in range(sub):
        dst_ref[pl.ds(j * SCAN_SEQS, SCAN_SEQS), :] = src_ref[pl.ds(j, SCAN_SEQS, stride=sub), :]


def _unpermute_rows(src_ref, dst_ref, sub):
    for j in range(sub):
        dst_ref[pl.ds(j, SCAN_SEQS, stride=sub), :] = src_ref[pl.ds(j * SCAN_SEQS, SCAN_SEQS), :]


def _local_scan(x_ref, lr, li, w, sub, reverse):
    hr = jnp.zeros((SCAN_SEQS, w), F32)
    hi = jnp.zeros((SCAN_SEQS, w), F32)
    order = range(sub - 1, -1, -1) if reverse else range(sub)
    for j in order:
        xr = x_ref[_scan_rows(j, sub), pl.ds(0, w)]
        xi = x_ref[_scan_rows(j, sub), pl.ds(w, w)]
        if reverse:
            hr, hi = lr * hr + li * hi + xr, lr * hi - li * hr + xi
        else:
            hr, hi = lr * hr - li * hi + xr, lr * hi + li * hr + xi
        x_ref[_scan_rows(j, sub), pl.ds(0, w)] = hr
        x_ref[_scan_rows(j, sub), pl.ds(w, w)] = hi
    return hr, hi


def _entering_states(er, ei, fr, fi, pow_ref, w, sub, reverse):
    lane = lax.broadcasted_iota(jnp.int32, (SCAN_SEQS, w), 0)
    if reverse:
        edge, shift1 = SCAN_SEQS - 1, SCAN_SEQS - 1
    else:
        edge, shift1 = 0, 1
    zr = jnp.where(lane == edge, pltpu.roll(fr, shift1, 0), pltpu.roll(er, shift1, 0))
    zi = jnp.where(lane == edge, pltpu.roll(fi, shift1, 0), pltpu.roll(ei, shift1, 0))
    for m in range(SCAN_SEQS.bit_length() - 1):
        step, row = 1 << m, sub - 1 + m
        ar = pow_ref[pl.ds(row, 1), pl.ds(0, w)]
        ai = pow_ref[pl.ds(row, 1), pl.ds(w, w)]
        if reverse:
            ai = -ai
            keep = lane < SCAN_SEQS - step
            sr = jnp.where(keep, pltpu.roll(zr, SCAN_SEQS - step, 0), 0.0)
            si = jnp.where(keep, pltpu.roll(zi, SCAN_SEQS - step, 0), 0.0)
        else:
            keep = lane >= step
            sr = jnp.where(keep, pltpu.roll(zr, step, 0), 0.0)
            si = jnp.where(keep, pltpu.roll(zi, step, 0), 0.0)
        pr, pi = _cmul(ar, ai, sr, si)
        zr, zi = zr + pr, zi + pi
    ar = pow_ref[pl.ds(sub - 1, 1), pl.ds(0, w)]
    ai = pow_ref[pl.ds(sub - 1, 1), pl.ds(w, w)]
    if reverse:
        ai = -ai
    pr, pi = _cmul(ar, ai, zr, zi)
    return zr, zi, er + pr, ei + pi


def _scan_fwd(p, mb, mc, powt, dskip):
    _, tp, d = p.shape
    nb, cb, w2 = mb.shape
    w = w2 // 2
    q = SCAN_TILE
    sub = q // SCAN_SEQS
    nt = tp // q
    ds = d // 2

    def body(ug_ref, mb_ref, mc_ref, pow_ref, d_ref, y_ref, bnd_ref, x_scr, carry, nat, perm):
        t = pl.program_id(1)

        @pl.when(t == 0)
        def _():
            carry[...] = jnp.zeros_like(carry)

        ugf = ug_ref[...].astype(F32)
        nat[...] = ugf
        _permute_rows(nat, perm, sub)
        x_scr[...] = _dot(perm[...].astype(BF16), mb_ref[...])
        lr = jnp.broadcast_to(pow_ref[pl.ds(0, 1), pl.ds(0, w)], (SCAN_SEQS, w))
        li = jnp.broadcast_to(pow_ref[pl.ds(0, 1), pl.ds(w, w)], (SCAN_SEQS, w))
        er, ei = _local_scan(x_scr, lr, li, w, sub, False)
        zr, zi, fr, fi = _entering_states(er, ei, carry[:, pl.ds(0, w)], carry[:, pl.ds(w, w)],
                                          pow_ref, w, sub, False)
        carry[:, pl.ds(0, w)] = fr
        carry[:, pl.ds(w, w)] = fi
        bnd_ref[:, pl.ds(0, w)] = fr
        bnd_ref[:, pl.ds(w, w)] = fi
        for j in range(sub):
            pr = pow_ref[pl.ds(j, 1), pl.ds(0, w)]
            pi = pow_ref[pl.ds(j, 1), pl.ds(w, w)]
            cr, ci = _cmul(pr, pi, zr, zi)
            x_scr[_scan_rows(j, sub), pl.ds(0, w)] += cr
            x_scr[_scan_rows(j, sub), pl.ds(w, w)] += ci
        hb = x_scr[...].astype(BF16)
        perm[...] = _dot_nt(hb, mc_ref[...])
        _unpermute_rows(perm, nat, sub)
        y_ref[...] = nat[...] + d_ref[...] * ugf

    in_specs = [pl.BlockSpec((None, q, cb), lambda b, t: (0, t, b)),
                pl.BlockSpec((None, cb, w2), lambda b, t: (b, 0, 0)),
                pl.BlockSpec((None, cb, w2), lambda b, t: (b, 0, 0)),
                pl.BlockSpec((None, powt.shape[1], w2), lambda b, t: (b, 0, 0)),
                pl.BlockSpec((1, cb), lambda b, t: (0, b))]
    out_specs = [pl.BlockSpec((q, cb), lambda b, t: (t, b)),
                 pl.BlockSpec((None, None, SCAN_SEQS, w2), lambda b, t: (b, t, 0, 0))]
    out_shape = [jax.ShapeDtypeStruct((tp, ds), F32), jax.ShapeDtypeStruct((nb, nt, SCAN_SEQS, w2), F32)]
    scratch = [pltpu.VMEM((q, w2), F32), pltpu.VMEM((SCAN_SEQS, w2), F32),
               pltpu.VMEM((q, cb), F32), pltpu.VMEM((q, cb), F32)]
    return pl.pallas_call(
        body, name="s5_scan_fwd", grid=(nb, nt), in_specs=in_specs, out_specs=out_specs,
        out_shape=out_shape, scratch_shapes=scratch, compiler_params=_params(("arbitrary", "arbitrary")),
    )(p, mb, mc, powt, dskip)


def _scan_bwd(p, dy, mb, mc, powt, dskip, bnd):
    _, tp, d = p.shape
    nb, cb, w2 = mb.shape
    w = w2 // 2
    q = SCAN_TILE
    sub = q // SCAN_SEQS
    nt = tp // q
    ds = d // 2

    def body(ug_ref, dy_ref, mb_ref, mc_ref, pow_ref, d_ref, bnd_ref,
             dug_ref, dmb_ref, dmc_ref, dlam_ref, dd_ref, x_scr, y_scr, gcarry, nat, perm):
        t = pl.program_id(1)
        tt = nt - 1 - t

        @pl.when(t == 0)
        def _():
            gcarry[...] = jnp.zeros_like(gcarry)
            dmb_ref[...] = jnp.zeros_like(dmb_ref)
            dmc_ref[...] = jnp.zeros_like(dmc_ref)
            dlam_ref[...] = jnp.zeros_like(dlam_ref)
            dd_ref[...] = jnp.zeros_like(dd_ref)

        ugf = ug_ref[...].astype(F32)
        dyf = dy_ref[...].astype(F32)
        dd_ref[...] += jnp.sum((dyf * ugf).reshape(q // SUBLANES, SUBLANES, cb), axis=0)
        nat[...] = ugf
        _permute_rows(nat, perm, sub)
        ug = perm[...].astype(BF16)
        nat[...] = dyf
        _permute_rows(nat, perm, sub)
        dyb = perm[...].astype(BF16)
        lr = jnp.broadcast_to(pow_ref[pl.ds(0, 1), pl.ds(0, w)], (SCAN_SEQS, w))
        li = jnp.broadcast_to(pow_ref[pl.ds(0, 1), pl.ds(w, w)], (SCAN_SEQS, w))

        x_scr[...] = _dot(ug, mb_ref[...])
        er, ei = _local_scan(x_scr, lr, li, w, sub, False)
        first = tt == 0
        pfr = jnp.where(first, 0.0, bnd_ref[:, pl.ds(0, w)])
        pfi = jnp.where(first, 0.0, bnd_ref[:, pl.ds(w, w)])
        hzr, hzi, _, _ = _entering_states(er, ei, pfr, pfi, pow_ref, w, sub, False)
        for j in range(sub):
            pr = pow_ref[pl.ds(j, 1), pl.ds(0, w)]
            pi = pow_ref[pl.ds(j, 1), pl.ds(w, w)]
            cr, ci = _cmul(pr, pi, hzr, hzi)
            x_scr[_scan_rows(j, sub), pl.ds(0, w)] += cr
            x_scr[_scan_rows(j, sub), pl.ds(w, w)] += ci

        y_scr[...] = _dot(dyb, mc_ref[...])
        er, ei = _local_scan(y_scr, lr, li, w, sub, True)
        gzr, gzi, fr, fi = _entering_states(er, ei, gcarry[:, pl.ds(0, w)], gcarry[:, pl.ds(w, w)],
                                            pow_ref, w, sub, True)
        gcarry[:, pl.ds(0, w)] = fr
        gcarry[:, pl.ds(w, w)] = fi
        accr = jnp.zeros((SCAN_SEQS, w), F32)
        acci = jnp.zeros((SCAN_SEQS, w), F32)
        for j in range(sub):
            pr = pow_ref[pl.ds(sub - 1 - j, 1), pl.ds(0, w)]
            pi = pow_ref[pl.ds(sub - 1 - j, 1), pl.ds(w, w)]
            cr, ci = _cmul(pr, -pi, gzr, gzi)
            gr = y_scr[_scan_rows(j, sub), pl.ds(0, w)] + cr
            gi = y_scr[_scan_rows(j, sub), pl.ds(w, w)] + ci
            y_scr[_scan_rows(j, sub), pl.ds(0, w)] = gr
            y_scr[_scan_rows(j, sub), pl.ds(w, w)] = gi
            if j == 0:
                hpr, hpi = hzr, hzi
            else:
                hpr = x_scr[_scan_rows(j - 1, sub), pl.ds(0, w)]
                hpi = x_scr[_scan_rows(j - 1, sub), pl.ds(w, w)]
            accr += hpr * gr + hpi * gi
            acci += hpr * gi - hpi * gr
        dlam_ref[:, pl.ds(0, w)] += accr
        dlam_ref[:, pl.ds(w, w)] += acci

        hb = x_scr[...].astype(BF16)
        gb = y_scr[...].astype(BF16)
        dmc_ref[...] += _dot_tn(dyb, hb)
        dmb_ref[...] += _dot_tn(ug, gb)
        perm[...] = _dot_nt(gb, mb_ref[...])
        _unpermute_rows(perm, nat, sub)
        dug_ref[...] = (nat[...] + d_ref[...] * dyf).astype(BF16)

    blk = lambda b, t: (b, 0, 0)
    return pl.pallas_call(
        body, name="s5_scan_bwd", grid=(nb, nt),
        in_specs=[pl.BlockSpec((None, q, cb), lambda b, t: (0, nt - 1 - t, b)),
                  pl.BlockSpec((q, cb), lambda b, t: (nt - 1 - t, b)),
                  pl.BlockSpec((None, cb, w2), blk),
                  pl.BlockSpec((None, cb, w2), blk),
                  pl.BlockSpec((None, powt.shape[1], w2), blk),
                  pl.BlockSpec((1, cb), lambda b, t: (0, b)),
                  pl.BlockSpec((None, None, SCAN_SEQS, w2),
                               lambda b, t: (b, jnp.maximum(nt - 2 - t, 0), 0, 0))],
        out_specs=[pl.BlockSpec((q, cb), lambda b, t: (nt - 1 - t, b)),
                   pl.BlockSpec((None, cb, w2), blk),
                   pl.BlockSpec((None, cb, w2), blk),
                   pl.BlockSpec((None, SCAN_SEQS, w2), blk),
                   pl.BlockSpec((SUBLANES, cb), lambda b, t: (0, b))],
        out_shape=[jax.ShapeDtypeStruct((tp, ds), BF16),
                   jax.ShapeDtypeStruct((nb, cb, w2), F32),
                   jax.ShapeDtypeStruct((nb, cb, w2), F32),
                   jax.ShapeDtypeStruct((nb, SCAN_SEQS, w2), F32),
                   jax.ShapeDtypeStruct((SUBLANES, ds), F32)],
        scratch_shapes=[pltpu.VMEM((q, w2), F32), pltpu.VMEM((q, w2), F32),
                        pltpu.VMEM((SCAN_SEQS, w2), F32), pltpu.VMEM((q, cb), F32), pltpu.VMEM((q, cb), F32)],
        compiler_params=_params(("arbitrary", "arbitrary")),
    )(p, dy, mb, mc, powt, dskip, bnd)


HALO = 16


def _mix_tile(ys5, p0, p1, p2, p3, prev_cin, cw, bgate, wglu, wco, d):
    dh = d // 2
    tm = ys5.shape[0]
    v = p0[:, dh:].astype(F32)
    gbr = p1[:, :dh].astype(F32)
    gcr = p1[:, dh:].astype(F32)
    gact = _gelu(ys5).astype(BF16)
    z = _dot(gact, wglu)
    z1, z2 = z[:, :d], z[:, d:]
    sg = _sigmoid(z2)
    y_ssm = z1 * sg
    cin = gcr * v
    ext = jnp.concatenate([cin, prev_cin], axis=0)
    r1 = pltpu.roll(ext, 1, 0)[:tm]
    r2 = pltpu.roll(ext, 2, 0)[:tm]
    cv = cw[2] * cin + cw[1] * r1 + cw[0] * r2
    cg = (gbr * cv).astype(BF16)
    y_conv = _dot(cg, wco)
    g_s = _sigmoid(p2.astype(F32) + bgate[:, :d])
    g_c = _sigmoid(p3.astype(F32) + bgate[:, d:])
    mixed = g_s * y_ssm + g_c * y_conv
    return dict(v=v, gb=gbr, gc=gcr, gact=gact, z1=z1, sg=sg, y_ssm=y_ssm, cin=cin, r1=r1, r2=r2,
                cv=cv, cg=cg, y_conv=y_conv, g_s=g_s, g_c=g_c, mixed=mixed)


def _mix_fwd(h1, ys5, p, cw, bgate, wglu, wco, wo, carry=None):
    tp, d = h1.shape
    dh = d // 2
    tm = ROW_ALIGN
    ni = tp // tm

    def body(*refs):
        refs, phases = split(refs)
        (h_ref, y_ref, p0_ref, p1_ref, p2_ref, p3_ref, cw_ref, bg_ref, wglu_ref, wco_ref, wo_ref,
         o_ref, prev) = refs
        _run_phases(phases, carry, pl.program_id(0), ni)

        @pl.when(pl.program_id(0) == 0)
        def _():
            prev[...] = jnp.zeros_like(prev)

        cw = [cw_ref[pl.ds(t, 1), :] for t in range(3)]
        f = _mix_tile(y_ref[...], p0_ref[...], p1_ref[...], p2_ref[...], p3_ref[...], prev[...],
                      cw, bg_ref[...], wglu_ref[...], wco_ref[...], d)
        prev[...] = f["cin"][tm - HALO:, :]
        o_ref[...] = h_ref[...] + _dot(f["mixed"].astype(BF16), wo_ref[...])

    row = pl.BlockSpec((tm, d), lambda i: (i, 0))
    full = lambda a: pl.BlockSpec(a.shape, lambda i: (0,) * a.ndim)
    pk = lambda k: pl.BlockSpec((None, tm, d), lambda i, k=k: (k, i, 0))
    in_specs = [row, pl.BlockSpec((tm, dh), lambda i: (i, 0)), pk(0), pk(1), pk(2), pk(3),
                full(cw), full(bgate), full(wglu), full(wco), full(wo)]
    out_specs, out_shape = [row], [jax.ShapeDtypeStruct((tp, d), F32)]
    args, scratch = [h1, ys5, p, p, p, p, cw, bgate, wglu, wco, wo], [pltpu.VMEM((HALO, dh), F32)]
    split = _attach_carry(carry, in_specs, args, out_specs, out_shape, scratch)
    return pl.pallas_call(
        body, name="mix_fwd", grid=(ni,), in_specs=in_specs, out_specs=out_specs, out_shape=out_shape,
        scratch_shapes=scratch, compiler_params=_params(("arbitrary",)),
    )(*args)


def _mix_bwd(dh2, ys5, p, cw, bgate, wglu, wco, wo):
    tp, d = dh2.shape
    dh = d // 2
    tm = ROW_ALIGN
    ni = tp // tm
    hb = tm // HALO

    def body(dh_ref, y_ref, p0_ref, p1_ref, p2_ref, p3_ref, h0_ref, h1_ref,
             cw_ref, bg_ref, wglu_ref, wco_ref, wo_ref,
             dys_ref, dpb_ref, dwo_ref, dwglu_ref, dwco_ref, dcw_ref, dbg_ref, nxt):
        i = pl.program_id(0)
        tt = ni - 1 - i

        @pl.when(i == 0)
        def _():
            nxt[...] = jnp.zeros_like(nxt)
            dwo_ref[...] = jnp.zeros_like(dwo_ref)
            dwglu_ref[...] = jnp.zeros_like(dwglu_ref)
            dwco_ref[...] = jnp.zeros_like(dwco_ref)
            dcw_ref[...] = jnp.zeros_like(dcw_ref)
            dbg_ref[...] = jnp.zeros_like(dbg_ref)

        cw = [cw_ref[pl.ds(t, 1), :] for t in range(3)]
        prev_cin = h1_ref[:, dh:].astype(F32) * h0_ref[:, dh:].astype(F32)
        prev_cin = jnp.where(tt == 0, 0.0, prev_cin)
        ys5 = y_ref[...]
        f = _mix_tile(ys5, p0_ref[...], p1_ref[...], p2_ref[...], p3_ref[...], prev_cin,
                      cw, bg_ref[...], wglu_ref[...], wco_ref[...], d)
        dhb = dh_ref[...].astype(BF16)
        dmixed = _dot_nt(dhb, wo_ref[...])
        dwo_ref[...] += _dot_tn(f["mixed"].astype(BF16), dhb)

        g_s, g_c, sg = f["g_s"], f["g_c"], f["sg"]
        dy_ssm = dmixed * g_s
        dy_conv = dmixed * g_c
        dp2 = dmixed * f["y_ssm"] * g_s * (1.0 - g_s)
        dp3 = dmixed * f["y_conv"] * g_c * (1.0 - g_c)
        dbg_ref[:, pl.ds(0, d)] += jnp.sum(dp2, axis=0, keepdims=True)
        dbg_ref[:, pl.ds(d, d)] += jnp.sum(dp3, axis=0, keepdims=True)

        dz = jnp.concatenate([dy_ssm * sg, dy_ssm * f["z1"] * sg * (1.0 - sg)], axis=1).astype(BF16)
        dwglu_ref[...] += _dot_tn(f["gact"], dz)
        dys_ref[...] = (_dot_nt(dz, wglu_ref[...]) * _gelu_grad(ys5)).astype(BF16)

        dycb = dy_conv.astype(BF16)
        dwco_ref[...] += _dot_tn(f["cg"], dycb)
        dcg = _dot_nt(dycb, wco_ref[...])
        dgb = dcg * f["cv"]
        dcv = dcg * f["gb"]
        ext = jnp.concatenate([dcv, nxt[...]], axis=0)
        n1 = pltpu.roll(ext, tm + HALO - 1, 0)[:tm]
        n2 = pltpu.roll(ext, tm + HALO - 2, 0)[:tm]
        nxt[...] = dcv[:HALO, :]
        dcin = cw[2] * dcv + cw[1] * n1 + cw[0] * n2
        dcw_ref[pl.ds(0, 1), :] += jnp.sum(dcv * f["r2"], axis=0, keepdims=True)
        dcw_ref[pl.ds(1, 1), :] += jnp.sum(dcv * f["r1"], axis=0, keepdims=True)
        dcw_ref[pl.ds(2, 1), :] += jnp.sum(dcv * f["cin"], axis=0, keepdims=True)
        dgc = dcin * f["v"]
        dv = dcin * f["gc"]
        dpb_ref[0] = jnp.concatenate([jnp.zeros_like(dv), dv], axis=1).astype(BF16)
        dpb_ref[1] = jnp.concatenate([dgb, dgc], axis=1).astype(BF16)
        dpb_ref[2] = dp2.astype(BF16)
        dpb_ref[3] = dp3.astype(BF16)

    rev = lambda i: ni - 1 - i
    row = pl.BlockSpec((tm, d), lambda i: (rev(i), 0))
    half = pl.BlockSpec((tm, dh), lambda i: (rev(i), 0))
    full = lambda a: pl.BlockSpec(a.shape, lambda i: (0,) * a.ndim)
    pk = lambda k: pl.BlockSpec((None, tm, d), lambda i, k=k: (k, rev(i), 0))
    halo = lambda k: pl.BlockSpec((None, HALO, d), lambda i, k=k: (k, jnp.maximum(rev(i) * hb - 1, 0), 0))
    acc = lambda shape: pl.BlockSpec(shape, lambda i: (0,) * len(shape))
    return pl.pallas_call(
        body, name="mix_bwd", grid=(ni,),
        in_specs=[row, half, pk(0), pk(1), pk(2), pk(3), halo(0), halo(1),
                  full(cw), full(bgate), full(wglu), full(wco), full(wo)],
        out_specs=[half, pl.BlockSpec((4, tm, d), lambda i: (0, rev(i), 0)),
                   acc((d, d)), acc((dh, 2 * d)), acc((dh, d)), acc((SUBLANES, dh)), acc((1, 2 * d))],
        out_shape=[jax.ShapeDtypeStruct((tp, dh), BF16), jax.ShapeDtypeStruct((4, tp, d), BF16),
                   jax.ShapeDtypeStruct((d, d), F32), jax.ShapeDtypeStruct((dh, 2 * d), F32),
                   jax.ShapeDtypeStruct((dh, d), F32), jax.ShapeDtypeStruct((SUBLANES, dh), F32),
                   jax.ShapeDtypeStruct((1, 2 * d), F32)],
        scratch_shapes=[pltpu.VMEM((HALO, dh), F32)],
        compiler_params=_params(("arbitrary",)),
    )(dh2, ys5, p, p, p, p, p, p, cw, bgate, wglu, wco, wo)


ANY = pl.BlockSpec(memory_space=pl.ANY)


def _position():
    return lax.axis_index("x"), lax.axis_index("y"), lax.axis_index("c")


def _remote(src, dst, ssem, rsem, dev):
    return pltpu.make_async_remote_copy(src_ref=src, dst_ref=dst, send_sem=ssem, recv_sem=rsem,
                                        device_id=dev, device_id_type=MESH)


def _cast_pieces(ws, pos):
    n = len(ws)

    def body(pos_ref, *refs):
        for w_ref, o_ref in zip(refs[:n], refs[n:]):
            o_ref[...] = w_ref[...].astype(BF16)

    halves = [(w.shape[0] // 2, w.shape[1]) for w in ws]
    return pl.pallas_call(
        body, name="cast_pieces",
        grid_spec=pltpu.PrefetchScalarGridSpec(
            num_scalar_prefetch=1, grid=(1,),
            in_specs=[pl.BlockSpec(hs, lambda i, pos: (pos[2], 0)) for hs in halves],
            out_specs=[pl.BlockSpec(hs, lambda i, pos: (0, 0)) for hs in halves]),
        out_shape=[jax.ShapeDtypeStruct(hs, BF16) for hs in halves],
        compiler_params=_params(("arbitrary",)),
    )(pos, *ws)


class _Carry:
    def __init__(self, name, arrays, out_shapes, nsem, nlsem, make, fracs):
        self.name, self.arrays, self.out_shapes = name, list(arrays), list(out_shapes)
        self.nsem, self.nlsem, self.make, self.fracs = nsem, max(nlsem, 1), make, fracs


def _carry_scratch(carry):
    return [pltpu.SemaphoreType.DMA((carry.nsem,)), pltpu.SemaphoreType.DMA((carry.nsem,)),
            pltpu.SemaphoreType.DMA((carry.nlsem,))]


def _run_carry(carry):
    na, no = len(carry.arrays), len(carry.out_shapes)

    def body(*refs):
        for phase in carry.make(refs[:na], refs[na:na + no], *refs[na + no:]):
            phase()

    return pl.pallas_call(
        body, name=carry.name, in_specs=[ANY] * na, out_specs=[ANY] * no, out_shape=carry.out_shapes,
        scratch_shapes=_carry_scratch(carry),
    )(*carry.arrays)


def _attach_carry(carry, in_specs, args, out_specs, out_shape, scratch):
    nhi, nho, nhs = len(in_specs), len(out_specs), len(scratch)
    if carry is None:
        return lambda refs: (list(refs), [])
    na, no = len(carry.arrays), len(carry.out_shapes)
    in_specs += [ANY] * na
    args += carry.arrays
    out_specs += [ANY] * no
    out_shape += carry.out_shapes
    scratch += _carry_scratch(carry)

    def split(refs):
        refs = list(refs)
        o = nhi + na
        host = refs[:nhi] + refs[o:o + nho] + refs[o + nho + no:o + nho + no + nhs]
        sems = refs[o + nho + no + nhs:]
        return host, carry.make(refs[nhi:o], refs[o + nho:o + nho + no], *sems)

    return split


def _run_phases(phases, carry, step, total):
    for phase, frac in zip(phases, carry.fracs if carry is not None else ()):
        pl.when(step == int(round(frac * (total - 1))))(phase)


def _allgather_carry(name, pieces, smalls, rows=None):
    n, ns = len(pieces), len(smalls)
    per = 14
    n_big = per * n

    def make(ins, outs, ssem, rsem, lsem):
        pin, sin = ins[:n], ins[n:n + ns]
        wall, sall = outs[:n], outs[n:n + ns]

        def row_copies():
            n_meta, seq = smalls[0].shape[0], rows[0].shape[0]
            h0 = outs[n + ns]
            cps = [pltpu.make_async_copy(ins[n + ns], h0.at[pl.ds(n_meta, seq), :], lsem.at[2 * n + ns])]
            if rows[1].shape[0]:
                cps.append(pltpu.make_async_copy(ins[n + ns + 1], h0.at[pl.ds(n_meta + seq, rows[1].shape[0]), :],
                                                 lsem.at[2 * n + ns + 1]))
            return cps

        def meta_copies():
            n_meta, mc = smalls[0].shape
            h0 = outs[n + ns]
            return [pltpu.make_async_copy(sall[0].at[k], h0.at[pl.ds(0, n_meta), pl.ds(k * mc, mc)],
                                          lsem.at[2 * n + ns + 2 + k]) for k in range(4)]

        x, y, c = _position()
        xnb, ynb, sib = (1 - x, y, c), (x, 1 - y, c), (x, y, 1 - c)
        chips = [(1 - x, y), (x, 1 - y), (1 - x, 1 - y)]
        r4 = [p.shape[0] // 2 for p in pin]
        own = lambda i, h: pin[i].at[pl.ds(h * r4[i], r4[i]), :]
        slot = lambda i, xx, yy, cc, h: wall[i].at[xx, yy, cc, h]
        cp = lambda src, dst, s, dev: _remote(src, dst, ssem.at[s], rsem.at[s], dev)
        to_sib = lambda i, xx, yy, h: cp(slot(i, xx, yy, c, h), slot(i, xx, yy, c, h),
                                         per * i + 6 + 4 * xx + 2 * yy + h, sib)

        def local():
            cps = [pltpu.make_async_copy(own(i, h), slot(i, x, y, c, h), lsem.at[2 * i + h])
                   for i in range(n) for h in range(2)]
            return cps + [pltpu.make_async_copy(sin[i], sall[i].at[2 * x + y], lsem.at[2 * n + i])
                          for i in range(ns)]

        def small(px, py, j, i, landing):
            s = n_big + j * ns + i
            return cp(sin[i], sall[i].at[landing], s, (px, py, c))

        def first_hop():
            for lc in local() + (row_copies() if rows else []):
                lc.start()
            for j, (px, py) in enumerate(chips):
                for i in range(ns):
                    small(px, py, j, i, 2 * x + y).start()
            for i in range(n):
                cp(own(i, 0), slot(i, x, y, c, 0), per * i, xnb).start()
                cp(own(i, 1), slot(i, x, y, c, 1), per * i + 1, ynb).start()
                for h in range(2):
                    cp(own(i, h), slot(i, x, y, c, h), per * i + 6 + 4 * x + 2 * y + h, sib).start()

        def second_hop():
            for lc in local():
                lc.wait()
            for i in range(n):
                cp(slot(i, 1 - x, y, c, 0), slot(i, 1 - x, y, c, 0), per * i, xnb).wait_recv()
                cp(slot(i, x, 1 - y, c, 1), slot(i, x, 1 - y, c, 1), per * i + 1, ynb).wait_recv()
                for j in range(2):
                    cp(slot(i, j, y, c, 0), slot(i, j, y, c, 0), per * i + 2 + j, ynb).start()
                    cp(slot(i, x, j, c, 1), slot(i, x, j, c, 1), per * i + 4 + j, xnb).start()
                to_sib(i, 1 - x, y, 0).start()
                to_sib(i, x, 1 - y, 1).start()

        def last_to_sibling():
            for i in range(n):
                for j in range(2):
                    cp(slot(i, j, 1 - y, c, 0), slot(i, j, 1 - y, c, 0), per * i + 2 + j, ynb).wait_recv()
                    cp(slot(i, 1 - x, j, c, 1), slot(i, 1 - x, j, c, 1), per * i + 4 + j, xnb).wait_recv()
                    to_sib(i, j, 1 - y, 0).start()
                    to_sib(i, 1 - x, j, 1).start()

        def finish():
            for i in range(n):
                for xx in range(2):
                    for yy in range(2):
                        for h in range(2):
                            s = per * i + 6 + 4 * xx + 2 * yy + h
                            cp(slot(i, xx, yy, 1 - c, h), slot(i, xx, yy, 1 - c, h), s, sib).wait_recv()
                            to_sib(i, xx, yy, h).wait_send()
                cp(own(i, 0), slot(i, x, y, c, 0), per * i, xnb).wait_send()
                cp(own(i, 1), slot(i, x, y, c, 1), per * i + 1, ynb).wait_send()
                for j in range(2):
                    cp(slot(i, j, y, c, 0), slot(i, j, y, c, 0), per * i + 2 + j, ynb).wait_send()
                    cp(slot(i, x, j, c, 1), slot(i, x, j, c, 1), per * i + 4 + j, xnb).wait_send()
            for j, (px, py) in enumerate(chips):
                for i in range(ns):
                    small(px, py, j, i, 2 * px + py).wait_recv()
                    small(px, py, j, i, 2 * x + y).wait_send()
            if rows:
                for lc in meta_copies():
                    lc.start()
                for lc in meta_copies() + row_copies():
                    lc.wait()

        return [first_hop, second_hop, last_to_sibling, finish]

    out_shapes = [jax.ShapeDtypeStruct((2, 2, 2, 2, a.shape[0] // 2, a.shape[1]), a.dtype) for a in pieces]
    out_shapes += [jax.ShapeDtypeStruct((4,) + a.shape, a.dtype) for a in smalls]
    arrays = list(pieces) + list(smalls)
    if rows:
        arrays += list(rows)
        total = smalls[0].shape[0] + rows[0].shape[0] + rows[1].shape[0]
        out_shapes.append(jax.ShapeDtypeStruct((total, rows[0].shape[1]), rows[0].dtype))
    return _Carry(name, arrays, out_shapes, n_big + 3 * ns, 2 * n + ns + 6, make, (0.0, 0.23, 0.73, 1.0))


def _exchange_carry(name, arrays, out_shapes, plan):
    count = plan([None] * len(arrays), [None] * len(out_shapes), None)

    def make(ins, outs, ssem, rsem, lsem):
        def copies():
            return [_remote(src, dst, ssem.at[j], rsem.at[j], peer)
                    for j, (src, dst, peer) in enumerate(plan(ins, outs, _position()))]

        def start():
            for c in copies():
                c.start()

        def wait():
            for c in copies():
                c.wait()

        return [start, wait]

    return _Carry(name, arrays, out_shapes, count, 0, make, (0.0, 1.0))


class _Grad:
    def __init__(self, arrs, kind, shard_shape):
        self.arrs, self.kind = list(arrs), kind
        self.rows, self.cols = shard_shape
        self.r2 = self.rows // 2

    def view(self, refs, k, h):
        r2 = self.r2
        if self.kind == "list":
            return refs[k].at[pl.ds(h * r2, r2), :]
        if self.kind == "stacked":
            return refs[0].at[k, pl.ds(h * r2, r2), :]
        if self.kind == "col":
            return refs[0].at[pl.ds(h * r2, r2), pl.ds(k * self.cols, self.cols)]
        return refs[0].at[pl.ds((2 * k + h) * r2, r2), :]

    def add_half(self, recv, pos):
        r2, cols = self.r2, self.cols
        n_in = len(self.arrs)
        tr = _row_tile(r2, cols)
        nt = r2 // tr

        def body(pos_ref, *refs):
            m_refs, (r_ref, of_ref, ob_ref) = refs[:n_in], refs[n_in:]
            mine = m_refs[0][...]
            for kk in range(1, n_in):
                mine = jnp.where(pl.program_id(1) == kk, m_refs[kk][...], mine)
            s = mine.astype(F32) + r_ref[...].astype(F32)
            of_ref[...] = s
            ob_ref[...] = s.astype(BF16)

        row = lambda t, pos: pos[2] * nt + t
        if self.kind == "list":
            specs = [pl.BlockSpec((tr, cols), lambda t, k, pos: (row(t, pos), 0))] * n_in
        elif self.kind == "stacked":
            specs = [pl.BlockSpec((None, tr, cols), lambda t, k, pos: (k, row(t, pos), 0))]
        elif self.kind == "col":
            specs = [pl.BlockSpec((tr, cols), lambda t, k, pos: (row(t, pos), k))]
        else:
            specs = [pl.BlockSpec((tr, cols), lambda t, k, pos: (2 * k * nt + row(t, pos), 0))]
        blk = pl.BlockSpec((None, tr, cols), lambda t, k, pos: (k, t, 0))
        return pl.pallas_call(
            body, name="rs_add_c",
            grid_spec=pltpu.PrefetchScalarGridSpec(
                num_scalar_prefetch=1, grid=(nt, 4), in_specs=specs + [blk], out_specs=[blk, blk]),
            out_shape=[jax.ShapeDtypeStruct((4, r2, cols), F32), jax.ShapeDtypeStruct((4, r2, cols), BF16)],
            compiler_params=_params(("arbitrary", "arbitrary")),
        )(pos, *self.arrs, recv)


def _row_tile(rows, cols):
    fits = [t for t in range(16, rows + 1, 16) if rows % t == 0 and t * cols * 4 <= 2 * 1024 * 1024]
    return max(fits) if fits else rows


def _adamw_math(w, g, m, v):
    m = ADAM_B1 * m + (1.0 - ADAM_B1) * g
    v = ADAM_B2 * v + (1.0 - ADAM_B2) * (g * g)
    m_hat = m / (1.0 - ADAM_B1 ** ADAM_STEP)
    v_hat = v / (1.0 - ADAM_B2 ** ADAM_STEP)
    delta = -ADAM_LR * (m_hat / (jnp.sqrt(v_hat) + ADAM_EPS) + ADAM_WD * w)
    return delta, m, v


def _adamw_big(w, m, v, own, sib, pos):
    rows, cols = w.shape
    r2 = rows // 2

    tr = _row_tile(r2, cols)
    nt = r2 // tr

    def body(pos_ref, w_ref, m_ref, v_ref, own_ref, sib_ref, g_ref, d_ref, nm_ref, nv_ref):
        h = pl.program_id(0)
        g = jnp.where(h == pos_ref[2], own_ref[...], sib_ref[...])
        g_ref[...] = g
        d_ref[...], nm_ref[...], nv_ref[...] = _adamw_math(w_ref[...], g, m_ref[...], v_ref[...])

    half = pl.BlockSpec((tr, cols), lambda h, t, pos: (h * nt + t, 0))
    piece = pl.BlockSpec((tr, cols), lambda h, t, pos: (t, 0))
    out = jax.ShapeDtypeStruct((rows, cols), F32)
    return pl.pallas_call(
        body, name="adamw",
        grid_spec=pltpu.PrefetchScalarGridSpec(
            num_scalar_prefetch=1, grid=(2, nt),
            in_specs=[half, half, half, piece, piece],
            out_specs=[half, half, half, half]),
        out_shape=[out, out, out, out],
        compiler_params=_params(("arbitrary", "arbitrary")),
    )(pos, w, m, v, own, sib)


def _add_hop1(s1fs, recvs, pos):
    n = len(s1fs)
    s1vs = [s.reshape((4, 2) + r.shape[2:]) for s, r in zip(s1fs, recvs)]

    def body(pos_ref, *refs):
        for m_ref, r_ref, of_ref, ob_ref in zip(refs[:n], refs[n:2 * n], refs[2 * n:3 * n], refs[3 * n:]):
            s = m_ref[...] + r_ref[...].astype(F32)
            of_ref[...] = s
            ob_ref[...] = s.astype(BF16)

    def mine(h, j, pos):
        return (jnp.where(h == 0, 2 * j + pos[1], 2 * pos[0] + j), h, 0, 0)

    tile = lambda r: (None, None) + r.shape[2:]
    blks = [pl.BlockSpec(tile(r), lambda h, j, pos: (h, j, 0, 0)) for r in recvs]
    outs = pl.pallas_call(
        body, name="rs_add_1",
        grid_spec=pltpu.PrefetchScalarGridSpec(
            num_scalar_prefetch=1, grid=(2, 2),
            in_specs=[pl.BlockSpec(tile(r), mine) for r in recvs] + blks, out_specs=blks + blks),
        out_shape=[jax.ShapeDtypeStruct(r.shape, F32) for r in recvs]
        + [jax.ShapeDtypeStruct(r.shape, BF16) for r in recvs],
        compiler_params=_params(("arbitrary", "arbitrary")),
    )(pos, *s1vs, *recvs)
    return list(zip(outs[:n], outs[n:]))


def _own_sum(s2fs, recvs, pos):
    n = len(s2fs)

    def body(pos_ref, *refs):
        for s_ref, r_ref, o_ref in zip(refs[:n], refs[n:2 * n], refs[2 * n:]):
            o_ref[...] = s_ref[...] + r_ref[...].astype(F32)

    blks = [pl.BlockSpec((None,) + r.shape[1:], lambda h, pos: (h, 0, 0)) for r in recvs]
    return pl.pallas_call(
        body, name="own_sum",
        grid_spec=pltpu.PrefetchScalarGridSpec(
            num_scalar_prefetch=1, grid=(2,),
            in_specs=[pl.BlockSpec((None, None) + r.shape[1:],
                                   lambda h, pos: (h, jnp.where(h == 0, pos[0], pos[1]), 0, 0)) for r in recvs]
            + blks, out_specs=blks),
        out_shape=[jax.ShapeDtypeStruct(r.shape, F32) for r in recvs],
        compiler_params=_params(("arbitrary",)),
    )(pos, *s2fs, *recvs)


def _add_small(a, b):
    def body(a_ref, b_ref, o_ref):
        o_ref[...] = a_ref[...] + b_ref[...]

    vm = pl.BlockSpec(memory_space=pltpu.VMEM)
    return pl.pallas_call(body, name="add_small", in_specs=[vm, vm], out_specs=vm,
                          out_shape=jax.ShapeDtypeStruct(a.shape, F32))(a, b)


def _adamw_small(ws, gs, ms, vs):
    n = len(ws)

    def body(*refs):
        for i in range(n):
            w_ref, g_ref, m_ref, v_ref, d_ref, nm_ref, nv_ref = (refs[j * n + i] for j in range(7))
            d_ref[...], nm_ref[...], nv_ref[...] = _adamw_math(w_ref[...], g_ref[...], m_ref[...], v_ref[...])

    vm = pl.BlockSpec(memory_space=pltpu.VMEM)
    outs = pl.pallas_call(body, name="adamw_small", in_specs=[vm] * (4 * n), out_specs=[vm] * (3 * n),
                          out_shape=[jax.ShapeDtypeStruct(w.shape, F32) for w in ws] * 3)(*ws, *gs, *ms, *vs)
    return outs[:n], outs[n:2 * n], outs[2 * n:]


class _ReduceScatter:
    def __init__(self, tag, grads, pos, extra=None):
        self.tag, self.grads, self.pos, self.stage, self.extra = tag, grads, pos, 0, extra

    def carry(self):
        grads, n = self.grads, len(self.grads)
        r4 = [g.r2 // 2 for g in grads]

        first = [sum(len(g.arrs) for g in grads[:i]) for i in range(n)]

        def plan_c(ins, outs, p):
            if p is None:
                return 4 * n
            x, y, c = p
            mine = lambda i: ins[first[i]:first[i] + len(grads[i].arrs)]
            return [(grads[i].view(mine(i), k, 1 - c), outs[i].at[k], (x, y, 1 - c))
                    for i in range(n) for k in range(4)]

        def plan_1(ins, outs, p):
            if p is None:
                return 4 * n
            x, y, c = p
            copies = []
            for i in range(n):
                for j in range(2):
                    copies.append((ins[i].at[2 * j + (1 - y), pl.ds(0, r4[i]), :], outs[i].at[0, j],
                                   (x, 1 - y, c)))
                    copies.append((ins[i].at[2 * (1 - x) + j, pl.ds(r4[i], r4[i]), :], outs[i].at[1, j],
                                   (1 - x, y, c)))
            return copies

        def plan_2(ins, outs, p):
            if p is None:
                return 2 * n
            x, y, c = p
            copies = []
            for i in range(n):
                copies.append((ins[i].at[0, 1 - x], outs[i].at[0], (1 - x, y, c)))
                copies.append((ins[i].at[1, 1 - y], outs[i].at[1], (x, 1 - y, c)))
            return copies

        def plan_s(ins, outs, p):
            if p is None:
                return n
            x, y, c = p
            return [(ins[i], outs[i], (x, y, 1 - c)) for i in range(n)]

        shape = lambda lead, dt: [jax.ShapeDtypeStruct(lead(g) + (g.cols,), dt) for g in grads]
        stage = self.stage
        if stage == 0:
            name, arrays, plan = "exchange_c", [a for g in grads for a in g.arrs], plan_c
            shapes = [jax.ShapeDtypeStruct((4, g.r2, g.cols), g.arrs[0].dtype) for g in grads]
        elif stage == 1:
            name, arrays, plan = "exchange_1", [s[1] for s in self.s1], plan_1
            shapes = shape(lambda g: (2, 2, g.r2 // 2), BF16)
        elif stage == 2:
            name, arrays, plan = "exchange_2", [s[1] for s in self.s2], plan_2
            shapes = shape(lambda g: (2, g.r2 // 2), BF16)
        else:
            name, arrays, plan, shapes = "exchange_sibling", self.own, plan_s, shape(lambda g: (g.r2,), F32)
        if self.extra is not None and stage < 3:
            def with_extra(ins, outs, p, plan=plan):
                if p is None:
                    return plan(ins[:-1], outs[:-1], None) + 1
                x, y, c = p
                peer = [(x, y, 1 - c), (x, 1 - y, c), (1 - x, y, c)][stage]
                return plan(ins[:-1], outs[:-1], p) + [(ins[-1], outs[-1], peer)]

            arrays = arrays + [self.extra]
            shapes = shapes + [jax.ShapeDtypeStruct(self.extra.shape, F32)]
            plan = with_extra
        return _exchange_carry(f"rs_{self.tag}_{name}", arrays, shapes, plan)

    def feed(self, recv):
        grads, pos = self.grads, self.pos
        recv = list(recv)
        if self.extra is not None and self.stage < 3:
            self.extra = _add_small(self.extra, recv.pop())
        if self.stage == 0:
            self.s1 = [g.add_half(r, pos) for g, r in zip(grads, recv)]
        elif self.stage == 1:
            self.s2 = _add_hop1([s[0] for s in self.s1], list(recv), pos)
        elif self.stage == 2:
            own = _own_sum([s[0] for s in self.s2], list(recv), pos)
            self.own = [o.reshape(g.r2, g.cols) for g, o in zip(grads, own)]
        else:
            self.sib = list(recv)
        self.stage += 1

    def run(self):
        while self.stage < 4:
            self.feed(_run_carry(self.carry()))

    def adamw(self, weights):
        return [_adamw_big(w, m, v, o, sb, self.pos) for (w, m, v), o, sb in zip(weights, self.own, self.sib)]


def _block_diag(t, nb):
    g, c, p = t.shape
    gb = g // nb
    t = t.reshape(nb, gb, c, p)
    eye = jnp.eye(gb, dtype=t.dtype)
    return jnp.einsum("bgcp,gh->bgchp", t, eye).reshape(nb, gb * c, gb * p)


def _s5_discretise(a_re, a_im, log_dt, b_re, b_im, c_re, c_im):
    g, p = a_re.shape
    nb = g // GROUPS_PER_BLOCK
    dt = jnp.exp(log_dt)[:, None]
    mag = jnp.exp(a_re * dt)
    lam_re = mag * jnp.cos(a_im * dt)
    lam_im = mag * jnp.sin(a_im * dt)
    den = a_re * a_re + a_im * a_im
    q_re = ((lam_re - 1.0) * a_re + lam_im * a_im) / den
    q_im = (lam_im * a_re - (lam_re - 1.0) * a_im) / den
    bb_re = q_re[..., None] * b_re - q_im[..., None] * b_im
    bb_im = q_re[..., None] * b_im + q_im[..., None] * b_re
    tr = lambda t: jnp.swapaxes(t, 1, 2)
    mb = jnp.concatenate([_block_diag(tr(bb_re), nb), _block_diag(tr(bb_im), nb)], axis=-1)
    mc = jnp.concatenate([_block_diag(c_re, nb), -_block_diag(c_im, nb)], axis=-1)
    lam = jnp.concatenate([lam_re.reshape(nb, -1), lam_im.reshape(nb, -1)], axis=-1)
    return mb, mc, lam


def _s5_powers(a_re, a_im, log_dt, sub):
    g, p = a_re.shape
    nb = g // GROUPS_PER_BLOCK
    dt = jnp.exp(log_dt)[:, None]
    ns = list(range(1, sub + 1)) + [sub << m for m in range(1, SCAN_SEQS.bit_length() - 1)]
    ns += [0] * (-len(ns) % SUBLANES)
    e = jnp.asarray(ns, F32)[:, None, None]
    mag = jnp.exp(a_re[None] * dt[None] * e)
    ang = a_im[None] * dt[None] * e
    re = (mag * jnp.cos(ang)).reshape(len(ns), nb, -1)
    im = (mag * jnp.sin(ang)).reshape(len(ns), nb, -1)
    return jnp.transpose(jnp.concatenate([re, im], axis=-1), (1, 0, 2))


def _pack(parts):
    flat = jnp.concatenate([a.reshape(-1).astype(F32) for a in parts])
    n = flat.shape[0]
    pad = -n % (SUBLANES * LANES)
    return jnp.pad(flat, (0, pad)).reshape(-1, LANES)


def _unpack(buf, like):
    flat = buf.reshape(-1)
    out, o = [], 0
    for a in like:
        out.append(flat[o:o + a.size].reshape(a.shape))
        o += a.size
    return out


def kernel(x, meta_tokens, g_ffn1, ffn1_w_gate, ffn1_w_up, ffn1_w_down, g_mix, w_in, b_gate, ssm_a_re, ssm_a_im, ssm_log_dt, ssm_b_re, ssm_b_im, ssm_c_re, ssm_c_im, ssm_d, ssm_w_glu, conv_w, conv_w_out, w_o, g_ffn2, ffn2_w_gate, ffn2_w_up, ffn2_w_down, g_final, loss_target, m_meta_tokens, m_g_ffn1, m_ffn1_w_gate, m_ffn1_w_up, m_ffn1_w_down, m_g_mix, m_w_in, m_b_gate, m_ssm_a_re, m_ssm_a_im, m_ssm_log_dt, m_ssm_b_re, m_ssm_b_im, m_ssm_c_re, m_ssm_c_im, m_ssm_d, m_ssm_w_glu, m_conv_w, m_conv_w_out, m_w_o, m_g_ffn2, m_ffn2_w_gate, m_ffn2_w_up, m_ffn2_w_down, m_g_final, v_meta_tokens, v_g_ffn1, v_ffn1_w_gate, v_ffn1_w_up, v_ffn1_w_down, v_g_mix, v_w_in, v_b_gate, v_ssm_a_re, v_ssm_a_im, v_ssm_log_dt, v_ssm_b_re, v_ssm_b_im, v_ssm_c_re, v_ssm_c_im, v_ssm_d, v_ssm_w_glu, v_conv_w, v_conv_w_out, v_w_o, v_g_ffn2, v_ffn2_w_gate, v_ffn2_w_up, v_ffn2_w_down, v_g_final):
    seq, d = x.shape[1], x.shape[2]
    n_meta = meta_tokens.shape[0]
    dh = d // 2
    tp = -(-(n_meta + seq) // ROW_ALIGN) * ROW_ALIGN
    mx, my, mc_ = _position()
    pos = jnp.stack([mx, my, mc_]).astype(jnp.int32)
    shard = 2 * mx + my

    big_names = ["ffn1_w_gate", "ffn1_w_up", "ffn1_w_down", "w_in", "ssm_w_glu", "conv_w_out", "w_o",
                 "ffn2_w_gate", "ffn2_w_up", "ffn2_w_down"]
    transposed = {0, 1, 7, 8}
    drop = lambda arrs: [jnp.swapaxes(a.reshape(a.shape[1:]), 0, 1) if i in transposed else a.reshape(a.shape[1:])
                         for i, a in enumerate(arrs)]
    big_w = drop([ffn1_w_gate, ffn1_w_up, ffn1_w_down, w_in, ssm_w_glu, conv_w_out, w_o,
                  ffn2_w_gate, ffn2_w_up, ffn2_w_down])
    big_m = drop([m_ffn1_w_gate, m_ffn1_w_up, m_ffn1_w_down, m_w_in, m_ssm_w_glu, m_conv_w_out,
                  m_w_o, m_ffn2_w_gate, m_ffn2_w_up, m_ffn2_w_down])
    big_v = drop([v_ffn1_w_gate, v_ffn1_w_up, v_ffn1_w_down, v_w_in, v_ssm_w_glu, v_conv_w_out,
                  v_w_o, v_ffn2_w_gate, v_ffn2_w_up, v_ffn2_w_down])
    pieces = _cast_pieces(big_w[:3], pos) + _cast_pieces(big_w[3:], pos)
    conv_local = conv_w.reshape(conv_w.shape[1], conv_w.shape[3])
    n_first = 3
    pad_rows = tp - n_meta - seq
    assert pad_rows > 0
    first = _run_carry(_allgather_carry("allgather_first", pieces[:n_first], [meta_tokens, conv_local],
                                        rows=(x.reshape(seq, d), jnp.zeros((pad_rows, d), F32))))
    smalls, h0 = first[n_first:n_first + 2], first[n_first + 2]
    stack4 = lambda wl: wl.reshape((4, -1, wl.shape[-1]))
    w1g, w1u, w1d = [stack4(wl) for wl in first[:n_first]]
    natural_cols = lambda s: jnp.transpose(s, (1, 0, 2)).reshape(s.shape[1], 4 * s.shape[2])
    cw_full = natural_cols(smalls[1])
    cw_pad = jnp.pad(cw_full, ((0, SUBLANES - cw_full.shape[0]), (0, 0)))

    s5_args = (ssm_a_re[0], ssm_a_im[0], ssm_log_dt[0], ssm_b_re[0], ssm_b_im[0], ssm_c_re[0], ssm_c_im[0])
    (mb, mc, _), disc_vjp = jax.vjp(_s5_discretise, *s5_args)
    powt = _s5_powers(ssm_a_re[0], ssm_a_im[0], ssm_log_dt[0], SCAN_TILE // SCAN_SEQS)
    mb16, mc16 = mb.astype(BF16), mc.astype(BF16)

    h1, a1, b1, n1, *mid = _ffn_fwd(h0, g_ffn1, w1g, w1u, w1d, "ffn1_fwd",
                                    carry=_allgather_carry("allgather_mixer", pieces[3:7], []))
    win_all, wglu_s, wco_s, wo_s = [stack4(wl) for wl in mid]
    wglu_all = natural_cols(wglu_s)
    wco_all = natural_cols(wco_s)
    wo_all = wo_s.reshape(d, d)
    u, p, w2g, w2u = _win_fwd(h1, g_mix, win_all, carry=_allgather_carry("allgather_ffn2_in", pieces[7:9], []))
    ys5, bnd = _scan_fwd(p, mb16, mc16, powt, ssm_d)
    h2, w2d = _mix_fwd(h1, ys5, p, cw_pad, b_gate, wglu_all, wco_all, wo_all,
                       carry=_allgather_carry("allgather_ffn2_out", pieces[9:], []))
    w2g, w2u, w2d = stack4(w2g), stack4(w2u), stack4(w2d)
    dh3, a2, b2, n2, dg_final, loss_part, dy3 = _ffn_fwd(
        h2, g_ffn2, w2g, w2u, w2d, "ffn2_fwd_loss",
        final=(g_final.reshape(1, d), loss_target.reshape(seq, d), n_meta, seq))

    dh2, dw2g, dw2u, dw2d, dg_ffn2 = _ffn_bwd(dh3, dy3, h2, n2, g_ffn2, a2, b2, w2g, w2u, w2d, "ffn2_bwd")
    dys5, dpb, dwo, dwglu, dwco, dcw, dbg = _mix_bwd(dh2, ys5, p, cw_pad, b_gate, wglu_all, wco_all, wo_all)
    dug, dmb, dmc, dlam, dd = _scan_bwd(p, dys5, mb16, mc16, powt, ssm_d, bnd)
    dh1, dwin, dg_mix, dy1 = _win_bwd(dpb, dug, u, win_all, h1, g_mix, dh2)
    shapes = [w.shape for w in big_w]
    kinds = ["list", "list", "list", "list", "col", "col", "row", "list", "list", "list"]
    rest_grads = [dwin, [dwglu], [dwco], [dwo], dw2g, dw2u, dw2d]
    rs_rest = _ReduceScatter("rest", [_Grad(a, k, s) for a, k, s in
                                      zip(rest_grads, kinds[n_first:], shapes[n_first:])], pos)
    grad_x, dw1g, dw1u, dw1d, dg_ffn1, grad_meta = _ffn_bwd(
        dh1, dy1, h0, n1, g_ffn1, a1, b1, w1g, w1u, w1d, "ffn1_bwd", chain=rs_rest, unpad=(n_meta, seq))
    s5_grads = disc_vjp((dmb, dmc, jnp.sum(dlam, axis=1)))
    local_small = [dg_ffn1, dg_mix, dbg, *s5_grads, jnp.sum(dd, axis=0), dg_ffn2, dg_final,
                   grad_meta, dcw[:conv_w.shape[1]]]
    rs_first = _ReduceScatter("first", [_Grad(a, k, s) for a, k, s in
                                        zip([dw1g, dw1u, dw1d], kinds[:n_first], shapes[:n_first])], pos,
                              extra=_pack(local_small))
    rs_first.run()
    wmv = list(zip(big_w, big_m, big_v))
    big_out = rs_first.adamw(wmv[:n_first]) + rs_rest.adamw(wmv[n_first:])
    def lead(i, o):
        o = jnp.swapaxes(o, 0, 1) if i in transposed else o
        return o.reshape((1,) + o.shape)

    big_out = {nme: tuple(lead(i, o) for o in outs) for i, (nme, outs) in enumerate(zip(big_names, big_out))}

    grad_x = grad_x.reshape(1, seq, d)

    small_names = ["g_ffn1", "g_mix", "b_gate", "ssm_a_re", "ssm_a_im", "ssm_log_dt", "ssm_b_re", "ssm_b_im",
                   "ssm_c_re", "ssm_c_im", "ssm_d", "g_ffn2", "g_final", "meta_tokens", "conv_w"]
    small_w = [g_ffn1, g_mix, b_gate, ssm_a_re, ssm_a_im, ssm_log_dt, ssm_b_re, ssm_b_im, ssm_c_re, ssm_c_im,
               ssm_d, g_ffn2, g_final, meta_tokens, conv_w]
    small_m = [m_g_ffn1, m_g_mix, m_b_gate, m_ssm_a_re, m_ssm_a_im, m_ssm_log_dt, m_ssm_b_re, m_ssm_b_im,
               m_ssm_c_re, m_ssm_c_im, m_ssm_d, m_g_ffn2, m_g_final, m_meta_tokens, m_conv_w]
    small_v = [v_g_ffn1, v_g_mix, v_b_gate, v_ssm_a_re, v_ssm_a_im, v_ssm_log_dt, v_ssm_b_re, v_ssm_b_im,
               v_ssm_c_re, v_ssm_c_im, v_ssm_d, v_g_ffn2, v_g_final, v_meta_tokens, v_conv_w]
    reduced = _unpack(rs_first.extra, local_small)
    reduced[-2] = lax.dynamic_slice_in_dim(reduced[-2], shard * meta_tokens.shape[1], meta_tokens.shape[1], 1)
    reduced[-1] = lax.dynamic_slice_in_dim(reduced[-1], shard * conv_w.shape[3], conv_w.shape[3], 1)
    small_g = [r.reshape(w.shape) for r, w in zip(reduced, small_w)]
    two_d = lambda arrs: [a.reshape(1, -1) if a.ndim == 1 else a for a in arrs]
    ds_, nm_, nv_ = _adamw_small(two_d(small_w), two_d(small_g), two_d(small_m), two_d(small_v))
    like = lambda outs: [o.reshape(w.shape) for o, w in zip(outs, small_w)]
    small_out = {nme: o for nme, o in zip(small_names, zip(small_g, like(ds_), like(nm_), like(nv_)))}

    loss = lax.psum(loss_part[0, 0], ("x", "y", "c"))
    order = ["meta_tokens", "g_ffn1", "ffn1_w_gate", "ffn1_w_up", "ffn1_w_down", "g_mix", "w_in", "b_gate",
             "ssm_a_re", "ssm_a_im", "ssm_log_dt", "ssm_b_re", "ssm_b_im", "ssm_c_re", "ssm_c_im", "ssm_d",
             "ssm_w_glu", "conv_w", "conv_w_out", "w_o", "g_ffn2", "ffn2_w_gate", "ffn2_w_up", "ffn2_w_down",
             "g_final"]
    res = {**big_out, **small_out}
    return (loss, grad_x, *[res[nme][0] for nme in order], *[res[nme][1] for nme in order],
            *[res[nme][2] for nme in order], *[res[nme][3] for nme in order])
```

```python
import functools
import math

import jax
import jax.numpy as jnp
from jax import lax
from jax.experimental import pallas as pl
from jax.experimental.pallas import tpu as pltpu

F32 = jnp.float32
BF16 = jnp.bfloat16
MESH = pl.DeviceIdType.MESH

RMS_EPS = 1e-6
ADAM_LR = 0.001
ADAM_B1 = 0.9
ADAM_B2 = 0.999
ADAM_EPS = 1e-08
ADAM_WD = 0.01
ADAM_STEP = 10

LANES = 128
SUBLANES = 8
VMEM_LIMIT = 56 * 1024 * 1024

ROW_ALIGN = 256
SCAN_TILE = 256
SCAN_SEQS = 16
GROUPS_PER_BLOCK = 8
MAX_ROW_CHUNKS = 32


def _params(sem, vmem=VMEM_LIMIT):
    return pltpu.CompilerParams(dimension_semantics=sem, vmem_limit_bytes=vmem)


def _pick_tile(n, candidates):
    for c in candidates:
        if n % c == 0:
            return c
    raise ValueError(f"no tile for {n}")


def _dot(a, b):
    return jnp.dot(a, b, preferred_element_type=F32)


def _dot_nt(a, b):
    return lax.dot_general(a, b, (((1,), (1,)), ((), ())), preferred_element_type=F32)


def _dot_tn(a, b):
    return lax.dot_general(a, b, (((0,), (0,)), ((), ())), preferred_element_type=F32)


def _sigmoid(x):
    return pl.reciprocal(1.0 + jnp.exp(-x), approx=True)


def _rms_stats(h):
    r = lax.rsqrt(jnp.mean(h * h, axis=-1, keepdims=True) + RMS_EPS)
    return h * r, r


def _rms_bwd(xhat, r, g, dn):
    dxh = dn * g
    return r * (dxh - xhat * jnp.mean(dxh * xhat, axis=-1, keepdims=True))


GELU_K = math.sqrt(2.0 / math.pi)
GELU_C = 0.044715


def _gelu(x):
    return 0.5 * x * (1.0 + jnp.tanh(GELU_K * (x + GELU_C * x * x * x)))


def _gelu_grad(x):
    t = jnp.tanh(GELU_K * (x + GELU_C * x * x * x))
    return 0.5 * (1.0 + t) + 0.5 * x * (1.0 - t * t) * GELU_K * (1.0 + 3.0 * GELU_C * x * x)


def _for_tile_rows(i, ni, tm, n_meta, seq, fn):
    pl.when(i == 0)(lambda: fn(0, min(tm - n_meta, seq), n_meta))
    if ni > 1:
        last_lo = (ni - 1) * tm - n_meta
        pl.when(i == ni - 1)(lambda: fn(last_lo, min(seq - last_lo, tm), 0))
    if ni > 2:
        pl.when((i > 0) & (i < ni - 1))(lambda: fn(pl.multiple_of(i * tm - n_meta, SUBLANES), tm, 0))


def _ffn_fwd(h, g, wg, wu, wd, name, final=None, carry=None):
    tp, d = h.shape
    ns, f4, _ = wg.shape
    tm = _pick_tile(tp, (768, 512, 256))
    ni = tp // tm

    def body(*refs):
        refs, phases = split(refs)
        if final is None:
            h_ref, g_ref, wg_ref, wu_ref, wd_ref, ho_ref, a_ref, b_ref, n_scr, acc = refs
        else:
            (h_ref, g_ref, wg_ref, wu_ref, wd_ref, gf_ref, tg_hbm,
             ho_ref, a_ref, b_ref, n_scr, dgf_ref, loss_ref, dy_ref, acc, tg_ref, tg_sem) = refs
        i = pl.program_id(0)
        k = pl.program_id(1)
        _run_phases(phases, carry, i * ns + k, ni * ns)

        if final is not None:
            def target_rows(lo, n, at):
                return pltpu.make_async_copy(tg_hbm.at[pl.ds(lo, n), :], tg_ref.at[pl.ds(at, n), :], tg_sem)

            def fetch_target(lo, n, at):
                if at > 0:
                    tg_ref[pl.ds(0, at), :] = jnp.zeros((at, d), F32)
                if at + n < tm:
                    tg_ref[pl.ds(at + n, tm - at - n), :] = jnp.zeros((tm - at - n, d), F32)
                target_rows(lo, n, at).start()

            pl.when(k == 0)(lambda: _for_tile_rows(i, ni, tm, final[2], final[3], fetch_target))

        @pl.when(k == 0)
        def _():
            xhat, _ = _rms_stats(h_ref[...])
            n_scr[...] = (xhat * g_ref[...]).astype(BF16)
            acc[...] = jnp.zeros_like(acc)

        n = n_scr[...]
        a = _dot_nt(n, wg_ref[...])
        b = _dot_nt(n, wu_ref[...])
        a_ref[...] = a.astype(BF16)
        b_ref[...] = b.astype(BF16)
        s = (a * _sigmoid(a) * b).astype(BF16)
        acc[...] += _dot(s, wd_ref[...])

        if final is None:
            @pl.when(k == ns - 1)
            def _():
                ho_ref[...] = h_ref[...] + 0.5 * acc[...]
        else:
            n_meta, seq = final[2], final[3]

            @pl.when((i == 0) & (k == 0))
            def _():
                dgf_ref[...] = jnp.zeros_like(dgf_ref)
                loss_ref[...] = jnp.zeros_like(loss_ref)

            @pl.when(k == ns - 1)
            def _():
                _for_tile_rows(i, ni, tm, n_meta, seq, lambda lo, n, at: target_rows(lo, n, at).wait())
                h3 = h_ref[...] + 0.5 * acc[...]
                xhat, r = _rms_stats(h3)
                gf = gf_ref[...]
                row = i * tm + lax.broadcasted_iota(jnp.int32, (tm, d), 0)
                valid = (row >= n_meta) & (row < n_meta + seq)
                diff = jnp.where(valid, xhat * gf - tg_ref[...], 0.0)
                dout = diff * (1.0 / d)
                loss_ref[...] += jnp.full(loss_ref.shape, 0.5 * jnp.sum(diff * diff) * (1.0 / d), F32)
                dgf_ref[...] += jnp.sum(dout * xhat, axis=0, keepdims=True)
                dh3 = _rms_bwd(xhat, r, gf, dout)
                ho_ref[...] = dh3
                dy_ref[...] = (0.5 * dh3).astype(BF16)

    row_spec = pl.BlockSpec((tm, d), lambda i, k: (i, 0))
    vec_spec = pl.BlockSpec((1, d), lambda i, k: (0, 0))
    in_specs = [row_spec, vec_spec,
                pl.BlockSpec((None, f4, d), lambda i, k: (k, 0, 0)),
                pl.BlockSpec((None, f4, d), lambda i, k: (k, 0, 0)),
                pl.BlockSpec((None, f4, d), lambda i, k: (k, 0, 0))]
    act_spec = pl.BlockSpec((None, tm, f4), lambda i, k: (k, i, 0))
    out_specs = [row_spec, act_spec, act_spec, row_spec]
    out_shape = [jax.ShapeDtypeStruct((tp, d), F32),
                 jax.ShapeDtypeStruct((ns, tp, f4), BF16),
                 jax.ShapeDtypeStruct((ns, tp, f4), BF16),
                 jax.ShapeDtypeStruct((tp, d), BF16)]
    args = [h, g, wg, wu, wd]
    scratch = [pltpu.VMEM((tm, d), F32)]
    if final is not None:
        in_specs += [vec_spec, ANY]
        args += [final[0], final[1]]
        out_specs += [vec_spec, pl.BlockSpec((1, LANES), lambda i, k: (0, 0)), row_spec]
        out_shape += [jax.ShapeDtypeStruct((1, d), F32), jax.ShapeDtypeStruct((1, LANES), F32),
                      jax.ShapeDtypeStruct((tp, d), BF16)]
        scratch += [pltpu.VMEM((tm, d), F32), pltpu.SemaphoreType.DMA(())]
    split = _attach_carry(carry, in_specs, args, out_specs, out_shape, scratch)
    return pl.pallas_call(
        body, name=name, grid=(ni, ns), in_specs=in_specs, out_specs=out_specs, out_shape=out_shape,
        scratch_shapes=scratch, compiler_params=_params(("arbitrary", "arbitrary")),
    )(*args)


def _ffn_bwd_shard(k, ns, dn_prev, dy, n, a, b, wg, wu, wd, tail, name, carry=None, unpad=None):
    tp, d = n.shape
    f4 = wg.shape[1]
    tm = _pick_tile(tp, (768, 512, 256))
    ni = tp // tm
    first, last = k == 0, k == ns - 1
    unpad = unpad if last else None

    def body(*refs):
        refs, phases = split(refs)
        acc_in = None if first else refs.pop(0)
        if last:
            dh_ref, h_ref, g_ref = refs[:3]
        else:
            dy_ref, n_ref = refs[:2]
        refs = refs[3 if last else 2:]
        a_ref, b_ref, wg_hbm, wu_hbm, wd_hbm = refs[:5]
        refs = refs[5:]
        acc_out, dwg_hbm, dwu_hbm, dwd_hbm = refs[:4]
        rest = refs[4:]
        dg_ref = rest.pop(0) if last else None
        head_ref = rest.pop(0) if unpad else None
        wg_ref, wu_ref, wd_ref, dwg_ref, dwu_ref, dwd_ref, wsem = rest[:7]
        i = pl.program_id(0)
        _run_phases(phases, carry, i, ni)
        if unpad:
            res_ref, res_sem = rest[7:]

            def real_rows(lo, cnt, at):
                return pltpu.make_async_copy(res_ref.at[pl.ds(at, cnt), :], acc_out.at[pl.ds(lo, cnt), :], res_sem)

            def wait_tile(tile):
                _for_tile_rows(tile, ni, tm, *unpad, lambda lo, cnt, at: real_rows(lo, cnt, at).wait())

        @pl.when(i == 0)
        def _():
            loads = [pltpu.make_async_copy(src.at[k], dst, wsem.at[j])
                     for j, (src, dst) in enumerate(((wg_hbm, wg_ref), (wu_hbm, wu_ref), (wd_hbm, wd_ref)))]
            for cp in loads:
                cp.start()
            dwg_ref[...] = jnp.zeros_like(dwg_ref)
            dwu_ref[...] = jnp.zeros_like(dwu_ref)
            dwd_ref[...] = jnp.zeros_like(dwd_ref)
            if last:
                dg_ref[...] = jnp.zeros_like(dg_ref)
            for cp in loads:
                cp.wait()

        if last:
            xhat, r = _rms_stats(h_ref[...])
            n = (xhat * g_ref[...]).astype(BF16)
            dy = (0.5 * dh_ref[...]).astype(BF16)
        else:
            n = n_ref[...]
            dy = dy_ref[...]
        av = a_ref[...].astype(F32)
        bv = b_ref[...].astype(F32)
        sg = _sigmoid(av)
        silu = av * sg
        ds = _dot_nt(dy, wd_ref[...])
        da = (ds * bv * (sg * (1.0 + av * (1.0 - sg)))).astype(BF16)
        db = (ds * silu).astype(BF16)
        s = (silu * bv).astype(BF16)
        dwd_ref[...] += _dot_tn(s, dy)
        dwg_ref[...] += _dot_tn(da, n)
        dwu_ref[...] += _dot_tn(db, n)
        dn = _dot(da, wg_ref[...]) + _dot(db, wu_ref[...])
        if not first:
            dn = dn + acc_in[...]
        if last:
            dg_ref[...] += jnp.sum(dn * xhat, axis=0, keepdims=True)
            dh_in = dh_ref[...] + _rms_bwd(xhat, r, g_ref[...], dn)
            if unpad:
                pl.when(i > 0)(lambda: wait_tile(i - 1))
                res_ref[...] = dh_in

                @pl.when(i == 0)
                def _():
                    head_ref[...] = res_ref[pl.ds(0, unpad[0]), :]

                _for_tile_rows(i, ni, tm, *unpad, lambda lo, cnt, at: real_rows(lo, cnt, at).start())
                pl.when(i == ni - 1)(lambda: wait_tile(i))
            else:
                acc_out[...] = dh_in
        else:
            acc_out[...] = dn

        @pl.when(i == ni - 1)
        def _():
            stores = []
            for j, (acc_ref, stage_ref, out_hbm) in enumerate(((dwg_ref, wg_ref, dwg_hbm), (dwu_ref, wu_ref, dwu_hbm),
                                                              (dwd_ref, wd_ref, dwd_hbm))):
                stage_ref[...] = acc_ref[...].astype(BF16)
                stores.append(pltpu.make_async_copy(stage_ref, out_hbm, wsem.at[j]))
                stores[-1].start()
            for cp in stores:
                cp.wait()

    row_spec = pl.BlockSpec((tm, d), lambda i: (i, 0))
    vec_spec = pl.BlockSpec((1, d), lambda i: (0, 0))
    act_spec = pl.BlockSpec((None, tm, f4), lambda i: (k, i, 0))
    in_specs = [act_spec, act_spec, ANY, ANY, ANY]
    args = [a, b, wg, wu, wd]
    if last:
        in_specs = [row_spec, row_spec, vec_spec] + in_specs
        args = list(tail) + args
    else:
        in_specs = [row_spec, row_spec] + in_specs
        args = [dy, n] + args
    if not first:
        in_specs.insert(0, row_spec)
        args.insert(0, dn_prev)
    out_specs = [row_spec, ANY, ANY, ANY]
    out_shape = [jax.ShapeDtypeStruct((tp, d), F32)] + [jax.ShapeDtypeStruct((f4, d), BF16)] * 3
    scratch = [pltpu.VMEM((f4, d), BF16)] * 3 + [pltpu.VMEM((f4, d), F32)] * 3 + [pltpu.SemaphoreType.DMA((3,))]
    if last:
        out_specs.append(vec_spec)
        out_shape.append(jax.ShapeDtypeStruct((1, d), F32))
    if unpad:
        out_specs[0] = ANY
        out_shape[0] = jax.ShapeDtypeStruct((unpad[1], d), F32)
        out_specs.append(pl.BlockSpec((unpad[0], d), lambda i: (0, 0)))
        out_shape.append(jax.ShapeDtypeStruct((unpad[0], d), F32))
        scratch += [pltpu.VMEM((tm, d), F32), pltpu.SemaphoreType.DMA(())]
    n_host = len(out_shape)
    split = _attach_carry(carry, in_specs, args, out_specs, out_shape, scratch)
    outs = pl.pallas_call(
        body, name=f"{name}_{k}", grid=(ni,), in_specs=in_specs, out_specs=out_specs, out_shape=out_shape,
        scratch_shapes=scratch, compiler_params=_params(("arbitrary",)),
    )(*args)
    return outs[:n_host], outs[n_host:]


def _ffn_bwd(dh_out, dy, h_in, n, g, a, b, wg, wu, wd, name, chain=None, unpad=None):
    ns = wg.shape[0]
    acc, dwg, dwu, dwd = None, [], [], []
    for k in range(ns):
        carry = chain.carry() if chain is not None else None
        outs, carried = _ffn_bwd_shard(k, ns, acc, dy, n, a, b, wg, wu, wd, (dh_out, h_in, g), name, carry, unpad)
        if chain is not None:
            chain.feed(carried)
        acc = outs[0]
        dwg.append(outs[1])
        dwu.append(outs[2])
        dwd.append(outs[3])
    return (acc, dwg, dwu, dwd) + tuple(outs[4:])


def _win_fwd(h, g, w_in, carry=None):
    tp, d = h.shape
    ns = w_in.shape[0]
    tm = _pick_tile(tp, (768, 512, 256))
    ni = tp // tm

    def body(*refs):
        (h_ref, g_ref, w_ref, u_ref, p_ref), phases = split(refs)
        _run_phases(phases, carry, pl.program_id(0), ni)
        xhat, _ = _rms_stats(h_ref[...])
        u = (xhat * g_ref[...]).astype(BF16)
        u_ref[...] = u
        for k in range(ns):
            p_ref[k] = _dot(u, w_ref[k]).astype(BF16)

    in_specs = [pl.BlockSpec((tm, d), lambda i: (i, 0)),
                pl.BlockSpec((1, d), lambda i: (0, 0)),
                pl.BlockSpec((ns, d, d), lambda i: (0, 0, 0))]
    out_specs = [pl.BlockSpec((tm, d), lambda i: (i, 0)),
                 pl.BlockSpec((ns, tm, d), lambda i: (0, i, 0))]
    out_shape = [jax.ShapeDtypeStruct((tp, d), BF16), jax.ShapeDtypeStruct((ns, tp, d), BF16)]
    args, scratch = [h, g, w_in], []
    split = _attach_carry(carry, in_specs, args, out_specs, out_shape, scratch)
    return pl.pallas_call(
        body, name="win_fwd", grid=(ni,), in_specs=in_specs, out_specs=out_specs, out_shape=out_shape,
        scratch_shapes=scratch, compiler_params=_params(("arbitrary",)),
    )(*args)


def _win_bwd_shard(k, ns, du_prev, dpb, dug, u, w_in, h1, g, dh2):
    tp, d = h1.shape
    dh = d // 2
    tm = _pick_tile(tp, (768, 512, 256))
    first, last = k == 0, k == ns - 1

    def body(*refs):
        refs = list(refs)
        acc_in = None if first else refs.pop(0)
        dug_ref = refs.pop(0) if first else None
        dp_ref, u_ref, w_ref = refs[:3]
        refs = refs[3:]
        if last:
            h_ref, g_ref, dh2_ref, acc_out, dw_ref, dg_ref, dy_ref, dw_acc = refs
        else:
            acc_out, dw_ref, dw_acc = refs
        i = pl.program_id(0)

        @pl.when(i == 0)
        def _():
            dw_acc[...] = jnp.zeros_like(dw_acc)
            if last:
                dg_ref[...] = jnp.zeros_like(dg_ref)

        dp = dp_ref[...]
        if first:
            dp = jnp.concatenate([dug_ref[...], dp[:, dh:]], axis=1)
        dw_acc[...] += _dot_tn(u_ref[...], dp)
        du = _dot_nt(dp, w_ref[...])
        if not first:
            du = du + acc_in[...]
        if last:
            xhat, r = _rms_stats(h_ref[...])
            dg_ref[...] += jnp.sum(du * xhat, axis=0, keepdims=True)
            dh1 = dh2_ref[...] + _rms_bwd(xhat, r, g_ref[...], du)
            acc_out[...] = dh1
            dy_ref[...] = (0.5 * dh1).astype(BF16)
        else:
            acc_out[...] = du

        @pl.when(i == tp // tm - 1)
        def _():
            dw_ref[...] = dw_acc[...].astype(BF16)

    row_spec = pl.BlockSpec((tm, d), lambda i: (i, 0))
    vec_spec = pl.BlockSpec((1, d), lambda i: (0, 0))
    in_specs = [pl.BlockSpec((None, tm, d), lambda i: (k, i, 0)), row_spec,
                pl.BlockSpec((None, d, d), lambda i: (k, 0, 0))]
    args = [dpb, u, w_in]
    if first:
        in_specs.insert(0, pl.BlockSpec((tm, dh), lambda i: (i, 0)))
        args.insert(0, dug)
    else:
        in_specs.insert(0, row_spec)
        args.insert(0, du_prev)
    out_specs = [row_spec, pl.BlockSpec((d, d), lambda i: (0, 0))]
    out_shape = [jax.ShapeDtypeStruct((tp, d), F32), jax.ShapeDtypeStruct((d, d), BF16)]
    if last:
        in_specs += [row_spec, vec_spec, row_spec]
        args += [h1, g, dh2]
        out_specs += [vec_spec, row_spec]
        out_shape += [jax.ShapeDtypeStruct((1, d), F32), jax.ShapeDtypeStruct((tp, d), BF16)]
    return pl.pallas_call(
        body, name=f"win_bwd_{k}", grid=(tp // tm,), in_specs=in_specs, out_specs=out_specs,
        out_shape=out_shape, scratch_shapes=[pltpu.VMEM((d, d), F32)],
        compiler_params=_params(("arbitrary",)),
    )(*args)


def _win_bwd(dpb, dug, u, w_in, h1, g, dh2):
    ns = w_in.shape[0]
    acc, dws = None, []
    for k in range(ns):
        outs = _win_bwd_shard(k, ns, acc, dpb, dug, u, w_in, h1, g, dh2)
        acc = outs[0]
        dws.append(outs[1])
    return acc, dws, outs[2], outs[3]


def _cmul(ar, ai, br, bi):
    return ar * br - ai * bi, ar * bi + ai * br


def _scan_rows(j, sub):
    return pl.ds(j * SCAN_SEQS, SCAN_SEQS)


def _permute_rows(src_ref, dst_ref, sub):
    for j in range(sub):
        dst_ref[pl.ds(j * SCAN_SEQS, SCAN_SEQS), :] = src_ref[pl.ds(j, SCAN_SEQS, stride=sub), :]


def _unpermute_rows(src_ref, dst_ref, sub):
    for j in range(sub):
        dst_ref[pl.ds(j, SCAN_SEQS, stride=sub), :] = src_ref[pl.ds(j * SCAN_SEQS, SCAN_SEQS), :]


def _local_scan(x_ref, lr, li, w, sub, reverse):
    hr = jnp.zeros((SCAN_SEQS, w), F32)
    hi = jnp.zeros((SCAN_SEQS, w), F32)
    order = range(sub - 1, -1, -1) if reverse else range(sub)
    for j in order:
        xr = x_ref[_scan_rows(j, sub), pl.ds(0, w)]
        xi = x_ref[_scan_rows(j, sub), pl.ds(w, w)]
        if reverse:
            hr, hi = lr * hr + li * hi + xr, lr * hi - li * hr + xi
        else:
            hr, hi = lr * hr - li * hi + xr, lr * hi + li * hr + xi
        x_ref[_scan_rows(j, sub), pl.ds(0, w)] = hr
        x_ref[_scan_rows(j, sub), pl.ds(w, w)] = hi
    return hr, hi


def _entering_states(er, ei, fr, fi, pow_ref, w, sub, reverse):
    lane = lax.broadcasted_iota(jnp.int32, (SCAN_SEQS, w), 0)
    if reverse:
        edge, shift1 = SCAN_SEQS - 1, SCAN_SEQS - 1
    else:
        edge, shift1 = 0, 1
    zr = jnp.where(lane == edge, pltpu.roll(fr, shift1, 0), pltpu.roll(er, shift1, 0))
    zi = jnp.where(lane == edge, pltpu.roll(fi, shift1, 0), pltpu.roll(ei, shift1, 0))
    for m in range(SCAN_SEQS.bit_length() - 1):
        step, row = 1 << m, sub - 1 + m
        ar = pow_ref[pl.ds(row, 1), pl.ds(0, w)]
        ai = pow_ref[pl.ds(row, 1), pl.ds(w, w)]
        if reverse:
            ai = -ai
            keep = lane < SCAN_SEQS - step
            sr = jnp.where(keep, pltpu.roll(zr, SCAN_SEQS - step, 0), 0.0)
            si = jnp.where(keep, pltpu.roll(zi, SCAN_SEQS - step, 0), 0.0)
        else:
            keep = lane >= step
            sr = jnp.where(keep, pltpu.roll(zr, step, 0), 0.0)
            si = jnp.where(keep, pltpu.roll(zi, step, 0), 0.0)
        pr, pi = _cmul(ar, ai, sr, si)
        zr, zi = zr + pr, zi + pi
    ar = pow_ref[pl.ds(sub - 1, 1), pl.ds(0, w)]
    ai = pow_ref[pl.ds(sub - 1, 1), pl.ds(w, w)]
    if reverse:
        ai = -ai
    pr, pi = _cmul(ar, ai, zr, zi)
    return zr, zi, er + pr, ei + pi


def _scan_fwd(p, mb, mc, powt, dskip):
    _, tp, d = p.shape
    nb, cb, w2 = mb.shape
    w = w2 // 2
    q = SCAN_TILE
    sub = q // SCAN_SEQS
    nt = tp // q
    ds = d // 2

    def body(ug_ref, mb_ref, mc_ref, pow_ref, d_ref, y_ref, bnd_ref, x_scr, carry, nat, perm):
        t = pl.program_id(1)

        @pl.when(t == 0)
        def _():
            carry[...] = jnp.zeros_like(carry)

        ugf = ug_ref[...].astype(F32)
        nat[...] = ugf
        _permute_rows(nat, perm, sub)
        x_scr[...] = _dot(perm[...].astype(BF16), mb_ref[...])
        lr = jnp.broadcast_to(pow_ref[pl.ds(0, 1), pl.ds(0, w)], (SCAN_SEQS, w))
        li = jnp.broadcast_to(pow_ref[pl.ds(0, 1), pl.ds(w, w)], (SCAN_SEQS, w))
        er, ei = _local_scan(x_scr, lr, li, w, sub, False)
        zr, zi, fr, fi = _entering_states(er, ei, carry[:, pl.ds(0, w)], carry[:, pl.ds(w, w)],
                                          pow_ref, w, sub, False)
        carry[:, pl.ds(0, w)] = fr
        carry[:, pl.ds(w, w)] = fi
        bnd_ref[:, pl.ds(0, w)] = fr
        bnd_ref[:, pl.ds(w, w)] = fi
        for j in range(sub):
            pr = pow_ref[pl.ds(j, 1), pl.ds(0, w)]
            pi = pow_ref[pl.ds(j, 1), pl.ds(w, w)]
            cr, ci = _cmul(pr, pi, zr, zi)
            x_scr[_scan_rows(j, sub), pl.ds(0, w)] += cr
            x_scr[_scan_rows(j, sub), pl.ds(w, w)] += ci
        hb = x_scr[...].astype(BF16)
        perm[...] = _dot_nt(hb, mc_ref[...])
        _unpermute_rows(perm, nat, sub)
        y_ref[...] = nat[...] + d_ref[...] * ugf

    in_specs = [pl.BlockSpec((None, q, cb), lambda b, t: (0, t, b)),
                pl.BlockSpec((None, cb, w2), lambda b, t: (b, 0, 0)),
                pl.BlockSpec((None, cb, w2), lambda b, t: (b, 0, 0)),
                pl.BlockSpec((None, powt.shape[1], w2), lambda b, t: (b, 0, 0)),
                pl.BlockSpec((1, cb), lambda b, t: (0, b))]
    out_specs = [pl.BlockSpec((q, cb), lambda b, t: (t, b)),
                 pl.BlockSpec((None, None, SCAN_SEQS, w2), lambda b, t: (b, t, 0, 0))]
    out_shape = [jax.ShapeDtypeStruct((tp, ds), F32), jax.ShapeDtypeStruct((nb, nt, SCAN_SEQS, w2), F32)]
    scratch = [pltpu.VMEM((q, w2), F32), pltpu.VMEM((SCAN_SEQS, w2), F32),
               pltpu.VMEM((q, cb), F32), pltpu.VMEM((q, cb), F32)]
    return pl.pallas_call(
        body, name="s5_scan_fwd", grid=(nb, nt), in_specs=in_specs, out_specs=out_specs,
        out_shape=out_shape, scratch_shapes=scratch, compiler_params=_params(("arbitrary", "arbitrary")),
    )(p, mb, mc, powt, dskip)


def _scan_bwd(p, dy, mb, mc, powt, dskip, bnd):
    _, tp, d = p.shape
    nb, cb, w2 = mb.shape
    w = w2 // 2
    q = SCAN_TILE
    sub = q // SCAN_SEQS
    nt = tp // q
    ds = d // 2

    def body(ug_ref, dy_ref, mb_ref, mc_ref, pow_ref, d_ref, bnd_ref,
             dug_ref, dmb_ref, dmc_ref, dlam_ref, dd_ref, x_scr, y_scr, gcarry, nat, perm):
        t = pl.program_id(1)
        tt = nt - 1 - t

        @pl.when(t == 0)
        def _():
            gcarry[...] = jnp.zeros_like(gcarry)
            dmb_ref[...] = jnp.zeros_like(dmb_ref)
            dmc_ref[...] = jnp.zeros_like(dmc_ref)
            dlam_ref[...] = jnp.zeros_like(dlam_ref)
            dd_ref[...] = jnp.zeros_like(dd_ref)

        ugf = ug_ref[...].astype(F32)
        dyf = dy_ref[...].astype(F32)
        dd_ref[...] += jnp.sum((dyf * ugf).reshape(q // SUBLANES, SUBLANES, cb), axis=0)
        nat[...] = ugf
        _permute_rows(nat, perm, sub)
        ug = perm[...].astype(BF16)
        nat[...] = dyf
        _permute_rows(nat, perm, sub)
        dyb = perm[...].astype(BF16)
        lr = jnp.broadcast_to(pow_ref[pl.ds(0, 1), pl.ds(0, w)], (SCAN_SEQS, w))
        li = jnp.broadcast_to(pow_ref[pl.ds(0, 1), pl.ds(w, w)], (SCAN_SEQS, w))

        x_scr[...] = _dot(ug, mb_ref[...])
        er, ei = _local_scan(x_scr, lr, li, w, sub, False)
        first = tt == 0
        pfr = jnp.where(first, 0.0, bnd_ref[:, pl.ds(0, w)])
        pfi = jnp.where(first, 0.0, bnd_ref[:, pl.ds(w, w)])
        hzr, hzi, _, _ = _entering_states(er, ei, pfr, pfi, pow_ref, w, sub, False)
        for j in range(sub):
            pr = pow_ref[pl.ds(j, 1), pl.ds(0, w)]
            pi = pow_ref[pl.ds(j, 1), pl.ds(w, w)]
            cr, ci = _cmul(pr, pi, hzr, hzi)
            x_scr[_scan_rows(j, sub), pl.ds(0, w)] += cr
            x_scr[_scan_rows(j, sub), pl.ds(w, w)] += ci

        y_scr[...] = _dot(dyb, mc_ref[...])
        er, ei = _local_scan(y_scr, lr, li, w, sub, True)
        gzr, gzi, fr, fi = _entering_states(er, ei, gcarry[:, pl.ds(0, w)], gcarry[:, pl.ds(w, w)],
                                            pow_ref, w, sub, True)
        gcarry[:, pl.ds(0, w)] = fr
        gcarry[:, pl.ds(w, w)] = fi
        accr = jnp.zeros((SCAN_SEQS, w), F32)
        acci = jnp.zeros((SCAN_SEQS, w), F32)
        for j in range(sub):
            pr = pow_ref[pl.ds(sub - 1 - j, 1), pl.ds(0, w)]
            pi = pow_ref[pl.ds(sub - 1 - j, 1), pl.ds(w, w)]
            cr, ci = _cmul(pr, -pi, gzr, gzi)
            gr = y_scr[_scan_rows(j, sub), pl.ds(0, w)] + cr
            gi = y_scr[_scan_rows(j, sub), pl.ds(w, w)] + ci
            y_scr[_scan_rows(j, sub), pl.ds(0, w)] = gr
            y_scr[_scan_rows(j, sub), pl.ds(w, w)] = gi
            if j == 0:
                hpr, hpi = hzr, hzi
            else:
                hpr = x_scr[_scan_rows(j - 1, sub), pl.ds(0, w)]
                hpi = x_scr[_scan_rows(j - 1, sub), pl.ds(w, w)]
            accr += hpr * gr + hpi * gi
            acci += hpr * gi - hpi * gr
        dlam_ref[:, pl.ds(0, w)] += accr
        dlam_ref[:, pl.ds(w, w)] += acci

        hb = x_scr[...].astype(BF16)
        gb = y_scr[...].astype(BF16)
        dmc_ref[...] += _dot_tn(dyb, hb)
        dmb_ref[...] += _dot_tn(ug, gb)
        perm[...] = _dot_nt(gb, mb_ref[...])
        _unpermute_rows(perm, nat, sub)
        dug_ref[...] = (nat[...] + d_ref[...] * dyf).astype(BF16)

    blk = lambda b, t: (b, 0, 0)
    return pl.pallas_call(
        body, name="s5_scan_bwd", grid=(nb, nt),
        in_specs=[pl.BlockSpec((None, q, cb), lambda b, t: (0, nt - 1 - t, b)),
                  pl.BlockSpec((q, cb), lambda b, t: (nt - 1 - t, b)),
                  pl.BlockSpec((None, cb, w2), blk),
                  pl.BlockSpec((None, cb, w2), blk),
                  pl.BlockSpec((None, powt.shape[1], w2), blk),
                  pl.BlockSpec((1, cb), lambda b, t: (0, b)),
                  pl.BlockSpec((None, None, SCAN_SEQS, w2),
                               lambda b, t: (b, jnp.maximum(nt - 2 - t, 0), 0, 0))],
        out_specs=[pl.BlockSpec((q, cb), lambda b, t: (nt - 1 - t, b)),
                   pl.BlockSpec((None, cb, w2), blk),
                   pl.BlockSpec((None, cb, w2), blk),
                   pl.BlockSpec((None, SCAN_SEQS, w2), blk),
                   pl.BlockSpec((SUBLANES, cb), lambda b, t: (0, b))],
        out_shape=[jax.ShapeDtypeStruct((tp, ds), BF16),
                   jax.ShapeDtypeStruct((nb, cb, w2), F32),
                   jax.ShapeDtypeStruct((nb, cb, w2), F32),
                   jax.ShapeDtypeStruct((nb, SCAN_SEQS, w2), F32),
                   jax.ShapeDtypeStruct((SUBLANES, ds), F32)],
        scratch_shapes=[pltpu.VMEM((q, w2), F32), pltpu.VMEM((q, w2), F32),
                        pltpu.VMEM((SCAN_SEQS, w2), F32), pltpu.VMEM((q, cb), F32), pltpu.VMEM((q, cb), F32)],
        compiler_params=_params(("arbitrary", "arbitrary")),
    )(p, dy, mb, mc, powt, dskip, bnd)


HALO = 16


def _mix_tile(ys5, p0, p1, p2, p3, prev_cin, cw, bgate, wglu, wco, d):
    dh = d // 2
    tm = ys5.shape[0]
    v = p0[:, dh:].astype(F32)
    gbr = p1[:, :dh].astype(F32)
    gcr = p1[:, dh:].astype(F32)
    gact = _gelu(ys5).astype(BF16)
    z = _dot(gact, wglu)
    z1, z2 = z[:, :d], z[:, d:]
    sg = _sigmoid(z2)
    y_ssm = z1 * sg
    cin = gcr * v
    ext = jnp.concatenate([cin, prev_cin], axis=0)
    r1 = pltpu.roll(ext, 1, 0)[:tm]
    r2 = pltpu.roll(ext, 2, 0)[:tm]
    cv = cw[2] * cin + cw[1] * r1 + cw[0] * r2
    cg = (gbr * cv).astype(BF16)
    y_conv = _dot(cg, wco)
    g_s = _sigmoid(p2.astype(F32) + bgate[:, :d])
    g_c = _sigmoid(p3.astype(F32) + bgate[:, d:])
    mixed = g_s * y_ssm + g_c * y_conv
    return dict(v=v, gb=gbr, gc=gcr, gact=gact, z1=z1, sg=sg, y_ssm=y_ssm, cin=cin, r1=r1, r2=r2,
                cv=cv, cg=cg, y_conv=y_conv, g_s=g_s, g_c=g_c, mixed=mixed)


def _mix_fwd(h1, ys5, p, cw, bgate, wglu, wco, wo, carry=None):
    tp, d = h1.shape
    dh = d // 2
    tm = ROW_ALIGN
    ni = tp // tm

    def body(*refs):
        refs, phases = split(refs)
        (h_ref, y_ref, p0_ref, p1_ref, p2_ref, p3_ref, cw_ref, bg_ref, wglu_ref, wco_ref, wo_ref,
         o_ref, prev) = refs
        _run_phases(phases, carry, pl.program_id(0), ni)

        @pl.when(pl.program_id(0) == 0)
        def _():
            prev[...] = jnp.zeros_like(prev)

        cw = [cw_ref[pl.ds(t, 1), :] for t in range(3)]
        f = _mix_tile(y_ref[...], p0_ref[...], p1_ref[...], p2_ref[...], p3_ref[...], prev[...],
                      cw, bg_ref[...], wglu_ref[...], wco_ref[...], d)
        prev[...] = f["cin"][tm - HALO:, :]
        o_ref[...] = h_ref[...] + _dot(f["mixed"].astype(BF16), wo_ref[...])

    row = pl.BlockSpec((tm, d), lambda i: (i, 0))
    full = lambda a: pl.BlockSpec(a.shape, lambda i: (0,) * a.ndim)
    pk = lambda k: pl.BlockSpec((None, tm, d), lambda i, k=k: (k, i, 0))
    in_specs = [row, pl.BlockSpec((tm, dh), lambda i: (i, 0)), pk(0), pk(1), pk(2), pk(3),
                full(cw), full(bgate), full(wglu), full(wco), full(wo)]
    out_specs, out_shape = [row], [jax.ShapeDtypeStruct((tp, d), F32)]
    args, scratch = [h1, ys5, p, p, p, p, cw, bgate, wglu, wco, wo], [pltpu.VMEM((HALO, dh), F32)]
    split = _attach_carry(carry, in_specs, args, out_specs, out_shape, scratch)
    return pl.pallas_call(
        body, name="mix_fwd", grid=(ni,), in_specs=in_specs, out_specs=out_specs, out_shape=out_shape,
        scratch_shapes=scratch, compiler_params=_params(("arbitrary",)),
    )(*args)


def _mix_bwd(dh2, ys5, p, cw, bgate, wglu, wco, wo):
    tp, d = dh2.shape
    dh = d // 2
    tm = ROW_ALIGN
    ni = tp // tm
    hb = tm // HALO

    def body(dh_ref, y_ref, p0_ref, p1_ref, p2_ref, p3_ref, h0_ref, h1_ref,
             cw_ref, bg_ref, wglu_ref, wco_ref, wo_ref,
             dys_ref, dpb_ref, dwo_ref, dwglu_ref, dwco_ref, dcw_ref, dbg_ref, nxt):
        i = pl.program_id(0)
        tt = ni - 1 - i

        @pl.when(i == 0)
        def _():
            nxt[...] = jnp.zeros_like(nxt)
            dwo_ref[...] = jnp.zeros_like(dwo_ref)
            dwglu_ref[...] = jnp.zeros_like(dwglu_ref)
            dwco_ref[...] = jnp.zeros_like(dwco_ref)
            dcw_ref[...] = jnp.zeros_like(dcw_ref)
            dbg_ref[...] = jnp.zeros_like(dbg_ref)

        cw = [cw_ref[pl.ds(t, 1), :] for t in range(3)]
        prev_cin = h1_ref[:, dh:].astype(F32) * h0_ref[:, dh:].astype(F32)
        prev_cin = jnp.where(tt == 0, 0.0, prev_cin)
        ys5 = y_ref[...]
        f = _mix_tile(ys5, p0_ref[...], p1_ref[...], p2_ref[...], p3_ref[...], prev_cin,
                      cw, bg_ref[...], wglu_ref[...], wco_ref[...], d)
        dhb = dh_ref[...].astype(BF16)
        dmixed = _dot_nt(dhb, wo_ref[...])
        dwo_ref[...] += _dot_tn(f["mixed"].astype(BF16), dhb)

        g_s, g_c, sg = f["g_s"], f["g_c"], f["sg"]
        dy_ssm = dmixed * g_s
        dy_conv = dmixed * g_c
        dp2 = dmixed * f["y_ssm"] * g_s * (1.0 - g_s)
        dp3 = dmixed * f["y_conv"] * g_c * (1.0 - g_c)
        dbg_ref[:, pl.ds(0, d)] += jnp.sum(dp2, axis=0, keepdims=True)
        dbg_ref[:, pl.ds(d, d)] += jnp.sum(dp3, axis=0, keepdims=True)

        dz = jnp.concatenate([dy_ssm * sg, dy_ssm * f["z1"] * sg * (1.0 - sg)], axis=1).astype(BF16)
        dwglu_ref[...] += _dot_tn(f["gact"], dz)
        dys_ref[...] = (_dot_nt(dz, wglu_ref[...]) * _gelu_grad(ys5)).astype(BF16)

        dycb = dy_conv.astype(BF16)
        dwco_ref[...] += _dot_tn(f["cg"], dycb)
        dcg = _dot_nt(dycb, wco_ref[...])
        dgb = dcg * f["cv"]
        dcv = dcg * f["gb"]
        ext = jnp.concatenate([dcv, nxt[...]], axis=0)
        n1 = pltpu.roll(ext, tm + HALO - 1, 0)[:tm]
        n2 = pltpu.roll(ext, tm + HALO - 2, 0)[:tm]
        nxt[...] = dcv[:HALO, :]
        dcin = cw[2] * dcv + cw[1] * n1 + cw[0] * n2
        dcw_ref[pl.ds(0, 1), :] += jnp.sum(dcv * f["r2"], axis=0, keepdims=True)
        dcw_ref[pl.ds(1, 1), :] += jnp.sum(dcv * f["r1"], axis=0, keepdims=True)
        dcw_ref[pl.ds(2, 1), :] += jnp.sum(dcv * f["cin"], axis=0, keepdims=True)
        dgc = dcin * f["v"]
        dv = dcin * f["gc"]
        dpb_ref[0] = jnp.concatenate([jnp.zeros_like(dv), dv], axis=1).astype(BF16)
        dpb_ref[1] = jnp.concatenate([dgb, dgc], axis=1).astype(BF16)
        dpb_ref[2] = dp2.astype(BF16)
        dpb_ref[3] = dp3.astype(BF16)

    rev = lambda i: ni - 1 - i
    row = pl.BlockSpec((tm, d), lambda i: (rev(i), 0))
    half = pl.BlockSpec((tm, dh), lambda i: (rev(i), 0))
    full = lambda a: pl.BlockSpec(a.shape, lambda i: (0,) * a.ndim)
    pk = lambda k: pl.BlockSpec((None, tm, d), lambda i, k=k: (k, rev(i), 0))
    halo = lambda k: pl.BlockSpec((None, HALO, d), lambda i, k=k: (k, jnp.maximum(rev(i) * hb - 1, 0), 0))
    acc = lambda shape: pl.BlockSpec(shape, lambda i: (0,) * len(shape))
    return pl.pallas_call(
        body, name="mix_bwd", grid=(ni,),
        in_specs=[row, half, pk(0), pk(1), pk(2), pk(3), halo(0), halo(1),
                  full(cw), full(bgate), full(wglu), full(wco), full(wo)],
        out_specs=[half, pl.BlockSpec((4, tm, d), lambda i: (0, rev(i), 0)),
                   acc((d, d)), acc((dh, 2 * d)), acc((dh, d)), acc((SUBLANES, dh)), acc((1, 2 * d))],
        out_shape=[jax.ShapeDtypeStruct((tp, dh), BF16), jax.ShapeDtypeStruct((4, tp, d), BF16),
                   jax.ShapeDtypeStruct((d, d), F32), jax.ShapeDtypeStruct((dh, 2 * d), F32),
                   jax.ShapeDtypeStruct((dh, d), F32), jax.ShapeDtypeStruct((SUBLANES, dh), F32),
                   jax.ShapeDtypeStruct((1, 2 * d), F32)],
        scratch_shapes=[pltpu.VMEM((HALO, dh), F32)],
        compiler_params=_params(("arbitrary",)),
    )(dh2, ys5, p, p, p, p, p, p, cw, bgate, wglu, wco, wo)


ANY = pl.BlockSpec(memory_space=pl.ANY)


def _position():
    return lax.axis_index("x"), lax.axis_index("y"), lax.axis_index("c")


def _remote(src, dst, ssem, rsem, dev):
    return pltpu.make_async_remote_copy(src_ref=src, dst_ref=dst, send_sem=ssem, recv_sem=rsem,
                                        device_id=dev, device_id_type=MESH)


def _cast_pieces(ws, pos):
    n = len(ws)

    def body(pos_ref, *refs):
        for w_ref, o_ref in zip(refs[:n], refs[n:]):
            o_ref[...] = w_ref[...].astype(BF16)

    halves = [(w.shape[0] // 2, w.shape[1]) for w in ws]
    return pl.pallas_call(
        body, name="cast_pieces",
        grid_spec=pltpu.PrefetchScalarGridSpec(
            num_scalar_prefetch=1, grid=(1,),
            in_specs=[pl.BlockSpec(hs, lambda i, pos: (pos[2], 0)) for hs in halves],
            out_specs=[pl.BlockSpec(hs, lambda i, pos: (0, 0)) for hs in halves]),
        out_shape=[jax.ShapeDtypeStruct(hs, BF16) for hs in halves],
        compiler_params=_params(("arbitrary",)),
    )(pos, *ws)


class _Carry:
    def __init__(self, name, arrays, out_shapes, nsem, nlsem, make, fracs):
        self.name, self.arrays, self.out_shapes = name, list(arrays), list(out_shapes)
        self.nsem, self.nlsem, self.make, self.fracs = nsem, max(nlsem, 1), make, fracs


def _carry_scratch(carry):
    return [pltpu.SemaphoreType.DMA((carry.nsem,)), pltpu.SemaphoreType.DMA((carry.nsem,)),
            pltpu.SemaphoreType.DMA((carry.nlsem,))]


def _run_carry(carry):
    na, no = len(carry.arrays), len(carry.out_shapes)

    def body(*refs):
        for phase in carry.make(refs[:na], refs[na:na + no], *refs[na + no:]):
            phase()

    return pl.pallas_call(
        body, name=carry.name, in_specs=[ANY] * na, out_specs=[ANY] * no, out_shape=carry.out_shapes,
        scratch_shapes=_carry_scratch(carry),
    )(*carry.arrays)


def _attach_carry(carry, in_specs, args, out_specs, out_shape, scratch):
    nhi, nho, nhs = len(in_specs), len(out_specs), len(scratch)
    if carry is None:
        return lambda refs: (list(refs), [])
    na, no = len(carry.arrays), len(carry.out_shapes)
    in_specs += [ANY] * na
    args += carry.arrays
    out_specs += [ANY] * no
    out_shape += carry.out_shapes
    scratch += _carry_scratch(carry)

    def split(refs):
        refs = list(refs)
        o = nhi + na
        host = refs[:nhi] + refs[o:o + nho] + refs[o + nho + no:o + nho + no + nhs]
        sems = refs[o + nho + no + nhs:]
        return host, carry.make(refs[nhi:o], refs[o + nho:o + nho + no], *sems)

    return split


def _run_phases(phases, carry, step, total):
    for phase, frac in zip(phases, carry.fracs if carry is not None else ()):
        pl.when(step == int(round(frac * (total - 1))))(phase)


def _allgather_carry(name, pieces, smalls, rows=None):
    n, ns = len(pieces), len(smalls)
    per = 14
    n_big = per * n
    chunks = max(c for c in range(1, MAX_ROW_CHUNKS + 1) if rows[0].shape[0] % (c * SUBLANES) == 0) if rows else 0

    def make(ins, outs, ssem, rsem, lsem):
        pin, sin = ins[:n], ins[n:n + ns]
        wall, sall = outs[:n], outs[n:n + ns]

        def row_copies():
            n_meta, seq = smalls[0].shape[0], rows[0].shape[0]
            h0, step = outs[n + ns], seq // chunks
            cps = [pltpu.make_async_copy(ins[n + ns].at[pl.ds(j * step, step), :],
                                         h0.at[pl.ds(n_meta + j * step, step), :], lsem.at[2 * n + ns + 5 + j])
                   for j in range(chunks)]
            cps.append(pltpu.make_async_copy(ins[n + ns + 1], h0.at[pl.ds(n_meta + seq, rows[1].shape[0]), :],
                                             lsem.at[2 * n + ns + 4]))
            return cps

        def meta_copies():
            n_meta, mc = smalls[0].shape
            h0 = outs[n + ns]
            return [pltpu.make_async_copy(sall[0].at[k], h0.at[pl.ds(0, n_meta), pl.ds(k * mc, mc)],
                                          lsem.at[2 * n + ns + k]) for k in range(4)]

        x, y, c = _position()
        xnb, ynb, sib = (1 - x, y, c), (x, 1 - y, c), (x, y, 1 - c)
        chips = [(1 - x, y), (x, 1 - y), (1 - x, 1 - y)]
        r4 = [p.shape[0] // 2 for p in pin]
        own = lambda i, h: pin[i].at[pl.ds(h * r4[i], r4[i]), :]
        slot = lambda i, xx, yy, cc, h: wall[i].at[xx, yy, cc, h]
        cp = lambda src, dst, s, dev: _remote(src, dst, ssem.at[s], rsem.at[s], dev)
        to_sib = lambda i, xx, yy, h: cp(slot(i, xx, yy, c, h), slot(i, xx, yy, c, h),
                                         per * i + 6 + 4 * xx + 2 * yy + h, sib)

        def local():
            cps = [pltpu.make_async_copy(own(i, h), slot(i, x, y, c, h), lsem.at[2 * i + h])
                   for i in range(n) for h in range(2)]
            return cps + [pltpu.make_async_copy(sin[i], sall[i].at[2 * x + y], lsem.at[2 * n + i])
                          for i in range(ns)]

        def small(px, py, j, i, landing):
            s = n_big + j * ns + i
            return cp(sin[i], sall[i].at[landing], s, (px, py, c))

        def first_hop():
            for lc in local() + (row_copies() if rows else []):
                lc.start()
            for j, (px, py) in enumerate(chips):
                for i in range(ns):
                    small(px, py, j, i, 2 * x + y).start()
            for i in range(n):
                cp(own(i, 0), slot(i, x, y, c, 0), per * i, xnb).start()
                cp(own(i, 1), slot(i, x, y, c, 1), per * i + 1, ynb).start()
                for h in range(2):
                    cp(own(i, h), slot(i, x, y, c, h), per * i + 6 + 4 * x + 2 * y + h, sib).start()

        def second_hop():
            for lc in local():
                lc.wait()
            for i in range(n):
                cp(slot(i, 1 - x, y, c, 0), slot(i, 1 - x, y, c, 0), per * i, xnb).wait_recv()
                cp(slot(i, x, 1 - y, c, 1), slot(i, x, 1 - y, c, 1), per * i + 1, ynb).wait_recv()
                for j in range(2):
                    cp(slot(i, j, y, c, 0), slot(i, j, y, c, 0), per * i + 2 + j, ynb).start()
                    cp(slot(i, x, j, c, 1), slot(i, x, j, c, 1), per * i + 4 + j, xnb).start()
                to_sib(i, 1 - x, y, 0).start()
                to_sib(i, x, 1 - y, 1).start()

        def last_to_sibling():
            for i in range(n):
                for j in range(2):
                    cp(slot(i, j, 1 - y, c, 0), slot(i, j, 1 - y, c, 0), per * i + 2 + j, ynb).wait_recv()
                    cp(slot(i, 1 - x, j, c, 1), slot(i, 1 - x, j, c, 1), per * i + 4 + j, xnb).wait_recv()
                    to_sib(i, j, 1 - y, 0).start()
                    to_sib(i, 1 - x, j, 1).start()

        def finish():
            for i in range(n):
                for xx in range(2):
                    for yy in range(2):
                        for h in range(2):
                            s = per * i + 6 + 4 * xx + 2 * yy + h
                            cp(slot(i, xx, yy, 1 - c, h), slot(i, xx, yy, 1 - c, h), s, sib).wait_recv()
                            to_sib(i, xx, yy, h).wait_send()
                cp(own(i, 0), slot(i, x, y, c, 0), per * i, xnb).wait_send()
                cp(own(i, 1), slot(i, x, y, c, 1), per * i + 1, ynb).wait_send()
                for j in range(2):
                    cp(slot(i, j, y, c, 0), slot(i, j, y, c, 0), per * i + 2 + j, ynb).wait_send()
                    cp(slot(i, x, j, c, 1), slot(i, x, j, c, 1), per * i + 4 + j, xnb).wait_send()
            for j, (px, py) in enumerate(chips):
                for i in range(ns):
                    small(px, py, j, i, 2 * px + py).wait_recv()
                    small(px, py, j, i, 2 * x + y).wait_send()
            if rows:
                for lc in meta_copies():
                    lc.start()
                for lc in meta_copies() + row_copies():
                    lc.wait()

        return [first_hop, second_hop, last_to_sibling, finish]

    out_shapes = [jax.ShapeDtypeStruct((2, 2, 2, 2, a.shape[0] // 2, a.shape[1]), a.dtype) for a in pieces]
    out_shapes += [jax.ShapeDtypeStruct((4,) + a.shape, a.dtype) for a in smalls]
    arrays = list(pieces) + list(smalls)
    if rows:
        arrays += list(rows)
        total = smalls[0].shape[0] + rows[0].shape[0] + rows[1].shape[0]
        out_shapes.append(jax.ShapeDtypeStruct((total, rows[0].shape[1]), rows[0].dtype))
    return _Carry(name, arrays, out_shapes, n_big + 3 * ns, 2 * n + ns + 5 + chunks, make, (0.0, 0.23, 0.73, 1.0))


def _exchange_carry(name, arrays, out_shapes, plan):
    count = plan([None] * len(arrays), [None] * len(out_shapes), None)

    def make(ins, outs, ssem, rsem, lsem):
        def copies():
            return [_remote(src, dst, ssem.at[j], rsem.at[j], peer)
                    for j, (src, dst, peer) in enumerate(plan(ins, outs, _position()))]

        def start():
            for c in copies():
                c.start()

        def wait():
            for c in copies():
                c.wait()

        return [start, wait]

    return _Carry(name, arrays, out_shapes, count, 0, make, (0.0, 1.0))


class _Grad:
    def __init__(self, arrs, kind, shard_shape):
        self.arrs, self.kind = list(arrs), kind
        self.rows, self.cols = shard_shape
        self.r2 = self.rows // 2

    def view(self, refs, k, h):
        r2 = self.r2
        if self.kind == "list":
            return refs[k].at[pl.ds(h * r2, r2), :]
        if self.kind == "stacked":
            return refs[0].at[k, pl.ds(h * r2, r2), :]
        if self.kind == "col":
            return refs[0].at[pl.ds(h * r2, r2), pl.ds(k * self.cols, self.cols)]
        return refs[0].at[pl.ds((2 * k + h) * r2, r2), :]

    def add_half(self, recv, pos):
        r2, cols = self.r2, self.cols
        n_in = len(self.arrs)
        tr = _row_tile(r2, cols)
        nt = r2 // tr

        def body(pos_ref, *refs):
            m_refs, (r_ref, of_ref, ob_ref) = refs[:n_in], refs[n_in:]
            mine = m_refs[0][...]
            for kk in range(1, n_in):
                mine = jnp.where(pl.program_id(1) == kk, m_refs[kk][...], mine)
            s = mine.astype(F32) + r_ref[...].astype(F32)
            of_ref[...] = s
            ob_ref[...] = s.astype(BF16)

        row = lambda t, pos: pos[2] * nt + t
        if self.kind == "list":
            specs = [pl.BlockSpec((tr, cols), lambda t, k, pos: (row(t, pos), 0))] * n_in
        elif self.kind == "stacked":
            specs = [pl.BlockSpec((None, tr, cols), lambda t, k, pos: (k, row(t, pos), 0))]
        elif self.kind == "col":
            specs = [pl.BlockSpec((tr, cols), lambda t, k, pos: (row(t, pos), k))]
        else:
            specs = [pl.BlockSpec((tr, cols), lambda t, k, pos: (2 * k * nt + row(t, pos), 0))]
        blk = pl.BlockSpec((None, tr, cols), lambda t, k, pos: (k, t, 0))
        return pl.pallas_call(
            body, name="rs_add_c",
            grid_spec=pltpu.PrefetchScalarGridSpec(
                num_scalar_prefetch=1, grid=(nt, 4), in_specs=specs + [blk], out_specs=[blk, blk]),
            out_shape=[jax.ShapeDtypeStruct((4, r2, cols), F32), jax.ShapeDtypeStruct((4, r2, cols), BF16)],
            compiler_params=_params(("arbitrary", "arbitrary")),
        )(pos, *self.arrs, recv)


def _row_tile(rows, cols):
    fits = [t for t in range(16, rows + 1, 16) if rows % t == 0 and t * cols * 4 <= 2 * 1024 * 1024]
    return max(fits) if fits else rows


def _adamw_math(w, g, m, v):
    m = ADAM_B1 * m + (1.0 - ADAM_B1) * g
    v = ADAM_B2 * v + (1.0 - ADAM_B2) * (g * g)
    m_hat = m / (1.0 - ADAM_B1 ** ADAM_STEP)
    v_hat = v / (1.0 - ADAM_B2 ** ADAM_STEP)
    delta = -ADAM_LR * (m_hat / (jnp.sqrt(v_hat) + ADAM_EPS) + ADAM_WD * w)
    return delta, m, v


def _adamw_big(w, m, v, own, sib, pos):
    rows, cols = w.shape
    r2 = rows // 2

    tr = _row_tile(r2, cols)
    nt = r2 // tr

    def body(pos_ref, w_ref, m_ref, v_ref, own_ref, sib_ref, g_ref, d_ref, nm_ref, nv_ref):
        h = pl.program_id(0)
        g = jnp.where(h == pos_ref[2], own_ref[...], sib_ref[...])
        g_ref[...] = g
        d_ref[...], nm_ref[...], nv_ref[...] = _adamw_math(w_ref[...], g, m_ref[...], v_ref[...])

    half = pl.BlockSpec((tr, cols), lambda h, t, pos: (h * nt + t, 0))
    piece = pl.BlockSpec((tr, cols), lambda h, t, pos: (t, 0))
    out = jax.ShapeDtypeStruct((rows, cols), F32)
    return pl.pallas_call(
        body, name="adamw",
        grid_spec=pltpu.PrefetchScalarGridSpec(
            num_scalar_prefetch=1, grid=(2, nt),
            in_specs=[half, half, half, piece, piece],
            out_specs=[half, half, half, half]),
        out_shape=[out, out, out, out],
        compiler_params=_params(("arbitrary", "arbitrary")),
    )(pos, w, m, v, own, sib)


def _add_hop1(s1fs, recvs, pos):
    n = len(s1fs)
    s1vs = [s.reshape((4, 2) + r.shape[2:]) for s, r in zip(s1fs, recvs)]

    def body(pos_ref, *refs):
        for m_ref, r_ref, of_ref, ob_ref in zip(refs[:n], refs[n:2 * n], refs[2 * n:3 * n], refs[3 * n:]):
            s = m_ref[...] + r_ref[...].astype(F32)
            of_ref[...] = s
            ob_ref[...] = s.astype(BF16)

    def mine(h, j, pos):
        return (jnp.where(h == 0, 2 * j + pos[1], 2 * pos[0] + j), h, 0, 0)

    tile = lambda r: (None, None) + r.shape[2:]
    blks = [pl.BlockSpec(tile(r), lambda h, j, pos: (h, j, 0, 0)) for r in recvs]
    outs = pl.pallas_call(
        body, name="rs_add_1",
        grid_spec=pltpu.PrefetchScalarGridSpec(
            num_scalar_prefetch=1, grid=(2, 2),
            in_specs=[pl.BlockSpec(tile(r), mine) for r in recvs] + blks, out_specs=blks + blks),
        out_shape=[jax.ShapeDtypeStruct(r.shape, F32) for r in recvs]
        + [jax.ShapeDtypeStruct(r.shape, BF16) for r in recvs],
        compiler_params=_params(("arbitrary", "arbitrary")),
    )(pos, *s1vs, *recvs)
    return list(zip(outs[:n], outs[n:]))


def _own_sum(s2fs, recvs, pos):
    n = len(s2fs)

    def body(pos_ref, *refs):
        for s_ref, r_ref, o_ref in zip(refs[:n], refs[n:2 * n], refs[2 * n:]):
            o_ref[...] = s_ref[...] + r_ref[...].astype(F32)

    blks = [pl.BlockSpec((None,) + r.shape[1:], lambda h, pos: (h, 0, 0)) for r in recvs]
    return pl.pallas_call(
        body, name="own_sum",
        grid_spec=pltpu.PrefetchScalarGridSpec(
            num_scalar_prefetch=1, grid=(2,),
            in_specs=[pl.BlockSpec((None, None) + r.shape[1:],
                                   lambda h, pos: (h, jnp.where(h == 0, pos[0], pos[1]), 0, 0)) for r in recvs]
            + blks, out_specs=blks),
        out_shape=[jax.ShapeDtypeStruct(r.shape, F32) for r in recvs],
        compiler_params=_params(("arbitrary",)),
    )(pos, *s2fs, *recvs)


def _add_small(a, b):
    def body(a_ref, b_ref, o_ref):
        o_ref[...] = a_ref[...] + b_ref[...]

    vm = pl.BlockSpec(memory_space=pltpu.VMEM)
    return pl.pallas_call(body, name="add_small", in_specs=[vm, vm], out_specs=vm,
                          out_shape=jax.ShapeDtypeStruct(a.shape, F32))(a, b)


def _adamw_small(ws, gs, ms, vs):
    n = len(ws)

    def body(*refs):
        for i in range(n):
            w_ref, g_ref, m_ref, v_ref, d_ref, nm_ref, nv_ref = (refs[j * n + i] for j in range(7))
            d_ref[...], nm_ref[...], nv_ref[...] = _adamw_math(w_ref[...], g_ref[...], m_ref[...], v_ref[...])

    vm = pl.BlockSpec(memory_space=pltpu.VMEM)
    outs = pl.pallas_call(body, name="adamw_small", in_specs=[vm] * (4 * n), out_specs=[vm] * (3 * n),
                          out_shape=[jax.ShapeDtypeStruct(w.shape, F32) for w in ws] * 3)(*ws, *gs, *ms, *vs)
    return outs[:n], outs[n:2 * n], outs[2 * n:]


class _ReduceScatter:
    def __init__(self, tag, grads, pos, extra=None):
        self.tag, self.grads, self.pos, self.stage, self.extra = tag, grads, pos, 0, extra

    def carry(self):
        grads, n = self.grads, len(self.grads)
        r4 = [g.r2 // 2 for g in grads]

        first = [sum(len(g.arrs) for g in grads[:i]) for i in range(n)]

        def plan_c(ins, outs, p):
            if p is None:
                return 4 * n
            x, y, c = p
            mine = lambda i: ins[first[i]:first[i] + len(grads[i].arrs)]
            return [(grads[i].view(mine(i), k, 1 - c), outs[i].at[k], (x, y, 1 - c))
                    for i in range(n) for k in range(4)]

        def plan_1(ins, outs, p):
            if p is None:
                return 4 * n
            x, y, c = p
            copies = []
            for i in range(n):
                for j in range(2):
                    copies.append((ins[i].at[2 * j + (1 - y), pl.ds(0, r4[i]), :], outs[i].at[0, j],
                                   (x, 1 - y, c)))
                    copies.append((ins[i].at[2 * (1 - x) + j, pl.ds(r4[i], r4[i]), :], outs[i].at[1, j],
                                   (1 - x, y, c)))
            return copies

        def plan_2(ins, outs, p):
            if p is None:
                return 2 * n
            x, y, c = p
            copies = []
            for i in range(n):
                copies.append((ins[i].at[0, 1 - x], outs[i].at[0], (1 - x, y, c)))
                copies.append((ins[i].at[1, 1 - y], outs[i].at[1], (x, 1 - y, c)))
            return copies

        def plan_s(ins, outs, p):
            if p is None:
                return n
            x, y, c = p
            return [(ins[i], outs[i], (x, y, 1 - c)) for i in range(n)]

        shape = lambda lead, dt: [jax.ShapeDtypeStruct(lead(g) + (g.cols,), dt) for g in grads]
        stage = self.stage
        if stage == 0:
            name, arrays, plan = "exchange_c", [a for g in grads for a in g.arrs], plan_c
            shapes = [jax.ShapeDtypeStruct((4, g.r2, g.cols), g.arrs[0].dtype) for g in grads]
        elif stage == 1:
            name, arrays, plan = "exchange_1", [s[1] for s in self.s1], plan_1
            shapes = shape(lambda g: (2, 2, g.r2 // 2), BF16)
        elif stage == 2:
            name, arrays, plan = "exchange_2", [s[1] for s in self.s2], plan_2
            shapes = shape(lambda g: (2, g.r2 // 2), BF16)
        else:
            name, arrays, plan, shapes = "exchange_sibling", self.own, plan_s, shape(lambda g: (g.r2,), F32)
        if self.extra is not None and stage < 3:
            def with_extra(ins, outs, p, plan=plan):
                if p is None:
                    return plan(ins[:-1], outs[:-1], None) + 1
                x, y, c = p
                peer = [(x, y, 1 - c), (x, 1 - y, c), (1 - x, y, c)][stage]
                return plan(ins[:-1], outs[:-1], p) + [(ins[-1], outs[-1], peer)]

            arrays = arrays + [self.extra]
            shapes = shapes + [jax.ShapeDtypeStruct(self.extra.shape, F32)]
            plan = with_extra
        return _exchange_carry(f"rs_{self.tag}_{name}", arrays, shapes, plan)

    def feed(self, recv):
        grads, pos = self.grads, self.pos
        recv = list(recv)
        if self.extra is not None and self.stage < 3:
            self.extra = _add_small(self.extra, recv.pop())
        if self.stage == 0:
            self.s1 = [g.add_half(r, pos) for g, r in zip(grads, recv)]
        elif self.stage == 1:
            self.s2 = _add_hop1([s[0] for s in self.s1], list(recv), pos)
        elif self.stage == 2:
            own = _own_sum([s[0] for s in self.s2], list(recv), pos)
            self.own = [o.reshape(g.r2, g.cols) for g, o in zip(grads, own)]
        else:
            self.sib = list(recv)
        self.stage += 1

    def run(self):
        while self.stage < 4:
            self.feed(_run_carry(self.carry()))

    def adamw(self, weights):
        return [_adamw_big(w, m, v, o, sb, self.pos) for (w, m, v), o, sb in zip(weights, self.own, self.sib)]


def _block_diag(t, nb):
    g, c, p = t.shape
    gb = g // nb
    t = t.reshape(nb, gb, c, p)
    eye = jnp.eye(gb, dtype=t.dtype)
    return jnp.einsum("bgcp,gh->bgchp", t, eye).reshape(nb, gb * c, gb * p)


def _s5_discretise(a_re, a_im, log_dt, b_re, b_im, c_re, c_im):
    g, p = a_re.shape
    nb = g // GROUPS_PER_BLOCK
    dt = jnp.exp(log_dt)[:, None]
    mag = jnp.exp(a_re * dt)
    lam_re = mag * jnp.cos(a_im * dt)
    lam_im = mag * jnp.sin(a_im * dt)
    den = a_re * a_re + a_im * a_im
    q_re = ((lam_re - 1.0) * a_re + lam_im * a_im) / den
    q_im = (lam_im * a_re - (lam_re - 1.0) * a_im) / den
    bb_re = q_re[..., None] * b_re - q_im[..., None] * b_im
    bb_im = q_re[..., None] * b_im + q_im[..., None] * b_re
    tr = lambda t: jnp.swapaxes(t, 1, 2)
    mb = jnp.concatenate([_block_diag(tr(bb_re), nb), _block_diag(tr(bb_im), nb)], axis=-1)
    mc = jnp.concatenate([_block_diag(c_re, nb), -_block_diag(c_im, nb)], axis=-1)
    lam = jnp.concatenate([lam_re.reshape(nb, -1), lam_im.reshape(nb, -1)], axis=-1)
    return mb, mc, lam


def _s5_powers(a_re, a_im, log_dt, sub):
    g, p = a_re.shape
    nb = g // GROUPS_PER_BLOCK
    dt = jnp.exp(log_dt)[:, None]
    ns = list(range(1, sub + 1)) + [sub << m for m in range(1, SCAN_SEQS.bit_length() - 1)]
    ns += [0] * (-len(ns) % SUBLANES)
    e = jnp.asarray(ns, F32)[:, None, None]
    mag = jnp.exp(a_re[None] * dt[None] * e)
    ang = a_im[None] * dt[None] * e
    re = (mag * jnp.cos(ang)).reshape(len(ns), nb, -1)
    im = (mag * jnp.sin(ang)).reshape(len(ns), nb, -1)
    return jnp.transpose(jnp.concatenate([re, im], axis=-1), (1, 0, 2))


def _pack(parts):
    flat = jnp.concatenate([a.reshape(-1).astype(F32) for a in parts])
    n = flat.shape[0]
    pad = -n % (SUBLANES * LANES)
    return jnp.pad(flat, (0, pad)).reshape(-1, LANES)


def _unpack(buf, like):
    flat = buf.reshape(-1)
    out, o = [], 0
    for a in like:
        out.append(flat[o:o + a.size].reshape(a.shape))
        o += a.size
    return out


def kernel(x, meta_tokens, g_ffn1, ffn1_w_gate, ffn1_w_up, ffn1_w_down, g_mix, w_in, b_gate, ssm_a_re, ssm_a_im, ssm_log_dt, ssm_b_re, ssm_b_im, ssm_c_re, ssm_c_im, ssm_d, ssm_w_glu, conv_w, conv_w_out, w_o, g_ffn2, ffn2_w_gate, ffn2_w_up, ffn2_w_down, g_final, loss_target, m_meta_tokens, m_g_ffn1, m_ffn1_w_gate, m_ffn1_w_up, m_ffn1_w_down, m_g_mix, m_w_in, m_b_gate, m_ssm_a_re, m_ssm_a_im, m_ssm_log_dt, m_ssm_b_re, m_ssm_b_im, m_ssm_c_re, m_ssm_c_im, m_ssm_d, m_ssm_w_glu, m_conv_w, m_conv_w_out, m_w_o, m_g_ffn2, m_ffn2_w_gate, m_ffn2_w_up, m_ffn2_w_down, m_g_final, v_meta_tokens, v_g_ffn1, v_ffn1_w_gate, v_ffn1_w_up, v_ffn1_w_down, v_g_mix, v_w_in, v_b_gate, v_ssm_a_re, v_ssm_a_im, v_ssm_log_dt, v_ssm_b_re, v_ssm_b_im, v_ssm_c_re, v_ssm_c_im, v_ssm_d, v_ssm_w_glu, v_conv_w, v_conv_w_out, v_w_o, v_g_ffn2, v_ffn2_w_gate, v_ffn2_w_up, v_ffn2_w_down, v_g_final):
    seq, d = x.shape[1], x.shape[2]
    n_meta = meta_tokens.shape[0]
    dh = d // 2
    tp = -(-(n_meta + seq) // ROW_ALIGN) * ROW_ALIGN
    mx, my, mc_ = _position()
    pos = jnp.stack([mx, my, mc_]).astype(jnp.int32)
    shard = 2 * mx + my

    big_names = ["ffn1_w_gate", "ffn1_w_up", "ffn1_w_down", "w_in", "ssm_w_glu", "conv_w_out", "w_o",
                 "ffn2_w_gate", "ffn2_w_up", "ffn2_w_down"]
    transposed = {0, 1, 7, 8}
    drop = lambda arrs: [jnp.swapaxes(a.reshape(a.shape[1:]), 0, 1) if i in transposed else a.reshape(a.shape[1:])
                         for i, a in enumerate(arrs)]
    big_w = drop([ffn1_w_gate, ffn1_w_up, ffn1_w_down, w_in, ssm_w_glu, conv_w_out, w_o,
                  ffn2_w_gate, ffn2_w_up, ffn2_w_down])
    big_m = drop([m_ffn1_w_gate, m_ffn1_w_up, m_ffn1_w_down, m_w_in, m_ssm_w_glu, m_conv_w_out,
                  m_w_o, m_ffn2_w_gate, m_ffn2_w_up, m_ffn2_w_down])
    big_v = drop([v_ffn1_w_gate, v_ffn1_w_up, v_ffn1_w_down, v_w_in, v_ssm_w_glu, v_conv_w_out,
                  v_w_o, v_ffn2_w_gate, v_ffn2_w_up, v_ffn2_w_down])
    pieces = _cast_pieces(big_w[:3], pos) + _cast_pieces(big_w[3:], pos)
    conv_local = conv_w.reshape(conv_w.shape[1], conv_w.shape[3])
    n_first = 3
    pad_rows = tp - n_meta - seq
    assert pad_rows > 0
    first = _run_carry(_allgather_carry("allgather_first", pieces[:n_first], [meta_tokens, conv_local],
                                        rows=(x.reshape(seq, d), jnp.zeros((pad_rows, d), F32))))
    smalls, h0 = first[n_first:n_first + 2], first[n_first + 2]
    stack4 = lambda wl: wl.reshape((4, -1, wl.shape[-1]))
    w1g, w1u, w1d = [stack4(wl) for wl in first[:n_first]]
    natural_cols = lambda s: jnp.transpose(s, (1, 0, 2)).reshape(s.shape[1], 4 * s.shape[2])
    cw_full = natural_cols(smalls[1])
    cw_pad = jnp.pad(cw_full, ((0, SUBLANES - cw_full.shape[0]), (0, 0)))

    s5_args = (ssm_a_re[0], ssm_a_im[0], ssm_log_dt[0], ssm_b_re[0], ssm_b_im[0], ssm_c_re[0], ssm_c_im[0])
    (mb, mc, _), disc_vjp = jax.vjp(_s5_discretise, *s5_args)
    powt = _s5_powers(ssm_a_re[0], ssm_a_im[0], ssm_log_dt[0], SCAN_TILE // SCAN_SEQS)
    mb16, mc16 = mb.astype(BF16), mc.astype(BF16)

    h1, a1, b1, n1, *mid = _ffn_fwd(h0, g_ffn1, w1g, w1u, w1d, "ffn1_fwd",
                                    carry=_allgather_carry("allgather_mixer", pieces[3:7], []))
    win_all, wglu_s, wco_s, wo_s = [stack4(wl) for wl in mid]
    wglu_all = natural_cols(wglu_s)
    wco_all = natural_cols(wco_s)
    wo_all = wo_s.reshape(d, d)
    u, p, w2g, w2u = _win_fwd(h1, g_mix, win_all, carry=_allgather_carry("allgather_ffn2_in", pieces[7:9], []))
    ys5, bnd = _scan_fwd(p, mb16, mc16, powt, ssm_d)
    h2, w2d = _mix_fwd(h1, ys5, p, cw_pad, b_gate, wglu_all, wco_all, wo_all,
                       carry=_allgather_carry("allgather_ffn2_out", pieces[9:], []))
    w2g, w2u, w2d = stack4(w2g), stack4(w2u), stack4(w2d)
    dh3, a2, b2, n2, dg_final, loss_part, dy3 = _ffn_fwd(
        h2, g_ffn2, w2g, w2u, w2d, "ffn2_fwd_loss",
        final=(g_final.reshape(1, d), loss_target.reshape(seq, d), n_meta, seq))

    dh2, dw2g, dw2u, dw2d, dg_ffn2 = _ffn_bwd(dh3, dy3, h2, n2, g_ffn2, a2, b2, w2g, w2u, w2d, "ffn2_bwd")
    dys5, dpb, dwo, dwglu, dwco, dcw, dbg = _mix_bwd(dh2, ys5, p, cw_pad, b_gate, wglu_all, wco_all, wo_all)
    dug, dmb, dmc, dlam, dd = _scan_bwd(p, dys5, mb16, mc16, powt, ssm_d, bnd)
    dh1, dwin, dg_mix, dy1 = _win_bwd(dpb, dug, u, win_all, h1, g_mix, dh2)
    shapes = [w.shape for w in big_w]
    kinds = ["list", "list", "list", "list", "col", "col", "row", "list", "list", "list"]
    rest_grads = [dwin, [dwglu], [dwco], [dwo], dw2g, dw2u, dw2d]
    rs_rest = _ReduceScatter("rest", [_Grad(a, k, s) for a, k, s in
                                      zip(rest_grads, kinds[n_first:], shapes[n_first:])], pos)
    grad_x, dw1g, dw1u, dw1d, dg_ffn1, grad_meta = _ffn_bwd(
        dh1, dy1, h0, n1, g_ffn1, a1, b1, w1g, w1u, w1d, "ffn1_bwd", chain=rs_rest, unpad=(n_meta, seq))
    s5_grads = disc_vjp((dmb, dmc, jnp.sum(dlam, axis=1)))
    local_small = [dg_ffn1, dg_mix, dbg, *s5_grads, jnp.sum(dd, axis=0), dg_ffn2, dg_final,
                   grad_meta, dcw[:conv_w.shape[1]]]
    rs_first = _ReduceScatter("first", [_Grad(a, k, s) for a, k, s in
                                        zip([dw1g, dw1u, dw1d], kinds[:n_first], shapes[:n_first])], pos,
                              extra=_pack(local_small))
    rs_first.run()
    wmv = list(zip(big_w, big_m, big_v))
    big_out = rs_first.adamw(wmv[:n_first]) + rs_rest.adamw(wmv[n_first:])
    def lead(i, o):
        o = jnp.swapaxes(o, 0, 1) if i in transposed else o
        return o.reshape((1,) + o.shape)

    big_out = {nme: tuple(lead(i, o) for o in outs) for i, (nme, outs) in enumerate(zip(big_names, big_out))}

    grad_x = grad_x.reshape(1, seq, d)

    small_names = ["g_ffn1", "g_mix", "b_gate", "ssm_a_re", "ssm_a_im", "ssm_log_dt", "ssm_b_re", "ssm_b_im",
                   "ssm_c_re", "ssm_c_im", "ssm_d", "g_ffn2", "g_final", "meta_tokens", "conv_w"]
    small_w = [g_ffn1, g_mix, b_gate, ssm_a_re, ssm_a_im, ssm_log_dt, ssm_b_re, ssm_b_im, ssm_c_re, ssm_c_im,
               ssm_d, g_ffn2, g_final, meta_tokens, conv_w]
    small_m = [m_g_ffn1, m_g_mix, m_b_gate, m_ssm_a_re, m_ssm_a_im, m_ssm_log_dt, m_ssm_b_re, m_ssm_b_im,
               m_ssm_c_re, m_ssm_c_im, m_ssm_d, m_g_ffn2, m_g_final, m_meta_tokens, m_conv_w]
    small_v = [v_g_ffn1, v_g_mix, v_b_gate, v_ssm_a_re, v_ssm_a_im, v_ssm_log_dt, v_ssm_b_re, v_ssm_b_im,
               v_ssm_c_re, v_ssm_c_im, v_ssm_d, v_g_ffn2, v_g_final, v_meta_tokens, v_conv_w]
    reduced = _unpack(rs_first.extra, local_small)
    reduced[-2] = lax.dynamic_slice_in_dim(reduced[-2], shard * meta_tokens.shape[1], meta_tokens.shape[1], 1)
    reduced[-1] = lax.dynamic_slice_in_dim(reduced[-1], shard * conv_w.shape[3], conv_w.shape[3], 1)
    small_g = [r.reshape(w.shape) for r, w in zip(reduced, small_w)]
    two_d = lambda arrs: [a.reshape(1, -1) if a.ndim == 1 else a for a in arrs]
    ds_, nm_, nv_ = _adamw_small(two_d(small_w), two_d(small_g), two_d(small_m), two_d(small_v))
    like = lambda outs: [o.reshape(w.shape) for o, w in zip(outs, small_w)]
    small_out = {nme: o for nme, o in zip(small_names, zip(small_g, like(ds_), like(nm_), like(nv_)))}

    loss = lax.psum(loss_part[0, 0], ("x", "y", "c"))
    order = ["meta_tokens", "g_ffn1", "ffn1_w_gate", "ffn1_w_up", "ffn1_w_down", "g_mix", "w_in", "b_gate",
             "ssm_a_re", "ssm_a_im", "ssm_log_dt", "ssm_b_re", "ssm_b_im", "ssm_c_re", "ssm_c_im", "ssm_d",
             "ssm_w_glu", "conv_w", "conv_w_out", "w_o", "g_ffn2", "ffn2_w_gate", "ffn2_w_up", "ffn2_w_down",
             "g_final"]
    res = {**big_out, **small_out}
    return (loss, grad_x, *[res[nme][0] for nme in order], *[res[nme][1] for nme in order],
            *[res[nme][2] for nme in order], *[res[nme][3] for nme in order])
```

```python
import functools
import math

import jax
import jax.numpy as jnp
from jax import lax
from jax.experimental import pallas as pl
from jax.experimental.pallas import tpu as pltpu

F32 = jnp.float32
BF16 = jnp.bfloat16
MESH = pl.DeviceIdType.MESH

RMS_EPS = 1e-6
ADAM_LR = 0.001
ADAM_B1 = 0.9
ADAM_B2 = 0.999
ADAM_EPS = 1e-08
ADAM_WD = 0.01
ADAM_STEP = 10

LANES = 128
SUBLANES = 8
VMEM_LIMIT = 56 * 1024 * 1024

ROW_ALIGN = 256
SCAN_TILE = 256
SCAN_SEQS = 16
GROUPS_PER_BLOCK = 8


def _params(sem, vmem=VMEM_LIMIT):
    return pltpu.CompilerParams(dimension_semantics=sem, vmem_limit_bytes=vmem)


def _pick_tile(n, candidates):
    for c in candidates:
        if n % c == 0:
            return c
    raise ValueError(f"no tile for {n}")


def _dot(a, b):
    return jnp.dot(a, b, preferred_element_type=F32)


def _dot_nt(a, b):
    return lax.dot_general(a, b, (((1,), (1,)), ((), ())), preferred_element_type=F32)


def _dot_tn(a, b):
    return lax.dot_general(a, b, (((0,), (0,)), ((), ())), preferred_element_type=F32)


def _sigmoid(x):
    return pl.reciprocal(1.0 + jnp.exp(-x), approx=True)


def _rms_stats(h):
    r = lax.rsqrt(jnp.mean(h * h, axis=-1, keepdims=True) + RMS_EPS)
    return h * r, r


def _rms_bwd(xhat, r, g, dn):
    dxh = dn * g
    return r * (dxh - xhat * jnp.mean(dxh * xhat, axis=-1, keepdims=True))


GELU_K = math.sqrt(2.0 / math.pi)
GELU_C = 0.044715


def _gelu(x):
    return 0.5 * x * (1.0 + jnp.tanh(GELU_K * (x + GELU_C * x * x * x)))


def _gelu_grad(x):
    t = jnp.tanh(GELU_K * (x + GELU_C * x * x * x))
    return 0.5 * (1.0 + t) + 0.5 * x * (1.0 - t * t) * GELU_K * (1.0 + 3.0 * GELU_C * x * x)


def _for_tile_rows(i, ni, tm, n_meta, seq, fn):
    pl.when(i == 0)(lambda: fn(0, min(tm - n_meta, seq), n_meta))
    if ni > 1:
        last_lo = (ni - 1) * tm - n_meta
        pl.when(i == ni - 1)(lambda: fn(last_lo, min(seq - last_lo, tm), 0))
    if ni > 2:
        pl.when((i > 0) & (i < ni - 1))(lambda: fn(pl.multiple_of(i * tm - n_meta, SUBLANES), tm, 0))


def _ffn_fwd(h, g, wg, wu, wd, name, final=None, carry=None):
    tp, d = h.shape
    ns, f4, _ = wg.shape
    tm = _pick_tile(tp, (768, 512, 256))
    ni = tp // tm

    def body(*refs):
        refs, phases = split(refs)
        if final is None:
            h_ref, g_ref, wg_ref, wu_ref, wd_ref, ho_ref, a_ref, b_ref, n_scr, acc = refs
        else:
            (h_ref, g_ref, wg_ref, wu_ref, wd_ref, gf_ref, tg_hbm,
             ho_ref, a_ref, b_ref, n_scr, dgf_ref, loss_ref, dy_ref, acc, tg_ref, tg_sem) = refs
        i = pl.program_id(0)
        k = pl.program_id(1)
        _run_phases(phases, carry, i * ns + k, ni * ns)

        if final is not None:
            def target_rows(lo, n, at):
                return pltpu.make_async_copy(tg_hbm.at[pl.ds(lo, n), :], tg_ref.at[pl.ds(at, n), :], tg_sem)

            def fetch_target(lo, n, at):
                if at > 0:
                    tg_ref[pl.ds(0, at), :] = jnp.zeros((at, d), F32)
                if at + n < tm:
                    tg_ref[pl.ds(at + n, tm - at - n), :] = jnp.zeros((tm - at - n, d), F32)
                target_rows(lo, n, at).start()

            pl.when(k == 0)(lambda: _for_tile_rows(i, ni, tm, final[2], final[3], fetch_target))

        @pl.when(k == 0)
        def _():
            xhat, _ = _rms_stats(h_ref[...])
            n_scr[...] = (xhat * g_ref[...]).astype(BF16)
            acc[...] = jnp.zeros_like(acc)

        n = n_scr[...]
        a = _dot_nt(n, wg_ref[...])
        b = _dot_nt(n, wu_ref[...])
        a_ref[...] = a.astype(BF16)
        b_ref[...] = b.astype(BF16)
        s = (a * _sigmoid(a) * b).astype(BF16)
        acc[...] += _dot(s, wd_ref[...])

        if final is None:
            @pl.when(k == ns - 1)
            def _():
                ho_ref[...] = h_ref[...] + 0.5 * acc[...]
        else:
            n_meta, seq = final[2], final[3]

            @pl.when((i == 0) & (k == 0))
            def _():
                dgf_ref[...] = jnp.zeros_like(dgf_ref)
                loss_ref[...] = jnp.zeros_like(loss_ref)

            @pl.when(k == ns - 1)
            def _():
                _for_tile_rows(i, ni, tm, n_meta, seq, lambda lo, n, at: target_rows(lo, n, at).wait())
                h3 = h_ref[...] + 0.5 * acc[...]
                xhat, r = _rms_stats(h3)
                gf = gf_ref[...]
                row = i * tm + lax.broadcasted_iota(jnp.int32, (tm, d), 0)
                valid = (row >= n_meta) & (row < n_meta + seq)
                diff = jnp.where(valid, xhat * gf - tg_ref[...], 0.0)
                dout = diff * (1.0 / d)
                loss_ref[...] += jnp.full(loss_ref.shape, 0.5 * jnp.sum(diff * diff) * (1.0 / d), F32)
                dgf_ref[...] += jnp.sum(dout * xhat, axis=0, keepdims=True)
                dh3 = _rms_bwd(xhat, r, gf, dout)
                ho_ref[...] = dh3
                dy_ref[...] = (0.5 * dh3).astype(BF16)

    row_spec = pl.BlockSpec((tm, d), lambda i, k: (i, 0))
    vec_spec = pl.BlockSpec((1, d), lambda i, k: (0, 0))
    in_specs = [row_spec, vec_spec,
                pl.BlockSpec((None, f4, d), lambda i, k: (k, 0, 0)),
                pl.BlockSpec((None, f4, d), lambda i, k: (k, 0, 0)),
                pl.BlockSpec((None, f4, d), lambda i, k: (k, 0, 0))]
    act_spec = pl.BlockSpec((None, tm, f4), lambda i, k: (k, i, 0))
    out_specs = [row_spec, act_spec, act_spec, row_spec]
    out_shape = [jax.ShapeDtypeStruct((tp, d), F32),
                 jax.ShapeDtypeStruct((ns, tp, f4), BF16),
                 jax.ShapeDtypeStruct((ns, tp, f4), BF16),
                 jax.ShapeDtypeStruct((tp, d), BF16)]
    args = [h, g, wg, wu, wd]
    scratch = [pltpu.VMEM((tm, d), F32)]
    if final is not None:
        in_specs += [vec_spec, ANY]
        args += [final[0], final[1]]
        out_specs += [vec_spec, pl.BlockSpec((1, LANES), lambda i, k: (0, 0)), row_spec]
        out_shape += [jax.ShapeDtypeStruct((1, d), F32), jax.ShapeDtypeStruct((1, LANES), F32),
                      jax.ShapeDtypeStruct((tp, d), BF16)]
        scratch += [pltpu.VMEM((tm, d), F32), pltpu.SemaphoreType.DMA(())]
    split = _attach_carry(carry, in_specs, args, out_specs, out_shape, scratch)
    return pl.pallas_call(
        body, name=name, grid=(ni, ns), in_specs=in_specs, out_specs=out_specs, out_shape=out_shape,
        scratch_shapes=scratch, compiler_params=_params(("arbitrary", "arbitrary")),
        input_output_aliases=split.aliases,
    )(*args)


def _ffn_bwd_shard(k, ns, dn_prev, dy, n, a, b, wg, wu, wd, tail, name, carry=None, unpad=None):
    tp, d = n.shape
    f4 = wg.shape[1]
    tm = _pick_tile(tp, (768, 512, 256))
    ni = tp // tm
    first, last = k == 0, k == ns - 1
    unpad = unpad if last else None

    def body(*refs):
        refs, phases = split(refs)
        acc_in = None if first else refs.pop(0)
        if last:
            dh_ref, h_ref, g_ref = refs[:3]
        else:
            dy_ref, n_ref = refs[:2]
        refs = refs[3 if last else 2:]
        a_ref, b_ref, wg_hbm, wu_hbm, wd_hbm = refs[:5]
        refs = refs[5:]
        acc_out, dwg_hbm, dwu_hbm, dwd_hbm = refs[:4]
        rest = refs[4:]
        dg_ref = rest.pop(0) if last else None
        head_ref = rest.pop(0) if unpad else None
        wg_ref, wu_ref, wd_ref, dwg_ref, dwu_ref, dwd_ref, wsem = rest[:7]
        i = pl.program_id(0)
        _run_phases(phases, carry, i, ni)
        if unpad:
            res_ref, res_sem = rest[7:]

            def real_rows(lo, cnt, at):
                return pltpu.make_async_copy(res_ref.at[pl.ds(at, cnt), :], acc_out.at[pl.ds(lo, cnt), :], res_sem)

            def wait_tile(tile):
                _for_tile_rows(tile, ni, tm, *unpad, lambda lo, cnt, at: real_rows(lo, cnt, at).wait())

        @pl.when(i == 0)
        def _():
            loads = [pltpu.make_async_copy(src.at[k], dst, wsem.at[j])
                     for j, (src, dst) in enumerate(((wg_hbm, wg_ref), (wu_hbm, wu_ref), (wd_hbm, wd_ref)))]
            for cp in loads:
                cp.start()
            dwg_ref[...] = jnp.zeros_like(dwg_ref)
            dwu_ref[...] = jnp.zeros_like(dwu_ref)
            dwd_ref[...] = jnp.zeros_like(dwd_ref)
            if last:
                dg_ref[...] = jnp.zeros_like(dg_ref)
            for cp in loads:
                cp.wait()

        if last:
            xhat, r = _rms_stats(h_ref[...])
            n = (xhat * g_ref[...]).astype(BF16)
            dy = (0.5 * dh_ref[...]).astype(BF16)
        else:
            n = n_ref[...]
            dy = dy_ref[...]
        av = a_ref[...].astype(F32)
        bv = b_ref[...].astype(F32)
        sg = _sigmoid(av)
        silu = av * sg
        ds = _dot_nt(dy, wd_ref[...])
        da = (ds * bv * (sg * (1.0 + av * (1.0 - sg)))).astype(BF16)
        db = (ds * silu).astype(BF16)
        s = (silu * bv).astype(BF16)
        dwd_ref[...] += _dot_tn(s, dy)
        dwg_ref[...] += _dot_tn(da, n)
        dwu_ref[...] += _dot_tn(db, n)
        dn = _dot(da, wg_ref[...]) + _dot(db, wu_ref[...])
        if not first:
            dn = dn + acc_in[...]
        if last:
            dg_ref[...] += jnp.sum(dn * xhat, axis=0, keepdims=True)
            dh_in = dh_ref[...] + _rms_bwd(xhat, r, g_ref[...], dn)
            if unpad:
                pl.when(i > 0)(lambda: wait_tile(i - 1))
                res_ref[...] = dh_in

                @pl.when(i == 0)
                def _():
                    head_ref[...] = res_ref[pl.ds(0, unpad[0]), :]

                _for_tile_rows(i, ni, tm, *unpad, lambda lo, cnt, at: real_rows(lo, cnt, at).start())
                pl.when(i == ni - 1)(lambda: wait_tile(i))
            else:
                acc_out[...] = dh_in
        else:
            acc_out[...] = dn

        @pl.when(i == ni - 1)
        def _():
            stores = []
            for j, (acc_ref, stage_ref, out_hbm) in enumerate(((dwg_ref, wg_ref, dwg_hbm), (dwu_ref, wu_ref, dwu_hbm),
                                                              (dwd_ref, wd_ref, dwd_hbm))):
                stage_ref[...] = acc_ref[...].astype(BF16)
                stores.append(pltpu.make_async_copy(stage_ref, out_hbm, wsem.at[j]))
                stores[-1].start()
            for cp in stores:
                cp.wait()

    row_spec = pl.BlockSpec((tm, d), lambda i: (i, 0))
    vec_spec = pl.BlockSpec((1, d), lambda i: (0, 0))
    act_spec = pl.BlockSpec((None, tm, f4), lambda i: (k, i, 0))
    in_specs = [act_spec, act_spec, ANY, ANY, ANY]
    args = [a, b, wg, wu, wd]
    if last:
        in_specs = [row_spec, row_spec, vec_spec] + in_specs
        args = list(tail) + args
    else:
        in_specs = [row_spec, row_spec] + in_specs
        args = [dy, n] + args
    if not first:
        in_specs.insert(0, row_spec)
        args.insert(0, dn_prev)
    out_specs = [row_spec, ANY, ANY, ANY]
    out_shape = [jax.ShapeDtypeStruct((tp, d), F32)] + [jax.ShapeDtypeStruct((f4, d), BF16)] * 3
    scratch = [pltpu.VMEM((f4, d), BF16)] * 3 + [pltpu.VMEM((f4, d), F32)] * 3 + [pltpu.SemaphoreType.DMA((3,))]
    if last:
        out_specs.append(vec_spec)
        out_shape.append(jax.ShapeDtypeStruct((1, d), F32))
    if unpad:
        out_specs[0] = ANY
        out_shape[0] = jax.ShapeDtypeStruct((unpad[1], d), F32)
        out_specs.append(pl.BlockSpec((unpad[0], d), lambda i: (0, 0)))
        out_shape.append(jax.ShapeDtypeStruct((unpad[0], d), F32))
        scratch += [pltpu.VMEM((tm, d), F32), pltpu.SemaphoreType.DMA(())]
    n_host = len(out_shape)
    split = _attach_carry(carry, in_specs, args, out_specs, out_shape, scratch)
    outs = pl.pallas_call(
        body, name=f"{name}_{k}", grid=(ni,), in_specs=in_specs, out_specs=out_specs, out_shape=out_shape,
        scratch_shapes=scratch, compiler_params=_params(("arbitrary",)), input_output_aliases=split.aliases,
    )(*args)
    return outs[:n_host], outs[n_host:]


def _ffn_bwd(dh_out, dy, h_in, n, g, a, b, wg, wu, wd, name, chain=None, unpad=None):
    ns = wg.shape[0]
    acc, dwg, dwu, dwd = None, [], [], []
    for k in range(ns):
        carry = chain.carry() if chain is not None else None
        outs, carried = _ffn_bwd_shard(k, ns, acc, dy, n, a, b, wg, wu, wd, (dh_out, h_in, g), name, carry, unpad)
        if chain is not None:
            chain.feed(carried)
        acc = outs[0]
        dwg.append(outs[1])
        dwu.append(outs[2])
        dwd.append(outs[3])
    return (acc, dwg, dwu, dwd) + tuple(outs[4:])


def _win_fwd(h, g, w_in, carry=None):
    tp, d = h.shape
    ns = w_in.shape[0]
    tm = _pick_tile(tp, (768, 512, 256))
    ni = tp // tm

    def body(*refs):
        (h_ref, g_ref, w_ref, u_ref, p_ref), phases = split(refs)
        _run_phases(phases, carry, pl.program_id(0), ni)
        xhat, _ = _rms_stats(h_ref[...])
        u = (xhat * g_ref[...]).astype(BF16)
        u_ref[...] = u
        for k in range(ns):
            p_ref[k] = _dot(u, w_ref[k]).astype(BF16)

    in_specs = [pl.BlockSpec((tm, d), lambda i: (i, 0)),
                pl.BlockSpec((1, d), lambda i: (0, 0)),
                pl.BlockSpec((ns, d, d), lambda i: (0, 0, 0))]
    out_specs = [pl.BlockSpec((tm, d), lambda i: (i, 0)),
                 pl.BlockSpec((ns, tm, d), lambda i: (0, i, 0))]
    out_shape = [jax.ShapeDtypeStruct((tp, d), BF16), jax.ShapeDtypeStruct((ns, tp, d), BF16)]
    args, scratch = [h, g, w_in], []
    split = _attach_carry(carry, in_specs, args, out_specs, out_shape, scratch)
    return pl.pallas_call(
        body, name="win_fwd", grid=(ni,), in_specs=in_specs, out_specs=out_specs, out_shape=out_shape,
        scratch_shapes=scratch, compiler_params=_params(("arbitrary",)), input_output_aliases=split.aliases,
    )(*args)


def _win_bwd_shard(k, ns, du_prev, dpb, dug, u, w_in, h1, g, dh2):
    tp, d = h1.shape
    dh = d // 2
    tm = _pick_tile(tp, (768, 512, 256))
    first, last = k == 0, k == ns - 1

    def body(*refs):
        refs = list(refs)
        acc_in = None if first else refs.pop(0)
        dug_ref = refs.pop(0) if first else None
        dp_ref, u_ref, w_ref = refs[:3]
        refs = refs[3:]
        if last:
            h_ref, g_ref, dh2_ref, acc_out, dw_ref, dg_ref, dy_ref, dw_acc = refs
        else:
            acc_out, dw_ref, dw_acc = refs
        i = pl.program_id(0)

        @pl.when(i == 0)
        def _():
            dw_acc[...] = jnp.zeros_like(dw_acc)
            if last:
                dg_ref[...] = jnp.zeros_like(dg_ref)

        dp = dp_ref[...]
        if first:
            dp = jnp.concatenate([dug_ref[...], dp[:, dh:]], axis=1)
        dw_acc[...] += _dot_tn(u_ref[...], dp)
        du = _dot_nt(dp, w_ref[...])
        if not first:
            du = du + acc_in[...]
        if last:
            xhat, r = _rms_stats(h_ref[...])
            dg_ref[...] += jnp.sum(du * xhat, axis=0, keepdims=True)
            dh1 = dh2_ref[...] + _rms_bwd(xhat, r, g_ref[...], du)
            acc_out[...] = dh1
            dy_ref[...] = (0.5 * dh1).astype(BF16)
        else:
            acc_out[...] = du

        @pl.when(i == tp // tm - 1)
        def _():
            dw_ref[...] = dw_acc[...].astype(BF16)

    row_spec = pl.BlockSpec((tm, d), lambda i: (i, 0))
    vec_spec = pl.BlockSpec((1, d), lambda i: (0, 0))
    in_specs = [pl.BlockSpec((None, tm, d), lambda i: (k, i, 0)), row_spec,
                pl.BlockSpec((None, d, d), lambda i: (k, 0, 0))]
    args = [dpb, u, w_in]
    if first:
        in_specs.insert(0, pl.BlockSpec((tm, dh), lambda i: (i, 0)))
        args.insert(0, dug)
    else:
        in_specs.insert(0, row_spec)
        args.insert(0, du_prev)
    out_specs = [row_spec, pl.BlockSpec((d, d), lambda i: (0, 0))]
    out_shape = [jax.ShapeDtypeStruct((tp, d), F32), jax.ShapeDtypeStruct((d, d), BF16)]
    if last:
        in_specs += [row_spec, vec_spec, row_spec]
        args += [h1, g, dh2]
        out_specs += [vec_spec, row_spec]
        out_shape += [jax.ShapeDtypeStruct((1, d), F32), jax.ShapeDtypeStruct((tp, d), BF16)]
    return pl.pallas_call(
        body, name=f"win_bwd_{k}", grid=(tp // tm,), in_specs=in_specs, out_specs=out_specs,
        out_shape=out_shape, scratch_shapes=[pltpu.VMEM((d, d), F32)],
        compiler_params=_params(("arbitrary",)),
    )(*args)


def _win_bwd(dpb, dug, u, w_in, h1, g, dh2):
    ns = w_in.shape[0]
    acc, dws = None, []
    for k in range(ns):
        outs = _win_bwd_shard(k, ns, acc, dpb, dug, u, w_in, h1, g, dh2)
        acc = outs[0]
        dws.append(outs[1])
    return acc, dws, outs[2], outs[3]


def _cmul(ar, ai, br, bi):
    return ar * br - ai * bi, ar * bi + ai * br


def _scan_rows(j, sub):
    return pl.ds(j * SCAN_SEQS, SCAN_SEQS)


def _permute_rows(src_ref, dst_ref, sub):
    for j in range(sub):
        dst_ref[pl.ds(j * SCAN_SEQS, SCAN_SEQS), :] = src_ref[pl.ds(j, SCAN_SEQS, stride=sub), :]


def _unpermute_rows(src_ref, dst_ref, sub):
    for j in range(sub):
        dst_ref[pl.ds(j, SCAN_SEQS, stride=sub), :] = src_ref[pl.ds(j * SCAN_SEQS, SCAN_SEQS), :]


def _local_scan(x_ref, lr, li, w, sub, reverse):
    hr = jnp.zeros((SCAN_SEQS, w), F32)
    hi = jnp.zeros((SCAN_SEQS, w), F32)
    order = range(sub - 1, -1, -1) if reverse else range(sub)
    for j in order:
        xr = x_ref[_scan_rows(j, sub), pl.ds(0, w)]
        xi = x_ref[_scan_rows(j, sub), pl.ds(w, w)]
        if reverse:
            hr, hi = lr * hr + li * hi + xr, lr * hi - li * hr + xi
        else:
            hr, hi = lr * hr - li * hi + xr, lr * hi + li * hr + xi
        x_ref[_scan_rows(j, sub), pl.ds(0, w)] = hr
        x_ref[_scan_rows(j, sub), pl.ds(w, w)] = hi
    return hr, hi


def _entering_states(er, ei, fr, fi, pow_ref, w, sub, reverse):
    lane = lax.broadcasted_iota(jnp.int32, (SCAN_SEQS, w), 0)
    if reverse:
        edge, shift1 = SCAN_SEQS - 1, SCAN_SEQS - 1
    else:
        edge, shift1 = 0, 1
    zr = jnp.where(lane == edge, pltpu.roll(fr, shift1, 0), pltpu.roll(er, shift1, 0))
    zi = jnp.where(lane == edge, pltpu.roll(fi, shift1, 0), pltpu.roll(ei, shift1, 0))
    for m in range(SCAN_SEQS.bit_length() - 1):
        step, row = 1 << m, sub - 1 + m
        ar = pow_ref[pl.ds(row, 1), pl.ds(0, w)]
        ai = pow_ref[pl.ds(row, 1), pl.ds(w, w)]
        if reverse:
            ai = -ai
            keep = lane < SCAN_SEQS - step
            sr = jnp.where(keep, pltpu.roll(zr, SCAN_SEQS - step, 0), 0.0)
            si = jnp.where(keep, pltpu.roll(zi, SCAN_SEQS - step, 0), 0.0)
        else:
            keep = lane >= step
            sr = jnp.where(keep, pltpu.roll(zr, step, 0), 0.0)
            si = jnp.where(keep, pltpu.roll(zi, step, 0), 0.0)
        pr, pi = _cmul(ar, ai, sr, si)
        zr, zi = zr + pr, zi + pi
    ar = pow_ref[pl.ds(sub - 1, 1), pl.ds(0, w)]
    ai = pow_ref[pl.ds(sub - 1, 1), pl.ds(w, w)]
    if reverse:
        ai = -ai
    pr, pi = _cmul(ar, ai, zr, zi)
    return zr, zi, er + pr, ei + pi


def _scan_fwd(p, mb, mc, powt, dskip):
    _, tp, d = p.shape
    nb, cb, w2 = mb.shape
    w = w2 // 2
    q = SCAN_TILE
    sub = q // SCAN_SEQS
    nt = tp // q
    ds = d // 2

    def body(ug_ref, mb_ref, mc_ref, pow_ref, d_ref, y_ref, bnd_ref, x_scr, carry, nat, perm):
        t = pl.program_id(1)

        @pl.when(t == 0)
        def _():
            carry[...] = jnp.zeros_like(carry)

        ugf = ug_ref[...].astype(F32)
        nat[...] = ugf
        _permute_rows(nat, perm, sub)
        x_scr[...] = _dot(perm[...].astype(BF16), mb_ref[...])
        lr = jnp.broadcast_to(pow_ref[pl.ds(0, 1), pl.ds(0, w)], (SCAN_SEQS, w))
        li = jnp.broadcast_to(pow_ref[pl.ds(0, 1), pl.ds(w, w)], (SCAN_SEQS, w))
        er, ei = _local_scan(x_scr, lr, li, w, sub, False)
        zr, zi, fr, fi = _entering_states(er, ei, carry[:, pl.ds(0, w)], carry[:, pl.ds(w, w)],
                                          pow_ref, w, sub, False)
        carry[:, pl.ds(0, w)] = fr
        carry[:, pl.ds(w, w)] = fi
        bnd_ref[:, pl.ds(0, w)] = fr
        bnd_ref[:, pl.ds(w, w)] = fi
        for j in range(sub):
            pr = pow_ref[pl.ds(j, 1), pl.ds(0, w)]
            pi = pow_ref[pl.ds(j, 1), pl.ds(w, w)]
            cr, ci = _cmul(pr, pi, zr, zi)
            x_scr[_scan_rows(j, sub), pl.ds(0, w)] += cr
            x_scr[_scan_rows(j, sub), pl.ds(w, w)] += ci
        hb = x_scr[...].astype(BF16)
        perm[...] = _dot_nt(hb, mc_ref[...])
        _unpermute_rows(perm, nat, sub)
        y_ref[...] = nat[...] + d_ref[...] * ugf

    in_specs = [pl.BlockSpec((None, q, cb), lambda b, t: (0, t, b)),
                pl.BlockSpec((None, cb, w2), lambda b, t: (b, 0, 0)),
                pl.BlockSpec((None, cb, w2), lambda b, t: (b, 0, 0)),
                pl.BlockSpec((None, powt.shape[1], w2), lambda b, t: (b, 0, 0)),
                pl.BlockSpec((1, cb), lambda b, t: (0, b))]
    out_specs = [pl.BlockSpec((q, cb), lambda b, t: (t, b)),
                 pl.BlockSpec((None, None, SCAN_SEQS, w2), lambda b, t: (b, t, 0, 0))]
    out_shape = [jax.ShapeDtypeStruct((tp, ds), F32), jax.ShapeDtypeStruct((nb, nt, SCAN_SEQS, w2), F32)]
    scratch = [pltpu.VMEM((q, w2), F32), pltpu.VMEM((SCAN_SEQS, w2), F32),
               pltpu.VMEM((q, cb), F32), pltpu.VMEM((q, cb), F32)]
    return pl.pallas_call(
        body, name="s5_scan_fwd", grid=(nb, nt), in_specs=in_specs, out_specs=out_specs,
        out_shape=out_shape, scratch_shapes=scratch, compiler_params=_params(("arbitrary", "arbitrary")),
    )(p, mb, mc, powt, dskip)


def _scan_bwd(p, dy, mb, mc, powt, dskip, bnd):
    _, tp, d = p.shape
    nb, cb, w2 = mb.shape
    w = w2 // 2
    q = SCAN_TILE
    sub = q // SCAN_SEQS
    nt = tp // q
    ds = d // 2

    def body(ug_ref, dy_ref, mb_ref, mc_ref, pow_ref, d_ref, bnd_ref,
             dug_ref, dmb_ref, dmc_ref, dlam_ref, dd_ref, x_scr, y_scr, gcarry, nat, perm):
        t = pl.program_id(1)
        tt = nt - 1 - t

        @pl.when(t == 0)
        def _():
            gcarry[...] = jnp.zeros_like(gcarry)
            dmb_ref[...] = jnp.zeros_like(dmb_ref)
            dmc_ref[...] = jnp.zeros_like(dmc_ref)
            dlam_ref[...] = jnp.zeros_like(dlam_ref)
            dd_ref[...] = jnp.zeros_like(dd_ref)

        ugf = ug_ref[...].astype(F32)
        dyf = dy_ref[...].astype(F32)
        dd_ref[...] += jnp.sum((dyf * ugf).reshape(q // SUBLANES, SUBLANES, cb), axis=0)
        nat[...] = ugf
        _permute_rows(nat, perm, sub)
        ug = perm[...].astype(BF16)
        nat[...] = dyf
        _permute_rows(nat, perm, sub)
        dyb = perm[...].astype(BF16)
        lr = jnp.broadcast_to(pow_ref[pl.ds(0, 1), pl.ds(0, w)], (SCAN_SEQS, w))
        li = jnp.broadcast_to(pow_ref[pl.ds(0, 1), pl.ds(w, w)], (SCAN_SEQS, w))

        x_scr[...] = _dot(ug, mb_ref[...])
        er, ei = _local_scan(x_scr, lr, li, w, sub, False)
        first = tt == 0
        pfr = jnp.where(first, 0.0, bnd_ref[:, pl.ds(0, w)])
        pfi = jnp.where(first, 0.0, bnd_ref[:, pl.ds(w, w)])
        hzr, hzi, _, _ = _entering_states(er, ei, pfr, pfi, pow_ref, w, sub, False)
        for j in range(sub):
            pr = pow_ref[pl.ds(j, 1), pl.ds(0, w)]
            pi = pow_ref[pl.ds(j, 1), pl.ds(w, w)]
            cr, ci = _cmul(pr, pi, hzr, hzi)
            x_scr[_scan_rows(j, sub), pl.ds(0, w)] += cr
            x_scr[_scan_rows(j, sub), pl.ds(w, w)] += ci

        y_scr[...] = _dot(dyb, mc_ref[...])
        er, ei = _local_scan(y_scr, lr, li, w, sub, True)
        gzr, gzi, fr, fi = _entering_states(er, ei, gcarry[:, pl.ds(0, w)], gcarry[:, pl.ds(w, w)],
                                            pow_ref, w, sub, True)
        gcarry[:, pl.ds(0, w)] = fr
        gcarry[:, pl.ds(w, w)] = fi
        accr = jnp.zeros((SCAN_SEQS, w), F32)
        acci = jnp.zeros((SCAN_SEQS, w), F32)
        for j in range(sub):
            pr = pow_ref[pl.ds(sub - 1 - j, 1), pl.ds(0, w)]
            pi = pow_ref[pl.ds(sub - 1 - j, 1), pl.ds(w, w)]
            cr, ci = _cmul(pr, -pi, gzr, gzi)
            gr = y_scr[_scan_rows(j, sub), pl.ds(0, w)] + cr
            gi = y_scr[_scan_rows(j, sub), pl.ds(w, w)] + ci
            y_scr[_scan_rows(j, sub), pl.ds(0, w)] = gr
            y_scr[_scan_rows(j, sub), pl.ds(w, w)] = gi
            if j == 0:
                hpr, hpi = hzr, hzi
            else:
                hpr = x_scr[_scan_rows(j - 1, sub), pl.ds(0, w)]
                hpi = x_scr[_scan_rows(j - 1, sub), pl.ds(w, w)]
            accr += hpr * gr + hpi * gi
            acci += hpr * gi - hpi * gr
        dlam_ref[:, pl.ds(0, w)] += accr
        dlam_ref[:, pl.ds(w, w)] += acci

        hb = x_scr[...].astype(BF16)
        gb = y_scr[...].astype(BF16)
        dmc_ref[...] += _dot_tn(dyb, hb)
        dmb_ref[...] += _dot_tn(ug, gb)
        perm[...] = _dot_nt(gb, mb_ref[...])
        _unpermute_rows(perm, nat, sub)
        dug_ref[...] = (nat[...] + d_ref[...] * dyf).astype(BF16)

    blk = lambda b, t: (b, 0, 0)
    return pl.pallas_call(
        body, name="s5_scan_bwd", grid=(nb, nt),
        in_specs=[pl.BlockSpec((None, q, cb), lambda b, t: (0, nt - 1 - t, b)),
                  pl.BlockSpec((q, cb), lambda b, t: (nt - 1 - t, b)),
                  pl.BlockSpec((None, cb, w2), blk),
                  pl.BlockSpec((None, cb, w2), blk),
                  pl.BlockSpec((None, powt.shape[1], w2), blk),
                  pl.BlockSpec((1, cb), lambda b, t: (0, b)),
                  pl.BlockSpec((None, None, SCAN_SEQS, w2),
                               lambda b, t: (b, jnp.maximum(nt - 2 - t, 0), 0, 0))],
        out_specs=[pl.BlockSpec((q, cb), lambda b, t: (nt - 1 - t, b)),
                   pl.BlockSpec((None, cb, w2), blk),
                   pl.BlockSpec((None, cb, w2), blk),
                   pl.BlockSpec((None, SCAN_SEQS, w2), blk),
                   pl.BlockSpec((SUBLANES, cb), lambda b, t: (0, b))],
        out_shape=[jax.ShapeDtypeStruct((tp, ds), BF16),
                   jax.ShapeDtypeStruct((nb, cb, w2), F32),
                   jax.ShapeDtypeStruct((nb, cb, w2), F32),
                   jax.ShapeDtypeStruct((nb, SCAN_SEQS, w2), F32),
                   jax.ShapeDtypeStruct((SUBLANES, ds), F32)],
        scratch_shapes=[pltpu.VMEM((q, w2), F32), pltpu.VMEM((q, w2), F32),
                        pltpu.VMEM((SCAN_SEQS, w2), F32), pltpu.VMEM((q, cb), F32), pltpu.VMEM((q, cb), F32)],
        compiler_params=_params(("arbitrary", "arbitrary")),
    )(p, dy, mb, mc, powt, dskip, bnd)


HALO = 16


def _mix_tile(ys5, p0, p1, p2, p3, prev_cin, cw, bgate, wglu, wco, d):
    dh = d // 2
    tm = ys5.shape[0]
    v = p0[:, dh:].astype(F32)
    gbr = p1[:, :dh].astype(F32)
    gcr = p1[:, dh:].astype(F32)
    gact = _gelu(ys5).astype(BF16)
    z = _dot(gact, wglu)
    z1, z2 = z[:, :d], z[:, d:]
    sg = _sigmoid(z2)
    y_ssm = z1 * sg
    cin = gcr * v
    ext = jnp.concatenate([cin, prev_cin], axis=0)
    r1 = pltpu.roll(ext, 1, 0)[:tm]
    r2 = pltpu.roll(ext, 2, 0)[:tm]
    cv = cw[2] * cin + cw[1] * r1 + cw[0] * r2
    cg = (gbr * cv).astype(BF16)
    y_conv = _dot(cg, wco)
    g_s = _sigmoid(p2.astype(F32) + bgate[:, :d])
    g_c = _sigmoid(p3.astype(F32) + bgate[:, d:])
    mixed = g_s * y_ssm + g_c * y_conv
    return dict(v=v, gb=gbr, gc=gcr, gact=gact, z1=z1, sg=sg, y_ssm=y_ssm, cin=cin, r1=r1, r2=r2,
                cv=cv, cg=cg, y_conv=y_conv, g_s=g_s, g_c=g_c, mixed=mixed)


def _mix_fwd(h1, ys5, p, cw, bgate, wglu, wco, wo, carry=None):
    tp, d = h1.shape
    dh = d // 2
    tm = ROW_ALIGN
    ni = tp // tm

    def body(*refs):
        refs, phases = split(refs)
        (h_ref, y_ref, p0_ref, p1_ref, p2_ref, p3_ref, cw_ref, bg_ref, wglu_ref, wco_ref, wo_ref,
         o_ref, prev) = refs
        _run_phases(phases, carry, pl.program_id(0), ni)

        @pl.when(pl.program_id(0) == 0)
        def _():
            prev[...] = jnp.zeros_like(prev)

        cw = [cw_ref[pl.ds(t, 1), :] for t in range(3)]
        f = _mix_tile(y_ref[...], p0_ref[...], p1_ref[...], p2_ref[...], p3_ref[...], prev[...],
                      cw, bg_ref[...], wglu_ref[...], wco_ref[...], d)
        prev[...] = f["cin"][tm - HALO:, :]
        o_ref[...] = h_ref[...] + _dot(f["mixed"].astype(BF16), wo_ref[...])

    row = pl.BlockSpec((tm, d), lambda i: (i, 0))
    full = lambda a: pl.BlockSpec(a.shape, lambda i: (0,) * a.ndim)
    pk = lambda k: pl.BlockSpec((None, tm, d), lambda i, k=k: (k, i, 0))
    in_specs = [row, pl.BlockSpec((tm, dh), lambda i: (i, 0)), pk(0), pk(1), pk(2), pk(3),
                full(cw), full(bgate), full(wglu), full(wco), full(wo)]
    out_specs, out_shape = [row], [jax.ShapeDtypeStruct((tp, d), F32)]
    args, scratch = [h1, ys5, p, p, p, p, cw, bgate, wglu, wco, wo], [pltpu.VMEM((HALO, dh), F32)]
    split = _attach_carry(carry, in_specs, args, out_specs, out_shape, scratch)
    return pl.pallas_call(
        body, name="mix_fwd", grid=(ni,), in_specs=in_specs, out_specs=out_specs, out_shape=out_shape,
        scratch_shapes=scratch, compiler_params=_params(("arbitrary",)), input_output_aliases=split.aliases,
    )(*args)


def _mix_bwd(dh2, ys5, p, cw, bgate, wglu, wco, wo):
    tp, d = dh2.shape
    dh = d // 2
    tm = ROW_ALIGN
    ni = tp // tm
    hb = tm // HALO

    def body(dh_ref, y_ref, p0_ref, p1_ref, p2_ref, p3_ref, h0_ref, h1_ref,
             cw_ref, bg_ref, wglu_ref, wco_ref, wo_ref,
             dys_ref, dpb_ref, dwo_ref, dwglu_ref, dwco_ref, dcw_ref, dbg_ref, nxt):
        i = pl.program_id(0)
        tt = ni - 1 - i

        @pl.when(i == 0)
        def _():
            nxt[...] = jnp.zeros_like(nxt)
            dwo_ref[...] = jnp.zeros_like(dwo_ref)
            dwglu_ref[...] = jnp.zeros_like(dwglu_ref)
            dwco_ref[...] = jnp.zeros_like(dwco_ref)
            dcw_ref[...] = jnp.zeros_like(dcw_ref)
            dbg_ref[...] = jnp.zeros_like(dbg_ref)

        cw = [cw_ref[pl.ds(t, 1), :] for t in range(3)]
        prev_cin = h1_ref[:, dh:].astype(F32) * h0_ref[:, dh:].astype(F32)
        prev_cin = jnp.where(tt == 0, 0.0, prev_cin)
        ys5 = y_ref[...]
        f = _mix_tile(ys5, p0_ref[...], p1_ref[...], p2_ref[...], p3_ref[...], prev_cin,
                      cw, bg_ref[...], wglu_ref[...], wco_ref[...], d)
        dhb = dh_ref[...].astype(BF16)
        dmixed = _dot_nt(dhb, wo_ref[...])
        dwo_ref[...] += _dot_tn(f["mixed"].astype(BF16), dhb)

        g_s, g_c, sg = f["g_s"], f["g_c"], f["sg"]
        dy_ssm = dmixed * g_s
        dy_conv = dmixed * g_c
        dp2 = dmixed * f["y_ssm"] * g_s * (1.0 - g_s)
        dp3 = dmixed * f["y_conv"] * g_c * (1.0 - g_c)
        dbg_ref[:, pl.ds(0, d)] += jnp.sum(dp2, axis=0, keepdims=True)
        dbg_ref[:, pl.ds(d, d)] += jnp.sum(dp3, axis=0, keepdims=True)

        dz = jnp.concatenate([dy_ssm * sg, dy_ssm * f["z1"] * sg * (1.0 - sg)], axis=1).astype(BF16)
        dwglu_ref[...] += _dot_tn(f["gact"], dz)
        dys_ref[...] = (_dot_nt(dz, wglu_ref[...]) * _gelu_grad(ys5)).astype(BF16)

        dycb = dy_conv.astype(BF16)
        dwco_ref[...] += _dot_tn(f["cg"], dycb)
        dcg = _dot_nt(dycb, wco_ref[...])
        dgb = dcg * f["cv"]
        dcv = dcg * f["gb"]
        ext = jnp.concatenate([dcv, nxt[...]], axis=0)
        n1 = pltpu.roll(ext, tm + HALO - 1, 0)[:tm]
        n2 = pltpu.roll(ext, tm + HALO - 2, 0)[:tm]
        nxt[...] = dcv[:HALO, :]
        dcin = cw[2] * dcv + cw[1] * n1 + cw[0] * n2
        dcw_ref[pl.ds(0, 1), :] += jnp.sum(dcv * f["r2"], axis=0, keepdims=True)
        dcw_ref[pl.ds(1, 1), :] += jnp.sum(dcv * f["r1"], axis=0, keepdims=True)
        dcw_ref[pl.ds(2, 1), :] += jnp.sum(dcv * f["cin"], axis=0, keepdims=True)
        dgc = dcin * f["v"]
        dv = dcin * f["gc"]
        dpb_ref[0] = jnp.concatenate([jnp.zeros_like(dv), dv], axis=1).astype(BF16)
        dpb_ref[1] = jnp.concatenate([dgb, dgc], axis=1).astype(BF16)
        dpb_ref[2] = dp2.astype(BF16)
        dpb_ref[3] = dp3.astype(BF16)

    rev = lambda i: ni - 1 - i
    row = pl.BlockSpec((tm, d), lambda i: (rev(i), 0))
    half = pl.BlockSpec((tm, dh), lambda i: (rev(i), 0))
    full = lambda a: pl.BlockSpec(a.shape, lambda i: (0,) * a.ndim)
    pk = lambda k: pl.BlockSpec((None, tm, d), lambda i, k=k: (k, rev(i), 0))
    halo = lambda k: pl.BlockSpec((None, HALO, d), lambda i, k=k: (k, jnp.maximum(rev(i) * hb - 1, 0), 0))
    acc = lambda shape: pl.BlockSpec(shape, lambda i: (0,) * len(shape))
    return pl.pallas_call(
        body, name="mix_bwd", grid=(ni,),
        in_specs=[row, half, pk(0), pk(1), pk(2), pk(3), halo(0), halo(1),
                  full(cw), full(bgate), full(wglu), full(wco), full(wo)],
        out_specs=[half, pl.BlockSpec((4, tm, d), lambda i: (0, rev(i), 0)),
                   acc((d, d)), acc((dh, 2 * d)), acc((dh, d)), acc((SUBLANES, dh)), acc((1, 2 * d))],
        out_shape=[jax.ShapeDtypeStruct((tp, dh), BF16), jax.ShapeDtypeStruct((4, tp, d), BF16),
                   jax.ShapeDtypeStruct((d, d), F32), jax.ShapeDtypeStruct((dh, 2 * d), F32),
                   jax.ShapeDtypeStruct((dh, d), F32), jax.ShapeDtypeStruct((SUBLANES, dh), F32),
                   jax.ShapeDtypeStruct((1, 2 * d), F32)],
        scratch_shapes=[pltpu.VMEM((HALO, dh), F32)],
        compiler_params=_params(("arbitrary",)),
    )(dh2, ys5, p, p, p, p, p, p, cw, bgate, wglu, wco, wo)


ANY = pl.BlockSpec(memory_space=pl.ANY)


def _position():
    return lax.axis_index("x"), lax.axis_index("y"), lax.axis_index("c")


def _remote(src, dst, ssem, rsem, dev):
    return pltpu.make_async_remote_copy(src_ref=src, dst_ref=dst, send_sem=ssem, recv_sem=rsem,
                                        device_id=dev, device_id_type=MESH)


def _cast_pieces(ws, pos):
    n = len(ws)

    def body(pos_ref, *refs):
        for w_ref, o_ref in zip(refs[:n], refs[n:]):
            r4 = o_ref.shape[1]
            o_ref[0] = w_ref[pl.ds(0, r4), :].astype(BF16)
            o_ref[1] = w_ref[pl.ds(r4, r4), :].astype(BF16)

    halves = [(w.shape[0] // 2, w.shape[1]) for w in ws]
    return pl.pallas_call(
        body, name="cast_pieces",
        grid_spec=pltpu.PrefetchScalarGridSpec(
            num_scalar_prefetch=1, grid=(1,),
            in_specs=[pl.BlockSpec(hs, lambda i, pos: (pos[2], 0)) for hs in halves],
            out_specs=[pl.BlockSpec((None, None, None, 2, r2 // 2, cols),
                                    lambda i, pos: (pos[0], pos[1], pos[2], 0, 0, 0)) for r2, cols in halves]),
        out_shape=[jax.ShapeDtypeStruct((2, 2, 2, 2, r2 // 2, cols), BF16) for r2, cols in halves],
        compiler_params=_params(("arbitrary",)),
    )(pos, *ws)


class _Carry:
    def __init__(self, name, arrays, out_shapes, nsem, nlsem, make, fracs, n_inplace=0):
        self.name, self.arrays, self.out_shapes = name, list(arrays), list(out_shapes)
        self.nsem, self.nlsem, self.make, self.fracs = nsem, max(nlsem, 1), make, fracs
        self.n_inplace = n_inplace


def _carry_scratch(carry):
    return [pltpu.SemaphoreType.DMA((carry.nsem,)), pltpu.SemaphoreType.DMA((carry.nsem,)),
            pltpu.SemaphoreType.DMA((carry.nlsem,))]


def _run_carry(carry):
    na, no = len(carry.arrays), len(carry.out_shapes)

    def body(*refs):
        for phase in carry.make(refs[:na], refs[na:na + no], *refs[na + no:]):
            phase()

    return pl.pallas_call(
        body, name=carry.name, in_specs=[ANY] * na, out_specs=[ANY] * no, out_shape=carry.out_shapes,
        scratch_shapes=_carry_scratch(carry), input_output_aliases={i: i for i in range(carry.n_inplace)},
    )(*carry.arrays)


def _attach_carry(carry, in_specs, args, out_specs, out_shape, scratch):
    nhi, nho, nhs = len(in_specs), len(out_specs), len(scratch)
    if carry is None:
        none = lambda refs: (list(refs), [])
        none.aliases = {}
        return none
    na, no = len(carry.arrays), len(carry.out_shapes)
    in_specs += [ANY] * na
    args += carry.arrays
    out_specs += [ANY] * no
    out_shape += carry.out_shapes
    scratch += _carry_scratch(carry)

    def split(refs):
        refs = list(refs)
        o = nhi + na
        host = refs[:nhi] + refs[o:o + nho] + refs[o + nho + no:o + nho + no + nhs]
        sems = refs[o + nho + no + nhs:]
        return host, carry.make(refs[nhi:o], refs[o + nho:o + nho + no], *sems)

    split.aliases = {nhi + i: nho + i for i in range(carry.n_inplace)}
    return split


def _run_phases(phases, carry, step, total):
    for phase, frac in zip(phases, carry.fracs if carry is not None else ()):
        pl.when(step == int(round(frac * (total - 1))))(phase)


def _allgather_carry(name, walls, smalls):
    n, ns = len(walls), len(smalls)
    per = 14
    n_big = per * n

    def make(ins, outs, ssem, rsem, lsem):
        sin = ins[n:]
        wall, sall = outs[:n], outs[n:]
        x, y, c = _position()
        xnb, ynb, sib = (1 - x, y, c), (x, 1 - y, c), (x, y, 1 - c)
        chips = [(1 - x, y), (x, 1 - y), (1 - x, 1 - y)]
        slot = lambda i, xx, yy, cc, h: wall[i].at[xx, yy, cc, h]
        own = lambda i, h: slot(i, x, y, c, h)
        cp = lambda src, dst, s, dev: _remote(src, dst, ssem.at[s], rsem.at[s], dev)
        to_sib = lambda i, xx, yy, h: cp(slot(i, xx, yy, c, h), slot(i, xx, yy, c, h),
                                         per * i + 6 + 4 * xx + 2 * yy + h, sib)

        def local():
            return [pltpu.make_async_copy(sin[i], sall[i].at[2 * x + y], lsem.at[i]) for i in range(ns)]

        def small(px, py, j, i, landing):
            s = n_big + j * ns + i
            return cp(sin[i], sall[i].at[landing], s, (px, py, c))

        def first_hop():
            for lc in local():
                lc.start()
            for j, (px, py) in enumerate(chips):
                for i in range(ns):
                    small(px, py, j, i, 2 * x + y).start()
            for i in range(n):
                cp(own(i, 0), slot(i, x, y, c, 0), per * i, xnb).start()
                cp(own(i, 1), slot(i, x, y, c, 1), per * i + 1, ynb).start()
                for h in range(2):
                    cp(own(i, h), slot(i, x, y, c, h), per * i + 6 + 4 * x + 2 * y + h, sib).start()

        def second_hop():
            for lc in local():
                lc.wait()
            for i in range(n):
                cp(slot(i, 1 - x, y, c, 0), slot(i, 1 - x, y, c, 0), per * i, xnb).wait_recv()
                cp(slot(i, x, 1 - y, c, 1), slot(i, x, 1 - y, c, 1), per * i + 1, ynb).wait_recv()
                for j in range(2):
                    cp(slot(i, j, y, c, 0), slot(i, j, y, c, 0), per * i + 2 + j, ynb).start()
                    cp(slot(i, x, j, c, 1), slot(i, x, j, c, 1), per * i + 4 + j, xnb).start()
                to_sib(i, 1 - x, y, 0).start()
                to_sib(i, x, 1 - y, 1).start()

        def last_to_sibling():
            for i in range(n):
                for j in range(2):
                    cp(slot(i, j, 1 - y, c, 0), slot(i, j, 1 - y, c, 0), per * i + 2 + j, ynb).wait_recv()
                    cp(slot(i, 1 - x, j, c, 1), slot(i, 1 - x, j, c, 1), per * i + 4 + j, xnb).wait_recv()
                    to_sib(i, j, 1 - y, 0).start()
                    to_sib(i, 1 - x, j, 1).start()

        def finish():
            for i in range(n):
                for xx in range(2):
                    for yy in range(2):
                        for h in range(2):
                            s = per * i + 6 + 4 * xx + 2 * yy + h
                            cp(slot(i, xx, yy, 1 - c, h), slot(i, xx, yy, 1 - c, h), s, sib).wait_recv()
                            to_sib(i, xx, yy, h).wait_send()
                cp(own(i, 0), slot(i, x, y, c, 0), per * i, xnb).wait_send()
                cp(own(i, 1), slot(i, x, y, c, 1), per * i + 1, ynb).wait_send()
                for j in range(2):
                    cp(slot(i, j, y, c, 0), slot(i, j, y, c, 0), per * i + 2 + j, ynb).wait_send()
                    cp(slot(i, x, j, c, 1), slot(i, x, j, c, 1), per * i + 4 + j, xnb).wait_send()
            for j, (px, py) in enumerate(chips):
                for i in range(ns):
                    small(px, py, j, i, 2 * px + py).wait_recv()
                    small(px, py, j, i, 2 * x + y).wait_send()

        return [first_hop, second_hop, last_to_sibling, finish]

    out_shapes = [jax.ShapeDtypeStruct(a.shape, a.dtype) for a in walls]
    out_shapes += [jax.ShapeDtypeStruct((4,) + a.shape, a.dtype) for a in smalls]
    return _Carry(name, list(walls) + list(smalls), out_shapes, n_big + 3 * ns, ns, make, (0.0, 0.23, 0.73, 1.0),
                  n_inplace=n)


def _exchange_carry(name, arrays, out_shapes, plan):
    count = plan([None] * len(arrays), [None] * len(out_shapes), None)

    def make(ins, outs, ssem, rsem, lsem):
        def copies():
            return [_remote(src, dst, ssem.at[j], rsem.at[j], peer)
                    for j, (src, dst, peer) in enumerate(plan(ins, outs, _position()))]

        def start():
            for c in copies():
                c.start()

        def wait():
            for c in copies():
                c.wait()

        return [start, wait]

    return _Carry(name, arrays, out_shapes, count, 0, make, (0.0, 1.0))


class _Grad:
    def __init__(self, arrs, kind, shard_shape):
        self.arrs, self.kind = list(arrs), kind
        self.rows, self.cols = shard_shape
        self.r2 = self.rows // 2

    def view(self, refs, k, h):
        r2 = self.r2
        if self.kind == "list":
            return refs[k].at[pl.ds(h * r2, r2), :]
        if self.kind == "stacked":
            return refs[0].at[k, pl.ds(h * r2, r2), :]
        if self.kind == "col":
            return refs[0].at[pl.ds(h * r2, r2), pl.ds(k * self.cols, self.cols)]
        return refs[0].at[pl.ds((2 * k + h) * r2, r2), :]

    def add_half(self, recv, pos):
        r2, cols = self.r2, self.cols
        n_in = len(self.arrs)
        tr = _row_tile(r2, cols)
        nt = r2 // tr

        def body(pos_ref, *refs):
            m_refs, (r_ref, of_ref, ob_ref) = refs[:n_in], refs[n_in:]
            mine = m_refs[0][...]
            for kk in range(1, n_in):
                mine = jnp.where(pl.program_id(1) == kk, m_refs[kk][...], mine)
            s = mine.astype(F32) + r_ref[...].astype(F32)
            of_ref[...] = s
            ob_ref[...] = s.astype(BF16)

        row = lambda t, pos: pos[2] * nt + t
        if self.kind == "list":
            specs = [pl.BlockSpec((tr, cols), lambda t, k, pos: (row(t, pos), 0))] * n_in
        elif self.kind == "stacked":
            specs = [pl.BlockSpec((None, tr, cols), lambda t, k, pos: (k, row(t, pos), 0))]
        elif self.kind == "col":
            specs = [pl.BlockSpec((tr, cols), lambda t, k, pos: (row(t, pos), k))]
        else:
            specs = [pl.BlockSpec((tr, cols), lambda t, k, pos: (2 * k * nt + row(t, pos), 0))]
        blk = pl.BlockSpec((None, tr, cols), lambda t, k, pos: (k, t, 0))
        return pl.pallas_call(
            body, name="rs_add_c",
            grid_spec=pltpu.PrefetchScalarGridSpec(
                num_scalar_prefetch=1, grid=(nt, 4), in_specs=specs + [blk], out_specs=[blk, blk]),
            out_shape=[jax.ShapeDtypeStruct((4, r2, cols), F32), jax.ShapeDtypeStruct((4, r2, cols), BF16)],
            compiler_params=_params(("arbitrary", "arbitrary")),
        )(pos, *self.arrs, recv)


def _row_tile(rows, cols):
    fits = [t for t in range(16, rows + 1, 16) if rows % t == 0 and t * cols * 4 <= 2 * 1024 * 1024]
    return max(fits) if fits else rows


def _adamw_math(w, g, m, v):
    m = ADAM_B1 * m + (1.0 - ADAM_B1) * g
    v = ADAM_B2 * v + (1.0 - ADAM_B2) * (g * g)
    m_hat = m / (1.0 - ADAM_B1 ** ADAM_STEP)
    v_hat = v / (1.0 - ADAM_B2 ** ADAM_STEP)
    delta = -ADAM_LR * (m_hat / (jnp.sqrt(v_hat) + ADAM_EPS) + ADAM_WD * w)
    return delta, m, v


def _adamw_big(w, m, v, own, sib, pos):
    rows, cols = w.shape
    r2 = rows // 2

    tr = _row_tile(r2, cols)
    nt = r2 // tr

    def body(pos_ref, w_ref, m_ref, v_ref, own_ref, sib_ref, g_ref, d_ref, nm_ref, nv_ref):
        h = pl.program_id(0)
        g = jnp.where(h == pos_ref[2], own_ref[...], sib_ref[...])
        g_ref[...] = g
        d_ref[...], nm_ref[...], nv_ref[...] = _adamw_math(w_ref[...], g, m_ref[...], v_ref[...])

    half = pl.BlockSpec((tr, cols), lambda h, t, pos: (h * nt + t, 0))
    piece = pl.BlockSpec((tr, cols), lambda h, t, pos: (t, 0))
    out = jax.ShapeDtypeStruct((rows, cols), F32)
    return pl.pallas_call(
        body, name="adamw",
        grid_spec=pltpu.PrefetchScalarGridSpec(
            num_scalar_prefetch=1, grid=(2, nt),
            in_specs=[half, half, half, piece, piece],
            out_specs=[half, half, half, half]),
        out_shape=[out, out, out, out],
        compiler_params=_params(("arbitrary", "arbitrary")),
    )(pos, w, m, v, own, sib)


def _add_hop1(s1fs, recvs, pos):
    n = len(s1fs)
    s1vs = [s.reshape((4, 2) + r.shape[2:]) for s, r in zip(s1fs, recvs)]

    def body(pos_ref, *refs):
        for m_ref, r_ref, of_ref, ob_ref in zip(refs[:n], refs[n:2 * n], refs[2 * n:3 * n], refs[3 * n:]):
            s = m_ref[...] + r_ref[...].astype(F32)
            of_ref[...] = s
            ob_ref[...] = s.astype(BF16)

    def mine(h, j, pos):
        return (jnp.where(h == 0, 2 * j + pos[1], 2 * pos[0] + j), h, 0, 0)

    tile = lambda r: (None, None) + r.shape[2:]
    blks = [pl.BlockSpec(tile(r), lambda h, j, pos: (h, j, 0, 0)) for r in recvs]
    outs = pl.pallas_call(
        body, name="rs_add_1",
        grid_spec=pltpu.PrefetchScalarGridSpec(
            num_scalar_prefetch=1, grid=(2, 2),
            in_specs=[pl.BlockSpec(tile(r), mine) for r in recvs] + blks, out_specs=blks + blks),
        out_shape=[jax.ShapeDtypeStruct(r.shape, F32) for r in recvs]
        + [jax.ShapeDtypeStruct(r.shape, BF16) for r in recvs],
        compiler_params=_params(("arbitrary", "arbitrary")),
    )(pos, *s1vs, *recvs)
    return list(zip(outs[:n], outs[n:]))


def _own_sum(s2fs, recvs, pos):
    n = len(s2fs)

    def body(pos_ref, *refs):
        for s_ref, r_ref, o_ref in zip(refs[:n], refs[n:2 * n], refs[2 * n:]):
            o_ref[...] = s_ref[...] + r_ref[...].astype(F32)

    blks = [pl.BlockSpec((None,) + r.shape[1:], lambda h, pos: (h, 0, 0)) for r in recvs]
    return pl.pallas_call(
        body, name="own_sum",
        grid_spec=pltpu.PrefetchScalarGridSpec(
            num_scalar_prefetch=1, grid=(2,),
            in_specs=[pl.BlockSpec((None, None) + r.shape[1:],
                                   lambda h, pos: (h, jnp.where(h == 0, pos[0], pos[1]), 0, 0)) for r in recvs]
            + blks, out_specs=blks),
        out_shape=[jax.ShapeDtypeStruct(r.shape, F32) for r in recvs],
        compiler_params=_params(("arbitrary",)),
    )(pos, *s2fs, *recvs)


def _add_small(a, b):
    def body(a_ref, b_ref, o_ref):
        o_ref[...] = a_ref[...] + b_ref[...]

    vm = pl.BlockSpec(memory_space=pltpu.VMEM)
    return pl.pallas_call(body, name="add_small", in_specs=[vm, vm], out_specs=vm,
                          out_shape=jax.ShapeDtypeStruct(a.shape, F32))(a, b)


def _adamw_small(ws, gs, ms, vs):
    n = len(ws)

    def body(*refs):
        for i in range(n):
            w_ref, g_ref, m_ref, v_ref, d_ref, nm_ref, nv_ref = (refs[j * n + i] for j in range(7))
            d_ref[...], nm_ref[...], nv_ref[...] = _adamw_math(w_ref[...], g_ref[...], m_ref[...], v_ref[...])

    vm = pl.BlockSpec(memory_space=pltpu.VMEM)
    outs = pl.pallas_call(body, name="adamw_small", in_specs=[vm] * (4 * n), out_specs=[vm] * (3 * n),
                          out_shape=[jax.ShapeDtypeStruct(w.shape, F32) for w in ws] * 3)(*ws, *gs, *ms, *vs)
    return outs[:n], outs[n:2 * n], outs[2 * n:]


class _ReduceScatter:
    def __init__(self, tag, grads, pos, extra=None):
        self.tag, self.grads, self.pos, self.stage, self.extra = tag, grads, pos, 0, extra

    def carry(self):
        grads, n = self.grads, len(self.grads)
        r4 = [g.r2 // 2 for g in grads]

        first = [sum(len(g.arrs) for g in grads[:i]) for i in range(n)]

        def plan_c(ins, outs, p):
            if p is None:
                return 4 * n
            x, y, c = p
            mine = lambda i: ins[first[i]:first[i] + len(grads[i].arrs)]
            return [(grads[i].view(mine(i), k, 1 - c), outs[i].at[k], (x, y, 1 - c))
                    for i in range(n) for k in range(4)]

        def plan_1(ins, outs, p):
            if p is None:
                return 4 * n
            x, y, c = p
            copies = []
            for i in range(n):
                for j in range(2):
                    copies.append((ins[i].at[2 * j + (1 - y), pl.ds(0, r4[i]), :], outs[i].at[0, j],
                                   (x, 1 - y, c)))
                    copies.append((ins[i].at[2 * (1 - x) + j, pl.ds(r4[i], r4[i]), :], outs[i].at[1, j],
                                   (1 - x, y, c)))
            return copies

        def plan_2(ins, outs, p):
            if p is None:
                return 2 * n
            x, y, c = p
            copies = []
            for i in range(n):
                copies.append((ins[i].at[0, 1 - x], outs[i].at[0], (1 - x, y, c)))
                copies.append((ins[i].at[1, 1 - y], outs[i].at[1], (x, 1 - y, c)))
            return copies

        def plan_s(ins, outs, p):
            if p is None:
                return n
            x, y, c = p
            return [(ins[i], outs[i], (x, y, 1 - c)) for i in range(n)]

        shape = lambda lead, dt: [jax.ShapeDtypeStruct(lead(g) + (g.cols,), dt) for g in grads]
        stage = self.stage
        if stage == 0:
            name, arrays, plan = "exchange_c", [a for g in grads for a in g.arrs], plan_c
            shapes = [jax.ShapeDtypeStruct((4, g.r2, g.cols), g.arrs[0].dtype) for g in grads]
        elif stage == 1:
            name, arrays, plan = "exchange_1", [s[1] for s in self.s1], plan_1
            shapes = shape(lambda g: (2, 2, g.r2 // 2), BF16)
        elif stage == 2:
            name, arrays, plan = "exchange_2", [s[1] for s in self.s2], plan_2
            shapes = shape(lambda g: (2, g.r2 // 2), BF16)
        else:
            name, arrays, plan, shapes = "exchange_sibling", self.own, plan_s, shape(lambda g: (g.r2,), F32)
        if self.extra is not None and stage < 3:
            def with_extra(ins, outs, p, plan=plan):
                if p is None:
                    return plan(ins[:-1], outs[:-1], None) + 1
                x, y, c = p
                peer = [(x, y, 1 - c), (x, 1 - y, c), (1 - x, y, c)][stage]
                return plan(ins[:-1], outs[:-1], p) + [(ins[-1], outs[-1], peer)]

            arrays = arrays + [self.extra]
            shapes = shapes + [jax.ShapeDtypeStruct(self.extra.shape, F32)]
            plan = with_extra
        return _exchange_carry(f"rs_{self.tag}_{name}", arrays, shapes, plan)

    def feed(self, recv):
        grads, pos = self.grads, self.pos
        recv = list(recv)
        if self.extra is not None and self.stage < 3:
            self.extra = _add_small(self.extra, recv.pop())
        if self.stage == 0:
            self.s1 = [g.add_half(r, pos) for g, r in zip(grads, recv)]
        elif self.stage == 1:
            self.s2 = _add_hop1([s[0] for s in self.s1], list(recv), pos)
        elif self.stage == 2:
            own = _own_sum([s[0] for s in self.s2], list(recv), pos)
            self.own = [o.reshape(g.r2, g.cols) for g, o in zip(grads, own)]
        else:
            self.sib = list(recv)
        self.stage += 1

    def run(self):
        while self.stage < 4:
            self.feed(_run_carry(self.carry()))

    def adamw(self, weights):
        return [_adamw_big(w, m, v, o, sb, self.pos) for (w, m, v), o, sb in zip(weights, self.own, self.sib)]


def _block_diag(t, nb):
    g, c, p = t.shape
    gb = g // nb
    t = t.reshape(nb, gb, c, p)
    eye = jnp.eye(gb, dtype=t.dtype)
    return jnp.einsum("bgcp,gh->bgchp", t, eye).reshape(nb, gb * c, gb * p)


def _s5_discretise(a_re, a_im, log_dt, b_re, b_im, c_re, c_im):
    g, p = a_re.shape
    nb = g // GROUPS_PER_BLOCK
    dt = jnp.exp(log_dt)[:, None]
    mag = jnp.exp(a_re * dt)
    lam_re = mag * jnp.cos(a_im * dt)
    lam_im = mag * jnp.sin(a_im * dt)
    den = a_re * a_re + a_im * a_im
    q_re = ((lam_re - 1.0) * a_re + lam_im * a_im) / den
    q_im = (lam_im * a_re - (lam_re - 1.0) * a_im) / den
    bb_re = q_re[..., None] * b_re - q_im[..., None] * b_im
    bb_im = q_re[..., None] * b_im + q_im[..., None] * b_re
    tr = lambda t: jnp.swapaxes(t, 1, 2)
    mb = jnp.concatenate([_block_diag(tr(bb_re), nb), _block_diag(tr(bb_im), nb)], axis=-1)
    mc = jnp.concatenate([_block_diag(c_re, nb), -_block_diag(c_im, nb)], axis=-1)
    lam = jnp.concatenate([lam_re.reshape(nb, -1), lam_im.reshape(nb, -1)], axis=-1)
    return mb, mc, lam


def _s5_powers(a_re, a_im, log_dt, sub):
    g, p = a_re.shape
    nb = g // GROUPS_PER_BLOCK
    dt = jnp.exp(log_dt)[:, None]
    ns = list(range(1, sub + 1)) + [sub << m for m in range(1, SCAN_SEQS.bit_length() - 1)]
    ns += [0] * (-len(ns) % SUBLANES)
    e = jnp.asarray(ns, F32)[:, None, None]
    mag = jnp.exp(a_re[None] * dt[None] * e)
    ang = a_im[None] * dt[None] * e
    re = (mag * jnp.cos(ang)).reshape(len(ns), nb, -1)
    im = (mag * jnp.sin(ang)).reshape(len(ns), nb, -1)
    return jnp.transpose(jnp.concatenate([re, im], axis=-1), (1, 0, 2))


def _pack(parts):
    flat = jnp.concatenate([a.reshape(-1).astype(F32) for a in parts])
    n = flat.shape[0]
    pad = -n % (SUBLANES * LANES)
    return jnp.pad(flat, (0, pad)).reshape(-1, LANES)


def _unpack(buf, like):
    flat = buf.reshape(-1)
    out, o = [], 0
    for a in like:
        out.append(flat[o:o + a.size].reshape(a.shape))
        o += a.size
    return out


def kernel(x, meta_tokens, g_ffn1, ffn1_w_gate, ffn1_w_up, ffn1_w_down, g_mix, w_in, b_gate, ssm_a_re, ssm_a_im, ssm_log_dt, ssm_b_re, ssm_b_im, ssm_c_re, ssm_c_im, ssm_d, ssm_w_glu, conv_w, conv_w_out, w_o, g_ffn2, ffn2_w_gate, ffn2_w_up, ffn2_w_down, g_final, loss_target, m_meta_tokens, m_g_ffn1, m_ffn1_w_gate, m_ffn1_w_up, m_ffn1_w_down, m_g_mix, m_w_in, m_b_gate, m_ssm_a_re, m_ssm_a_im, m_ssm_log_dt, m_ssm_b_re, m_ssm_b_im, m_ssm_c_re, m_ssm_c_im, m_ssm_d, m_ssm_w_glu, m_conv_w, m_conv_w_out, m_w_o, m_g_ffn2, m_ffn2_w_gate, m_ffn2_w_up, m_ffn2_w_down, m_g_final, v_meta_tokens, v_g_ffn1, v_ffn1_w_gate, v_ffn1_w_up, v_ffn1_w_down, v_g_mix, v_w_in, v_b_gate, v_ssm_a_re, v_ssm_a_im, v_ssm_log_dt, v_ssm_b_re, v_ssm_b_im, v_ssm_c_re, v_ssm_c_im, v_ssm_d, v_ssm_w_glu, v_conv_w, v_conv_w_out, v_w_o, v_g_ffn2, v_ffn2_w_gate, v_ffn2_w_up, v_ffn2_w_down, v_g_final):
    seq, d = x.shape[1], x.shape[2]
    n_meta = meta_tokens.shape[0]
    dh = d // 2
    tp = -(-(n_meta + seq) // ROW_ALIGN) * ROW_ALIGN
    mx, my, mc_ = _position()
    pos = jnp.stack([mx, my, mc_]).astype(jnp.int32)
    shard = 2 * mx + my

    big_names = ["ffn1_w_gate", "ffn1_w_up", "ffn1_w_down", "w_in", "ssm_w_glu", "conv_w_out", "w_o",
                 "ffn2_w_gate", "ffn2_w_up", "ffn2_w_down"]
    transposed = {0, 1, 7, 8}
    drop = lambda arrs: [jnp.swapaxes(a.reshape(a.shape[1:]), 0, 1) if i in transposed else a.reshape(a.shape[1:])
                         for i, a in enumerate(arrs)]
    big_w = drop([ffn1_w_gate, ffn1_w_up, ffn1_w_down, w_in, ssm_w_glu, conv_w_out, w_o,
                  ffn2_w_gate, ffn2_w_up, ffn2_w_down])
    big_m = drop([m_ffn1_w_gate, m_ffn1_w_up, m_ffn1_w_down, m_w_in, m_ssm_w_glu, m_conv_w_out,
                  m_w_o, m_ffn2_w_gate, m_ffn2_w_up, m_ffn2_w_down])
    big_v = drop([v_ffn1_w_gate, v_ffn1_w_up, v_ffn1_w_down, v_w_in, v_ssm_w_glu, v_conv_w_out,
                  v_w_o, v_ffn2_w_gate, v_ffn2_w_up, v_ffn2_w_down])
    pieces = _cast_pieces(big_w[:3], pos) + _cast_pieces(big_w[3:], pos)
    conv_local = conv_w.reshape(conv_w.shape[1], conv_w.shape[3])
    n_first = 3
    first = _run_carry(_allgather_carry("allgather_first", pieces[:n_first], [meta_tokens, conv_local]))
    smalls = first[n_first:]
    stack4 = lambda wl: wl.reshape((4, -1, wl.shape[-1]))
    w1g, w1u, w1d = [stack4(wl) for wl in first[:n_first]]
    natural_cols = lambda s: jnp.transpose(s, (1, 0, 2)).reshape(s.shape[1], 4 * s.shape[2])
    meta_full = natural_cols(smalls[0])
    cw_full = natural_cols(smalls[1])
    cw_pad = jnp.pad(cw_full, ((0, SUBLANES - cw_full.shape[0]), (0, 0)))

    s5_args = (ssm_a_re[0], ssm_a_im[0], ssm_log_dt[0], ssm_b_re[0], ssm_b_im[0], ssm_c_re[0], ssm_c_im[0])
    (mb, mc, _), disc_vjp = jax.vjp(_s5_discretise, *s5_args)
    powt = _s5_powers(ssm_a_re[0], ssm_a_im[0], ssm_log_dt[0], SCAN_TILE // SCAN_SEQS)
    mb16, mc16 = mb.astype(BF16), mc.astype(BF16)

    pad_rows = tp - n_meta - seq
    h0 = jnp.concatenate([meta_full, x.reshape(seq, d), jnp.zeros((pad_rows, d), F32)], axis=0)
    h1, a1, b1, n1, *mid = _ffn_fwd(h0, g_ffn1, w1g, w1u, w1d, "ffn1_fwd",
                                    carry=_allgather_carry("allgather_mixer", pieces[3:7], []))
    win_all, wglu_s, wco_s, wo_s = [stack4(wl) for wl in mid]
    wglu_all = natural_cols(wglu_s)
    wco_all = natural_cols(wco_s)
    wo_all = wo_s.reshape(d, d)
    u, p, w2g, w2u = _win_fwd(h1, g_mix, win_all, carry=_allgather_carry("allgather_ffn2_in", pieces[7:9], []))
    ys5, bnd = _scan_fwd(p, mb16, mc16, powt, ssm_d)
    h2, w2d = _mix_fwd(h1, ys5, p, cw_pad, b_gate, wglu_all, wco_all, wo_all,
                       carry=_allgather_carry("allgather_ffn2_out", pieces[9:], []))
    w2g, w2u, w2d = stack4(w2g), stack4(w2u), stack4(w2d)
    dh3, a2, b2, n2, dg_final, loss_part, dy3 = _ffn_fwd(
        h2, g_ffn2, w2g, w2u, w2d, "ffn2_fwd_loss",
        final=(g_final.reshape(1, d), loss_target.reshape(seq, d), n_meta, seq))

    dh2, dw2g, dw2u, dw2d, dg_ffn2 = _ffn_bwd(dh3, dy3, h2, n2, g_ffn2, a2, b2, w2g, w2u, w2d, "ffn2_bwd")
    dys5, dpb, dwo, dwglu, dwco, dcw, dbg = _mix_bwd(dh2, ys5, p, cw_pad, b_gate, wglu_all, wco_all, wo_all)
    dug, dmb, dmc, dlam, dd = _scan_bwd(p, dys5, mb16, mc16, powt, ssm_d, bnd)
    dh1, dwin, dg_mix, dy1 = _win_bwd(dpb, dug, u, win_all, h1, g_mix, dh2)
    shapes = [w.shape for w in big_w]
    kinds = ["list", "list", "list", "list", "col", "col", "row", "list", "list", "list"]
    rest_grads = [dwin, [dwglu], [dwco], [dwo], dw2g, dw2u, dw2d]
    rs_rest = _ReduceScatter("rest", [_Grad(a, k, s) for a, k, s in
                                      zip(rest_grads, kinds[n_first:], shapes[n_first:])], pos)
    grad_x, dw1g, dw1u, dw1d, dg_ffn1, grad_meta = _ffn_bwd(
        dh1, dy1, h0, n1, g_ffn1, a1, b1, w1g, w1u, w1d, "ffn1_bwd", chain=rs_rest, unpad=(n_meta, seq))
    s5_grads = disc_vjp((dmb, dmc, jnp.sum(dlam, axis=1)))
    local_small = [dg_ffn1, dg_mix, dbg, *s5_grads, jnp.sum(dd, axis=0), dg_ffn2, dg_final,
                   grad_meta, dcw[:conv_w.shape[1]]]
    rs_first = _ReduceScatter("first", [_Grad(a, k, s) for a, k, s in
                                        zip([dw1g, dw1u, dw1d], kinds[:n_first], shapes[:n_first])], pos,
                              extra=_pack(local_small))
    rs_first.run()
    wmv = list(zip(big_w, big_m, big_v))
    big_out = rs_first.adamw(wmv[:n_first]) + rs_rest.adamw(wmv[n_first:])
    def lead(i, o):
        o = jnp.swapaxes(o, 0, 1) if i in transposed else o
        return o.reshape((1,) + o.shape)

    big_out = {nme: tuple(lead(i, o) for o in outs) for i, (nme, outs) in enumerate(zip(big_names, big_out))}

    grad_x = grad_x.reshape(1, seq, d)

    small_names = ["g_ffn1", "g_mix", "b_gate", "ssm_a_re", "ssm_a_im", "ssm_log_dt", "ssm_b_re", "ssm_b_im",
                   "ssm_c_re", "ssm_c_im", "ssm_d", "g_ffn2", "g_final", "meta_tokens", "conv_w"]
    small_w = [g_ffn1, g_mix, b_gate, ssm_a_re, ssm_a_im, ssm_log_dt, ssm_b_re, ssm_b_im, ssm_c_re, ssm_c_im,
               ssm_d, g_ffn2, g_final, meta_tokens, conv_w]
    small_m = [m_g_ffn1, m_g_mix, m_b_gate, m_ssm_a_re, m_ssm_a_im, m_ssm_log_dt, m_ssm_b_re, m_ssm_b_im,
               m_ssm_c_re, m_ssm_c_im, m_ssm_d, m_g_ffn2, m_g_final, m_meta_tokens, m_conv_w]
    small_v = [v_g_ffn1, v_g_mix, v_b_gate, v_ssm_a_re, v_ssm_a_im, v_ssm_log_dt, v_ssm_b_re, v_ssm_b_im,
               v_ssm_c_re, v_ssm_c_im, v_ssm_d, v_g_ffn2, v_g_final, v_meta_tokens, v_conv_w]
    reduced = _unpack(rs_first.extra, local_small)
    reduced[-2] = lax.dynamic_slice_in_dim(reduced[-2], shard * meta_tokens.shape[1], meta_tokens.shape[1], 1)
    reduced[-1] = lax.dynamic_slice_in_dim(reduced[-1], shard * conv_w.shape[3], conv_w.shape[3], 1)
    small_g = [r.reshape(w.shape) for r, w in zip(reduced, small_w)]
    two_d = lambda arrs: [a.reshape(1, -1) if a.ndim == 1 else a for a in arrs]
    ds_, nm_, nv_ = _adamw_small(two_d(small_w), two_d(small_g), two_d(small_m), two_d(small_v))
    like = lambda outs: [o.reshape(w.shape) for o, w in zip(outs, small_w)]
    small_out = {nme: o for nme, o in zip(small_names, zip(small_g, like(ds_), like(nm_), like(nv_)))}

    loss = lax.psum(loss_part[0, 0], ("x", "y", "c"))
    order = ["meta_tokens", "g_ffn1", "ffn1_w_gate", "ffn1_w_up", "ffn1_w_down", "g_mix", "w_in", "b_gate",
             "ssm_a_re", "ssm_a_im", "ssm_log_dt", "ssm_b_re", "ssm_b_im", "ssm_c_re", "ssm_c_im", "ssm_d",
             "ssm_w_glu", "conv_w", "conv_w_out", "w_o", "g_ffn2", "ffn2_w_gate", "ffn2_w_up", "ffn2_w_down",
             "g_final"]
    res = {**big_out, **small_out}
    return (loss, grad_x, *[res[nme][0] for nme in order], *[res[nme][1] for nme in order],
            *[res[nme][2] for nme in order], *[res[nme][3] for nme in order])
```

```python
import functools
import math

import jax
import jax.numpy as jnp
from jax import lax
from jax.experimental import pallas as pl
from jax.experimental.pallas import tpu as pltpu

F32 = jnp.float32
BF16 = jnp.bfloat16
MESH = pl.DeviceIdType.MESH

RMS_EPS = 1e-6
ADAM_LR = 0.001
ADAM_B1 = 0.9
ADAM_B2 = 0.999
ADAM_EPS = 1e-08
ADAM_WD = 0.01
ADAM_STEP = 10

LANES = 128
SUBLANES = 8
VMEM_LIMIT = 56 * 1024 * 1024
ADD_GROUP_BYTES = VMEM_LIMIT // 3

ROW_ALIGN = 256
SCAN_TILE = 256
SCAN_SEQS = 16
GROUPS_PER_BLOCK = 8


def _params(sem, vmem=VMEM_LIMIT):
    return pltpu.CompilerParams(dimension_semantics=sem, vmem_limit_bytes=vmem)


def _pick_tile(n, candidates):
    for c in candidates:
        if n % c == 0:
            return c
    raise ValueError(f"no tile for {n}")


def _dot(a, b):
    return jnp.dot(a, b, preferred_element_type=F32)


def _dot_nt(a, b):
    return lax.dot_general(a, b, (((1,), (1,)), ((), ())), preferred_element_type=F32)


def _dot_tn(a, b):
    return lax.dot_general(a, b, (((0,), (0,)), ((), ())), preferred_element_type=F32)


def _sigmoid(x):
    return pl.reciprocal(1.0 + jnp.exp(-x), approx=True)


def _rms_stats(h):
    r = lax.rsqrt(jnp.mean(h * h, axis=-1, keepdims=True) + RMS_EPS)
    return h * r, r


def _rms_bwd(xhat, r, g, dn):
    dxh = dn * g
    return r * (dxh - xhat * jnp.mean(dxh * xhat, axis=-1, keepdims=True))


GELU_K = math.sqrt(2.0 / math.pi)
GELU_C = 0.044715


def _gelu(x):
    return 0.5 * x * (1.0 + jnp.tanh(GELU_K * (x + GELU_C * x * x * x)))


def _gelu_grad(x):
    t = jnp.tanh(GELU_K * (x + GELU_C * x * x * x))
    return 0.5 * (1.0 + t) + 0.5 * x * (1.0 - t * t) * GELU_K * (1.0 + 3.0 * GELU_C * x * x)


def _for_tile_rows(i, ni, tm, n_meta, seq, fn):
    pl.when(i == 0)(lambda: fn(0, min(tm - n_meta, seq), n_meta))
    if ni > 1:
        last_lo = (ni - 1) * tm - n_meta
        pl.when(i == ni - 1)(lambda: fn(last_lo, min(seq - last_lo, tm), 0))
    if ni > 2:
        pl.when((i > 0) & (i < ni - 1))(lambda: fn(pl.multiple_of(i * tm - n_meta, SUBLANES), tm, 0))


def _ffn_fwd(h, g, wg, wu, wd, name, final=None, carry=None):
    tp, d = h.shape
    ns, f4, _ = wg.shape
    tm = _pick_tile(tp, (768, 512, 256))
    ni = tp // tm

    def body(*refs):
        refs, phases = split(refs)
        if final is None:
            h_ref, g_ref, wg_ref, wu_ref, wd_ref, ho_ref, a_ref, b_ref, n_scr, acc = refs
        else:
            (h_ref, g_ref, wg_ref, wu_ref, wd_ref, gf_ref, tg_hbm,
             ho_ref, a_ref, b_ref, n_scr, dgf_ref, loss_ref, dy_ref, acc, tg_ref, tg_sem) = refs
        i = pl.program_id(0)
        k = pl.program_id(1)
        _run_phases(phases, carry, i * ns + k, ni * ns)

        if final is not None:
            def target_rows(lo, n, at):
                return pltpu.make_async_copy(tg_hbm.at[pl.ds(lo, n), :], tg_ref.at[pl.ds(at, n), :], tg_sem)

            def fetch_target(lo, n, at):
                if at > 0:
                    tg_ref[pl.ds(0, at), :] = jnp.zeros((at, d), F32)
                if at + n < tm:
                    tg_ref[pl.ds(at + n, tm - at - n), :] = jnp.zeros((tm - at - n, d), F32)
                target_rows(lo, n, at).start()

            pl.when(k == 0)(lambda: _for_tile_rows(i, ni, tm, final[2], final[3], fetch_target))

        @pl.when(k == 0)
        def _():
            xhat, _ = _rms_stats(h_ref[...])
            n_scr[...] = (xhat * g_ref[...]).astype(BF16)
            acc[...] = jnp.zeros_like(acc)

        n = n_scr[...]
        a = _dot_nt(n, wg_ref[...])
        b = _dot_nt(n, wu_ref[...])
        a_ref[...] = a.astype(BF16)
        b_ref[...] = b.astype(BF16)
        s = (a * _sigmoid(a) * b).astype(BF16)
        acc[...] += _dot(s, wd_ref[...])

        if final is None:
            @pl.when(k == ns - 1)
            def _():
                ho_ref[...] = h_ref[...] + 0.5 * acc[...]
        else:
            n_meta, seq = final[2], final[3]

            @pl.when((i == 0) & (k == 0))
            def _():
                dgf_ref[...] = jnp.zeros_like(dgf_ref)
                loss_ref[...] = jnp.zeros_like(loss_ref)

            @pl.when(k == ns - 1)
            def _():
                _for_tile_rows(i, ni, tm, n_meta, seq, lambda lo, n, at: target_rows(lo, n, at).wait())
                h3 = h_ref[...] + 0.5 * acc[...]
                xhat, r = _rms_stats(h3)
                gf = gf_ref[...]
                row = i * tm + lax.broadcasted_iota(jnp.int32, (tm, d), 0)
                valid = (row >= n_meta) & (row < n_meta + seq)
                diff = jnp.where(valid, xhat * gf - tg_ref[...], 0.0)
                dout = diff * (1.0 / d)
                loss_ref[...] += jnp.full(loss_ref.shape, 0.5 * jnp.sum(diff * diff) * (1.0 / d), F32)
                dgf_ref[...] += jnp.sum(dout * xhat, axis=0, keepdims=True)
                dh3 = _rms_bwd(xhat, r, gf, dout)
                ho_ref[...] = dh3
                dy_ref[...] = (0.5 * dh3).astype(BF16)

    row_spec = pl.BlockSpec((tm, d), lambda i, k: (i, 0))
    vec_spec = pl.BlockSpec((1, d), lambda i, k: (0, 0))
    in_specs = [row_spec, vec_spec,
                pl.BlockSpec((None, f4, d), lambda i, k: (k, 0, 0)),
                pl.BlockSpec((None, f4, d), lambda i, k: (k, 0, 0)),
                pl.BlockSpec((None, f4, d), lambda i, k: (k, 0, 0))]
    act_spec = pl.BlockSpec((None, tm, f4), lambda i, k: (k, i, 0))
    out_specs = [row_spec, act_spec, act_spec, row_spec]
    out_shape = [jax.ShapeDtypeStruct((tp, d), F32),
                 jax.ShapeDtypeStruct((ns, tp, f4), BF16),
                 jax.ShapeDtypeStruct((ns, tp, f4), BF16),
                 jax.ShapeDtypeStruct((tp, d), BF16)]
    args = [h, g, wg, wu, wd]
    scratch = [pltpu.VMEM((tm, d), F32)]
    if final is not None:
        in_specs += [vec_spec, ANY]
        args += [final[0], final[1]]
        out_specs += [vec_spec, pl.BlockSpec((1, LANES), lambda i, k: (0, 0)), row_spec]
        out_shape += [jax.ShapeDtypeStruct((1, d), F32), jax.ShapeDtypeStruct((1, LANES), F32),
                      jax.ShapeDtypeStruct((tp, d), BF16)]
        scratch += [pltpu.VMEM((tm, d), F32), pltpu.SemaphoreType.DMA(())]
    split = _attach_carry(carry, in_specs, args, out_specs, out_shape, scratch)
    return pl.pallas_call(
        body, name=name, grid=(ni, ns), in_specs=in_specs, out_specs=out_specs, out_shape=out_shape,
        scratch_shapes=scratch, compiler_params=_params(("arbitrary", "arbitrary")),
        input_output_aliases=split.aliases,
    )(*args)


def _ffn_bwd_shard(k, ns, dn_prev, dy, n, a, b, wg, wu, wd, tail, name, carry=None, unpad=None):
    tp, d = n.shape
    f4 = wg.shape[1]
    tm = _pick_tile(tp, (768, 512, 256))
    ni = tp // tm
    first, last = k == 0, k == ns - 1
    unpad = unpad if last else None

    def body(*refs):
        refs, phases = split(refs)
        acc_in = None if first else refs.pop(0)
        if last:
            dh_ref, h_ref, g_ref = refs[:3]
        else:
            dy_ref, n_ref = refs[:2]
        refs = refs[3 if last else 2:]
        a_ref, b_ref, wg_hbm, wu_hbm, wd_hbm = refs[:5]
        refs = refs[5:]
        acc_out, dwg_hbm, dwu_hbm, dwd_hbm = refs[:4]
        rest = refs[4:]
        dg_ref = rest.pop(0) if last else None
        head_ref = rest.pop(0) if unpad else None
        wg_ref, wu_ref, wd_ref, dwg_ref, dwu_ref, dwd_ref, wsem = rest[:7]
        i = pl.program_id(0)
        _run_phases(phases, carry, i, ni)
        if unpad:
            res_ref, res_sem = rest[7:]

            def real_rows(lo, cnt, at):
                return pltpu.make_async_copy(res_ref.at[pl.ds(at, cnt), :], acc_out.at[pl.ds(lo, cnt), :], res_sem)

            def wait_tile(tile):
                _for_tile_rows(tile, ni, tm, *unpad, lambda lo, cnt, at: real_rows(lo, cnt, at).wait())

        @pl.when(i == 0)
        def _():
            loads = [pltpu.make_async_copy(src.at[k], dst, wsem.at[j])
                     for j, (src, dst) in enumerate(((wg_hbm, wg_ref), (wu_hbm, wu_ref), (wd_hbm, wd_ref)))]
            for cp in loads:
                cp.start()
            dwg_ref[...] = jnp.zeros_like(dwg_ref)
            dwu_ref[...] = jnp.zeros_like(dwu_ref)
            dwd_ref[...] = jnp.zeros_like(dwd_ref)
            if last:
                dg_ref[...] = jnp.zeros_like(dg_ref)
            for cp in loads:
                cp.wait()

        if last:
            xhat, r = _rms_stats(h_ref[...])
            n = (xhat * g_ref[...]).astype(BF16)
            dy = (0.5 * dh_ref[...]).astype(BF16)
        else:
            n = n_ref[...]
            dy = dy_ref[...]
        av = a_ref[...].astype(F32)
        bv = b_ref[...].astype(F32)
        sg = _sigmoid(av)
        silu = av * sg
        ds = _dot_nt(dy, wd_ref[...])
        da = (ds * bv * (sg * (1.0 + av * (1.0 - sg)))).astype(BF16)
        db = (ds * silu).astype(BF16)
        s = (silu * bv).astype(BF16)
        dwd_ref[...] += _dot_tn(s, dy)
        dwg_ref[...] += _dot_tn(da, n)
        dwu_ref[...] += _dot_tn(db, n)
        dn = _dot(da, wg_ref[...]) + _dot(db, wu_ref[...])
        if not first:
            dn = dn + acc_in[...]
        if last:
            dg_ref[...] += jnp.sum(dn * xhat, axis=0, keepdims=True)
            dh_in = dh_ref[...] + _rms_bwd(xhat, r, g_ref[...], dn)
            if unpad:
                pl.when(i > 0)(lambda: wait_tile(i - 1))
                res_ref[...] = dh_in

                @pl.when(i == 0)
                def _():
                    head_ref[...] = res_ref[pl.ds(0, unpad[0]), :]

                _for_tile_rows(i, ni, tm, *unpad, lambda lo, cnt, at: real_rows(lo, cnt, at).start())
                pl.when(i == ni - 1)(lambda: wait_tile(i))
            else:
                acc_out[...] = dh_in
        else:
            acc_out[...] = dn

        @pl.when(i == ni - 1)
        def _():
            stores = []
            for j, (acc_ref, stage_ref, out_hbm) in enumerate(((dwg_ref, wg_ref, dwg_hbm), (dwu_ref, wu_ref, dwu_hbm),
                                                              (dwd_ref, wd_ref, dwd_hbm))):
                stage_ref[...] = acc_ref[...].astype(BF16)
                stores.append(pltpu.make_async_copy(stage_ref, out_hbm, wsem.at[j]))
                stores[-1].start()
            for cp in stores:
                cp.wait()

    row_spec = pl.BlockSpec((tm, d), lambda i: (i, 0))
    vec_spec = pl.BlockSpec((1, d), lambda i: (0, 0))
    act_spec = pl.BlockSpec((None, tm, f4), lambda i: (k, i, 0))
    in_specs = [act_spec, act_spec, ANY, ANY, ANY]
    args = [a, b, wg, wu, wd]
    if last:
        in_specs = [row_spec, row_spec, vec_spec] + in_specs
        args = list(tail) + args
    else:
        in_specs = [row_spec, row_spec] + in_specs
        args = [dy, n] + args
    if not first:
        in_specs.insert(0, row_spec)
        args.insert(0, dn_prev)
    out_specs = [row_spec, ANY, ANY, ANY]
    out_shape = [jax.ShapeDtypeStruct((tp, d), F32)] + [jax.ShapeDtypeStruct((f4, d), BF16)] * 3
    scratch = [pltpu.VMEM((f4, d), BF16)] * 3 + [pltpu.VMEM((f4, d), F32)] * 3 + [pltpu.SemaphoreType.DMA((3,))]
    if last:
        out_specs.append(vec_spec)
        out_shape.append(jax.ShapeDtypeStruct((1, d), F32))
    if unpad:
        out_specs[0] = ANY
        out_shape[0] = jax.ShapeDtypeStruct((unpad[1], d), F32)
        out_specs.append(pl.BlockSpec((unpad[0], d), lambda i: (0, 0)))
        out_shape.append(jax.ShapeDtypeStruct((unpad[0], d), F32))
        scratch += [pltpu.VMEM((tm, d), F32), pltpu.SemaphoreType.DMA(())]
    n_host = len(out_shape)
    split = _attach_carry(carry, in_specs, args, out_specs, out_shape, scratch)
    outs = pl.pallas_call(
        body, name=f"{name}_{k}", grid=(ni,), in_specs=in_specs, out_specs=out_specs, out_shape=out_shape,
        scratch_shapes=scratch, compiler_params=_params(("arbitrary",)), input_output_aliases=split.aliases,
    )(*args)
    return outs[:n_host], outs[n_host:]


def _ffn_bwd(dh_out, dy, h_in, n, g, a, b, wg, wu, wd, name, chain=None, unpad=None):
    ns = wg.shape[0]
    acc, dwg, dwu, dwd = None, [], [], []
    for k in range(ns):
        carry = chain.carry() if chain is not None else None
        outs, carried = _ffn_bwd_shard(k, ns, acc, dy, n, a, b, wg, wu, wd, (dh_out, h_in, g), name, carry, unpad)
        if chain is not None:
            chain.feed(carried)
        acc = outs[0]
        dwg.append(outs[1])
        dwu.append(outs[2])
        dwd.append(outs[3])
    return (acc, dwg, dwu, dwd) + tuple(outs[4:])


def _win_fwd(h, g, w_in, carry=None):
    tp, d = h.shape
    ns = w_in.shape[0]
    tm = _pick_tile(tp, (768, 512, 256))
    ni = tp // tm

    def body(*refs):
        (h_ref, g_ref, w_ref, u_ref, p_ref), phases = split(refs)
        _run_phases(phases, carry, pl.program_id(0), ni)
        xhat, _ = _rms_stats(h_ref[...])
        u = (xhat * g_ref[...]).astype(BF16)
        u_ref[...] = u
        for k in range(ns):
            p_ref[k] = _dot(u, w_ref[k]).astype(BF16)

    in_specs = [pl.BlockSpec((tm, d), lambda i: (i, 0)),
                pl.BlockSpec((1, d), lambda i: (0, 0)),
                pl.BlockSpec((ns, d, d), lambda i: (0, 0, 0))]
    out_specs = [pl.BlockSpec((tm, d), lambda i: (i, 0)),
                 pl.BlockSpec((ns, tm, d), lambda i: (0, i, 0))]
    out_shape = [jax.ShapeDtypeStruct((tp, d), BF16), jax.ShapeDtypeStruct((ns, tp, d), BF16)]
    args, scratch = [h, g, w_in], []
    split = _attach_carry(carry, in_specs, args, out_specs, out_shape, scratch)
    return pl.pallas_call(
        body, name="win_fwd", grid=(ni,), in_specs=in_specs, out_specs=out_specs, out_shape=out_shape,
        scratch_shapes=scratch, compiler_params=_params(("arbitrary",)), input_output_aliases=split.aliases,
    )(*args)


def _win_bwd_shard(k, ns, du_prev, dpb, dug, u, w_in, h1, g, dh2):
    tp, d = h1.shape
    dh = d // 2
    tm = _pick_tile(tp, (768, 512, 256))
    first, last = k == 0, k == ns - 1

    def body(*refs):
        refs = list(refs)
        acc_in = None if first else refs.pop(0)
        dug_ref = refs.pop(0) if first else None
        dp_ref, u_ref, w_ref = refs[:3]
        refs = refs[3:]
        if last:
            h_ref, g_ref, dh2_ref, acc_out, dw_ref, dg_ref, dy_ref, dw_acc = refs
        else:
            acc_out, dw_ref, dw_acc = refs
        i = pl.program_id(0)

        @pl.when(i == 0)
        def _():
            dw_acc[...] = jnp.zeros_like(dw_acc)
            if last:
                dg_ref[...] = jnp.zeros_like(dg_ref)

        dp = dp_ref[...]
        if first:
            dp = jnp.concatenate([dug_ref[...], dp[:, dh:]], axis=1)
        dw_acc[...] += _dot_tn(u_ref[...], dp)
        du = _dot_nt(dp, w_ref[...])
        if not first:
            du = du + acc_in[...]
        if last:
            xhat, r = _rms_stats(h_ref[...])
            dg_ref[...] += jnp.sum(du * xhat, axis=0, keepdims=True)
            dh1 = dh2_ref[...] + _rms_bwd(xhat, r, g_ref[...], du)
            acc_out[...] = dh1
            dy_ref[...] = (0.5 * dh1).astype(BF16)
        else:
            acc_out[...] = du

        @pl.when(i == tp // tm - 1)
        def _():
            dw_ref[...] = dw_acc[...].astype(BF16)

    row_spec = pl.BlockSpec((tm, d), lambda i: (i, 0))
    vec_spec = pl.BlockSpec((1, d), lambda i: (0, 0))
    in_specs = [pl.BlockSpec((None, tm, d), lambda i: (k, i, 0)), row_spec,
                pl.BlockSpec((None, d, d), lambda i: (k, 0, 0))]
    args = [dpb, u, w_in]
    if first:
        in_specs.insert(0, pl.BlockSpec((tm, dh), lambda i: (i, 0)))
        args.insert(0, dug)
    else:
        in_specs.insert(0, row_spec)
        args.insert(0, du_prev)
    out_specs = [row_spec, pl.BlockSpec((d, d), lambda i: (0, 0))]
    out_shape = [jax.ShapeDtypeStruct((tp, d), F32), jax.ShapeDtypeStruct((d, d), BF16)]
    if last:
        in_specs += [row_spec, vec_spec, row_spec]
        args += [h1, g, dh2]
        out_specs += [vec_spec, row_spec]
        out_shape += [jax.ShapeDtypeStruct((1, d), F32), jax.ShapeDtypeStruct((tp, d), BF16)]
    return pl.pallas_call(
        body, name=f"win_bwd_{k}", grid=(tp // tm,), in_specs=in_specs, out_specs=out_specs,
        out_shape=out_shape, scratch_shapes=[pltpu.VMEM((d, d), F32)],
        compiler_params=_params(("arbitrary",)),
    )(*args)


def _win_bwd(dpb, dug, u, w_in, h1, g, dh2):
    ns = w_in.shape[0]
    acc, dws = None, []
    for k in range(ns):
        outs = _win_bwd_shard(k, ns, acc, dpb, dug, u, w_in, h1, g, dh2)
        acc = outs[0]
        dws.append(outs[1])
    return acc, dws, outs[2], outs[3]


def _cmul(ar, ai, br, bi):
    return ar * br - ai * bi, ar * bi + ai * br


def _scan_rows(j, sub):
    return pl.ds(j * SCAN_SEQS, SCAN_SEQS)


def _permute_rows(src_ref, dst_ref, sub):
    for j in range(sub):
        dst_ref[pl.ds(j * SCAN_SEQS, SCAN_SEQS), :] = src_ref[pl.ds(j, SCAN_SEQS, stride=sub), :]


def _unpermute_rows(src_ref, dst_ref, sub):
    for j in range(sub):
        dst_ref[pl.ds(j, SCAN_SEQS, stride=sub), :] = src_ref[pl.ds(j * SCAN_SEQS, SCAN_SEQS), :]


def _local_scan(x_ref, lr, li, w, sub, reverse):
    hr = jnp.zeros((SCAN_SEQS, w), F32)
    hi = jnp.zeros((SCAN_SEQS, w), F32)
    order = range(sub - 1, -1, -1) if reverse else range(sub)
    for j in order:
        xr = x_ref[_scan_rows(j, sub), pl.ds(0, w)]
        xi = x_ref[_scan_rows(j, sub), pl.ds(w, w)]
        if reverse:
            hr, hi = lr * hr + li * hi + xr, lr * hi - li * hr + xi
        else:
            hr, hi = lr * hr - li * hi + xr, lr * hi + li * hr + xi
        x_ref[_scan_rows(j, sub), pl.ds(0, w)] = hr
        x_ref[_scan_rows(j, sub), pl.ds(w, w)] = hi
    return hr, hi


def _entering_states(er, ei, fr, fi, pow_ref, w, sub, reverse):
    lane = lax.broadcasted_iota(jnp.int32, (SCAN_SEQS, w), 0)
    if reverse:
        edge, shift1 = SCAN_SEQS - 1, SCAN_SEQS - 1
    else:
        edge, shift1 = 0, 1
    zr = jnp.where(lane == edge, pltpu.roll(fr, shift1, 0), pltpu.roll(er, shift1, 0))
    zi = jnp.where(lane == edge, pltpu.roll(fi, shift1, 0), pltpu.roll(ei, shift1, 0))
    for m in range(SCAN_SEQS.bit_length() - 1):
        step, row = 1 << m, sub - 1 + m
        ar = pow_ref[pl.ds(row, 1), pl.ds(0, w)]
        ai = pow_ref[pl.ds(row, 1), pl.ds(w, w)]
        if reverse:
            ai = -ai
            keep = lane < SCAN_SEQS - step
            sr = jnp.where(keep, pltpu.roll(zr, SCAN_SEQS - step, 0), 0.0)
            si = jnp.where(keep, pltpu.roll(zi, SCAN_SEQS - step, 0), 0.0)
        else:
            keep = lane >= step
            sr = jnp.where(keep, pltpu.roll(zr, step, 0), 0.0)
            si = jnp.where(keep, pltpu.roll(zi, step, 0), 0.0)
        pr, pi = _cmul(ar, ai, sr, si)
        zr, zi = zr + pr, zi + pi
    ar = pow_ref[pl.ds(sub - 1, 1), pl.ds(0, w)]
    ai = pow_ref[pl.ds(sub - 1, 1), pl.ds(w, w)]
    if reverse:
        ai = -ai
    pr, pi = _cmul(ar, ai, zr, zi)
    return zr, zi, er + pr, ei + pi


def _scan_fwd(p, mb, mc, powt, dskip):
    _, tp, d = p.shape
    nb, cb, w2 = mb.shape
    w = w2 // 2
    q = SCAN_TILE
    sub = q // SCAN_SEQS
    nt = tp // q
    ds = d // 2

    def body(ug_ref, mb_ref, mc_ref, pow_ref, d_ref, y_ref, bnd_ref, x_scr, carry, nat, perm):
        t = pl.program_id(1)

        @pl.when(t == 0)
        def _():
            carry[...] = jnp.zeros_like(carry)

        ugf = ug_ref[...].astype(F32)
        nat[...] = ugf
        _permute_rows(nat, perm, sub)
        x_scr[...] = _dot(perm[...].astype(BF16), mb_ref[...])
        lr = jnp.broadcast_to(pow_ref[pl.ds(0, 1), pl.ds(0, w)], (SCAN_SEQS, w))
        li = jnp.broadcast_to(pow_ref[pl.ds(0, 1), pl.ds(w, w)], (SCAN_SEQS, w))
        er, ei = _local_scan(x_scr, lr, li, w, sub, False)
        zr, zi, fr, fi = _entering_states(er, ei, carry[:, pl.ds(0, w)], carry[:, pl.ds(w, w)],
                                          pow_ref, w, sub, False)
        carry[:, pl.ds(0, w)] = fr
        carry[:, pl.ds(w, w)] = fi
        bnd_ref[:, pl.ds(0, w)] = fr
        bnd_ref[:, pl.ds(w, w)] = fi
        for j in range(sub):
            pr = pow_ref[pl.ds(j, 1), pl.ds(0, w)]
            pi = pow_ref[pl.ds(j, 1), pl.ds(w, w)]
            cr, ci = _cmul(pr, pi, zr, zi)
            x_scr[_scan_rows(j, sub), pl.ds(0, w)] += cr
            x_scr[_scan_rows(j, sub), pl.ds(w, w)] += ci
        hb = x_scr[...].astype(BF16)
        perm[...] = _dot_nt(hb, mc_ref[...])
        _unpermute_rows(perm, nat, sub)
        y_ref[...] = nat[...] + d_ref[...] * ugf

    in_specs = [pl.BlockSpec((None, q, cb), lambda b, t: (0, t, b)),
                pl.BlockSpec((None, cb, w2), lambda b, t: (b, 0, 0)),
                pl.BlockSpec((None, cb, w2), lambda b, t: (b, 0, 0)),
                pl.BlockSpec((None, powt.shape[1], w2), lambda b, t: (b, 0, 0)),
                pl.BlockSpec((1, cb), lambda b, t: (0, b))]
    out_specs = [pl.BlockSpec((q, cb), lambda b, t: (t, b)),
                 pl.BlockSpec((None, None, SCAN_SEQS, w2), lambda b, t: (b, t, 0, 0))]
    out_shape = [jax.ShapeDtypeStruct((tp, ds), F32), jax.ShapeDtypeStruct((nb, nt, SCAN_SEQS, w2), F32)]
    scratch = [pltpu.VMEM((q, w2), F32), pltpu.VMEM((SCAN_SEQS, w2), F32),
               pltpu.VMEM((q, cb), F32), pltpu.VMEM((q, cb), F32)]
    return pl.pallas_call(
        body, name="s5_scan_fwd", grid=(nb, nt), in_specs=in_specs, out_specs=out_specs,
        out_shape=out_shape, scratch_shapes=scratch, compiler_params=_params(("arbitrary", "arbitrary")),
    )(p, mb, mc, powt, dskip)


def _scan_bwd(p, dy, mb, mc, powt, dskip, bnd):
    _, tp, d = p.shape
    nb, cb, w2 = mb.shape
    w = w2 // 2
    q = SCAN_TILE
    sub = q // SCAN_SEQS
    nt = tp // q
    ds = d // 2

    def body(ug_ref, dy_ref, mb_ref, mc_ref, pow_ref, d_ref, bnd_ref,
             dug_ref, dmb_ref, dmc_ref, dlam_ref, dd_ref, x_scr, y_scr, gcarry, nat, perm):
        t = pl.program_id(1)
        tt = nt - 1 - t

        @pl.when(t == 0)
        def _():
            gcarry[...] = jnp.zeros_like(gcarry)
            dmb_ref[...] = jnp.zeros_like(dmb_ref)
            dmc_ref[...] = jnp.zeros_like(dmc_ref)
            dlam_ref[...] = jnp.zeros_like(dlam_ref)
            dd_ref[...] = jnp.zeros_like(dd_ref)

        ugf = ug_ref[...].astype(F32)
        dyf = dy_ref[...].astype(F32)
        dd_ref[...] += jnp.sum((dyf * ugf).reshape(q // SUBLANES, SUBLANES, cb), axis=0)
        nat[...] = ugf
        _permute_rows(nat, perm, sub)
        ug = perm[...].astype(BF16)
        nat[...] = dyf
        _permute_rows(nat, perm, sub)
        dyb = perm[...].astype(BF16)
        lr = jnp.broadcast_to(pow_ref[pl.ds(0, 1), pl.ds(0, w)], (SCAN_SEQS, w))
        li = jnp.broadcast_to(pow_ref[pl.ds(0, 1), pl.ds(w, w)], (SCAN_SEQS, w))

        x_scr[...] = _dot(ug, mb_ref[...])
        er, ei = _local_scan(x_scr, lr, li, w, sub, False)
        first = tt == 0
        pfr = jnp.where(first, 0.0, bnd_ref[:, pl.ds(0, w)])
        pfi = jnp.where(first, 0.0, bnd_ref[:, pl.ds(w, w)])
        hzr, hzi, _, _ = _entering_states(er, ei, pfr, pfi, pow_ref, w, sub, False)
        for j in range(sub):
            pr = pow_ref[pl.ds(j, 1), pl.ds(0, w)]
            pi = pow_ref[pl.ds(j, 1), pl.ds(w, w)]
            cr, ci = _cmul(pr, pi, hzr, hzi)
            x_scr[_scan_rows(j, sub), pl.ds(0, w)] += cr
            x_scr[_scan_rows(j, sub), pl.ds(w, w)] += ci

        y_scr[...] = _dot(dyb, mc_ref[...])
        er, ei = _local_scan(y_scr, lr, li, w, sub, True)
        gzr, gzi, fr, fi = _entering_states(er, ei, gcarry[:, pl.ds(0, w)], gcarry[:, pl.ds(w, w)],
                                            pow_ref, w, sub, True)
        gcarry[:, pl.ds(0, w)] = fr
        gcarry[:, pl.ds(w, w)] = fi
        accr = jnp.zeros((SCAN_SEQS, w), F32)
        acci = jnp.zeros((SCAN_SEQS, w), F32)
        for j in range(sub):
            pr = pow_ref[pl.ds(sub - 1 - j, 1), pl.ds(0, w)]
            pi = pow_ref[pl.ds(sub - 1 - j, 1), pl.ds(w, w)]
            cr, ci = _cmul(pr, -pi, gzr, gzi)
            gr = y_scr[_scan_rows(j, sub), pl.ds(0, w)] + cr
            gi = y_scr[_scan_rows(j, sub), pl.ds(w, w)] + ci
            y_scr[_scan_rows(j, sub), pl.ds(0, w)] = gr
            y_scr[_scan_rows(j, sub), pl.ds(w, w)] = gi
            if j == 0:
                hpr, hpi = hzr, hzi
            else:
                hpr = x_scr[_scan_rows(j - 1, sub), pl.ds(0, w)]
                hpi = x_scr[_scan_rows(j - 1, sub), pl.ds(w, w)]
            accr += hpr * gr + hpi * gi
            acci += hpr * gi - hpi * gr
        dlam_ref[:, pl.ds(0, w)] += accr
        dlam_ref[:, pl.ds(w, w)] += acci

        hb = x_scr[...].astype(BF16)
        gb = y_scr[...].astype(BF16)
        dmc_ref[...] += _dot_tn(dyb, hb)
        dmb_ref[...] += _dot_tn(ug, gb)
        perm[...] = _dot_nt(gb, mb_ref[...])
        _unpermute_rows(perm, nat, sub)
        dug_ref[...] = (nat[...] + d_ref[...] * dyf).astype(BF16)

    blk = lambda b, t: (b, 0, 0)
    return pl.pallas_call(
        body, name="s5_scan_bwd", grid=(nb, nt),
        in_specs=[pl.BlockSpec((None, q, cb), lambda b, t: (0, nt - 1 - t, b)),
                  pl.BlockSpec((q, cb), lambda b, t: (nt - 1 - t, b)),
                  pl.BlockSpec((None, cb, w2), blk),
                  pl.BlockSpec((None, cb, w2), blk),
                  pl.BlockSpec((None, powt.shape[1], w2), blk),
                  pl.BlockSpec((1, cb), lambda b, t: (0, b)),
                  pl.BlockSpec((None, None, SCAN_SEQS, w2),
                               lambda b, t: (b, jnp.maximum(nt - 2 - t, 0), 0, 0))],
        out_specs=[pl.BlockSpec((q, cb), lambda b, t: (nt - 1 - t, b)),
                   pl.BlockSpec((None, cb, w2), blk),
                   pl.BlockSpec((None, cb, w2), blk),
                   pl.BlockSpec((None, SCAN_SEQS, w2), blk),
                   pl.BlockSpec((SUBLANES, cb), lambda b, t: (0, b))],
        out_shape=[jax.ShapeDtypeStruct((tp, ds), BF16),
                   jax.ShapeDtypeStruct((nb, cb, w2), F32),
                   jax.ShapeDtypeStruct((nb, cb, w2), F32),
                   jax.ShapeDtypeStruct((nb, SCAN_SEQS, w2), F32),
                   jax.ShapeDtypeStruct((SUBLANES, ds), F32)],
        scratch_shapes=[pltpu.VMEM((q, w2), F32), pltpu.VMEM((q, w2), F32),
                        pltpu.VMEM((SCAN_SEQS, w2), F32), pltpu.VMEM((q, cb), F32), pltpu.VMEM((q, cb), F32)],
        compiler_params=_params(("arbitrary", "arbitrary")),
    )(p, dy, mb, mc, powt, dskip, bnd)


HALO = 16


def _mix_tile(ys5, p0, p1, p2, p3, prev_cin, cw, bgate, wglu, wco, d):
    dh = d // 2
    tm = ys5.shape[0]
    v = p0[:, dh:].astype(F32)
    gbr = p1[:, :dh].astype(F32)
    gcr = p1[:, dh:].astype(F32)
    gact = _gelu(ys5).astype(BF16)
    z = _dot(gact, wglu)
    z1, z2 = z[:, :d], z[:, d:]
    sg = _sigmoid(z2)
    y_ssm = z1 * sg
    cin = gcr * v
    ext = jnp.concatenate([cin, prev_cin], axis=0)
    r1 = pltpu.roll(ext, 1, 0)[:tm]
    r2 = pltpu.roll(ext, 2, 0)[:tm]
    cv = cw[2] * cin + cw[1] * r1 + cw[0] * r2
    cg = (gbr * cv).astype(BF16)
    y_conv = _dot(cg, wco)
    g_s = _sigmoid(p2.astype(F32) + bgate[:, :d])
    g_c = _sigmoid(p3.astype(F32) + bgate[:, d:])
    mixed = g_s * y_ssm + g_c * y_conv
    return dict(v=v, gb=gbr, gc=gcr, gact=gact, z1=z1, sg=sg, y_ssm=y_ssm, cin=cin, r1=r1, r2=r2,
                cv=cv, cg=cg, y_conv=y_conv, g_s=g_s, g_c=g_c, mixed=mixed)


def _mix_fwd(h1, ys5, p, cw, bgate, wglu, wco, wo, carry=None):
    tp, d = h1.shape
    dh = d // 2
    tm = ROW_ALIGN
    ni = tp // tm

    def body(*refs):
        refs, phases = split(refs)
        (h_ref, y_ref, p0_ref, p1_ref, p2_ref, p3_ref, cw_ref, bg_ref, wglu_ref, wco_ref, wo_ref,
         o_ref, prev) = refs
        _run_phases(phases, carry, pl.program_id(0), ni)

        @pl.when(pl.program_id(0) == 0)
        def _():
            prev[...] = jnp.zeros_like(prev)

        cw = [cw_ref[pl.ds(t, 1), :] for t in range(3)]
        f = _mix_tile(y_ref[...], p0_ref[...], p1_ref[...], p2_ref[...], p3_ref[...], prev[...],
                      cw, bg_ref[...], wglu_ref[...], wco_ref[...], d)
        prev[...] = f["cin"][tm - HALO:, :]
        o_ref[...] = h_ref[...] + _dot(f["mixed"].astype(BF16), wo_ref[...])

    row = pl.BlockSpec((tm, d), lambda i: (i, 0))
    full = lambda a: pl.BlockSpec(a.shape, lambda i: (0,) * a.ndim)
    pk = lambda k: pl.BlockSpec((None, tm, d), lambda i, k=k: (k, i, 0))
    in_specs = [row, pl.BlockSpec((tm, dh), lambda i: (i, 0)), pk(0), pk(1), pk(2), pk(3),
                full(cw), full(bgate), full(wglu), full(wco), full(wo)]
    out_specs, out_shape = [row], [jax.ShapeDtypeStruct((tp, d), F32)]
    args, scratch = [h1, ys5, p, p, p, p, cw, bgate, wglu, wco, wo], [pltpu.VMEM((HALO, dh), F32)]
    split = _attach_carry(carry, in_specs, args, out_specs, out_shape, scratch)
    return pl.pallas_call(
        body, name="mix_fwd", grid=(ni,), in_specs=in_specs, out_specs=out_specs, out_shape=out_shape,
        scratch_shapes=scratch, compiler_params=_params(("arbitrary",)), input_output_aliases=split.aliases,
    )(*args)


def _mix_bwd(dh2, ys5, p, cw, bgate, wglu, wco, wo):
    tp, d = dh2.shape
    dh = d // 2
    tm = ROW_ALIGN
    ni = tp // tm
    hb = tm // HALO

    def body(dh_ref, y_ref, p0_ref, p1_ref, p2_ref, p3_ref, h0_ref, h1_ref,
             cw_ref, bg_ref, wglu_ref, wco_ref, wo_ref,
             dys_ref, dpb_ref, dwo_ref, dwglu_ref, dwco_ref, dcw_ref, dbg_ref, nxt):
        i = pl.program_id(0)
        tt = ni - 1 - i

        @pl.when(i == 0)
        def _():
            nxt[...] = jnp.zeros_like(nxt)
            dwo_ref[...] = jnp.zeros_like(dwo_ref)
            dwglu_ref[...] = jnp.zeros_like(dwglu_ref)
            dwco_ref[...] = jnp.zeros_like(dwco_ref)
            dcw_ref[...] = jnp.zeros_like(dcw_ref)
            dbg_ref[...] = jnp.zeros_like(dbg_ref)

        cw = [cw_ref[pl.ds(t, 1), :] for t in range(3)]
        prev_cin = h1_ref[:, dh:].astype(F32) * h0_ref[:, dh:].astype(F32)
        prev_cin = jnp.where(tt == 0, 0.0, prev_cin)
        ys5 = y_ref[...]
        f = _mix_tile(ys5, p0_ref[...], p1_ref[...], p2_ref[...], p3_ref[...], prev_cin,
                      cw, bg_ref[...], wglu_ref[...], wco_ref[...], d)
        dhb = dh_ref[...].astype(BF16)
        dmixed = _dot_nt(dhb, wo_ref[...])
        dwo_ref[...] += _dot_tn(f["mixed"].astype(BF16), dhb)

        g_s, g_c, sg = f["g_s"], f["g_c"], f["sg"]
        dy_ssm = dmixed * g_s
        dy_conv = dmixed * g_c
        dp2 = dmixed * f["y_ssm"] * g_s * (1.0 - g_s)
        dp3 = dmixed * f["y_conv"] * g_c * (1.0 - g_c)
        dbg_ref[:, pl.ds(0, d)] += jnp.sum(dp2, axis=0, keepdims=True)
        dbg_ref[:, pl.ds(d, d)] += jnp.sum(dp3, axis=0, keepdims=True)

        dz = jnp.concatenate([dy_ssm * sg, dy_ssm * f["z1"] * sg * (1.0 - sg)], axis=1).astype(BF16)
        dwglu_ref[...] += _dot_tn(f["gact"], dz)
        dys_ref[...] = (_dot_nt(dz, wglu_ref[...]) * _gelu_grad(ys5)).astype(BF16)

        dycb = dy_conv.astype(BF16)
        dwco_ref[...] += _dot_tn(f["cg"], dycb)
        dcg = _dot_nt(dycb, wco_ref[...])
        dgb = dcg * f["cv"]
        dcv = dcg * f["gb"]
        ext = jnp.concatenate([dcv, nxt[...]], axis=0)
        n1 = pltpu.roll(ext, tm + HALO - 1, 0)[:tm]
        n2 = pltpu.roll(ext, tm + HALO - 2, 0)[:tm]
        nxt[...] = dcv[:HALO, :]
        dcin = cw[2] * dcv + cw[1] * n1 + cw[0] * n2
        dcw_ref[pl.ds(0, 1), :] += jnp.sum(dcv * f["r2"], axis=0, keepdims=True)
        dcw_ref[pl.ds(1, 1), :] += jnp.sum(dcv * f["r1"], axis=0, keepdims=True)
        dcw_ref[pl.ds(2, 1), :] += jnp.sum(dcv * f["cin"], axis=0, keepdims=True)
        dgc = dcin * f["v"]
        dv = dcin * f["gc"]
        dpb_ref[0] = jnp.concatenate([jnp.zeros_like(dv), dv], axis=1).astype(BF16)
        dpb_ref[1] = jnp.concatenate([dgb, dgc], axis=1).astype(BF16)
        dpb_ref[2] = dp2.astype(BF16)
        dpb_ref[3] = dp3.astype(BF16)

    rev = lambda i: ni - 1 - i
    row = pl.BlockSpec((tm, d), lambda i: (rev(i), 0))
    half = pl.BlockSpec((tm, dh), lambda i: (rev(i), 0))
    full = lambda a: pl.BlockSpec(a.shape, lambda i: (0,) * a.ndim)
    pk = lambda k: pl.BlockSpec((None, tm, d), lambda i, k=k: (k, rev(i), 0))
    halo = lambda k: pl.BlockSpec((None, HALO, d), lambda i, k=k: (k, jnp.maximum(rev(i) * hb - 1, 0), 0))
    acc = lambda shape: pl.BlockSpec(shape, lambda i: (0,) * len(shape))
    return pl.pallas_call(
        body, name="mix_bwd", grid=(ni,),
        in_specs=[row, half, pk(0), pk(1), pk(2), pk(3), halo(0), halo(1),
                  full(cw), full(bgate), full(wglu), full(wco), full(wo)],
        out_specs=[half, pl.BlockSpec((4, tm, d), lambda i: (0, rev(i), 0)),
                   acc((d, d)), acc((dh, 2 * d)), acc((dh, d)), acc((SUBLANES, dh)), acc((1, 2 * d))],
        out_shape=[jax.ShapeDtypeStruct((tp, dh), BF16), jax.ShapeDtypeStruct((4, tp, d), BF16),
                   jax.ShapeDtypeStruct((d, d), F32), jax.ShapeDtypeStruct((dh, 2 * d), F32),
                   jax.ShapeDtypeStruct((dh, d), F32), jax.ShapeDtypeStruct((SUBLANES, dh), F32),
                   jax.ShapeDtypeStruct((1, 2 * d), F32)],
        scratch_shapes=[pltpu.VMEM((HALO, dh), F32)],
        compiler_params=_params(("arbitrary",)),
    )(dh2, ys5, p, p, p, p, p, p, cw, bgate, wglu, wco, wo)


ANY = pl.BlockSpec(memory_space=pl.ANY)


def _position():
    return lax.axis_index("x"), lax.axis_index("y"), lax.axis_index("c")


def _remote(src, dst, ssem, rsem, dev):
    return pltpu.make_async_remote_copy(src_ref=src, dst_ref=dst, send_sem=ssem, recv_sem=rsem,
                                        device_id=dev, device_id_type=MESH)


def _cast_pieces(ws, pos):
    n = len(ws)

    def body(pos_ref, *refs):
        for w_ref, o_ref in zip(refs[:n], refs[n:]):
            r4 = o_ref.shape[1]
            o_ref[0] = w_ref[pl.ds(0, r4), :].astype(BF16)
            o_ref[1] = w_ref[pl.ds(r4, r4), :].astype(BF16)

    halves = [(w.shape[0] // 2, w.shape[1]) for w in ws]
    return pl.pallas_call(
        body, name="cast_pieces",
        grid_spec=pltpu.PrefetchScalarGridSpec(
            num_scalar_prefetch=1, grid=(1,),
            in_specs=[pl.BlockSpec(hs, lambda i, pos: (pos[2], 0)) for hs in halves],
            out_specs=[pl.BlockSpec((None, None, None, 2, r2 // 2, cols),
                                    lambda i, pos: (pos[0], pos[1], pos[2], 0, 0, 0)) for r2, cols in halves]),
        out_shape=[jax.ShapeDtypeStruct((2, 2, 2, 2, r2 // 2, cols), BF16) for r2, cols in halves],
        compiler_params=_params(("arbitrary",)),
    )(pos, *ws)


class _Carry:
    def __init__(self, name, arrays, out_shapes, nsem, nlsem, make, fracs, n_inplace=0):
        self.name, self.arrays, self.out_shapes = name, list(arrays), list(out_shapes)
        self.nsem, self.nlsem, self.make, self.fracs = nsem, max(nlsem, 1), make, fracs
        self.n_inplace = n_inplace


def _carry_scratch(carry):
    return [pltpu.SemaphoreType.DMA((carry.nsem,)), pltpu.SemaphoreType.DMA((carry.nsem,)),
            pltpu.SemaphoreType.DMA((carry.nlsem,))]


def _run_carry(carry):
    na, no = len(carry.arrays), len(carry.out_shapes)

    def body(*refs):
        for phase in carry.make(refs[:na], refs[na:na + no], *refs[na + no:]):
            phase()

    return pl.pallas_call(
        body, name=carry.name, in_specs=[ANY] * na, out_specs=[ANY] * no, out_shape=carry.out_shapes,
        scratch_shapes=_carry_scratch(carry), input_output_aliases={i: i for i in range(carry.n_inplace)},
    )(*carry.arrays)


def _attach_carry(carry, in_specs, args, out_specs, out_shape, scratch):
    nhi, nho, nhs = len(in_specs), len(out_specs), len(scratch)
    if carry is None:
        none = lambda refs: (list(refs), [])
        none.aliases = {}
        return none
    na, no = len(carry.arrays), len(carry.out_shapes)
    in_specs += [ANY] * na
    args += carry.arrays
    out_specs += [ANY] * no
    out_shape += carry.out_shapes
    scratch += _carry_scratch(carry)

    def split(refs):
        refs = list(refs)
        o = nhi + na
        host = refs[:nhi] + refs[o:o + nho] + refs[o + nho + no:o + nho + no + nhs]
        sems = refs[o + nho + no + nhs:]
        return host, carry.make(refs[nhi:o], refs[o + nho:o + nho + no], *sems)

    split.aliases = {nhi + i: nho + i for i in range(carry.n_inplace)}
    return split


def _run_phases(phases, carry, step, total):
    for phase, frac in zip(phases, carry.fracs if carry is not None else ()):
        pl.when(step == int(round(frac * (total - 1))))(phase)


def _allgather_carry(name, walls, smalls):
    n, ns = len(walls), len(smalls)
    per = 14
    n_big = per * n

    def make(ins, outs, ssem, rsem, lsem):
        sin = ins[n:]
        wall, sall = outs[:n], outs[n:]
        x, y, c = _position()
        xnb, ynb, sib = (1 - x, y, c), (x, 1 - y, c), (x, y, 1 - c)
        chips = [(1 - x, y), (x, 1 - y), (1 - x, 1 - y)]
        slot = lambda i, xx, yy, cc, h: wall[i].at[xx, yy, cc, h]
        own = lambda i, h: slot(i, x, y, c, h)
        cp = lambda src, dst, s, dev: _remote(src, dst, ssem.at[s], rsem.at[s], dev)
        to_sib = lambda i, xx, yy, h: cp(slot(i, xx, yy, c, h), slot(i, xx, yy, c, h),
                                         per * i + 6 + 4 * xx + 2 * yy + h, sib)

        def local():
            return [pltpu.make_async_copy(sin[i], sall[i].at[2 * x + y], lsem.at[i]) for i in range(ns)]

        def small(px, py, j, i, landing):
            s = n_big + j * ns + i
            return cp(sin[i], sall[i].at[landing], s, (px, py, c))

        def first_hop():
            for lc in local():
                lc.start()
            for j, (px, py) in enumerate(chips):
                for i in range(ns):
                    small(px, py, j, i, 2 * x + y).start()
            for i in range(n):
                cp(own(i, 0), slot(i, x, y, c, 0), per * i, xnb).start()
                cp(own(i, 1), slot(i, x, y, c, 1), per * i + 1, ynb).start()
                for h in range(2):
                    cp(own(i, h), slot(i, x, y, c, h), per * i + 6 + 4 * x + 2 * y + h, sib).start()

        def second_hop():
            for lc in local():
                lc.wait()
            for i in range(n):
                cp(slot(i, 1 - x, y, c, 0), slot(i, 1 - x, y, c, 0), per * i, xnb).wait_recv()
                cp(slot(i, x, 1 - y, c, 1), slot(i, x, 1 - y, c, 1), per * i + 1, ynb).wait_recv()
                for j in range(2):
                    cp(slot(i, j, y, c, 0), slot(i, j, y, c, 0), per * i + 2 + j, ynb).start()
                    cp(slot(i, x, j, c, 1), slot(i, x, j, c, 1), per * i + 4 + j, xnb).start()
                to_sib(i, 1 - x, y, 0).start()
                to_sib(i, x, 1 - y, 1).start()

        def last_to_sibling():
            for i in range(n):
                for j in range(2):
                    cp(slot(i, j, 1 - y, c, 0), slot(i, j, 1 - y, c, 0), per * i + 2 + j, ynb).wait_recv()
                    cp(slot(i, 1 - x, j, c, 1), slot(i, 1 - x, j, c, 1), per * i + 4 + j, xnb).wait_recv()
                    to_sib(i, j, 1 - y, 0).start()
                    to_sib(i, 1 - x, j, 1).start()

        def finish():
            for i in range(n):
                for xx in range(2):
                    for yy in range(2):
                        for h in range(2):
                            s = per * i + 6 + 4 * xx + 2 * yy + h
                            cp(slot(i, xx, yy, 1 - c, h), slot(i, xx, yy, 1 - c, h), s, sib).wait_recv()
                            to_sib(i, xx, yy, h).wait_send()
                cp(own(i, 0), slot(i, x, y, c, 0), per * i, xnb).wait_send()
                cp(own(i, 1), slot(i, x, y, c, 1), per * i + 1, ynb).wait_send()
                for j in range(2):
                    cp(slot(i, j, y, c, 0), slot(i, j, y, c, 0), per * i + 2 + j, ynb).wait_send()
                    cp(slot(i, x, j, c, 1), slot(i, x, j, c, 1), per * i + 4 + j, xnb).wait_send()
            for j, (px, py) in enumerate(chips):
                for i in range(ns):
                    small(px, py, j, i, 2 * px + py).wait_recv()
                    small(px, py, j, i, 2 * x + y).wait_send()

        return [first_hop, second_hop, last_to_sibling, finish]

    out_shapes = [jax.ShapeDtypeStruct(a.shape, a.dtype) for a in walls]
    out_shapes += [jax.ShapeDtypeStruct((4,) + a.shape, a.dtype) for a in smalls]
    return _Carry(name, list(walls) + list(smalls), out_shapes, n_big + 3 * ns, ns, make, (0.0, 0.23, 0.73, 1.0),
                  n_inplace=n)


def _exchange_carry(name, arrays, out_shapes, plan):
    count = plan([None] * len(arrays), [None] * len(out_shapes), None)

    def make(ins, outs, ssem, rsem, lsem):
        def copies():
            return [_remote(src, dst, ssem.at[j], rsem.at[j], peer)
                    for j, (src, dst, peer) in enumerate(plan(ins, outs, _position()))]

        def start():
            for c in copies():
                c.start()

        def wait():
            for c in copies():
                c.wait()

        return [start, wait]

    return _Carry(name, arrays, out_shapes, count, 0, make, (0.0, 1.0))


class _Grad:
    def __init__(self, arrs, kind, shard_shape):
        self.arrs, self.kind = list(arrs), kind
        self.rows, self.cols = shard_shape
        self.r2 = self.rows // 2

    def view(self, refs, k, h):
        r2 = self.r2
        if self.kind == "list":
            return refs[k].at[pl.ds(h * r2, r2), :]
        if self.kind == "stacked":
            return refs[0].at[k, pl.ds(h * r2, r2), :]
        if self.kind == "col":
            return refs[0].at[pl.ds(h * r2, r2), pl.ds(k * self.cols, self.cols)]
        return refs[0].at[pl.ds((2 * k + h) * r2, r2), :]

    def half_specs(self):
        r2, cols = self.r2, self.cols
        if self.kind == "list":
            return [pl.BlockSpec((r2, cols), lambda k, pos: (pos[2], 0))] * len(self.arrs)
        if self.kind == "stacked":
            return [pl.BlockSpec((None, r2, cols), lambda k, pos: (k, pos[2], 0))]
        if self.kind == "col":
            return [pl.BlockSpec((r2, cols), lambda k, pos: (pos[2], k))]
        return [pl.BlockSpec((r2, cols), lambda k, pos: (2 * k + pos[2], 0))]

    def step_bytes(self):
        return self.r2 * self.cols * (len(self.arrs) * self.arrs[0].dtype.itemsize + self.arrs[0].dtype.itemsize + 6)


def _add_halves(grads, recvs, pos):
    n = len(grads)
    counts = [len(g.arrs) for g in grads]
    n_mine = sum(counts)

    def body(pos_ref, *refs):
        o = 0
        for i in range(n):
            m_refs = refs[o:o + counts[i]]
            o += counts[i]
            r_ref, of_ref, ob_ref = refs[n_mine + i], refs[n_mine + n + i], refs[n_mine + 2 * n + i]
            mine = m_refs[0][...]
            for kk in range(1, counts[i]):
                mine = jnp.where(pl.program_id(0) == kk, m_refs[kk][...], mine)
            s = mine.astype(F32) + r_ref[...].astype(F32)
            of_ref[...] = s
            ob_ref[...] = s.astype(BF16)

    blks = [pl.BlockSpec((None, g.r2, g.cols), lambda k, pos: (k, 0, 0)) for g in grads]
    outs = pl.pallas_call(
        body, name="rs_add_c",
        grid_spec=pltpu.PrefetchScalarGridSpec(
            num_scalar_prefetch=1, grid=(4,),
            in_specs=[spec for g in grads for spec in g.half_specs()] + blks, out_specs=blks + blks),
        out_shape=[jax.ShapeDtypeStruct((4, g.r2, g.cols), F32) for g in grads]
        + [jax.ShapeDtypeStruct((4, g.r2, g.cols), BF16) for g in grads],
        compiler_params=_params(("arbitrary",)),
    )(pos, *[a for g in grads for a in g.arrs], *recvs)
    return list(zip(outs[:n], outs[n:]))


def _row_tile(rows, cols):
    fits = [t for t in range(16, rows + 1, 16) if rows % t == 0 and t * cols * 4 <= 2 * 1024 * 1024]
    return max(fits) if fits else rows


def _adamw_math(w, g, m, v):
    m = ADAM_B1 * m + (1.0 - ADAM_B1) * g
    v = ADAM_B2 * v + (1.0 - ADAM_B2) * (g * g)
    m_hat = m / (1.0 - ADAM_B1 ** ADAM_STEP)
    v_hat = v / (1.0 - ADAM_B2 ** ADAM_STEP)
    delta = -ADAM_LR * (m_hat / (jnp.sqrt(v_hat) + ADAM_EPS) + ADAM_WD * w)
    return delta, m, v


def _adamw_big(w, m, v, own, sib, pos):
    rows, cols = w.shape
    r2 = rows // 2

    tr = _row_tile(r2, cols)
    nt = r2 // tr

    def body(pos_ref, w_ref, m_ref, v_ref, own_ref, sib_ref, g_ref, d_ref, nm_ref, nv_ref):
        h = pl.program_id(0)
        g = jnp.where(h == pos_ref[2], own_ref[...], sib_ref[...])
        g_ref[...] = g
        d_ref[...], nm_ref[...], nv_ref[...] = _adamw_math(w_ref[...], g, m_ref[...], v_ref[...])

    half = pl.BlockSpec((tr, cols), lambda h, t, pos: (h * nt + t, 0))
    piece = pl.BlockSpec((tr, cols), lambda h, t, pos: (t, 0))
    out = jax.ShapeDtypeStruct((rows, cols), F32)
    return pl.pallas_call(
        body, name="adamw",
        grid_spec=pltpu.PrefetchScalarGridSpec(
            num_scalar_prefetch=1, grid=(2, nt),
            in_specs=[half, half, half, piece, piece],
            out_specs=[half, half, half, half]),
        out_shape=[out, out, out, out],
        compiler_params=_params(("arbitrary", "arbitrary")),
    )(pos, w, m, v, own, sib)


def _add_hop1(s1fs, recvs, pos):
    n = len(s1fs)
    s1vs = [s.reshape((4, 2) + r.shape[2:]) for s, r in zip(s1fs, recvs)]

    def body(pos_ref, *refs):
        for m_ref, r_ref, of_ref, ob_ref in zip(refs[:n], refs[n:2 * n], refs[2 * n:3 * n], refs[3 * n:]):
            s = m_ref[...] + r_ref[...].astype(F32)
            of_ref[...] = s
            ob_ref[...] = s.astype(BF16)

    def mine(h, j, pos):
        return (jnp.where(h == 0, 2 * j + pos[1], 2 * pos[0] + j), h, 0, 0)

    tile = lambda r: (None, None) + r.shape[2:]
    blks = [pl.BlockSpec(tile(r), lambda h, j, pos: (h, j, 0, 0)) for r in recvs]
    outs = pl.pallas_call(
        body, name="rs_add_1",
        grid_spec=pltpu.PrefetchScalarGridSpec(
            num_scalar_prefetch=1, grid=(2, 2),
            in_specs=[pl.BlockSpec(tile(r), mine) for r in recvs] + blks, out_specs=blks + blks),
        out_shape=[jax.ShapeDtypeStruct(r.shape, F32) for r in recvs]
        + [jax.ShapeDtypeStruct(r.shape, BF16) for r in recvs],
        compiler_params=_params(("arbitrary", "arbitrary")),
    )(pos, *s1vs, *recvs)
    return list(zip(outs[:n], outs[n:]))


def _own_sum(s2fs, recvs, pos):
    n = len(s2fs)

    def body(pos_ref, *refs):
        for s_ref, r_ref, o_ref in zip(refs[:n], refs[n:2 * n], refs[2 * n:]):
            o_ref[...] = s_ref[...] + r_ref[...].astype(F32)

    blks = [pl.BlockSpec((None,) + r.shape[1:], lambda h, pos: (h, 0, 0)) for r in recvs]
    return pl.pallas_call(
        body, name="own_sum",
        grid_spec=pltpu.PrefetchScalarGridSpec(
            num_scalar_prefetch=1, grid=(2,),
            in_specs=[pl.BlockSpec((None, None) + r.shape[1:],
                                   lambda h, pos: (h, jnp.where(h == 0, pos[0], pos[1]), 0, 0)) for r in recvs]
            + blks, out_specs=blks),
        out_shape=[jax.ShapeDtypeStruct(r.shape, F32) for r in recvs],
        compiler_params=_params(("arbitrary",)),
    )(pos, *s2fs, *recvs)


def _add_small(a, b):
    def body(a_ref, b_ref, o_ref):
        o_ref[...] = a_ref[...] + b_ref[...]

    vm = pl.BlockSpec(memory_space=pltpu.VMEM)
    return pl.pallas_call(body, name="add_small", in_specs=[vm, vm], out_specs=vm,
                          out_shape=jax.ShapeDtypeStruct(a.shape, F32))(a, b)


def _adamw_small(ws, gs, ms, vs):
    n = len(ws)

    def body(*refs):
        for i in range(n):
            w_ref, g_ref, m_ref, v_ref, d_ref, nm_ref, nv_ref = (refs[j * n + i] for j in range(7))
            d_ref[...], nm_ref[...], nv_ref[...] = _adamw_math(w_ref[...], g_ref[...], m_ref[...], v_ref[...])

    vm = pl.BlockSpec(memory_space=pltpu.VMEM)
    outs = pl.pallas_call(body, name="adamw_small", in_specs=[vm] * (4 * n), out_specs=[vm] * (3 * n),
                          out_shape=[jax.ShapeDtypeStruct(w.shape, F32) for w in ws] * 3)(*ws, *gs, *ms, *vs)
    return outs[:n], outs[n:2 * n], outs[2 * n:]


class _ReduceScatter:
    def __init__(self, tag, grads, pos, extra=None):
        self.tag, self.grads, self.pos, self.stage, self.extra = tag, grads, pos, 0, extra

    def carry(self):
        grads, n = self.grads, len(self.grads)
        r4 = [g.r2 // 2 for g in grads]

        first = [sum(len(g.arrs) for g in grads[:i]) for i in range(n)]

        def plan_c(ins, outs, p):
            if p is None:
                return 4 * n
            x, y, c = p
            mine = lambda i: ins[first[i]:first[i] + len(grads[i].arrs)]
            return [(grads[i].view(mine(i), k, 1 - c), outs[i].at[k], (x, y, 1 - c))
                    for i in range(n) for k in range(4)]

        def plan_1(ins, outs, p):
            if p is None:
                return 4 * n
            x, y, c = p
            copies = []
            for i in range(n):
                for j in range(2):
                    copies.append((ins[i].at[2 * j + (1 - y), pl.ds(0, r4[i]), :], outs[i].at[0, j],
                                   (x, 1 - y, c)))
                    copies.append((ins[i].at[2 * (1 - x) + j, pl.ds(r4[i], r4[i]), :], outs[i].at[1, j],
                                   (1 - x, y, c)))
            return copies

        def plan_2(ins, outs, p):
            if p is None:
                return 2 * n
            x, y, c = p
            copies = []
            for i in range(n):
                copies.append((ins[i].at[0, 1 - x], outs[i].at[0], (1 - x, y, c)))
                copies.append((ins[i].at[1, 1 - y], outs[i].at[1], (x, 1 - y, c)))
            return copies

        def plan_s(ins, outs, p):
            if p is None:
                return n
            x, y, c = p
            return [(ins[i], outs[i], (x, y, 1 - c)) for i in range(n)]

        shape = lambda lead, dt: [jax.ShapeDtypeStruct(lead(g) + (g.cols,), dt) for g in grads]
        stage = self.stage
        if stage == 0:
            name, arrays, plan = "exchange_c", [a for g in grads for a in g.arrs], plan_c
            shapes = [jax.ShapeDtypeStruct((4, g.r2, g.cols), g.arrs[0].dtype) for g in grads]
        elif stage == 1:
            name, arrays, plan = "exchange_1", [s[1] for s in self.s1], plan_1
            shapes = shape(lambda g: (2, 2, g.r2 // 2), BF16)
        elif stage == 2:
            name, arrays, plan = "exchange_2", [s[1] for s in self.s2], plan_2
            shapes = shape(lambda g: (2, g.r2 // 2), BF16)
        else:
            name, arrays, plan, shapes = "exchange_sibling", self.own, plan_s, shape(lambda g: (g.r2,), F32)
        if self.extra is not None and stage < 3:
            def with_extra(ins, outs, p, plan=plan):
                if p is None:
                    return plan(ins[:-1], outs[:-1], None) + 1
                x, y, c = p
                peer = [(x, y, 1 - c), (x, 1 - y, c), (1 - x, y, c)][stage]
                return plan(ins[:-1], outs[:-1], p) + [(ins[-1], outs[-1], peer)]

            arrays = arrays + [self.extra]
            shapes = shapes + [jax.ShapeDtypeStruct(self.extra.shape, F32)]
            plan = with_extra
        return _exchange_carry(f"rs_{self.tag}_{name}", arrays, shapes, plan)

    def feed(self, recv):
        grads, pos = self.grads, self.pos
        recv = list(recv)
        if self.extra is not None and self.stage < 3:
            self.extra = _add_small(self.extra, recv.pop())
        if self.stage == 0:
            self.s1, start = [], 0
            while start < len(grads):
                end, size = start, 0
                while end < len(grads) and (end == start or size + grads[end].step_bytes() <= ADD_GROUP_BYTES):
                    size += grads[end].step_bytes()
                    end += 1
                self.s1 += _add_halves(grads[start:end], recv[start:end], pos)
                start = end
        elif self.stage == 1:
            self.s2 = _add_hop1([s[0] for s in self.s1], list(recv), pos)
        elif self.stage == 2:
            own = _own_sum([s[0] for s in self.s2], list(recv), pos)
            self.own = [o.reshape(g.r2, g.cols) for g, o in zip(grads, own)]
        else:
            self.sib = list(recv)
        self.stage += 1

    def run(self):
        while self.stage < 4:
            self.feed(_run_carry(self.carry()))

    def adamw(self, weights):
        return [_adamw_big(w, m, v, o, sb, self.pos) for (w, m, v), o, sb in zip(weights, self.own, self.sib)]


def _block_diag(t, nb):
    g, c, p = t.shape
    gb = g // nb
    t = t.reshape(nb, gb, c, p)
    eye = jnp.eye(gb, dtype=t.dtype)
    return jnp.einsum("bgcp,gh->bgchp", t, eye).reshape(nb, gb * c, gb * p)


def _s5_discretise(a_re, a_im, log_dt, b_re, b_im, c_re, c_im):
    g, p = a_re.shape
    nb = g // GROUPS_PER_BLOCK
    dt = jnp.exp(log_dt)[:, None]
    mag = jnp.exp(a_re * dt)
    lam_re = mag * jnp.cos(a_im * dt)
    lam_im = mag * jnp.sin(a_im * dt)
    den = a_re * a_re + a_im * a_im
    q_re = ((lam_re - 1.0) * a_re + lam_im * a_im) / den
    q_im = (lam_im * a_re - (lam_re - 1.0) * a_im) / den
    bb_re = q_re[..., None] * b_re - q_im[..., None] * b_im
    bb_im = q_re[..., None] * b_im + q_im[..., None] * b_re
    tr = lambda t: jnp.swapaxes(t, 1, 2)
    mb = jnp.concatenate([_block_diag(tr(bb_re), nb), _block_diag(tr(bb_im), nb)], axis=-1)
    mc = jnp.concatenate([_block_diag(c_re, nb), -_block_diag(c_im, nb)], axis=-1)
    lam = jnp.concatenate([lam_re.reshape(nb, -1), lam_im.reshape(nb, -1)], axis=-1)
    return mb, mc, lam


def _s5_powers(a_re, a_im, log_dt, sub):
    g, p = a_re.shape
    nb = g // GROUPS_PER_BLOCK
    dt = jnp.exp(log_dt)[:, None]
    ns = list(range(1, sub + 1)) + [sub << m for m in range(1, SCAN_SEQS.bit_length() - 1)]
    ns += [0] * (-len(ns) % SUBLANES)
    e = jnp.asarray(ns, F32)[:, None, None]
    mag = jnp.exp(a_re[None] * dt[None] * e)
    ang = a_im[None] * dt[None] * e
    re = (mag * jnp.cos(ang)).reshape(len(ns), nb, -1)
    im = (mag * jnp.sin(ang)).reshape(len(ns), nb, -1)
    return jnp.transpose(jnp.concatenate([re, im], axis=-1), (1, 0, 2))


def _pack(parts):
    flat = jnp.concatenate([a.reshape(-1).astype(F32) for a in parts])
    n = flat.shape[0]
    pad = -n % (SUBLANES * LANES)
    return jnp.pad(flat, (0, pad)).reshape(-1, LANES)


def _unpack(buf, like):
    flat = buf.reshape(-1)
    out, o = [], 0
    for a in like:
        out.append(flat[o:o + a.size].reshape(a.shape))
        o += a.size
    return out


def kernel(x, meta_tokens, g_ffn1, ffn1_w_gate, ffn1_w_up, ffn1_w_down, g_mix, w_in, b_gate, ssm_a_re, ssm_a_im, ssm_log_dt, ssm_b_re, ssm_b_im, ssm_c_re, ssm_c_im, ssm_d, ssm_w_glu, conv_w, conv_w_out, w_o, g_ffn2, ffn2_w_gate, ffn2_w_up, ffn2_w_down, g_final, loss_target, m_meta_tokens, m_g_ffn1, m_ffn1_w_gate, m_ffn1_w_up, m_ffn1_w_down, m_g_mix, m_w_in, m_b_gate, m_ssm_a_re, m_ssm_a_im, m_ssm_log_dt, m_ssm_b_re, m_ssm_b_im, m_ssm_c_re, m_ssm_c_im, m_ssm_d, m_ssm_w_glu, m_conv_w, m_conv_w_out, m_w_o, m_g_ffn2, m_ffn2_w_gate, m_ffn2_w_up, m_ffn2_w_down, m_g_final, v_meta_tokens, v_g_ffn1, v_ffn1_w_gate, v_ffn1_w_up, v_ffn1_w_down, v_g_mix, v_w_in, v_b_gate, v_ssm_a_re, v_ssm_a_im, v_ssm_log_dt, v_ssm_b_re, v_ssm_b_im, v_ssm_c_re, v_ssm_c_im, v_ssm_d, v_ssm_w_glu, v_conv_w, v_conv_w_out, v_w_o, v_g_ffn2, v_ffn2_w_gate, v_ffn2_w_up, v_ffn2_w_down, v_g_final):
    seq, d = x.shape[1], x.shape[2]
    n_meta = meta_tokens.shape[0]
    dh = d // 2
    tp = -(-(n_meta + seq) // ROW_ALIGN) * ROW_ALIGN
    mx, my, mc_ = _position()
    pos = jnp.stack([mx, my, mc_]).astype(jnp.int32)
    shard = 2 * mx + my

    big_names = ["ffn1_w_gate", "ffn1_w_up", "ffn1_w_down", "w_in", "ssm_w_glu", "conv_w_out", "w_o",
                 "ffn2_w_gate", "ffn2_w_up", "ffn2_w_down"]
    transposed = {0, 1, 7, 8}
    drop = lambda arrs: [jnp.swapaxes(a.reshape(a.shape[1:]), 0, 1) if i in transposed else a.reshape(a.shape[1:])
                         for i, a in enumerate(arrs)]
    big_w = drop([ffn1_w_gate, ffn1_w_up, ffn1_w_down, w_in, ssm_w_glu, conv_w_out, w_o,
                  ffn2_w_gate, ffn2_w_up, ffn2_w_down])
    big_m = drop([m_ffn1_w_gate, m_ffn1_w_up, m_ffn1_w_down, m_w_in, m_ssm_w_glu, m_conv_w_out,
                  m_w_o, m_ffn2_w_gate, m_ffn2_w_up, m_ffn2_w_down])
    big_v = drop([v_ffn1_w_gate, v_ffn1_w_up, v_ffn1_w_down, v_w_in, v_ssm_w_glu, v_conv_w_out,
                  v_w_o, v_ffn2_w_gate, v_ffn2_w_up, v_ffn2_w_down])
    pieces = _cast_pieces(big_w[:3], pos) + _cast_pieces(big_w[3:], pos)
    conv_local = conv_w.reshape(conv_w.shape[1], conv_w.shape[3])
    n_first = 3
    first = _run_carry(_allgather_carry("allgather_first", pieces[:n_first], [meta_tokens, conv_local]))
    smalls = first[n_first:]
    stack4 = lambda wl: wl.reshape((4, -1, wl.shape[-1]))
    w1g, w1u, w1d = [stack4(wl) for wl in first[:n_first]]
    natural_cols = lambda s: jnp.transpose(s, (1, 0, 2)).reshape(s.shape[1], 4 * s.shape[2])
    meta_full = natural_cols(smalls[0])
    cw_full = natural_cols(smalls[1])
    cw_pad = jnp.pad(cw_full, ((0, SUBLANES - cw_full.shape[0]), (0, 0)))

    s5_args = (ssm_a_re[0], ssm_a_im[0], ssm_log_dt[0], ssm_b_re[0], ssm_b_im[0], ssm_c_re[0], ssm_c_im[0])
    (mb, mc, _), disc_vjp = jax.vjp(_s5_discretise, *s5_args)
    powt = _s5_powers(ssm_a_re[0], ssm_a_im[0], ssm_log_dt[0], SCAN_TILE // SCAN_SEQS)
    mb16, mc16 = mb.astype(BF16), mc.astype(BF16)

    pad_rows = tp - n_meta - seq
    h0 = jnp.concatenate([meta_full, x.reshape(seq, d), jnp.zeros((pad_rows, d), F32)], axis=0)
    h1, a1, b1, n1, *mid = _ffn_fwd(h0, g_ffn1, w1g, w1u, w1d, "ffn1_fwd",
                                    carry=_allgather_carry("allgather_mixer", pieces[3:7], []))
    win_all, wglu_s, wco_s, wo_s = [stack4(wl) for wl in mid]
    wglu_all = natural_cols(wglu_s)
    wco_all = natural_cols(wco_s)
    wo_all = wo_s.reshape(d, d)
    u, p, w2g, w2u = _win_fwd(h1, g_mix, win_all, carry=_allgather_carry("allgather_ffn2_in", pieces[7:9], []))
    ys5, bnd = _scan_fwd(p, mb16, mc16, powt, ssm_d)
    h2, w2d = _mix_fwd(h1, ys5, p, cw_pad, b_gate, wglu_all, wco_all, wo_all,
                       carry=_allgather_carry("allgather_ffn2_out", pieces[9:], []))
    w2g, w2u, w2d = stack4(w2g), stack4(w2u), stack4(w2d)
    dh3, a2, b2, n2, dg_final, loss_part, dy3 = _ffn_fwd(
        h2, g_ffn2, w2g, w2u, w2d, "ffn2_fwd_loss",
        final=(g_final.reshape(1, d), loss_target.reshape(seq, d), n_meta, seq))

    dh2, dw2g, dw2u, dw2d, dg_ffn2 = _ffn_bwd(dh3, dy3, h2, n2, g_ffn2, a2, b2, w2g, w2u, w2d, "ffn2_bwd")
    dys5, dpb, dwo, dwglu, dwco, dcw, dbg = _mix_bwd(dh2, ys5, p, cw_pad, b_gate, wglu_all, wco_all, wo_all)
    dug, dmb, dmc, dlam, dd = _scan_bwd(p, dys5, mb16, mc16, powt, ssm_d, bnd)
    dh1, dwin, dg_mix, dy1 = _win_bwd(dpb, dug, u, win_all, h1, g_mix, dh2)
    shapes = [w.shape for w in big_w]
    kinds = ["list", "list", "list", "list", "col", "col", "row", "list", "list", "list"]
    rest_grads = [dwin, [dwglu], [dwco], [dwo], dw2g, dw2u, dw2d]
    rs_rest = _ReduceScatter("rest", [_Grad(a, k, s) for a, k, s in
                                      zip(rest_grads, kinds[n_first:], shapes[n_first:])], pos)
    grad_x, dw1g, dw1u, dw1d, dg_ffn1, grad_meta = _ffn_bwd(
        dh1, dy1, h0, n1, g_ffn1, a1, b1, w1g, w1u, w1d, "ffn1_bwd", chain=rs_rest, unpad=(n_meta, seq))
    s5_grads = disc_vjp((dmb, dmc, jnp.sum(dlam, axis=1)))
    local_small = [dg_ffn1, dg_mix, dbg, *s5_grads, jnp.sum(dd, axis=0), dg_ffn2, dg_final,
                   grad_meta, dcw[:conv_w.shape[1]]]
    rs_first = _ReduceScatter("first", [_Grad(a, k, s) for a, k, s in
                                        zip([dw1g, dw1u, dw1d], kinds[:n_first], shapes[:n_first])], pos,
                              extra=_pack(local_small))
    rs_first.run()
    wmv = list(zip(big_w, big_m, big_v))
    big_out = rs_first.adamw(wmv[:n_first]) + rs_rest.adamw(wmv[n_first:])
    def lead(i, o):
        o = jnp.swapaxes(o, 0, 1) if i in transposed else o
        return o.reshape((1,) + o.shape)

    big_out = {nme: tuple(lead(i, o) for o in outs) for i, (nme, outs) in enumerate(zip(big_names, big_out))}

    grad_x = grad_x.reshape(1, seq, d)

    small_names = ["g_ffn1", "g_mix", "b_gate", "ssm_a_re", "ssm_a_im", "ssm_log_dt", "ssm_b_re", "ssm_b_im",
                   "ssm_c_re", "ssm_c_im", "ssm_d", "g_ffn2", "g_final", "meta_tokens", "conv_w"]
    small_w = [g_ffn1, g_mix, b_gate, ssm_a_re, ssm_a_im, ssm_log_dt, ssm_b_re, ssm_b_im, ssm_c_re, ssm_c_im,
               ssm_d, g_ffn2, g_final, meta_tokens, conv_w]
    small_m = [m_g_ffn1, m_g_mix, m_b_gate, m_ssm_a_re, m_ssm_a_im, m_ssm_log_dt, m_ssm_b_re, m_ssm_b_im,
               m_ssm_c_re, m_ssm_c_im, m_ssm_d, m_g_ffn2, m_g_final, m_meta_tokens, m_conv_w]
    small_v = [v_g_ffn1, v_g_mix, v_b_gate, v_ssm_a_re, v_ssm_a_im, v_ssm_log_dt, v_ssm_b_re, v_ssm_b_im,
               v_ssm_c_re, v_ssm_c_im, v_ssm_d, v_g_ffn2, v_g_final, v_meta_tokens, v_conv_w]
    reduced = _unpack(rs_first.extra, local_small)
    reduced[-2] = lax.dynamic_slice_in_dim(reduced[-2], shard * meta_tokens.shape[1], meta_tokens.shape[1], 1)
    reduced[-1] = lax.dynamic_slice_in_dim(reduced[-1], shard * conv_w.shape[3], conv_w.shape[3], 1)
    small_g = [r.reshape(w.shape) for r, w in zip(reduced, small_w)]
    two_d = lambda arrs: [a.reshape(1, -1) if a.ndim == 1 else a for a in arrs]
    ds_, nm_, nv_ = _adamw_small(two_d(small_w), two_d(small_g), two_d(small_m), two_d(small_v))
    like = lambda outs: [o.reshape(w.shape) for o, w in zip(outs, small_w)]
    small_out = {nme: o for nme, o in zip(small_names, zip(small_g, like(ds_), like(nm_), like(nv_)))}

    loss = lax.psum(loss_part[0, 0], ("x", "y", "c"))
    order = ["meta_tokens", "g_ffn1", "ffn1_w_gate", "ffn1_w_up", "ffn1_w_down", "g_mix", "w_in", "b_gate",
             "ssm_a_re", "ssm_a_im", "ssm_log_dt", "ssm_b_re", "ssm_b_im", "ssm_c_re", "ssm_c_im", "ssm_d",
             "ssm_w_glu", "conv_w", "conv_w_out", "w_o", "g_ffn2", "ffn2_w_gate", "ffn2_w_up", "ffn2_w_down",
             "g_final"]
    res = {**big_out, **small_out}
    return (loss, grad_x, *[res[nme][0] for nme in order], *[res[nme][1] for nme in order],
            *[res[nme][2] for nme in order], *[res[nme][3] for nme in order])
```

```python
import functools
import math

import jax
import jax.numpy as jnp
from jax import lax
from jax.experimental import pallas as pl
from jax.experimental.pallas import tpu as pltpu

F32 = jnp.float32
BF16 = jnp.bfloat16
MESH = pl.DeviceIdType.MESH

RMS_EPS = 1e-6
ADAM_LR = 0.001
ADAM_B1 = 0.9
ADAM_B2 = 0.999
ADAM_EPS = 1e-08
ADAM_WD = 0.01
ADAM_STEP = 10

LANES = 128
SUBLANES = 8
VMEM_LIMIT = 56 * 1024 * 1024
ADD_GROUP_BYTES = VMEM_LIMIT // 3

ROW_ALIGN = 256
SCAN_TILE = 256
SCAN_SEQS = 16
GROUPS_PER_BLOCK = 8


def _params(sem, vmem=VMEM_LIMIT):
    return pltpu.CompilerParams(dimension_semantics=sem, vmem_limit_bytes=vmem)


def _pick_tile(n, candidates):
    for c in candidates:
        if n % c == 0:
            return c
    raise ValueError(f"no tile for {n}")


def _dot(a, b):
    return jnp.dot(a, b, preferred_element_type=F32)


def _dot_nt(a, b):
    return lax.dot_general(a, b, (((1,), (1,)), ((), ())), preferred_element_type=F32)


def _dot_tn(a, b):
    return lax.dot_general(a, b, (((0,), (0,)), ((), ())), preferred_element_type=F32)


def _sigmoid(x):
    return pl.reciprocal(1.0 + jnp.exp(-x), approx=True)


def _rms_stats(h):
    r = lax.rsqrt(jnp.mean(h * h, axis=-1, keepdims=True) + RMS_EPS)
    return h * r, r


def _rms_bwd(xhat, r, g, dn):
    dxh = dn * g
    return r * (dxh - xhat * jnp.mean(dxh * xhat, axis=-1, keepdims=True))


GELU_K = math.sqrt(2.0 / math.pi)
GELU_C = 0.044715


def _gelu(x):
    return 0.5 * x * (1.0 + jnp.tanh(GELU_K * (x + GELU_C * x * x * x)))


def _gelu_grad(x):
    t = jnp.tanh(GELU_K * (x + GELU_C * x * x * x))
    return 0.5 * (1.0 + t) + 0.5 * x * (1.0 - t * t) * GELU_K * (1.0 + 3.0 * GELU_C * x * x)


def _for_tile_rows(i, ni, tm, n_meta, seq, fn):
    pl.when(i == 0)(lambda: fn(0, min(tm - n_meta, seq), n_meta))
    if ni > 1:
        last_lo = (ni - 1) * tm - n_meta
        pl.when(i == ni - 1)(lambda: fn(last_lo, min(seq - last_lo, tm), 0))
    if ni > 2:
        pl.when((i > 0) & (i < ni - 1))(lambda: fn(pl.multiple_of(i * tm - n_meta, SUBLANES), tm, 0))


def _ffn_fwd(h, g, wg, wu, wd, name, final=None, carry=None):
    tp, d = h.shape
    ns, f4, _ = wg.shape
    tm = _pick_tile(tp, (768, 512, 256))
    ni = tp // tm

    def body(*refs):
        refs, phases = split(refs)
        if final is None:
            h_ref, g_ref, wg_ref, wu_ref, wd_ref, ho_ref, a_ref, b_ref, n_scr, acc = refs
        else:
            (h_ref, g_ref, wg_ref, wu_ref, wd_ref, gf_ref, tg_hbm,
             ho_ref, a_ref, b_ref, n_scr, dgf_ref, loss_ref, dy_ref, acc, tg_ref, tg_sem) = refs
        i = pl.program_id(0)
        k = pl.program_id(1)
        _run_phases(phases, carry, i * ns + k, ni * ns)

        if final is not None:
            def target_rows(lo, n, at):
                return pltpu.make_async_copy(tg_hbm.at[pl.ds(lo, n), :], tg_ref.at[pl.ds(at, n), :], tg_sem)

            def fetch_target(lo, n, at):
                if at > 0:
                    tg_ref[pl.ds(0, at), :] = jnp.zeros((at, d), F32)
                if at + n < tm:
                    tg_ref[pl.ds(at + n, tm - at - n), :] = jnp.zeros((tm - at - n, d), F32)
                target_rows(lo, n, at).start()

            pl.when(k == 0)(lambda: _for_tile_rows(i, ni, tm, final[2], final[3], fetch_target))

        @pl.when(k == 0)
        def _():
            xhat, _ = _rms_stats(h_ref[...])
            n_scr[...] = (xhat * g_ref[...]).astype(BF16)
            acc[...] = jnp.zeros_like(acc)

        n = n_scr[...]
        a = _dot_nt(n, wg_ref[...])
        b = _dot_nt(n, wu_ref[...])
        a_ref[...] = a.astype(BF16)
        b_ref[...] = b.astype(BF16)
        s = (a * _sigmoid(a) * b).astype(BF16)
        acc[...] += _dot(s, wd_ref[...])

        if final is None:
            @pl.when(k == ns - 1)
            def _():
                ho_ref[...] = h_ref[...] + 0.5 * acc[...]
        else:
            n_meta, seq = final[2], final[3]

            @pl.when((i == 0) & (k == 0))
            def _():
                dgf_ref[...] = jnp.zeros_like(dgf_ref)
                loss_ref[...] = jnp.zeros_like(loss_ref)

            @pl.when(k == ns - 1)
            def _():
                _for_tile_rows(i, ni, tm, n_meta, seq, lambda lo, n, at: target_rows(lo, n, at).wait())
                h3 = h_ref[...] + 0.5 * acc[...]
                xhat, r = _rms_stats(h3)
                gf = gf_ref[...]
                row = i * tm + lax.broadcasted_iota(jnp.int32, (tm, d), 0)
                valid = (row >= n_meta) & (row < n_meta + seq)
                diff = jnp.where(valid, xhat * gf - tg_ref[...], 0.0)
                dout = diff * (1.0 / d)
                loss_ref[...] += jnp.full(loss_ref.shape, 0.5 * jnp.sum(diff * diff) * (1.0 / d), F32)
                dgf_ref[...] += jnp.sum(dout * xhat, axis=0, keepdims=True)
                dh3 = _rms_bwd(xhat, r, gf, dout)
                ho_ref[...] = dh3
                dy_ref[...] = (0.5 * dh3).astype(BF16)

    row_spec = pl.BlockSpec((tm, d), lambda i, k: (i, 0))
    vec_spec = pl.BlockSpec((1, d), lambda i, k: (0, 0))
    in_specs = [row_spec, vec_spec,
                pl.BlockSpec((None, f4, d), lambda i, k: (k, 0, 0)),
                pl.BlockSpec((None, f4, d), lambda i, k: (k, 0, 0)),
                pl.BlockSpec((None, f4, d), lambda i, k: (k, 0, 0))]
    act_spec = pl.BlockSpec((None, tm, f4), lambda i, k: (k, i, 0))
    out_specs = [row_spec, act_spec, act_spec, row_spec]
    out_shape = [jax.ShapeDtypeStruct((tp, d), F32),
                 jax.ShapeDtypeStruct((ns, tp, f4), BF16),
                 jax.ShapeDtypeStruct((ns, tp, f4), BF16),
                 jax.ShapeDtypeStruct((tp, d), BF16)]
    args = [h, g, wg, wu, wd]
    scratch = [pltpu.VMEM((tm, d), F32)]
    if final is not None:
        in_specs += [vec_spec, ANY]
        args += [final[0], final[1]]
        out_specs += [vec_spec, pl.BlockSpec((1, LANES), lambda i, k: (0, 0)), row_spec]
        out_shape += [jax.ShapeDtypeStruct((1, d), F32), jax.ShapeDtypeStruct((1, LANES), F32),
                      jax.ShapeDtypeStruct((tp, d), BF16)]
        scratch += [pltpu.VMEM((tm, d), F32), pltpu.SemaphoreType.DMA(())]
    split = _attach_carry(carry, in_specs, args, out_specs, out_shape, scratch)
    return pl.pallas_call(
        body, name=name, grid=(ni, ns), in_specs=in_specs, out_specs=out_specs, out_shape=out_shape,
        scratch_shapes=scratch, compiler_params=_params(("arbitrary", "arbitrary")),
        input_output_aliases=split.aliases,
    )(*args)


def _ffn_bwd_shard(k, ns, dn_prev, dy, n, a, b, wg, wu, wd, tail, name, carry=None, unpad=None):
    tp, d = n.shape
    f4 = wg.shape[1]
    tm = _pick_tile(tp, (768, 512, 256))
    ni = tp // tm
    first, last = k == 0, k == ns - 1
    unpad = unpad if last else None

    def body(*refs):
        refs, phases = split(refs)
        acc_in = None if first else refs.pop(0)
        if last:
            dh_ref, h_ref, g_ref = refs[:3]
        else:
            dy_ref, n_ref = refs[:2]
        refs = refs[3 if last else 2:]
        a_ref, b_ref, wg_hbm, wu_hbm, wd_hbm = refs[:5]
        refs = refs[5:]
        acc_out, dwg_hbm, dwu_hbm, dwd_hbm = refs[:4]
        rest = refs[4:]
        dg_ref = rest.pop(0) if last else None
        head_ref = rest.pop(0) if unpad else None
        wg_ref, wu_ref, wd_ref, dwg_ref, dwu_ref, dwd_ref, wsem = rest[:7]
        i = pl.program_id(0)
        _run_phases(phases, carry, i, ni)
        if unpad:
            res_ref, res_sem = rest[7:]

            def real_rows(lo, cnt, at):
                return pltpu.make_async_copy(res_ref.at[pl.ds(at, cnt), :], acc_out.at[pl.ds(lo, cnt), :], res_sem)

            def wait_tile(tile):
                _for_tile_rows(tile, ni, tm, *unpad, lambda lo, cnt, at: real_rows(lo, cnt, at).wait())

        @pl.when(i == 0)
        def _():
            loads = [pltpu.make_async_copy(src.at[k], dst, wsem.at[j])
                     for j, (src, dst) in enumerate(((wg_hbm, wg_ref), (wu_hbm, wu_ref), (wd_hbm, wd_ref)))]
            for cp in loads:
                cp.start()
            dwg_ref[...] = jnp.zeros_like(dwg_ref)
            dwu_ref[...] = jnp.zeros_like(dwu_ref)
            dwd_ref[...] = jnp.zeros_like(dwd_ref)
            if last:
                dg_ref[...] = jnp.zeros_like(dg_ref)
            for cp in loads:
                cp.wait()

        if last:
            xhat, r = _rms_stats(h_ref[...])
            n = (xhat * g_ref[...]).astype(BF16)
            dy = (0.5 * dh_ref[...]).astype(BF16)
        else:
            n = n_ref[...]
            dy = dy_ref[...]
        av = a_ref[...].astype(F32)
        bv = b_ref[...].astype(F32)
        sg = _sigmoid(av)
        silu = av * sg
        ds = _dot_nt(dy, wd_ref[...])
        da = (ds * bv * (sg * (1.0 + av * (1.0 - sg)))).astype(BF16)
        db = (ds * silu).astype(BF16)
        s = (silu * bv).astype(BF16)
        dwd_ref[...] += _dot_tn(s, dy)
        dwg_ref[...] += _dot_tn(da, n)
        dwu_ref[...] += _dot_tn(db, n)
        dn = _dot(da, wg_ref[...]) + _dot(db, wu_ref[...])
        if not first:
            dn = dn + acc_in[...]
        if last:
            dg_ref[...] += jnp.sum(dn * xhat, axis=0, keepdims=True)
            dh_in = dh_ref[...] + _rms_bwd(xhat, r, g_ref[...], dn)
            if unpad:
                pl.when(i > 0)(lambda: wait_tile(i - 1))
                res_ref[...] = dh_in

                @pl.when(i == 0)
                def _():
                    head_ref[...] = res_ref[pl.ds(0, unpad[0]), :]

                _for_tile_rows(i, ni, tm, *unpad, lambda lo, cnt, at: real_rows(lo, cnt, at).start())
                pl.when(i == ni - 1)(lambda: wait_tile(i))
            else:
                acc_out[...] = dh_in
        else:
            acc_out[...] = dn

        @pl.when(i == ni - 1)
        def _():
            stores = []
            for j, (acc_ref, stage_ref, out_hbm) in enumerate(((dwg_ref, wg_ref, dwg_hbm), (dwu_ref, wu_ref, dwu_hbm),
                                                              (dwd_ref, wd_ref, dwd_hbm))):
                stage_ref[...] = acc_ref[...].astype(BF16)
                stores.append(pltpu.make_async_copy(stage_ref, out_hbm, wsem.at[j]))
                stores[-1].start()
            for cp in stores:
                cp.wait()

    row_spec = pl.BlockSpec((tm, d), lambda i: (i, 0))
    vec_spec = pl.BlockSpec((1, d), lambda i: (0, 0))
    act_spec = pl.BlockSpec((None, tm, f4), lambda i: (k, i, 0))
    in_specs = [act_spec, act_spec, ANY, ANY, ANY]
    args = [a, b, wg, wu, wd]
    if last:
        in_specs = [row_spec, row_spec, vec_spec] + in_specs
        args = list(tail) + args
    else:
        in_specs = [row_spec, row_spec] + in_specs
        args = [dy, n] + args
    if not first:
        in_specs.insert(0, row_spec)
        args.insert(0, dn_prev)
    out_specs = [row_spec, ANY, ANY, ANY]
    out_shape = [jax.ShapeDtypeStruct((tp, d), F32)] + [jax.ShapeDtypeStruct((f4, d), BF16)] * 3
    scratch = [pltpu.VMEM((f4, d), BF16)] * 3 + [pltpu.VMEM((f4, d), F32)] * 3 + [pltpu.SemaphoreType.DMA((3,))]
    if last:
        out_specs.append(vec_spec)
        out_shape.append(jax.ShapeDtypeStruct((1, d), F32))
    if unpad:
        out_specs[0] = ANY
        out_shape[0] = jax.ShapeDtypeStruct((unpad[1], d), F32)
        out_specs.append(pl.BlockSpec((unpad[0], d), lambda i: (0, 0)))
        out_shape.append(jax.ShapeDtypeStruct((unpad[0], d), F32))
        scratch += [pltpu.VMEM((tm, d), F32), pltpu.SemaphoreType.DMA(())]
    n_host = len(out_shape)
    split = _attach_carry(carry, in_specs, args, out_specs, out_shape, scratch)
    outs = pl.pallas_call(
        body, name=f"{name}_{k}", grid=(ni,), in_specs=in_specs, out_specs=out_specs, out_shape=out_shape,
        scratch_shapes=scratch, compiler_params=_params(("arbitrary",)), input_output_aliases=split.aliases,
    )(*args)
    return outs[:n_host], outs[n_host:]


def _ffn_bwd(dh_out, dy, h_in, n, g, a, b, wg, wu, wd, name, chain=None, unpad=None):
    ns = wg.shape[0]
    acc, dwg, dwu, dwd = None, [], [], []
    for k in range(ns):
        carry = chain.carry() if chain is not None else None
        outs, carried = _ffn_bwd_shard(k, ns, acc, dy, n, a, b, wg, wu, wd, (dh_out, h_in, g), name, carry, unpad)
        if chain is not None:
            chain.feed(carried)
        acc = outs[0]
        dwg.append(outs[1])
        dwu.append(outs[2])
        dwd.append(outs[3])
    return (acc, dwg, dwu, dwd) + tuple(outs[4:])


def _win_fwd(h, g, w_in, carry=None):
    tp, d = h.shape
    ns = w_in.shape[0]
    tm = _pick_tile(tp, (768, 512, 256))
    ni = tp // tm

    def body(*refs):
        (h_ref, g_ref, w_ref, u_ref, p_ref), phases = split(refs)
        _run_phases(phases, carry, pl.program_id(0), ni)
        xhat, _ = _rms_stats(h_ref[...])
        u = (xhat * g_ref[...]).astype(BF16)
        u_ref[...] = u
        for k in range(ns):
            p_ref[k] = _dot(u, w_ref[k]).astype(BF16)

    in_specs = [pl.BlockSpec((tm, d), lambda i: (i, 0)),
                pl.BlockSpec((1, d), lambda i: (0, 0)),
                pl.BlockSpec((ns, d, d), lambda i: (0, 0, 0))]
    out_specs = [pl.BlockSpec((tm, d), lambda i: (i, 0)),
                 pl.BlockSpec((ns, tm, d), lambda i: (0, i, 0))]
    out_shape = [jax.ShapeDtypeStruct((tp, d), BF16), jax.ShapeDtypeStruct((ns, tp, d), BF16)]
    args, scratch = [h, g, w_in], []
    split = _attach_carry(carry, in_specs, args, out_specs, out_shape, scratch)
    return pl.pallas_call(
        body, name="win_fwd", grid=(ni,), in_specs=in_specs, out_specs=out_specs, out_shape=out_shape,
        scratch_shapes=scratch, compiler_params=_params(("arbitrary",)), input_output_aliases=split.aliases,
    )(*args)


def _win_bwd_shard(k, ns, du_prev, dpb, dug, u, w_in, h1, g, dh2):
    tp, d = h1.shape
    dh = d // 2
    tm = _pick_tile(tp, (768, 512, 256))
    first, last = k == 0, k == ns - 1

    def body(*refs):
        refs = list(refs)
        acc_in = None if first else refs.pop(0)
        dug_ref = refs.pop(0) if first else None
        dp_ref, u_ref, w_ref = refs[:3]
        refs = refs[3:]
        if last:
            h_ref, g_ref, dh2_ref, acc_out, dw_ref, dg_ref, dy_ref, dw_acc = refs
        else:
            acc_out, dw_ref, dw_acc = refs
        i = pl.program_id(0)

        @pl.when(i == 0)
        def _():
            dw_acc[...] = jnp.zeros_like(dw_acc)
            if last:
                dg_ref[...] = jnp.zeros_like(dg_ref)

        dp = dp_ref[...]
        if first:
            dp = jnp.concatenate([dug_ref[...], dp[:, dh:]], axis=1)
        dw_acc[...] += _dot_tn(u_ref[...], dp)
        du = _dot_nt(dp, w_ref[...])
        if not first:
            du = du + acc_in[...]
        if last:
            xhat, r = _rms_stats(h_ref[...])
            dg_ref[...] += jnp.sum(du * xhat, axis=0, keepdims=True)
            dh1 = dh2_ref[...] + _rms_bwd(xhat, r, g_ref[...], du)
            acc_out[...] = dh1
            dy_ref[...] = (0.5 * dh1).astype(BF16)
        else:
            acc_out[...] = du

        @pl.when(i == tp // tm - 1)
        def _():
            dw_ref[...] = dw_acc[...].astype(BF16)

    row_spec = pl.BlockSpec((tm, d), lambda i: (i, 0))
    vec_spec = pl.BlockSpec((1, d), lambda i: (0, 0))
    in_specs = [pl.BlockSpec((None, tm, d), lambda i: (k, i, 0)), row_spec,
                pl.BlockSpec((None, d, d), lambda i: (k, 0, 0))]
    args = [dpb, u, w_in]
    if first:
        in_specs.insert(0, pl.BlockSpec((tm, dh), lambda i: (i, 0)))
        args.insert(0, dug)
    else:
        in_specs.insert(0, row_spec)
        args.insert(0, du_prev)
    out_specs = [row_spec, pl.BlockSpec((d, d), lambda i: (0, 0))]
    out_shape = [jax.ShapeDtypeStruct((tp, d), F32), jax.ShapeDtypeStruct((d, d), BF16)]
    if last:
        in_specs += [row_spec, vec_spec, row_spec]
        args += [h1, g, dh2]
        out_specs += [vec_spec, row_spec]
        out_shape += [jax.ShapeDtypeStruct((1, d), F32), jax.ShapeDtypeStruct((tp, d), BF16)]
    return pl.pallas_call(
        body, name=f"win_bwd_{k}", grid=(tp // tm,), in_specs=in_specs, out_specs=out_specs,
        out_shape=out_shape, scratch_shapes=[pltpu.VMEM((d, d), F32)],
        compiler_params=_params(("arbitrary",)),
    )(*args)


def _win_bwd(dpb, dug, u, w_in, h1, g, dh2):
    ns = w_in.shape[0]
    acc, dws = None, []
    for k in range(ns):
        outs = _win_bwd_shard(k, ns, acc, dpb, dug, u, w_in, h1, g, dh2)
        acc = outs[0]
        dws.append(outs[1])
    return acc, dws, outs[2], outs[3]


def _cmul(ar, ai, br, bi):
    return ar * br - ai * bi, ar * bi + ai * br


def _scan_rows(j, sub):
    return pl.ds(j * SCAN_SEQS, SCAN_SEQS)


def _permute_rows(src_ref, dst_ref, sub):
    for j in range(sub):
        dst_ref[pl.ds(j * SCAN_SEQS, SCAN_SEQS), :] = src_ref[pl.ds(j, SCAN_SEQS, stride=sub), :]


def _unpermute_rows(src_ref, dst_ref, sub):
    for j in range(sub):
        dst_ref[pl.ds(j, SCAN_SEQS, stride=sub), :] = src_ref[pl.ds(j * SCAN_SEQS, SCAN_SEQS), :]


def _local_scan(x_ref, lr, li, w, sub, reverse):
    hr = jnp.zeros((SCAN_SEQS, w), F32)
    hi = jnp.zeros((SCAN_SEQS, w), F32)
    order = range(sub - 1, -1, -1) if reverse else range(sub)
    for j in order:
        xr = x_ref[_scan_rows(j, sub), pl.ds(0, w)]
        xi = x_ref[_scan_rows(j, sub), pl.ds(w, w)]
        if reverse:
            hr, hi = lr * hr + li * hi + xr, lr * hi - li * hr + xi
        else:
            hr, hi = lr * hr - li * hi + xr, lr * hi + li * hr + xi
        x_ref[_scan_rows(j, sub), pl.ds(0, w)] = hr
        x_ref[_scan_rows(j, sub), pl.ds(w, w)] = hi
    return hr, hi


def _entering_states(er, ei, fr, fi, pow_ref, w, sub, reverse):
    lane = lax.broadcasted_iota(jnp.int32, (SCAN_SEQS, w), 0)
    if reverse:
        edge, shift1 = SCAN_SEQS - 1, SCAN_SEQS - 1
    else:
        edge, shift1 = 0, 1
    zr = jnp.where(lane == edge, pltpu.roll(fr, shift1, 0), pltpu.roll(er, shift1, 0))
    zi = jnp.where(lane == edge, pltpu.roll(fi, shift1, 0), pltpu.roll(ei, shift1, 0))
    for m in range(SCAN_SEQS.bit_length() - 1):
        step, row = 1 << m, sub - 1 + m
        ar = pow_ref[pl.ds(row, 1), pl.ds(0, w)]
        ai = pow_ref[pl.ds(row, 1), pl.ds(w, w)]
        if reverse:
            ai = -ai
            keep = lane < SCAN_SEQS - step
            sr = jnp.where(keep, pltpu.roll(zr, SCAN_SEQS - step, 0), 0.0)
            si = jnp.where(keep, pltpu.roll(zi, SCAN_SEQS - step, 0), 0.0)
        else:
            keep = lane >= step
            sr = jnp.where(keep, pltpu.roll(zr, step, 0), 0.0)
            si = jnp.where(keep, pltpu.roll(zi, step, 0), 0.0)
        pr, pi = _cmul(ar, ai, sr, si)
        zr, zi = zr + pr, zi + pi
    ar = pow_ref[pl.ds(sub - 1, 1), pl.ds(0, w)]
    ai = pow_ref[pl.ds(sub - 1, 1), pl.ds(w, w)]
    if reverse:
        ai = -ai
    pr, pi = _cmul(ar, ai, zr, zi)
    return zr, zi, er + pr, ei + pi


def _scan_fwd(p, mb, mc, powt, dskip):
    _, tp, d = p.shape
    nb, cb, w2 = mb.shape
    w = w2 // 2
    q = SCAN_TILE
    sub = q // SCAN_SEQS
    nt = tp // q
    ds = d // 2

    def body(ug_ref, mb_ref, mc_ref, pow_ref, d_ref, y_ref, bnd_ref, x_scr, carry, nat, perm):
        t = pl.program_id(1)

        @pl.when(t == 0)
        def _():
            carry[...] = jnp.zeros_like(carry)

        ugf = ug_ref[...].astype(F32)
        nat[...] = ugf
        _permute_rows(nat, perm, sub)
        x_scr[...] = _dot(perm[...].astype(BF16), mb_ref[...])
        lr = jnp.broadcast_to(pow_ref[pl.ds(0, 1), pl.ds(0, w)], (SCAN_SEQS, w))
        li = jnp.broadcast_to(pow_ref[pl.ds(0, 1), pl.ds(w, w)], (SCAN_SEQS, w))
        er, ei = _local_scan(x_scr, lr, li, w, sub, False)
        zr, zi, fr, fi = _entering_states(er, ei, carry[:, pl.ds(0, w)], carry[:, pl.ds(w, w)],
                                          pow_ref, w, sub, False)
        carry[:, pl.ds(0, w)] = fr
        carry[:, pl.ds(w, w)] = fi
        bnd_ref[:, pl.ds(0, w)] = fr
        bnd_ref[:, pl.ds(w, w)] = fi
        for j in range(sub):
            pr = pow_ref[pl.ds(j, 1), pl.ds(0, w)]
            pi = pow_ref[pl.ds(j, 1), pl.ds(w, w)]
            cr, ci = _cmul(pr, pi, zr, zi)
            x_scr[_scan_rows(j, sub), pl.ds(0, w)] += cr
            x_scr[_scan_rows(j, sub), pl.ds(w, w)] += ci
        hb = x_scr[...].astype(BF16)
        perm[...] = _dot_nt(hb, mc_ref[...])
        _unpermute_rows(perm, nat, sub)
        y_ref[...] = nat[...] + d_ref[...] * ugf

    in_specs = [pl.BlockSpec((None, q, cb), lambda b, t: (0, t, b)),
                pl.BlockSpec((None, cb, w2), lambda b, t: (b, 0, 0)),
                pl.BlockSpec((None, cb, w2), lambda b, t: (b, 0, 0)),
                pl.BlockSpec((None, powt.shape[1], w2), lambda b, t: (b, 0, 0)),
                pl.BlockSpec((1, cb), lambda b, t: (0, b))]
    out_specs = [pl.BlockSpec((q, cb), lambda b, t: (t, b)),
                 pl.BlockSpec((None, None, SCAN_SEQS, w2), lambda b, t: (b, t, 0, 0))]
    out_shape = [jax.ShapeDtypeStruct((tp, ds), F32), jax.ShapeDtypeStruct((nb, nt, SCAN_SEQS, w2), F32)]
    scratch = [pltpu.VMEM((q, w2), F32), pltpu.VMEM((SCAN_SEQS, w2), F32),
               pltpu.VMEM((q, cb), F32), pltpu.VMEM((q, cb), F32)]
    return pl.pallas_call(
        body, name="s5_scan_fwd", grid=(nb, nt), in_specs=in_specs, out_specs=out_specs,
        out_shape=out_shape, scratch_shapes=scratch, compiler_params=_params(("arbitrary", "arbitrary")),
    )(p, mb, mc, powt, dskip)


def _scan_bwd(p, dy, mb, mc, powt, dskip, bnd):
    _, tp, d = p.shape
    nb, cb, w2 = mb.shape
    w = w2 // 2
    q = SCAN_TILE
    sub = q // SCAN_SEQS
    nt = tp // q
    ds = d // 2

    def body(ug_ref, dy_ref, mb_ref, mc_ref, pow_ref, d_ref, bnd_ref,
             dug_ref, dmb_ref, dmc_ref, dlam_ref, dd_ref, x_scr, y_scr, gcarry, nat, perm):
        t = pl.program_id(1)
        tt = nt - 1 - t

        @pl.when(t == 0)
        def _():
            gcarry[...] = jnp.zeros_like(gcarry)
            dmb_ref[...] = jnp.zeros_like(dmb_ref)
            dmc_ref[...] = jnp.zeros_like(dmc_ref)
            dlam_ref[...] = jnp.zeros_like(dlam_ref)
            dd_ref[...] = jnp.zeros_like(dd_ref)

        ugf = ug_ref[...].astype(F32)
        dyf = dy_ref[...].astype(F32)
        dd_ref[...] += jnp.sum((dyf * ugf).reshape(q // SUBLANES, SUBLANES, cb), axis=0)
        nat[...] = ugf
        _permute_rows(nat, perm, sub)
        ug = perm[...].astype(BF16)
        nat[...] = dyf
        _permute_rows(nat, perm, sub)
        dyb = perm[...].astype(BF16)
        lr = jnp.broadcast_to(pow_ref[pl.ds(0, 1), pl.ds(0, w)], (SCAN_SEQS, w))
        li = jnp.broadcast_to(pow_ref[pl.ds(0, 1), pl.ds(w, w)], (SCAN_SEQS, w))

        x_scr[...] = _dot(ug, mb_ref[...])
        er, ei = _local_scan(x_scr, lr, li, w, sub, False)
        first = tt == 0
        pfr = jnp.where(first, 0.0, bnd_ref[:, pl.ds(0, w)])
        pfi = jnp.where(first, 0.0, bnd_ref[:, pl.ds(w, w)])
        hzr, hzi, _, _ = _entering_states(er, ei, pfr, pfi, pow_ref, w, sub, False)
        for j in range(sub):
            pr = pow_ref[pl.ds(j, 1), pl.ds(0, w)]
            pi = pow_ref[pl.ds(j, 1), pl.ds(w, w)]
            cr, ci = _cmul(pr, pi, hzr, hzi)
            x_scr[_scan_rows(j, sub), pl.ds(0, w)] += cr
            x_scr[_scan_rows(j, sub), pl.ds(w, w)] += ci

        y_scr[...] = _dot(dyb, mc_ref[...])
        er, ei = _local_scan(y_scr, lr, li, w, sub, True)
        gzr, gzi, fr, fi = _entering_states(er, ei, gcarry[:, pl.ds(0, w)], gcarry[:, pl.ds(w, w)],
                                            pow_ref, w, sub, True)
        gcarry[:, pl.ds(0, w)] = fr
        gcarry[:, pl.ds(w, w)] = fi
        accr = jnp.zeros((SCAN_SEQS, w), F32)
        acci = jnp.zeros((SCAN_SEQS, w), F32)
        for j in range(sub):
            pr = pow_ref[pl.ds(sub - 1 - j, 1), pl.ds(0, w)]
            pi = pow_ref[pl.ds(sub - 1 - j, 1), pl.ds(w, w)]
            cr, ci = _cmul(pr, -pi, gzr, gzi)
            gr = y_scr[_scan_rows(j, sub), pl.ds(0, w)] + cr
            gi = y_scr[_scan_rows(j, sub), pl.ds(w, w)] + ci
            y_scr[_scan_rows(j, sub), pl.ds(0, w)] = gr
            y_scr[_scan_rows(j, sub), pl.ds(w, w)] = gi
            if j == 0:
                hpr, hpi = hzr, hzi
            else:
                hpr = x_scr[_scan_rows(j - 1, sub), pl.ds(0, w)]
                hpi = x_scr[_scan_rows(j - 1, sub), pl.ds(w, w)]
            accr += hpr * gr + hpi * gi
            acci += hpr * gi - hpi * gr
        dlam_ref[:, pl.ds(0, w)] += accr
        dlam_ref[:, pl.ds(w, w)] += acci

        hb = x_scr[...].astype(BF16)
        gb = y_scr[...].astype(BF16)
        dmc_ref[...] += _dot_tn(dyb, hb)
        dmb_ref[...] += _dot_tn(ug, gb)
        perm[...] = _dot_nt(gb, mb_ref[...])
        _unpermute_rows(perm, nat, sub)
        dug_ref[...] = (nat[...] + d_ref[...] * dyf).astype(BF16)

    blk = lambda b, t: (b, 0, 0)
    return pl.pallas_call(
        body, name="s5_scan_bwd", grid=(nb, nt),
        in_specs=[pl.BlockSpec((None, q, cb), lambda b, t: (0, nt - 1 - t, b)),
                  pl.BlockSpec((q, cb), lambda b, t: (nt - 1 - t, b)),
                  pl.BlockSpec((None, cb, w2), blk),
                  pl.BlockSpec((None, cb, w2), blk),
                  pl.BlockSpec((None, powt.shape[1], w2), blk),
                  pl.BlockSpec((1, cb), lambda b, t: (0, b)),
                  pl.BlockSpec((None, None, SCAN_SEQS, w2),
                               lambda b, t: (b, jnp.maximum(nt - 2 - t, 0), 0, 0))],
        out_specs=[pl.BlockSpec((q, cb), lambda b, t: (nt - 1 - t, b)),
                   pl.BlockSpec((None, cb, w2), blk),
                   pl.BlockSpec((None, cb, w2), blk),
                   pl.BlockSpec((None, SCAN_SEQS, w2), blk),
                   pl.BlockSpec((SUBLANES, cb), lambda b, t: (0, b))],
        out_shape=[jax.ShapeDtypeStruct((tp, ds), BF16),
                   jax.ShapeDtypeStruct((nb, cb, w2), F32),
                   jax.ShapeDtypeStruct((nb, cb, w2), F32),
                   jax.ShapeDtypeStruct((nb, SCAN_SEQS, w2), F32),
                   jax.ShapeDtypeStruct((SUBLANES, ds), F32)],
        scratch_shapes=[pltpu.VMEM((q, w2), F32), pltpu.VMEM((q, w2), F32),
                        pltpu.VMEM((SCAN_SEQS, w2), F32), pltpu.VMEM((q, cb), F32), pltpu.VMEM((q, cb), F32)],
        compiler_params=_params(("arbitrary", "arbitrary")),
    )(p, dy, mb, mc, powt, dskip, bnd)


HALO = 16


def _mix_tile(ys5, p0, p1, p2, p3, prev_cin, cw, bgate, wglu, wco, d):
    dh = d // 2
    tm = ys5.shape[0]
    v = p0[:, dh:].astype(F32)
    gbr = p1[:, :dh].astype(F32)
    gcr = p1[:, dh:].astype(F32)
    gact = _gelu(ys5).astype(BF16)
    z = _dot(gact, wglu)
    z1, z2 = z[:, :d], z[:, d:]
    sg = _sigmoid(z2)
    y_ssm = z1 * sg
    cin = gcr * v
    ext = jnp.concatenate([cin, prev_cin], axis=0)
    r1 = pltpu.roll(ext, 1, 0)[:tm]
    r2 = pltpu.roll(ext, 2, 0)[:tm]
    cv = cw[2] * cin + cw[1] * r1 + cw[0] * r2
    cg = (gbr * cv).astype(BF16)
    y_conv = _dot(cg, wco)
    g_s = _sigmoid(p2.astype(F32) + bgate[:, :d])
    g_c = _sigmoid(p3.astype(F32) + bgate[:, d:])
    mixed = g_s * y_ssm + g_c * y_conv
    return dict(v=v, gb=gbr, gc=gcr, gact=gact, z1=z1, sg=sg, y_ssm=y_ssm, cin=cin, r1=r1, r2=r2,
                cv=cv, cg=cg, y_conv=y_conv, g_s=g_s, g_c=g_c, mixed=mixed)


def _mix_fwd(h1, ys5, p, cw, bgate, wglu, wco, wo, carry=None):
    tp, d = h1.shape
    dh = d // 2
    tm = ROW_ALIGN
    ni = tp // tm

    def body(*refs):
        refs, phases = split(refs)
        (h_ref, y_ref, p0_ref, p1_ref, p2_ref, p3_ref, cw_ref, bg_ref, wglu_ref, wco_ref, wo_ref,
         o_ref, prev) = refs
        _run_phases(phases, carry, pl.program_id(0), ni)

        @pl.when(pl.program_id(0) == 0)
        def _():
            prev[...] = jnp.zeros_like(prev)

        cw = [cw_ref[pl.ds(t, 1), :] for t in range(3)]
        f = _mix_tile(y_ref[...], p0_ref[...], p1_ref[...], p2_ref[...], p3_ref[...], prev[...],
                      cw, bg_ref[...], wglu_ref[...], wco_ref[...], d)
        prev[...] = f["cin"][tm - HALO:, :]
        o_ref[...] = h_ref[...] + _dot(f["mixed"].astype(BF16), wo_ref[...])

    row = pl.BlockSpec((tm, d), lambda i: (i, 0))
    full = lambda a: pl.BlockSpec(a.shape, lambda i: (0,) * a.ndim)
    pk = lambda k: pl.BlockSpec((None, tm, d), lambda i, k=k: (k, i, 0))
    in_specs = [row, pl.BlockSpec((tm, dh), lambda i: (i, 0)), pk(0), pk(1), pk(2), pk(3),
                full(cw), full(bgate), full(wglu), full(wco), full(wo)]
    out_specs, out_shape = [row], [jax.ShapeDtypeStruct((tp, d), F32)]
    args, scratch = [h1, ys5, p, p, p, p, cw, bgate, wglu, wco, wo], [pltpu.VMEM((HALO, dh), F32)]
    split = _attach_carry(carry, in_specs, args, out_specs, out_shape, scratch)
    return pl.pallas_call(
        body, name="mix_fwd", grid=(ni,), in_specs=in_specs, out_specs=out_specs, out_shape=out_shape,
        scratch_shapes=scratch, compiler_params=_params(("arbitrary",)), input_output_aliases=split.aliases,
    )(*args)


def _mix_bwd(dh2, ys5, p, cw, bgate, wglu, wco, wo):
    tp, d = dh2.shape
    dh = d // 2
    tm = ROW_ALIGN
    ni = tp // tm
    hb = tm // HALO

    def body(dh_ref, y_ref, p0_ref, p1_ref, p2_ref, p3_ref, h0_ref, h1_ref,
             cw_ref, bg_ref, wglu_ref, wco_ref, wo_ref,
             dys_ref, dpb_ref, dwo_ref, dwglu_ref, dwco_ref, dcw_ref, dbg_ref, nxt):
        i = pl.program_id(0)
        tt = ni - 1 - i

        @pl.when(i == 0)
        def _():
            nxt[...] = jnp.zeros_like(nxt)
            dwo_ref[...] = jnp.zeros_like(dwo_ref)
            dwglu_ref[...] = jnp.zeros_like(dwglu_ref)
            dwco_ref[...] = jnp.zeros_like(dwco_ref)
            dcw_ref[...] = jnp.zeros_like(dcw_ref)
            dbg_ref[...] = jnp.zeros_like(dbg_ref)

        cw = [cw_ref[pl.ds(t, 1), :] for t in range(3)]
        prev_cin = h1_ref[:, dh:].astype(F32) * h0_ref[:, dh:].astype(F32)
        prev_cin = jnp.where(tt == 0, 0.0, prev_cin)
        ys5 = y_ref[...]
        f = _mix_tile(ys5, p0_ref[...], p1_ref[...], p2_ref[...], p3_ref[...], prev_cin,
                      cw, bg_ref[...], wglu_ref[...], wco_ref[...], d)
        dhb = dh_ref[...].astype(BF16)
        dmixed = _dot_nt(dhb, wo_ref[...])
        dwo_ref[...] += _dot_tn(f["mixed"].astype(BF16), dhb)

        g_s, g_c, sg = f["g_s"], f["g_c"], f["sg"]
        dy_ssm = dmixed * g_s
        dy_conv = dmixed * g_c
        dp2 = dmixed * f["y_ssm"] * g_s * (1.0 - g_s)
        dp3 = dmixed * f["y_conv"] * g_c * (1.0 - g_c)
        dbg_ref[:, pl.ds(0, d)] += jnp.sum(dp2, axis=0, keepdims=True)
        dbg_ref[:, pl.ds(d, d)] += jnp.sum(dp3, axis=0, keepdims=True)

        dz = jnp.concatenate([dy_ssm * sg, dy_ssm * f["z1"] * sg * (1.0 - sg)], axis=1).astype(BF16)
        dwglu_ref[...] += _dot_tn(f["gact"], dz)
        dys_ref[...] = (_dot_nt(dz, wglu_ref[...]) * _gelu_grad(ys5)).astype(BF16)

        dycb = dy_conv.astype(BF16)
        dwco_ref[...] += _dot_tn(f["cg"], dycb)
        dcg = _dot_nt(dycb, wco_ref[...])
        dgb = dcg * f["cv"]
        dcv = dcg * f["gb"]
        ext = jnp.concatenate([dcv, nxt[...]], axis=0)
        n1 = pltpu.roll(ext, tm + HALO - 1, 0)[:tm]
        n2 = pltpu.roll(ext, tm + HALO - 2, 0)[:tm]
        nxt[...] = dcv[:HALO, :]
        dcin = cw[2] * dcv + cw[1] * n1 + cw[0] * n2
        dcw_ref[pl.ds(0, 1), :] += jnp.sum(dcv * f["r2"], axis=0, keepdims=True)
        dcw_ref[pl.ds(1, 1), :] += jnp.sum(dcv * f["r1"], axis=0, keepdims=True)
        dcw_ref[pl.ds(2, 1), :] += jnp.sum(dcv * f["cin"], axis=0, keepdims=True)
        dgc = dcin * f["v"]
        dv = dcin * f["gc"]
        dpb_ref[0] = jnp.concatenate([jnp.zeros_like(dv), dv], axis=1).astype(BF16)
        dpb_ref[1] = jnp.concatenate([dgb, dgc], axis=1).astype(BF16)
        dpb_ref[2] = dp2.astype(BF16)
        dpb_ref[3] = dp3.astype(BF16)

    rev = lambda i: ni - 1 - i
    row = pl.BlockSpec((tm, d), lambda i: (rev(i), 0))
    half = pl.BlockSpec((tm, dh), lambda i: (rev(i), 0))
    full = lambda a: pl.BlockSpec(a.shape, lambda i: (0,) * a.ndim)
    pk = lambda k: pl.BlockSpec((None, tm, d), lambda i, k=k: (k, rev(i), 0))
    halo = lambda k: pl.BlockSpec((None, HALO, d), lambda i, k=k: (k, jnp.maximum(rev(i) * hb - 1, 0), 0))
    acc = lambda shape: pl.BlockSpec(shape, lambda i: (0,) * len(shape))
    return pl.pallas_call(
        body, name="mix_bwd", grid=(ni,),
        in_specs=[row, half, pk(0), pk(1), pk(2), pk(3), halo(0), halo(1),
                  full(cw), full(bgate), full(wglu), full(wco), full(wo)],
        out_specs=[half, pl.BlockSpec((4, tm, d), lambda i: (0, rev(i), 0)),
                   acc((d, d)), acc((dh, 2 * d)), acc((dh, d)), acc((SUBLANES, dh)), acc((1, 2 * d))],
        out_shape=[jax.ShapeDtypeStruct((tp, dh), BF16), jax.ShapeDtypeStruct((4, tp, d), BF16),
                   jax.ShapeDtypeStruct((d, d), F32), jax.ShapeDtypeStruct((dh, 2 * d), F32),
                   jax.ShapeDtypeStruct((dh, d), F32), jax.ShapeDtypeStruct((SUBLANES, dh), F32),
                   jax.ShapeDtypeStruct((1, 2 * d), F32)],
        scratch_shapes=[pltpu.VMEM((HALO, dh), F32)],
        compiler_params=_params(("arbitrary",)),
    )(dh2, ys5, p, p, p, p, p, p, cw, bgate, wglu, wco, wo)


ANY = pl.BlockSpec(memory_space=pl.ANY)


def _position():
    return lax.axis_index("x"), lax.axis_index("y"), lax.axis_index("c")


def _remote(src, dst, ssem, rsem, dev):
    return pltpu.make_async_remote_copy(src_ref=src, dst_ref=dst, send_sem=ssem, recv_sem=rsem,
                                        device_id=dev, device_id_type=MESH)


def _cast_pieces(ws, pos):
    n = len(ws)

    def body(pos_ref, *refs):
        for w_ref, o_ref in zip(refs[:n], refs[n:]):
            r4 = o_ref.shape[1]
            o_ref[0] = w_ref[pl.ds(0, r4), :].astype(BF16)
            o_ref[1] = w_ref[pl.ds(r4, r4), :].astype(BF16)

    halves = [(w.shape[0] // 2, w.shape[1]) for w in ws]
    return pl.pallas_call(
        body, name="cast_pieces",
        grid_spec=pltpu.PrefetchScalarGridSpec(
            num_scalar_prefetch=1, grid=(1,),
            in_specs=[pl.BlockSpec(hs, lambda i, pos: (pos[2], 0)) for hs in halves],
            out_specs=[pl.BlockSpec((None, None, None, 2, r2 // 2, cols),
                                    lambda i, pos: (pos[0], pos[1], pos[2], 0, 0, 0)) for r2, cols in halves]),
        out_shape=[jax.ShapeDtypeStruct((2, 2, 2, 2, r2 // 2, cols), BF16) for r2, cols in halves],
        compiler_params=_params(("arbitrary",)),
    )(pos, *ws)


class _Carry:
    def __init__(self, name, arrays, out_shapes, nsem, nlsem, make, fracs, n_inplace=0):
        self.name, self.arrays, self.out_shapes = name, list(arrays), list(out_shapes)
        self.nsem, self.nlsem, self.make, self.fracs = nsem, max(nlsem, 1), make, fracs
        self.n_inplace = n_inplace


def _carry_scratch(carry):
    return [pltpu.SemaphoreType.DMA((carry.nsem,)), pltpu.SemaphoreType.DMA((carry.nsem,)),
            pltpu.SemaphoreType.DMA((carry.nlsem,))]


def _run_carry(carry):
    na, no = len(carry.arrays), len(carry.out_shapes)

    def body(*refs):
        for phase in carry.make(refs[:na], refs[na:na + no], *refs[na + no:]):
            phase()

    return pl.pallas_call(
        body, name=carry.name, in_specs=[ANY] * na, out_specs=[ANY] * no, out_shape=carry.out_shapes,
        scratch_shapes=_carry_scratch(carry), input_output_aliases={i: i for i in range(carry.n_inplace)},
    )(*carry.arrays)


def _attach_carry(carry, in_specs, args, out_specs, out_shape, scratch):
    nhi, nho, nhs = len(in_specs), len(out_specs), len(scratch)
    if carry is None:
        none = lambda refs: (list(refs), [])
        none.aliases = {}
        return none
    na, no = len(carry.arrays), len(carry.out_shapes)
    in_specs += [ANY] * na
    args += carry.arrays
    out_specs += [ANY] * no
    out_shape += carry.out_shapes
    scratch += _carry_scratch(carry)

    def split(refs):
        refs = list(refs)
        o = nhi + na
        host = refs[:nhi] + refs[o:o + nho] + refs[o + nho + no:o + nho + no + nhs]
        sems = refs[o + nho + no + nhs:]
        return host, carry.make(refs[nhi:o], refs[o + nho:o + nho + no], *sems)

    split.aliases = {nhi + i: nho + i for i in range(carry.n_inplace)}
    return split


def _run_phases(phases, carry, step, total):
    for phase, frac in zip(phases, carry.fracs if carry is not None else ()):
        pl.when(step == int(round(frac * (total - 1))))(phase)


def _allgather_carry(name, walls, smalls):
    n, ns = len(walls), len(smalls)
    per = 14
    n_big = per * n

    def make(ins, outs, ssem, rsem, lsem):
        sin = ins[n:]
        wall, sall = outs[:n], outs[n:]
        x, y, c = _position()
        xnb, ynb, sib = (1 - x, y, c), (x, 1 - y, c), (x, y, 1 - c)
        chips = [(1 - x, y), (x, 1 - y), (1 - x, 1 - y)]
        slot = lambda i, xx, yy, cc, h: wall[i].at[xx, yy, cc, h]
        own = lambda i, h: slot(i, x, y, c, h)
        cp = lambda src, dst, s, dev: _remote(src, dst, ssem.at[s], rsem.at[s], dev)
        to_sib = lambda i, xx, yy, h: cp(slot(i, xx, yy, c, h), slot(i, xx, yy, c, h),
                                         per * i + 6 + 4 * xx + 2 * yy + h, sib)

        def local():
            return [pltpu.make_async_copy(sin[i], sall[i].at[2 * x + y], lsem.at[i]) for i in range(ns)]

        def small(px, py, j, i, landing):
            s = n_big + j * ns + i
            return cp(sin[i], sall[i].at[landing], s, (px, py, c))

        def first_hop():
            for lc in local():
                lc.start()
            for j, (px, py) in enumerate(chips):
                for i in range(ns):
                    small(px, py, j, i, 2 * x + y).start()
            for i in range(n):
                cp(own(i, 0), slot(i, x, y, c, 0), per * i, xnb).start()
                cp(own(i, 1), slot(i, x, y, c, 1), per * i + 1, ynb).start()
                for h in range(2):
                    cp(own(i, h), slot(i, x, y, c, h), per * i + 6 + 4 * x + 2 * y + h, sib).start()

        def second_hop():
            for lc in local():
                lc.wait()
            for i in range(n):
                cp(slot(i, 1 - x, y, c, 0), slot(i, 1 - x, y, c, 0), per * i, xnb).wait_recv()
                cp(slot(i, x, 1 - y, c, 1), slot(i, x, 1 - y, c, 1), per * i + 1, ynb).wait_recv()
                for j in range(2):
                    cp(slot(i, j, y, c, 0), slot(i, j, y, c, 0), per * i + 2 + j, ynb).start()
                    cp(slot(i, x, j, c, 1), slot(i, x, j, c, 1), per * i + 4 + j, xnb).start()
                to_sib(i, 1 - x, y, 0).start()
                to_sib(i, x, 1 - y, 1).start()

        def last_to_sibling():
            for i in range(n):
                for j in range(2):
                    cp(slot(i, j, 1 - y, c, 0), slot(i, j, 1 - y, c, 0), per * i + 2 + j, ynb).wait_recv()
                    cp(slot(i, 1 - x, j, c, 1), slot(i, 1 - x, j, c, 1), per * i + 4 + j, xnb).wait_recv()
                    to_sib(i, j, 1 - y, 0).start()
                    to_sib(i, 1 - x, j, 1).start()

        def finish():
            for i in range(n):
                for xx in range(2):
                    for yy in range(2):
                        for h in range(2):
                            s = per * i + 6 + 4 * xx + 2 * yy + h
                            cp(slot(i, xx, yy, 1 - c, h), slot(i, xx, yy, 1 - c, h), s, sib).wait_recv()
                            to_sib(i, xx, yy, h).wait_send()
                cp(own(i, 0), slot(i, x, y, c, 0), per * i, xnb).wait_send()
                cp(own(i, 1), slot(i, x, y, c, 1), per * i + 1, ynb).wait_send()
                for j in range(2):
                    cp(slot(i, j, y, c, 0), slot(i, j, y, c, 0), per * i + 2 + j, ynb).wait_send()
                    cp(slot(i, x, j, c, 1), slot(i, x, j, c, 1), per * i + 4 + j, xnb).wait_send()
            for j, (px, py) in enumerate(chips):
                for i in range(ns):
                    small(px, py, j, i, 2 * px + py).wait_recv()
                    small(px, py, j, i, 2 * x + y).wait_send()

        return [first_hop, second_hop, last_to_sibling, finish]

    out_shapes = [jax.ShapeDtypeStruct(a.shape, a.dtype) for a in walls]
    out_shapes += [jax.ShapeDtypeStruct((4,) + a.shape, a.dtype) for a in smalls]
    return _Carry(name, list(walls) + list(smalls), out_shapes, n_big + 3 * ns, ns, make, (0.0, 0.23, 0.73, 1.0),
                  n_inplace=n)


def _exchange_carry(name, arrays, out_shapes, plan):
    count = plan([None] * len(arrays), [None] * len(out_shapes), None)

    def make(ins, outs, ssem, rsem, lsem):
        def copies():
            return [_remote(src, dst, ssem.at[j], rsem.at[j], peer)
                    for j, (src, dst, peer) in enumerate(plan(ins, outs, _position()))]

        def start():
            for c in copies():
                c.start()

        def wait():
            for c in copies():
                c.wait()

        return [start, wait]

    return _Carry(name, arrays, out_shapes, count, 0, make, (0.0, 1.0))


class _Grad:
    def __init__(self, arrs, kind, shard_shape):
        self.arrs, self.kind = list(arrs), kind
        self.rows, self.cols = shard_shape
        self.r2 = self.rows // 2

    def view(self, refs, k, h):
        r2 = self.r2
        if self.kind == "list":
            return refs[k].at[pl.ds(h * r2, r2), :]
        if self.kind == "stacked":
            return refs[0].at[k, pl.ds(h * r2, r2), :]
        if self.kind == "col":
            return refs[0].at[pl.ds(h * r2, r2), pl.ds(k * self.cols, self.cols)]
        return refs[0].at[pl.ds((2 * k + h) * r2, r2), :]

    def half_specs(self):
        r2, cols = self.r2, self.cols
        if self.kind == "list":
            return [pl.BlockSpec((r2, cols), lambda k, pos: (pos[2], 0))] * len(self.arrs)
        if self.kind == "stacked":
            return [pl.BlockSpec((None, r2, cols), lambda k, pos: (k, pos[2], 0))]
        if self.kind == "col":
            return [pl.BlockSpec((r2, cols), lambda k, pos: (pos[2], k))]
        return [pl.BlockSpec((r2, cols), lambda k, pos: (2 * k + pos[2], 0))]

    def step_bytes(self):
        return self.r2 * self.cols * (len(self.arrs) * self.arrs[0].dtype.itemsize + self.arrs[0].dtype.itemsize + 6)


def _add_halves(grads, recvs, pos):
    n = len(grads)
    counts = [len(g.arrs) for g in grads]
    n_mine = sum(counts)

    def body(pos_ref, *refs):
        o = 0
        for i in range(n):
            m_refs = refs[o:o + counts[i]]
            o += counts[i]
            r_ref, of_ref, ob_ref = refs[n_mine + i], refs[n_mine + n + i], refs[n_mine + 2 * n + i]
            mine = m_refs[0][...]
            for kk in range(1, counts[i]):
                mine = jnp.where(pl.program_id(0) == kk, m_refs[kk][...], mine)
            s = mine.astype(F32) + r_ref[...].astype(F32)
            of_ref[...] = s
            ob_ref[...] = s.astype(BF16)

    blks = [pl.BlockSpec((None, g.r2, g.cols), lambda k, pos: (k, 0, 0)) for g in grads]
    outs = pl.pallas_call(
        body, name="rs_add_c",
        grid_spec=pltpu.PrefetchScalarGridSpec(
            num_scalar_prefetch=1, grid=(4,),
            in_specs=[spec for g in grads for spec in g.half_specs()] + blks, out_specs=blks + blks),
        out_shape=[jax.ShapeDtypeStruct((4, g.r2, g.cols), F32) for g in grads]
        + [jax.ShapeDtypeStruct((4, g.r2, g.cols), BF16) for g in grads],
        compiler_params=_params(("arbitrary",)),
    )(pos, *[a for g in grads for a in g.arrs], *recvs)
    return list(zip(outs[:n], outs[n:]))


def _row_tile(rows, cols):
    fits = [t for t in range(16, rows + 1, 16) if rows % t == 0 and t * cols * 4 <= 2 * 1024 * 1024]
    return max(fits) if fits else rows


def _adamw_math(w, g, m, v):
    m = ADAM_B1 * m + (1.0 - ADAM_B1) * g
    v = ADAM_B2 * v + (1.0 - ADAM_B2) * (g * g)
    m_hat = m / (1.0 - ADAM_B1 ** ADAM_STEP)
    v_hat = v / (1.0 - ADAM_B2 ** ADAM_STEP)
    delta = -ADAM_LR * (m_hat / (jnp.sqrt(v_hat) + ADAM_EPS) + ADAM_WD * w)
    return delta, m, v


def _adamw_big(w, m, v, own, sib, pos, carry=None):
    rows, cols = w.shape
    r2 = rows // 2

    tr = _row_tile(r2, cols)
    nt = r2 // tr

    def body(pos_ref, *refs):
        (w_ref, m_ref, v_ref, own_ref, sib_ref, g_ref, d_ref, nm_ref, nv_ref), phases = split(refs)
        h = pl.program_id(0)
        _run_phases(phases, carry, h * nt + pl.program_id(1), 2 * nt)
        g = jnp.where(h == pos_ref[2], own_ref[...], sib_ref[...])
        g_ref[...] = g
        d_ref[...], nm_ref[...], nv_ref[...] = _adamw_math(w_ref[...], g, m_ref[...], v_ref[...])

    half = pl.BlockSpec((tr, cols), lambda h, t, pos: (h * nt + t, 0))
    piece = pl.BlockSpec((tr, cols), lambda h, t, pos: (t, 0))
    out = jax.ShapeDtypeStruct((rows, cols), F32)
    in_specs, args = [half, half, half, piece, piece], [w, m, v, own, sib]
    out_specs, out_shape, scratch = [half, half, half, half], [out, out, out, out], []
    split = _attach_carry(carry, in_specs, args, out_specs, out_shape, scratch)
    outs = pl.pallas_call(
        body, name="adamw",
        grid_spec=pltpu.PrefetchScalarGridSpec(
            num_scalar_prefetch=1, grid=(2, nt), in_specs=in_specs, out_specs=out_specs, scratch_shapes=scratch),
        out_shape=out_shape,
        compiler_params=_params(("arbitrary", "arbitrary")),
    )(pos, *args)
    return outs[:4], outs[4:]


def _add_hop1(s1fs, recvs, pos):
    n = len(s1fs)
    s1vs = [s.reshape((4, 2) + r.shape[2:]) for s, r in zip(s1fs, recvs)]

    def body(pos_ref, *refs):
        for m_ref, r_ref, of_ref, ob_ref in zip(refs[:n], refs[n:2 * n], refs[2 * n:3 * n], refs[3 * n:]):
            s = m_ref[...] + r_ref[...].astype(F32)
            of_ref[...] = s
            ob_ref[...] = s.astype(BF16)

    def mine(h, j, pos):
        return (jnp.where(h == 0, 2 * j + pos[1], 2 * pos[0] + j), h, 0, 0)

    tile = lambda r: (None, None) + r.shape[2:]
    blks = [pl.BlockSpec(tile(r), lambda h, j, pos: (h, j, 0, 0)) for r in recvs]
    outs = pl.pallas_call(
        body, name="rs_add_1",
        grid_spec=pltpu.PrefetchScalarGridSpec(
            num_scalar_prefetch=1, grid=(2, 2),
            in_specs=[pl.BlockSpec(tile(r), mine) for r in recvs] + blks, out_specs=blks + blks),
        out_shape=[jax.ShapeDtypeStruct(r.shape, F32) for r in recvs]
        + [jax.ShapeDtypeStruct(r.shape, BF16) for r in recvs],
        compiler_params=_params(("arbitrary", "arbitrary")),
    )(pos, *s1vs, *recvs)
    return list(zip(outs[:n], outs[n:]))


def _own_sum(s2fs, recvs, pos):
    n = len(s2fs)

    def body(pos_ref, *refs):
        for s_ref, r_ref, o_ref in zip(refs[:n], refs[n:2 * n], refs[2 * n:]):
            o_ref[...] = s_ref[...] + r_ref[...].astype(F32)

    blks = [pl.BlockSpec((None,) + r.shape[1:], lambda h, pos: (h, 0, 0)) for r in recvs]
    return pl.pallas_call(
        body, name="own_sum",
        grid_spec=pltpu.PrefetchScalarGridSpec(
            num_scalar_prefetch=1, grid=(2,),
            in_specs=[pl.BlockSpec((None, None) + r.shape[1:],
                                   lambda h, pos: (h, jnp.where(h == 0, pos[0], pos[1]), 0, 0)) for r in recvs]
            + blks, out_specs=blks),
        out_shape=[jax.ShapeDtypeStruct(r.shape, F32) for r in recvs],
        compiler_params=_params(("arbitrary",)),
    )(pos, *s2fs, *recvs)


def _add_small(a, b):
    def body(a_ref, b_ref, o_ref):
        o_ref[...] = a_ref[...] + b_ref[...]

    vm = pl.BlockSpec(memory_space=pltpu.VMEM)
    return pl.pallas_call(body, name="add_small", in_specs=[vm, vm], out_specs=vm,
                          out_shape=jax.ShapeDtypeStruct(a.shape, F32))(a, b)


def _adamw_small(ws, gs, ms, vs):
    n = len(ws)

    def body(*refs):
        for i in range(n):
            w_ref, g_ref, m_ref, v_ref, d_ref, nm_ref, nv_ref = (refs[j * n + i] for j in range(7))
            d_ref[...], nm_ref[...], nv_ref[...] = _adamw_math(w_ref[...], g_ref[...], m_ref[...], v_ref[...])

    vm = pl.BlockSpec(memory_space=pltpu.VMEM)
    outs = pl.pallas_call(body, name="adamw_small", in_specs=[vm] * (4 * n), out_specs=[vm] * (3 * n),
                          out_shape=[jax.ShapeDtypeStruct(w.shape, F32) for w in ws] * 3)(*ws, *gs, *ms, *vs)
    return outs[:n], outs[n:2 * n], outs[2 * n:]


class _ReduceScatter:
    def __init__(self, tag, grads, pos, extra=None):
        self.tag, self.grads, self.pos, self.stage, self.extra = tag, grads, pos, 0, extra

    def carry(self):
        grads, n = self.grads, len(self.grads)
        r4 = [g.r2 // 2 for g in grads]

        first = [sum(len(g.arrs) for g in grads[:i]) for i in range(n)]

        def plan_c(ins, outs, p):
            if p is None:
                return 4 * n
            x, y, c = p
            mine = lambda i: ins[first[i]:first[i] + len(grads[i].arrs)]
            return [(grads[i].view(mine(i), k, 1 - c), outs[i].at[k], (x, y, 1 - c))
                    for i in range(n) for k in range(4)]

        def plan_1(ins, outs, p):
            if p is None:
                return 4 * n
            x, y, c = p
            copies = []
            for i in range(n):
                for j in range(2):
                    copies.append((ins[i].at[2 * j + (1 - y), pl.ds(0, r4[i]), :], outs[i].at[0, j],
                                   (x, 1 - y, c)))
                    copies.append((ins[i].at[2 * (1 - x) + j, pl.ds(r4[i], r4[i]), :], outs[i].at[1, j],
                                   (1 - x, y, c)))
            return copies

        def plan_2(ins, outs, p):
            if p is None:
                return 2 * n
            x, y, c = p
            copies = []
            for i in range(n):
                copies.append((ins[i].at[0, 1 - x], outs[i].at[0], (1 - x, y, c)))
                copies.append((ins[i].at[1, 1 - y], outs[i].at[1], (x, 1 - y, c)))
            return copies

        def plan_s(ins, outs, p):
            if p is None:
                return n
            x, y, c = p
            return [(ins[i], outs[i], (x, y, 1 - c)) for i in range(n)]

        shape = lambda lead, dt: [jax.ShapeDtypeStruct(lead(g) + (g.cols,), dt) for g in grads]
        stage = self.stage
        if stage == 0:
            name, arrays, plan = "exchange_c", [a for g in grads for a in g.arrs], plan_c
            shapes = [jax.ShapeDtypeStruct((4, g.r2, g.cols), g.arrs[0].dtype) for g in grads]
        elif stage == 1:
            name, arrays, plan = "exchange_1", [s[1] for s in self.s1], plan_1
            shapes = shape(lambda g: (2, 2, g.r2 // 2), BF16)
        elif stage == 2:
            name, arrays, plan = "exchange_2", [s[1] for s in self.s2], plan_2
            shapes = shape(lambda g: (2, g.r2 // 2), BF16)
        else:
            name, arrays, plan, shapes = "exchange_sibling", self.own, plan_s, shape(lambda g: (g.r2,), F32)
        if self.extra is not None and stage < 3:
            def with_extra(ins, outs, p, plan=plan):
                if p is None:
                    return plan(ins[:-1], outs[:-1], None) + 1
                x, y, c = p
                peer = [(x, y, 1 - c), (x, 1 - y, c), (1 - x, y, c)][stage]
                return plan(ins[:-1], outs[:-1], p) + [(ins[-1], outs[-1], peer)]

            arrays = arrays + [self.extra]
            shapes = shapes + [jax.ShapeDtypeStruct(self.extra.shape, F32)]
            plan = with_extra
        return _exchange_carry(f"rs_{self.tag}_{name}", arrays, shapes, plan)

    def feed(self, recv):
        grads, pos = self.grads, self.pos
        recv = list(recv)
        if self.extra is not None and self.stage < 3:
            self.extra = _add_small(self.extra, recv.pop())
        if self.stage == 0:
            self.s1, start = [], 0
            while start < len(grads):
                end, size = start, 0
                while end < len(grads) and (end == start or size + grads[end].step_bytes() <= ADD_GROUP_BYTES):
                    size += grads[end].step_bytes()
                    end += 1
                self.s1 += _add_halves(grads[start:end], recv[start:end], pos)
                start = end
        elif self.stage == 1:
            self.s2 = _add_hop1([s[0] for s in self.s1], list(recv), pos)
        elif self.stage == 2:
            own = _own_sum([s[0] for s in self.s2], list(recv), pos)
            self.own = [o.reshape(g.r2, g.cols) for g, o in zip(grads, own)]
        else:
            self.sib = list(recv)
        self.stage += 1

    def run(self, hosts=()):
        hosts, hosted = list(hosts), []
        while self.stage < 4:
            carry = self.carry()
            if hosts:
                results, carried = hosts.pop(0)(carry)
                hosted.append(results)
            else:
                carried = _run_carry(carry)
            self.feed(carried)
        return hosted

    def adamw_calls(self, weights):
        return [functools.partial(_adamw_big, w, m, v, o, sb, self.pos)
                for (w, m, v), o, sb in zip(weights, self.own, self.sib)]


def _block_diag(t, nb):
    g, c, p = t.shape
    gb = g // nb
    t = t.reshape(nb, gb, c, p)
    eye = jnp.eye(gb, dtype=t.dtype)
    return jnp.einsum("bgcp,gh->bgchp", t, eye).reshape(nb, gb * c, gb * p)


def _s5_discretise(a_re, a_im, log_dt, b_re, b_im, c_re, c_im):
    g, p = a_re.shape
    nb = g // GROUPS_PER_BLOCK
    dt = jnp.exp(log_dt)[:, None]
    mag = jnp.exp(a_re * dt)
    lam_re = mag * jnp.cos(a_im * dt)
    lam_im = mag * jnp.sin(a_im * dt)
    den = a_re * a_re + a_im * a_im
    q_re = ((lam_re - 1.0) * a_re + lam_im * a_im) / den
    q_im = (lam_im * a_re - (lam_re - 1.0) * a_im) / den
    bb_re = q_re[..., None] * b_re - q_im[..., None] * b_im
    bb_im = q_re[..., None] * b_im + q_im[..., None] * b_re
    tr = lambda t: jnp.swapaxes(t, 1, 2)
    mb = jnp.concatenate([_block_diag(tr(bb_re), nb), _block_diag(tr(bb_im), nb)], axis=-1)
    mc = jnp.concatenate([_block_diag(c_re, nb), -_block_diag(c_im, nb)], axis=-1)
    lam = jnp.concatenate([lam_re.reshape(nb, -1), lam_im.reshape(nb, -1)], axis=-1)
    return mb, mc, lam


def _s5_powers(a_re, a_im, log_dt, sub):
    g, p = a_re.shape
    nb = g // GROUPS_PER_BLOCK
    dt = jnp.exp(log_dt)[:, None]
    ns = list(range(1, sub + 1)) + [sub << m for m in range(1, SCAN_SEQS.bit_length() - 1)]
    ns += [0] * (-len(ns) % SUBLANES)
    e = jnp.asarray(ns, F32)[:, None, None]
    mag = jnp.exp(a_re[None] * dt[None] * e)
    ang = a_im[None] * dt[None] * e
    re = (mag * jnp.cos(ang)).reshape(len(ns), nb, -1)
    im = (mag * jnp.sin(ang)).reshape(len(ns), nb, -1)
    return jnp.transpose(jnp.concatenate([re, im], axis=-1), (1, 0, 2))


def _pack(parts):
    flat = jnp.concatenate([a.reshape(-1).astype(F32) for a in parts])
    n = flat.shape[0]
    pad = -n % (SUBLANES * LANES)
    return jnp.pad(flat, (0, pad)).reshape(-1, LANES)


def _unpack(buf, like):
    flat = buf.reshape(-1)
    out, o = [], 0
    for a in like:
        out.append(flat[o:o + a.size].reshape(a.shape))
        o += a.size
    return out


def kernel(x, meta_tokens, g_ffn1, ffn1_w_gate, ffn1_w_up, ffn1_w_down, g_mix, w_in, b_gate, ssm_a_re, ssm_a_im, ssm_log_dt, ssm_b_re, ssm_b_im, ssm_c_re, ssm_c_im, ssm_d, ssm_w_glu, conv_w, conv_w_out, w_o, g_ffn2, ffn2_w_gate, ffn2_w_up, ffn2_w_down, g_final, loss_target, m_meta_tokens, m_g_ffn1, m_ffn1_w_gate, m_ffn1_w_up, m_ffn1_w_down, m_g_mix, m_w_in, m_b_gate, m_ssm_a_re, m_ssm_a_im, m_ssm_log_dt, m_ssm_b_re, m_ssm_b_im, m_ssm_c_re, m_ssm_c_im, m_ssm_d, m_ssm_w_glu, m_conv_w, m_conv_w_out, m_w_o, m_g_ffn2, m_ffn2_w_gate, m_ffn2_w_up, m_ffn2_w_down, m_g_final, v_meta_tokens, v_g_ffn1, v_ffn1_w_gate, v_ffn1_w_up, v_ffn1_w_down, v_g_mix, v_w_in, v_b_gate, v_ssm_a_re, v_ssm_a_im, v_ssm_log_dt, v_ssm_b_re, v_ssm_b_im, v_ssm_c_re, v_ssm_c_im, v_ssm_d, v_ssm_w_glu, v_conv_w, v_conv_w_out, v_w_o, v_g_ffn2, v_ffn2_w_gate, v_ffn2_w_up, v_ffn2_w_down, v_g_final):
    seq, d = x.shape[1], x.shape[2]
    n_meta = meta_tokens.shape[0]
    dh = d // 2
    tp = -(-(n_meta + seq) // ROW_ALIGN) * ROW_ALIGN
    mx, my, mc_ = _position()
    pos = jnp.stack([mx, my, mc_]).astype(jnp.int32)
    shard = 2 * mx + my

    big_names = ["ffn1_w_gate", "ffn1_w_up", "ffn1_w_down", "w_in", "ssm_w_glu", "conv_w_out", "w_o",
                 "ffn2_w_gate", "ffn2_w_up", "ffn2_w_down"]
    transposed = {0, 1, 7, 8}
    drop = lambda arrs: [jnp.swapaxes(a.reshape(a.shape[1:]), 0, 1) if i in transposed else a.reshape(a.shape[1:])
                         for i, a in enumerate(arrs)]
    big_w = drop([ffn1_w_gate, ffn1_w_up, ffn1_w_down, w_in, ssm_w_glu, conv_w_out, w_o,
                  ffn2_w_gate, ffn2_w_up, ffn2_w_down])
    big_m = drop([m_ffn1_w_gate, m_ffn1_w_up, m_ffn1_w_down, m_w_in, m_ssm_w_glu, m_conv_w_out,
                  m_w_o, m_ffn2_w_gate, m_ffn2_w_up, m_ffn2_w_down])
    big_v = drop([v_ffn1_w_gate, v_ffn1_w_up, v_ffn1_w_down, v_w_in, v_ssm_w_glu, v_conv_w_out,
                  v_w_o, v_ffn2_w_gate, v_ffn2_w_up, v_ffn2_w_down])
    pieces = _cast_pieces(big_w[:3], pos) + _cast_pieces(big_w[3:], pos)
    conv_local = conv_w.reshape(conv_w.shape[1], conv_w.shape[3])
    n_first = 3
    first = _run_carry(_allgather_carry("allgather_first", pieces[:n_first], [meta_tokens, conv_local]))
    smalls = first[n_first:]
    stack4 = lambda wl: wl.reshape((4, -1, wl.shape[-1]))
    w1g, w1u, w1d = [stack4(wl) for wl in first[:n_first]]
    natural_cols = lambda s: jnp.transpose(s, (1, 0, 2)).reshape(s.shape[1], 4 * s.shape[2])
    meta_full = natural_cols(smalls[0])
    cw_full = natural_cols(smalls[1])
    cw_pad = jnp.pad(cw_full, ((0, SUBLANES - cw_full.shape[0]), (0, 0)))

    s5_args = (ssm_a_re[0], ssm_a_im[0], ssm_log_dt[0], ssm_b_re[0], ssm_b_im[0], ssm_c_re[0], ssm_c_im[0])
    (mb, mc, _), disc_vjp = jax.vjp(_s5_discretise, *s5_args)
    powt = _s5_powers(ssm_a_re[0], ssm_a_im[0], ssm_log_dt[0], SCAN_TILE // SCAN_SEQS)
    mb16, mc16 = mb.astype(BF16), mc.astype(BF16)

    pad_rows = tp - n_meta - seq
    h0 = jnp.concatenate([meta_full, x.reshape(seq, d), jnp.zeros((pad_rows, d), F32)], axis=0)
    h1, a1, b1, n1, *mid = _ffn_fwd(h0, g_ffn1, w1g, w1u, w1d, "ffn1_fwd",
                                    carry=_allgather_carry("allgather_mixer", pieces[3:7], []))
    win_all, wglu_s, wco_s, wo_s = [stack4(wl) for wl in mid]
    wglu_all = natural_cols(wglu_s)
    wco_all = natural_cols(wco_s)
    wo_all = wo_s.reshape(d, d)
    u, p, w2g, w2u = _win_fwd(h1, g_mix, win_all, carry=_allgather_carry("allgather_ffn2_in", pieces[7:9], []))
    ys5, bnd = _scan_fwd(p, mb16, mc16, powt, ssm_d)
    h2, w2d = _mix_fwd(h1, ys5, p, cw_pad, b_gate, wglu_all, wco_all, wo_all,
                       carry=_allgather_carry("allgather_ffn2_out", pieces[9:], []))
    w2g, w2u, w2d = stack4(w2g), stack4(w2u), stack4(w2d)
    dh3, a2, b2, n2, dg_final, loss_part, dy3 = _ffn_fwd(
        h2, g_ffn2, w2g, w2u, w2d, "ffn2_fwd_loss",
        final=(g_final.reshape(1, d), loss_target.reshape(seq, d), n_meta, seq))

    dh2, dw2g, dw2u, dw2d, dg_ffn2 = _ffn_bwd(dh3, dy3, h2, n2, g_ffn2, a2, b2, w2g, w2u, w2d, "ffn2_bwd")
    dys5, dpb, dwo, dwglu, dwco, dcw, dbg = _mix_bwd(dh2, ys5, p, cw_pad, b_gate, wglu_all, wco_all, wo_all)
    dug, dmb, dmc, dlam, dd = _scan_bwd(p, dys5, mb16, mc16, powt, ssm_d, bnd)
    dh1, dwin, dg_mix, dy1 = _win_bwd(dpb, dug, u, win_all, h1, g_mix, dh2)
    shapes = [w.shape for w in big_w]
    kinds = ["list", "list", "list", "list", "col", "col", "row", "list", "list", "list"]
    rest_grads = [dwin, [dwglu], [dwco], [dwo], dw2g, dw2u, dw2d]
    rs_rest = _ReduceScatter("rest", [_Grad(a, k, s) for a, k, s in
                                      zip(rest_grads, kinds[n_first:], shapes[n_first:])], pos)
    grad_x, dw1g, dw1u, dw1d, dg_ffn1, grad_meta = _ffn_bwd(
        dh1, dy1, h0, n1, g_ffn1, a1, b1, w1g, w1u, w1d, "ffn1_bwd", chain=rs_rest, unpad=(n_meta, seq))
    s5_grads = disc_vjp((dmb, dmc, jnp.sum(dlam, axis=1)))
    local_small = [dg_ffn1, dg_mix, dbg, *s5_grads, jnp.sum(dd, axis=0), dg_ffn2, dg_final,
                   grad_meta, dcw[:conv_w.shape[1]]]
    rs_first = _ReduceScatter("first", [_Grad(a, k, s) for a, k, s in
                                        zip([dw1g, dw1u, dw1d], kinds[:n_first], shapes[:n_first])], pos,
                              extra=_pack(local_small))
    wmv = list(zip(big_w, big_m, big_v))
    rest_calls = rs_rest.adamw_calls(wmv[n_first:])
    hosts = [0, 4, 5, 6]
    hosted = dict(zip(hosts, rs_first.run([rest_calls[i] for i in hosts])))
    rest_out = [hosted[i] if i in hosted else call()[0] for i, call in enumerate(rest_calls)]
    big_out = [call()[0] for call in rs_first.adamw_calls(wmv[:n_first])] + rest_out
    def lead(i, o):
        o = jnp.swapaxes(o, 0, 1) if i in transposed else o
        return o.reshape((1,) + o.shape)

    big_out = {nme: tuple(lead(i, o) for o in outs) for i, (nme, outs) in enumerate(zip(big_names, big_out))}

    grad_x = grad_x.reshape(1, seq, d)

    small_names = ["g_ffn1", "g_mix", "b_gate", "ssm_a_re", "ssm_a_im", "ssm_log_dt", "ssm_b_re", "ssm_b_im",
                   "ssm_c_re", "ssm_c_im", "ssm_d", "g_ffn2", "g_final", "meta_tokens", "conv_w"]
    small_w = [g_ffn1, g_mix, b_gate, ssm_a_re, ssm_a_im, ssm_log_dt, ssm_b_re, ssm_b_im, ssm_c_re, ssm_c_im,
               ssm_d, g_ffn2, g_final, meta_tokens, conv_w]
    small_m = [m_g_ffn1, m_g_mix, m_b_gate, m_ssm_a_re, m_ssm_a_im, m_ssm_log_dt, m_ssm_b_re, m_ssm_b_im,
               m_ssm_c_re, m_ssm_c_im, m_ssm_d, m_g_ffn2, m_g_final, m_meta_tokens, m_conv_w]
    small_v = [v_g_ffn1, v_g_mix, v_b_gate, v_ssm_a_re, v_ssm_a_im, v_ssm_log_dt, v_ssm_b_re, v_ssm_b_im,
               v_ssm_c_re, v_ssm_c_im, v_ssm_d, v_g_ffn2, v_g_final, v_meta_tokens, v_conv_w]
    reduced = _unpack(rs_first.extra, local_small)
    reduced[-2] = lax.dynamic_slice_in_dim(reduced[-2], shard * meta_tokens.shape[1], meta_tokens.shape[1], 1)
    reduced[-1] = lax.dynamic_slice_in_dim(reduced[-1], shard * conv_w.shape[3], conv_w.shape[3], 1)
    small_g = [r.reshape(w.shape) for r, w in zip(reduced, small_w)]
    two_d = lambda arrs: [a.reshape(1, -1) if a.ndim == 1 else a for a in arrs]
    ds_, nm_, nv_ = _adamw_small(two_d(small_w), two_d(small_g), two_d(small_m), two_d(small_v))
    like = lambda outs: [o.reshape(w.shape) for o, w in zip(outs, small_w)]
    small_out = {nme: o for nme, o in zip(small_names, zip(small_g, like(ds_), like(nm_), like(nv_)))}

    loss = lax.psum(loss_part[0, 0], ("x", "y", "c"))
    order = ["meta_tokens", "g_ffn1", "ffn1_w_gate", "ffn1_w_up", "ffn1_w_down", "g_mix", "w_in", "b_gate",
             "ssm_a_re", "ssm_a_im", "ssm_log_dt", "ssm_b_re", "ssm_b_im", "ssm_c_re", "ssm_c_im", "ssm_d",
             "ssm_w_glu", "conv_w", "conv_w_out", "w_o", "g_ffn2", "ffn2_w_gate", "ffn2_w_up", "ffn2_w_down",
             "g_final"]
    res = {**big_out, **small_out}
    return (loss, grad_x, *[res[nme][0] for nme in order], *[res[nme][1] for nme in order],
            *[res[nme][2] for nme in order], *[res[nme][3] for nme in order])
```

```python
import functools
import math

import jax
import jax.numpy as jnp
from jax import lax
from jax.experimental import pallas as pl
from jax.experimental.pallas import tpu as pltpu

F32 = jnp.float32
BF16 = jnp.bfloat16
MESH = pl.DeviceIdType.MESH

RMS_EPS = 1e-6
ADAM_LR = 0.001
ADAM_B1 = 0.9
ADAM_B2 = 0.999
ADAM_EPS = 1e-08
ADAM_WD = 0.01
ADAM_STEP = 10

LANES = 128
SUBLANES = 8
VMEM_LIMIT = 56 * 1024 * 1024
ADD_GROUP_BYTES = VMEM_LIMIT // 3

ROW_ALIGN = 256
SCAN_TILE = 256
SCAN_SEQS = 16
GROUPS_PER_BLOCK = 8


def _params(sem, vmem=VMEM_LIMIT):
    return pltpu.CompilerParams(dimension_semantics=sem, vmem_limit_bytes=vmem)


def _pick_tile(n, candidates):
    for c in candidates:
        if n % c == 0:
            return c
    raise ValueError(f"no tile for {n}")


def _dot(a, b):
    return jnp.dot(a, b, preferred_element_type=F32)


def _dot_nt(a, b):
    return lax.dot_general(a, b, (((1,), (1,)), ((), ())), preferred_element_type=F32)


def _dot_tn(a, b):
    return lax.dot_general(a, b, (((0,), (0,)), ((), ())), preferred_element_type=F32)


def _sigmoid(x):
    return pl.reciprocal(1.0 + jnp.exp(-x), approx=True)


def _rms_stats(h):
    r = lax.rsqrt(jnp.mean(h * h, axis=-1, keepdims=True) + RMS_EPS)
    return h * r, r


def _rms_bwd(xhat, r, g, dn):
    dxh = dn * g
    return r * (dxh - xhat * jnp.mean(dxh * xhat, axis=-1, keepdims=True))


GELU_K = math.sqrt(2.0 / math.pi)
GELU_C = 0.044715


def _gelu(x):
    return 0.5 * x * (1.0 + jnp.tanh(GELU_K * (x + GELU_C * x * x * x)))


def _gelu_grad(x):
    t = jnp.tanh(GELU_K * (x + GELU_C * x * x * x))
    return 0.5 * (1.0 + t) + 0.5 * x * (1.0 - t * t) * GELU_K * (1.0 + 3.0 * GELU_C * x * x)


def _for_tile_rows(i, ni, tm, n_meta, seq, fn):
    pl.when(i == 0)(lambda: fn(0, min(tm - n_meta, seq), n_meta))
    if ni > 1:
        last_lo = (ni - 1) * tm - n_meta
        pl.when(i == ni - 1)(lambda: fn(last_lo, min(seq - last_lo, tm), 0))
    if ni > 2:
        pl.when((i > 0) & (i < ni - 1))(lambda: fn(pl.multiple_of(i * tm - n_meta, SUBLANES), tm, 0))


def _ffn_fwd(h, g, wg, wu, wd, name, final=None, carry=None):
    tp, d = h.shape
    ns, f4, _ = wg.shape
    tm = _pick_tile(tp, (768, 512, 256))
    ni = tp // tm

    def body(*refs):
        refs, phases = split(refs)
        if final is None:
            h_ref, g_ref, wg_ref, wu_ref, wd_ref, ho_ref, a_ref, b_ref, n_scr, acc = refs
        else:
            (h_ref, g_ref, wg_ref, wu_ref, wd_ref, gf_ref, tg_hbm,
             ho_ref, a_ref, b_ref, n_scr, dgf_ref, loss_ref, dy_ref, acc, tg_ref, tg_sem) = refs
        i = pl.program_id(0)
        k = pl.program_id(1)
        _run_phases(phases, carry, i * ns + k, ni * ns)

        if final is not None:
            def target_rows(lo, n, at):
                return pltpu.make_async_copy(tg_hbm.at[pl.ds(lo, n), :], tg_ref.at[pl.ds(at, n), :], tg_sem)

            def fetch_target(lo, n, at):
                if at > 0:
                    tg_ref[pl.ds(0, at), :] = jnp.zeros((at, d), F32)
                if at + n < tm:
                    tg_ref[pl.ds(at + n, tm - at - n), :] = jnp.zeros((tm - at - n, d), F32)
                target_rows(lo, n, at).start()

            pl.when(k == 0)(lambda: _for_tile_rows(i, ni, tm, final[2], final[3], fetch_target))

        @pl.when(k == 0)
        def _():
            xhat, _ = _rms_stats(h_ref[...])
            n_scr[...] = (xhat * g_ref[...]).astype(BF16)
            acc[...] = jnp.zeros_like(acc)

        n = n_scr[...]
        a = _dot_nt(n, wg_ref[...])
        b = _dot_nt(n, wu_ref[...])
        a_ref[...] = a.astype(BF16)
        b_ref[...] = b.astype(BF16)
        s = (a * _sigmoid(a) * b).astype(BF16)
        acc[...] += _dot(s, wd_ref[...])

        if final is None:
            @pl.when(k == ns - 1)
            def _():
                ho_ref[...] = h_ref[...] + 0.5 * acc[...]
        else:
            n_meta, seq = final[2], final[3]

            @pl.when((i == 0) & (k == 0))
            def _():
                dgf_ref[...] = jnp.zeros_like(dgf_ref)
                loss_ref[...] = jnp.zeros_like(loss_ref)

            @pl.when(k == ns - 1)
            def _():
                _for_tile_rows(i, ni, tm, n_meta, seq, lambda lo, n, at: target_rows(lo, n, at).wait())
                h3 = h_ref[...] + 0.5 * acc[...]
                xhat, r = _rms_stats(h3)
                gf = gf_ref[...]
                row = i * tm + lax.broadcasted_iota(jnp.int32, (tm, d), 0)
                valid = (row >= n_meta) & (row < n_meta + seq)
                diff = jnp.where(valid, xhat * gf - tg_ref[...], 0.0)
                dout = diff * (1.0 / d)
                loss_ref[...] += jnp.full(loss_ref.shape, 0.5 * jnp.sum(diff * diff) * (1.0 / d), F32)
                dgf_ref[...] += jnp.sum(dout * xhat, axis=0, keepdims=True)
                dh3 = _rms_bwd(xhat, r, gf, dout)
                ho_ref[...] = dh3
                dy_ref[...] = (0.5 * dh3).astype(BF16)

    row_spec = pl.BlockSpec((tm, d), lambda i, k: (i, 0))
    vec_spec = pl.BlockSpec((1, d), lambda i, k: (0, 0))
    in_specs = [row_spec, vec_spec,
                pl.BlockSpec((None, f4, d), lambda i, k: (k, 0, 0)),
                pl.BlockSpec((None, f4, d), lambda i, k: (k, 0, 0)),
                pl.BlockSpec((None, f4, d), lambda i, k: (k, 0, 0))]
    act_spec = pl.BlockSpec((None, tm, f4), lambda i, k: (k, i, 0))
    out_specs = [row_spec, act_spec, act_spec, row_spec]
    out_shape = [jax.ShapeDtypeStruct((tp, d), F32),
                 jax.ShapeDtypeStruct((ns, tp, f4), BF16),
                 jax.ShapeDtypeStruct((ns, tp, f4), BF16),
                 jax.ShapeDtypeStruct((tp, d), BF16)]
    args = [h, g, wg, wu, wd]
    scratch = [pltpu.VMEM((tm, d), F32)]
    if final is not None:
        in_specs += [vec_spec, ANY]
        args += [final[0], final[1]]
        out_specs += [vec_spec, pl.BlockSpec((1, LANES), lambda i, k: (0, 0)), row_spec]
        out_shape += [jax.ShapeDtypeStruct((1, d), F32), jax.ShapeDtypeStruct((1, LANES), F32),
                      jax.ShapeDtypeStruct((tp, d), BF16)]
        scratch += [pltpu.VMEM((tm, d), F32), pltpu.SemaphoreType.DMA(())]
    split = _attach_carry(carry, in_specs, args, out_specs, out_shape, scratch)
    return pl.pallas_call(
        body, name=name, grid=(ni, ns), in_specs=in_specs, out_specs=out_specs, out_shape=out_shape,
        scratch_shapes=scratch, compiler_params=_params(("arbitrary", "arbitrary")),
        input_output_aliases=split.aliases,
    )(*args)


def _ffn_bwd_shard(k, ns, dn_prev, dy, n, a, b, wg, wu, wd, tail, name, carry=None, unpad=None):
    tp, d = n.shape
    f4 = wg.shape[1]
    tm = _pick_tile(tp, (768, 512, 256))
    ni = tp // tm
    first, last = k == 0, k == ns - 1
    unpad = unpad if last else None

    def body(*refs):
        refs, phases = split(refs)
        acc_in = None if first else refs.pop(0)
        if last:
            dh_ref, h_ref, g_ref = refs[:3]
        else:
            dy_ref, n_ref = refs[:2]
        refs = refs[3 if last else 2:]
        a_ref, b_ref, wg_hbm, wu_hbm, wd_hbm = refs[:5]
        refs = refs[5:]
        acc_out, dwg_hbm, dwu_hbm, dwd_hbm = refs[:4]
        rest = refs[4:]
        dg_ref = rest.pop(0) if last else None
        head_ref = rest.pop(0) if unpad else None
        wg_ref, wu_ref, wd_ref, dwg_ref, dwu_ref, dwd_ref, wsem = rest[:7]
        i = pl.program_id(0)
        _run_phases(phases, carry, i, ni)
        if unpad:
            res_ref, res_sem = rest[7:]

            def real_rows(lo, cnt, at):
                return pltpu.make_async_copy(res_ref.at[pl.ds(at, cnt), :], acc_out.at[pl.ds(lo, cnt), :], res_sem)

            def wait_tile(tile):
                _for_tile_rows(tile, ni, tm, *unpad, lambda lo, cnt, at: real_rows(lo, cnt, at).wait())

        @pl.when(i == 0)
        def _():
            loads = [pltpu.make_async_copy(src.at[k], dst, wsem.at[j])
                     for j, (src, dst) in enumerate(((wg_hbm, wg_ref), (wu_hbm, wu_ref), (wd_hbm, wd_ref)))]
            for cp in loads:
                cp.start()
            dwg_ref[...] = jnp.zeros_like(dwg_ref)
            dwu_ref[...] = jnp.zeros_like(dwu_ref)
            dwd_ref[...] = jnp.zeros_like(dwd_ref)
            if last:
                dg_ref[...] = jnp.zeros_like(dg_ref)
            for cp in loads:
                cp.wait()

        if last:
            xhat, r = _rms_stats(h_ref[...])
            n = (xhat * g_ref[...]).astype(BF16)
            dy = (0.5 * dh_ref[...]).astype(BF16)
        else:
            n = n_ref[...]
            dy = dy_ref[...]
        av = a_ref[...].astype(F32)
        bv = b_ref[...].astype(F32)
        sg = _sigmoid(av)
        silu = av * sg
        ds = _dot_nt(dy, wd_ref[...])
        da = (ds * bv * (sg * (1.0 + av * (1.0 - sg)))).astype(BF16)
        db = (ds * silu).astype(BF16)
        s = (silu * bv).astype(BF16)
        dwd_ref[...] += _dot_tn(s, dy)
        dwg_ref[...] += _dot_tn(da, n)
        dwu_ref[...] += _dot_tn(db, n)
        dn = _dot(da, wg_ref[...]) + _dot(db, wu_ref[...])
        if not first:
            dn = dn + acc_in[...]
        if last:
            dg_ref[...] += jnp.sum(dn * xhat, axis=0, keepdims=True)
            dh_in = dh_ref[...] + _rms_bwd(xhat, r, g_ref[...], dn)
            if unpad:
                pl.when(i > 0)(lambda: wait_tile(i - 1))
                res_ref[...] = dh_in

                @pl.when(i == 0)
                def _():
                    head_ref[...] = res_ref[pl.ds(0, unpad[0]), :]

                _for_tile_rows(i, ni, tm, *unpad, lambda lo, cnt, at: real_rows(lo, cnt, at).start())
                pl.when(i == ni - 1)(lambda: wait_tile(i))
            else:
                acc_out[...] = dh_in
        else:
            acc_out[...] = dn

        @pl.when(i == ni - 1)
        def _():
            stores = []
            for j, (acc_ref, stage_ref, out_hbm) in enumerate(((dwg_ref, wg_ref, dwg_hbm), (dwu_ref, wu_ref, dwu_hbm),
                                                              (dwd_ref, wd_ref, dwd_hbm))):
                stage_ref[...] = acc_ref[...].astype(BF16)
                stores.append(pltpu.make_async_copy(stage_ref, out_hbm, wsem.at[j]))
                stores[-1].start()
            for cp in stores:
                cp.wait()

    row_spec = pl.BlockSpec((tm, d), lambda i: (i, 0))
    vec_spec = pl.BlockSpec((1, d), lambda i: (0, 0))
    act_spec = pl.BlockSpec((None, tm, f4), lambda i: (k, i, 0))
    in_specs = [act_spec, act_spec, ANY, ANY, ANY]
    args = [a, b, wg, wu, wd]
    if last:
        in_specs = [row_spec, row_spec, vec_spec] + in_specs
        args = list(tail) + args
    else:
        in_specs = [row_spec, row_spec] + in_specs
        args = [dy, n] + args
    if not first:
        in_specs.insert(0, row_spec)
        args.insert(0, dn_prev)
    out_specs = [row_spec, ANY, ANY, ANY]
    out_shape = [jax.ShapeDtypeStruct((tp, d), F32)] + [jax.ShapeDtypeStruct((f4, d), BF16)] * 3
    scratch = [pltpu.VMEM((f4, d), BF16)] * 3 + [pltpu.VMEM((f4, d), F32)] * 3 + [pltpu.SemaphoreType.DMA((3,))]
    if last:
        out_specs.append(vec_spec)
        out_shape.append(jax.ShapeDtypeStruct((1, d), F32))
    if unpad:
        out_specs[0] = ANY
        out_shape[0] = jax.ShapeDtypeStruct((unpad[1], d), F32)
        out_specs.append(pl.BlockSpec((unpad[0], d), lambda i: (0, 0)))
        out_shape.append(jax.ShapeDtypeStruct((unpad[0], d), F32))
        scratch += [pltpu.VMEM((tm, d), F32), pltpu.SemaphoreType.DMA(())]
    n_host = len(out_shape)
    split = _attach_carry(carry, in_specs, args, out_specs, out_shape, scratch)
    outs = pl.pallas_call(
        body, name=f"{name}_{k}", grid=(ni,), in_specs=in_specs, out_specs=out_specs, out_shape=out_shape,
        scratch_shapes=scratch, compiler_params=_params(("arbitrary",)), input_output_aliases=split.aliases,
    )(*args)
    return outs[:n_host], outs[n_host:]


def _ffn_bwd(dh_out, dy, h_in, n, g, a, b, wg, wu, wd, name, chain=None, unpad=None):
    ns = wg.shape[0]
    acc, dwg, dwu, dwd = None, [], [], []
    for k in range(ns):
        carry = chain.carry() if chain is not None else None
        outs, carried = _ffn_bwd_shard(k, ns, acc, dy, n, a, b, wg, wu, wd, (dh_out, h_in, g), name, carry, unpad)
        if chain is not None:
            chain.feed(carried)
        acc = outs[0]
        dwg.append(outs[1])
        dwu.append(outs[2])
        dwd.append(outs[3])
    return (acc, dwg, dwu, dwd) + tuple(outs[4:])


def _win_fwd(h, g, w_in, carry=None):
    tp, d = h.shape
    ns = w_in.shape[0]
    tm = _pick_tile(tp, (768, 512, 256))
    ni = tp // tm

    def body(*refs):
        (h_ref, g_ref, w_ref, u_ref, p_ref), phases = split(refs)
        _run_phases(phases, carry, pl.program_id(0), ni)
        xhat, _ = _rms_stats(h_ref[...])
        u = (xhat * g_ref[...]).astype(BF16)
        u_ref[...] = u
        for k in range(ns):
            p_ref[k] = _dot(u, w_ref[k]).astype(BF16)

    in_specs = [pl.BlockSpec((tm, d), lambda i: (i, 0)),
                pl.BlockSpec((1, d), lambda i: (0, 0)),
                pl.BlockSpec((ns, d, d), lambda i: (0, 0, 0))]
    out_specs = [pl.BlockSpec((tm, d), lambda i: (i, 0)),
                 pl.BlockSpec((ns, tm, d), lambda i: (0, i, 0))]
    out_shape = [jax.ShapeDtypeStruct((tp, d), BF16), jax.ShapeDtypeStruct((ns, tp, d), BF16)]
    args, scratch = [h, g, w_in], []
    split = _attach_carry(carry, in_specs, args, out_specs, out_shape, scratch)
    return pl.pallas_call(
        body, name="win_fwd", grid=(ni,), in_specs=in_specs, out_specs=out_specs, out_shape=out_shape,
        scratch_shapes=scratch, compiler_params=_params(("arbitrary",)), input_output_aliases=split.aliases,
    )(*args)


def _win_bwd_shard(k, ns, du_prev, dpb, dug, u, w_in, h1, g, dh2):
    tp, d = h1.shape
    dh = d // 2
    tm = _pick_tile(tp, (768, 512, 256))
    first, last = k == 0, k == ns - 1

    def body(*refs):
        refs = list(refs)
        acc_in = None if first else refs.pop(0)
        dug_ref = refs.pop(0) if first else None
        dp_ref, u_ref, w_ref = refs[:3]
        refs = refs[3:]
        if last:
            h_ref, g_ref, dh2_ref, acc_out, dw_ref, dg_ref, dy_ref, dw_acc = refs
        else:
            acc_out, dw_ref, dw_acc = refs
        i = pl.program_id(0)

        @pl.when(i == 0)
        def _():
            dw_acc[...] = jnp.zeros_like(dw_acc)
            if last:
                dg_ref[...] = jnp.zeros_like(dg_ref)

        dp = dp_ref[...]
        if first:
            dp = jnp.concatenate([dug_ref[...], dp[:, dh:]], axis=1)
        dw_acc[...] += _dot_tn(u_ref[...], dp)
        du = _dot_nt(dp, w_ref[...])
        if not first:
            du = du + acc_in[...]
        if last:
            xhat, r = _rms_stats(h_ref[...])
            dg_ref[...] += jnp.sum(du * xhat, axis=0, keepdims=True)
            dh1 = dh2_ref[...] + _rms_bwd(xhat, r, g_ref[...], du)
            acc_out[...] = dh1
            dy_ref[...] = (0.5 * dh1).astype(BF16)
        else:
            acc_out[...] = du

        @pl.when(i == tp // tm - 1)
        def _():
            dw_ref[...] = dw_acc[...].astype(BF16)

    row_spec = pl.BlockSpec((tm, d), lambda i: (i, 0))
    vec_spec = pl.BlockSpec((1, d), lambda i: (0, 0))
    in_specs = [pl.BlockSpec((None, tm, d), lambda i: (k, i, 0)), row_spec,
                pl.BlockSpec((None, d, d), lambda i: (k, 0, 0))]
    args = [dpb, u, w_in]
    if first:
        in_specs.insert(0, pl.BlockSpec((tm, dh), lambda i: (i, 0)))
        args.insert(0, dug)
    else:
        in_specs.insert(0, row_spec)
        args.insert(0, du_prev)
    out_specs = [row_spec, pl.BlockSpec((d, d), lambda i: (0, 0))]
    out_shape = [jax.ShapeDtypeStruct((tp, d), F32), jax.ShapeDtypeStruct((d, d), BF16)]
    if last:
        in_specs += [row_spec, vec_spec, row_spec]
        args += [h1, g, dh2]
        out_specs += [vec_spec, row_spec]
        out_shape += [jax.ShapeDtypeStruct((1, d), F32), jax.ShapeDtypeStruct((tp, d), BF16)]
    return pl.pallas_call(
        body, name=f"win_bwd_{k}", grid=(tp // tm,), in_specs=in_specs, out_specs=out_specs,
        out_shape=out_shape, scratch_shapes=[pltpu.VMEM((d, d), F32)],
        compiler_params=_params(("arbitrary",)),
    )(*args)


def _win_bwd(dpb, dug, u, w_in, h1, g, dh2):
    ns = w_in.shape[0]
    acc, dws = None, []
    for k in range(ns):
        outs = _win_bwd_shard(k, ns, acc, dpb, dug, u, w_in, h1, g, dh2)
        acc = outs[0]
        dws.append(outs[1])
    return acc, dws, outs[2], outs[3]


def _cmul(ar, ai, br, bi):
    return ar * br - ai * bi, ar * bi + ai * br


def _scan_rows(j, sub):
    return pl.ds(j * SCAN_SEQS, SCAN_SEQS)


def _permute_rows(src_ref, dst_ref, sub):
    for j in range(sub):
        dst_ref[pl.ds(j * SCAN_SEQS, SCAN_SEQS), :] = src_ref[pl.ds(j, SCAN_SEQS, stride=sub), :]


def _unpermute_rows(src_ref, dst_ref, sub):
    for j in range(sub):
        dst_ref[pl.ds(j, SCAN_SEQS, stride=sub), :] = src_ref[pl.ds(j * SCAN_SEQS, SCAN_SEQS), :]


def _local_scan(x_ref, lr, li, w, sub, reverse):
    hr = jnp.zeros((SCAN_SEQS, w), F32)
    hi = jnp.zeros((SCAN_SEQS, w), F32)
    order = range(sub - 1, -1, -1) if reverse else range(sub)
    for j in order:
        xr = x_ref[_scan_rows(j, sub), pl.ds(0, w)]
        xi = x_ref[_scan_rows(j, sub), pl.ds(w, w)]
        if reverse:
            hr, hi = lr * hr + li * hi + xr, lr * hi - li * hr + xi
        else:
            hr, hi = lr * hr - li * hi + xr, lr * hi + li * hr + xi
        x_ref[_scan_rows(j, sub), pl.ds(0, w)] = hr
        x_ref[_scan_rows(j, sub), pl.ds(w, w)] = hi
    return hr, hi


def _entering_states(er, ei, fr, fi, pow_ref, w, sub, reverse):
    lane = lax.broadcasted_iota(jnp.int32, (SCAN_SEQS, w), 0)
    if reverse:
        edge, shift1 = SCAN_SEQS - 1, SCAN_SEQS - 1
    else:
        edge, shift1 = 0, 1
    zr = jnp.where(lane == edge, pltpu.roll(fr, shift1, 0), pltpu.roll(er, shift1, 0))
    zi = jnp.where(lane == edge, pltpu.roll(fi, shift1, 0), pltpu.roll(ei, shift1, 0))
    for m in range(SCAN_SEQS.bit_length() - 1):
        step, row = 1 << m, sub - 1 + m
        ar = pow_ref[pl.ds(row, 1), pl.ds(0, w)]
        ai = pow_ref[pl.ds(row, 1), pl.ds(w, w)]
        if reverse:
            ai = -ai
            keep = lane < SCAN_SEQS - step
            sr = jnp.where(keep, pltpu.roll(zr, SCAN_SEQS - step, 0), 0.0)
            si = jnp.where(keep, pltpu.roll(zi, SCAN_SEQS - step, 0), 0.0)
        else:
            keep = lane >= step
            sr = jnp.where(keep, pltpu.roll(zr, step, 0), 0.0)
            si = jnp.where(keep, pltpu.roll(zi, step, 0), 0.0)
        pr, pi = _cmul(ar, ai, sr, si)
        zr, zi = zr + pr, zi + pi
    ar = pow_ref[pl.ds(sub - 1, 1), pl.ds(0, w)]
    ai = pow_ref[pl.ds(sub - 1, 1), pl.ds(w, w)]
    if reverse:
        ai = -ai
    pr, pi = _cmul(ar, ai, zr, zi)
    return zr, zi, er + pr, ei + pi


def _scan_fwd(p, mb, mc, powt, dskip):
    _, tp, d = p.shape
    nb, cb, w2 = mb.shape
    w = w2 // 2
    q = SCAN_TILE
    sub = q // SCAN_SEQS
    nt = tp // q
    ds = d // 2

    def body(ug_ref, mb_ref, mc_ref, pow_ref, d_ref, y_ref, bnd_ref, x_scr, carry, nat, perm):
        t = pl.program_id(1)

        @pl.when(t == 0)
        def _():
            carry[...] = jnp.zeros_like(carry)

        ugf = ug_ref[...].astype(F32)
        nat[...] = ugf
        _permute_rows(nat, perm, sub)
        x_scr[...] = _dot(perm[...].astype(BF16), mb_ref[...])
        lr = jnp.broadcast_to(pow_ref[pl.ds(0, 1), pl.ds(0, w)], (SCAN_SEQS, w))
        li = jnp.broadcast_to(pow_ref[pl.ds(0, 1), pl.ds(w, w)], (SCAN_SEQS, w))
        er, ei = _local_scan(x_scr, lr, li, w, sub, False)
        zr, zi, fr, fi = _entering_states(er, ei, carry[:, pl.ds(0, w)], carry[:, pl.ds(w, w)],
                                          pow_ref, w, sub, False)
        carry[:, pl.ds(0, w)] = fr
        carry[:, pl.ds(w, w)] = fi
        bnd_ref[:, pl.ds(0, w)] = fr
        bnd_ref[:, pl.ds(w, w)] = fi
        for j in range(sub):
            pr = pow_ref[pl.ds(j, 1), pl.ds(0, w)]
            pi = pow_ref[pl.ds(j, 1), pl.ds(w, w)]
            cr, ci = _cmul(pr, pi, zr, zi)
            x_scr[_scan_rows(j, sub), pl.ds(0, w)] += cr
            x_scr[_scan_rows(j, sub), pl.ds(w, w)] += ci
        hb = x_scr[...].astype(BF16)
        perm[...] = _dot_nt(hb, mc_ref[...])
        _unpermute_rows(perm, nat, sub)
        y_ref[...] = nat[...] + d_ref[...] * ugf

    in_specs = [pl.BlockSpec((None, q, cb), lambda b, t: (0, t, b)),
                pl.BlockSpec((None, cb, w2), lambda b, t: (b, 0, 0)),
                pl.BlockSpec((None, cb, w2), lambda b, t: (b, 0, 0)),
                pl.BlockSpec((None, powt.shape[1], w2), lambda b, t: (b, 0, 0)),
                pl.BlockSpec((1, cb), lambda b, t: (0, b))]
    out_specs = [pl.BlockSpec((q, cb), lambda b, t: (t, b)),
                 pl.BlockSpec((None, None, SCAN_SEQS, w2), lambda b, t: (b, t, 0, 0))]
    out_shape = [jax.ShapeDtypeStruct((tp, ds), F32), jax.ShapeDtypeStruct((nb, nt, SCAN_SEQS, w2), F32)]
    scratch = [pltpu.VMEM((q, w2), F32), pltpu.VMEM((SCAN_SEQS, w2), F32),
               pltpu.VMEM((q, cb), F32), pltpu.VMEM((q, cb), F32)]
    return pl.pallas_call(
        body, name="s5_scan_fwd", grid=(nb, nt), in_specs=in_specs, out_specs=out_specs,
        out_shape=out_shape, scratch_shapes=scratch, compiler_params=_params(("arbitrary", "arbitrary")),
    )(p, mb, mc, powt, dskip)


def _scan_bwd(p, dy, mb, mc, powt, dskip, bnd):
    _, tp, d = p.shape
    nb, cb, w2 = mb.shape
    w = w2 // 2
    q = SCAN_TILE
    sub = q // SCAN_SEQS
    nt = tp // q
    ds = d // 2

    def body(ug_ref, dy_ref, mb_ref, mc_ref, pow_ref, d_ref, bnd_ref,
             dug_ref, dmb_ref, dmc_ref, dlam_ref, dd_ref, x_scr, y_scr, gcarry, nat, perm):
        t = pl.program_id(1)
        tt = nt - 1 - t

        @pl.when(t == 0)
        def _():
            gcarry[...] = jnp.zeros_like(gcarry)
            dmb_ref[...] = jnp.zeros_like(dmb_ref)
            dmc_ref[...] = jnp.zeros_like(dmc_ref)
            dlam_ref[...] = jnp.zeros_like(dlam_ref)
            dd_ref[...] = jnp.zeros_like(dd_ref)

        ugf = ug_ref[...].astype(F32)
        dyf = dy_ref[...].astype(F32)
        dd_ref[...] += jnp.sum((dyf * ugf).reshape(q // SUBLANES, SUBLANES, cb), axis=0)
        nat[...] = ugf
        _permute_rows(nat, perm, sub)
        ug = perm[...].astype(BF16)
        nat[...] = dyf
        _permute_rows(nat, perm, sub)
        dyb = perm[...].astype(BF16)
        lr = jnp.broadcast_to(pow_ref[pl.ds(0, 1), pl.ds(0, w)], (SCAN_SEQS, w))
        li = jnp.broadcast_to(pow_ref[pl.ds(0, 1), pl.ds(w, w)], (SCAN_SEQS, w))

        x_scr[...] = _dot(ug, mb_ref[...])
        er, ei = _local_scan(x_scr, lr, li, w, sub, False)
        first = tt == 0
        pfr = jnp.where(first, 0.0, bnd_ref[:, pl.ds(0, w)])
        pfi = jnp.where(first, 0.0, bnd_ref[:, pl.ds(w, w)])
        hzr, hzi, _, _ = _entering_states(er, ei, pfr, pfi, pow_ref, w, sub, False)
        for j in range(sub):
            pr = pow_ref[pl.ds(j, 1), pl.ds(0, w)]
            pi = pow_ref[pl.ds(j, 1), pl.ds(w, w)]
            cr, ci = _cmul(pr, pi, hzr, hzi)
            x_scr[_scan_rows(j, sub), pl.ds(0, w)] += cr
            x_scr[_scan_rows(j, sub), pl.ds(w, w)] += ci

        y_scr[...] = _dot(dyb, mc_ref[...])
        er, ei = _local_scan(y_scr, lr, li, w, sub, True)
        gzr, gzi, fr, fi = _entering_states(er, ei, gcarry[:, pl.ds(0, w)], gcarry[:, pl.ds(w, w)],
                                            pow_ref, w, sub, True)
        gcarry[:, pl.ds(0, w)] = fr
        gcarry[:, pl.ds(w, w)] = fi
        accr = jnp.zeros((SCAN_SEQS, w), F32)
        acci = jnp.zeros((SCAN_SEQS, w), F32)
        for j in range(sub):
            pr = pow_ref[pl.ds(sub - 1 - j, 1), pl.ds(0, w)]
            pi = pow_ref[pl.ds(sub - 1 - j, 1), pl.ds(w, w)]
            cr, ci = _cmul(pr, -pi, gzr, gzi)
            gr = y_scr[_scan_rows(j, sub), pl.ds(0, w)] + cr
            gi = y_scr[_scan_rows(j, sub), pl.ds(w, w)] + ci
            y_scr[_scan_rows(j, sub), pl.ds(0, w)] = gr
            y_scr[_scan_rows(j, sub), pl.ds(w, w)] = gi
            if j == 0:
                hpr, hpi = hzr, hzi
            else:
                hpr = x_scr[_scan_rows(j - 1, sub), pl.ds(0, w)]
                hpi = x_scr[_scan_rows(j - 1, sub), pl.ds(w, w)]
            accr += hpr * gr + hpi * gi
            acci += hpr * gi - hpi * gr
        dlam_ref[:, pl.ds(0, w)] += accr
        dlam_ref[:, pl.ds(w, w)] += acci

        hb = x_scr[...].astype(BF16)
        gb = y_scr[...].astype(BF16)
        dmc_ref[...] += _dot_tn(dyb, hb)
        dmb_ref[...] += _dot_tn(ug, gb)
        perm[...] = _dot_nt(gb, mb_ref[...])
        _unpermute_rows(perm, nat, sub)
        dug_ref[...] = (nat[...] + d_ref[...] * dyf).astype(BF16)

    blk = lambda b, t: (b, 0, 0)
    return pl.pallas_call(
        body, name="s5_scan_bwd", grid=(nb, nt),
        in_specs=[pl.BlockSpec((None, q, cb), lambda b, t: (0, nt - 1 - t, b)),
                  pl.BlockSpec((q, cb), lambda b, t: (nt - 1 - t, b)),
                  pl.BlockSpec((None, cb, w2), blk),
                  pl.BlockSpec((None, cb, w2), blk),
                  pl.BlockSpec((None, powt.shape[1], w2), blk),
                  pl.BlockSpec((1, cb), lambda b, t: (0, b)),
                  pl.BlockSpec((None, None, SCAN_SEQS, w2),
                               lambda b, t: (b, jnp.maximum(nt - 2 - t, 0), 0, 0))],
        out_specs=[pl.BlockSpec((q, cb), lambda b, t: (nt - 1 - t, b)),
                   pl.BlockSpec((None, cb, w2), blk),
                   pl.BlockSpec((None, cb, w2), blk),
                   pl.BlockSpec((None, SCAN_SEQS, w2), blk),
                   pl.BlockSpec((SUBLANES, cb), lambda b, t: (0, b))],
        out_shape=[jax.ShapeDtypeStruct((tp, ds), BF16),
                   jax.ShapeDtypeStruct((nb, cb, w2), F32),
                   jax.ShapeDtypeStruct((nb, cb, w2), F32),
                   jax.ShapeDtypeStruct((nb, SCAN_SEQS, w2), F32),
                   jax.ShapeDtypeStruct((SUBLANES, ds), F32)],
        scratch_shapes=[pltpu.VMEM((q, w2), F32), pltpu.VMEM((q, w2), F32),
                        pltpu.VMEM((SCAN_SEQS, w2), F32), pltpu.VMEM((q, cb), F32), pltpu.VMEM((q, cb), F32)],
        compiler_params=_params(("arbitrary", "arbitrary")),
    )(p, dy, mb, mc, powt, dskip, bnd)


HALO = 16


def _mix_tile(ys5, p0, p1, p2, p3, prev_cin, cw, bgate, wglu, wco, d):
    dh = d // 2
    tm = ys5.shape[0]
    v = p0[:, dh:].astype(F32)
    gbr = p1[:, :dh].astype(F32)
    gcr = p1[:, dh:].astype(F32)
    gact = _gelu(ys5).astype(BF16)
    z = _dot(gact, wglu)
    z1, z2 = z[:, :d], z[:, d:]
    sg = _sigmoid(z2)
    y_ssm = z1 * sg
    cin = gcr * v
    ext = jnp.concatenate([cin, prev_cin], axis=0)
    r1 = pltpu.roll(ext, 1, 0)[:tm]
    r2 = pltpu.roll(ext, 2, 0)[:tm]
    cv = cw[2] * cin + cw[1] * r1 + cw[0] * r2
    cg = (gbr * cv).astype(BF16)
    y_conv = _dot(cg, wco)
    g_s = _sigmoid(p2.astype(F32) + bgate[:, :d])
    g_c = _sigmoid(p3.astype(F32) + bgate[:, d:])
    mixed = g_s * y_ssm + g_c * y_conv
    return dict(v=v, gb=gbr, gc=gcr, gact=gact, z1=z1, sg=sg, y_ssm=y_ssm, cin=cin, r1=r1, r2=r2,
                cv=cv, cg=cg, y_conv=y_conv, g_s=g_s, g_c=g_c, mixed=mixed)


def _mix_fwd(h1, ys5, p, cw, bgate, wglu, wco, wo, carry=None):
    tp, d = h1.shape
    dh = d // 2
    tm = ROW_ALIGN
    ni = tp // tm

    def body(*refs):
        refs, phases = split(refs)
        (h_ref, y_ref, p0_ref, p1_ref, p2_ref, p3_ref, cw_ref, bg_ref, wglu_ref, wco_ref, wo_ref,
         o_ref, prev) = refs
        _run_phases(phases, carry, pl.program_id(0), ni)

        @pl.when(pl.program_id(0) == 0)
        def _():
            prev[...] = jnp.zeros_like(prev)

        cw = [cw_ref[pl.ds(t, 1), :] for t in range(3)]
        f = _mix_tile(y_ref[...], p0_ref[...], p1_ref[...], p2_ref[...], p3_ref[...], prev[...],
                      cw, bg_ref[...], wglu_ref[...], wco_ref[...], d)
        prev[...] = f["cin"][tm - HALO:, :]
        o_ref[...] = h_ref[...] + _dot(f["mixed"].astype(BF16), wo_ref[...])

    row = pl.BlockSpec((tm, d), lambda i: (i, 0))
    full = lambda a: pl.BlockSpec(a.shape, lambda i: (0,) * a.ndim)
    pk = lambda k: pl.BlockSpec((None, tm, d), lambda i, k=k: (k, i, 0))
    in_specs = [row, pl.BlockSpec((tm, dh), lambda i: (i, 0)), pk(0), pk(1), pk(2), pk(3),
                full(cw), full(bgate), full(wglu), full(wco), full(wo)]
    out_specs, out_shape = [row], [jax.ShapeDtypeStruct((tp, d), F32)]
    args, scratch = [h1, ys5, p, p, p, p, cw, bgate, wglu, wco, wo], [pltpu.VMEM((HALO, dh), F32)]
    split = _attach_carry(carry, in_specs, args, out_specs, out_shape, scratch)
    return pl.pallas_call(
        body, name="mix_fwd", grid=(ni,), in_specs=in_specs, out_specs=out_specs, out_shape=out_shape,
        scratch_shapes=scratch, compiler_params=_params(("arbitrary",)), input_output_aliases=split.aliases,
    )(*args)


def _mix_bwd(dh2, ys5, p, cw, bgate, wglu, wco, wo):
    tp, d = dh2.shape
    dh = d // 2
    tm = ROW_ALIGN
    ni = tp // tm
    hb = tm // HALO

    def body(dh_ref, y_ref, p0_ref, p1_ref, p2_ref, p3_ref, h0_ref, h1_ref,
             cw_ref, bg_ref, wglu_ref, wco_ref, wo_ref,
             dys_ref, dpb_ref, dwo_ref, dwglu_ref, dwco_ref, dcw_ref, dbg_ref, nxt):
        i = pl.program_id(0)
        tt = ni - 1 - i

        @pl.when(i == 0)
        def _():
            nxt[...] = jnp.zeros_like(nxt)
            dwo_ref[...] = jnp.zeros_like(dwo_ref)
            dwglu_ref[...] = jnp.zeros_like(dwglu_ref)
            dwco_ref[...] = jnp.zeros_like(dwco_ref)
            dcw_ref[...] = jnp.zeros_like(dcw_ref)
            dbg_ref[...] = jnp.zeros_like(dbg_ref)

        cw = [cw_ref[pl.ds(t, 1), :] for t in range(3)]
        prev_cin = h1_ref[:, dh:].astype(F32) * h0_ref[:, dh:].astype(F32)
        prev_cin = jnp.where(tt == 0, 0.0, prev_cin)
        ys5 = y_ref[...]
        f = _mix_tile(ys5, p0_ref[...], p1_ref[...], p2_ref[...], p3_ref[...], prev_cin,
                      cw, bg_ref[...], wglu_ref[...], wco_ref[...], d)
        dhb = dh_ref[...].astype(BF16)
        dmixed = _dot_nt(dhb, wo_ref[...])
        dwo_ref[...] += _dot_tn(f["mixed"].astype(BF16), dhb)

        g_s, g_c, sg = f["g_s"], f["g_c"], f["sg"]
        dy_ssm = dmixed * g_s
        dy_conv = dmixed * g_c
        dp2 = dmixed * f["y_ssm"] * g_s * (1.0 - g_s)
        dp3 = dmixed * f["y_conv"] * g_c * (1.0 - g_c)
        dbg_ref[:, pl.ds(0, d)] += jnp.sum(dp2, axis=0, keepdims=True)
        dbg_ref[:, pl.ds(d, d)] += jnp.sum(dp3, axis=0, keepdims=True)

        dz = jnp.concatenate([dy_ssm * sg, dy_ssm * f["z1"] * sg * (1.0 - sg)], axis=1).astype(BF16)
        dwglu_ref[...] += _dot_tn(f["gact"], dz)
        dys_ref[...] = (_dot_nt(dz, wglu_ref[...]) * _gelu_grad(ys5)).astype(BF16)

        dycb = dy_conv.astype(BF16)
        dwco_ref[...] += _dot_tn(f["cg"], dycb)
        dcg = _dot_nt(dycb, wco_ref[...])
        dgb = dcg * f["cv"]
        dcv = dcg * f["gb"]
        ext = jnp.concatenate([dcv, nxt[...]], axis=0)
        n1 = pltpu.roll(ext, tm + HALO - 1, 0)[:tm]
        n2 = pltpu.roll(ext, tm + HALO - 2, 0)[:tm]
        nxt[...] = dcv[:HALO, :]
        dcin = cw[2] * dcv + cw[1] * n1 + cw[0] * n2
        dcw_ref[pl.ds(0, 1), :] += jnp.sum(dcv * f["r2"], axis=0, keepdims=True)
        dcw_ref[pl.ds(1, 1), :] += jnp.sum(dcv * f["r1"], axis=0, keepdims=True)
        dcw_ref[pl.ds(2, 1), :] += jnp.sum(dcv * f["cin"], axis=0, keepdims=True)
        dgc = dcin * f["v"]
        dv = dcin * f["gc"]
        dpb_ref[0] = jnp.concatenate([jnp.zeros_like(dv), dv], axis=1).astype(BF16)
        dpb_ref[1] = jnp.concatenate([dgb, dgc], axis=1).astype(BF16)
        dpb_ref[2] = dp2.astype(BF16)
        dpb_ref[3] = dp3.astype(BF16)

    rev = lambda i: ni - 1 - i
    row = pl.BlockSpec((tm, d), lambda i: (rev(i), 0))
    half = pl.BlockSpec((tm, dh), lambda i: (rev(i), 0))
    full = lambda a: pl.BlockSpec(a.shape, lambda i: (0,) * a.ndim)
    pk = lambda k: pl.BlockSpec((None, tm, d), lambda i, k=k: (k, rev(i), 0))
    halo = lambda k: pl.BlockSpec((None, HALO, d), lambda i, k=k: (k, jnp.maximum(rev(i) * hb - 1, 0), 0))
    acc = lambda shape: pl.BlockSpec(shape, lambda i: (0,) * len(shape))
    return pl.pallas_call(
        body, name="mix_bwd", grid=(ni,),
        in_specs=[row, half, pk(0), pk(1), pk(2), pk(3), halo(0), halo(1),
                  full(cw), full(bgate), full(wglu), full(wco), full(wo)],
        out_specs=[half, pl.BlockSpec((4, tm, d), lambda i: (0, rev(i), 0)),
                   acc((d, d)), acc((dh, 2 * d)), acc((dh, d)), acc((SUBLANES, dh)), acc((1, 2 * d))],
        out_shape=[jax.ShapeDtypeStruct((tp, dh), BF16), jax.ShapeDtypeStruct((4, tp, d), BF16),
                   jax.ShapeDtypeStruct((d, d), F32), jax.ShapeDtypeStruct((dh, 2 * d), F32),
                   jax.ShapeDtypeStruct((dh, d), F32), jax.ShapeDtypeStruct((SUBLANES, dh), F32),
                   jax.ShapeDtypeStruct((1, 2 * d), F32)],
        scratch_shapes=[pltpu.VMEM((HALO, dh), F32)],
        compiler_params=_params(("arbitrary",)),
    )(dh2, ys5, p, p, p, p, p, p, cw, bgate, wglu, wco, wo)


ANY = pl.BlockSpec(memory_space=pl.ANY)


def _position():
    return lax.axis_index("x"), lax.axis_index("y"), lax.axis_index("c")


def _remote(src, dst, ssem, rsem, dev):
    return pltpu.make_async_remote_copy(src_ref=src, dst_ref=dst, send_sem=ssem, recv_sem=rsem,
                                        device_id=dev, device_id_type=MESH)


def _cast_pieces(ws, pos):
    n = len(ws)

    def body(pos_ref, *refs):
        for w_ref, o_ref in zip(refs[:n], refs[n:]):
            r4 = o_ref.shape[1]
            o_ref[0] = w_ref[pl.ds(0, r4), :].astype(BF16)
            o_ref[1] = w_ref[pl.ds(r4, r4), :].astype(BF16)

    halves = [(w.shape[0] // 2, w.shape[1]) for w in ws]
    return pl.pallas_call(
        body, name="cast_pieces",
        grid_spec=pltpu.PrefetchScalarGridSpec(
            num_scalar_prefetch=1, grid=(1,),
            in_specs=[pl.BlockSpec(hs, lambda i, pos: (pos[2], 0)) for hs in halves],
            out_specs=[pl.BlockSpec((None, None, None, 2, r2 // 2, cols),
                                    lambda i, pos: (pos[0], pos[1], pos[2], 0, 0, 0)) for r2, cols in halves]),
        out_shape=[jax.ShapeDtypeStruct((2, 2, 2, 2, r2 // 2, cols), BF16) for r2, cols in halves],
        compiler_params=_params(("arbitrary",)),
    )(pos, *ws)


class _Carry:
    def __init__(self, name, arrays, out_shapes, nsem, nlsem, make, fracs, n_inplace=0):
        self.name, self.arrays, self.out_shapes = name, list(arrays), list(out_shapes)
        self.nsem, self.nlsem, self.make, self.fracs = nsem, max(nlsem, 1), make, fracs
        self.n_inplace = n_inplace


def _carry_scratch(carry):
    return [pltpu.SemaphoreType.DMA((carry.nsem,)), pltpu.SemaphoreType.DMA((carry.nsem,)),
            pltpu.SemaphoreType.DMA((carry.nlsem,))]


def _run_carry(carry):
    na, no = len(carry.arrays), len(carry.out_shapes)

    def body(*refs):
        for phase in carry.make(refs[:na], refs[na:na + no], *refs[na + no:]):
            phase()

    return pl.pallas_call(
        body, name=carry.name, in_specs=[ANY] * na, out_specs=[ANY] * no, out_shape=carry.out_shapes,
        scratch_shapes=_carry_scratch(carry), input_output_aliases={i: i for i in range(carry.n_inplace)},
    )(*carry.arrays)


def _attach_carry(carry, in_specs, args, out_specs, out_shape, scratch):
    nhi, nho, nhs = len(in_specs), len(out_specs), len(scratch)
    if carry is None:
        none = lambda refs: (list(refs), [])
        none.aliases = {}
        return none
    na, no = len(carry.arrays), len(carry.out_shapes)
    in_specs += [ANY] * na
    args += carry.arrays
    out_specs += [ANY] * no
    out_shape += carry.out_shapes
    scratch += _carry_scratch(carry)

    def split(refs):
        refs = list(refs)
        o = nhi + na
        host = refs[:nhi] + refs[o:o + nho] + refs[o + nho + no:o + nho + no + nhs]
        sems = refs[o + nho + no + nhs:]
        return host, carry.make(refs[nhi:o], refs[o + nho:o + nho + no], *sems)

    split.aliases = {nhi + i: nho + i for i in range(carry.n_inplace)}
    split.carry_outs = lambda refs: list(refs)[nhi + na + nho:nhi + na + nho + no]
    return split


def _run_phases(phases, carry, step, total):
    for phase, frac in zip(phases, carry.fracs if carry is not None else ()):
        pl.when(step == int(round(frac * (total - 1))))(phase)


def _allgather_carry(name, walls, smalls):
    n, ns = len(walls), len(smalls)
    per = 14
    n_big = per * n

    def make(ins, outs, ssem, rsem, lsem):
        sin = ins[n:]
        wall, sall = outs[:n], outs[n:]
        x, y, c = _position()
        xnb, ynb, sib = (1 - x, y, c), (x, 1 - y, c), (x, y, 1 - c)
        chips = [(1 - x, y), (x, 1 - y), (1 - x, 1 - y)]
        slot = lambda i, xx, yy, cc, h: wall[i].at[xx, yy, cc, h]
        own = lambda i, h: slot(i, x, y, c, h)
        cp = lambda src, dst, s, dev: _remote(src, dst, ssem.at[s], rsem.at[s], dev)
        to_sib = lambda i, xx, yy, h: cp(slot(i, xx, yy, c, h), slot(i, xx, yy, c, h),
                                         per * i + 6 + 4 * xx + 2 * yy + h, sib)

        def local():
            return [pltpu.make_async_copy(sin[i], sall[i].at[2 * x + y], lsem.at[i]) for i in range(ns)]

        def small(px, py, j, i, landing):
            s = n_big + j * ns + i
            return cp(sin[i], sall[i].at[landing], s, (px, py, c))

        def first_hop():
            for lc in local():
                lc.start()
            for j, (px, py) in enumerate(chips):
                for i in range(ns):
                    small(px, py, j, i, 2 * x + y).start()
            for i in range(n):
                cp(own(i, 0), slot(i, x, y, c, 0), per * i, xnb).start()
                cp(own(i, 1), slot(i, x, y, c, 1), per * i + 1, ynb).start()
                for h in range(2):
                    cp(own(i, h), slot(i, x, y, c, h), per * i + 6 + 4 * x + 2 * y + h, sib).start()

        def second_hop():
            for lc in local():
                lc.wait()
            for i in range(n):
                cp(slot(i, 1 - x, y, c, 0), slot(i, 1 - x, y, c, 0), per * i, xnb).wait_recv()
                cp(slot(i, x, 1 - y, c, 1), slot(i, x, 1 - y, c, 1), per * i + 1, ynb).wait_recv()
                for j in range(2):
                    cp(slot(i, j, y, c, 0), slot(i, j, y, c, 0), per * i + 2 + j, ynb).start()
                    cp(slot(i, x, j, c, 1), slot(i, x, j, c, 1), per * i + 4 + j, xnb).start()
                to_sib(i, 1 - x, y, 0).start()
                to_sib(i, x, 1 - y, 1).start()

        def last_to_sibling():
            for i in range(n):
                for j in range(2):
                    cp(slot(i, j, 1 - y, c, 0), slot(i, j, 1 - y, c, 0), per * i + 2 + j, ynb).wait_recv()
                    cp(slot(i, 1 - x, j, c, 1), slot(i, 1 - x, j, c, 1), per * i + 4 + j, xnb).wait_recv()
                    to_sib(i, j, 1 - y, 0).start()
                    to_sib(i, 1 - x, j, 1).start()

        def finish():
            for i in range(n):
                for xx in range(2):
                    for yy in range(2):
                        for h in range(2):
                            s = per * i + 6 + 4 * xx + 2 * yy + h
                            cp(slot(i, xx, yy, 1 - c, h), slot(i, xx, yy, 1 - c, h), s, sib).wait_recv()
                            to_sib(i, xx, yy, h).wait_send()
                cp(own(i, 0), slot(i, x, y, c, 0), per * i, xnb).wait_send()
                cp(own(i, 1), slot(i, x, y, c, 1), per * i + 1, ynb).wait_send()
                for j in range(2):
                    cp(slot(i, j, y, c, 0), slot(i, j, y, c, 0), per * i + 2 + j, ynb).wait_send()
                    cp(slot(i, x, j, c, 1), slot(i, x, j, c, 1), per * i + 4 + j, xnb).wait_send()
            for j, (px, py) in enumerate(chips):
                for i in range(ns):
                    small(px, py, j, i, 2 * px + py).wait_recv()
                    small(px, py, j, i, 2 * x + y).wait_send()

        return [first_hop, second_hop, last_to_sibling, finish]

    out_shapes = [jax.ShapeDtypeStruct(a.shape, a.dtype) for a in walls]
    out_shapes += [jax.ShapeDtypeStruct((4,) + a.shape, a.dtype) for a in smalls]
    return _Carry(name, list(walls) + list(smalls), out_shapes, n_big + 3 * ns, ns, make, (0.0, 0.23, 0.73, 1.0),
                  n_inplace=n)


def _assemble_rows(x2d, n_meta, tp, carry, meta_at):
    seq, d = x2d.shape
    tm = ROW_ALIGN
    ni = tp // tm

    def body(*refs):
        (x_hbm, o_ref, buf, sem), phases = split(refs)
        shards = split.carry_outs(refs)[meta_at]
        s = pl.program_id(0)
        i = ni - 1 - s
        _run_phases(phases, carry, s, ni)

        def fetch(lo, n, at):
            if at > 0:
                buf[pl.ds(0, at), :] = jnp.zeros((at, d), F32)
            if at + n < tm:
                buf[pl.ds(at + n, tm - at - n), :] = jnp.zeros((tm - at - n, d), F32)
            cp = pltpu.make_async_copy(x_hbm.at[pl.ds(lo, n), :], buf.at[pl.ds(at, n), :], sem)
            cp.start()
            cp.wait()

        _for_tile_rows(i, ni, tm, n_meta, seq, fetch)

        @pl.when(i == 0)
        def _():
            mc = d // 4
            for k in range(4):
                cp = pltpu.make_async_copy(shards.at[k], buf.at[pl.ds(0, n_meta), pl.ds(k * mc, mc)], sem)
                cp.start()
                cp.wait()

        o_ref[...] = buf[...]

    in_specs, args = [ANY], [x2d]
    out_specs = [pl.BlockSpec((tm, d), lambda s: (ni - 1 - s, 0))]
    out_shape = [jax.ShapeDtypeStruct((tp, d), F32)]
    scratch = [pltpu.VMEM((tm, d), F32), pltpu.SemaphoreType.DMA(())]
    split = _attach_carry(carry, in_specs, args, out_specs, out_shape, scratch)
    return pl.pallas_call(
        body, name="assemble_rows", grid=(ni,), in_specs=in_specs, out_specs=out_specs, out_shape=out_shape,
        scratch_shapes=scratch, compiler_params=_params(("arbitrary",)), input_output_aliases=split.aliases,
    )(*args)


def _exchange_carry(name, arrays, out_shapes, plan):
    count = plan([None] * len(arrays), [None] * len(out_shapes), None)

    def make(ins, outs, ssem, rsem, lsem):
        def copies():
            return [_remote(src, dst, ssem.at[j], rsem.at[j], peer)
                    for j, (src, dst, peer) in enumerate(plan(ins, outs, _position()))]

        def start():
            for c in copies():
                c.start()

        def wait():
            for c in copies():
                c.wait()

        return [start, wait]

    return _Carry(name, arrays, out_shapes, count, 0, make, (0.0, 1.0))


class _Grad:
    def __init__(self, arrs, kind, shard_shape):
        self.arrs, self.kind = list(arrs), kind
        self.rows, self.cols = shard_shape
        self.r2 = self.rows // 2

    def view(self, refs, k, h):
        r2 = self.r2
        if self.kind == "list":
            return refs[k].at[pl.ds(h * r2, r2), :]
        if self.kind == "stacked":
            return refs[0].at[k, pl.ds(h * r2, r2), :]
        if self.kind == "col":
            return refs[0].at[pl.ds(h * r2, r2), pl.ds(k * self.cols, self.cols)]
        return refs[0].at[pl.ds((2 * k + h) * r2, r2), :]

    def half_specs(self):
        r2, cols = self.r2, self.cols
        if self.kind == "list":
            return [pl.BlockSpec((r2, cols), lambda k, pos: (pos[2], 0))] * len(self.arrs)
        if self.kind == "stacked":
            return [pl.BlockSpec((None, r2, cols), lambda k, pos: (k, pos[2], 0))]
        if self.kind == "col":
            return [pl.BlockSpec((r2, cols), lambda k, pos: (pos[2], k))]
        return [pl.BlockSpec((r2, cols), lambda k, pos: (2 * k + pos[2], 0))]

    def step_bytes(self):
        return self.r2 * self.cols * (len(self.arrs) * self.arrs[0].dtype.itemsize + self.arrs[0].dtype.itemsize + 6)


def _add_halves(grads, recvs, pos):
    n = len(grads)
    counts = [len(g.arrs) for g in grads]
    n_mine = sum(counts)

    def body(pos_ref, *refs):
        o = 0
        for i in range(n):
            m_refs = refs[o:o + counts[i]]
            o += counts[i]
            r_ref, of_ref, ob_ref = refs[n_mine + i], refs[n_mine + n + i], refs[n_mine + 2 * n + i]
            mine = m_refs[0][...]
            for kk in range(1, counts[i]):
                mine = jnp.where(pl.program_id(0) == kk, m_refs[kk][...], mine)
            s = mine.astype(F32) + r_ref[...].astype(F32)
            of_ref[...] = s
            ob_ref[...] = s.astype(BF16)

    blks = [pl.BlockSpec((None, g.r2, g.cols), lambda k, pos: (k, 0, 0)) for g in grads]
    outs = pl.pallas_call(
        body, name="rs_add_c",
        grid_spec=pltpu.PrefetchScalarGridSpec(
            num_scalar_prefetch=1, grid=(4,),
            in_specs=[spec for g in grads for spec in g.half_specs()] + blks, out_specs=blks + blks),
        out_shape=[jax.ShapeDtypeStruct((4, g.r2, g.cols), F32) for g in grads]
        + [jax.ShapeDtypeStruct((4, g.r2, g.cols), BF16) for g in grads],
        compiler_params=_params(("arbitrary",)),
    )(pos, *[a for g in grads for a in g.arrs], *recvs)
    return list(zip(outs[:n], outs[n:]))


def _row_tile(rows, cols):
    fits = [t for t in range(16, rows + 1, 16) if rows % t == 0 and t * cols * 4 <= 2 * 1024 * 1024]
    return max(fits) if fits else rows


def _adamw_math(w, g, m, v):
    m = ADAM_B1 * m + (1.0 - ADAM_B1) * g
    v = ADAM_B2 * v + (1.0 - ADAM_B2) * (g * g)
    m_hat = m / (1.0 - ADAM_B1 ** ADAM_STEP)
    v_hat = v / (1.0 - ADAM_B2 ** ADAM_STEP)
    delta = -ADAM_LR * (m_hat / (jnp.sqrt(v_hat) + ADAM_EPS) + ADAM_WD * w)
    return delta, m, v


def _adamw_big(w, m, v, own, sib, pos):
    rows, cols = w.shape
    r2 = rows // 2

    tr = _row_tile(r2, cols)
    nt = r2 // tr

    def body(pos_ref, w_ref, m_ref, v_ref, own_ref, sib_ref, g_ref, d_ref, nm_ref, nv_ref):
        h = pl.program_id(0)
        g = jnp.where(h == pos_ref[2], own_ref[...], sib_ref[...])
        g_ref[...] = g
        d_ref[...], nm_ref[...], nv_ref[...] = _adamw_math(w_ref[...], g, m_ref[...], v_ref[...])

    half = pl.BlockSpec((tr, cols), lambda h, t, pos: (h * nt + t, 0))
    piece = pl.BlockSpec((tr, cols), lambda h, t, pos: (t, 0))
    out = jax.ShapeDtypeStruct((rows, cols), F32)
    return pl.pallas_call(
        body, name="adamw",
        grid_spec=pltpu.PrefetchScalarGridSpec(
            num_scalar_prefetch=1, grid=(2, nt),
            in_specs=[half, half, half, piece, piece],
            out_specs=[half, half, half, half]),
        out_shape=[out, out, out, out],
        compiler_params=_params(("arbitrary", "arbitrary")),
    )(pos, w, m, v, own, sib)


def _add_hop1(s1fs, recvs, pos):
    n = len(s1fs)
    s1vs = [s.reshape((4, 2) + r.shape[2:]) for s, r in zip(s1fs, recvs)]

    def body(pos_ref, *refs):
        for m_ref, r_ref, of_ref, ob_ref in zip(refs[:n], refs[n:2 * n], refs[2 * n:3 * n], refs[3 * n:]):
            s = m_ref[...] + r_ref[...].astype(F32)
            of_ref[...] = s
            ob_ref[...] = s.astype(BF16)

    def mine(h, j, pos):
        return (jnp.where(h == 0, 2 * j + pos[1], 2 * pos[0] + j), h, 0, 0)

    tile = lambda r: (None, None) + r.shape[2:]
    blks = [pl.BlockSpec(tile(r), lambda h, j, pos: (h, j, 0, 0)) for r in recvs]
    outs = pl.pallas_call(
        body, name="rs_add_1",
        grid_spec=pltpu.PrefetchScalarGridSpec(
            num_scalar_prefetch=1, grid=(2, 2),
            in_specs=[pl.BlockSpec(tile(r), mine) for r in recvs] + blks, out_specs=blks + blks),
        out_shape=[jax.ShapeDtypeStruct(r.shape, F32) for r in recvs]
        + [jax.ShapeDtypeStruct(r.shape, BF16) for r in recvs],
        compiler_params=_params(("arbitrary", "arbitrary")),
    )(pos, *s1vs, *recvs)
    return list(zip(outs[:n], outs[n:]))


def _own_sum(s2fs, recvs, pos):
    n = len(s2fs)

    def body(pos_ref, *refs):
        for s_ref, r_ref, o_ref in zip(refs[:n], refs[n:2 * n], refs[2 * n:]):
            o_ref[...] = s_ref[...] + r_ref[...].astype(F32)

    blks = [pl.BlockSpec((None,) + r.shape[1:], lambda h, pos: (h, 0, 0)) for r in recvs]
    return pl.pallas_call(
        body, name="own_sum",
        grid_spec=pltpu.PrefetchScalarGridSpec(
            num_scalar_prefetch=1, grid=(2,),
            in_specs=[pl.BlockSpec((None, None) + r.shape[1:],
                                   lambda h, pos: (h, jnp.where(h == 0, pos[0], pos[1]), 0, 0)) for r in recvs]
            + blks, out_specs=blks),
        out_shape=[jax.ShapeDtypeStruct(r.shape, F32) for r in recvs],
        compiler_params=_params(("arbitrary",)),
    )(pos, *s2fs, *recvs)


def _add_small(a, b):
    def body(a_ref, b_ref, o_ref):
        o_ref[...] = a_ref[...] + b_ref[...]

    vm = pl.BlockSpec(memory_space=pltpu.VMEM)
    return pl.pallas_call(body, name="add_small", in_specs=[vm, vm], out_specs=vm,
                          out_shape=jax.ShapeDtypeStruct(a.shape, F32))(a, b)


def _adamw_small(ws, gs, ms, vs):
    n = len(ws)

    def body(*refs):
        for i in range(n):
            w_ref, g_ref, m_ref, v_ref, d_ref, nm_ref, nv_ref = (refs[j * n + i] for j in range(7))
            d_ref[...], nm_ref[...], nv_ref[...] = _adamw_math(w_ref[...], g_ref[...], m_ref[...], v_ref[...])

    vm = pl.BlockSpec(memory_space=pltpu.VMEM)
    outs = pl.pallas_call(body, name="adamw_small", in_specs=[vm] * (4 * n), out_specs=[vm] * (3 * n),
                          out_shape=[jax.ShapeDtypeStruct(w.shape, F32) for w in ws] * 3)(*ws, *gs, *ms, *vs)
    return outs[:n], outs[n:2 * n], outs[2 * n:]


class _ReduceScatter:
    def __init__(self, tag, grads, pos, extra=None):
        self.tag, self.grads, self.pos, self.stage, self.extra = tag, grads, pos, 0, extra

    def carry(self):
        grads, n = self.grads, len(self.grads)
        r4 = [g.r2 // 2 for g in grads]

        first = [sum(len(g.arrs) for g in grads[:i]) for i in range(n)]

        def plan_c(ins, outs, p):
            if p is None:
                return 4 * n
            x, y, c = p
            mine = lambda i: ins[first[i]:first[i] + len(grads[i].arrs)]
            return [(grads[i].view(mine(i), k, 1 - c), outs[i].at[k], (x, y, 1 - c))
                    for i in range(n) for k in range(4)]

        def plan_1(ins, outs, p):
            if p is None:
                return 4 * n
            x, y, c = p
            copies = []
            for i in range(n):
                for j in range(2):
                    copies.append((ins[i].at[2 * j + (1 - y), pl.ds(0, r4[i]), :], outs[i].at[0, j],
                                   (x, 1 - y, c)))
                    copies.append((ins[i].at[2 * (1 - x) + j, pl.ds(r4[i], r4[i]), :], outs[i].at[1, j],
                                   (1 - x, y, c)))
            return copies

        def plan_2(ins, outs, p):
            if p is None:
                return 2 * n
            x, y, c = p
            copies = []
            for i in range(n):
                copies.append((ins[i].at[0, 1 - x], outs[i].at[0], (1 - x, y, c)))
                copies.append((ins[i].at[1, 1 - y], outs[i].at[1], (x, 1 - y, c)))
            return copies

        def plan_s(ins, outs, p):
            if p is None:
                return n
            x, y, c = p
            return [(ins[i], outs[i], (x, y, 1 - c)) for i in range(n)]

        shape = lambda lead, dt: [jax.ShapeDtypeStruct(lead(g) + (g.cols,), dt) for g in grads]
        stage = self.stage
        if stage == 0:
            name, arrays, plan = "exchange_c", [a for g in grads for a in g.arrs], plan_c
            shapes = [jax.ShapeDtypeStruct((4, g.r2, g.cols), g.arrs[0].dtype) for g in grads]
        elif stage == 1:
            name, arrays, plan = "exchange_1", [s[1] for s in self.s1], plan_1
            shapes = shape(lambda g: (2, 2, g.r2 // 2), BF16)
        elif stage == 2:
            name, arrays, plan = "exchange_2", [s[1] for s in self.s2], plan_2
            shapes = shape(lambda g: (2, g.r2 // 2), BF16)
        else:
            name, arrays, plan, shapes = "exchange_sibling", self.own, plan_s, shape(lambda g: (g.r2,), F32)
        if self.extra is not None and stage < 3:
            def with_extra(ins, outs, p, plan=plan):
                if p is None:
                    return plan(ins[:-1], outs[:-1], None) + 1
                x, y, c = p
                peer = [(x, y, 1 - c), (x, 1 - y, c), (1 - x, y, c)][stage]
                return plan(ins[:-1], outs[:-1], p) + [(ins[-1], outs[-1], peer)]

            arrays = arrays + [self.extra]
            shapes = shapes + [jax.ShapeDtypeStruct(self.extra.shape, F32)]
            plan = with_extra
        return _exchange_carry(f"rs_{self.tag}_{name}", arrays, shapes, plan)

    def feed(self, recv):
        grads, pos = self.grads, self.pos
        recv = list(recv)
        if self.extra is not None and self.stage < 3:
            self.extra = _add_small(self.extra, recv.pop())
        if self.stage == 0:
            self.s1, start = [], 0
            while start < len(grads):
                end, size = start, 0
                while end < len(grads) and (end == start or size + grads[end].step_bytes() <= ADD_GROUP_BYTES):
                    size += grads[end].step_bytes()
                    end += 1
                self.s1 += _add_halves(grads[start:end], recv[start:end], pos)
                start = end
        elif self.stage == 1:
            self.s2 = _add_hop1([s[0] for s in self.s1], list(recv), pos)
        elif self.stage == 2:
            own = _own_sum([s[0] for s in self.s2], list(recv), pos)
            self.own = [o.reshape(g.r2, g.cols) for g, o in zip(grads, own)]
        else:
            self.sib = list(recv)
        self.stage += 1

    def run(self):
        while self.stage < 4:
            self.feed(_run_carry(self.carry()))

    def adamw(self, weights):
        return [_adamw_big(w, m, v, o, sb, self.pos) for (w, m, v), o, sb in zip(weights, self.own, self.sib)]


def _block_diag(t, nb):
    g, c, p = t.shape
    gb = g // nb
    t = t.reshape(nb, gb, c, p)
    eye = jnp.eye(gb, dtype=t.dtype)
    return jnp.einsum("bgcp,gh->bgchp", t, eye).reshape(nb, gb * c, gb * p)


def _s5_discretise(a_re, a_im, log_dt, b_re, b_im, c_re, c_im):
    g, p = a_re.shape
    nb = g // GROUPS_PER_BLOCK
    dt = jnp.exp(log_dt)[:, None]
    mag = jnp.exp(a_re * dt)
    lam_re = mag * jnp.cos(a_im * dt)
    lam_im = mag * jnp.sin(a_im * dt)
    den = a_re * a_re + a_im * a_im
    q_re = ((lam_re - 1.0) * a_re + lam_im * a_im) / den
    q_im = (lam_im * a_re - (lam_re - 1.0) * a_im) / den
    bb_re = q_re[..., None] * b_re - q_im[..., None] * b_im
    bb_im = q_re[..., None] * b_im + q_im[..., None] * b_re
    tr = lambda t: jnp.swapaxes(t, 1, 2)
    mb = jnp.concatenate([_block_diag(tr(bb_re), nb), _block_diag(tr(bb_im), nb)], axis=-1)
    mc = jnp.concatenate([_block_diag(c_re, nb), -_block_diag(c_im, nb)], axis=-1)
    lam = jnp.concatenate([lam_re.reshape(nb, -1), lam_im.reshape(nb, -1)], axis=-1)
    return mb, mc, lam


def _s5_powers(a_re, a_im, log_dt, sub):
    g, p = a_re.shape
    nb = g // GROUPS_PER_BLOCK
    dt = jnp.exp(log_dt)[:, None]
    ns = list(range(1, sub + 1)) + [sub << m for m in range(1, SCAN_SEQS.bit_length() - 1)]
    ns += [0] * (-len(ns) % SUBLANES)
    e = jnp.asarray(ns, F32)[:, None, None]
    mag = jnp.exp(a_re[None] * dt[None] * e)
    ang = a_im[None] * dt[None] * e
    re = (mag * jnp.cos(ang)).reshape(len(ns), nb, -1)
    im = (mag * jnp.sin(ang)).reshape(len(ns), nb, -1)
    return jnp.transpose(jnp.concatenate([re, im], axis=-1), (1, 0, 2))


def _pack(parts):
    flat = jnp.concatenate([a.reshape(-1).astype(F32) for a in parts])
    n = flat.shape[0]
    pad = -n % (SUBLANES * LANES)
    return jnp.pad(flat, (0, pad)).reshape(-1, LANES)


def _unpack(buf, like):
    flat = buf.reshape(-1)
    out, o = [], 0
    for a in like:
        out.append(flat[o:o + a.size].reshape(a.shape))
        o += a.size
    return out


def kernel(x, meta_tokens, g_ffn1, ffn1_w_gate, ffn1_w_up, ffn1_w_down, g_mix, w_in, b_gate, ssm_a_re, ssm_a_im, ssm_log_dt, ssm_b_re, ssm_b_im, ssm_c_re, ssm_c_im, ssm_d, ssm_w_glu, conv_w, conv_w_out, w_o, g_ffn2, ffn2_w_gate, ffn2_w_up, ffn2_w_down, g_final, loss_target, m_meta_tokens, m_g_ffn1, m_ffn1_w_gate, m_ffn1_w_up, m_ffn1_w_down, m_g_mix, m_w_in, m_b_gate, m_ssm_a_re, m_ssm_a_im, m_ssm_log_dt, m_ssm_b_re, m_ssm_b_im, m_ssm_c_re, m_ssm_c_im, m_ssm_d, m_ssm_w_glu, m_conv_w, m_conv_w_out, m_w_o, m_g_ffn2, m_ffn2_w_gate, m_ffn2_w_up, m_ffn2_w_down, m_g_final, v_meta_tokens, v_g_ffn1, v_ffn1_w_gate, v_ffn1_w_up, v_ffn1_w_down, v_g_mix, v_w_in, v_b_gate, v_ssm_a_re, v_ssm_a_im, v_ssm_log_dt, v_ssm_b_re, v_ssm_b_im, v_ssm_c_re, v_ssm_c_im, v_ssm_d, v_ssm_w_glu, v_conv_w, v_conv_w_out, v_w_o, v_g_ffn2, v_ffn2_w_gate, v_ffn2_w_up, v_ffn2_w_down, v_g_final):
    seq, d = x.shape[1], x.shape[2]
    n_meta = meta_tokens.shape[0]
    dh = d // 2
    tp = -(-(n_meta + seq) // ROW_ALIGN) * ROW_ALIGN
    mx, my, mc_ = _position()
    pos = jnp.stack([mx, my, mc_]).astype(jnp.int32)
    shard = 2 * mx + my

    big_names = ["ffn1_w_gate", "ffn1_w_up", "ffn1_w_down", "w_in", "ssm_w_glu", "conv_w_out", "w_o",
                 "ffn2_w_gate", "ffn2_w_up", "ffn2_w_down"]
    transposed = {0, 1, 7, 8}
    drop = lambda arrs: [jnp.swapaxes(a.reshape(a.shape[1:]), 0, 1) if i in transposed else a.reshape(a.shape[1:])
                         for i, a in enumerate(arrs)]
    big_w = drop([ffn1_w_gate, ffn1_w_up, ffn1_w_down, w_in, ssm_w_glu, conv_w_out, w_o,
                  ffn2_w_gate, ffn2_w_up, ffn2_w_down])
    big_m = drop([m_ffn1_w_gate, m_ffn1_w_up, m_ffn1_w_down, m_w_in, m_ssm_w_glu, m_conv_w_out,
                  m_w_o, m_ffn2_w_gate, m_ffn2_w_up, m_ffn2_w_down])
    big_v = drop([v_ffn1_w_gate, v_ffn1_w_up, v_ffn1_w_down, v_w_in, v_ssm_w_glu, v_conv_w_out,
                  v_w_o, v_ffn2_w_gate, v_ffn2_w_up, v_ffn2_w_down])
    pieces = _cast_pieces(big_w[:3], pos) + _cast_pieces(big_w[3:], pos)
    conv_local = conv_w.reshape(conv_w.shape[1], conv_w.shape[3])
    n_first = 3
    h0, *first = _assemble_rows(
        x.reshape(seq, d), n_meta, tp,
        _allgather_carry("allgather_first", pieces[:n_first], [meta_tokens, conv_local]), meta_at=n_first)
    smalls = first[n_first:]
    stack4 = lambda wl: wl.reshape((4, -1, wl.shape[-1]))
    w1g, w1u, w1d = [stack4(wl) for wl in first[:n_first]]
    natural_cols = lambda s: jnp.transpose(s, (1, 0, 2)).reshape(s.shape[1], 4 * s.shape[2])
    cw_full = natural_cols(smalls[1])
    cw_pad = jnp.pad(cw_full, ((0, SUBLANES - cw_full.shape[0]), (0, 0)))

    s5_args = (ssm_a_re[0], ssm_a_im[0], ssm_log_dt[0], ssm_b_re[0], ssm_b_im[0], ssm_c_re[0], ssm_c_im[0])
    (mb, mc, _), disc_vjp = jax.vjp(_s5_discretise, *s5_args)
    powt = _s5_powers(ssm_a_re[0], ssm_a_im[0], ssm_log_dt[0], SCAN_TILE // SCAN_SEQS)
    mb16, mc16 = mb.astype(BF16), mc.astype(BF16)

    h1, a1, b1, n1, *mid = _ffn_fwd(h0, g_ffn1, w1g, w1u, w1d, "ffn1_fwd",
                                    carry=_allgather_carry("allgather_mixer", pieces[3:7], []))
    win_all, wglu_s, wco_s, wo_s = [stack4(wl) for wl in mid]
    wglu_all = natural_cols(wglu_s)
    wco_all = natural_cols(wco_s)
    wo_all = wo_s.reshape(d, d)
    u, p, w2g, w2u = _win_fwd(h1, g_mix, win_all, carry=_allgather_carry("allgather_ffn2_in", pieces[7:9], []))
    ys5, bnd = _scan_fwd(p, mb16, mc16, powt, ssm_d)
    h2, w2d = _mix_fwd(h1, ys5, p, cw_pad, b_gate, wglu_all, wco_all, wo_all,
                       carry=_allgather_carry("allgather_ffn2_out", pieces[9:], []))
    w2g, w2u, w2d = stack4(w2g), stack4(w2u), stack4(w2d)
    dh3, a2, b2, n2, dg_final, loss_part, dy3 = _ffn_fwd(
        h2, g_ffn2, w2g, w2u, w2d, "ffn2_fwd_loss",
        final=(g_final.reshape(1, d), loss_target.reshape(seq, d), n_meta, seq))

    dh2, dw2g, dw2u, dw2d, dg_ffn2 = _ffn_bwd(dh3, dy3, h2, n2, g_ffn2, a2, b2, w2g, w2u, w2d, "ffn2_bwd")
    dys5, dpb, dwo, dwglu, dwco, dcw, dbg = _mix_bwd(dh2, ys5, p, cw_pad, b_gate, wglu_all, wco_all, wo_all)
    dug, dmb, dmc, dlam, dd = _scan_bwd(p, dys5, mb16, mc16, powt, ssm_d, bnd)
    dh1, dwin, dg_mix, dy1 = _win_bwd(dpb, dug, u, win_all, h1, g_mix, dh2)
    shapes = [w.shape for w in big_w]
    kinds = ["list", "list", "list", "list", "col", "col", "row", "list", "list", "list"]
    rest_grads = [dwin, [dwglu], [dwco], [dwo], dw2g, dw2u, dw2d]
    rs_rest = _ReduceScatter("rest", [_Grad(a, k, s) for a, k, s in
                                      zip(rest_grads, kinds[n_first:], shapes[n_first:])], pos)
    grad_x, dw1g, dw1u, dw1d, dg_ffn1, grad_meta = _ffn_bwd(
        dh1, dy1, h0, n1, g_ffn1, a1, b1, w1g, w1u, w1d, "ffn1_bwd", chain=rs_rest, unpad=(n_meta, seq))
    s5_grads = disc_vjp((dmb, dmc, jnp.sum(dlam, axis=1)))
    local_small = [dg_ffn1, dg_mix, dbg, *s5_grads, jnp.sum(dd, axis=0), dg_ffn2, dg_final,
                   grad_meta, dcw[:conv_w.shape[1]]]
    rs_first = _ReduceScatter("first", [_Grad(a, k, s) for a, k, s in
                                        zip([dw1g, dw1u, dw1d], kinds[:n_first], shapes[:n_first])], pos,
                              extra=_pack(local_small))
    rs_first.run()
    wmv = list(zip(big_w, big_m, big_v))
    big_out = rs_first.adamw(wmv[:n_first]) + rs_rest.adamw(wmv[n_first:])
    def lead(i, o):
        o = jnp.swapaxes(o, 0, 1) if i in transposed else o
        return o.reshape((1,) + o.shape)

    big_out = {nme: tuple(lead(i, o) for o in outs) for i, (nme, outs) in enumerate(zip(big_names, big_out))}

    grad_x = grad_x.reshape(1, seq, d)

    small_names = ["g_ffn1", "g_mix", "b_gate", "ssm_a_re", "ssm_a_im", "ssm_log_dt", "ssm_b_re", "ssm_b_im",
                   "ssm_c_re", "ssm_c_im", "ssm_d", "g_ffn2", "g_final", "meta_tokens", "conv_w"]
    small_w = [g_ffn1, g_mix, b_gate, ssm_a_re, ssm_a_im, ssm_log_dt, ssm_b_re, ssm_b_im, ssm_c_re, ssm_c_im,
               ssm_d, g_ffn2, g_final, meta_tokens, conv_w]
    small_m = [m_g_ffn1, m_g_mix, m_b_gate, m_ssm_a_re, m_ssm_a_im, m_ssm_log_dt, m_ssm_b_re, m_ssm_b_im,
               m_ssm_c_re, m_ssm_c_im, m_ssm_d, m_g_ffn2, m_g_final, m_meta_tokens, m_conv_w]
    small_v = [v_g_ffn1, v_g_mix, v_b_gate, v_ssm_a_re, v_ssm_a_im, v_ssm_log_dt, v_ssm_b_re, v_ssm_b_im,
               v_ssm_c_re, v_ssm_c_im, v_ssm_d, v_g_ffn2, v_g_final, v_meta_tokens, v_conv_w]
    reduced = _unpack(rs_first.extra, local_small)
    reduced[-2] = lax.dynamic_slice_in_dim(reduced[-2], shard * meta_tokens.shape[1], meta_tokens.shape[1], 1)
    reduced[-1] = lax.dynamic_slice_in_dim(reduced[-1], shard * conv_w.shape[3], conv_w.shape[3], 1)
    small_g = [r.reshape(w.shape) for r, w in zip(reduced, small_w)]
    two_d = lambda arrs: [a.reshape(1, -1) if a.ndim == 1 else a for a in arrs]
    ds_, nm_, nv_ = _adamw_small(two_d(small_w), two_d(small_g), two_d(small_m), two_d(small_v))
    like = lambda outs: [o.reshape(w.shape) for o, w in zip(outs, small_w)]
    small_out = {nme: o for nme, o in zip(small_names, zip(small_g, like(ds_), like(nm_), like(nv_)))}

    loss = lax.psum(loss_part[0, 0], ("x", "y", "c"))
    order = ["meta_tokens", "g_ffn1", "ffn1_w_gate", "ffn1_w_up", "ffn1_w_down", "g_mix", "w_in", "b_gate",
             "ssm_a_re", "ssm_a_im", "ssm_log_dt", "ssm_b_re", "ssm_b_im", "ssm_c_re", "ssm_c_im", "ssm_d",
             "ssm_w_glu", "conv_w", "conv_w_out", "w_o", "g_ffn2", "ffn2_w_gate", "ffn2_w_up", "ffn2_w_down",
             "g_final"]
    res = {**big_out, **small_out}
    return (loss, grad_x, *[res[nme][0] for nme in order], *[res[nme][1] for nme in order],
            *[res[nme][2] for nme in order], *[res[nme][3] for nme in order])
```

```python
import functools
import math

import jax
import jax.numpy as jnp
from jax import lax
from jax.experimental import pallas as pl
from jax.experimental.pallas import tpu as pltpu

F32 = jnp.float32
BF16 = jnp.bfloat16
MESH = pl.DeviceIdType.MESH

RMS_EPS = 1e-6
ADAM_LR = 0.001
ADAM_B1 = 0.9
ADAM_B2 = 0.999
ADAM_EPS = 1e-08
ADAM_WD = 0.01
ADAM_STEP = 10

LANES = 128
SUBLANES = 8
VMEM_LIMIT = 56 * 1024 * 1024
ADD_GROUP_BYTES = VMEM_LIMIT // 3

ROW_ALIGN = 256
SCAN_TILE = 256
SCAN_SEQS = 16
GROUPS_PER_BLOCK = 8


def _params(sem, vmem=VMEM_LIMIT):
    return pltpu.CompilerParams(dimension_semantics=sem, vmem_limit_bytes=vmem)


def _pick_tile(n, candidates):
    for c in candidates:
        if n % c == 0:
            return c
    raise ValueError(f"no tile for {n}")


def _dot(a, b):
    return jnp.dot(a, b, preferred_element_type=F32)


def _dot_nt(a, b):
    return lax.dot_general(a, b, (((1,), (1,)), ((), ())), preferred_element_type=F32)


def _dot_tn(a, b):
    return lax.dot_general(a, b, (((0,), (0,)), ((), ())), preferred_element_type=F32)


def _sigmoid(x):
    return pl.reciprocal(1.0 + jnp.exp(-x), approx=True)


def _rms_stats(h):
    r = lax.rsqrt(jnp.mean(h * h, axis=-1, keepdims=True) + RMS_EPS)
    return h * r, r


def _rms_bwd(xhat, r, g, dn):
    dxh = dn * g
    return r * (dxh - xhat * jnp.mean(dxh * xhat, axis=-1, keepdims=True))


GELU_K = math.sqrt(2.0 / math.pi)
GELU_C = 0.044715


def _gelu(x):
    return 0.5 * x * (1.0 + jnp.tanh(GELU_K * (x + GELU_C * x * x * x)))


def _gelu_grad(x):
    t = jnp.tanh(GELU_K * (x + GELU_C * x * x * x))
    return 0.5 * (1.0 + t) + 0.5 * x * (1.0 - t * t) * GELU_K * (1.0 + 3.0 * GELU_C * x * x)


def _for_tile_rows(i, ni, tm, n_meta, seq, fn):
    pl.when(i == 0)(lambda: fn(0, min(tm - n_meta, seq), n_meta))
    if ni > 1:
        last_lo = (ni - 1) * tm - n_meta
        pl.when(i == ni - 1)(lambda: fn(last_lo, min(seq - last_lo, tm), 0))
    if ni > 2:
        pl.when((i > 0) & (i < ni - 1))(lambda: fn(pl.multiple_of(i * tm - n_meta, SUBLANES), tm, 0))


def _ffn_fwd(h, g, wg, wu, wd, name, final=None, carry=None):
    tp, d = h.shape
    ns, f4, _ = wg.shape
    tm = _pick_tile(tp, (768, 512, 256))
    ni = tp // tm

    def body(*refs):
        refs, phases = split(refs)
        if final is None:
            h_ref, g_ref, wg_ref, wu_ref, wd_ref, ho_ref, a_ref, b_ref, n_scr, acc = refs
        else:
            (h_ref, g_ref, wg_ref, wu_ref, wd_ref, gf_ref, tg_hbm,
             ho_ref, a_ref, b_ref, n_scr, dgf_ref, loss_ref, dy_ref, acc, tg_ref, tg_sem) = refs
        i = pl.program_id(0)
        k = pl.program_id(1)
        _run_phases(phases, carry, i * ns + k, ni * ns)

        if final is not None:
            def target_rows(lo, n, at):
                return pltpu.make_async_copy(tg_hbm.at[pl.ds(lo, n), :], tg_ref.at[pl.ds(at, n), :], tg_sem)

            def fetch_target(lo, n, at):
                if at > 0:
                    tg_ref[pl.ds(0, at), :] = jnp.zeros((at, d), F32)
                if at + n < tm:
                    tg_ref[pl.ds(at + n, tm - at - n), :] = jnp.zeros((tm - at - n, d), F32)
                target_rows(lo, n, at).start()

            pl.when(k == 0)(lambda: _for_tile_rows(i, ni, tm, final[2], final[3], fetch_target))

        @pl.when(k == 0)
        def _():
            xhat, _ = _rms_stats(h_ref[...])
            n_scr[...] = (xhat * g_ref[...]).astype(BF16)
            acc[...] = jnp.zeros_like(acc)

        n = n_scr[...]
        a = _dot_nt(n, wg_ref[...])
        b = _dot_nt(n, wu_ref[...])
        a_ref[...] = a.astype(BF16)
        b_ref[...] = b.astype(BF16)
        s = (a * _sigmoid(a) * b).astype(BF16)
        acc[...] += _dot(s, wd_ref[...])

        if final is None:
            @pl.when(k == ns - 1)
            def _():
                ho_ref[...] = h_ref[...] + 0.5 * acc[...]
        else:
            n_meta, seq = final[2], final[3]

            @pl.when((i == 0) & (k == 0))
            def _():
                dgf_ref[...] = jnp.zeros_like(dgf_ref)
                loss_ref[...] = jnp.zeros_like(loss_ref)

            @pl.when(k == ns - 1)
            def _():
                _for_tile_rows(i, ni, tm, n_meta, seq, lambda lo, n, at: target_rows(lo, n, at).wait())
                h3 = h_ref[...] + 0.5 * acc[...]
                xhat, r = _rms_stats(h3)
                gf = gf_ref[...]
                row = i * tm + lax.broadcasted_iota(jnp.int32, (tm, d), 0)
                valid = (row >= n_meta) & (row < n_meta + seq)
                diff = jnp.where(valid, xhat * gf - tg_ref[...], 0.0)
                dout = diff * (1.0 / d)
                loss_ref[...] += jnp.full(loss_ref.shape, 0.5 * jnp.sum(diff * diff) * (1.0 / d), F32)
                dgf_ref[...] += jnp.sum(dout * xhat, axis=0, keepdims=True)
                dh3 = _rms_bwd(xhat, r, gf, dout)
                ho_ref[...] = dh3
                dy_ref[...] = (0.5 * dh3).astype(BF16)

    row_spec = pl.BlockSpec((tm, d), lambda i, k: (i, 0))
    vec_spec = pl.BlockSpec((1, d), lambda i, k: (0, 0))
    in_specs = [row_spec, vec_spec,
                pl.BlockSpec((None, f4, d), lambda i, k: (k, 0, 0)),
                pl.BlockSpec((None, f4, d), lambda i, k: (k, 0, 0)),
                pl.BlockSpec((None, f4, d), lambda i, k: (k, 0, 0))]
    act_spec = pl.BlockSpec((None, tm, f4), lambda i, k: (k, i, 0))
    out_specs = [row_spec, act_spec, act_spec, row_spec]
    out_shape = [jax.ShapeDtypeStruct((tp, d), F32),
                 jax.ShapeDtypeStruct((ns, tp, f4), BF16),
                 jax.ShapeDtypeStruct((ns, tp, f4), BF16),
                 jax.ShapeDtypeStruct((tp, d), BF16)]
    args = [h, g, wg, wu, wd]
    scratch = [pltpu.VMEM((tm, d), F32)]
    if final is not None:
        in_specs += [vec_spec, ANY]
        args += [final[0], final[1]]
        out_specs += [vec_spec, pl.BlockSpec((1, LANES), lambda i, k: (0, 0)), row_spec]
        out_shape += [jax.ShapeDtypeStruct((1, d), F32), jax.ShapeDtypeStruct((1, LANES), F32),
                      jax.ShapeDtypeStruct((tp, d), BF16)]
        scratch += [pltpu.VMEM((tm, d), F32), pltpu.SemaphoreType.DMA(())]
    split = _attach_carry(carry, in_specs, args, out_specs, out_shape, scratch)
    return pl.pallas_call(
        body, name=name, grid=(ni, ns), in_specs=in_specs, out_specs=out_specs, out_shape=out_shape,
        scratch_shapes=scratch, compiler_params=_params(("arbitrary", "arbitrary")),
        input_output_aliases=split.aliases,
    )(*args)


def _ffn_bwd_shard(k, ns, dn_prev, dy, n, a, b, wg, wu, wd, tail, name, carry=None, unpad=None):
    tp, d = n.shape
    f4 = wg.shape[1]
    tm = _pick_tile(tp, (768, 512, 256))
    ni = tp // tm
    first, last = k == 0, k == ns - 1
    unpad = unpad if last else None

    def body(*refs):
        refs, phases = split(refs)
        acc_in = None if first else refs.pop(0)
        if last:
            dh_ref, h_ref, g_ref = refs[:3]
        else:
            dy_ref, n_ref = refs[:2]
        refs = refs[3 if last else 2:]
        a_ref, b_ref, wg_hbm, wu_hbm, wd_hbm = refs[:5]
        refs = refs[5:]
        acc_out, dwg_hbm, dwu_hbm, dwd_hbm = refs[:4]
        rest = refs[4:]
        dg_ref = rest.pop(0) if last else None
        head_ref = rest.pop(0) if unpad else None
        wg_ref, wu_ref, wd_ref, dwg_ref, dwu_ref, dwd_ref, wsem = rest[:7]
        i = pl.program_id(0)
        _run_phases(phases, carry, i, ni)
        if unpad:
            res_ref, res_sem = rest[7:]

            def real_rows(lo, cnt, at):
                return pltpu.make_async_copy(res_ref.at[pl.ds(at, cnt), :], acc_out.at[pl.ds(lo, cnt), :], res_sem)

            def wait_tile(tile):
                _for_tile_rows(tile, ni, tm, *unpad, lambda lo, cnt, at: real_rows(lo, cnt, at).wait())

        @pl.when(i == 0)
        def _():
            loads = [pltpu.make_async_copy(src.at[k], dst, wsem.at[j])
                     for j, (src, dst) in enumerate(((wg_hbm, wg_ref), (wu_hbm, wu_ref), (wd_hbm, wd_ref)))]
            for cp in loads:
                cp.start()
            dwg_ref[...] = jnp.zeros_like(dwg_ref)
            dwu_ref[...] = jnp.zeros_like(dwu_ref)
            dwd_ref[...] = jnp.zeros_like(dwd_ref)
            if last:
                dg_ref[...] = jnp.zeros_like(dg_ref)
            for cp in loads:
                cp.wait()

        if last:
            xhat, r = _rms_stats(h_ref[...])
            n = (xhat * g_ref[...]).astype(BF16)
            dy = (0.5 * dh_ref[...]).astype(BF16)
        else:
            n = n_ref[...]
            dy = dy_ref[...]
        av = a_ref[...].astype(F32)
        bv = b_ref[...].astype(F32)
        sg = _sigmoid(av)
        silu = av * sg
        ds = _dot_nt(dy, wd_ref[...])
        da = (ds * bv * (sg * (1.0 + av * (1.0 - sg)))).astype(BF16)
        db = (ds * silu).astype(BF16)
        s = (silu * bv).astype(BF16)
        dwd_ref[...] += _dot_tn(s, dy)
        dwg_ref[...] += _dot_tn(da, n)
        dwu_ref[...] += _dot_tn(db, n)
        dn = _dot(da, wg_ref[...]) + _dot(db, wu_ref[...])
        if not first:
            dn = dn + acc_in[...]
        if last:
            dg_ref[...] += jnp.sum(dn * xhat, axis=0, keepdims=True)
            dh_in = dh_ref[...] + _rms_bwd(xhat, r, g_ref[...], dn)
            if unpad:
                pl.when(i > 0)(lambda: wait_tile(i - 1))
                res_ref[...] = dh_in

                @pl.when(i == 0)
                def _():
                    head_ref[...] = res_ref[pl.ds(0, unpad[0]), :]

                _for_tile_rows(i, ni, tm, *unpad, lambda lo, cnt, at: real_rows(lo, cnt, at).start())
                pl.when(i == ni - 1)(lambda: wait_tile(i))
            else:
                acc_out[...] = dh_in
        else:
            acc_out[...] = dn

        @pl.when(i == ni - 1)
        def _():
            stores = []
            for j, (acc_ref, stage_ref, out_hbm) in enumerate(((dwg_ref, wg_ref, dwg_hbm), (dwu_ref, wu_ref, dwu_hbm),
                                                              (dwd_ref, wd_ref, dwd_hbm))):
                stage_ref[...] = acc_ref[...].astype(BF16)
                stores.append(pltpu.make_async_copy(stage_ref, out_hbm, wsem.at[j]))
                stores[-1].start()
            for cp in stores:
                cp.wait()

    row_spec = pl.BlockSpec((tm, d), lambda i: (i, 0))
    vec_spec = pl.BlockSpec((1, d), lambda i: (0, 0))
    act_spec = pl.BlockSpec((None, tm, f4), lambda i: (k, i, 0))
    in_specs = [act_spec, act_spec, ANY, ANY, ANY]
    args = [a, b, wg, wu, wd]
    if last:
        in_specs = [row_spec, row_spec, vec_spec] + in_specs
        args = list(tail) + args
    else:
        in_specs = [row_spec, row_spec] + in_specs
        args = [dy, n] + args
    if not first:
        in_specs.insert(0, row_spec)
        args.insert(0, dn_prev)
    out_specs = [row_spec, ANY, ANY, ANY]
    out_shape = [jax.ShapeDtypeStruct((tp, d), F32)] + [jax.ShapeDtypeStruct((f4, d), BF16)] * 3
    scratch = [pltpu.VMEM((f4, d), BF16)] * 3 + [pltpu.VMEM((f4, d), F32)] * 3 + [pltpu.SemaphoreType.DMA((3,))]
    if last:
        out_specs.append(vec_spec)
        out_shape.append(jax.ShapeDtypeStruct((1, d), F32))
    if unpad:
        out_specs[0] = ANY
        out_shape[0] = jax.ShapeDtypeStruct((unpad[1], d), F32)
        out_specs.append(pl.BlockSpec((unpad[0], d), lambda i: (0, 0)))
        out_shape.append(jax.ShapeDtypeStruct((unpad[0], d), F32))
        scratch += [pltpu.VMEM((tm, d), F32), pltpu.SemaphoreType.DMA(())]
    n_host = len(out_shape)
    split = _attach_carry(carry, in_specs, args, out_specs, out_shape, scratch)
    outs = pl.pallas_call(
        body, name=f"{name}_{k}", grid=(ni,), in_specs=in_specs, out_specs=out_specs, out_shape=out_shape,
        scratch_shapes=scratch, compiler_params=_params(("arbitrary",)), input_output_aliases=split.aliases,
    )(*args)
    return outs[:n_host], outs[n_host:]


def _ffn_bwd(dh_out, dy, h_in, n, g, a, b, wg, wu, wd, name, chain=None, unpad=None):
    ns = wg.shape[0]
    acc, dwg, dwu, dwd = None, [], [], []
    for k in range(ns):
        carry = chain.carry() if chain is not None else None
        outs, carried = _ffn_bwd_shard(k, ns, acc, dy, n, a, b, wg, wu, wd, (dh_out, h_in, g), name, carry, unpad)
        if chain is not None:
            chain.feed(carried)
        acc = outs[0]
        dwg.append(outs[1])
        dwu.append(outs[2])
        dwd.append(outs[3])
    return (acc, dwg, dwu, dwd) + tuple(outs[4:])


def _win_fwd(h, g, w_in, carry=None):
    tp, d = h.shape
    ns = w_in.shape[0]
    tm = _pick_tile(tp, (768, 512, 256))
    ni = tp // tm

    def body(*refs):
        (h_ref, g_ref, w_ref, u_ref, p_ref), phases = split(refs)
        _run_phases(phases, carry, pl.program_id(0), ni)
        xhat, _ = _rms_stats(h_ref[...])
        u = (xhat * g_ref[...]).astype(BF16)
        u_ref[...] = u
        for k in range(ns):
            p_ref[k] = _dot(u, w_ref[k]).astype(BF16)

    in_specs = [pl.BlockSpec((tm, d), lambda i: (i, 0)),
                pl.BlockSpec((1, d), lambda i: (0, 0)),
                pl.BlockSpec((ns, d, d), lambda i: (0, 0, 0))]
    out_specs = [pl.BlockSpec((tm, d), lambda i: (i, 0)),
                 pl.BlockSpec((ns, tm, d), lambda i: (0, i, 0))]
    out_shape = [jax.ShapeDtypeStruct((tp, d), BF16), jax.ShapeDtypeStruct((ns, tp, d), BF16)]
    args, scratch = [h, g, w_in], []
    split = _attach_carry(carry, in_specs, args, out_specs, out_shape, scratch)
    return pl.pallas_call(
        body, name="win_fwd", grid=(ni,), in_specs=in_specs, out_specs=out_specs, out_shape=out_shape,
        scratch_shapes=scratch, compiler_params=_params(("arbitrary",)), input_output_aliases=split.aliases,
    )(*args)


def _win_bwd_shard(k, ns, du_prev, dpb, dug, u, w_in, h1, g, dh2):
    tp, d = h1.shape
    dh = d // 2
    tm = _pick_tile(tp, (768, 512, 256))
    first, last = k == 0, k == ns - 1

    def body(*refs):
        refs = list(refs)
        acc_in = None if first else refs.pop(0)
        dug_ref = refs.pop(0) if first else None
        dp_ref, u_ref, w_ref = refs[:3]
        refs = refs[3:]
        if last:
            h_ref, g_ref, dh2_ref, acc_out, dw_ref, dg_ref, dy_ref, dw_acc = refs
        else:
            acc_out, dw_ref, dw_acc = refs
        i = pl.program_id(0)

        @pl.when(i == 0)
        def _():
            dw_acc[...] = jnp.zeros_like(dw_acc)
            if last:
                dg_ref[...] = jnp.zeros_like(dg_ref)

        dp = dp_ref[...]
        if first:
            dp = jnp.concatenate([dug_ref[...], dp[:, dh:]], axis=1)
        dw_acc[...] += _dot_tn(u_ref[...], dp)
        du = _dot_nt(dp, w_ref[...])
        if not first:
            du = du + acc_in[...]
        if last:
            xhat, r = _rms_stats(h_ref[...])
            dg_ref[...] += jnp.sum(du * xhat, axis=0, keepdims=True)
            dh1 = dh2_ref[...] + _rms_bwd(xhat, r, g_ref[...], du)
            acc_out[...] = dh1
            dy_ref[...] = (0.5 * dh1).astype(BF16)
        else:
            acc_out[...] = du

        @pl.when(i == tp // tm - 1)
        def _():
            dw_ref[...] = dw_acc[...].astype(BF16)

    row_spec = pl.BlockSpec((tm, d), lambda i: (i, 0))
    vec_spec = pl.BlockSpec((1, d), lambda i: (0, 0))
    in_specs = [pl.BlockSpec((None, tm, d), lambda i: (k, i, 0)), row_spec,
                pl.BlockSpec((None, d, d), lambda i: (k, 0, 0))]
    args = [dpb, u, w_in]
    if first:
        in_specs.insert(0, pl.BlockSpec((tm, dh), lambda i: (i, 0)))
        args.insert(0, dug)
    else:
        in_specs.insert(0, row_spec)
        args.insert(0, du_prev)
    out_specs = [row_spec, pl.BlockSpec((d, d), lambda i: (0, 0))]
    out_shape = [jax.ShapeDtypeStruct((tp, d), F32), jax.ShapeDtypeStruct((d, d), BF16)]
    if last:
        in_specs += [row_spec, vec_spec, row_spec]
        args += [h1, g, dh2]
        out_specs += [vec_spec, row_spec]
        out_shape += [jax.ShapeDtypeStruct((1, d), F32), jax.ShapeDtypeStruct((tp, d), BF16)]
    return pl.pallas_call(
        body, name=f"win_bwd_{k}", grid=(tp // tm,), in_specs=in_specs, out_specs=out_specs,
        out_shape=out_shape, scratch_shapes=[pltpu.VMEM((d, d), F32)],
        compiler_params=_params(("arbitrary",)),
    )(*args)


def _win_bwd(dpb, dug, u, w_in, h1, g, dh2):
    ns = w_in.shape[0]
    acc, dws = None, []
    for k in range(ns):
        outs = _win_bwd_shard(k, ns, acc, dpb, dug, u, w_in, h1, g, dh2)
        acc = outs[0]
        dws.append(outs[1])
    return acc, dws, outs[2], outs[3]


def _cmul(ar, ai, br, bi):
    return ar * br - ai * bi, ar * bi + ai * br


def _scan_rows(j, sub):
    return pl.ds(j * SCAN_SEQS, SCAN_SEQS)


def _permute_rows(src_ref, dst_ref, sub):
    for j in range(sub):
        dst_ref[pl.ds(j * SCAN_SEQS, SCAN_SEQS), :] = src_ref[pl.ds(j, SCAN_SEQS, stride=sub), :]


def _unpermute_rows(src_ref, dst_ref, sub):
    for j in range(sub):
        dst_ref[pl.ds(j, SCAN_SEQS, stride=sub), :] = src_ref[pl.ds(j * SCAN_SEQS, SCAN_SEQS), :]


def _local_scan(x_ref, lr, li, w, sub, reverse):
    hr = jnp.zeros((SCAN_SEQS, w), F32)
    hi = jnp.zeros((SCAN_SEQS, w), F32)
    order = range(sub - 1, -1, -1) if reverse else range(sub)
    for j in order:
        xr = x_ref[_scan_rows(j, sub), pl.ds(0, w)]
        xi = x_ref[_scan_rows(j, sub), pl.ds(w, w)]
        if reverse:
            hr, hi = lr * hr + li * hi + xr, lr * hi - li * hr + xi
        else:
            hr, hi = lr * hr - li * hi + xr, lr * hi + li * hr + xi
        x_ref[_scan_rows(j, sub), pl.ds(0, w)] = hr
        x_ref[_scan_rows(j, sub), pl.ds(w, w)] = hi
    return hr, hi


def _entering_states(er, ei, fr, fi, pow_ref, w, sub, reverse):
    lane = lax.broadcasted_iota(jnp.int32, (SCAN_SEQS, w), 0)
    if reverse:
        edge, shift1 = SCAN_SEQS - 1, SCAN_SEQS - 1
    else:
        edge, shift1 = 0, 1
    zr = jnp.where(lane == edge, pltpu.roll(fr, shift1, 0), pltpu.roll(er, shift1, 0))
    zi = jnp.where(lane == edge, pltpu.roll(fi, shift1, 0), pltpu.roll(ei, shift1, 0))
    for m in range(SCAN_SEQS.bit_length() - 1):
        step, row = 1 << m, sub - 1 + m
        ar = pow_ref[pl.ds(row, 1), pl.ds(0, w)]
        ai = pow_ref[pl.ds(row, 1), pl.ds(w, w)]
        if reverse:
            ai = -ai
            keep = lane < SCAN_SEQS - step
            sr = jnp.where(keep, pltpu.roll(zr, SCAN_SEQS - step, 0), 0.0)
            si = jnp.where(keep, pltpu.roll(zi, SCAN_SEQS - step, 0), 0.0)
        else:
            keep = lane >= step
            sr = jnp.where(keep, pltpu.roll(zr, step, 0), 0.0)
            si = jnp.where(keep, pltpu.roll(zi, step, 0), 0.0)
        pr, pi = _cmul(ar, ai, sr, si)
        zr, zi = zr + pr, zi + pi
    ar = pow_ref[pl.ds(sub - 1, 1), pl.ds(0, w)]
    ai = pow_ref[pl.ds(sub - 1, 1), pl.ds(w, w)]
    if reverse:
        ai = -ai
    pr, pi = _cmul(ar, ai, zr, zi)
    return zr, zi, er + pr, ei + pi


def _scan_fwd(p, mb, mc, powt, dskip):
    _, tp, d = p.shape
    nb, cb, w2 = mb.shape
    w = w2 // 2
    q = SCAN_TILE
    sub = q // SCAN_SEQS
    nt = tp // q
    ds = d // 2

    def body(ug_ref, mb_ref, mc_ref, pow_ref, d_ref, y_ref, bnd_ref, x_scr, carry, nat, perm):
        t = pl.program_id(1)

        @pl.when(t == 0)
        def _():
            carry[...] = jnp.zeros_like(carry)

        ugf = ug_ref[...].astype(F32)
        nat[...] = ugf
        _permute_rows(nat, perm, sub)
        x_scr[...] = _dot(perm[...].astype(BF16), mb_ref[...])
        lr = jnp.broadcast_to(pow_ref[pl.ds(0, 1), pl.ds(0, w)], (SCAN_SEQS, w))
        li = jnp.broadcast_to(pow_ref[pl.ds(0, 1), pl.ds(w, w)], (SCAN_SEQS, w))
        er, ei = _local_scan(x_scr, lr, li, w, sub, False)
        zr, zi, fr, fi = _entering_states(er, ei, carry[:, pl.ds(0, w)], carry[:, pl.ds(w, w)],
                                          pow_ref, w, sub, False)
        carry[:, pl.ds(0, w)] = fr
        carry[:, pl.ds(w, w)] = fi
        bnd_ref[:, pl.ds(0, w)] = fr
        bnd_ref[:, pl.ds(w, w)] = fi
        for j in range(sub):
            pr = pow_ref[pl.ds(j, 1), pl.ds(0, w)]
            pi = pow_ref[pl.ds(j, 1), pl.ds(w, w)]
            cr, ci = _cmul(pr, pi, zr, zi)
            x_scr[_scan_rows(j, sub), pl.ds(0, w)] += cr
            x_scr[_scan_rows(j, sub), pl.ds(w, w)] += ci
        hb = x_scr[...].astype(BF16)
        perm[...] = _dot_nt(hb, mc_ref[...])
        _unpermute_rows(perm, nat, sub)
        y_ref[...] = nat[...] + d_ref[...] * ugf

    in_specs = [pl.BlockSpec((None, q, cb), lambda b, t: (0, t, b)),
                pl.BlockSpec((None, cb, w2), lambda b, t: (b, 0, 0)),
                pl.BlockSpec((None, cb, w2), lambda b, t: (b, 0, 0)),
                pl.BlockSpec((None, powt.shape[1], w2), lambda b, t: (b, 0, 0)),
                pl.BlockSpec((1, cb), lambda b, t: (0, b))]
    out_specs = [pl.BlockSpec((q, cb), lambda b, t: (t, b)),
                 pl.BlockSpec((None, None, SCAN_SEQS, w2), lambda b, t: (b, t, 0, 0))]
    out_shape = [jax.ShapeDtypeStruct((tp, ds), F32), jax.ShapeDtypeStruct((nb, nt, SCAN_SEQS, w2), F32)]
    scratch = [pltpu.VMEM((q, w2), F32), pltpu.VMEM((SCAN_SEQS, w2), F32),
               pltpu.VMEM((q, cb), F32), pltpu.VMEM((q, cb), F32)]
    return pl.pallas_call(
        body, name="s5_scan_fwd", grid=(nb, nt), in_specs=in_specs, out_specs=out_specs,
        out_shape=out_shape, scratch_shapes=scratch, compiler_params=_params(("arbitrary", "arbitrary")),
    )(p, mb, mc, powt, dskip)


def _scan_bwd(p, dy, mb, mc, powt, dskip, bnd):
    _, tp, d = p.shape
    nb, cb, w2 = mb.shape
    w = w2 // 2
    q = SCAN_TILE
    sub = q // SCAN_SEQS
    nt = tp // q
    ds = d // 2

    def body(ug_ref, dy_ref, mb_ref, mc_ref, pow_ref, d_ref, bnd_ref,
             dug_ref, dmb_ref, dmc_ref, dlam_ref, dd_ref, x_scr, y_scr, gcarry, nat, perm):
        t = pl.program_id(1)
        tt = nt - 1 - t

        @pl.when(t == 0)
        def _():
            gcarry[...] = jnp.zeros_like(gcarry)
            dmb_ref[...] = jnp.zeros_like(dmb_ref)
            dmc_ref[...] = jnp.zeros_like(dmc_ref)
            dlam_ref[...] = jnp.zeros_like(dlam_ref)
            dd_ref[...] = jnp.zeros_like(dd_ref)

        ugf = ug_ref[...].astype(F32)
        dyf = dy_ref[...].astype(F32)
        dd_ref[...] += jnp.sum((dyf * ugf).reshape(q // SUBLANES, SUBLANES, cb), axis=0)
        nat[...] = ugf
        _permute_rows(nat, perm, sub)
        ug = perm[...].astype(BF16)
        nat[...] = dyf
        _permute_rows(nat, perm, sub)
        dyb = perm[...].astype(BF16)
        lr = jnp.broadcast_to(pow_ref[pl.ds(0, 1), pl.ds(0, w)], (SCAN_SEQS, w))
        li = jnp.broadcast_to(pow_ref[pl.ds(0, 1), pl.ds(w, w)], (SCAN_SEQS, w))

        x_scr[...] = _dot(ug, mb_ref[...])
        er, ei = _local_scan(x_scr, lr, li, w, sub, False)
        first = tt == 0
        pfr = jnp.where(first, 0.0, bnd_ref[:, pl.ds(0, w)])
        pfi = jnp.where(first, 0.0, bnd_ref[:, pl.ds(w, w)])
        hzr, hzi, _, _ = _entering_states(er, ei, pfr, pfi, pow_ref, w, sub, False)
        for j in range(sub):
            pr = pow_ref[pl.ds(j, 1), pl.ds(0, w)]
            pi = pow_ref[pl.ds(j, 1), pl.ds(w, w)]
            cr, ci = _cmul(pr, pi, hzr, hzi)
            x_scr[_scan_rows(j, sub), pl.ds(0, w)] += cr
            x_scr[_scan_rows(j, sub), pl.ds(w, w)] += ci

        y_scr[...] = _dot(dyb, mc_ref[...])
        er, ei = _local_scan(y_scr, lr, li, w, sub, True)
        gzr, gzi, fr, fi = _entering_states(er, ei, gcarry[:, pl.ds(0, w)], gcarry[:, pl.ds(w, w)],
                                            pow_ref, w, sub, True)
        gcarry[:, pl.ds(0, w)] = fr
        gcarry[:, pl.ds(w, w)] = fi
        accr = jnp.zeros((SCAN_SEQS, w), F32)
        acci = jnp.zeros((SCAN_SEQS, w), F32)
        for j in range(sub):
            pr = pow_ref[pl.ds(sub - 1 - j, 1), pl.ds(0, w)]
            pi = pow_ref[pl.ds(sub - 1 - j, 1), pl.ds(w, w)]
            cr, ci = _cmul(pr, -pi, gzr, gzi)
            gr = y_scr[_scan_rows(j, sub), pl.ds(0, w)] + cr
            gi = y_scr[_scan_rows(j, sub), pl.ds(w, w)] + ci
            y_scr[_scan_rows(j, sub), pl.ds(0, w)] = gr
            y_scr[_scan_rows(j, sub), pl.ds(w, w)] = gi
            if j == 0:
                hpr, hpi = hzr, hzi
            else:
                hpr = x_scr[_scan_rows(j - 1, sub), pl.ds(0, w)]
                hpi = x_scr[_scan_rows(j - 1, sub), pl.ds(w, w)]
            accr += hpr * gr + hpi * gi
            acci += hpr * gi - hpi * gr
        dlam_ref[:, pl.ds(0, w)] += accr
        dlam_ref[:, pl.ds(w, w)] += acci

        hb = x_scr[...].astype(BF16)
        gb = y_scr[...].astype(BF16)
        dmc_ref[...] += _dot_tn(dyb, hb)
        dmb_ref[...] += _dot_tn(ug, gb)
        perm[...] = _dot_nt(gb, mb_ref[...])
        _unpermute_rows(perm, nat, sub)
        dug_ref[...] = (nat[...] + d_ref[...] * dyf).astype(BF16)

    blk = lambda b, t: (b, 0, 0)
    return pl.pallas_call(
        body, name="s5_scan_bwd", grid=(nb, nt),
        in_specs=[pl.BlockSpec((None, q, cb), lambda b, t: (0, nt - 1 - t, b)),
                  pl.BlockSpec((q, cb), lambda b, t: (nt - 1 - t, b)),
                  pl.BlockSpec((None, cb, w2), blk),
                  pl.BlockSpec((None, cb, w2), blk),
                  pl.BlockSpec((None, powt.shape[1], w2), blk),
                  pl.BlockSpec((1, cb), lambda b, t: (0, b)),
                  pl.BlockSpec((None, None, SCAN_SEQS, w2),
                               lambda b, t: (b, jnp.maximum(nt - 2 - t, 0), 0, 0))],
        out_specs=[pl.BlockSpec((q, cb), lambda b, t: (nt - 1 - t, b)),
                   pl.BlockSpec((None, cb, w2), blk),
                   pl.BlockSpec((None, cb, w2), blk),
                   pl.BlockSpec((None, SCAN_SEQS, w2), blk),
                   pl.BlockSpec((SUBLANES, cb), lambda b, t: (0, b))],
        out_shape=[jax.ShapeDtypeStruct((tp, ds), BF16),
                   jax.ShapeDtypeStruct((nb, cb, w2), F32),
                   jax.ShapeDtypeStruct((nb, cb, w2), F32),
                   jax.ShapeDtypeStruct((nb, SCAN_SEQS, w2), F32),
                   jax.ShapeDtypeStruct((SUBLANES, ds), F32)],
        scratch_shapes=[pltpu.VMEM((q, w2), F32), pltpu.VMEM((q, w2), F32),
                        pltpu.VMEM((SCAN_SEQS, w2), F32), pltpu.VMEM((q, cb), F32), pltpu.VMEM((q, cb), F32)],
        compiler_params=_params(("arbitrary", "arbitrary")),
    )(p, dy, mb, mc, powt, dskip, bnd)


HALO = 16


def _mix_tile(ys5, p0, p1, p2, p3, prev_cin, cw, bgate, wglu, wco, d):
    dh = d // 2
    tm = ys5.shape[0]
    v = p0[:, dh:].astype(F32)
    gbr = p1[:, :dh].astype(F32)
    gcr = p1[:, dh:].astype(F32)
    gact = _gelu(ys5).astype(BF16)
    z = _dot(gact, wglu)
    z1, z2 = z[:, :d], z[:, d:]
    sg = _sigmoid(z2)
    y_ssm = z1 * sg
    cin = gcr * v
    ext = jnp.concatenate([cin, prev_cin], axis=0)
    r1 = pltpu.roll(ext, 1, 0)[:tm]
    r2 = pltpu.roll(ext, 2, 0)[:tm]
    cv = cw[2] * cin + cw[1] * r1 + cw[0] * r2
    cg = (gbr * cv).astype(BF16)
    y_conv = _dot(cg, wco)
    g_s = _sigmoid(p2.astype(F32) + bgate[:, :d])
    g_c = _sigmoid(p3.astype(F32) + bgate[:, d:])
    mixed = g_s * y_ssm + g_c * y_conv
    return dict(v=v, gb=gbr, gc=gcr, gact=gact, z1=z1, sg=sg, y_ssm=y_ssm, cin=cin, r1=r1, r2=r2,
                cv=cv, cg=cg, y_conv=y_conv, g_s=g_s, g_c=g_c, mixed=mixed)


def _mix_fwd(h1, ys5, p, cw, bgate, wglu, wco, wo, carry=None):
    tp, d = h1.shape
    dh = d // 2
    tm = ROW_ALIGN
    ni = tp // tm

    def body(*refs):
        refs, phases = split(refs)
        (h_ref, y_ref, p0_ref, p1_ref, p2_ref, p3_ref, cw_ref, bg_ref, wglu_ref, wco_ref, wo_ref,
         o_ref, prev) = refs
        _run_phases(phases, carry, pl.program_id(0), ni)

        @pl.when(pl.program_id(0) == 0)
        def _():
            prev[...] = jnp.zeros_like(prev)

        cw = [cw_ref[pl.ds(t, 1), :] for t in range(3)]
        f = _mix_tile(y_ref[...], p0_ref[...], p1_ref[...], p2_ref[...], p3_ref[...], prev[...],
                      cw, bg_ref[...], wglu_ref[...], wco_ref[...], d)
        prev[...] = f["cin"][tm - HALO:, :]
        o_ref[...] = h_ref[...] + _dot(f["mixed"].astype(BF16), wo_ref[...])

    row = pl.BlockSpec((tm, d), lambda i: (i, 0))
    full = lambda a: pl.BlockSpec(a.shape, lambda i: (0,) * a.ndim)
    pk = lambda k: pl.BlockSpec((None, tm, d), lambda i, k=k: (k, i, 0))
    in_specs = [row, pl.BlockSpec((tm, dh), lambda i: (i, 0)), pk(0), pk(1), pk(2), pk(3),
                full(cw), full(bgate), full(wglu), full(wco), full(wo)]
    out_specs, out_shape = [row], [jax.ShapeDtypeStruct((tp, d), F32)]
    args, scratch = [h1, ys5, p, p, p, p, cw, bgate, wglu, wco, wo], [pltpu.VMEM((HALO, dh), F32)]
    split = _attach_carry(carry, in_specs, args, out_specs, out_shape, scratch)
    return pl.pallas_call(
        body, name="mix_fwd", grid=(ni,), in_specs=in_specs, out_specs=out_specs, out_shape=out_shape,
        scratch_shapes=scratch, compiler_params=_params(("arbitrary",)), input_output_aliases=split.aliases,
    )(*args)


def _mix_bwd(dh2, ys5, p, cw, bgate, wglu, wco, wo):
    tp, d = dh2.shape
    dh = d // 2
    tm = ROW_ALIGN
    ni = tp // tm
    hb = tm // HALO

    def body(dh_ref, y_ref, p0_ref, p1_ref, p2_ref, p3_ref, h0_ref, h1_ref,
             cw_ref, bg_ref, wglu_ref, wco_ref, wo_ref,
             dys_ref, dpb_ref, dwo_ref, dwglu_ref, dwco_ref, dcw_ref, dbg_ref, nxt):
        i = pl.program_id(0)
        tt = ni - 1 - i

        @pl.when(i == 0)
        def _():
            nxt[...] = jnp.zeros_like(nxt)
            dwo_ref[...] = jnp.zeros_like(dwo_ref)
            dwglu_ref[...] = jnp.zeros_like(dwglu_ref)
            dwco_ref[...] = jnp.zeros_like(dwco_ref)
            dcw_ref[...] = jnp.zeros_like(dcw_ref)
            dbg_ref[...] = jnp.zeros_like(dbg_ref)

        cw = [cw_ref[pl.ds(t, 1), :] for t in range(3)]
        prev_cin = h1_ref[:, dh:].astype(F32) * h0_ref[:, dh:].astype(F32)
        prev_cin = jnp.where(tt == 0, 0.0, prev_cin)
        ys5 = y_ref[...]
        f = _mix_tile(ys5, p0_ref[...], p1_ref[...], p2_ref[...], p3_ref[...], prev_cin,
                      cw, bg_ref[...], wglu_ref[...], wco_ref[...], d)
        dhb = dh_ref[...].astype(BF16)
        dmixed = _dot_nt(dhb, wo_ref[...])
        dwo_ref[...] += _dot_tn(f["mixed"].astype(BF16), dhb)

        g_s, g_c, sg = f["g_s"], f["g_c"], f["sg"]
        dy_ssm = dmixed * g_s
        dy_conv = dmixed * g_c
        dp2 = dmixed * f["y_ssm"] * g_s * (1.0 - g_s)
        dp3 = dmixed * f["y_conv"] * g_c * (1.0 - g_c)
        dbg_ref[:, pl.ds(0, d)] += jnp.sum(dp2, axis=0, keepdims=True)
        dbg_ref[:, pl.ds(d, d)] += jnp.sum(dp3, axis=0, keepdims=True)

        dz = jnp.concatenate([dy_ssm * sg, dy_ssm * f["z1"] * sg * (1.0 - sg)], axis=1).astype(BF16)
        dwglu_ref[...] += _dot_tn(f["gact"], dz)
        dys_ref[...] = (_dot_nt(dz, wglu_ref[...]) * _gelu_grad(ys5)).astype(BF16)

        dycb = dy_conv.astype(BF16)
        dwco_ref[...] += _dot_tn(f["cg"], dycb)
        dcg = _dot_nt(dycb, wco_ref[...])
        dgb = dcg * f["cv"]
        dcv = dcg * f["gb"]
        ext = jnp.concatenate([dcv, nxt[...]], axis=0)
        n1 = pltpu.roll(ext, tm + HALO - 1, 0)[:tm]
        n2 = pltpu.roll(ext, tm + HALO - 2, 0)[:tm]
        nxt[...] = dcv[:HALO, :]
        dcin = cw[2] * dcv + cw[1] * n1 + cw[0] * n2
        dcw_ref[pl.ds(0, 1), :] += jnp.sum(dcv * f["r2"], axis=0, keepdims=True)
        dcw_ref[pl.ds(1, 1), :] += jnp.sum(dcv * f["r1"], axis=0, keepdims=True)
        dcw_ref[pl.ds(2, 1), :] += jnp.sum(dcv * f["cin"], axis=0, keepdims=True)
        dgc = dcin * f["v"]
        dv = dcin * f["gc"]
        dpb_ref[0] = jnp.concatenate([jnp.zeros_like(dv), dv], axis=1).astype(BF16)
        dpb_ref[1] = jnp.concatenate([dgb, dgc], axis=1).astype(BF16)
        dpb_ref[2] = dp2.astype(BF16)
        dpb_ref[3] = dp3.astype(BF16)

    rev = lambda i: ni - 1 - i
    row = pl.BlockSpec((tm, d), lambda i: (rev(i), 0))
    half = pl.BlockSpec((tm, dh), lambda i: (rev(i), 0))
    full = lambda a: pl.BlockSpec(a.shape, lambda i: (0,) * a.ndim)
    pk = lambda k: pl.BlockSpec((None, tm, d), lambda i, k=k: (k, rev(i), 0))
    halo = lambda k: pl.BlockSpec((None, HALO, d), lambda i, k=k: (k, jnp.maximum(rev(i) * hb - 1, 0), 0))
    acc = lambda shape: pl.BlockSpec(shape, lambda i: (0,) * len(shape))
    return pl.pallas_call(
        body, name="mix_bwd", grid=(ni,),
        in_specs=[row, half, pk(0), pk(1), pk(2), pk(3), halo(0), halo(1),
                  full(cw), full(bgate), full(wglu), full(wco), full(wo)],
        out_specs=[half, pl.BlockSpec((4, tm, d), lambda i: (0, rev(i), 0)),
                   acc((d, d)), acc((dh, 2 * d)), acc((dh, d)), acc((SUBLANES, dh)), acc((1, 2 * d))],
        out_shape=[jax.ShapeDtypeStruct((tp, dh), BF16), jax.ShapeDtypeStruct((4, tp, d), BF16),
                   jax.ShapeDtypeStruct((d, d), F32), jax.ShapeDtypeStruct((dh, 2 * d), F32),
                   jax.ShapeDtypeStruct((dh, d), F32), jax.ShapeDtypeStruct((SUBLANES, dh), F32),
                   jax.ShapeDtypeStruct((1, 2 * d), F32)],
        scratch_shapes=[pltpu.VMEM((HALO, dh), F32)],
        compiler_params=_params(("arbitrary",)),
    )(dh2, ys5, p, p, p, p, p, p, cw, bgate, wglu, wco, wo)


ANY = pl.BlockSpec(memory_space=pl.ANY)


def _position():
    return lax.axis_index("x"), lax.axis_index("y"), lax.axis_index("c")


def _remote(src, dst, ssem, rsem, dev):
    return pltpu.make_async_remote_copy(src_ref=src, dst_ref=dst, send_sem=ssem, recv_sem=rsem,
                                        device_id=dev, device_id_type=MESH)


def _cast_pieces(ws, pos):
    n = len(ws)

    def body(pos_ref, *refs):
        for w_ref, o_ref in zip(refs[:n], refs[n:]):
            r4 = o_ref.shape[1]
            o_ref[0] = w_ref[pl.ds(0, r4), :].astype(BF16)
            o_ref[1] = w_ref[pl.ds(r4, r4), :].astype(BF16)

    halves = [(w.shape[0] // 2, w.shape[1]) for w in ws]
    return pl.pallas_call(
        body, name="cast_pieces",
        grid_spec=pltpu.PrefetchScalarGridSpec(
            num_scalar_prefetch=1, grid=(1,),
            in_specs=[pl.BlockSpec(hs, lambda i, pos: (pos[2], 0)) for hs in halves],
            out_specs=[pl.BlockSpec((None, None, None, 2, r2 // 2, cols),
                                    lambda i, pos: (pos[0], pos[1], pos[2], 0, 0, 0)) for r2, cols in halves]),
        out_shape=[jax.ShapeDtypeStruct((2, 2, 2, 2, r2 // 2, cols), BF16) for r2, cols in halves],
        compiler_params=_params(("arbitrary",)),
    )(pos, *ws)


class _Carry:
    def __init__(self, name, arrays, out_shapes, nsem, nlsem, make, fracs, n_inplace=0):
        self.name, self.arrays, self.out_shapes = name, list(arrays), list(out_shapes)
        self.nsem, self.nlsem, self.make, self.fracs = nsem, max(nlsem, 1), make, fracs
        self.n_inplace = n_inplace


def _carry_scratch(carry):
    return [pltpu.SemaphoreType.DMA((carry.nsem,)), pltpu.SemaphoreType.DMA((carry.nsem,)),
            pltpu.SemaphoreType.DMA((carry.nlsem,))]


def _run_carry(carry):
    na, no = len(carry.arrays), len(carry.out_shapes)

    def body(*refs):
        for phase in carry.make(refs[:na], refs[na:na + no], *refs[na + no:]):
            phase()

    return pl.pallas_call(
        body, name=carry.name, in_specs=[ANY] * na, out_specs=[ANY] * no, out_shape=carry.out_shapes,
        scratch_shapes=_carry_scratch(carry), input_output_aliases={i: i for i in range(carry.n_inplace)},
    )(*carry.arrays)


def _attach_carry(carry, in_specs, args, out_specs, out_shape, scratch):
    nhi, nho, nhs = len(in_specs), len(out_specs), len(scratch)
    if carry is None:
        none = lambda refs: (list(refs), [])
        none.aliases = {}
        return none
    na, no = len(carry.arrays), len(carry.out_shapes)
    in_specs += [ANY] * na
    args += carry.arrays
    out_specs += [ANY] * no
    out_shape += carry.out_shapes
    scratch += _carry_scratch(carry)

    def split(refs):
        refs = list(refs)
        o = nhi + na
        host = refs[:nhi] + refs[o:o + nho] + refs[o + nho + no:o + nho + no + nhs]
        sems = refs[o + nho + no + nhs:]
        return host, carry.make(refs[nhi:o], refs[o + nho:o + nho + no], *sems)

    split.aliases = {nhi + i: nho + i for i in range(carry.n_inplace)}
    split.carry_outs = lambda refs: list(refs)[nhi + na + nho:nhi + na + nho + no]
    return split


def _run_phases(phases, carry, step, total):
    for phase, frac in zip(phases, carry.fracs if carry is not None else ()):
        pl.when(step == int(round(frac * (total - 1))))(phase)


def _allgather_carry(name, walls, smalls, fracs=(0.0, 0.23, 0.73, 1.0)):
    n, ns = len(walls), len(smalls)
    per = 14
    n_big = per * n

    def make(ins, outs, ssem, rsem, lsem):
        sin = ins[n:]
        wall, sall = outs[:n], outs[n:]
        x, y, c = _position()
        xnb, ynb, sib = (1 - x, y, c), (x, 1 - y, c), (x, y, 1 - c)
        chips = [(1 - x, y), (x, 1 - y), (1 - x, 1 - y)]
        slot = lambda i, xx, yy, cc, h: wall[i].at[xx, yy, cc, h]
        own = lambda i, h: slot(i, x, y, c, h)
        cp = lambda src, dst, s, dev: _remote(src, dst, ssem.at[s], rsem.at[s], dev)
        to_sib = lambda i, xx, yy, h: cp(slot(i, xx, yy, c, h), slot(i, xx, yy, c, h),
                                         per * i + 6 + 4 * xx + 2 * yy + h, sib)

        def local():
            return [pltpu.make_async_copy(sin[i], sall[i].at[2 * x + y], lsem.at[i]) for i in range(ns)]

        def small(px, py, j, i, landing):
            s = n_big + j * ns + i
            return cp(sin[i], sall[i].at[landing], s, (px, py, c))

        def first_hop():
            for lc in local():
                lc.start()
            for j, (px, py) in enumerate(chips):
                for i in range(ns):
                    small(px, py, j, i, 2 * x + y).start()
            for i in range(n):
                cp(own(i, 0), slot(i, x, y, c, 0), per * i, xnb).start()
                cp(own(i, 1), slot(i, x, y, c, 1), per * i + 1, ynb).start()
                for h in range(2):
                    cp(own(i, h), slot(i, x, y, c, h), per * i + 6 + 4 * x + 2 * y + h, sib).start()

        def second_hop():
            for lc in local():
                lc.wait()
            for i in range(n):
                cp(slot(i, 1 - x, y, c, 0), slot(i, 1 - x, y, c, 0), per * i, xnb).wait_recv()
                cp(slot(i, x, 1 - y, c, 1), slot(i, x, 1 - y, c, 1), per * i + 1, ynb).wait_recv()
                for j in range(2):
                    cp(slot(i, j, y, c, 0), slot(i, j, y, c, 0), per * i + 2 + j, ynb).start()
                    cp(slot(i, x, j, c, 1), slot(i, x, j, c, 1), per * i + 4 + j, xnb).start()
                to_sib(i, 1 - x, y, 0).start()
                to_sib(i, x, 1 - y, 1).start()

        def last_to_sibling():
            for i in range(n):
                for j in range(2):
                    cp(slot(i, j, 1 - y, c, 0), slot(i, j, 1 - y, c, 0), per * i + 2 + j, ynb).wait_recv()
                    cp(slot(i, 1 - x, j, c, 1), slot(i, 1 - x, j, c, 1), per * i + 4 + j, xnb).wait_recv()
                    to_sib(i, j, 1 - y, 0).start()
                    to_sib(i, 1 - x, j, 1).start()

        def finish():
            for i in range(n):
                for xx in range(2):
                    for yy in range(2):
                        for h in range(2):
                            s = per * i + 6 + 4 * xx + 2 * yy + h
                            cp(slot(i, xx, yy, 1 - c, h), slot(i, xx, yy, 1 - c, h), s, sib).wait_recv()
                            to_sib(i, xx, yy, h).wait_send()
                cp(own(i, 0), slot(i, x, y, c, 0), per * i, xnb).wait_send()
                cp(own(i, 1), slot(i, x, y, c, 1), per * i + 1, ynb).wait_send()
                for j in range(2):
                    cp(slot(i, j, y, c, 0), slot(i, j, y, c, 0), per * i + 2 + j, ynb).wait_send()
                    cp(slot(i, x, j, c, 1), slot(i, x, j, c, 1), per * i + 4 + j, xnb).wait_send()
            for j, (px, py) in enumerate(chips):
                for i in range(ns):
                    small(px, py, j, i, 2 * px + py).wait_recv()
                    small(px, py, j, i, 2 * x + y).wait_send()

        return [first_hop, second_hop, last_to_sibling, finish]

    out_shapes = [jax.ShapeDtypeStruct(a.shape, a.dtype) for a in walls]
    out_shapes += [jax.ShapeDtypeStruct((4,) + a.shape, a.dtype) for a in smalls]
    return _Carry(name, list(walls) + list(smalls), out_shapes, n_big + 3 * ns, ns, make, fracs,
                  n_inplace=n)


def _assemble_rows(x2d, n_meta, tp, carry, meta_at):
    seq, d = x2d.shape
    tm = ROW_ALIGN
    ni = tp // tm

    def body(*refs):
        (x_hbm, o_ref, buf, sem), phases = split(refs)
        shards = split.carry_outs(refs)[meta_at]
        s = pl.program_id(0)
        i = ni - 1 - s
        _run_phases(phases, carry, s, ni)

        def fetch(lo, n, at):
            if at > 0:
                buf[pl.ds(0, at), :] = jnp.zeros((at, d), F32)
            if at + n < tm:
                buf[pl.ds(at + n, tm - at - n), :] = jnp.zeros((tm - at - n, d), F32)
            cp = pltpu.make_async_copy(x_hbm.at[pl.ds(lo, n), :], buf.at[pl.ds(at, n), :], sem)
            cp.start()
            cp.wait()

        _for_tile_rows(i, ni, tm, n_meta, seq, fetch)

        @pl.when(i == 0)
        def _():
            mc = d // 4
            for k in range(4):
                cp = pltpu.make_async_copy(shards.at[k], buf.at[pl.ds(0, n_meta), pl.ds(k * mc, mc)], sem)
                cp.start()
                cp.wait()

        o_ref[...] = buf[...]

    in_specs, args = [ANY], [x2d]
    out_specs = [pl.BlockSpec((tm, d), lambda s: (ni - 1 - s, 0))]
    out_shape = [jax.ShapeDtypeStruct((tp, d), F32)]
    scratch = [pltpu.VMEM((tm, d), F32), pltpu.SemaphoreType.DMA(())]
    split = _attach_carry(carry, in_specs, args, out_specs, out_shape, scratch)
    return pl.pallas_call(
        body, name="assemble_rows", grid=(ni,), in_specs=in_specs, out_specs=out_specs, out_shape=out_shape,
        scratch_shapes=scratch, compiler_params=_params(("arbitrary",)), input_output_aliases=split.aliases,
    )(*args)


def _exchange_carry(name, arrays, out_shapes, plan):
    count = plan([None] * len(arrays), [None] * len(out_shapes), None)

    def make(ins, outs, ssem, rsem, lsem):
        def copies():
            return [_remote(src, dst, ssem.at[j], rsem.at[j], peer)
                    for j, (src, dst, peer) in enumerate(plan(ins, outs, _position()))]

        def start():
            for c in copies():
                c.start()

        def wait():
            for c in copies():
                c.wait()

        return [start, wait]

    return _Carry(name, arrays, out_shapes, count, 0, make, (0.0, 1.0))


class _Grad:
    def __init__(self, arrs, kind, shard_shape):
        self.arrs, self.kind = list(arrs), kind
        self.rows, self.cols = shard_shape
        self.r2 = self.rows // 2

    def view(self, refs, k, h):
        r2 = self.r2
        if self.kind == "list":
            return refs[k].at[pl.ds(h * r2, r2), :]
        if self.kind == "stacked":
            return refs[0].at[k, pl.ds(h * r2, r2), :]
        if self.kind == "col":
            return refs[0].at[pl.ds(h * r2, r2), pl.ds(k * self.cols, self.cols)]
        return refs[0].at[pl.ds((2 * k + h) * r2, r2), :]

    def half_specs(self):
        r2, cols = self.r2, self.cols
        if self.kind == "list":
            return [pl.BlockSpec((r2, cols), lambda k, pos: (pos[2], 0))] * len(self.arrs)
        if self.kind == "stacked":
            return [pl.BlockSpec((None, r2, cols), lambda k, pos: (k, pos[2], 0))]
        if self.kind == "col":
            return [pl.BlockSpec((r2, cols), lambda k, pos: (pos[2], k))]
        return [pl.BlockSpec((r2, cols), lambda k, pos: (2 * k + pos[2], 0))]

    def step_bytes(self):
        return self.r2 * self.cols * (len(self.arrs) * self.arrs[0].dtype.itemsize + self.arrs[0].dtype.itemsize + 6)


def _add_halves(grads, recvs, pos):
    n = len(grads)
    counts = [len(g.arrs) for g in grads]
    n_mine = sum(counts)

    def body(pos_ref, *refs):
        o = 0
        for i in range(n):
            m_refs = refs[o:o + counts[i]]
            o += counts[i]
            r_ref, of_ref, ob_ref = refs[n_mine + i], refs[n_mine + n + i], refs[n_mine + 2 * n + i]
            mine = m_refs[0][...]
            for kk in range(1, counts[i]):
                mine = jnp.where(pl.program_id(0) == kk, m_refs[kk][...], mine)
            s = mine.astype(F32) + r_ref[...].astype(F32)
            of_ref[...] = s
            ob_ref[...] = s.astype(BF16)

    blks = [pl.BlockSpec((None, g.r2, g.cols), lambda k, pos: (k, 0, 0)) for g in grads]
    outs = pl.pallas_call(
        body, name="rs_add_c",
        grid_spec=pltpu.PrefetchScalarGridSpec(
            num_scalar_prefetch=1, grid=(4,),
            in_specs=[spec for g in grads for spec in g.half_specs()] + blks, out_specs=blks + blks),
        out_shape=[jax.ShapeDtypeStruct((4, g.r2, g.cols), F32) for g in grads]
        + [jax.ShapeDtypeStruct((4, g.r2, g.cols), BF16) for g in grads],
        compiler_params=_params(("arbitrary",)),
    )(pos, *[a for g in grads for a in g.arrs], *recvs)
    return list(zip(outs[:n], outs[n:]))


def _row_tile(rows, cols):
    fits = [t for t in range(16, rows + 1, 16) if rows % t == 0 and t * cols * 4 <= 2 * 1024 * 1024]
    return max(fits) if fits else rows


def _adamw_math(w, g, m, v):
    m = ADAM_B1 * m + (1.0 - ADAM_B1) * g
    v = ADAM_B2 * v + (1.0 - ADAM_B2) * (g * g)
    m_hat = m / (1.0 - ADAM_B1 ** ADAM_STEP)
    v_hat = v / (1.0 - ADAM_B2 ** ADAM_STEP)
    delta = -ADAM_LR * (m_hat / (jnp.sqrt(v_hat) + ADAM_EPS) + ADAM_WD * w)
    return delta, m, v


def _adamw_big(w, m, v, own, sib, pos):
    rows, cols = w.shape
    r2 = rows // 2

    tr = _row_tile(r2, cols)
    nt = r2 // tr

    def body(pos_ref, w_ref, m_ref, v_ref, own_ref, sib_ref, g_ref, d_ref, nm_ref, nv_ref):
        h = pl.program_id(0)
        g = jnp.where(h == pos_ref[2], own_ref[...], sib_ref[...])
        g_ref[...] = g
        d_ref[...], nm_ref[...], nv_ref[...] = _adamw_math(w_ref[...], g, m_ref[...], v_ref[...])

    half = pl.BlockSpec((tr, cols), lambda h, t, pos: (h * nt + t, 0))
    piece = pl.BlockSpec((tr, cols), lambda h, t, pos: (t, 0))
    out = jax.ShapeDtypeStruct((rows, cols), F32)
    return pl.pallas_call(
        body, name="adamw",
        grid_spec=pltpu.PrefetchScalarGridSpec(
            num_scalar_prefetch=1, grid=(2, nt),
            in_specs=[half, half, half, piece, piece],
            out_specs=[half, half, half, half]),
        out_shape=[out, out, out, out],
        compiler_params=_params(("arbitrary", "arbitrary")),
    )(pos, w, m, v, own, sib)


def _add_hop1(s1fs, recvs, pos):
    n = len(s1fs)
    s1vs = [s.reshape((4, 2) + r.shape[2:]) for s, r in zip(s1fs, recvs)]

    def body(pos_ref, *refs):
        for m_ref, r_ref, of_ref, ob_ref in zip(refs[:n], refs[n:2 * n], refs[2 * n:3 * n], refs[3 * n:]):
            s = m_ref[...] + r_ref[...].astype(F32)
            of_ref[...] = s
            ob_ref[...] = s.astype(BF16)

    def mine(h, j, pos):
        return (jnp.where(h == 0, 2 * j + pos[1], 2 * pos[0] + j), h, 0, 0)

    tile = lambda r: (None, None) + r.shape[2:]
    blks = [pl.BlockSpec(tile(r), lambda h, j, pos: (h, j, 0, 0)) for r in recvs]
    outs = pl.pallas_call(
        body, name="rs_add_1",
        grid_spec=pltpu.PrefetchScalarGridSpec(
            num_scalar_prefetch=1, grid=(2, 2),
            in_specs=[pl.BlockSpec(tile(r), mine) for r in recvs] + blks, out_specs=blks + blks),
        out_shape=[jax.ShapeDtypeStruct(r.shape, F32) for r in recvs]
        + [jax.ShapeDtypeStruct(r.shape, BF16) for r in recvs],
        compiler_params=_params(("arbitrary", "arbitrary")),
    )(pos, *s1vs, *recvs)
    return list(zip(outs[:n], outs[n:]))


def _own_sum(s2fs, recvs, pos):
    n = len(s2fs)

    def body(pos_ref, *refs):
        for s_ref, r_ref, o_ref in zip(refs[:n], refs[n:2 * n], refs[2 * n:]):
            o_ref[...] = s_ref[...] + r_ref[...].astype(F32)

    blks = [pl.BlockSpec((None,) + r.shape[1:], lambda h, pos: (h, 0, 0)) for r in recvs]
    return pl.pallas_call(
        body, name="own_sum",
        grid_spec=pltpu.PrefetchScalarGridSpec(
            num_scalar_prefetch=1, grid=(2,),
            in_specs=[pl.BlockSpec((None, None) + r.shape[1:],
                                   lambda h, pos: (h, jnp.where(h == 0, pos[0], pos[1]), 0, 0)) for r in recvs]
            + blks, out_specs=blks),
        out_shape=[jax.ShapeDtypeStruct(r.shape, F32) for r in recvs],
        compiler_params=_params(("arbitrary",)),
    )(pos, *s2fs, *recvs)


def _add_small(a, b):
    def body(a_ref, b_ref, o_ref):
        o_ref[...] = a_ref[...] + b_ref[...]

    vm = pl.BlockSpec(memory_space=pltpu.VMEM)
    return pl.pallas_call(body, name="add_small", in_specs=[vm, vm], out_specs=vm,
                          out_shape=jax.ShapeDtypeStruct(a.shape, F32))(a, b)


def _adamw_small(ws, gs, ms, vs):
    n = len(ws)

    def body(*refs):
        for i in range(n):
            w_ref, g_ref, m_ref, v_ref, d_ref, nm_ref, nv_ref = (refs[j * n + i] for j in range(7))
            d_ref[...], nm_ref[...], nv_ref[...] = _adamw_math(w_ref[...], g_ref[...], m_ref[...], v_ref[...])

    vm = pl.BlockSpec(memory_space=pltpu.VMEM)
    outs = pl.pallas_call(body, name="adamw_small", in_specs=[vm] * (4 * n), out_specs=[vm] * (3 * n),
                          out_shape=[jax.ShapeDtypeStruct(w.shape, F32) for w in ws] * 3)(*ws, *gs, *ms, *vs)
    return outs[:n], outs[n:2 * n], outs[2 * n:]


class _ReduceScatter:
    def __init__(self, tag, grads, pos, extra=None):
        self.tag, self.grads, self.pos, self.stage, self.extra = tag, grads, pos, 0, extra

    def carry(self):
        grads, n = self.grads, len(self.grads)
        r4 = [g.r2 // 2 for g in grads]

        first = [sum(len(g.arrs) for g in grads[:i]) for i in range(n)]

        def plan_c(ins, outs, p):
            if p is None:
                return 4 * n
            x, y, c = p
            mine = lambda i: ins[first[i]:first[i] + len(grads[i].arrs)]
            return [(grads[i].view(mine(i), k, 1 - c), outs[i].at[k], (x, y, 1 - c))
                    for i in range(n) for k in range(4)]

        def plan_1(ins, outs, p):
            if p is None:
                return 4 * n
            x, y, c = p
            copies = []
            for i in range(n):
                for j in range(2):
                    copies.append((ins[i].at[2 * j + (1 - y), pl.ds(0, r4[i]), :], outs[i].at[0, j],
                                   (x, 1 - y, c)))
                    copies.append((ins[i].at[2 * (1 - x) + j, pl.ds(r4[i], r4[i]), :], outs[i].at[1, j],
                                   (1 - x, y, c)))
            return copies

        def plan_2(ins, outs, p):
            if p is None:
                return 2 * n
            x, y, c = p
            copies = []
            for i in range(n):
                copies.append((ins[i].at[0, 1 - x], outs[i].at[0], (1 - x, y, c)))
                copies.append((ins[i].at[1, 1 - y], outs[i].at[1], (x, 1 - y, c)))
            return copies

        def plan_s(ins, outs, p):
            if p is None:
                return n
            x, y, c = p
            return [(ins[i], outs[i], (x, y, 1 - c)) for i in range(n)]

        shape = lambda lead, dt: [jax.ShapeDtypeStruct(lead(g) + (g.cols,), dt) for g in grads]
        stage = self.stage
        if stage == 0:
            name, arrays, plan = "exchange_c", [a for g in grads for a in g.arrs], plan_c
            shapes = [jax.ShapeDtypeStruct((4, g.r2, g.cols), g.arrs[0].dtype) for g in grads]
        elif stage == 1:
            name, arrays, plan = "exchange_1", [s[1] for s in self.s1], plan_1
            shapes = shape(lambda g: (2, 2, g.r2 // 2), BF16)
        elif stage == 2:
            name, arrays, plan = "exchange_2", [s[1] for s in self.s2], plan_2
            shapes = shape(lambda g: (2, g.r2 // 2), BF16)
        else:
            name, arrays, plan, shapes = "exchange_sibling", self.own, plan_s, shape(lambda g: (g.r2,), F32)
        if self.extra is not None and stage < 3:
            def with_extra(ins, outs, p, plan=plan):
                if p is None:
                    return plan(ins[:-1], outs[:-1], None) + 1
                x, y, c = p
                peer = [(x, y, 1 - c), (x, 1 - y, c), (1 - x, y, c)][stage]
                return plan(ins[:-1], outs[:-1], p) + [(ins[-1], outs[-1], peer)]

            arrays = arrays + [self.extra]
            shapes = shapes + [jax.ShapeDtypeStruct(self.extra.shape, F32)]
            plan = with_extra
        return _exchange_carry(f"rs_{self.tag}_{name}", arrays, shapes, plan)

    def feed(self, recv):
        grads, pos = self.grads, self.pos
        recv = list(recv)
        if self.extra is not None and self.stage < 3:
            self.extra = _add_small(self.extra, recv.pop())
        if self.stage == 0:
            self.s1, start = [], 0
            while start < len(grads):
                end, size = start, 0
                while end < len(grads) and (end == start or size + grads[end].step_bytes() <= ADD_GROUP_BYTES):
                    size += grads[end].step_bytes()
                    end += 1
                self.s1 += _add_halves(grads[start:end], recv[start:end], pos)
                start = end
        elif self.stage == 1:
            self.s2 = _add_hop1([s[0] for s in self.s1], list(recv), pos)
        elif self.stage == 2:
            own = _own_sum([s[0] for s in self.s2], list(recv), pos)
            self.own = [o.reshape(g.r2, g.cols) for g, o in zip(grads, own)]
        else:
            self.sib = list(recv)
        self.stage += 1

    def run(self):
        while self.stage < 4:
            self.feed(_run_carry(self.carry()))

    def adamw(self, weights):
        return [_adamw_big(w, m, v, o, sb, self.pos) for (w, m, v), o, sb in zip(weights, self.own, self.sib)]


def _block_diag(t, nb):
    g, c, p = t.shape
    gb = g // nb
    t = t.reshape(nb, gb, c, p)
    eye = jnp.eye(gb, dtype=t.dtype)
    return jnp.einsum("bgcp,gh->bgchp", t, eye).reshape(nb, gb * c, gb * p)


def _s5_discretise(a_re, a_im, log_dt, b_re, b_im, c_re, c_im):
    g, p = a_re.shape
    nb = g // GROUPS_PER_BLOCK
    dt = jnp.exp(log_dt)[:, None]
    mag = jnp.exp(a_re * dt)
    lam_re = mag * jnp.cos(a_im * dt)
    lam_im = mag * jnp.sin(a_im * dt)
    den = a_re * a_re + a_im * a_im
    q_re = ((lam_re - 1.0) * a_re + lam_im * a_im) / den
    q_im = (lam_im * a_re - (lam_re - 1.0) * a_im) / den
    bb_re = q_re[..., None] * b_re - q_im[..., None] * b_im
    bb_im = q_re[..., None] * b_im + q_im[..., None] * b_re
    tr = lambda t: jnp.swapaxes(t, 1, 2)
    mb = jnp.concatenate([_block_diag(tr(bb_re), nb), _block_diag(tr(bb_im), nb)], axis=-1)
    mc = jnp.concatenate([_block_diag(c_re, nb), -_block_diag(c_im, nb)], axis=-1)
    lam = jnp.concatenate([lam_re.reshape(nb, -1), lam_im.reshape(nb, -1)], axis=-1)
    return mb, mc, lam


def _s5_powers(a_re, a_im, log_dt, sub):
    g, p = a_re.shape
    nb = g // GROUPS_PER_BLOCK
    dt = jnp.exp(log_dt)[:, None]
    ns = list(range(1, sub + 1)) + [sub << m for m in range(1, SCAN_SEQS.bit_length() - 1)]
    ns += [0] * (-len(ns) % SUBLANES)
    e = jnp.asarray(ns, F32)[:, None, None]
    mag = jnp.exp(a_re[None] * dt[None] * e)
    ang = a_im[None] * dt[None] * e
    re = (mag * jnp.cos(ang)).reshape(len(ns), nb, -1)
    im = (mag * jnp.sin(ang)).reshape(len(ns), nb, -1)
    return jnp.transpose(jnp.concatenate([re, im], axis=-1), (1, 0, 2))


def _pack(parts):
    flat = jnp.concatenate([a.reshape(-1).astype(F32) for a in parts])
    n = flat.shape[0]
    pad = -n % (SUBLANES * LANES)
    return jnp.pad(flat, (0, pad)).reshape(-1, LANES)


def _unpack(buf, like):
    flat = buf.reshape(-1)
    out, o = [], 0
    for a in like:
        out.append(flat[o:o + a.size].reshape(a.shape))
        o += a.size
    return out


def kernel(x, meta_tokens, g_ffn1, ffn1_w_gate, ffn1_w_up, ffn1_w_down, g_mix, w_in, b_gate, ssm_a_re, ssm_a_im, ssm_log_dt, ssm_b_re, ssm_b_im, ssm_c_re, ssm_c_im, ssm_d, ssm_w_glu, conv_w, conv_w_out, w_o, g_ffn2, ffn2_w_gate, ffn2_w_up, ffn2_w_down, g_final, loss_target, m_meta_tokens, m_g_ffn1, m_ffn1_w_gate, m_ffn1_w_up, m_ffn1_w_down, m_g_mix, m_w_in, m_b_gate, m_ssm_a_re, m_ssm_a_im, m_ssm_log_dt, m_ssm_b_re, m_ssm_b_im, m_ssm_c_re, m_ssm_c_im, m_ssm_d, m_ssm_w_glu, m_conv_w, m_conv_w_out, m_w_o, m_g_ffn2, m_ffn2_w_gate, m_ffn2_w_up, m_ffn2_w_down, m_g_final, v_meta_tokens, v_g_ffn1, v_ffn1_w_gate, v_ffn1_w_up, v_ffn1_w_down, v_g_mix, v_w_in, v_b_gate, v_ssm_a_re, v_ssm_a_im, v_ssm_log_dt, v_ssm_b_re, v_ssm_b_im, v_ssm_c_re, v_ssm_c_im, v_ssm_d, v_ssm_w_glu, v_conv_w, v_conv_w_out, v_w_o, v_g_ffn2, v_ffn2_w_gate, v_ffn2_w_up, v_ffn2_w_down, v_g_final):
    seq, d = x.shape[1], x.shape[2]
    n_meta = meta_tokens.shape[0]
    dh = d // 2
    tp = -(-(n_meta + seq) // ROW_ALIGN) * ROW_ALIGN
    mx, my, mc_ = _position()
    pos = jnp.stack([mx, my, mc_]).astype(jnp.int32)
    shard = 2 * mx + my

    big_names = ["ffn1_w_gate", "ffn1_w_up", "ffn1_w_down", "w_in", "ssm_w_glu", "conv_w_out", "w_o",
                 "ffn2_w_gate", "ffn2_w_up", "ffn2_w_down"]
    transposed = {0, 1, 7, 8}
    drop = lambda arrs: [jnp.swapaxes(a.reshape(a.shape[1:]), 0, 1) if i in transposed else a.reshape(a.shape[1:])
                         for i, a in enumerate(arrs)]
    big_w = drop([ffn1_w_gate, ffn1_w_up, ffn1_w_down, w_in, ssm_w_glu, conv_w_out, w_o,
                  ffn2_w_gate, ffn2_w_up, ffn2_w_down])
    big_m = drop([m_ffn1_w_gate, m_ffn1_w_up, m_ffn1_w_down, m_w_in, m_ssm_w_glu, m_conv_w_out,
                  m_w_o, m_ffn2_w_gate, m_ffn2_w_up, m_ffn2_w_down])
    big_v = drop([v_ffn1_w_gate, v_ffn1_w_up, v_ffn1_w_down, v_w_in, v_ssm_w_glu, v_conv_w_out,
                  v_w_o, v_ffn2_w_gate, v_ffn2_w_up, v_ffn2_w_down])
    pieces = _cast_pieces(big_w[:3], pos) + _cast_pieces(big_w[3:], pos)
    conv_local = conv_w.reshape(conv_w.shape[1], conv_w.shape[3])
    n_first = 3
    h0, *first = _assemble_rows(
        x.reshape(seq, d), n_meta, tp,
        _allgather_carry("allgather_first", pieces[:n_first], [meta_tokens, conv_local],
                         fracs=(0.0, 0.9, 0.97, 1.0)), meta_at=n_first)
    smalls = first[n_first:]
    stack4 = lambda wl: wl.reshape((4, -1, wl.shape[-1]))
    w1g, w1u, w1d = [stack4(wl) for wl in first[:n_first]]
    natural_cols = lambda s: jnp.transpose(s, (1, 0, 2)).reshape(s.shape[1], 4 * s.shape[2])
    cw_full = natural_cols(smalls[1])
    cw_pad = jnp.pad(cw_full, ((0, SUBLANES - cw_full.shape[0]), (0, 0)))

    s5_args = (ssm_a_re[0], ssm_a_im[0], ssm_log_dt[0], ssm_b_re[0], ssm_b_im[0], ssm_c_re[0], ssm_c_im[0])
    (mb, mc, _), disc_vjp = jax.vjp(_s5_discretise, *s5_args)
    powt = _s5_powers(ssm_a_re[0], ssm_a_im[0], ssm_log_dt[0], SCAN_TILE // SCAN_SEQS)
    mb16, mc16 = mb.astype(BF16), mc.astype(BF16)

    h1, a1, b1, n1, *mid = _ffn_fwd(h0, g_ffn1, w1g, w1u, w1d, "ffn1_fwd",
                                    carry=_allgather_carry("allgather_mixer", pieces[3:7], []))
    win_all, wglu_s, wco_s, wo_s = [stack4(wl) for wl in mid]
    wglu_all = natural_cols(wglu_s)
    wco_all = natural_cols(wco_s)
    wo_all = wo_s.reshape(d, d)
    u, p, w2g, w2u = _win_fwd(h1, g_mix, win_all, carry=_allgather_carry("allgather_ffn2_in", pieces[7:9], []))
    ys5, bnd = _scan_fwd(p, mb16, mc16, powt, ssm_d)
    h2, w2d = _mix_fwd(h1, ys5, p, cw_pad, b_gate, wglu_all, wco_all, wo_all,
                       carry=_allgather_carry("allgather_ffn2_out", pieces[9:], []))
    w2g, w2u, w2d = stack4(w2g), stack4(w2u), stack4(w2d)
    dh3, a2, b2, n2, dg_final, loss_part, dy3 = _ffn_fwd(
        h2, g_ffn2, w2g, w2u, w2d, "ffn2_fwd_loss",
        final=(g_final.reshape(1, d), loss_target.reshape(seq, d), n_meta, seq))

    dh2, dw2g, dw2u, dw2d, dg_ffn2 = _ffn_bwd(dh3, dy3, h2, n2, g_ffn2, a2, b2, w2g, w2u, w2d, "ffn2_bwd")
    dys5, dpb, dwo, dwglu, dwco, dcw, dbg = _mix_bwd(dh2, ys5, p, cw_pad, b_gate, wglu_all, wco_all, wo_all)
    dug, dmb, dmc, dlam, dd = _scan_bwd(p, dys5, mb16, mc16, powt, ssm_d, bnd)
    dh1, dwin, dg_mix, dy1 = _win_bwd(dpb, dug, u, win_all, h1, g_mix, dh2)
    shapes = [w.shape for w in big_w]
    kinds = ["list", "list", "list", "list", "col", "col", "row", "list", "list", "list"]
    rest_grads = [dwin, [dwglu], [dwco], [dwo], dw2g, dw2u, dw2d]
    rs_rest = _ReduceScatter("rest", [_Grad(a, k, s) for a, k, s in
                                      zip(rest_grads, kinds[n_first:], shapes[n_first:])], pos)
    grad_x, dw1g, dw1u, dw1d, dg_ffn1, grad_meta = _ffn_bwd(
        dh1, dy1, h0, n1, g_ffn1, a1, b1, w1g, w1u, w1d, "ffn1_bwd", chain=rs_rest, unpad=(n_meta, seq))
    s5_grads = disc_vjp((dmb, dmc, jnp.sum(dlam, axis=1)))
    local_small = [dg_ffn1, dg_mix, dbg, *s5_grads, jnp.sum(dd, axis=0), dg_ffn2, dg_final,
                   grad_meta, dcw[:conv_w.shape[1]]]
    rs_first = _ReduceScatter("first", [_Grad(a, k, s) for a, k, s in
                                        zip([dw1g, dw1u, dw1d], kinds[:n_first], shapes[:n_first])], pos,
                              extra=_pack(local_small))
    rs_first.run()
    wmv = list(zip(big_w, big_m, big_v))
    big_out = rs_first.adamw(wmv[:n_first]) + rs_rest.adamw(wmv[n_first:])
    def lead(i, o):
        o = jnp.swapaxes(o, 0, 1) if i in transposed else o
        return o.reshape((1,) + o.shape)

    big_out = {nme: tuple(lead(i, o) for o in outs) for i, (nme, outs) in enumerate(zip(big_names, big_out))}

    grad_x = grad_x.reshape(1, seq, d)

    small_names = ["g_ffn1", "g_mix", "b_gate", "ssm_a_re", "ssm_a_im", "ssm_log_dt", "ssm_b_re", "ssm_b_im",
                   "ssm_c_re", "ssm_c_im", "ssm_d", "g_ffn2", "g_final", "meta_tokens", "conv_w"]
    small_w = [g_ffn1, g_mix, b_gate, ssm_a_re, ssm_a_im, ssm_log_dt, ssm_b_re, ssm_b_im, ssm_c_re, ssm_c_im,
               ssm_d, g_ffn2, g_final, meta_tokens, conv_w]
    small_m = [m_g_ffn1, m_g_mix, m_b_gate, m_ssm_a_re, m_ssm_a_im, m_ssm_log_dt, m_ssm_b_re, m_ssm_b_im,
               m_ssm_c_re, m_ssm_c_im, m_ssm_d, m_g_ffn2, m_g_final, m_meta_tokens, m_conv_w]
    small_v = [v_g_ffn1, v_g_mix, v_b_gate, v_ssm_a_re, v_ssm_a_im, v_ssm_log_dt, v_ssm_b_re, v_ssm_b_im,
               v_ssm_c_re, v_ssm_c_im, v_ssm_d, v_g_ffn2, v_g_final, v_meta_tokens, v_conv_w]
    reduced = _unpack(rs_first.extra, local_small)
    reduced[-2] = lax.dynamic_slice_in_dim(reduced[-2], shard * meta_tokens.shape[1], meta_tokens.shape[1], 1)
    reduced[-1] = lax.dynamic_slice_in_dim(reduced[-1], shard * conv_w.shape[3], conv_w.shape[3], 1)
    small_g = [r.reshape(w.shape) for r, w in zip(reduced, small_w)]
    two_d = lambda arrs: [a.reshape(1, -1) if a.ndim == 1 else a for a in arrs]
    ds_, nm_, nv_ = _adamw_small(two_d(small_w), two_d(small_g), two_d(small_m), two_d(small_v))
    like = lambda outs: [o.reshape(w.shape) for o, w in zip(outs, small_w)]
    small_out = {nme: o for nme, o in zip(small_names, zip(small_g, like(ds_), like(nm_), like(nv_)))}

    loss = lax.psum(loss_part[0, 0], ("x", "y", "c"))
    order = ["meta_tokens", "g_ffn1", "ffn1_w_gate", "ffn1_w_up", "ffn1_w_down", "g_mix", "w_in", "b_gate",
             "ssm_a_re", "ssm_a_im", "ssm_log_dt", "ssm_b_re", "ssm_b_im", "ssm_c_re", "ssm_c_im", "ssm_d",
             "ssm_w_glu", "conv_w", "conv_w_out", "w_o", "g_ffn2", "ffn2_w_gate", "ffn2_w_up", "ffn2_w_down",
             "g_final"]
    res = {**big_out, **small_out}
    return (loss, grad_x, *[res[nme][0] for nme in order], *[res[nme][1] for nme in order],
            *[res[nme][2] for nme in order], *[res[nme][3] for nme in order])
```

```python
import functools
import math

import jax
import jax.numpy as jnp
from jax import lax
from jax.experimental import pallas as pl
from jax.experimental.pallas import tpu as pltpu

F32 = jnp.float32
BF16 = jnp.bfloat16
MESH = pl.DeviceIdType.MESH

RMS_EPS = 1e-6
ADAM_LR = 0.001
ADAM_B1 = 0.9
ADAM_B2 = 0.999
ADAM_EPS = 1e-08
ADAM_WD = 0.01
ADAM_STEP = 10

LANES = 128
SUBLANES = 8
VMEM_LIMIT = 56 * 1024 * 1024
ADD_GROUP_BYTES = VMEM_LIMIT // 3

ROW_ALIGN = 256
SCAN_TILE = 256
SCAN_SEQS = 16
GROUPS_PER_BLOCK = 8


def _params(sem, vmem=VMEM_LIMIT):
    return pltpu.CompilerParams(dimension_semantics=sem, vmem_limit_bytes=vmem)


def _pick_tile(n, candidates):
    for c in candidates:
        if n % c == 0:
            return c
    raise ValueError(f"no tile for {n}")


def _dot(a, b):
    return jnp.dot(a, b, preferred_element_type=F32)


def _dot_nt(a, b):
    return lax.dot_general(a, b, (((1,), (1,)), ((), ())), preferred_element_type=F32)


def _dot_tn(a, b):
    return lax.dot_general(a, b, (((0,), (0,)), ((), ())), preferred_element_type=F32)


def _sigmoid(x):
    return pl.reciprocal(1.0 + jnp.exp(-x), approx=True)


def _rms_stats(h):
    r = lax.rsqrt(jnp.mean(h * h, axis=-1, keepdims=True) + RMS_EPS)
    return h * r, r


def _rms_bwd(xhat, r, g, dn):
    dxh = dn * g
    return r * (dxh - xhat * jnp.mean(dxh * xhat, axis=-1, keepdims=True))


GELU_K = math.sqrt(2.0 / math.pi)
GELU_C = 0.044715


def _gelu(x):
    return 0.5 * x * (1.0 + jnp.tanh(GELU_K * (x + GELU_C * x * x * x)))


def _gelu_grad(x):
    t = jnp.tanh(GELU_K * (x + GELU_C * x * x * x))
    return 0.5 * (1.0 + t) + 0.5 * x * (1.0 - t * t) * GELU_K * (1.0 + 3.0 * GELU_C * x * x)


def _for_tile_rows(i, ni, tm, n_meta, seq, fn):
    pl.when(i == 0)(lambda: fn(0, min(tm - n_meta, seq), n_meta))
    if ni > 1:
        last_lo = (ni - 1) * tm - n_meta
        pl.when(i == ni - 1)(lambda: fn(last_lo, min(seq - last_lo, tm), 0))
    if ni > 2:
        pl.when((i > 0) & (i < ni - 1))(lambda: fn(pl.multiple_of(i * tm - n_meta, SUBLANES), tm, 0))


def _ffn_fwd(h, g, wg, wu, wd, name, final=None, carry=None):
    tp, d = h.shape
    ns, f4, _ = wg.shape
    tm = _pick_tile(tp, (768, 512, 256))
    ni = tp // tm

    def body(*refs):
        refs, phases = split(refs)
        if final is None:
            h_ref, g_ref, wg_ref, wu_ref, wd_ref, ho_ref, a_ref, b_ref, n_scr, acc = refs
        else:
            (h_ref, g_ref, wg_ref, wu_ref, wd_ref, gf_ref, tg_hbm,
             ho_ref, a_ref, b_ref, n_scr, dgf_ref, loss_ref, dy_ref, acc, tg_ref, tg_sem) = refs
        i = pl.program_id(0)
        k = pl.program_id(1)
        _run_phases(phases, carry, i * ns + k, ni * ns)

        if final is not None:
            def target_rows(lo, n, at):
                return pltpu.make_async_copy(tg_hbm.at[pl.ds(lo, n), :], tg_ref.at[pl.ds(at, n), :], tg_sem)

            def fetch_target(lo, n, at):
                if at > 0:
                    tg_ref[pl.ds(0, at), :] = jnp.zeros((at, d), F32)
                if at + n < tm:
                    tg_ref[pl.ds(at + n, tm - at - n), :] = jnp.zeros((tm - at - n, d), F32)
                target_rows(lo, n, at).start()

            pl.when(k == 0)(lambda: _for_tile_rows(i, ni, tm, final[2], final[3], fetch_target))

        @pl.when(k == 0)
        def _():
            xhat, _ = _rms_stats(h_ref[...])
            n_scr[...] = (xhat * g_ref[...]).astype(BF16)
            acc[...] = jnp.zeros_like(acc)

        n = n_scr[...]
        a = _dot_nt(n, wg_ref[...])
        b = _dot_nt(n, wu_ref[...])
        a_ref[...] = a.astype(BF16)
        b_ref[...] = b.astype(BF16)
        s = (a * _sigmoid(a) * b).astype(BF16)
        acc[...] += _dot(s, wd_ref[...])

        if final is None:
            @pl.when(k == ns - 1)
            def _():
                ho_ref[...] = h_ref[...] + 0.5 * acc[...]
        else:
            n_meta, seq = final[2], final[3]

            @pl.when((i == 0) & (k == 0))
            def _():
                dgf_ref[...] = jnp.zeros_like(dgf_ref)
                loss_ref[...] = jnp.zeros_like(loss_ref)

            @pl.when(k == ns - 1)
            def _():
                _for_tile_rows(i, ni, tm, n_meta, seq, lambda lo, n, at: target_rows(lo, n, at).wait())
                h3 = h_ref[...] + 0.5 * acc[...]
                xhat, r = _rms_stats(h3)
                gf = gf_ref[...]
                row = i * tm + lax.broadcasted_iota(jnp.int32, (tm, d), 0)
                valid = (row >= n_meta) & (row < n_meta + seq)
                diff = jnp.where(valid, xhat * gf - tg_ref[...], 0.0)
                dout = diff * (1.0 / d)
                loss_ref[...] += jnp.full(loss_ref.shape, 0.5 * jnp.sum(diff * diff) * (1.0 / d), F32)
                dgf_ref[...] += jnp.sum(dout * xhat, axis=0, keepdims=True)
                dh3 = _rms_bwd(xhat, r, gf, dout)
                ho_ref[...] = dh3
                dy_ref[...] = (0.5 * dh3).astype(BF16)

    row_spec = pl.BlockSpec((tm, d), lambda i, k: (i, 0))
    vec_spec = pl.BlockSpec((1, d), lambda i, k: (0, 0))
    in_specs = [row_spec, vec_spec,
                pl.BlockSpec((None, f4, d), lambda i, k: (k, 0, 0)),
                pl.BlockSpec((None, f4, d), lambda i, k: (k, 0, 0)),
                pl.BlockSpec((None, f4, d), lambda i, k: (k, 0, 0))]
    act_spec = pl.BlockSpec((None, tm, f4), lambda i, k: (k, i, 0))
    out_specs = [row_spec, act_spec, act_spec, row_spec]
    out_shape = [jax.ShapeDtypeStruct((tp, d), F32),
                 jax.ShapeDtypeStruct((ns, tp, f4), BF16),
                 jax.ShapeDtypeStruct((ns, tp, f4), BF16),
                 jax.ShapeDtypeStruct((tp, d), BF16)]
    args = [h, g, wg, wu, wd]
    scratch = [pltpu.VMEM((tm, d), F32)]
    if final is not None:
        in_specs += [vec_spec, ANY]
        args += [final[0], final[1]]
        out_specs += [vec_spec, pl.BlockSpec((1, LANES), lambda i, k: (0, 0)), row_spec]
        out_shape += [jax.ShapeDtypeStruct((1, d), F32), jax.ShapeDtypeStruct((1, LANES), F32),
                      jax.ShapeDtypeStruct((tp, d), BF16)]
        scratch += [pltpu.VMEM((tm, d), F32), pltpu.SemaphoreType.DMA(())]
    split = _attach_carry(carry, in_specs, args, out_specs, out_shape, scratch)
    return pl.pallas_call(
        body, name=name, grid=(ni, ns), in_specs=in_specs, out_specs=out_specs, out_shape=out_shape,
        scratch_shapes=scratch, compiler_params=_params(("arbitrary", "arbitrary")),
        input_output_aliases=split.aliases,
    )(*args)


def _ffn_bwd_shard(k, ns, dn_prev, dy, n, a, b, wg, wu, wd, tail, name, carry=None, unpad=None):
    tp, d = n.shape
    f4 = wg.shape[1]
    tm = _pick_tile(tp, (768, 512, 256))
    ni = tp // tm
    first, last = k == 0, k == ns - 1
    unpad = unpad if last else None

    def body(*refs):
        refs, phases = split(refs)
        acc_in = None if first else refs.pop(0)
        if last:
            dh_ref, h_ref, g_ref = refs[:3]
        else:
            dy_ref, n_ref = refs[:2]
        refs = refs[3 if last else 2:]
        a_ref, b_ref, wg_hbm, wu_hbm, wd_hbm = refs[:5]
        refs = refs[5:]
        acc_out, dwg_hbm, dwu_hbm, dwd_hbm = refs[:4]
        rest = refs[4:]
        dg_ref = rest.pop(0) if last else None
        head_ref = rest.pop(0) if unpad else None
        wg_ref, wu_ref, wd_ref, dwg_ref, dwu_ref, dwd_ref, wsem, da_scr, db_scr, s_scr = rest[:10]
        i = pl.program_id(0)
        _run_phases(phases, carry, i, ni)
        if unpad:
            res_ref, res_sem = rest[10:]

            def real_rows(lo, cnt, at):
                return pltpu.make_async_copy(res_ref.at[pl.ds(at, cnt), :], acc_out.at[pl.ds(lo, cnt), :], res_sem)

            def wait_tile(tile):
                _for_tile_rows(tile, ni, tm, *unpad, lambda lo, cnt, at: real_rows(lo, cnt, at).wait())

        @pl.when(i == 0)
        def _():
            loads = [pltpu.make_async_copy(src.at[k], dst, wsem.at[j])
                     for j, (src, dst) in enumerate(((wg_hbm, wg_ref), (wu_hbm, wu_ref), (wd_hbm, wd_ref)))]
            for cp in loads:
                cp.start()
            dwg_ref[...] = jnp.zeros_like(dwg_ref)
            dwu_ref[...] = jnp.zeros_like(dwu_ref)
            dwd_ref[...] = jnp.zeros_like(dwd_ref)
            if last:
                dg_ref[...] = jnp.zeros_like(dg_ref)
            for cp in loads:
                cp.wait()

        if last:
            xhat, r = _rms_stats(h_ref[...])
            n = (xhat * g_ref[...]).astype(BF16)
            dy = (0.5 * dh_ref[...]).astype(BF16)
        else:
            n = n_ref[...]
            dy = dy_ref[...]
        for c0 in range(0, tm, ROW_ALIGN):
            rows = pl.ds(c0, ROW_ALIGN)
            av = a_ref[rows, :].astype(F32)
            bv = b_ref[rows, :].astype(F32)
            sg = _sigmoid(av)
            silu = av * sg
            ds = _dot_nt(dy[c0:c0 + ROW_ALIGN], wd_ref[...])
            da_scr[rows, :] = (ds * bv * (sg * (1.0 + av * (1.0 - sg)))).astype(BF16)
            db_scr[rows, :] = (ds * silu).astype(BF16)
            s_scr[rows, :] = (silu * bv).astype(BF16)
        da, db, s = da_scr[...], db_scr[...], s_scr[...]
        dwd_ref[...] += _dot_tn(s, dy)
        dwg_ref[...] += _dot_tn(da, n)
        dwu_ref[...] += _dot_tn(db, n)
        dn = _dot(da, wg_ref[...]) + _dot(db, wu_ref[...])
        if not first:
            dn = dn + acc_in[...]
        if last:
            dg_ref[...] += jnp.sum(dn * xhat, axis=0, keepdims=True)
            dh_in = dh_ref[...] + _rms_bwd(xhat, r, g_ref[...], dn)
            if unpad:
                pl.when(i > 0)(lambda: wait_tile(i - 1))
                res_ref[...] = dh_in

                @pl.when(i == 0)
                def _():
                    head_ref[...] = res_ref[pl.ds(0, unpad[0]), :]

                _for_tile_rows(i, ni, tm, *unpad, lambda lo, cnt, at: real_rows(lo, cnt, at).start())
                pl.when(i == ni - 1)(lambda: wait_tile(i))
            else:
                acc_out[...] = dh_in
        else:
            acc_out[...] = dn

        @pl.when(i == ni - 1)
        def _():
            stores = []
            for j, (acc_ref, stage_ref, out_hbm) in enumerate(((dwg_ref, wg_ref, dwg_hbm), (dwu_ref, wu_ref, dwu_hbm),
                                                              (dwd_ref, wd_ref, dwd_hbm))):
                stage_ref[...] = acc_ref[...].astype(BF16)
                stores.append(pltpu.make_async_copy(stage_ref, out_hbm, wsem.at[j]))
                stores[-1].start()
            for cp in stores:
                cp.wait()

    row_spec = pl.BlockSpec((tm, d), lambda i: (i, 0))
    vec_spec = pl.BlockSpec((1, d), lambda i: (0, 0))
    act_spec = pl.BlockSpec((None, tm, f4), lambda i: (k, i, 0))
    in_specs = [act_spec, act_spec, ANY, ANY, ANY]
    args = [a, b, wg, wu, wd]
    if last:
        in_specs = [row_spec, row_spec, vec_spec] + in_specs
        args = list(tail) + args
    else:
        in_specs = [row_spec, row_spec] + in_specs
        args = [dy, n] + args
    if not first:
        in_specs.insert(0, row_spec)
        args.insert(0, dn_prev)
    out_specs = [row_spec, ANY, ANY, ANY]
    out_shape = [jax.ShapeDtypeStruct((tp, d), F32)] + [jax.ShapeDtypeStruct((f4, d), BF16)] * 3
    scratch = [pltpu.VMEM((f4, d), BF16)] * 3 + [pltpu.VMEM((f4, d), F32)] * 3 + [pltpu.SemaphoreType.DMA((3,))]
    scratch += [pltpu.VMEM((tm, f4), BF16)] * 3
    if last:
        out_specs.append(vec_spec)
        out_shape.append(jax.ShapeDtypeStruct((1, d), F32))
    if unpad:
        out_specs[0] = ANY
        out_shape[0] = jax.ShapeDtypeStruct((unpad[1], d), F32)
        out_specs.append(pl.BlockSpec((unpad[0], d), lambda i: (0, 0)))
        out_shape.append(jax.ShapeDtypeStruct((unpad[0], d), F32))
        scratch += [pltpu.VMEM((tm, d), F32), pltpu.SemaphoreType.DMA(())]
    n_host = len(out_shape)
    split = _attach_carry(carry, in_specs, args, out_specs, out_shape, scratch)
    outs = pl.pallas_call(
        body, name=f"{name}_{k}", grid=(ni,), in_specs=in_specs, out_specs=out_specs, out_shape=out_shape,
        scratch_shapes=scratch, compiler_params=_params(("arbitrary",)), input_output_aliases=split.aliases,
    )(*args)
    return outs[:n_host], outs[n_host:]


def _ffn_bwd(dh_out, dy, h_in, n, g, a, b, wg, wu, wd, name, chain=None, unpad=None):
    ns = wg.shape[0]
    acc, dwg, dwu, dwd = None, [], [], []
    for k in range(ns):
        carry = chain.carry() if chain is not None else None
        outs, carried = _ffn_bwd_shard(k, ns, acc, dy, n, a, b, wg, wu, wd, (dh_out, h_in, g), name, carry, unpad)
        if chain is not None:
            chain.feed(carried)
        acc = outs[0]
        dwg.append(outs[1])
        dwu.append(outs[2])
        dwd.append(outs[3])
    return (acc, dwg, dwu, dwd) + tuple(outs[4:])


def _win_fwd(h, g, w_in, carry=None):
    tp, d = h.shape
    ns = w_in.shape[0]
    tm = _pick_tile(tp, (768, 512, 256))
    ni = tp // tm

    def body(*refs):
        (h_ref, g_ref, w_ref, u_ref, p_ref), phases = split(refs)
        _run_phases(phases, carry, pl.program_id(0), ni)
        xhat, _ = _rms_stats(h_ref[...])
        u = (xhat * g_ref[...]).astype(BF16)
        u_ref[...] = u
        for k in range(ns):
            p_ref[k] = _dot(u, w_ref[k]).astype(BF16)

    in_specs = [pl.BlockSpec((tm, d), lambda i: (i, 0)),
                pl.BlockSpec((1, d), lambda i: (0, 0)),
                pl.BlockSpec((ns, d, d), lambda i: (0, 0, 0))]
    out_specs = [pl.BlockSpec((tm, d), lambda i: (i, 0)),
                 pl.BlockSpec((ns, tm, d), lambda i: (0, i, 0))]
    out_shape = [jax.ShapeDtypeStruct((tp, d), BF16), jax.ShapeDtypeStruct((ns, tp, d), BF16)]
    args, scratch = [h, g, w_in], []
    split = _attach_carry(carry, in_specs, args, out_specs, out_shape, scratch)
    return pl.pallas_call(
        body, name="win_fwd", grid=(ni,), in_specs=in_specs, out_specs=out_specs, out_shape=out_shape,
        scratch_shapes=scratch, compiler_params=_params(("arbitrary",)), input_output_aliases=split.aliases,
    )(*args)


def _win_bwd_shard(k, ns, du_prev, dpb, dug, u, w_in, h1, g, dh2):
    tp, d = h1.shape
    dh = d // 2
    tm = _pick_tile(tp, (768, 512, 256))
    first, last = k == 0, k == ns - 1

    def body(*refs):
        refs = list(refs)
        acc_in = None if first else refs.pop(0)
        dug_ref = refs.pop(0) if first else None
        dp_ref, u_ref, w_ref = refs[:3]
        refs = refs[3:]
        if last:
            h_ref, g_ref, dh2_ref, acc_out, dw_ref, dg_ref, dy_ref, dw_acc = refs
        else:
            acc_out, dw_ref, dw_acc = refs
        i = pl.program_id(0)

        @pl.when(i == 0)
        def _():
            dw_acc[...] = jnp.zeros_like(dw_acc)
            if last:
                dg_ref[...] = jnp.zeros_like(dg_ref)

        dp = dp_ref[...]
        if first:
            dp = jnp.concatenate([dug_ref[...], dp[:, dh:]], axis=1)
        dw_acc[...] += _dot_tn(u_ref[...], dp)
        du = _dot_nt(dp, w_ref[...])
        if not first:
            du = du + acc_in[...]
        if last:
            xhat, r = _rms_stats(h_ref[...])
            dg_ref[...] += jnp.sum(du * xhat, axis=0, keepdims=True)
            dh1 = dh2_ref[...] + _rms_bwd(xhat, r, g_ref[...], du)
            acc_out[...] = dh1
            dy_ref[...] = (0.5 * dh1).astype(BF16)
        else:
            acc_out[...] = du

        @pl.when(i == tp // tm - 1)
        def _():
            dw_ref[...] = dw_acc[...].astype(BF16)

    row_spec = pl.BlockSpec((tm, d), lambda i: (i, 0))
    vec_spec = pl.BlockSpec((1, d), lambda i: (0, 0))
    in_specs = [pl.BlockSpec((None, tm, d), lambda i: (k, i, 0)), row_spec,
                pl.BlockSpec((None, d, d), lambda i: (k, 0, 0))]
    args = [dpb, u, w_in]
    if first:
        in_specs.insert(0, pl.BlockSpec((tm, dh), lambda i: (i, 0)))
        args.insert(0, dug)
    else:
        in_specs.insert(0, row_spec)
        args.insert(0, du_prev)
    out_specs = [row_spec, pl.BlockSpec((d, d), lambda i: (0, 0))]
    out_shape = [jax.ShapeDtypeStruct((tp, d), F32), jax.ShapeDtypeStruct((d, d), BF16)]
    if last:
        in_specs += [row_spec, vec_spec, row_spec]
        args += [h1, g, dh2]
        out_specs += [vec_spec, row_spec]
        out_shape += [jax.ShapeDtypeStruct((1, d), F32), jax.ShapeDtypeStruct((tp, d), BF16)]
    return pl.pallas_call(
        body, name=f"win_bwd_{k}", grid=(tp // tm,), in_specs=in_specs, out_specs=out_specs,
        out_shape=out_shape, scratch_shapes=[pltpu.VMEM((d, d), F32)],
        compiler_params=_params(("arbitrary",)),
    )(*args)


def _win_bwd(dpb, dug, u, w_in, h1, g, dh2):
    ns = w_in.shape[0]
    acc, dws = None, []
    for k in range(ns):
        outs = _win_bwd_shard(k, ns, acc, dpb, dug, u, w_in, h1, g, dh2)
        acc = outs[0]
        dws.append(outs[1])
    return acc, dws, outs[2], outs[3]


def _cmul(ar, ai, br, bi):
    return ar * br - ai * bi, ar * bi + ai * br


def _scan_rows(j, sub):
    return pl.ds(j * SCAN_SEQS, SCAN_SEQS)


def _permute_rows(src_ref, dst_ref, sub):
    for j in range(sub):
        dst_ref[pl.ds(j * SCAN_SEQS, SCAN_SEQS), :] = src_ref[pl.ds(j, SCAN_SEQS, stride=sub), :]


def _unpermute_rows(src_ref, dst_ref, sub):
    for j in range(sub):
        dst_ref[pl.ds(j, SCAN_SEQS, stride=sub), :] = src_ref[pl.ds(j * SCAN_SEQS, SCAN_SEQS), :]


def _local_scan(x_ref, lr, li, w, sub, reverse):
    hr = jnp.zeros((SCAN_SEQS, w), F32)
    hi = jnp.zeros((SCAN_SEQS, w), F32)
    order = range(sub - 1, -1, -1) if reverse else range(sub)
    for j in order:
        xr = x_ref[_scan_rows(j, sub), pl.ds(0, w)]
        xi = x_ref[_scan_rows(j, sub), pl.ds(w, w)]
        if reverse:
            hr, hi = lr * hr + li * hi + xr, lr * hi - li * hr + xi
        else:
            hr, hi = lr * hr - li * hi + xr, lr * hi + li * hr + xi
        x_ref[_scan_rows(j, sub), pl.ds(0, w)] = hr
        x_ref[_scan_rows(j, sub), pl.ds(w, w)] = hi
    return hr, hi


def _entering_states(er, ei, fr, fi, pow_ref, w, sub, reverse):
    lane = lax.broadcasted_iota(jnp.int32, (SCAN_SEQS, w), 0)
    if reverse:
        edge, shift1 = SCAN_SEQS - 1, SCAN_SEQS - 1
    else:
        edge, shift1 = 0, 1
    zr = jnp.where(lane == edge, pltpu.roll(fr, shift1, 0), pltpu.roll(er, shift1, 0))
    zi = jnp.where(lane == edge, pltpu.roll(fi, shift1, 0), pltpu.roll(ei, shift1, 0))
    for m in range(SCAN_SEQS.bit_length() - 1):
        step, row = 1 << m, sub - 1 + m
        ar = pow_ref[pl.ds(row, 1), pl.ds(0, w)]
        ai = pow_ref[pl.ds(row, 1), pl.ds(w, w)]
        if reverse:
            ai = -ai
            keep = lane < SCAN_SEQS - step
            sr = jnp.where(keep, pltpu.roll(zr, SCAN_SEQS - step, 0), 0.0)
            si = jnp.where(keep, pltpu.roll(zi, SCAN_SEQS - step, 0), 0.0)
        else:
            keep = lane >= step
            sr = jnp.where(keep, pltpu.roll(zr, step, 0), 0.0)
            si = jnp.where(keep, pltpu.roll(zi, step, 0), 0.0)
        pr, pi = _cmul(ar, ai, sr, si)
        zr, zi = zr + pr, zi + pi
    ar = pow_ref[pl.ds(sub - 1, 1), pl.ds(0, w)]
    ai = pow_ref[pl.ds(sub - 1, 1), pl.ds(w, w)]
    if reverse:
        ai = -ai
    pr, pi = _cmul(ar, ai, zr, zi)
    return zr, zi, er + pr, ei + pi


def _scan_fwd(p, mb, mc, powt, dskip):
    _, tp, d = p.shape
    nb, cb, w2 = mb.shape
    w = w2 // 2
    q = SCAN_TILE
    sub = q // SCAN_SEQS
    nt = tp // q
    ds = d // 2

    def body(ug_ref, mb_ref, mc_ref, pow_ref, d_ref, y_ref, bnd_ref, x_scr, carry, nat, perm):
        t = pl.program_id(1)

        @pl.when(t == 0)
        def _():
            carry[...] = jnp.zeros_like(carry)

        ugf = ug_ref[...].astype(F32)
        nat[...] = ugf
        _permute_rows(nat, perm, sub)
        x_scr[...] = _dot(perm[...].astype(BF16), mb_ref[...])
        lr = jnp.broadcast_to(pow_ref[pl.ds(0, 1), pl.ds(0, w)], (SCAN_SEQS, w))
        li = jnp.broadcast_to(pow_ref[pl.ds(0, 1), pl.ds(w, w)], (SCAN_SEQS, w))
        er, ei = _local_scan(x_scr, lr, li, w, sub, False)
        zr, zi, fr, fi = _entering_states(er, ei, carry[:, pl.ds(0, w)], carry[:, pl.ds(w, w)],
                                          pow_ref, w, sub, False)
        carry[:, pl.ds(0, w)] = fr
        carry[:, pl.ds(w, w)] = fi
        bnd_ref[:, pl.ds(0, w)] = fr
        bnd_ref[:, pl.ds(w, w)] = fi
        for j in range(sub):
            pr = pow_ref[pl.ds(j, 1), pl.ds(0, w)]
            pi = pow_ref[pl.ds(j, 1), pl.ds(w, w)]
            cr, ci = _cmul(pr, pi, zr, zi)
            x_scr[_scan_rows(j, sub), pl.ds(0, w)] += cr
            x_scr[_scan_rows(j, sub), pl.ds(w, w)] += ci
        hb = x_scr[...].astype(BF16)
        perm[...] = _dot_nt(hb, mc_ref[...])
        _unpermute_rows(perm, nat, sub)
        y_ref[...] = nat[...] + d_ref[...] * ugf

    in_specs = [pl.BlockSpec((None, q, cb), lambda b, t: (0, t, b)),
                pl.BlockSpec((None, cb, w2), lambda b, t: (b, 0, 0)),
                pl.BlockSpec((None, cb, w2), lambda b, t: (b, 0, 0)),
                pl.BlockSpec((None, powt.shape[1], w2), lambda b, t: (b, 0, 0)),
                pl.BlockSpec((1, cb), lambda b, t: (0, b))]
    out_specs = [pl.BlockSpec((q, cb), lambda b, t: (t, b)),
                 pl.BlockSpec((None, None, SCAN_SEQS, w2), lambda b, t: (b, t, 0, 0))]
    out_shape = [jax.ShapeDtypeStruct((tp, ds), F32), jax.ShapeDtypeStruct((nb, nt, SCAN_SEQS, w2), F32)]
    scratch = [pltpu.VMEM((q, w2), F32), pltpu.VMEM((SCAN_SEQS, w2), F32),
               pltpu.VMEM((q, cb), F32), pltpu.VMEM((q, cb), F32)]
    return pl.pallas_call(
        body, name="s5_scan_fwd", grid=(nb, nt), in_specs=in_specs, out_specs=out_specs,
        out_shape=out_shape, scratch_shapes=scratch, compiler_params=_params(("arbitrary", "arbitrary")),
    )(p, mb, mc, powt, dskip)


def _scan_bwd(p, dy, mb, mc, powt, dskip, bnd):
    _, tp, d = p.shape
    nb, cb, w2 = mb.shape
    w = w2 // 2
    q = SCAN_TILE
    sub = q // SCAN_SEQS
    nt = tp // q
    ds = d // 2

    def body(ug_ref, dy_ref, mb_ref, mc_ref, pow_ref, d_ref, bnd_ref,
             dug_ref, dmb_ref, dmc_ref, dlam_ref, dd_ref, x_scr, y_scr, gcarry, nat, perm):
        t = pl.program_id(1)
        tt = nt - 1 - t

        @pl.when(t == 0)
        def _():
            gcarry[...] = jnp.zeros_like(gcarry)
            dmb_ref[...] = jnp.zeros_like(dmb_ref)
            dmc_ref[...] = jnp.zeros_like(dmc_ref)
            dlam_ref[...] = jnp.zeros_like(dlam_ref)
            dd_ref[...] = jnp.zeros_like(dd_ref)

        ugf = ug_ref[...].astype(F32)
        dyf = dy_ref[...].astype(F32)
        dd_ref[...] += jnp.sum((dyf * ugf).reshape(q // SUBLANES, SUBLANES, cb), axis=0)
        nat[...] = ugf
        _permute_rows(nat, perm, sub)
        ug = perm[...].astype(BF16)
        nat[...] = dyf
        _permute_rows(nat, perm, sub)
        dyb = perm[...].astype(BF16)
        lr = jnp.broadcast_to(pow_ref[pl.ds(0, 1), pl.ds(0, w)], (SCAN_SEQS, w))
        li = jnp.broadcast_to(pow_ref[pl.ds(0, 1), pl.ds(w, w)], (SCAN_SEQS, w))

        x_scr[...] = _dot(ug, mb_ref[...])
        er, ei = _local_scan(x_scr, lr, li, w, sub, False)
        first = tt == 0
        pfr = jnp.where(first, 0.0, bnd_ref[:, pl.ds(0, w)])
        pfi = jnp.where(first, 0.0, bnd_ref[:, pl.ds(w, w)])
        hzr, hzi, _, _ = _entering_states(er, ei, pfr, pfi, pow_ref, w, sub, False)
        for j in range(sub):
            pr = pow_ref[pl.ds(j, 1), pl.ds(0, w)]
            pi = pow_ref[pl.ds(j, 1), pl.ds(w, w)]
            cr, ci = _cmul(pr, pi, hzr, hzi)
            x_scr[_scan_rows(j, sub), pl.ds(0, w)] += cr
            x_scr[_scan_rows(j, sub), pl.ds(w, w)] += ci

        y_scr[...] = _dot(dyb, mc_ref[...])
        er, ei = _local_scan(y_scr, lr, li, w, sub, True)
        gzr, gzi, fr, fi = _entering_states(er, ei, gcarry[:, pl.ds(0, w)], gcarry[:, pl.ds(w, w)],
                                            pow_ref, w, sub, True)
        gcarry[:, pl.ds(0, w)] = fr
        gcarry[:, pl.ds(w, w)] = fi
        accr = jnp.zeros((SCAN_SEQS, w), F32)
        acci = jnp.zeros((SCAN_SEQS, w), F32)
        for j in range(sub):
            pr = pow_ref[pl.ds(sub - 1 - j, 1), pl.ds(0, w)]
            pi = pow_ref[pl.ds(sub - 1 - j, 1), pl.ds(w, w)]
            cr, ci = _cmul(pr, -pi, gzr, gzi)
            gr = y_scr[_scan_rows(j, sub), pl.ds(0, w)] + cr
            gi = y_scr[_scan_rows(j, sub), pl.ds(w, w)] + ci
            y_scr[_scan_rows(j, sub), pl.ds(0, w)] = gr
            y_scr[_scan_rows(j, sub), pl.ds(w, w)] = gi
            if j == 0:
                hpr, hpi = hzr, hzi
            else:
                hpr = x_scr[_scan_rows(j - 1, sub), pl.ds(0, w)]
                hpi = x_scr[_scan_rows(j - 1, sub), pl.ds(w, w)]
            accr += hpr * gr + hpi * gi
            acci += hpr * gi - hpi * gr
        dlam_ref[:, pl.ds(0, w)] += accr
        dlam_ref[:, pl.ds(w, w)] += acci

        hb = x_scr[...].astype(BF16)
        gb = y_scr[...].astype(BF16)
        dmc_ref[...] += _dot_tn(dyb, hb)
        dmb_ref[...] += _dot_tn(ug, gb)
        perm[...] = _dot_nt(gb, mb_ref[...])
        _unpermute_rows(perm, nat, sub)
        dug_ref[...] = (nat[...] + d_ref[...] * dyf).astype(BF16)

    blk = lambda b, t: (b, 0, 0)
    return pl.pallas_call(
        body, name="s5_scan_bwd", grid=(nb, nt),
        in_specs=[pl.BlockSpec((None, q, cb), lambda b, t: (0, nt - 1 - t, b)),
                  pl.BlockSpec((q, cb), lambda b, t: (nt - 1 - t, b)),
                  pl.BlockSpec((None, cb, w2), blk),
                  pl.BlockSpec((None, cb, w2), blk),
                  pl.BlockSpec((None, powt.shape[1], w2), blk),
                  pl.BlockSpec((1, cb), lambda b, t: (0, b)),
                  pl.BlockSpec((None, None, SCAN_SEQS, w2),
                               lambda b, t: (b, jnp.maximum(nt - 2 - t, 0), 0, 0))],
        out_specs=[pl.BlockSpec((q, cb), lambda b, t: (nt - 1 - t, b)),
                   pl.BlockSpec((None, cb, w2), blk),
                   pl.BlockSpec((None, cb, w2), blk),
                   pl.BlockSpec((None, SCAN_SEQS, w2), blk),
                   pl.BlockSpec((SUBLANES, cb), lambda b, t: (0, b))],
        out_shape=[jax.ShapeDtypeStruct((tp, ds), BF16),
                   jax.ShapeDtypeStruct((nb, cb, w2), F32),
                   jax.ShapeDtypeStruct((nb, cb, w2), F32),
                   jax.ShapeDtypeStruct((nb, SCAN_SEQS, w2), F32),
                   jax.ShapeDtypeStruct((SUBLANES, ds), F32)],
        scratch_shapes=[pltpu.VMEM((q, w2), F32), pltpu.VMEM((q, w2), F32),
                        pltpu.VMEM((SCAN_SEQS, w2), F32), pltpu.VMEM((q, cb), F32), pltpu.VMEM((q, cb), F32)],
        compiler_params=_params(("arbitrary", "arbitrary")),
    )(p, dy, mb, mc, powt, dskip, bnd)


HALO = 16


def _mix_tile(ys5, p0, p1, p2, p3, prev_cin, cw, bgate, wglu, wco, d):
    dh = d // 2
    tm = ys5.shape[0]
    v = p0[:, dh:].astype(F32)
    gbr = p1[:, :dh].astype(F32)
    gcr = p1[:, dh:].astype(F32)
    gact = _gelu(ys5).astype(BF16)
    z = _dot(gact, wglu)
    z1, z2 = z[:, :d], z[:, d:]
    sg = _sigmoid(z2)
    y_ssm = z1 * sg
    cin = gcr * v
    ext = jnp.concatenate([cin, prev_cin], axis=0)
    r1 = pltpu.roll(ext, 1, 0)[:tm]
    r2 = pltpu.roll(ext, 2, 0)[:tm]
    cv = cw[2] * cin + cw[1] * r1 + cw[0] * r2
    cg = (gbr * cv).astype(BF16)
    y_conv = _dot(cg, wco)
    g_s = _sigmoid(p2.astype(F32) + bgate[:, :d])
    g_c = _sigmoid(p3.astype(F32) + bgate[:, d:])
    mixed = g_s * y_ssm + g_c * y_conv
    return dict(v=v, gb=gbr, gc=gcr, gact=gact, z1=z1, sg=sg, y_ssm=y_ssm, cin=cin, r1=r1, r2=r2,
                cv=cv, cg=cg, y_conv=y_conv, g_s=g_s, g_c=g_c, mixed=mixed)


def _mix_fwd(h1, ys5, p, cw, bgate, wglu, wco, wo, carry=None):
    tp, d = h1.shape
    dh = d // 2
    tm = ROW_ALIGN
    ni = tp // tm

    def body(*refs):
        refs, phases = split(refs)
        (h_ref, y_ref, p0_ref, p1_ref, p2_ref, p3_ref, cw_ref, bg_ref, wglu_ref, wco_ref, wo_ref,
         o_ref, prev) = refs
        _run_phases(phases, carry, pl.program_id(0), ni)

        @pl.when(pl.program_id(0) == 0)
        def _():
            prev[...] = jnp.zeros_like(prev)

        cw = [cw_ref[pl.ds(t, 1), :] for t in range(3)]
        f = _mix_tile(y_ref[...], p0_ref[...], p1_ref[...], p2_ref[...], p3_ref[...], prev[...],
                      cw, bg_ref[...], wglu_ref[...], wco_ref[...], d)
        prev[...] = f["cin"][tm - HALO:, :]
        o_ref[...] = h_ref[...] + _dot(f["mixed"].astype(BF16), wo_ref[...])

    row = pl.BlockSpec((tm, d), lambda i: (i, 0))
    full = lambda a: pl.BlockSpec(a.shape, lambda i: (0,) * a.ndim)
    pk = lambda k: pl.BlockSpec((None, tm, d), lambda i, k=k: (k, i, 0))
    in_specs = [row, pl.BlockSpec((tm, dh), lambda i: (i, 0)), pk(0), pk(1), pk(2), pk(3),
                full(cw), full(bgate), full(wglu), full(wco), full(wo)]
    out_specs, out_shape = [row], [jax.ShapeDtypeStruct((tp, d), F32)]
    args, scratch = [h1, ys5, p, p, p, p, cw, bgate, wglu, wco, wo], [pltpu.VMEM((HALO, dh), F32)]
    split = _attach_carry(carry, in_specs, args, out_specs, out_shape, scratch)
    return pl.pallas_call(
        body, name="mix_fwd", grid=(ni,), in_specs=in_specs, out_specs=out_specs, out_shape=out_shape,
        scratch_shapes=scratch, compiler_params=_params(("arbitrary",)), input_output_aliases=split.aliases,
    )(*args)


def _mix_bwd(dh2, ys5, p, cw, bgate, wglu, wco, wo):
    tp, d = dh2.shape
    dh = d // 2
    tm = ROW_ALIGN
    ni = tp // tm
    hb = tm // HALO

    def body(dh_ref, y_ref, p0_ref, p1_ref, p2_ref, p3_ref, h0_ref, h1_ref,
             cw_ref, bg_ref, wglu_ref, wco_ref, wo_ref,
             dys_ref, dpb_ref, dwo_ref, dwglu_ref, dwco_ref, dcw_ref, dbg_ref, nxt):
        i = pl.program_id(0)
        tt = ni - 1 - i

        @pl.when(i == 0)
        def _():
            nxt[...] = jnp.zeros_like(nxt)
            dwo_ref[...] = jnp.zeros_like(dwo_ref)
            dwglu_ref[...] = jnp.zeros_like(dwglu_ref)
            dwco_ref[...] = jnp.zeros_like(dwco_ref)
            dcw_ref[...] = jnp.zeros_like(dcw_ref)
            dbg_ref[...] = jnp.zeros_like(dbg_ref)

        cw = [cw_ref[pl.ds(t, 1), :] for t in range(3)]
        prev_cin = h1_ref[:, dh:].astype(F32) * h0_ref[:, dh:].astype(F32)
        prev_cin = jnp.where(tt == 0, 0.0, prev_cin)
        ys5 = y_ref[...]
        f = _mix_tile(ys5, p0_ref[...], p1_ref[...], p2_ref[...], p3_ref[...], prev_cin,
                      cw, bg_ref[...], wglu_ref[...], wco_ref[...], d)
        dhb = dh_ref[...].astype(BF16)
        dmixed = _dot_nt(dhb, wo_ref[...])
        dwo_ref[...] += _dot_tn(f["mixed"].astype(BF16), dhb)

        g_s, g_c, sg = f["g_s"], f["g_c"], f["sg"]
        dy_ssm = dmixed * g_s
        dy_conv = dmixed * g_c
        dp2 = dmixed * f["y_ssm"] * g_s * (1.0 - g_s)
        dp3 = dmixed * f["y_conv"] * g_c * (1.0 - g_c)
        dbg_ref[:, pl.ds(0, d)] += jnp.sum(dp2, axis=0, keepdims=True)
        dbg_ref[:, pl.ds(d, d)] += jnp.sum(dp3, axis=0, keepdims=True)

        dz = jnp.concatenate([dy_ssm * sg, dy_ssm * f["z1"] * sg * (1.0 - sg)], axis=1).astype(BF16)
        dwglu_ref[...] += _dot_tn(f["gact"], dz)
        dys_ref[...] = (_dot_nt(dz, wglu_ref[...]) * _gelu_grad(ys5)).astype(BF16)

        dycb = dy_conv.astype(BF16)
        dwco_ref[...] += _dot_tn(f["cg"], dycb)
        dcg = _dot_nt(dycb, wco_ref[...])
        dgb = dcg * f["cv"]
        dcv = dcg * f["gb"]
        ext = jnp.concatenate([dcv, nxt[...]], axis=0)
        n1 = pltpu.roll(ext, tm + HALO - 1, 0)[:tm]
        n2 = pltpu.roll(ext, tm + HALO - 2, 0)[:tm]
        nxt[...] = dcv[:HALO, :]
        dcin = cw[2] * dcv + cw[1] * n1 + cw[0] * n2
        dcw_ref[pl.ds(0, 1), :] += jnp.sum(dcv * f["r2"], axis=0, keepdims=True)
        dcw_ref[pl.ds(1, 1), :] += jnp.sum(dcv * f["r1"], axis=0, keepdims=True)
        dcw_ref[pl.ds(2, 1), :] += jnp.sum(dcv * f["cin"], axis=0, keepdims=True)
        dgc = dcin * f["v"]
        dv = dcin * f["gc"]
        dpb_ref[0] = jnp.concatenate([jnp.zeros_like(dv), dv], axis=1).astype(BF16)
        dpb_ref[1] = jnp.concatenate([dgb, dgc], axis=1).astype(BF16)
        dpb_ref[2] = dp2.astype(BF16)
        dpb_ref[3] = dp3.astype(BF16)

    rev = lambda i: ni - 1 - i
    row = pl.BlockSpec((tm, d), lambda i: (rev(i), 0))
    half = pl.BlockSpec((tm, dh), lambda i: (rev(i), 0))
    full = lambda a: pl.BlockSpec(a.shape, lambda i: (0,) * a.ndim)
    pk = lambda k: pl.BlockSpec((None, tm, d), lambda i, k=k: (k, rev(i), 0))
    halo = lambda k: pl.BlockSpec((None, HALO, d), lambda i, k=k: (k, jnp.maximum(rev(i) * hb - 1, 0), 0))
    acc = lambda shape: pl.BlockSpec(shape, lambda i: (0,) * len(shape))
    return pl.pallas_call(
        body, name="mix_bwd", grid=(ni,),
        in_specs=[row, half, pk(0), pk(1), pk(2), pk(3), halo(0), halo(1),
                  full(cw), full(bgate), full(wglu), full(wco), full(wo)],
        out_specs=[half, pl.BlockSpec((4, tm, d), lambda i: (0, rev(i), 0)),
                   acc((d, d)), acc((dh, 2 * d)), acc((dh, d)), acc((SUBLANES, dh)), acc((1, 2 * d))],
        out_shape=[jax.ShapeDtypeStruct((tp, dh), BF16), jax.ShapeDtypeStruct((4, tp, d), BF16),
                   jax.ShapeDtypeStruct((d, d), F32), jax.ShapeDtypeStruct((dh, 2 * d), F32),
                   jax.ShapeDtypeStruct((dh, d), F32), jax.ShapeDtypeStruct((SUBLANES, dh), F32),
                   jax.ShapeDtypeStruct((1, 2 * d), F32)],
        scratch_shapes=[pltpu.VMEM((HALO, dh), F32)],
        compiler_params=_params(("arbitrary",)),
    )(dh2, ys5, p, p, p, p, p, p, cw, bgate, wglu, wco, wo)


ANY = pl.BlockSpec(memory_space=pl.ANY)


def _position():
    return lax.axis_index("x"), lax.axis_index("y"), lax.axis_index("c")


def _remote(src, dst, ssem, rsem, dev):
    return pltpu.make_async_remote_copy(src_ref=src, dst_ref=dst, send_sem=ssem, recv_sem=rsem,
                                        device_id=dev, device_id_type=MESH)


def _cast_pieces(ws, pos):
    n = len(ws)

    def body(pos_ref, *refs):
        for w_ref, o_ref in zip(refs[:n], refs[n:]):
            r4 = o_ref.shape[1]
            o_ref[0] = w_ref[pl.ds(0, r4), :].astype(BF16)
            o_ref[1] = w_ref[pl.ds(r4, r4), :].astype(BF16)

    halves = [(w.shape[0] // 2, w.shape[1]) for w in ws]
    return pl.pallas_call(
        body, name="cast_pieces",
        grid_spec=pltpu.PrefetchScalarGridSpec(
            num_scalar_prefetch=1, grid=(1,),
            in_specs=[pl.BlockSpec(hs, lambda i, pos: (pos[2], 0)) for hs in halves],
            out_specs=[pl.BlockSpec((None, None, None, 2, r2 // 2, cols),
                                    lambda i, pos: (pos[0], pos[1], pos[2], 0, 0, 0)) for r2, cols in halves]),
        out_shape=[jax.ShapeDtypeStruct((2, 2, 2, 2, r2 // 2, cols), BF16) for r2, cols in halves],
        compiler_params=_params(("arbitrary",)),
    )(pos, *ws)


class _Carry:
    def __init__(self, name, arrays, out_shapes, nsem, nlsem, make, fracs, n_inplace=0):
        self.name, self.arrays, self.out_shapes = name, list(arrays), list(out_shapes)
        self.nsem, self.nlsem, self.make, self.fracs = nsem, max(nlsem, 1), make, fracs
        self.n_inplace = n_inplace


def _carry_scratch(carry):
    return [pltpu.SemaphoreType.DMA((carry.nsem,)), pltpu.SemaphoreType.DMA((carry.nsem,)),
            pltpu.SemaphoreType.DMA((carry.nlsem,))]


def _run_carry(carry):
    na, no = len(carry.arrays), len(carry.out_shapes)

    def body(*refs):
        for phase in carry.make(refs[:na], refs[na:na + no], *refs[na + no:]):
            phase()

    return pl.pallas_call(
        body, name=carry.name, in_specs=[ANY] * na, out_specs=[ANY] * no, out_shape=carry.out_shapes,
        scratch_shapes=_carry_scratch(carry), input_output_aliases={i: i for i in range(carry.n_inplace)},
    )(*carry.arrays)


def _attach_carry(carry, in_specs, args, out_specs, out_shape, scratch):
    nhi, nho, nhs = len(in_specs), len(out_specs), len(scratch)
    if carry is None:
        none = lambda refs: (list(refs), [])
        none.aliases = {}
        return none
    na, no = len(carry.arrays), len(carry.out_shapes)
    in_specs += [ANY] * na
    args += carry.arrays
    out_specs += [ANY] * no
    out_shape += carry.out_shapes
    scratch += _carry_scratch(carry)

    def split(refs):
        refs = list(refs)
        o = nhi + na
        host = refs[:nhi] + refs[o:o + nho] + refs[o + nho + no:o + nho + no + nhs]
        sems = refs[o + nho + no + nhs:]
        return host, carry.make(refs[nhi:o], refs[o + nho:o + nho + no], *sems)

    split.aliases = {nhi + i: nho + i for i in range(carry.n_inplace)}
    split.carry_outs = lambda refs: list(refs)[nhi + na + nho:nhi + na + nho + no]
    return split


def _run_phases(phases, carry, step, total):
    for phase, frac in zip(phases, carry.fracs if carry is not None else ()):
        pl.when(step == int(round(frac * (total - 1))))(phase)


def _allgather_carry(name, walls, smalls):
    n, ns = len(walls), len(smalls)
    per = 14
    n_big = per * n

    def make(ins, outs, ssem, rsem, lsem):
        sin = ins[n:]
        wall, sall = outs[:n], outs[n:]
        x, y, c = _position()
        xnb, ynb, sib = (1 - x, y, c), (x, 1 - y, c), (x, y, 1 - c)
        chips = [(1 - x, y), (x, 1 - y), (1 - x, 1 - y)]
        slot = lambda i, xx, yy, cc, h: wall[i].at[xx, yy, cc, h]
        own = lambda i, h: slot(i, x, y, c, h)
        cp = lambda src, dst, s, dev: _remote(src, dst, ssem.at[s], rsem.at[s], dev)
        to_sib = lambda i, xx, yy, h: cp(slot(i, xx, yy, c, h), slot(i, xx, yy, c, h),
                                         per * i + 6 + 4 * xx + 2 * yy + h, sib)

        def local():
            return [pltpu.make_async_copy(sin[i], sall[i].at[2 * x + y], lsem.at[i]) for i in range(ns)]

        def small(px, py, j, i, landing):
            s = n_big + j * ns + i
            return cp(sin[i], sall[i].at[landing], s, (px, py, c))

        def first_hop():
            for lc in local():
                lc.start()
            for j, (px, py) in enumerate(chips):
                for i in range(ns):
                    small(px, py, j, i, 2 * x + y).start()
            for i in range(n):
                cp(own(i, 0), slot(i, x, y, c, 0), per * i, xnb).start()
                cp(own(i, 1), slot(i, x, y, c, 1), per * i + 1, ynb).start()
                for h in range(2):
                    cp(own(i, h), slot(i, x, y, c, h), per * i + 6 + 4 * x + 2 * y + h, sib).start()

        def second_hop():
            for lc in local():
                lc.wait()
            for i in range(n):
                cp(slot(i, 1 - x, y, c, 0), slot(i, 1 - x, y, c, 0), per * i, xnb).wait_recv()
                cp(slot(i, x, 1 - y, c, 1), slot(i, x, 1 - y, c, 1), per * i + 1, ynb).wait_recv()
                for j in range(2):
                    cp(slot(i, j, y, c, 0), slot(i, j, y, c, 0), per * i + 2 + j, ynb).start()
                    cp(slot(i, x, j, c, 1), slot(i, x, j, c, 1), per * i + 4 + j, xnb).start()
                to_sib(i, 1 - x, y, 0).start()
                to_sib(i, x, 1 - y, 1).start()

        def last_to_sibling():
            for i in range(n):
                for j in range(2):
                    cp(slot(i, j, 1 - y, c, 0), slot(i, j, 1 - y, c, 0), per * i + 2 + j, ynb).wait_recv()
                    cp(slot(i, 1 - x, j, c, 1), slot(i, 1 - x, j, c, 1), per * i + 4 + j, xnb).wait_recv()
                    to_sib(i, j, 1 - y, 0).start()
                    to_sib(i, 1 - x, j, 1).start()

        def finish():
            for i in range(n):
                for xx in range(2):
                    for yy in range(2):
                        for h in range(2):
                            s = per * i + 6 + 4 * xx + 2 * yy + h
                            cp(slot(i, xx, yy, 1 - c, h), slot(i, xx, yy, 1 - c, h), s, sib).wait_recv()
                            to_sib(i, xx, yy, h).wait_send()
                cp(own(i, 0), slot(i, x, y, c, 0), per * i, xnb).wait_send()
                cp(own(i, 1), slot(i, x, y, c, 1), per * i + 1, ynb).wait_send()
                for j in range(2):
                    cp(slot(i, j, y, c, 0), slot(i, j, y, c, 0), per * i + 2 + j, ynb).wait_send()
                    cp(slot(i, x, j, c, 1), slot(i, x, j, c, 1), per * i + 4 + j, xnb).wait_send()
            for j, (px, py) in enumerate(chips):
                for i in range(ns):
                    small(px, py, j, i, 2 * px + py).wait_recv()
                    small(px, py, j, i, 2 * x + y).wait_send()

        return [first_hop, second_hop, last_to_sibling, finish]

    out_shapes = [jax.ShapeDtypeStruct(a.shape, a.dtype) for a in walls]
    out_shapes += [jax.ShapeDtypeStruct((4,) + a.shape, a.dtype) for a in smalls]
    return _Carry(name, list(walls) + list(smalls), out_shapes, n_big + 3 * ns, ns, make, (0.0, 0.23, 0.73, 1.0),
                  n_inplace=n)


def _assemble_rows(x2d, n_meta, tp, carry, meta_at):
    seq, d = x2d.shape
    tm = ROW_ALIGN
    ni = tp // tm

    def body(*refs):
        (x_hbm, o_ref, buf, sem), phases = split(refs)
        shards = split.carry_outs(refs)[meta_at]
        s = pl.program_id(0)
        i = ni - 1 - s
        _run_phases(phases, carry, s, ni)

        def fetch(lo, n, at):
            if at > 0:
                buf[pl.ds(0, at), :] = jnp.zeros((at, d), F32)
            if at + n < tm:
                buf[pl.ds(at + n, tm - at - n), :] = jnp.zeros((tm - at - n, d), F32)
            cp = pltpu.make_async_copy(x_hbm.at[pl.ds(lo, n), :], buf.at[pl.ds(at, n), :], sem)
            cp.start()
            cp.wait()

        _for_tile_rows(i, ni, tm, n_meta, seq, fetch)

        @pl.when(i == 0)
        def _():
            mc = d // 4
            for k in range(4):
                cp = pltpu.make_async_copy(shards.at[k], buf.at[pl.ds(0, n_meta), pl.ds(k * mc, mc)], sem)
                cp.start()
                cp.wait()

        o_ref[...] = buf[...]

    in_specs, args = [ANY], [x2d]
    out_specs = [pl.BlockSpec((tm, d), lambda s: (ni - 1 - s, 0))]
    out_shape = [jax.ShapeDtypeStruct((tp, d), F32)]
    scratch = [pltpu.VMEM((tm, d), F32), pltpu.SemaphoreType.DMA(())]
    split = _attach_carry(carry, in_specs, args, out_specs, out_shape, scratch)
    return pl.pallas_call(
        body, name="assemble_rows", grid=(ni,), in_specs=in_specs, out_specs=out_specs, out_shape=out_shape,
        scratch_shapes=scratch, compiler_params=_params(("arbitrary",)), input_output_aliases=split.aliases,
    )(*args)


def _exchange_carry(name, arrays, out_shapes, plan):
    count = plan([None] * len(arrays), [None] * len(out_shapes), None)

    def make(ins, outs, ssem, rsem, lsem):
        def copies():
            return [_remote(src, dst, ssem.at[j], rsem.at[j], peer)
                    for j, (src, dst, peer) in enumerate(plan(ins, outs, _position()))]

        def start():
            for c in copies():
                c.start()

        def wait():
            for c in copies():
                c.wait()

        return [start, wait]

    return _Carry(name, arrays, out_shapes, count, 0, make, (0.0, 1.0))


class _Grad:
    def __init__(self, arrs, kind, shard_shape):
        self.arrs, self.kind = list(arrs), kind
        self.rows, self.cols = shard_shape
        self.r2 = self.rows // 2

    def view(self, refs, k, h):
        r2 = self.r2
        if self.kind == "list":
            return refs[k].at[pl.ds(h * r2, r2), :]
        if self.kind == "stacked":
            return refs[0].at[k, pl.ds(h * r2, r2), :]
        if self.kind == "col":
            return refs[0].at[pl.ds(h * r2, r2), pl.ds(k * self.cols, self.cols)]
        return refs[0].at[pl.ds((2 * k + h) * r2, r2), :]

    def half_specs(self):
        r2, cols = self.r2, self.cols
        if self.kind == "list":
            return [pl.BlockSpec((r2, cols), lambda k, pos: (pos[2], 0))] * len(self.arrs)
        if self.kind == "stacked":
            return [pl.BlockSpec((None, r2, cols), lambda k, pos: (k, pos[2], 0))]
        if self.kind == "col":
            return [pl.BlockSpec((r2, cols), lambda k, pos: (pos[2], k))]
        return [pl.BlockSpec((r2, cols), lambda k, pos: (2 * k + pos[2], 0))]

    def step_bytes(self):
        return self.r2 * self.cols * (len(self.arrs) * self.arrs[0].dtype.itemsize + self.arrs[0].dtype.itemsize + 6)


def _add_halves(grads, recvs, pos):
    n = len(grads)
    counts = [len(g.arrs) for g in grads]
    n_mine = sum(counts)

    def body(pos_ref, *refs):
        o = 0
        for i in range(n):
            m_refs = refs[o:o + counts[i]]
            o += counts[i]
            r_ref, of_ref, ob_ref = refs[n_mine + i], refs[n_mine + n + i], refs[n_mine + 2 * n + i]
            mine = m_refs[0][...]
            for kk in range(1, counts[i]):
                mine = jnp.where(pl.program_id(0) == kk, m_refs[kk][...], mine)
            s = mine.astype(F32) + r_ref[...].astype(F32)
            of_ref[...] = s
            ob_ref[...] = s.astype(BF16)

    blks = [pl.BlockSpec((None, g.r2, g.cols), lambda k, pos: (k, 0, 0)) for g in grads]
    outs = pl.pallas_call(
        body, name="rs_add_c",
        grid_spec=pltpu.PrefetchScalarGridSpec(
            num_scalar_prefetch=1, grid=(4,),
            in_specs=[spec for g in grads for spec in g.half_specs()] + blks, out_specs=blks + blks),
        out_shape=[jax.ShapeDtypeStruct((4, g.r2, g.cols), F32) for g in grads]
        + [jax.ShapeDtypeStruct((4, g.r2, g.cols), BF16) for g in grads],
        compiler_params=_params(("arbitrary",)),
    )(pos, *[a for g in grads for a in g.arrs], *recvs)
    return list(zip(outs[:n], outs[n:]))


def _row_tile(rows, cols):
    fits = [t for t in range(16, rows + 1, 16) if rows % t == 0 and t * cols * 4 <= 2 * 1024 * 1024]
    return max(fits) if fits else rows


def _adamw_math(w, g, m, v):
    m = ADAM_B1 * m + (1.0 - ADAM_B1) * g
    v = ADAM_B2 * v + (1.0 - ADAM_B2) * (g * g)
    m_hat = m / (1.0 - ADAM_B1 ** ADAM_STEP)
    v_hat = v / (1.0 - ADAM_B2 ** ADAM_STEP)
    delta = -ADAM_LR * (m_hat / (jnp.sqrt(v_hat) + ADAM_EPS) + ADAM_WD * w)
    return delta, m, v


def _adamw_big(w, m, v, own, sib, pos):
    rows, cols = w.shape
    r2 = rows // 2

    tr = _row_tile(r2, cols)
    nt = r2 // tr

    def body(pos_ref, w_ref, m_ref, v_ref, own_ref, sib_ref, g_ref, d_ref, nm_ref, nv_ref):
        h = pl.program_id(0)
        g = jnp.where(h == pos_ref[2], own_ref[...], sib_ref[...])
        g_ref[...] = g
        d_ref[...], nm_ref[...], nv_ref[...] = _adamw_math(w_ref[...], g, m_ref[...], v_ref[...])

    half = pl.BlockSpec((tr, cols), lambda h, t, pos: (h * nt + t, 0))
    piece = pl.BlockSpec((tr, cols), lambda h, t, pos: (t, 0))
    out = jax.ShapeDtypeStruct((rows, cols), F32)
    return pl.pallas_call(
        body, name="adamw",
        grid_spec=pltpu.PrefetchScalarGridSpec(
            num_scalar_prefetch=1, grid=(2, nt),
            in_specs=[half, half, half, piece, piece],
            out_specs=[half, half, half, half]),
        out_shape=[out, out, out, out],
        compiler_params=_params(("arbitrary", "arbitrary")),
    )(pos, w, m, v, own, sib)


def _add_hop1(s1fs, recvs, pos):
    n = len(s1fs)
    s1vs = [s.reshape((4, 2) + r.shape[2:]) for s, r in zip(s1fs, recvs)]

    def body(pos_ref, *refs):
        for m_ref, r_ref, of_ref, ob_ref in zip(refs[:n], refs[n:2 * n], refs[2 * n:3 * n], refs[3 * n:]):
            s = m_ref[...] + r_ref[...].astype(F32)
            of_ref[...] = s
            ob_ref[...] = s.astype(BF16)

    def mine(h, j, pos):
        return (jnp.where(h == 0, 2 * j + pos[1], 2 * pos[0] + j), h, 0, 0)

    tile = lambda r: (None, None) + r.shape[2:]
    blks = [pl.BlockSpec(tile(r), lambda h, j, pos: (h, j, 0, 0)) for r in recvs]
    outs = pl.pallas_call(
        body, name="rs_add_1",
        grid_spec=pltpu.PrefetchScalarGridSpec(
            num_scalar_prefetch=1, grid=(2, 2),
            in_specs=[pl.BlockSpec(tile(r), mine) for r in recvs] + blks, out_specs=blks + blks),
        out_shape=[jax.ShapeDtypeStruct(r.shape, F32) for r in recvs]
        + [jax.ShapeDtypeStruct(r.shape, BF16) for r in recvs],
        compiler_params=_params(("arbitrary", "arbitrary")),
    )(pos, *s1vs, *recvs)
    return list(zip(outs[:n], outs[n:]))


def _own_sum(s2fs, recvs, pos):
    n = len(s2fs)

    def body(pos_ref, *refs):
        for s_ref, r_ref, o_ref in zip(refs[:n], refs[n:2 * n], refs[2 * n:]):
            o_ref[...] = s_ref[...] + r_ref[...].astype(F32)

    blks = [pl.BlockSpec((None,) + r.shape[1:], lambda h, pos: (h, 0, 0)) for r in recvs]
    return pl.pallas_call(
        body, name="own_sum",
        grid_spec=pltpu.PrefetchScalarGridSpec(
            num_scalar_prefetch=1, grid=(2,),
            in_specs=[pl.BlockSpec((None, None) + r.shape[1:],
                                   lambda h, pos: (h, jnp.where(h == 0, pos[0], pos[1]), 0, 0)) for r in recvs]
            + blks, out_specs=blks),
        out_shape=[jax.ShapeDtypeStruct(r.shape, F32) for r in recvs],
        compiler_params=_params(("arbitrary",)),
    )(pos, *s2fs, *recvs)


def _add_small(a, b):
    def body(a_ref, b_ref, o_ref):
        o_ref[...] = a_ref[...] + b_ref[...]

    vm = pl.BlockSpec(memory_space=pltpu.VMEM)
    return pl.pallas_call(body, name="add_small", in_specs=[vm, vm], out_specs=vm,
                          out_shape=jax.ShapeDtypeStruct(a.shape, F32))(a, b)


def _adamw_small(ws, gs, ms, vs):
    n = len(ws)

    def body(*refs):
        for i in range(n):
            w_ref, g_ref, m_ref, v_ref, d_ref, nm_ref, nv_ref = (refs[j * n + i] for j in range(7))
            d_ref[...], nm_ref[...], nv_ref[...] = _adamw_math(w_ref[...], g_ref[...], m_ref[...], v_ref[...])

    vm = pl.BlockSpec(memory_space=pltpu.VMEM)
    outs = pl.pallas_call(body, name="adamw_small", in_specs=[vm] * (4 * n), out_specs=[vm] * (3 * n),
                          out_shape=[jax.ShapeDtypeStruct(w.shape, F32) for w in ws] * 3)(*ws, *gs, *ms, *vs)
    return outs[:n], outs[n:2 * n], outs[2 * n:]


class _ReduceScatter:
    def __init__(self, tag, grads, pos, extra=None):
        self.tag, self.grads, self.pos, self.stage, self.extra = tag, grads, pos, 0, extra

    def carry(self):
        grads, n = self.grads, len(self.grads)
        r4 = [g.r2 // 2 for g in grads]

        first = [sum(len(g.arrs) for g in grads[:i]) for i in range(n)]

        def plan_c(ins, outs, p):
            if p is None:
                return 4 * n
            x, y, c = p
            mine = lambda i: ins[first[i]:first[i] + len(grads[i].arrs)]
            return [(grads[i].view(mine(i), k, 1 - c), outs[i].at[k], (x, y, 1 - c))
                    for i in range(n) for k in range(4)]

        def plan_1(ins, outs, p):
            if p is None:
                return 4 * n
            x, y, c = p
            copies = []
            for i in range(n):
                for j in range(2):
                    copies.append((ins[i].at[2 * j + (1 - y), pl.ds(0, r4[i]), :], outs[i].at[0, j],
                                   (x, 1 - y, c)))
                    copies.append((ins[i].at[2 * (1 - x) + j, pl.ds(r4[i], r4[i]), :], outs[i].at[1, j],
                                   (1 - x, y, c)))
            return copies

        def plan_2(ins, outs, p):
            if p is None:
                return 2 * n
            x, y, c = p
            copies = []
            for i in range(n):
                copies.append((ins[i].at[0, 1 - x], outs[i].at[0], (1 - x, y, c)))
                copies.append((ins[i].at[1, 1 - y], outs[i].at[1], (x, 1 - y, c)))
            return copies

        def plan_s(ins, outs, p):
            if p is None:
                return n
            x, y, c = p
            return [(ins[i], outs[i], (x, y, 1 - c)) for i in range(n)]

        shape = lambda lead, dt: [jax.ShapeDtypeStruct(lead(g) + (g.cols,), dt) for g in grads]
        stage = self.stage
        if stage == 0:
            name, arrays, plan = "exchange_c", [a for g in grads for a in g.arrs], plan_c
            shapes = [jax.ShapeDtypeStruct((4, g.r2, g.cols), g.arrs[0].dtype) for g in grads]
        elif stage == 1:
            name, arrays, plan = "exchange_1", [s[1] for s in self.s1], plan_1
            shapes = shape(lambda g: (2, 2, g.r2 // 2), BF16)
        elif stage == 2:
            name, arrays, plan = "exchange_2", [s[1] for s in self.s2], plan_2
            shapes = shape(lambda g: (2, g.r2 // 2), BF16)
        else:
            name, arrays, plan, shapes = "exchange_sibling", self.own, plan_s, shape(lambda g: (g.r2,), F32)
        if self.extra is not None and stage < 3:
            def with_extra(ins, outs, p, plan=plan):
                if p is None:
                    return plan(ins[:-1], outs[:-1], None) + 1
                x, y, c = p
                peer = [(x, y, 1 - c), (x, 1 - y, c), (1 - x, y, c)][stage]
                return plan(ins[:-1], outs[:-1], p) + [(ins[-1], outs[-1], peer)]

            arrays = arrays + [self.extra]
            shapes = shapes + [jax.ShapeDtypeStruct(self.extra.shape, F32)]
            plan = with_extra
        return _exchange_carry(f"rs_{self.tag}_{name}", arrays, shapes, plan)

    def feed(self, recv):
        grads, pos = self.grads, self.pos
        recv = list(recv)
        if self.extra is not None and self.stage < 3:
            self.extra = _add_small(self.extra, recv.pop())
        if self.stage == 0:
            self.s1, start = [], 0
            while start < len(grads):
                end, size = start, 0
                while end < len(grads) and (end == start or size + grads[end].step_bytes() <= ADD_GROUP_BYTES):
                    size += grads[end].step_bytes()
                    end += 1
                self.s1 += _add_halves(grads[start:end], recv[start:end], pos)
                start = end
        elif self.stage == 1:
            self.s2 = _add_hop1([s[0] for s in self.s1], list(recv), pos)
        elif self.stage == 2:
            own = _own_sum([s[0] for s in self.s2], list(recv), pos)
            self.own = [o.reshape(g.r2, g.cols) for g, o in zip(grads, own)]
        else:
            self.sib = list(recv)
        self.stage += 1

    def run(self):
        while self.stage < 4:
            self.feed(_run_carry(self.carry()))

    def adamw(self, weights):
        return [_adamw_big(w, m, v, o, sb, self.pos) for (w, m, v), o, sb in zip(weights, self.own, self.sib)]


def _block_diag(t, nb):
    g, c, p = t.shape
    gb = g // nb
    t = t.reshape(nb, gb, c, p)
    eye = jnp.eye(gb, dtype=t.dtype)
    return jnp.einsum("bgcp,gh->bgchp", t, eye).reshape(nb, gb * c, gb * p)


def _s5_discretise(a_re, a_im, log_dt, b_re, b_im, c_re, c_im):
    g, p = a_re.shape
    nb = g // GROUPS_PER_BLOCK
    dt = jnp.exp(log_dt)[:, None]
    mag = jnp.exp(a_re * dt)
    lam_re = mag * jnp.cos(a_im * dt)
    lam_im = mag * jnp.sin(a_im * dt)
    den = a_re * a_re + a_im * a_im
    q_re = ((lam_re - 1.0) * a_re + lam_im * a_im) / den
    q_im = (lam_im * a_re - (lam_re - 1.0) * a_im) / den
    bb_re = q_re[..., None] * b_re - q_im[..., None] * b_im
    bb_im = q_re[..., None] * b_im + q_im[..., None] * b_re
    tr = lambda t: jnp.swapaxes(t, 1, 2)
    mb = jnp.concatenate([_block_diag(tr(bb_re), nb), _block_diag(tr(bb_im), nb)], axis=-1)
    mc = jnp.concatenate([_block_diag(c_re, nb), -_block_diag(c_im, nb)], axis=-1)
    lam = jnp.concatenate([lam_re.reshape(nb, -1), lam_im.reshape(nb, -1)], axis=-1)
    return mb, mc, lam


def _s5_powers(a_re, a_im, log_dt, sub):
    g, p = a_re.shape
    nb = g // GROUPS_PER_BLOCK
    dt = jnp.exp(log_dt)[:, None]
    ns = list(range(1, sub + 1)) + [sub << m for m in range(1, SCAN_SEQS.bit_length() - 1)]
    ns += [0] * (-len(ns) % SUBLANES)
    e = jnp.asarray(ns, F32)[:, None, None]
    mag = jnp.exp(a_re[None] * dt[None] * e)
    ang = a_im[None] * dt[None] * e
    re = (mag * jnp.cos(ang)).reshape(len(ns), nb, -1)
    im = (mag * jnp.sin(ang)).reshape(len(ns), nb, -1)
    return jnp.transpose(jnp.concatenate([re, im], axis=-1), (1, 0, 2))


def _pack(parts):
    flat = jnp.concatenate([a.reshape(-1).astype(F32) for a in parts])
    n = flat.shape[0]
    pad = -n % (SUBLANES * LANES)
    return jnp.pad(flat, (0, pad)).reshape(-1, LANES)


def _unpack(buf, like):
    flat = buf.reshape(-1)
    out, o = [], 0
    for a in like:
        out.append(flat[o:o + a.size].reshape(a.shape))
        o += a.size
    return out


def kernel(x, meta_tokens, g_ffn1, ffn1_w_gate, ffn1_w_up, ffn1_w_down, g_mix, w_in, b_gate, ssm_a_re, ssm_a_im, ssm_log_dt, ssm_b_re, ssm_b_im, ssm_c_re, ssm_c_im, ssm_d, ssm_w_glu, conv_w, conv_w_out, w_o, g_ffn2, ffn2_w_gate, ffn2_w_up, ffn2_w_down, g_final, loss_target, m_meta_tokens, m_g_ffn1, m_ffn1_w_gate, m_ffn1_w_up, m_ffn1_w_down, m_g_mix, m_w_in, m_b_gate, m_ssm_a_re, m_ssm_a_im, m_ssm_log_dt, m_ssm_b_re, m_ssm_b_im, m_ssm_c_re, m_ssm_c_im, m_ssm_d, m_ssm_w_glu, m_conv_w, m_conv_w_out, m_w_o, m_g_ffn2, m_ffn2_w_gate, m_ffn2_w_up, m_ffn2_w_down, m_g_final, v_meta_tokens, v_g_ffn1, v_ffn1_w_gate, v_ffn1_w_up, v_ffn1_w_down, v_g_mix, v_w_in, v_b_gate, v_ssm_a_re, v_ssm_a_im, v_ssm_log_dt, v_ssm_b_re, v_ssm_b_im, v_ssm_c_re, v_ssm_c_im, v_ssm_d, v_ssm_w_glu, v_conv_w, v_conv_w_out, v_w_o, v_g_ffn2, v_ffn2_w_gate, v_ffn2_w_up, v_ffn2_w_down, v_g_final):
    seq, d = x.shape[1], x.shape[2]
    n_meta = meta_tokens.shape[0]
    dh = d // 2
    tp = -(-(n_meta + seq) // ROW_ALIGN) * ROW_ALIGN
    mx, my, mc_ = _position()
    pos = jnp.stack([mx, my, mc_]).astype(jnp.int32)
    shard = 2 * mx + my

    big_names = ["ffn1_w_gate", "ffn1_w_up", "ffn1_w_down", "w_in", "ssm_w_glu", "conv_w_out", "w_o",
                 "ffn2_w_gate", "ffn2_w_up", "ffn2_w_down"]
    transposed = {0, 1, 7, 8}
    drop = lambda arrs: [jnp.swapaxes(a.reshape(a.shape[1:]), 0, 1) if i in transposed else a.reshape(a.shape[1:])
                         for i, a in enumerate(arrs)]
    big_w = drop([ffn1_w_gate, ffn1_w_up, ffn1_w_down, w_in, ssm_w_glu, conv_w_out, w_o,
                  ffn2_w_gate, ffn2_w_up, ffn2_w_down])
    big_m = drop([m_ffn1_w_gate, m_ffn1_w_up, m_ffn1_w_down, m_w_in, m_ssm_w_glu, m_conv_w_out,
                  m_w_o, m_ffn2_w_gate, m_ffn2_w_up, m_ffn2_w_down])
    big_v = drop([v_ffn1_w_gate, v_ffn1_w_up, v_ffn1_w_down, v_w_in, v_ssm_w_glu, v_conv_w_out,
                  v_w_o, v_ffn2_w_gate, v_ffn2_w_up, v_ffn2_w_down])
    pieces = _cast_pieces(big_w[:3], pos) + _cast_pieces(big_w[3:], pos)
    conv_local = conv_w.reshape(conv_w.shape[1], conv_w.shape[3])
    n_first = 3
    h0, *first = _assemble_rows(
        x.reshape(seq, d), n_meta, tp,
        _allgather_carry("allgather_first", pieces[:n_first], [meta_tokens, conv_local]), meta_at=n_first)
    smalls = first[n_first:]
    stack4 = lambda wl: wl.reshape((4, -1, wl.shape[-1]))
    w1g, w1u, w1d = [stack4(wl) for wl in first[:n_first]]
    natural_cols = lambda s: jnp.transpose(s, (1, 0, 2)).reshape(s.shape[1], 4 * s.shape[2])
    cw_full = natural_cols(smalls[1])
    cw_pad = jnp.pad(cw_full, ((0, SUBLANES - cw_full.shape[0]), (0, 0)))

    s5_args = (ssm_a_re[0], ssm_a_im[0], ssm_log_dt[0], ssm_b_re[0], ssm_b_im[0], ssm_c_re[0], ssm_c_im[0])
    (mb, mc, _), disc_vjp = jax.vjp(_s5_discretise, *s5_args)
    powt = _s5_powers(ssm_a_re[0], ssm_a_im[0], ssm_log_dt[0], SCAN_TILE // SCAN_SEQS)
    mb16, mc16 = mb.astype(BF16), mc.astype(BF16)

    h1, a1, b1, n1, *mid = _ffn_fwd(h0, g_ffn1, w1g, w1u, w1d, "ffn1_fwd",
                                    carry=_allgather_carry("allgather_mixer", pieces[3:7], []))
    win_all, wglu_s, wco_s, wo_s = [stack4(wl) for wl in mid]
    wglu_all = natural_cols(wglu_s)
    wco_all = natural_cols(wco_s)
    wo_all = wo_s.reshape(d, d)
    u, p, w2g, w2u = _win_fwd(h1, g_mix, win_all, carry=_allgather_carry("allgather_ffn2_in", pieces[7:9], []))
    ys5, bnd = _scan_fwd(p, mb16, mc16, powt, ssm_d)
    h2, w2d = _mix_fwd(h1, ys5, p, cw_pad, b_gate, wglu_all, wco_all, wo_all,
                       carry=_allgather_carry("allgather_ffn2_out", pieces[9:], []))
    w2g, w2u, w2d = stack4(w2g), stack4(w2u), stack4(w2d)
    dh3, a2, b2, n2, dg_final, loss_part, dy3 = _ffn_fwd(
        h2, g_ffn2, w2g, w2u, w2d, "ffn2_fwd_loss",
        final=(g_final.reshape(1, d), loss_target.reshape(seq, d), n_meta, seq))

    dh2, dw2g, dw2u, dw2d, dg_ffn2 = _ffn_bwd(dh3, dy3, h2, n2, g_ffn2, a2, b2, w2g, w2u, w2d, "ffn2_bwd")
    dys5, dpb, dwo, dwglu, dwco, dcw, dbg = _mix_bwd(dh2, ys5, p, cw_pad, b_gate, wglu_all, wco_all, wo_all)
    dug, dmb, dmc, dlam, dd = _scan_bwd(p, dys5, mb16, mc16, powt, ssm_d, bnd)
    dh1, dwin, dg_mix, dy1 = _win_bwd(dpb, dug, u, win_all, h1, g_mix, dh2)
    shapes = [w.shape for w in big_w]
    kinds = ["list", "list", "list", "list", "col", "col", "row", "list", "list", "list"]
    rest_grads = [dwin, [dwglu], [dwco], [dwo], dw2g, dw2u, dw2d]
    rs_rest = _ReduceScatter("rest", [_Grad(a, k, s) for a, k, s in
                                      zip(rest_grads, kinds[n_first:], shapes[n_first:])], pos)
    grad_x, dw1g, dw1u, dw1d, dg_ffn1, grad_meta = _ffn_bwd(
        dh1, dy1, h0, n1, g_ffn1, a1, b1, w1g, w1u, w1d, "ffn1_bwd", chain=rs_rest, unpad=(n_meta, seq))
    s5_grads = disc_vjp((dmb, dmc, jnp.sum(dlam, axis=1)))
    local_small = [dg_ffn1, dg_mix, dbg, *s5_grads, jnp.sum(dd, axis=0), dg_ffn2, dg_final,
                   grad_meta, dcw[:conv_w.shape[1]]]
    rs_first = _ReduceScatter("first", [_Grad(a, k, s) for a, k, s in
                                        zip([dw1g, dw1u, dw1d], kinds[:n_first], shapes[:n_first])], pos,
                              extra=_pack(local_small))
    rs_first.run()
    wmv = list(zip(big_w, big_m, big_v))
    big_out = rs_first.adamw(wmv[:n_first]) + rs_rest.adamw(wmv[n_first:])
    def lead(i, o):
        o = jnp.swapaxes(o, 0, 1) if i in transposed else o
        return o.reshape((1,) + o.shape)

    big_out = {nme: tuple(lead(i, o) for o in outs) for i, (nme, outs) in enumerate(zip(big_names, big_out))}

    grad_x = grad_x.reshape(1, seq, d)

    small_names = ["g_ffn1", "g_mix", "b_gate", "ssm_a_re", "ssm_a_im", "ssm_log_dt", "ssm_b_re", "ssm_b_im",
                   "ssm_c_re", "ssm_c_im", "ssm_d", "g_ffn2", "g_final", "meta_tokens", "conv_w"]
    small_w = [g_ffn1, g_mix, b_gate, ssm_a_re, ssm_a_im, ssm_log_dt, ssm_b_re, ssm_b_im, ssm_c_re, ssm_c_im,
               ssm_d, g_ffn2, g_final, meta_tokens, conv_w]
    small_m = [m_g_ffn1, m_g_mix, m_b_gate, m_ssm_a_re, m_ssm_a_im, m_ssm_log_dt, m_ssm_b_re, m_ssm_b_im,
               m_ssm_c_re, m_ssm_c_im, m_ssm_d, m_g_ffn2, m_g_final, m_meta_tokens, m_conv_w]
    small_v = [v_g_ffn1, v_g_mix, v_b_gate, v_ssm_a_re, v_ssm_a_im, v_ssm_log_dt, v_ssm_b_re, v_ssm_b_im,
               v_ssm_c_re, v_ssm_c_im, v_ssm_d, v_g_ffn2, v_g_final, v_meta_tokens, v_conv_w]
    reduced = _unpack(rs_first.extra, local_small)
    reduced[-2] = lax.dynamic_slice_in_dim(reduced[-2], shard * meta_tokens.shape[1], meta_tokens.shape[1], 1)
    reduced[-1] = lax.dynamic_slice_in_dim(reduced[-1], shard * conv_w.shape[3], conv_w.shape[3], 1)
    small_g = [r.reshape(w.shape) for r, w in zip(reduced, small_w)]
    two_d = lambda arrs: [a.reshape(1, -1) if a.ndim == 1 else a for a in arrs]
    ds_, nm_, nv_ = _adamw_small(two_d(small_w), two_d(small_g), two_d(small_m), two_d(small_v))
    like = lambda outs: [o.reshape(w.shape) for o, w in zip(outs, small_w)]
    small_out = {nme: o for nme, o in zip(small_names, zip(small_g, like(ds_), like(nm_), like(nv_)))}

    loss = lax.psum(loss_part[0, 0], ("x", "y", "c"))
    order = ["meta_tokens", "g_ffn1", "ffn1_w_gate", "ffn1_w_up", "ffn1_w_down", "g_mix", "w_in", "b_gate",
             "ssm_a_re", "ssm_a_im", "ssm_log_dt", "ssm_b_re", "ssm_b_im", "ssm_c_re", "ssm_c_im", "ssm_d",
             "ssm_w_glu", "conv_w", "conv_w_out", "w_o", "g_ffn2", "ffn2_w_gate", "ffn2_w_up", "ffn2_w_down",
             "g_final"]
    res = {**big_out, **small_out}
    return (loss, grad_x, *[res[nme][0] for nme in order], *[res[nme][1] for nme in order],
            *[res[nme][2] for nme in order], *[res[nme][3] for nme in order])
```
